```python
import math
import jax, jax.numpy as jnp
from jax import lax
import numpy as np

D_MODEL = 2048
BATCH = 4
SEQ = 2048
DEPTH = 1

MLA_HEADS = 8
QK_NOPE_DIM = 128
QK_ROPE_DIM = 64
V_HEAD_DIM = 128
Q_LORA_RANK = 768
KV_LORA_RANK = 512
ROPE_THETA = 10000.0
ATTN_BLOCK = 128
MLA_WIDTH = MLA_HEADS * V_HEAD_DIM

SGU_GROUPS = 8
SGU_GROUP_DIM = 128
SGU_CHUNK = 128
SGU_WIDTH = SGU_GROUPS * SGU_GROUP_DIM

MIX_WIDTH = MLA_WIDTH + SGU_WIDTH
IN_PROJ_DIM = Q_LORA_RANK + KV_LORA_RANK + QK_ROPE_DIM + 2 * SGU_WIDTH

N_GROUPS = 4
EXPERTS_PER_GROUP = 8
N_EXPERTS = N_GROUPS * EXPERTS_PER_GROUP
TOP_K_EXPERTS = 2
EXPERT_FF = 512

DEEPNORM_ALPHA = (2 * DEPTH) ** 0.25
DEEPNORM_BETA = (8 * DEPTH) ** -0.25
EPS = 1e-6
N_MOD = 6

kernel_name = "hymba_mla_sgu_hiermoe_deepnorm_adaln"


def _layernorm(x, g=None, b=None):
    xf = x.astype(jnp.float32)
    mu = jnp.mean(xf, axis=-1, keepdims=True)
    var = jnp.mean(jnp.square(xf - mu), axis=-1, keepdims=True)
    y = (xf - mu) * lax.rsqrt(var + EPS)
    if g is not None:
        y = y * g.astype(jnp.float32) + b.astype(jnp.float32)
    return y.astype(x.dtype)


def _rmsnorm(x, g):
    xf = x.astype(jnp.float32)
    y = xf * lax.rsqrt(jnp.mean(jnp.square(xf), axis=-1, keepdims=True) + EPS)
    return (y * g.astype(jnp.float32)).astype(x.dtype)


def _rope(x, cos, sin):
    x1, x2 = jnp.split(x, 2, axis=-1)
    return jnp.concatenate([x1 * cos - x2 * sin, x2 * cos + x1 * sin], axis=-1)


def _causal_block_attention(q, k, v):
    B, S, H, Dk = q.shape
    nb = S // ATTN_BLOCK
    scale = Dk ** -0.5
    q_blocks = q.reshape(B, nb, ATTN_BLOCK, H, Dk).transpose(1, 0, 2, 3, 4)
    key_pos = jnp.arange(S)

    def one_block(args):
        qb, bi = args
        s = jnp.einsum('bqhd,bkhd->bhqk', qb, k).astype(jnp.float32) * scale
        q_pos = bi * ATTN_BLOCK + jnp.arange(ATTN_BLOCK)
        mask = key_pos[None, :] <= q_pos[:, None]
        s = jnp.where(mask, s, -1e30)
        p = jax.nn.softmax(s, axis=-1).astype(v.dtype)
        return jnp.einsum('bhqk,bkhd->bqhd', p, v)

    out = lax.map(one_block, (q_blocks, jnp.arange(nb)))
    return out.transpose(1, 0, 2, 3, 4).reshape(B, S, H, v.shape[-1])


def _mixer(h, positions, w_in, q_norm_g, w_uq, kv_norm_g, w_ukv,
           sgu_norm_g, sgu_norm_b, w_spatial, b_spatial, w_o):
    B, S, _ = h.shape
    proj = h @ w_in
    c_q, c_kv, k_pe, z = jnp.split(
        proj, [Q_LORA_RANK, Q_LORA_RANK + KV_LORA_RANK, Q_LORA_RANK + KV_LORA_RANK + QK_ROPE_DIM], axis=-1)

    q = (_rmsnorm(c_q, q_norm_g) @ w_uq).reshape(B, S, MLA_HEADS, QK_NOPE_DIM + QK_ROPE_DIM)
    q_nope, q_pe = jnp.split(q, [QK_NOPE_DIM], axis=-1)
    kv = (_rmsnorm(c_kv, kv_norm_g) @ w_ukv).reshape(B, S, MLA_HEADS, QK_NOPE_DIM + V_HEAD_DIM)
    k_nope, v = jnp.split(kv, [QK_NOPE_DIM], axis=-1)
    inv_freq = 1.0 / (ROPE_THETA ** (jnp.arange(0, QK_ROPE_DIM, 2, dtype=jnp.float32) / QK_ROPE_DIM))
    ang = positions.astype(jnp.float32)[..., None] * inv_freq
    cos = jnp.cos(ang).astype(h.dtype)
    sin = jnp.sin(ang).astype(h.dtype)
    q_pe = _rope(q_pe, cos[:, :, None, :], sin[:, :, None, :])
    k_pe = _rope(k_pe, cos, sin)
    q_full = jnp.concatenate([q_nope, q_pe], axis=-1)
    k_full = jnp.concatenate(
        [k_nope, jnp.broadcast_to(k_pe[:, :, None, :], (B, S, MLA_HEADS, QK_ROPE_DIM))], axis=-1)
    attn = _causal_block_attention(q_full, k_full, v).reshape(B, S, MLA_WIDTH)

    u, vs = jnp.split(jax.nn.gelu(z), 2, axis=-1)
    vs = _layernorm(vs, sgu_norm_g, sgu_norm_b)
    nc = S // SGU_CHUNK
    vs = vs.reshape(B, nc, SGU_CHUNK, SGU_GROUPS, SGU_GROUP_DIM)
    causal = jnp.tril(jnp.ones((SGU_CHUNK, SGU_CHUNK), dtype=bool))
    ws = jnp.where(causal[None], w_spatial, jnp.zeros_like(w_spatial))
    mixed = jnp.einsum('gts,bcsgd->bctgd', ws, vs) + b_spatial.T[None, None, :, :, None]
    sgu = u * mixed.reshape(B, S, SGU_WIDTH)

    return jnp.concatenate([attn, sgu], axis=-1) @ w_o


def _hier_moe(h, w_router_group, b_router_group, w_router_expert, b_router_expert,
              w_gate, w_up, w_down):
    B, S, D = h.shape
    xt = h.reshape(B * S, D)
    lg = (xt @ w_router_group + b_router_group).astype(jnp.float32)
    pg = jax.nn.softmax(lg, axis=-1)
    pg_top, g_idx = lax.top_k(pg, 1)
    le = (xt @ w_router_expert + b_router_expert).astype(jnp.float32).reshape(-1, N_GROUPS, EXPERTS_PER_GROUP)
    le_sel = jnp.take_along_axis(le, g_idx[:, :, None], axis=1)[:, 0]
    pe = jax.nn.softmax(le_sel, axis=-1)
    pe_top, e_idx = lax.top_k(pe, TOP_K_EXPERTS)
    pe_top = pe_top / jnp.sum(pe_top, axis=-1, keepdims=True)
    weights = pg_top * pe_top
    expert_ids = g_idx * EXPERTS_PER_GROUP + e_idx
    combine = jnp.sum(jax.nn.one_hot(expert_ids, N_EXPERTS, dtype=jnp.float32) * weights[..., None], axis=1)
    combine = combine.astype(h.dtype)
    out = jnp.zeros_like(xt)
    for e in range(N_EXPERTS):
        hid = jax.nn.silu(xt @ w_gate[e]) * (xt @ w_up[e])
        out = out + combine[:, e:e + 1] * (hid @ w_down[e])
    return out.reshape(B, S, D)


def setup_inputs(seed: int = 0) -> dict:
    key = jax.random.key(seed)
    ks = jax.random.split(key, 26)
    L, D = DEPTH, D_MODEL
    f32 = jnp.float32

    def nrm(k, shape, scale):
        return jax.random.normal(k, shape, f32) * scale

    x = jax.random.normal(ks[0], (BATCH, SEQ, D), f32)
    c = jax.random.normal(ks[1], (BATCH, D), f32)
    offsets = jax.random.randint(ks[2], (BATCH,), 0, 4096, dtype=jnp.int32)
    positions = (offsets[:, None] + jnp.arange(SEQ, dtype=jnp.int32)[None, :]).astype(jnp.int32)
    return {
        "x": x,
        "c": c,
        "positions": positions,
        "w_ada": nrm(ks[3], (L, D, N_MOD * D), 0.5 * D ** -0.5),
        "b_ada": nrm(ks[4], (L, N_MOD * D), 0.01),
        "w_in": nrm(ks[5], (L, D, IN_PROJ_DIM), D ** -0.5),
        "q_norm_g": 1.0 + nrm(ks[6], (L, Q_LORA_RANK), 0.02),
        "w_uq": nrm(ks[7], (L, Q_LORA_RANK, MLA_HEADS * (QK_NOPE_DIM + QK_ROPE_DIM)), Q_LORA_RANK ** -0.5),
        "kv_norm_g": 1.0 + nrm(ks[8], (L, KV_LORA_RANK), 0.02),
        "w_ukv": nrm(ks[9], (L, KV_LORA_RANK, MLA_HEADS * (QK_NOPE_DIM + V_HEAD_DIM)), KV_LORA_RANK ** -0.5),
        "sgu_norm_g": 1.0 + nrm(ks[10], (L, SGU_WIDTH), 0.02),
        "sgu_norm_b": nrm(ks[11], (L, SGU_WIDTH), 0.01),
        "w_spatial": nrm(ks[12], (L, SGU_GROUPS, SGU_CHUNK, SGU_CHUNK), SGU_CHUNK ** -0.5),
        "b_spatial": 1.0 + nrm(ks[13], (L, SGU_GROUPS, SGU_CHUNK), 0.02),
        "w_o": nrm(ks[14], (L, MIX_WIDTH, D), DEEPNORM_BETA * MIX_WIDTH ** -0.5),
        "ln1_g": 1.0 + nrm(ks[15], (L, D), 0.02),
        "ln1_b": nrm(ks[16], (L, D), 0.01),
        "w_router_group": nrm(ks[17], (L, D, N_GROUPS), D ** -0.5),
        "b_router_group": nrm(ks[18], (L, N_GROUPS), 0.01),
        "w_router_expert": nrm(ks[19], (L, D, N_EXPERTS), D ** -0.5),
        "b_router_expert": nrm(ks[20], (L, N_EXPERTS), 0.01),
        "w_gate": nrm(ks[21], (L, N_EXPERTS, D, EXPERT_FF), D ** -0.5),
        "w_up": nrm(ks[22], (L, N_EXPERTS, D, EXPERT_FF), D ** -0.5),
        "w_down": nrm(ks[23], (L, N_EXPERTS, EXPERT_FF, D), DEEPNORM_BETA * EXPERT_FF ** -0.5),
        "ln2_g": 1.0 + nrm(ks[24], (L, D), 0.02),
        "ln2_b": nrm(ks[25], (L, D), 0.01),
    }


def reference(x, c, positions, w_ada, b_ada, w_in, q_norm_g, w_uq, kv_norm_g, w_ukv,
              sgu_norm_g, sgu_norm_b, w_spatial, b_spatial, w_o, ln1_g, ln1_b,
              w_router_group, b_router_group, w_router_expert, b_router_expert,
              w_gate, w_up, w_down, ln2_g, ln2_b):
    for l in range(DEPTH):
        mod = (c @ w_ada[l] + b_ada[l])[:, None, :]
        sh1, sc1, g1, sh2, sc2, g2 = jnp.split(mod, N_MOD, axis=-1)
        h = _layernorm(x) * (1.0 + sc1) + sh1
        y = _mixer(h, positions, w_in[l], q_norm_g[l], w_uq[l], kv_norm_g[l], w_ukv[l],
                   sgu_norm_g[l], sgu_norm_b[l], w_spatial[l], b_spatial[l], w_o[l])
        x = _layernorm(DEEPNORM_ALPHA * x + g1 * y, ln1_g[l], ln1_b[l])
        h = _layernorm(x) * (1.0 + sc2) + sh2
        y = _hier_moe(h, w_router_group[l], b_router_group[l], w_router_expert[l], b_router_expert[l],
                      w_gate[l], w_up[l], w_down[l])
        x = _layernorm(DEEPNORM_ALPHA * x + g2 * y, ln2_g[l], ln2_b[l])
    return x
```

```python
import functools

import jax
import jax.numpy as jnp
import numpy as np
from jax import lax
from jax.experimental import pallas as pl
from jax.experimental.pallas import tpu as pltpu

D_MODEL = 2048
BATCH = 4
SEQ = 2048
N_TOK = BATCH * SEQ

MLA_HEADS = 8
QK_NOPE_DIM = 128
QK_ROPE_DIM = 64
QK_DIM = QK_NOPE_DIM + QK_ROPE_DIM
V_HEAD_DIM = 128
Q_LORA_RANK = 768
KV_LORA_RANK = 512
ROPE_THETA = 10000.0
MLA_WIDTH = MLA_HEADS * V_HEAD_DIM

SGU_GROUPS = 8
SGU_GROUP_DIM = 128
SGU_CHUNK = 128
SGU_WIDTH = SGU_GROUPS * SGU_GROUP_DIM

N_GROUPS = 4
EXPERTS_PER_GROUP = 8
N_EXPERTS = N_GROUPS * EXPERTS_PER_GROUP
TOP_K = 2
EXPERT_FF = 512

DEEPNORM_ALPHA = 2.0 ** 0.25
EPS = 1e-6
N_MOD = 6
NEG_BIG = -1e30

LANES = 128
VMEM_LIMIT = 56 * 1024 * 1024

ADA_TN = 1024
TOK_TM = 256
ATT_TQ = 512
ATT_TK = 512
MOE_TM = 256
MOE_TILES = (N_TOK * TOP_K + N_EXPERTS * (MOE_TM - 1)) // MOE_TM + 1
MOE_ROWS = MOE_TILES * MOE_TM

F32 = jnp.float32
BF16 = jnp.bfloat16


def _cparams(sem):
    return pltpu.CompilerParams(dimension_semantics=sem, vmem_limit_bytes=VMEM_LIMIT)


def _const_spec(shape):
    nd = len(shape)
    return pl.BlockSpec(shape, lambda *_: (0,) * nd, pipeline_mode=pl.Buffered(1))


def _ln_rows(x):
    mu = jnp.mean(x, axis=-1, keepdims=True)
    xc = x - mu
    var = jnp.mean(xc * xc, axis=-1, keepdims=True)
    return xc * lax.rsqrt(var + EPS)


def _rms_rows(x):
    return x * lax.rsqrt(jnp.mean(x * x, axis=-1, keepdims=True) + EPS)


def _gelu_tanh(x):
    c = np.sqrt(2.0 / np.pi).astype(np.float32)
    return 0.5 * x * (1.0 + jnp.tanh(c * (x + 0.044715 * (x * x * x))))


def _ada_body(c_ref, w_ref, b_ref, o_ref):
    o_ref[...] = jnp.dot(c_ref[...].astype(BF16), w_ref[...].astype(BF16),
                         preferred_element_type=F32) + b_ref[...]


def _ada(c, w, b):
    n = w.shape[1]
    return pl.pallas_call(
        _ada_body,
        grid=(n // ADA_TN,),
        in_specs=[pl.BlockSpec((BATCH, D_MODEL), lambda j: (0, 0)),
                  pl.BlockSpec((D_MODEL, ADA_TN), lambda j: (0, j)),
                  pl.BlockSpec((1, ADA_TN), lambda j: (0, j))],
        out_specs=pl.BlockSpec((BATCH, ADA_TN), lambda j: (0, j)),
        out_shape=jax.ShapeDtypeStruct((BATCH, n), F32),
        compiler_params=_cparams(("parallel",)),
        name="ada",
    )(c, w, b)


def _inproj_body(x_ref, mod_ref, wq_ref, wkv_ref, wkpe_ref, wz_ref, gq_ref, gkv_ref, sg_ref, sb_ref,
                 cq_ref, ckv_ref, kpe_ref, u_ref, vs_ref):
    sh = mod_ref[0, 0:1, :]
    sc = mod_ref[0, 1:2, :]
    h = (_ln_rows(x_ref[...]) * (1.0 + sc) + sh).astype(BF16)
    cq = jnp.dot(h, wq_ref[...], preferred_element_type=F32)
    cq_ref[...] = (_rms_rows(cq) * gq_ref[...]).astype(BF16)
    ckv = jnp.dot(h, wkv_ref[...], preferred_element_type=F32)
    ckv_ref[...] = (_rms_rows(ckv) * gkv_ref[...]).astype(BF16)
    kpe_ref[...] = jnp.dot(h, wkpe_ref[...], preferred_element_type=F32)
    gz = _gelu_tanh(jnp.dot(h, wz_ref[...], preferred_element_type=F32))
    u_ref[...] = gz[:, :SGU_WIDTH]
    vs_ref[...] = (_ln_rows(gz[:, SGU_WIDTH:]) * sg_ref[...] + sb_ref[...]).astype(BF16)


def _inproj(x2, mod3, wq, wkv, wkpe, wz, gq, gkv, sg, sb):
    tm = TOK_TM
    tiles_per_batch = SEQ // tm
    row = lambda w: pl.BlockSpec((tm, w), lambda i: (i, 0))
    return pl.pallas_call(
        _inproj_body,
        grid=(N_TOK // tm,),
        in_specs=[row(D_MODEL),
                  pl.BlockSpec((1, N_MOD, D_MODEL), lambda i: (i // tiles_per_batch, 0, 0)),
                  _const_spec(wq.shape), _const_spec(wkv.shape), _const_spec(wkpe.shape), _const_spec(wz.shape),
                  _const_spec(gq.shape), _const_spec(gkv.shape), _const_spec(sg.shape), _const_spec(sb.shape)],
        out_specs=[row(Q_LORA_RANK), row(KV_LORA_RANK), row(LANES), row(SGU_WIDTH), row(SGU_WIDTH)],
        out_shape=[jax.ShapeDtypeStruct((N_TOK, Q_LORA_RANK), BF16),
                   jax.ShapeDtypeStruct((N_TOK, KV_LORA_RANK), BF16),
                   jax.ShapeDtypeStruct((N_TOK, LANES), F32),
                   jax.ShapeDtypeStruct((N_TOK, SGU_WIDTH), F32),
                   jax.ShapeDtypeStruct((N_TOK, SGU_WIDTH), BF16)],
        compiler_params=_cparams(("parallel",)),
        name="inproj",
    )(x2, mod3, wq, wkv, wkpe, wz, gq, gkv, sg, sb)


def _rope(x, cos, sin):
    w = x.shape[-1]
    lane = lax.broadcasted_iota(jnp.int32, x.shape, 1)
    first_half = (lane % QK_ROPE_DIM) < (QK_ROPE_DIM // 2)
    rot = jnp.where(first_half,
                    -pltpu.roll(x, w - QK_ROPE_DIM // 2, 1),
                    pltpu.roll(x, QK_ROPE_DIM // 2, 1))
    return x * cos + rot * sin


def _qkv_body(cq_ref, ckv_ref, kpe_ref, pos_ref, invf_ref, wuq_ref, wukv_ref, q_ref, k_ref, v_ref):
    ang = pos_ref[...].astype(F32) * invf_ref[...]
    cos = jnp.cos(ang)
    sin = jnp.sin(ang)
    scale = np.float32(QK_DIM ** -0.5)
    q = jnp.dot(cq_ref[...], wuq_ref[...], preferred_element_type=F32) * scale
    q_pe = _rope(q[:, MLA_HEADS * QK_NOPE_DIM:], cos, sin)
    kv = jnp.dot(ckv_ref[...], wukv_ref[...], preferred_element_type=F32)
    k_pe = _rope(kpe_ref[...], cos[:, :LANES], sin[:, :LANES])[:, :QK_ROPE_DIM].astype(BF16)
    for h in range(MLA_HEADS):
        q_ref[0, h, :, 0:QK_NOPE_DIM] = q[:, h * QK_NOPE_DIM:(h + 1) * QK_NOPE_DIM].astype(BF16)
        q_ref[0, h, :, QK_NOPE_DIM:QK_DIM] = q_pe[:, h * QK_ROPE_DIM:(h + 1) * QK_ROPE_DIM].astype(BF16)
        k_ref[0, h, :, 0:QK_NOPE_DIM] = kv[:, h * QK_NOPE_DIM:(h + 1) * QK_NOPE_DIM].astype(BF16)
        k_ref[0, h, :, QK_NOPE_DIM:QK_DIM] = k_pe
        v_ref[0, h, :, :] = kv[:, MLA_WIDTH + h * V_HEAD_DIM:MLA_WIDTH + (h + 1) * V_HEAD_DIM].astype(BF16)


def _qkv(cqn, ckvn, kpe, pos2, invf, wuq, wukv):
    tm = TOK_TM
    tpb = SEQ // tm
    row = lambda w: pl.BlockSpec((tm, w), lambda i: (i, 0))
    head_out = lambda w: pl.BlockSpec((1, MLA_HEADS, tm, w), lambda i: (i // tpb, 0, i % tpb, 0))
    return pl.pallas_call(
        _qkv_body,
        grid=(N_TOK // tm,),
        in_specs=[row(Q_LORA_RANK), row(KV_LORA_RANK), row(LANES), row(1),
                  _const_spec(invf.shape), _const_spec(wuq.shape), _const_spec(wukv.shape)],
        out_specs=[head_out(QK_DIM), head_out(QK_DIM), head_out(V_HEAD_DIM)],
        out_shape=[jax.ShapeDtypeStruct((BATCH, MLA_HEADS, SEQ, QK_DIM), BF16),
                   jax.ShapeDtypeStruct((BATCH, MLA_HEADS, SEQ, QK_DIM), BF16),
                   jax.ShapeDtypeStruct((BATCH, MLA_HEADS, SEQ, V_HEAD_DIM), BF16)],
        compiler_params=_cparams(("parallel",)),
        name="qkv",
    )(cqn, ckvn, kpe, pos2, invf, wuq, wukv)


def _attn_body(q_ref, k_ref, v_ref, o_ref):
    i = pl.program_id(2)
    q = q_ref[0, 0]

    def step(j, carry, masked):
        m, l, acc = carry
        start = pl.multiple_of(j * ATT_TK, ATT_TK)
        k = k_ref[0, 0, pl.ds(start, ATT_TK), :]
        v = v_ref[0, 0, pl.ds(start, ATT_TK), :]
        s = lax.dot_general(q, k, (((1,), (1,)), ((), ())), preferred_element_type=F32)
        if masked:
            r = lax.broadcasted_iota(jnp.int32, s.shape, 0)
            c = lax.broadcasted_iota(jnp.int32, s.shape, 1)
            s = jnp.where(c <= r, s, NEG_BIG)
        m_new = jnp.maximum(m, jnp.max(s, axis=-1, keepdims=True))
        p = jnp.exp(s - m_new)
        a = jnp.exp(m - m_new)
        l = a * l + jnp.sum(p, axis=-1, keepdims=True)
        acc = a * acc + jnp.dot(p.astype(BF16), v, preferred_element_type=F32)
        return m_new, l, acc

    init = (jnp.full((ATT_TQ, 1), NEG_BIG, F32), jnp.zeros((ATT_TQ, 1), F32),
            jnp.zeros((ATT_TQ, V_HEAD_DIM), F32))
    carry = lax.fori_loop(0, i, lambda j, c: step(j, c, False), init)
    m, l, acc = step(i, carry, True)
    o_ref[0] = (acc / l).astype(BF16)


def _attn(q, k, v):
    assert ATT_TQ == ATT_TK
    return pl.pallas_call(
        _attn_body,
        grid=(BATCH, MLA_HEADS, SEQ // ATT_TQ),
        in_specs=[pl.BlockSpec((1, 1, ATT_TQ, QK_DIM), lambda b, h, i: (b, h, i, 0)),
                  pl.BlockSpec((1, 1, SEQ, QK_DIM), lambda b, h, i: (b, h, 0, 0)),
                  pl.BlockSpec((1, 1, SEQ, V_HEAD_DIM), lambda b, h, i: (b, h, 0, 0))],
        out_specs=pl.BlockSpec((1, ATT_TQ, V_HEAD_DIM), lambda b, h, i: (b, i, h)),
        out_shape=jax.ShapeDtypeStruct((BATCH, SEQ, MLA_WIDTH), BF16),
        compiler_params=_cparams(("parallel", "parallel", "arbitrary")),
        name="attn",
    )(q, k, v)


def _mixout_body(x_ref, mod_ref, attn_ref, u_ref, vs_ref, wsp_ref, bsp_ref, woa_ref, wos_ref,
                 g1_ref, b1_ref, wr_ref, br_ref, x1_ref, h2_ref, lg_ref, sgu_scr):
    r = lax.broadcasted_iota(jnp.int32, (SGU_CHUNK, SGU_CHUNK), 0)
    c = lax.broadcasted_iota(jnp.int32, (SGU_CHUNK, SGU_CHUNK), 1)
    causal = c <= r
    for g in range(SGU_GROUPS):
        ws = jnp.where(causal, wsp_ref[g], 0.0).astype(BF16)
        bias = bsp_ref[:, g:g + 1]
        cols = slice(g * SGU_GROUP_DIM, (g + 1) * SGU_GROUP_DIM)
        for ch in range(TOK_TM // SGU_CHUNK):
            rows = slice(ch * SGU_CHUNK, (ch + 1) * SGU_CHUNK)
            mixed = jnp.dot(ws, vs_ref[rows, cols], preferred_element_type=F32) + bias
            sgu_scr[rows, cols] = (u_ref[rows, cols] * mixed).astype(BF16)
    y = (jnp.dot(attn_ref[...], woa_ref[...], preferred_element_type=F32)
         + jnp.dot(sgu_scr[...], wos_ref[...], preferred_element_type=F32))
    gate1 = mod_ref[0, 2:3, :]
    sh2 = mod_ref[0, 3:4, :]
    sc2 = mod_ref[0, 4:5, :]
    x1 = _ln_rows(DEEPNORM_ALPHA * x_ref[...] + gate1 * y) * g1_ref[...] + b1_ref[...]
    x1_ref[...] = x1
    h2 = (_ln_rows(x1) * (1.0 + sc2) + sh2).astype(BF16)
    h2_ref[...] = h2
    lg_ref[...] = jnp.dot(h2, wr_ref[...], preferred_element_type=F32) + br_ref[...]


def _mixout(x2, mod3, attn, u, vs, wsp, bsp_t, woa, wos, g1, b1, wr, br):
    tm = TOK_TM
    tpb = SEQ // tm
    row = lambda w: pl.BlockSpec((tm, w), lambda i: (i, 0))
    return pl.pallas_call(
        _mixout_body,
        grid=(N_TOK // tm,),
        in_specs=[row(D_MODEL),
                  pl.BlockSpec((1, N_MOD, D_MODEL), lambda i: (i // tpb, 0, 0)),
                  row(MLA_WIDTH), row(SGU_WIDTH), row(SGU_WIDTH),
                  _const_spec(wsp.shape), _const_spec(bsp_t.shape), _const_spec(woa.shape), _const_spec(wos.shape),
                  _const_spec(g1.shape), _const_spec(b1.shape), _const_spec(wr.shape), _const_spec(br.shape)],
        out_specs=[row(D_MODEL), row(D_MODEL), row(LANES)],
        out_shape=[jax.ShapeDtypeStruct((N_TOK, D_MODEL), F32),
                   jax.ShapeDtypeStruct((N_TOK, D_MODEL), BF16),
                   jax.ShapeDtypeStruct((N_TOK, LANES), F32)],
        scratch_shapes=[pltpu.VMEM((tm, SGU_WIDTH), BF16)],
        compiler_params=_cparams(("parallel",)),
        name="mix_out",
    )(x2, mod3, attn, u, vs, wsp, bsp_t, woa, wos, g1, b1, wr, br)


def _route_body(lg_ref, o_ref):
    lg = lg_ref[...]
    lane = lax.broadcasted_iota(jnp.int32, lg.shape, 1)
    big = jnp.int32(LANES)

    def top1(vals):
        m = jnp.max(vals, axis=-1, keepdims=True)
        idx = jnp.min(jnp.where(vals == m, lane, big), axis=-1, keepdims=True)
        return m, idx

    is_group = lane < N_GROUPS
    glog = jnp.where(is_group, lg, -jnp.inf)
    gmax, gidx = top1(glog)
    pg_top = 1.0 / jnp.sum(jnp.exp(glog - gmax), axis=-1, keepdims=True)
    eid = lane - N_GROUPS
    sel = (eid >= gidx * EXPERTS_PER_GROUP) & (eid < (gidx + 1) * EXPERTS_PER_GROUP)
    elog = jnp.where(sel, lg, -jnp.inf)
    m1, i1 = top1(elog)
    m2, i2 = top1(jnp.where(lane == i1, -jnp.inf, elog))
    e2 = jnp.exp(m2 - m1)
    w1 = pg_top / (1.0 + e2)
    w2 = pg_top * e2 / (1.0 + e2)
    out = jnp.where(lane == 0, (i1 - N_GROUPS).astype(F32),
                    jnp.where(lane == 1, (i2 - N_GROUPS).astype(F32),
                              jnp.where(lane == 2, w1, jnp.where(lane == 3, w2, 0.0))))
    o_ref[...] = out


def _route(logits):
    tm = 1024
    return pl.pallas_call(
        _route_body,
        grid=(N_TOK // tm,),
        in_specs=[pl.BlockSpec((tm, LANES), lambda i: (i, 0))],
        out_specs=pl.BlockSpec((tm, LANES), lambda i: (i, 0)),
        out_shape=jax.ShapeDtypeStruct((N_TOK, LANES), F32),
        compiler_params=_cparams(("parallel",)),
        name="route",
    )(logits)


def _moe_body(te_ref, tv_ref, x_ref, wg_ref, wu_ref, wd_ref, y_ref):
    i = pl.program_id(0)

    @pl.when(tv_ref[i] > 0)
    def _():
        x = x_ref[...]
        g = jnp.dot(x, wg_ref[0].astype(BF16), preferred_element_type=F32)
        u = jnp.dot(x, wu_ref[0].astype(BF16), preferred_element_type=F32)
        hid = (g * jax.nn.sigmoid(g) * u).astype(BF16)
        y_ref[...] = jnp.dot(hid, wd_ref[0].astype(BF16), preferred_element_type=F32).astype(BF16)

    @pl.when(tv_ref[i] == 0)
    def _():
        y_ref[...] = jnp.zeros_like(y_ref)


def _moe(tile_expert, tile_valid, xs, wg, wu, wd):
    tm = MOE_TM
    grid_spec = pltpu.PrefetchScalarGridSpec(
        num_scalar_prefetch=2,
        grid=(MOE_TILES,),
        in_specs=[pl.BlockSpec((tm, D_MODEL), lambda i, te, tv: (i, 0)),
                  pl.BlockSpec((1, D_MODEL, EXPERT_FF), lambda i, te, tv: (te[i], 0, 0)),
                  pl.BlockSpec((1, D_MODEL, EXPERT_FF), lambda i, te, tv: (te[i], 0, 0)),
                  pl.BlockSpec((1, EXPERT_FF, D_MODEL), lambda i, te, tv: (te[i], 0, 0))],
        out_specs=pl.BlockSpec((tm, D_MODEL), lambda i, te, tv: (i, 0)),
    )
    return pl.pallas_call(
        _moe_body,
        grid_spec=grid_spec,
        out_shape=jax.ShapeDtypeStruct((MOE_ROWS, D_MODEL), BF16),
        compiler_params=_cparams(("arbitrary",)),
        name="moe",
    )(tile_expert, tile_valid, xs, wg, wu, wd)


def _final_body(x1_ref, mod_ref, ya_ref, yb_ref, rt_ref, g2_ref, b2_ref, o_ref):
    w0 = rt_ref[:, 2:3]
    w1 = rt_ref[:, 3:4]
    y = w0 * ya_ref[...].astype(F32) + w1 * yb_ref[...].astype(F32)
    gate2 = mod_ref[0, 5:6, :]
    o_ref[...] = _ln_rows(DEEPNORM_ALPHA * x1_ref[...] + gate2 * y) * g2_ref[...] + b2_ref[...]


def _final(x1, mod3, ya, yb, route, g2, b2):
    tm = 512
    tpb = SEQ // tm
    row = lambda w: pl.BlockSpec((tm, w), lambda i: (i, 0))
    return pl.pallas_call(
        _final_body,
        grid=(N_TOK // tm,),
        in_specs=[row(D_MODEL),
                  pl.BlockSpec((1, N_MOD, D_MODEL), lambda i: (i // tpb, 0, 0)),
                  row(D_MODEL), row(D_MODEL), row(LANES),
                  _const_spec(g2.shape), _const_spec(b2.shape)],
        out_specs=row(D_MODEL),
        out_shape=jax.ShapeDtypeStruct((N_TOK, D_MODEL), F32),
        compiler_params=_cparams(("parallel",)),
        name="final",
    )(x1, mod3, ya, yb, route, g2, b2)


def _dispatch_plan(e_ids):
    flat = e_ids.reshape(-1)
    onehot = (flat[:, None] == jnp.arange(N_EXPERTS, dtype=jnp.int32)[None, :]).astype(jnp.int32)
    csum = jnp.cumsum(onehot, axis=0)
    rank = jnp.take_along_axis(csum, flat[:, None], axis=1)[:, 0] - 1
    counts = csum[-1]
    tiles = (counts + MOE_TM - 1) // MOE_TM
    tile_end = jnp.cumsum(tiles)
    offsets = (tile_end - tiles) * MOE_TM
    pos = offsets[flat] + rank
    tile_ids = jnp.arange(MOE_TILES, dtype=jnp.int32)
    tile_expert = jnp.minimum(jnp.searchsorted(tile_end, tile_ids, side="right"), N_EXPERTS - 1).astype(jnp.int32)
    tile_valid = (tile_ids < tile_end[-1]).astype(jnp.int32)
    last_valid = jnp.max(jnp.where(tile_valid > 0, tile_expert, 0))
    tile_expert = jnp.where(tile_valid > 0, tile_expert, last_valid)
    tok = jnp.arange(N_TOK * TOP_K, dtype=jnp.int32) // TOP_K
    src_token = jnp.zeros((MOE_ROWS,), jnp.int32).at[pos].set(tok)
    return pos.reshape(N_TOK, TOP_K), src_token, tile_expert, tile_valid


def kernel(x, c, positions, w_ada, b_ada, w_in, q_norm_g, w_uq, kv_norm_g, w_ukv, sgu_norm_g, sgu_norm_b,
           w_spatial, b_spatial, w_o, ln1_g, ln1_b, w_router_group, b_router_group, w_router_expert,
           b_router_expert, w_gate, w_up, w_down, ln2_g, ln2_b):
    l = 0
    x2 = x.reshape(N_TOK, D_MODEL)
    mod3 = _ada(c, w_ada[l], b_ada[l][None, :]).reshape(BATCH, N_MOD, D_MODEL)

    w_in_l = w_in[l]
    o1, o2, o3 = Q_LORA_RANK, Q_LORA_RANK + KV_LORA_RANK, Q_LORA_RANK + KV_LORA_RANK + QK_ROPE_DIM
    wq = w_in_l[:, :o1].astype(BF16)
    wkv = w_in_l[:, o1:o2].astype(BF16)
    wkpe = jnp.pad(w_in_l[:, o2:o3], ((0, 0), (0, LANES - QK_ROPE_DIM))).astype(BF16)
    wz = w_in_l[:, o3:].astype(BF16)
    wuq3 = w_uq[l].reshape(Q_LORA_RANK, MLA_HEADS, QK_DIM)
    wuq = jnp.concatenate([wuq3[:, :, :QK_NOPE_DIM].reshape(Q_LORA_RANK, -1),
                           wuq3[:, :, QK_NOPE_DIM:].reshape(Q_LORA_RANK, -1)], axis=1).astype(BF16)
    wukv3 = w_ukv[l].reshape(KV_LORA_RANK, MLA_HEADS, QK_NOPE_DIM + V_HEAD_DIM)
    wukv = jnp.concatenate([wukv3[:, :, :QK_NOPE_DIM].reshape(KV_LORA_RANK, -1),
                            wukv3[:, :, QK_NOPE_DIM:].reshape(KV_LORA_RANK, -1)], axis=1).astype(BF16)
    woa = w_o[l][:MLA_WIDTH].astype(BF16)
    wos = w_o[l][MLA_WIDTH:].astype(BF16)
    n_r = N_GROUPS + N_EXPERTS
    wr = jnp.pad(jnp.concatenate([w_router_group[l], w_router_expert[l]], axis=1),
                 ((0, 0), (0, LANES - n_r))).astype(BF16)
    br = jnp.pad(jnp.concatenate([b_router_group[l], b_router_expert[l]]), (0, LANES - n_r))[None, :]
    inv_freq = 1.0 / (ROPE_THETA ** (jnp.arange(0, QK_ROPE_DIM, 2, dtype=F32) / QK_ROPE_DIM))
    invf = jnp.tile(inv_freq, MLA_HEADS * 2)[None, :]

    cqn, ckvn, kpe, u, vs = _inproj(x2, mod3, wq, wkv, wkpe, wz, q_norm_g[l][None, :], kv_norm_g[l][None, :],
                                    sgu_norm_g[l][None, :], sgu_norm_b[l][None, :])
    q, k, v = _qkv(cqn, ckvn, kpe, positions.reshape(N_TOK, 1), invf, wuq, wukv)
    attn = _attn(q, k, v).reshape(N_TOK, MLA_WIDTH)
    x1, h2, logits = _mixout(x2, mod3, attn, u, vs, w_spatial[l], b_spatial[l].T, woa, wos,
                             ln1_g[l][None, :], ln1_b[l][None, :], wr, br)
    route = _route(logits)
    e_ids = route[:, :TOP_K].astype(jnp.int32)
    pos, src_token, tile_expert, tile_valid = _dispatch_plan(e_ids)
    xs = jnp.take(h2, src_token, axis=0)
    ys = _moe(tile_expert, tile_valid, xs, w_gate[l], w_up[l], w_down[l])
    ya = jnp.take(ys, pos[:, 0], axis=0)
    yb = jnp.take(ys, pos[:, 1], axis=0)
    out = _final(x1, mod3, ya, yb, route, ln2_g[l][None, :], ln2_b[l][None, :])
    return out.reshape(BATCH, SEQ, D_MODEL)
```

```python
import functools

import jax
import jax.numpy as jnp
import numpy as np
from jax import lax
from jax.experimental import pallas as pl
from jax.experimental.pallas import tpu as pltpu

D_MODEL = 2048
BATCH = 4
SEQ = 2048
N_TOK = BATCH * SEQ

MLA_HEADS = 8
QK_NOPE_DIM = 128
QK_ROPE_DIM = 64
QK_DIM = QK_NOPE_DIM + QK_ROPE_DIM
V_HEAD_DIM = 128
Q_LORA_RANK = 768
KV_LORA_RANK = 512
ROPE_THETA = 10000.0
MLA_WIDTH = MLA_HEADS * V_HEAD_DIM

SGU_GROUPS = 8
SGU_GROUP_DIM = 128
SGU_CHUNK = 128
SGU_WIDTH = SGU_GROUPS * SGU_GROUP_DIM

N_GROUPS = 4
EXPERTS_PER_GROUP = 8
N_EXPERTS = N_GROUPS * EXPERTS_PER_GROUP
TOP_K = 2
EXPERT_FF = 512

DEEPNORM_ALPHA = 2.0 ** 0.25
EPS = 1e-6
N_MOD = 6
NEG_BIG = -1e30

LANES = 128
VMEM_LIMIT = 56 * 1024 * 1024

ADA_TN = 1024
TOK_TM = 256
ATT_TQ = 512
ATT_TK = 512
MOE_TM = 256
MOE_TILES = (N_TOK * TOP_K + N_EXPERTS * (MOE_TM - 1)) // MOE_TM + 1
MOE_ROWS = MOE_TILES * MOE_TM
PLAN_TM = 512
DISPATCH_CHUNK = 256
FINAL_TM = 256
assert MOE_TILES <= LANES

F32 = jnp.float32
BF16 = jnp.bfloat16
U32 = jnp.uint32
HALF_D = D_MODEL // 2


def _cparams(sem):
    return pltpu.CompilerParams(dimension_semantics=sem, vmem_limit_bytes=VMEM_LIMIT)


def _const_spec(shape):
    nd = len(shape)
    return pl.BlockSpec(shape, lambda *_: (0,) * nd, pipeline_mode=pl.Buffered(1))


def _ln_rows(x):
    mu = jnp.mean(x, axis=-1, keepdims=True)
    xc = x - mu
    var = jnp.mean(xc * xc, axis=-1, keepdims=True)
    return xc * lax.rsqrt(var + EPS)


def _rms_rows(x):
    return x * lax.rsqrt(jnp.mean(x * x, axis=-1, keepdims=True) + EPS)


def _pack_halves(x):
    half = x.shape[-1] // 2
    return pltpu.pack_elementwise([x[:, :half], x[:, half:]], packed_dtype=BF16)


def _unpack_halves(w):
    lo = pltpu.unpack_elementwise(w, index=0, packed_dtype=BF16, unpacked_dtype=F32)
    hi = pltpu.unpack_elementwise(w, index=1, packed_dtype=BF16, unpacked_dtype=F32)
    return lo, hi


def _gelu_tanh(x):
    c = np.sqrt(2.0 / np.pi).astype(np.float32)
    return 0.5 * x * (1.0 + jnp.tanh(c * (x + 0.044715 * (x * x * x))))


def _ada_body(c_ref, w_ref, b_ref, o_ref):
    o_ref[...] = jnp.dot(c_ref[...].astype(BF16), w_ref[...].astype(BF16),
                         preferred_element_type=F32) + b_ref[...]


def _ada(c, w, b):
    n = w.shape[1]
    return pl.pallas_call(
        _ada_body,
        grid=(n // ADA_TN,),
        in_specs=[pl.BlockSpec((BATCH, D_MODEL), lambda j: (0, 0)),
                  pl.BlockSpec((D_MODEL, ADA_TN), lambda j: (0, j)),
                  pl.BlockSpec((1, ADA_TN), lambda j: (0, j))],
        out_specs=pl.BlockSpec((BATCH, ADA_TN), lambda j: (0, j)),
        out_shape=jax.ShapeDtypeStruct((BATCH, n), F32),
        compiler_params=_cparams(("parallel",)),
        name="ada",
    )(c, w, b)


def _inproj_body(x_ref, mod_ref, wq_ref, wkv_ref, wkpe_ref, wz_ref, gq_ref, gkv_ref, sg_ref, sb_ref,
                 cq_ref, ckv_ref, kpe_ref, u_ref, vs_ref):
    sh = mod_ref[0, 0:1, :]
    sc = mod_ref[0, 1:2, :]
    h = (_ln_rows(x_ref[...]) * (1.0 + sc) + sh).astype(BF16)
    cq = jnp.dot(h, wq_ref[...], preferred_element_type=F32)
    cq_ref[...] = (_rms_rows(cq) * gq_ref[...]).astype(BF16)
    ckv = jnp.dot(h, wkv_ref[...], preferred_element_type=F32)
    ckv_ref[...] = (_rms_rows(ckv) * gkv_ref[...]).astype(BF16)
    kpe_ref[...] = jnp.dot(h, wkpe_ref[...], preferred_element_type=F32)
    gz = _gelu_tanh(jnp.dot(h, wz_ref[...], preferred_element_type=F32))
    u_ref[...] = gz[:, :SGU_WIDTH]
    vs_ref[...] = (_ln_rows(gz[:, SGU_WIDTH:]) * sg_ref[...] + sb_ref[...]).astype(BF16)


def _inproj(x2, mod3, wq, wkv, wkpe, wz, gq, gkv, sg, sb):
    tm = TOK_TM
    tiles_per_batch = SEQ // tm
    row = lambda w: pl.BlockSpec((tm, w), lambda i: (i, 0))
    return pl.pallas_call(
        _inproj_body,
        grid=(N_TOK // tm,),
        in_specs=[row(D_MODEL),
                  pl.BlockSpec((1, N_MOD, D_MODEL), lambda i: (i // tiles_per_batch, 0, 0)),
                  _const_spec(wq.shape), _const_spec(wkv.shape), _const_spec(wkpe.shape), _const_spec(wz.shape),
                  _const_spec(gq.shape), _const_spec(gkv.shape), _const_spec(sg.shape), _const_spec(sb.shape)],
        out_specs=[row(Q_LORA_RANK), row(KV_LORA_RANK), row(LANES), row(SGU_WIDTH), row(SGU_WIDTH)],
        out_shape=[jax.ShapeDtypeStruct((N_TOK, Q_LORA_RANK), BF16),
                   jax.ShapeDtypeStruct((N_TOK, KV_LORA_RANK), BF16),
                   jax.ShapeDtypeStruct((N_TOK, LANES), F32),
                   jax.ShapeDtypeStruct((N_TOK, SGU_WIDTH), F32),
                   jax.ShapeDtypeStruct((N_TOK, SGU_WIDTH), BF16)],
        compiler_params=_cparams(("parallel",)),
        name="inproj",
    )(x2, mod3, wq, wkv, wkpe, wz, gq, gkv, sg, sb)


def _rope(x, cos, sin):
    w = x.shape[-1]
    lane = lax.broadcasted_iota(jnp.int32, x.shape, 1)
    first_half = (lane % QK_ROPE_DIM) < (QK_ROPE_DIM // 2)
    rot = jnp.where(first_half,
                    -pltpu.roll(x, w - QK_ROPE_DIM // 2, 1),
                    pltpu.roll(x, QK_ROPE_DIM // 2, 1))
    return x * cos + rot * sin


def _qkv_body(cq_ref, ckv_ref, kpe_ref, pos_ref, invf_ref, wuq_ref, wukv_ref, q_ref, k_ref, v_ref):
    ang = pos_ref[...].astype(F32) * invf_ref[...]
    cos = jnp.cos(ang)
    sin = jnp.sin(ang)
    scale = np.float32(QK_DIM ** -0.5)
    q = jnp.dot(cq_ref[...], wuq_ref[...], preferred_element_type=F32) * scale
    q_pe = _rope(q[:, MLA_HEADS * QK_NOPE_DIM:], cos, sin)
    kv = jnp.dot(ckv_ref[...], wukv_ref[...], preferred_element_type=F32)
    k_pe = _rope(kpe_ref[...], cos[:, :LANES], sin[:, :LANES])[:, :QK_ROPE_DIM].astype(BF16)
    for h in range(MLA_HEADS):
        q_ref[0, h, :, 0:QK_NOPE_DIM] = q[:, h * QK_NOPE_DIM:(h + 1) * QK_NOPE_DIM].astype(BF16)
        q_ref[0, h, :, QK_NOPE_DIM:QK_DIM] = q_pe[:, h * QK_ROPE_DIM:(h + 1) * QK_ROPE_DIM].astype(BF16)
        k_ref[0, h, :, 0:QK_NOPE_DIM] = kv[:, h * QK_NOPE_DIM:(h + 1) * QK_NOPE_DIM].astype(BF16)
        k_ref[0, h, :, QK_NOPE_DIM:QK_DIM] = k_pe
        v_ref[0, h, :, :] = kv[:, MLA_WIDTH + h * V_HEAD_DIM:MLA_WIDTH + (h + 1) * V_HEAD_DIM].astype(BF16)


def _qkv(cqn, ckvn, kpe, pos2, invf, wuq, wukv):
    tm = TOK_TM
    tpb = SEQ // tm
    row = lambda w: pl.BlockSpec((tm, w), lambda i: (i, 0))
    head_out = lambda w: pl.BlockSpec((1, MLA_HEADS, tm, w), lambda i: (i // tpb, 0, i % tpb, 0))
    return pl.pallas_call(
        _qkv_body,
        grid=(N_TOK // tm,),
        in_specs=[row(Q_LORA_RANK), row(KV_LORA_RANK), row(LANES), row(1),
                  _const_spec(invf.shape), _const_spec(wuq.shape), _const_spec(wukv.shape)],
        out_specs=[head_out(QK_DIM), head_out(QK_DIM), head_out(V_HEAD_DIM)],
        out_shape=[jax.ShapeDtypeStruct((BATCH, MLA_HEADS, SEQ, QK_DIM), BF16),
                   jax.ShapeDtypeStruct((BATCH, MLA_HEADS, SEQ, QK_DIM), BF16),
                   jax.ShapeDtypeStruct((BATCH, MLA_HEADS, SEQ, V_HEAD_DIM), BF16)],
        compiler_params=_cparams(("parallel",)),
        name="qkv",
    )(cqn, ckvn, kpe, pos2, invf, wuq, wukv)


def _attn_body(q_ref, k_ref, v_ref, o_ref):
    i = pl.program_id(2)
    q = q_ref[0, 0]

    def step(j, carry, masked):
        m, l, acc = carry
        start = pl.multiple_of(j * ATT_TK, ATT_TK)
        k = k_ref[0, 0, pl.ds(start, ATT_TK), :]
        v = v_ref[0, 0, pl.ds(start, ATT_TK), :]
        s = lax.dot_general(q, k, (((1,), (1,)), ((), ())), preferred_element_type=F32)
        if masked:
            r = lax.broadcasted_iota(jnp.int32, s.shape, 0)
            c = lax.broadcasted_iota(jnp.int32, s.shape, 1)
            s = jnp.where(c <= r, s, NEG_BIG)
        m_new = jnp.maximum(m, jnp.max(s, axis=-1, keepdims=True))
        p = jnp.exp(s - m_new)
        a = jnp.exp(m - m_new)
        l = a * l + jnp.sum(p, axis=-1, keepdims=True)
        acc = a * acc + jnp.dot(p.astype(BF16), v, preferred_element_type=F32)
        return m_new, l, acc

    init = (jnp.full((ATT_TQ, 1), NEG_BIG, F32), jnp.zeros((ATT_TQ, 1), F32),
            jnp.zeros((ATT_TQ, V_HEAD_DIM), F32))
    carry = lax.fori_loop(0, i, lambda j, c: step(j, c, False), init)
    m, l, acc = step(i, carry, True)
    o_ref[0] = (acc / l).astype(BF16)


def _attn(q, k, v):
    assert ATT_TQ == ATT_TK
    return pl.pallas_call(
        _attn_body,
        grid=(BATCH, MLA_HEADS, SEQ // ATT_TQ),
        in_specs=[pl.BlockSpec((1, 1, ATT_TQ, QK_DIM), lambda b, h, i: (b, h, i, 0)),
                  pl.BlockSpec((1, 1, SEQ, QK_DIM), lambda b, h, i: (b, h, 0, 0)),
                  pl.BlockSpec((1, 1, SEQ, V_HEAD_DIM), lambda b, h, i: (b, h, 0, 0))],
        out_specs=pl.BlockSpec((1, ATT_TQ, V_HEAD_DIM), lambda b, h, i: (b, i, h)),
        out_shape=jax.ShapeDtypeStruct((BATCH, SEQ, MLA_WIDTH), BF16),
        compiler_params=_cparams(("parallel", "parallel", "arbitrary")),
        name="attn",
    )(q, k, v)


def _mixout_body(x_ref, mod_ref, attn_ref, u_ref, vs_ref, wsp_ref, bsp_ref, woa_ref, wos_ref,
                 g1_ref, b1_ref, wr_ref, br_ref, x1_ref, h2_ref, lg_ref, sgu_scr):
    r = lax.broadcasted_iota(jnp.int32, (SGU_CHUNK, SGU_CHUNK), 0)
    c = lax.broadcasted_iota(jnp.int32, (SGU_CHUNK, SGU_CHUNK), 1)
    causal = c <= r
    for g in range(SGU_GROUPS):
        ws = jnp.where(causal, wsp_ref[g], 0.0).astype(BF16)
        bias = bsp_ref[:, g:g + 1]
        cols = slice(g * SGU_GROUP_DIM, (g + 1) * SGU_GROUP_DIM)
        for ch in range(TOK_TM // SGU_CHUNK):
            rows = slice(ch * SGU_CHUNK, (ch + 1) * SGU_CHUNK)
            mixed = jnp.dot(ws, vs_ref[rows, cols], preferred_element_type=F32) + bias
            sgu_scr[rows, cols] = (u_ref[rows, cols] * mixed).astype(BF16)
    y = (jnp.dot(attn_ref[...], woa_ref[...], preferred_element_type=F32)
         + jnp.dot(sgu_scr[...], wos_ref[...], preferred_element_type=F32))
    gate1 = mod_ref[0, 2:3, :]
    sh2 = mod_ref[0, 3:4, :]
    sc2 = mod_ref[0, 4:5, :]
    x1 = _ln_rows(DEEPNORM_ALPHA * x_ref[...] + gate1 * y) * g1_ref[...] + b1_ref[...]
    x1_ref[...] = x1
    h2 = _ln_rows(x1) * (1.0 + sc2) + sh2
    h2_ref[...] = _pack_halves(h2)
    lg_ref[...] = jnp.dot(h2.astype(BF16), wr_ref[...], preferred_element_type=F32) + br_ref[...]


def _mixout(x2, mod3, attn, u, vs, wsp, bsp_t, woa, wos, g1, b1, wr, br):
    tm = TOK_TM
    tpb = SEQ // tm
    row = lambda w: pl.BlockSpec((tm, w), lambda i: (i, 0))
    return pl.pallas_call(
        _mixout_body,
        grid=(N_TOK // tm,),
        in_specs=[row(D_MODEL),
                  pl.BlockSpec((1, N_MOD, D_MODEL), lambda i: (i // tpb, 0, 0)),
                  row(MLA_WIDTH), row(SGU_WIDTH), row(SGU_WIDTH),
                  _const_spec(wsp.shape), _const_spec(bsp_t.shape), _const_spec(woa.shape), _const_spec(wos.shape),
                  _const_spec(g1.shape), _const_spec(b1.shape), _const_spec(wr.shape), _const_spec(br.shape)],
        out_specs=[row(D_MODEL), row(HALF_D), row(LANES)],
        out_shape=[jax.ShapeDtypeStruct((N_TOK, D_MODEL), F32),
                   jax.ShapeDtypeStruct((N_TOK, HALF_D), U32),
                   jax.ShapeDtypeStruct((N_TOK, LANES), F32)],
        scratch_shapes=[pltpu.VMEM((tm, SGU_WIDTH), BF16)],
        compiler_params=_cparams(("parallel",)),
        name="mix_out",
    )(x2, mod3, attn, u, vs, wsp, bsp_t, woa, wos, g1, b1, wr, br)


def _route_body(lg_ref, o_ref):
    lg = lg_ref[...]
    lane = lax.broadcasted_iota(jnp.int32, lg.shape, 1)
    big = jnp.int32(LANES)

    def top1(vals):
        m = jnp.max(vals, axis=-1, keepdims=True)
        idx = jnp.min(jnp.where(vals == m, lane, big), axis=-1, keepdims=True)
        return m, idx

    is_group = lane < N_GROUPS
    glog = jnp.where(is_group, lg, -jnp.inf)
    gmax, gidx = top1(glog)
    pg_top = 1.0 / jnp.sum(jnp.exp(glog - gmax), axis=-1, keepdims=True)
    eid = lane - N_GROUPS
    sel = (eid >= gidx * EXPERTS_PER_GROUP) & (eid < (gidx + 1) * EXPERTS_PER_GROUP)
    elog = jnp.where(sel, lg, -jnp.inf)
    m1, i1 = top1(elog)
    m2, i2 = top1(jnp.where(lane == i1, -jnp.inf, elog))
    e2 = jnp.exp(m2 - m1)
    w1 = pg_top / (1.0 + e2)
    w2 = pg_top * e2 / (1.0 + e2)
    out = jnp.where(lane == 0, (i1 - N_GROUPS).astype(F32),
                    jnp.where(lane == 1, (i2 - N_GROUPS).astype(F32),
                              jnp.where(lane == 2, w1, jnp.where(lane == 3, w2, 0.0))))
    o_ref[...] = out


def _route(logits):
    tm = 1024
    return pl.pallas_call(
        _route_body,
        grid=(N_TOK // tm,),
        in_specs=[pl.BlockSpec((tm, LANES), lambda i: (i, 0))],
        out_specs=pl.BlockSpec((tm, LANES), lambda i: (i, 0)),
        out_shape=jax.ShapeDtypeStruct((N_TOK, LANES), F32),
        compiler_params=_cparams(("parallel",)),
        name="route",
    )(logits)


def _plan_body(rt_ref, pos_ref, tt_ref, rank_scr, cnt_scr):
    ph = pl.program_id(0)
    i = pl.program_id(1)
    t = PLAN_TM
    lane = lax.broadcasted_iota(jnp.int32, (t, LANES), 1)
    rt = rt_ref[...]
    oh0 = lane.astype(F32) == rt[:, 0:1]
    oh1 = lane.astype(F32) == rt[:, 1:2]
    rows = pl.ds(pl.multiple_of(i * t, t), t)

    @pl.when(ph == 0)
    def _():
        @pl.when(i == 0)
        def _():
            cnt_scr[...] = jnp.zeros_like(cnt_scr)

        s = jnp.where(oh0 | oh1, 1.0, 0.0)
        r = lax.broadcasted_iota(jnp.int32, (t, t), 0)
        c = lax.broadcasted_iota(jnp.int32, (t, t), 1)
        before = jnp.where(c < r, 1.0, 0.0).astype(BF16)
        csum = jnp.dot(before, s.astype(BF16), preferred_element_type=F32) + cnt_scr[...]
        rank0 = jnp.sum(jnp.where(oh0, csum, 0.0), axis=-1, keepdims=True)
        rank1 = jnp.sum(jnp.where(oh1, csum, 0.0), axis=-1, keepdims=True)
        rank_scr[rows, :] = jnp.where(lane == 0, rank0, jnp.where(lane == 1, rank1, 0.0))
        cnt_scr[...] += jnp.sum(s, axis=0, keepdims=True)

    @pl.when(ph == 1)
    def _():
        counts = cnt_scr[...]
        tiles = jnp.floor((counts + (MOE_TM - 1)) * (1.0 / MOE_TM))
        r = lax.broadcasted_iota(jnp.int32, (LANES, LANES), 0)
        c = lax.broadcasted_iota(jnp.int32, (LANES, LANES), 1)
        upto = jnp.where(r <= c, 1.0, 0.0).astype(BF16)
        tiles8 = jnp.broadcast_to(tiles, (8, LANES)).astype(BF16)
        tile_end = jnp.dot(tiles8, upto, preferred_element_type=F32)[0:1]
        offs = (tile_end - tiles) * MOE_TM
        rk = rank_scr[rows, :]
        p0 = jnp.sum(jnp.where(oh0, offs, 0.0), axis=-1, keepdims=True) + rk[:, 0:1]
        p1 = jnp.sum(jnp.where(oh1, offs, 0.0), axis=-1, keepdims=True) + rk[:, 1:2]
        pos_ref[...] = jnp.where(lane == 0, p0, jnp.where(lane == 1, p1, 0.0)).astype(jnp.int32)

        lane_e = lax.broadcasted_iota(jnp.int32, (LANES, LANES), 1)
        tile_id = lax.broadcasted_iota(jnp.int32, (LANES, LANES), 0).astype(F32)
        is_e = lane_e < N_EXPERTS
        total = jnp.max(tile_end, axis=-1, keepdims=True)
        t_exp = jnp.sum(jnp.where(is_e & (tile_end <= tile_id), 1.0, 0.0), axis=-1, keepdims=True)
        t_valid = jnp.where(tile_id[:, 0:1] < total, 1.0, 0.0)
        last_exp = jnp.sum(jnp.where(is_e & (tile_end <= total - 1.0), 1.0, 0.0), axis=-1, keepdims=True)
        t_exp = jnp.where(t_valid > 0, t_exp, last_exp)
        t_first = jnp.sum(jnp.where(is_e & (tiles > 0) & ((tile_end - tiles) == tile_id), 1.0, 0.0),
                          axis=-1, keepdims=True)
        t_last = jnp.sum(jnp.where(is_e & (tiles > 0) & ((tile_end - 1.0) == tile_id), 1.0, 0.0),
                         axis=-1, keepdims=True)
        t_clear = jnp.maximum(t_last, 1.0 - t_valid)
        tt_ref[...] = jnp.where(lane_e == 0, t_exp, jnp.where(lane_e == 1, t_valid,
                                jnp.where(lane_e == 2, t_first,
                                          jnp.where(lane_e == 3, t_clear, 0.0)))).astype(jnp.int32)


def _plan(route):
    t = PLAN_TM
    return pl.pallas_call(
        _plan_body,
        grid=(2, N_TOK // t),
        in_specs=[pl.BlockSpec((t, LANES), lambda ph, i: (i, 0))],
        out_specs=[pl.BlockSpec((t, LANES), lambda ph, i: (i * ph, 0)),
                   pl.BlockSpec((LANES, LANES), lambda ph, i: (0, 0))],
        out_shape=[jax.ShapeDtypeStruct((N_TOK, LANES), jnp.int32),
                   jax.ShapeDtypeStruct((LANES, LANES), jnp.int32)],
        scratch_shapes=[pltpu.VMEM((N_TOK, LANES), F32), pltpu.VMEM((1, LANES), F32)],
        compiler_params=_cparams(("arbitrary", "arbitrary")),
        name="plan",
    )(route)


def _rows_wait(ref, n_rows, sem):
    pltpu.make_async_copy(ref.at[pl.ds(0, n_rows)], ref.at[pl.ds(0, n_rows)], sem).wait()


def _dispatch_body(pos_ref, clear_ref, h_hbm, xs_hbm, zbuf, sem_z, sem_s):
    zbuf[...] = jnp.zeros_like(zbuf)

    def zero_copy(tile):
        start = pl.multiple_of(tile * MOE_TM, MOE_TM)
        return pltpu.make_async_copy(zbuf, xs_hbm.at[pl.ds(start, MOE_TM)], sem_z)

    def clear_start(tile, carry):
        @pl.when(clear_ref[tile] > 0)
        def _():
            zero_copy(tile).start()
        return carry

    def clear_wait(tile, carry):
        @pl.when(clear_ref[tile] > 0)
        def _():
            zero_copy(tile).wait()
        return carry

    lax.fori_loop(0, MOE_TILES, clear_start, 0)
    lax.fori_loop(0, MOE_TILES, clear_wait, 0)

    n_chunks = N_TOK // DISPATCH_CHUNK

    def chunk(ci, carry):
        def tok(j, carry):
            t = ci * DISPATCH_CHUNK + j
            src = h_hbm.at[pl.ds(t, 1)]
            pltpu.make_async_copy(src, xs_hbm.at[pl.ds(pos_ref[2 * t], 1)], sem_s).start()
            pltpu.make_async_copy(src, xs_hbm.at[pl.ds(pos_ref[2 * t + 1], 1)], sem_s).start()
            return carry

        lax.fori_loop(0, DISPATCH_CHUNK, tok, 0, unroll=8)

        @pl.when(ci > 0)
        def _():
            _rows_wait(xs_hbm, TOP_K * DISPATCH_CHUNK, sem_s)
        return carry

    lax.fori_loop(0, n_chunks, chunk, 0)
    _rows_wait(xs_hbm, TOP_K * DISPATCH_CHUNK, sem_s)


def _dispatch(pos_flat, tile_clear, h2p):
    grid_spec = pltpu.PrefetchScalarGridSpec(
        num_scalar_prefetch=2,
        grid=(1,),
        in_specs=[pl.BlockSpec(memory_space=pl.ANY)],
        out_specs=pl.BlockSpec(memory_space=pl.ANY),
        scratch_shapes=[pltpu.VMEM((MOE_TM, HALF_D), U32), pltpu.SemaphoreType.DMA(()), pltpu.SemaphoreType.DMA(())],
    )
    return pl.pallas_call(
        _dispatch_body,
        grid_spec=grid_spec,
        out_shape=jax.ShapeDtypeStruct((MOE_ROWS, HALF_D), U32),
        compiler_params=_cparams(("arbitrary",)),
        name="dispatch",
    )(pos_flat, tile_clear, h2p)


def _moe_body(te_ref, tv_ref, tf_ref, x_ref, wg_ref, wu_ref, wd_ref, y_ref, wg_s, wu_s, wd_s):
    i = pl.program_id(0)

    @pl.when(tf_ref[i] > 0)
    def _():
        wg_s[...] = wg_ref[0].astype(BF16)
        wu_s[...] = wu_ref[0].astype(BF16)
        wd_s[...] = wd_ref[0].astype(BF16)

    @pl.when(tv_ref[i] > 0)
    def _():
        lo, hi = _unpack_halves(x_ref[...])
        xa = lo.astype(BF16)
        xb = hi.astype(BF16)
        g = (jnp.dot(xa, wg_s[:HALF_D, :], preferred_element_type=F32)
             + jnp.dot(xb, wg_s[HALF_D:, :], preferred_element_type=F32))
        u = (jnp.dot(xa, wu_s[:HALF_D, :], preferred_element_type=F32)
             + jnp.dot(xb, wu_s[HALF_D:, :], preferred_element_type=F32))
        hid = (g * jax.nn.sigmoid(g) * u).astype(BF16)
        y_ref[...] = _pack_halves(jnp.dot(hid, wd_s[...], preferred_element_type=F32))

    @pl.when(tv_ref[i] == 0)
    def _():
        y_ref[...] = jnp.zeros_like(y_ref)


def _moe(tile_expert, tile_valid, tile_first, xs, wg, wu, wd):
    tm = MOE_TM
    grid_spec = pltpu.PrefetchScalarGridSpec(
        num_scalar_prefetch=3,
        grid=(MOE_TILES,),
        in_specs=[pl.BlockSpec((tm, HALF_D), lambda i, te, tv, tf: (i, 0)),
                  pl.BlockSpec((1, D_MODEL, EXPERT_FF), lambda i, te, tv, tf: (te[i], 0, 0)),
                  pl.BlockSpec((1, D_MODEL, EXPERT_FF), lambda i, te, tv, tf: (te[i], 0, 0)),
                  pl.BlockSpec((1, EXPERT_FF, D_MODEL), lambda i, te, tv, tf: (te[i], 0, 0))],
        out_specs=pl.BlockSpec((tm, HALF_D), lambda i, te, tv, tf: (i, 0)),
        scratch_shapes=[pltpu.VMEM((D_MODEL, EXPERT_FF), BF16), pltpu.VMEM((D_MODEL, EXPERT_FF), BF16),
                        pltpu.VMEM((EXPERT_FF, D_MODEL), BF16)],
    )
    return pl.pallas_call(
        _moe_body,
        grid_spec=grid_spec,
        out_shape=jax.ShapeDtypeStruct((MOE_ROWS, HALF_D), U32),
        compiler_params=_cparams(("arbitrary",)),
        name="moe",
    )(tile_expert, tile_valid, tile_first, xs, wg, wu, wd)


def _final_body(pos_ref, x1_ref, mod_ref, rt_ref, g2_ref, b2_ref, ys_hbm, o_ref, buf, sem):
    i = pl.program_id(0)
    n = pl.num_programs(0)
    tm = FINAL_TM

    def issue(tile, slot):
        def tok(j, carry):
            t = tile * tm + j
            pltpu.make_async_copy(ys_hbm.at[pl.ds(pos_ref[2 * t], 1)], buf.at[slot, pl.ds(j, 1)],
                                  sem.at[slot]).start()
            pltpu.make_async_copy(ys_hbm.at[pl.ds(pos_ref[2 * t + 1], 1)], buf.at[slot, pl.ds(tm + j, 1)],
                                  sem.at[slot]).start()
            return carry

        lax.fori_loop(0, tm, tok, 0, unroll=8)

    @pl.when(i == 0)
    def _():
        issue(0, 0)

    @pl.when(i + 1 < n)
    def _():
        issue(i + 1, (i + 1) % 2)

    slot = i % 2
    pltpu.make_async_copy(ys_hbm.at[pl.ds(0, TOP_K * tm)], buf.at[slot], sem.at[slot]).wait()
    a_lo, a_hi = _unpack_halves(buf[slot, 0:tm, :])
    b_lo, b_hi = _unpack_halves(buf[slot, tm:2 * tm, :])
    w0 = rt_ref[:, 2:3]
    w1 = rt_ref[:, 3:4]
    y = jnp.concatenate([w0 * a_lo + w1 * b_lo, w0 * a_hi + w1 * b_hi], axis=1)
    gate2 = mod_ref[0, 5:6, :]
    o_ref[...] = _ln_rows(DEEPNORM_ALPHA * x1_ref[...] + gate2 * y) * g2_ref[...] + b2_ref[...]


def _final(pos_flat, x1, mod3, route, g2, b2, ys):
    tm = FINAL_TM
    tpb = SEQ // tm
    row = lambda w: pl.BlockSpec((tm, w), lambda i, p: (i, 0))
    grid_spec = pltpu.PrefetchScalarGridSpec(
        num_scalar_prefetch=1,
        grid=(N_TOK // tm,),
        in_specs=[row(D_MODEL),
                  pl.BlockSpec((1, N_MOD, D_MODEL), lambda i, p: (i // tpb, 0, 0)),
                  row(LANES),
                  pl.BlockSpec(g2.shape, lambda i, p: (0, 0)),
                  pl.BlockSpec(b2.shape, lambda i, p: (0, 0)),
                  pl.BlockSpec(memory_space=pl.ANY)],
        out_specs=row(D_MODEL),
        scratch_shapes=[pltpu.VMEM((2, TOP_K * tm, HALF_D), U32), pltpu.SemaphoreType.DMA((2,))],
    )
    return pl.pallas_call(
        _final_body,
        grid_spec=grid_spec,
        out_shape=jax.ShapeDtypeStruct((N_TOK, D_MODEL), F32),
        compiler_params=_cparams(("arbitrary",)),
        name="final",
    )(pos_flat, x1, mod3, route, g2, b2, ys)


def kernel(x, c, positions, w_ada, b_ada, w_in, q_norm_g, w_uq, kv_norm_g, w_ukv, sgu_norm_g, sgu_norm_b,
           w_spatial, b_spatial, w_o, ln1_g, ln1_b, w_router_group, b_router_group, w_router_expert,
           b_router_expert, w_gate, w_up, w_down, ln2_g, ln2_b):
    l = 0
    x2 = x.reshape(N_TOK, D_MODEL)
    mod3 = _ada(c, w_ada[l], b_ada[l][None, :]).reshape(BATCH, N_MOD, D_MODEL)

    w_in_l = w_in[l]
    o1, o2, o3 = Q_LORA_RANK, Q_LORA_RANK + KV_LORA_RANK, Q_LORA_RANK + KV_LORA_RANK + QK_ROPE_DIM
    wq = w_in_l[:, :o1].astype(BF16)
    wkv = w_in_l[:, o1:o2].astype(BF16)
    wkpe = jnp.pad(w_in_l[:, o2:o3], ((0, 0), (0, LANES - QK_ROPE_DIM))).astype(BF16)
    wz = w_in_l[:, o3:].astype(BF16)
    wuq3 = w_uq[l].reshape(Q_LORA_RANK, MLA_HEADS, QK_DIM)
    wuq = jnp.concatenate([wuq3[:, :, :QK_NOPE_DIM].reshape(Q_LORA_RANK, -1),
                           wuq3[:, :, QK_NOPE_DIM:].reshape(Q_LORA_RANK, -1)], axis=1).astype(BF16)
    wukv3 = w_ukv[l].reshape(KV_LORA_RANK, MLA_HEADS, QK_NOPE_DIM + V_HEAD_DIM)
    wukv = jnp.concatenate([wukv3[:, :, :QK_NOPE_DIM].reshape(KV_LORA_RANK, -1),
                            wukv3[:, :, QK_NOPE_DIM:].reshape(KV_LORA_RANK, -1)], axis=1).astype(BF16)
    woa = w_o[l][:MLA_WIDTH].astype(BF16)
    wos = w_o[l][MLA_WIDTH:].astype(BF16)
    n_r = N_GROUPS + N_EXPERTS
    wr = jnp.pad(jnp.concatenate([w_router_group[l], w_router_expert[l]], axis=1),
                 ((0, 0), (0, LANES - n_r))).astype(BF16)
    br = jnp.pad(jnp.concatenate([b_router_group[l], b_router_expert[l]]), (0, LANES - n_r))[None, :]
    inv_freq = 1.0 / (ROPE_THETA ** (jnp.arange(0, QK_ROPE_DIM, 2, dtype=F32) / QK_ROPE_DIM))
    invf = jnp.tile(inv_freq, MLA_HEADS * 2)[None, :]

    cqn, ckvn, kpe, u, vs = _inproj(x2, mod3, wq, wkv, wkpe, wz, q_norm_g[l][None, :], kv_norm_g[l][None, :],
                                    sgu_norm_g[l][None, :], sgu_norm_b[l][None, :])
    q, k, v = _qkv(cqn, ckvn, kpe, positions.reshape(N_TOK, 1), invf, wuq, wukv)
    attn = _attn(q, k, v).reshape(N_TOK, MLA_WIDTH)
    x1, h2, logits = _mixout(x2, mod3, attn, u, vs, w_spatial[l], b_spatial[l].T, woa, wos,
                             ln1_g[l][None, :], ln1_b[l][None, :], wr, br)
    route = _route(logits)
    pos_tab, tile_tab = _plan(route)
    pos_flat = pos_tab[:, :TOP_K].reshape(-1)
    xs = _dispatch(pos_flat, tile_tab[:MOE_TILES, 3], h2)
    ys = _moe(tile_tab[:MOE_TILES, 0], tile_tab[:MOE_TILES, 1], tile_tab[:MOE_TILES, 2], xs,
              w_gate[l], w_up[l], w_down[l])
    out = _final(pos_flat, x1, mod3, route, ln2_g[l][None, :], ln2_b[l][None, :], ys)
    return out.reshape(BATCH, SEQ, D_MODEL)
```

```python
import functools

import jax
import jax.numpy as jnp
import numpy as np
from jax import lax
from jax.experimental import pallas as pl
from jax.experimental.pallas import tpu as pltpu

D_MODEL = 2048
BATCH = 4
SEQ = 2048
N_TOK = BATCH * SEQ

MLA_HEADS = 8
QK_NOPE_DIM = 128
QK_ROPE_DIM = 64
QK_DIM = QK_NOPE_DIM + QK_ROPE_DIM
V_HEAD_DIM = 128
Q_LORA_RANK = 768
KV_LORA_RANK = 512
ROPE_THETA = 10000.0
MLA_WIDTH = MLA_HEADS * V_HEAD_DIM

SGU_GROUPS = 8
SGU_GROUP_DIM = 128
SGU_CHUNK = 128
SGU_WIDTH = SGU_GROUPS * SGU_GROUP_DIM

N_GROUPS = 4
EXPERTS_PER_GROUP = 8
N_EXPERTS = N_GROUPS * EXPERTS_PER_GROUP
TOP_K = 2
EXPERT_FF = 512

DEEPNORM_ALPHA = 2.0 ** 0.25
EPS = 1e-6
N_MOD = 6
NEG_BIG = -1e30

LANES = 128
VMEM_LIMIT = 56 * 1024 * 1024

ADA_TN = 1024
TOK_TM = 256
ATT_TQ = 512
ATT_TK = 512
MOE_TM = 256
MOE_TILES = (N_TOK * TOP_K + N_EXPERTS * (MOE_TM - 1)) // MOE_TM + 1
MOE_ROWS = MOE_TILES * MOE_TM
PLAN_TM = 512
DISPATCH_TM = 1024
FINAL_TM = 256
assert MOE_TILES <= LANES

F32 = jnp.float32
BF16 = jnp.bfloat16
U32 = jnp.uint32
HALF_D = D_MODEL // 2


def _cparams(sem):
    return pltpu.CompilerParams(dimension_semantics=sem, vmem_limit_bytes=VMEM_LIMIT)


def _const_spec(shape):
    nd = len(shape)
    return pl.BlockSpec(shape, lambda *_: (0,) * nd, pipeline_mode=pl.Buffered(1))


def _ln_rows(x):
    mu = jnp.mean(x, axis=-1, keepdims=True)
    xc = x - mu
    var = jnp.mean(xc * xc, axis=-1, keepdims=True)
    return xc * lax.rsqrt(var + EPS)


def _rms_rows(x):
    return x * lax.rsqrt(jnp.mean(x * x, axis=-1, keepdims=True) + EPS)


def _pack_halves(x):
    half = x.shape[-1] // 2
    return pltpu.pack_elementwise([x[:, :half], x[:, half:]], packed_dtype=BF16)


def _unpack_halves(w):
    lo = pltpu.unpack_elementwise(w, index=0, packed_dtype=BF16, unpacked_dtype=F32)
    hi = pltpu.unpack_elementwise(w, index=1, packed_dtype=BF16, unpacked_dtype=F32)
    return lo, hi


def _gelu_tanh(x):
    c = np.sqrt(2.0 / np.pi).astype(np.float32)
    return 0.5 * x * (1.0 + jnp.tanh(c * (x + 0.044715 * (x * x * x))))


def _ada_body(c_ref, w_ref, b_ref, o_ref):
    o_ref[...] = jnp.dot(c_ref[...].astype(BF16), w_ref[...].astype(BF16),
                         preferred_element_type=F32) + b_ref[...]


def _ada(c, w, b):
    n = w.shape[1]
    return pl.pallas_call(
        _ada_body,
        grid=(n // ADA_TN,),
        in_specs=[pl.BlockSpec((BATCH, D_MODEL), lambda j: (0, 0)),
                  pl.BlockSpec((D_MODEL, ADA_TN), lambda j: (0, j)),
                  pl.BlockSpec((1, ADA_TN), lambda j: (0, j))],
        out_specs=pl.BlockSpec((BATCH, ADA_TN), lambda j: (0, j)),
        out_shape=jax.ShapeDtypeStruct((BATCH, n), F32),
        compiler_params=_cparams(("parallel",)),
        name="ada",
    )(c, w, b)


def _inproj_body(x_ref, mod_ref, wq_ref, wkv_ref, wkpe_ref, wz_ref, gq_ref, gkv_ref, sg_ref, sb_ref,
                 cq_ref, ckv_ref, kpe_ref, u_ref, vs_ref):
    sh = mod_ref[0, 0:1, :]
    sc = mod_ref[0, 1:2, :]
    h = (_ln_rows(x_ref[...]) * (1.0 + sc) + sh).astype(BF16)
    cq = jnp.dot(h, wq_ref[...], preferred_element_type=F32)
    cq_ref[...] = (_rms_rows(cq) * gq_ref[...]).astype(BF16)
    ckv = jnp.dot(h, wkv_ref[...], preferred_element_type=F32)
    ckv_ref[...] = (_rms_rows(ckv) * gkv_ref[...]).astype(BF16)
    kpe_ref[...] = jnp.dot(h, wkpe_ref[...], preferred_element_type=F32)
    gz = _gelu_tanh(jnp.dot(h, wz_ref[...], preferred_element_type=F32))
    u_ref[...] = gz[:, :SGU_WIDTH]
    vs_ref[...] = (_ln_rows(gz[:, SGU_WIDTH:]) * sg_ref[...] + sb_ref[...]).astype(BF16)


def _inproj(x2, mod3, wq, wkv, wkpe, wz, gq, gkv, sg, sb):
    tm = TOK_TM
    tiles_per_batch = SEQ // tm
    row = lambda w: pl.BlockSpec((tm, w), lambda i: (i, 0))
    return pl.pallas_call(
        _inproj_body,
        grid=(N_TOK // tm,),
        in_specs=[row(D_MODEL),
                  pl.BlockSpec((1, N_MOD, D_MODEL), lambda i: (i // tiles_per_batch, 0, 0)),
                  _const_spec(wq.shape), _const_spec(wkv.shape), _const_spec(wkpe.shape), _const_spec(wz.shape),
                  _const_spec(gq.shape), _const_spec(gkv.shape), _const_spec(sg.shape), _const_spec(sb.shape)],
        out_specs=[row(Q_LORA_RANK), row(KV_LORA_RANK), row(LANES), row(SGU_WIDTH), row(SGU_WIDTH)],
        out_shape=[jax.ShapeDtypeStruct((N_TOK, Q_LORA_RANK), BF16),
                   jax.ShapeDtypeStruct((N_TOK, KV_LORA_RANK), BF16),
                   jax.ShapeDtypeStruct((N_TOK, LANES), F32),
                   jax.ShapeDtypeStruct((N_TOK, SGU_WIDTH), F32),
                   jax.ShapeDtypeStruct((N_TOK, SGU_WIDTH), BF16)],
        compiler_params=_cparams(("parallel",)),
        name="inproj",
    )(x2, mod3, wq, wkv, wkpe, wz, gq, gkv, sg, sb)


def _rope(x, cos, sin):
    w = x.shape[-1]
    lane = lax.broadcasted_iota(jnp.int32, x.shape, 1)
    first_half = (lane % QK_ROPE_DIM) < (QK_ROPE_DIM // 2)
    rot = jnp.where(first_half,
                    -pltpu.roll(x, w - QK_ROPE_DIM // 2, 1),
                    pltpu.roll(x, QK_ROPE_DIM // 2, 1))
    return x * cos + rot * sin


def _qkv_body(cq_ref, ckv_ref, kpe_ref, pos_ref, invf_ref, wuq_ref, wukv_ref, q_ref, k_ref, v_ref):
    ang = pos_ref[...].astype(F32) * invf_ref[...]
    cos1 = jnp.cos(ang)
    sin1 = jnp.sin(ang)
    reps = MLA_HEADS * QK_ROPE_DIM // LANES
    cos = jnp.concatenate([cos1] * reps, axis=1)
    sin = jnp.concatenate([sin1] * reps, axis=1)
    scale = np.float32(QK_DIM ** -0.5)
    q = jnp.dot(cq_ref[...], wuq_ref[...], preferred_element_type=F32) * scale
    q_pe = _rope(q[:, MLA_HEADS * QK_NOPE_DIM:], cos, sin)
    kv = jnp.dot(ckv_ref[...], wukv_ref[...], preferred_element_type=F32)
    k_pe = _rope(kpe_ref[...], cos1, sin1)[:, :QK_ROPE_DIM].astype(BF16)
    for h in range(MLA_HEADS):
        q_ref[0, h, :, 0:QK_NOPE_DIM] = q[:, h * QK_NOPE_DIM:(h + 1) * QK_NOPE_DIM].astype(BF16)
        q_ref[0, h, :, QK_NOPE_DIM:QK_DIM] = q_pe[:, h * QK_ROPE_DIM:(h + 1) * QK_ROPE_DIM].astype(BF16)
        k_ref[0, h, :, 0:QK_NOPE_DIM] = kv[:, h * QK_NOPE_DIM:(h + 1) * QK_NOPE_DIM].astype(BF16)
        k_ref[0, h, :, QK_NOPE_DIM:QK_DIM] = k_pe
        v_ref[0, h, :, :] = kv[:, MLA_WIDTH + h * V_HEAD_DIM:MLA_WIDTH + (h + 1) * V_HEAD_DIM].astype(BF16)


def _qkv(cqn, ckvn, kpe, pos2, invf, wuq, wukv):
    tm = TOK_TM
    tpb = SEQ // tm
    row = lambda w: pl.BlockSpec((tm, w), lambda i: (i, 0))
    head_out = lambda w: pl.BlockSpec((1, MLA_HEADS, tm, w), lambda i: (i // tpb, 0, i % tpb, 0))
    return pl.pallas_call(
        _qkv_body,
        grid=(N_TOK // tm,),
        in_specs=[row(Q_LORA_RANK), row(KV_LORA_RANK), row(LANES), row(1),
                  _const_spec(invf.shape), _const_spec(wuq.shape), _const_spec(wukv.shape)],
        out_specs=[head_out(QK_DIM), head_out(QK_DIM), head_out(V_HEAD_DIM)],
        out_shape=[jax.ShapeDtypeStruct((BATCH, MLA_HEADS, SEQ, QK_DIM), BF16),
                   jax.ShapeDtypeStruct((BATCH, MLA_HEADS, SEQ, QK_DIM), BF16),
                   jax.ShapeDtypeStruct((BATCH, MLA_HEADS, SEQ, V_HEAD_DIM), BF16)],
        compiler_params=_cparams(("parallel",)),
        name="qkv",
    )(cqn, ckvn, kpe, pos2, invf, wuq, wukv)


def _attn_body(q_ref, k_ref, v_ref, o_ref):
    i = pl.program_id(2)
    q = q_ref[0, 0]

    def step(j, carry, masked):
        m, l, acc = carry
        start = pl.multiple_of(j * ATT_TK, ATT_TK)
        k = k_ref[0, 0, pl.ds(start, ATT_TK), :]
        v = v_ref[0, 0, pl.ds(start, ATT_TK), :]
        s = lax.dot_general(q, k, (((1,), (1,)), ((), ())), preferred_element_type=F32)
        if masked:
            r = lax.broadcasted_iota(jnp.int32, s.shape, 0)
            c = lax.broadcasted_iota(jnp.int32, s.shape, 1)
            s = jnp.where(c <= r, s, NEG_BIG)
        m_new = jnp.maximum(m, jnp.max(s, axis=-1, keepdims=True))
        p = jnp.exp(s - m_new)
        a = jnp.exp(m - m_new)
        l = a * l + jnp.sum(p, axis=-1, keepdims=True)
        acc = a * acc + jnp.dot(p.astype(BF16), v, preferred_element_type=F32)
        return m_new, l, acc

    init = (jnp.full((ATT_TQ, 1), NEG_BIG, F32), jnp.zeros((ATT_TQ, 1), F32),
            jnp.zeros((ATT_TQ, V_HEAD_DIM), F32))
    carry = lax.fori_loop(0, i, lambda j, c: step(j, c, False), init)
    m, l, acc = step(i, carry, True)
    o_ref[0] = (acc / l).astype(BF16)


def _attn(q, k, v):
    assert ATT_TQ == ATT_TK
    return pl.pallas_call(
        _attn_body,
        grid=(BATCH, MLA_HEADS, SEQ // ATT_TQ),
        in_specs=[pl.BlockSpec((1, 1, ATT_TQ, QK_DIM), lambda b, h, i: (b, h, i, 0)),
                  pl.BlockSpec((1, 1, SEQ, QK_DIM), lambda b, h, i: (b, h, 0, 0)),
                  pl.BlockSpec((1, 1, SEQ, V_HEAD_DIM), lambda b, h, i: (b, h, 0, 0))],
        out_specs=pl.BlockSpec((1, ATT_TQ, V_HEAD_DIM), lambda b, h, i: (b, i, h)),
        out_shape=jax.ShapeDtypeStruct((BATCH, SEQ, MLA_WIDTH), BF16),
        compiler_params=_cparams(("parallel", "parallel", "arbitrary")),
        name="attn",
    )(q, k, v)


def _mixout_body(x_ref, mod_ref, attn_ref, u_ref, vs_ref, wsp_ref, bsp_ref, woa_ref, wos_ref,
                 g1_ref, b1_ref, wr_ref, br_ref, x1_ref, h2_ref, lg_ref, sgu_scr):
    r = lax.broadcasted_iota(jnp.int32, (SGU_CHUNK, SGU_CHUNK), 0)
    c = lax.broadcasted_iota(jnp.int32, (SGU_CHUNK, SGU_CHUNK), 1)
    causal = c <= r
    for g in range(SGU_GROUPS):
        ws = jnp.where(causal, wsp_ref[g], 0.0).astype(BF16)
        bias = bsp_ref[:, g:g + 1]
        cols = slice(g * SGU_GROUP_DIM, (g + 1) * SGU_GROUP_DIM)
        for ch in range(TOK_TM // SGU_CHUNK):
            rows = slice(ch * SGU_CHUNK, (ch + 1) * SGU_CHUNK)
            mixed = jnp.dot(ws, vs_ref[rows, cols], preferred_element_type=F32) + bias
            sgu_scr[rows, cols] = (u_ref[rows, cols] * mixed).astype(BF16)
    y = (jnp.dot(attn_ref[...], woa_ref[...], preferred_element_type=F32)
         + jnp.dot(sgu_scr[...], wos_ref[...], preferred_element_type=F32))
    gate1 = mod_ref[0, 2:3, :]
    sh2 = mod_ref[0, 3:4, :]
    sc2 = mod_ref[0, 4:5, :]
    x1 = _ln_rows(DEEPNORM_ALPHA * x_ref[...] + gate1 * y) * g1_ref[...] + b1_ref[...]
    x1_ref[...] = x1
    h2 = _ln_rows(x1) * (1.0 + sc2) + sh2
    h2_ref[...] = _pack_halves(h2)
    lg_ref[...] = jnp.dot(h2.astype(BF16), wr_ref[...], preferred_element_type=F32) + br_ref[...]


def _mixout(x2, mod3, attn, u, vs, wsp, bsp_t, woa, wos, g1, b1, wr, br):
    tm = TOK_TM
    tpb = SEQ // tm
    row = lambda w: pl.BlockSpec((tm, w), lambda i: (i, 0))
    return pl.pallas_call(
        _mixout_body,
        grid=(N_TOK // tm,),
        in_specs=[row(D_MODEL),
                  pl.BlockSpec((1, N_MOD, D_MODEL), lambda i: (i // tpb, 0, 0)),
                  row(MLA_WIDTH), row(SGU_WIDTH), row(SGU_WIDTH),
                  _const_spec(wsp.shape), _const_spec(bsp_t.shape), _const_spec(woa.shape), _const_spec(wos.shape),
                  _const_spec(g1.shape), _const_spec(b1.shape), _const_spec(wr.shape), _const_spec(br.shape)],
        out_specs=[row(D_MODEL), row(HALF_D), row(LANES)],
        out_shape=[jax.ShapeDtypeStruct((N_TOK, D_MODEL), F32),
                   jax.ShapeDtypeStruct((N_TOK, HALF_D), U32),
                   jax.ShapeDtypeStruct((N_TOK, LANES), F32)],
        scratch_shapes=[pltpu.VMEM((tm, SGU_WIDTH), BF16)],
        compiler_params=_cparams(("parallel",)),
        name="mix_out",
    )(x2, mod3, attn, u, vs, wsp, bsp_t, woa, wos, g1, b1, wr, br)


def _route_body(lg_ref, o_ref):
    lg = lg_ref[...]
    lane = lax.broadcasted_iota(jnp.int32, lg.shape, 1)
    big = jnp.int32(LANES)

    def top1(vals):
        m = jnp.max(vals, axis=-1, keepdims=True)
        idx = jnp.min(jnp.where(vals == m, lane, big), axis=-1, keepdims=True)
        return m, idx

    is_group = lane < N_GROUPS
    glog = jnp.where(is_group, lg, -jnp.inf)
    gmax, gidx = top1(glog)
    pg_top = 1.0 / jnp.sum(jnp.exp(glog - gmax), axis=-1, keepdims=True)
    eid = lane - N_GROUPS
    sel = (eid >= gidx * EXPERTS_PER_GROUP) & (eid < (gidx + 1) * EXPERTS_PER_GROUP)
    elog = jnp.where(sel, lg, -jnp.inf)
    m1, i1 = top1(elog)
    m2, i2 = top1(jnp.where(lane == i1, -jnp.inf, elog))
    e2 = jnp.exp(m2 - m1)
    w1 = pg_top / (1.0 + e2)
    w2 = pg_top * e2 / (1.0 + e2)
    out = jnp.where(lane == 0, (i1 - N_GROUPS).astype(F32),
                    jnp.where(lane == 1, (i2 - N_GROUPS).astype(F32),
                              jnp.where(lane == 2, w1, jnp.where(lane == 3, w2, 0.0))))
    o_ref[...] = out


def _route(logits):
    tm = 1024
    return pl.pallas_call(
        _route_body,
        grid=(N_TOK // tm,),
        in_specs=[pl.BlockSpec((tm, LANES), lambda i: (i, 0))],
        out_specs=pl.BlockSpec((tm, LANES), lambda i: (i, 0)),
        out_shape=jax.ShapeDtypeStruct((N_TOK, LANES), F32),
        compiler_params=_cparams(("parallel",)),
        name="route",
    )(logits)


def _plan_body(rt_ref, pos_ref, tt_ref, rank_scr, cnt_scr):
    ph = pl.program_id(0)
    i = pl.program_id(1)
    t = PLAN_TM
    lane = lax.broadcasted_iota(jnp.int32, (t, LANES), 1)
    rt = rt_ref[...]
    oh0 = lane.astype(F32) == rt[:, 0:1]
    oh1 = lane.astype(F32) == rt[:, 1:2]
    rows = pl.ds(pl.multiple_of(i * t, t), t)

    @pl.when(ph == 0)
    def _():
        @pl.when(i == 0)
        def _():
            cnt_scr[...] = jnp.zeros_like(cnt_scr)

        s = jnp.where(oh0 | oh1, 1.0, 0.0)
        r = lax.broadcasted_iota(jnp.int32, (t, t), 0)
        c = lax.broadcasted_iota(jnp.int32, (t, t), 1)
        before = jnp.where(c < r, 1.0, 0.0).astype(BF16)
        csum = jnp.dot(before, s.astype(BF16), preferred_element_type=F32) + cnt_scr[...]
        rank0 = jnp.sum(jnp.where(oh0, csum, 0.0), axis=-1, keepdims=True)
        rank1 = jnp.sum(jnp.where(oh1, csum, 0.0), axis=-1, keepdims=True)
        rank_scr[rows, :] = jnp.where(lane == 0, rank0, jnp.where(lane == 1, rank1, 0.0))
        cnt_scr[...] += jnp.sum(s, axis=0, keepdims=True)

    @pl.when(ph == 1)
    def _():
        counts = cnt_scr[...]
        tiles = jnp.floor((counts + (MOE_TM - 1)) * (1.0 / MOE_TM))
        r = lax.broadcasted_iota(jnp.int32, (LANES, LANES), 0)
        c = lax.broadcasted_iota(jnp.int32, (LANES, LANES), 1)
        upto = jnp.where(r <= c, 1.0, 0.0).astype(BF16)
        tiles8 = jnp.broadcast_to(tiles, (8, LANES)).astype(BF16)
        tile_end = jnp.dot(tiles8, upto, preferred_element_type=F32)[0:1]
        offs = (tile_end - tiles) * MOE_TM
        rk = rank_scr[rows, :]
        p0 = jnp.sum(jnp.where(oh0, offs, 0.0), axis=-1, keepdims=True) + rk[:, 0:1]
        p1 = jnp.sum(jnp.where(oh1, offs, 0.0), axis=-1, keepdims=True) + rk[:, 1:2]
        pos_ref[...] = jnp.where(lane == 0, p0, jnp.where(lane == 1, p1, 0.0)).astype(jnp.int32)

        lane_e = lax.broadcasted_iota(jnp.int32, (LANES, LANES), 1)
        tile_id = lax.broadcasted_iota(jnp.int32, (LANES, LANES), 0).astype(F32)
        is_e = lane_e < N_EXPERTS
        total = jnp.max(tile_end, axis=-1, keepdims=True)
        t_exp = jnp.sum(jnp.where(is_e & (tile_end <= tile_id), 1.0, 0.0), axis=-1, keepdims=True)
        t_valid = jnp.where(tile_id[:, 0:1] < total, 1.0, 0.0)
        last_exp = jnp.sum(jnp.where(is_e & (tile_end <= total - 1.0), 1.0, 0.0), axis=-1, keepdims=True)
        t_exp = jnp.where(t_valid > 0, t_exp, last_exp)
        t_first = jnp.sum(jnp.where(is_e & (tiles > 0) & ((tile_end - tiles) == tile_id), 1.0, 0.0),
                          axis=-1, keepdims=True)
        t_last = jnp.sum(jnp.where(is_e & (tiles > 0) & ((tile_end - 1.0) == tile_id), 1.0, 0.0),
                         axis=-1, keepdims=True)
        t_clear = jnp.maximum(t_last, 1.0 - t_valid)
        tt_ref[...] = jnp.where(lane_e == 0, t_exp, jnp.where(lane_e == 1, t_valid,
                                jnp.where(lane_e == 2, t_first,
                                          jnp.where(lane_e == 3, t_clear, 0.0)))).astype(jnp.int32)


def _plan(route):
    t = PLAN_TM
    return pl.pallas_call(
        _plan_body,
        grid=(2, N_TOK // t),
        in_specs=[pl.BlockSpec((t, LANES), lambda ph, i: (i, 0))],
        out_specs=[pl.BlockSpec((t, LANES), lambda ph, i: (i * ph, 0)),
                   pl.BlockSpec((LANES, LANES), lambda ph, i: (0, 0))],
        out_shape=[jax.ShapeDtypeStruct((N_TOK, LANES), jnp.int32),
                   jax.ShapeDtypeStruct((LANES, LANES), jnp.int32)],
        scratch_shapes=[pltpu.VMEM((N_TOK, LANES), F32), pltpu.VMEM((1, LANES), F32)],
        compiler_params=_cparams(("arbitrary", "arbitrary")),
        name="plan",
    )(route)


def _rows_wait(ref, n_rows, sem):
    pltpu.make_async_copy(ref.at[pl.ds(0, n_rows)], ref.at[pl.ds(0, n_rows)], sem).wait()


def _dispatch_body(pos_ref, clear_ref, h_ref, xs_hbm, zbuf, sem_z, sem_s):
    i = pl.program_id(0)

    @pl.when(i == 0)
    def _():
        zbuf[...] = _pack_halves(jnp.zeros((MOE_TM, D_MODEL), F32))

        def zero_copy(tile):
            start = pl.multiple_of(tile * MOE_TM, MOE_TM)
            return pltpu.make_async_copy(zbuf, xs_hbm.at[pl.ds(start, MOE_TM)], sem_z)

        def clear_start(tile, carry):
            @pl.when(clear_ref[tile] > 0)
            def _():
                zero_copy(tile).start()
            return carry

        def clear_wait(tile, carry):
            @pl.when(clear_ref[tile] > 0)
            def _():
                zero_copy(tile).wait()
            return carry

        lax.fori_loop(0, MOE_TILES, clear_start, 0)
        lax.fori_loop(0, MOE_TILES, clear_wait, 0)

    def tok(j, carry):
        t = i * DISPATCH_TM + j
        src = h_ref.at[pl.ds(j, 1)]
        pltpu.make_async_copy(src, xs_hbm.at[pl.ds(pos_ref[2 * t], 1)], sem_s).start()
        pltpu.make_async_copy(src, xs_hbm.at[pl.ds(pos_ref[2 * t + 1], 1)], sem_s).start()
        return carry

    lax.fori_loop(0, DISPATCH_TM, tok, 0, unroll=8)
    _rows_wait(xs_hbm, TOP_K * DISPATCH_TM, sem_s)


def _dispatch(pos_flat, tile_clear, h2p):
    grid_spec = pltpu.PrefetchScalarGridSpec(
        num_scalar_prefetch=2,
        grid=(N_TOK // DISPATCH_TM,),
        in_specs=[pl.BlockSpec((DISPATCH_TM, HALF_D), lambda i, p, c: (i, 0))],
        out_specs=pl.BlockSpec(memory_space=pl.ANY),
        scratch_shapes=[pltpu.VMEM((MOE_TM, HALF_D), U32), pltpu.SemaphoreType.DMA(()), pltpu.SemaphoreType.DMA(())],
    )
    return pl.pallas_call(
        _dispatch_body,
        grid_spec=grid_spec,
        out_shape=jax.ShapeDtypeStruct((MOE_ROWS, HALF_D), U32),
        compiler_params=_cparams(("arbitrary",)),
        name="dispatch",
    )(pos_flat, tile_clear, h2p)


def _moe_body(te_ref, tv_ref, tf_ref, x_ref, wg_ref, wu_ref, wd_ref, y_ref, wg_s, wu_s, wd_s):
    i = pl.program_id(0)

    @pl.when(tf_ref[i] > 0)
    def _():
        wg_s[...] = wg_ref[0].astype(BF16)
        wu_s[...] = wu_ref[0].astype(BF16)
        wd_s[...] = wd_ref[0].astype(BF16)

    @pl.when(tv_ref[i] > 0)
    def _():
        lo, hi = _unpack_halves(x_ref[...])
        xa = lo.astype(BF16)
        xb = hi.astype(BF16)
        g = (jnp.dot(xa, wg_s[:HALF_D, :], preferred_element_type=F32)
             + jnp.dot(xb, wg_s[HALF_D:, :], preferred_element_type=F32))
        u = (jnp.dot(xa, wu_s[:HALF_D, :], preferred_element_type=F32)
             + jnp.dot(xb, wu_s[HALF_D:, :], preferred_element_type=F32))
        hid = (g * jax.nn.sigmoid(g) * u).astype(BF16)
        y_ref[...] = _pack_halves(jnp.dot(hid, wd_s[...], preferred_element_type=F32))

    @pl.when(tv_ref[i] == 0)
    def _():
        y_ref[...] = _pack_halves(jnp.zeros((MOE_TM, D_MODEL), F32))


def _moe(tile_expert, tile_valid, tile_first, xs, wg, wu, wd):
    tm = MOE_TM
    grid_spec = pltpu.PrefetchScalarGridSpec(
        num_scalar_prefetch=3,
        grid=(MOE_TILES,),
        in_specs=[pl.BlockSpec((tm, HALF_D), lambda i, te, tv, tf: (i, 0)),
                  pl.BlockSpec((1, D_MODEL, EXPERT_FF), lambda i, te, tv, tf: (te[i], 0, 0)),
                  pl.BlockSpec((1, D_MODEL, EXPERT_FF), lambda i, te, tv, tf: (te[i], 0, 0)),
                  pl.BlockSpec((1, EXPERT_FF, D_MODEL), lambda i, te, tv, tf: (te[i], 0, 0))],
        out_specs=pl.BlockSpec((tm, HALF_D), lambda i, te, tv, tf: (i, 0)),
        scratch_shapes=[pltpu.VMEM((D_MODEL, EXPERT_FF), BF16), pltpu.VMEM((D_MODEL, EXPERT_FF), BF16),
                        pltpu.VMEM((EXPERT_FF, D_MODEL), BF16)],
    )
    return pl.pallas_call(
        _moe_body,
        grid_spec=grid_spec,
        out_shape=jax.ShapeDtypeStruct((MOE_ROWS, HALF_D), U32),
        compiler_params=_cparams(("arbitrary",)),
        name="moe",
    )(tile_expert, tile_valid, tile_first, xs, wg, wu, wd)


def _final_body(pos_ref, x1_ref, mod_ref, rt_ref, g2_ref, b2_ref, ys_hbm, o_ref, buf, sem):
    i = pl.program_id(0)
    n = pl.num_programs(0)
    tm = FINAL_TM

    def issue(tile, slot):
        def tok(j, carry):
            t = tile * tm + j
            pltpu.make_async_copy(ys_hbm.at[pl.ds(pos_ref[2 * t], 1)], buf.at[slot, pl.ds(j, 1)],
                                  sem.at[slot]).start()
            pltpu.make_async_copy(ys_hbm.at[pl.ds(pos_ref[2 * t + 1], 1)], buf.at[slot, pl.ds(tm + j, 1)],
                                  sem.at[slot]).start()
            return carry

        lax.fori_loop(0, tm, tok, 0, unroll=8)

    @pl.when(i == 0)
    def _():
        issue(0, 0)

    @pl.when(i + 1 < n)
    def _():
        issue(i + 1, (i + 1) % 2)

    slot = i % 2
    pltpu.make_async_copy(ys_hbm.at[pl.ds(0, TOP_K * tm)], buf.at[slot], sem.at[slot]).wait()
    a_lo, a_hi = _unpack_halves(buf[slot, 0:tm, :])
    b_lo, b_hi = _unpack_halves(buf[slot, tm:2 * tm, :])
    w0 = rt_ref[:, 2:3]
    w1 = rt_ref[:, 3:4]
    y = jnp.concatenate([w0 * a_lo + w1 * b_lo, w0 * a_hi + w1 * b_hi], axis=1)
    gate2 = mod_ref[0, 5:6, :]
    o_ref[...] = _ln_rows(DEEPNORM_ALPHA * x1_ref[...] + gate2 * y) * g2_ref[...] + b2_ref[...]


def _final(pos_flat, x1, mod3, route, g2, b2, ys):
    tm = FINAL_TM
    tpb = SEQ // tm
    row = lambda w: pl.BlockSpec((tm, w), lambda i, p: (i, 0))
    grid_spec = pltpu.PrefetchScalarGridSpec(
        num_scalar_prefetch=1,
        grid=(N_TOK // tm,),
        in_specs=[row(D_MODEL),
                  pl.BlockSpec((1, N_MOD, D_MODEL), lambda i, p: (i // tpb, 0, 0)),
                  row(LANES),
                  pl.BlockSpec(g2.shape, lambda i, p: (0, 0)),
                  pl.BlockSpec(b2.shape, lambda i, p: (0, 0)),
                  pl.BlockSpec(memory_space=pl.ANY)],
        out_specs=row(D_MODEL),
        scratch_shapes=[pltpu.VMEM((2, TOP_K * tm, HALF_D), U32), pltpu.SemaphoreType.DMA((2,))],
    )
    return pl.pallas_call(
        _final_body,
        grid_spec=grid_spec,
        out_shape=jax.ShapeDtypeStruct((N_TOK, D_MODEL), F32),
        compiler_params=_cparams(("arbitrary",)),
        name="final",
    )(pos_flat, x1, mod3, route, g2, b2, ys)


def kernel(x, c, positions, w_ada, b_ada, w_in, q_norm_g, w_uq, kv_norm_g, w_ukv, sgu_norm_g, sgu_norm_b,
           w_spatial, b_spatial, w_o, ln1_g, ln1_b, w_router_group, b_router_group, w_router_expert,
           b_router_expert, w_gate, w_up, w_down, ln2_g, ln2_b):
    l = 0
    x2 = x.reshape(N_TOK, D_MODEL)
    mod3 = _ada(c, w_ada[l], b_ada[l][None, :]).reshape(BATCH, N_MOD, D_MODEL)

    w_in_l = w_in[l]
    o1, o2, o3 = Q_LORA_RANK, Q_LORA_RANK + KV_LORA_RANK, Q_LORA_RANK + KV_LORA_RANK + QK_ROPE_DIM
    wq = w_in_l[:, :o1].astype(BF16)
    wkv = w_in_l[:, o1:o2].astype(BF16)
    wkpe = jnp.pad(w_in_l[:, o2:o3], ((0, 0), (0, LANES - QK_ROPE_DIM))).astype(BF16)
    wz = w_in_l[:, o3:].astype(BF16)
    wuq3 = w_uq[l].reshape(Q_LORA_RANK, MLA_HEADS, QK_DIM)
    wuq = jnp.concatenate([wuq3[:, :, :QK_NOPE_DIM].reshape(Q_LORA_RANK, -1),
                           wuq3[:, :, QK_NOPE_DIM:].reshape(Q_LORA_RANK, -1)], axis=1).astype(BF16)
    wukv3 = w_ukv[l].reshape(KV_LORA_RANK, MLA_HEADS, QK_NOPE_DIM + V_HEAD_DIM)
    wukv = jnp.concatenate([wukv3[:, :, :QK_NOPE_DIM].reshape(KV_LORA_RANK, -1),
                            wukv3[:, :, QK_NOPE_DIM:].reshape(KV_LORA_RANK, -1)], axis=1).astype(BF16)
    woa = w_o[l][:MLA_WIDTH].astype(BF16)
    wos = w_o[l][MLA_WIDTH:].astype(BF16)
    n_r = N_GROUPS + N_EXPERTS
    wr = jnp.pad(jnp.concatenate([w_router_group[l], w_router_expert[l]], axis=1),
                 ((0, 0), (0, LANES - n_r))).astype(BF16)
    br = jnp.pad(jnp.concatenate([b_router_group[l], b_router_expert[l]]), (0, LANES - n_r))[None, :]
    inv_freq = 1.0 / (ROPE_THETA ** (jnp.arange(0, QK_ROPE_DIM, 2, dtype=F32) / QK_ROPE_DIM))
    invf = jnp.tile(inv_freq, 2 * LANES // QK_ROPE_DIM)[None, :]

    cqn, ckvn, kpe, u, vs = _inproj(x2, mod3, wq, wkv, wkpe, wz, q_norm_g[l][None, :], kv_norm_g[l][None, :],
                                    sgu_norm_g[l][None, :], sgu_norm_b[l][None, :])
    q, k, v = _qkv(cqn, ckvn, kpe, positions.reshape(N_TOK, 1), invf, wuq, wukv)
    attn = _attn(q, k, v).reshape(N_TOK, MLA_WIDTH)
    x1, h2, logits = _mixout(x2, mod3, attn, u, vs, w_spatial[l], b_spatial[l].T, woa, wos,
                             ln1_g[l][None, :], ln1_b[l][None, :], wr, br)
    route = _route(logits)
    pos_tab, tile_tab = _plan(route)
    pos_flat = pos_tab[:, :TOP_K].reshape(-1)
    xs = _dispatch(pos_flat, tile_tab[:MOE_TILES, 3], h2)
    ys = _moe(tile_tab[:MOE_TILES, 0], tile_tab[:MOE_TILES, 1], tile_tab[:MOE_TILES, 2], xs,
              w_gate[l], w_up[l], w_down[l])
    out = _final(pos_flat, x1, mod3, route, ln2_g[l][None, :], ln2_b[l][None, :], ys)
    return out.reshape(BATCH, SEQ, D_MODEL)
```

```python
import functools

import jax
import jax.numpy as jnp
import numpy as np
from jax import lax
from jax.experimental import pallas as pl
from jax.experimental.pallas import tpu as pltpu

D_MODEL = 2048
BATCH = 4
SEQ = 2048
N_TOK = BATCH * SEQ

MLA_HEADS = 8
QK_NOPE_DIM = 128
QK_ROPE_DIM = 64
QK_DIM = QK_NOPE_DIM + QK_ROPE_DIM
V_HEAD_DIM = 128
Q_LORA_RANK = 768
KV_LORA_RANK = 512
ROPE_THETA = 10000.0
MLA_WIDTH = MLA_HEADS * V_HEAD_DIM

SGU_GROUPS = 8
SGU_GROUP_DIM = 128
SGU_CHUNK = 128
SGU_WIDTH = SGU_GROUPS * SGU_GROUP_DIM

N_GROUPS = 4
EXPERTS_PER_GROUP = 8
N_EXPERTS = N_GROUPS * EXPERTS_PER_GROUP
TOP_K = 2
EXPERT_FF = 512

DEEPNORM_ALPHA = 2.0 ** 0.25
EPS = 1e-6
N_MOD = 6
NEG_BIG = -1e30

LANES = 128
VMEM_LIMIT = 56 * 1024 * 1024

ADA_TN = 1024
TOK_TM = 256
ATT_TQ = 512
ATT_TK = 512
MOE_TM = 256
MOE_TILES = (N_TOK * TOP_K + N_EXPERTS * (MOE_TM - 1)) // MOE_TM + 1
MOE_ROWS = MOE_TILES * MOE_TM
PLAN_TM = 512
DISPATCH_TM = 1024
FINAL_TM = 256
assert MOE_TILES <= LANES

F32 = jnp.float32
BF16 = jnp.bfloat16
U32 = jnp.uint32
HALF_D = D_MODEL // 2


def _cparams(sem):
    return pltpu.CompilerParams(dimension_semantics=sem, vmem_limit_bytes=VMEM_LIMIT)


def _const_spec(shape):
    nd = len(shape)
    return pl.BlockSpec(shape, lambda *_: (0,) * nd, pipeline_mode=pl.Buffered(1))


def _ln_rows(x):
    mu = jnp.mean(x, axis=-1, keepdims=True)
    xc = x - mu
    var = jnp.mean(xc * xc, axis=-1, keepdims=True)
    return xc * lax.rsqrt(var + EPS)


def _rms_rows(x):
    return x * lax.rsqrt(jnp.mean(x * x, axis=-1, keepdims=True) + EPS)


def _pack_halves(x):
    half = x.shape[-1] // 2
    return pltpu.pack_elementwise([x[:, :half], x[:, half:]], packed_dtype=BF16)


def _unpack_halves(w):
    lo = pltpu.unpack_elementwise(w, index=0, packed_dtype=BF16, unpacked_dtype=F32)
    hi = pltpu.unpack_elementwise(w, index=1, packed_dtype=BF16, unpacked_dtype=F32)
    return lo, hi


def _gelu_tanh(x):
    c = np.sqrt(2.0 / np.pi).astype(np.float32)
    return 0.5 * x * (1.0 + jnp.tanh(c * (x + 0.044715 * (x * x * x))))


def _ada_body(c_ref, w_ref, b_ref, o_ref):
    o_ref[...] = jnp.dot(c_ref[...].astype(BF16), w_ref[...].astype(BF16),
                         preferred_element_type=F32) + b_ref[...]


def _ada(c, w, b):
    n = w.shape[1]
    return pl.pallas_call(
        _ada_body,
        grid=(n // ADA_TN,),
        in_specs=[pl.BlockSpec((BATCH, D_MODEL), lambda j: (0, 0)),
                  pl.BlockSpec((D_MODEL, ADA_TN), lambda j: (0, j)),
                  pl.BlockSpec((1, ADA_TN), lambda j: (0, j))],
        out_specs=pl.BlockSpec((BATCH, ADA_TN), lambda j: (0, j)),
        out_shape=jax.ShapeDtypeStruct((BATCH, n), F32),
        compiler_params=_cparams(("parallel",)),
        name="ada",
    )(c, w, b)


def _inproj_body(x_ref, mod_ref, wq_ref, wkv_ref, wkpe_ref, wz_ref, gq_ref, gkv_ref, sg_ref, sb_ref,
                 cq_ref, ckv_ref, kpe_ref, u_ref, vs_ref):
    sh = mod_ref[0, 0:1, :]
    sc = mod_ref[0, 1:2, :]
    h = (_ln_rows(x_ref[...]) * (1.0 + sc) + sh).astype(BF16)
    cq = jnp.dot(h, wq_ref[...], preferred_element_type=F32)
    cq_ref[...] = (_rms_rows(cq) * gq_ref[...]).astype(BF16)
    ckv = jnp.dot(h, wkv_ref[...], preferred_element_type=F32)
    ckv_ref[...] = (_rms_rows(ckv) * gkv_ref[...]).astype(BF16)
    kpe_ref[...] = jnp.dot(h, wkpe_ref[...], preferred_element_type=F32)
    gz = _gelu_tanh(jnp.dot(h, wz_ref[...], preferred_element_type=F32))
    u_ref[...] = gz[:, :SGU_WIDTH]
    vs_ref[...] = (_ln_rows(gz[:, SGU_WIDTH:]) * sg_ref[...] + sb_ref[...]).astype(BF16)


def _inproj(x2, mod3, wq, wkv, wkpe, wz, gq, gkv, sg, sb):
    tm = TOK_TM
    tiles_per_batch = SEQ // tm
    row = lambda w: pl.BlockSpec((tm, w), lambda i: (i, 0))
    return pl.pallas_call(
        _inproj_body,
        grid=(N_TOK // tm,),
        in_specs=[row(D_MODEL),
                  pl.BlockSpec((1, N_MOD, D_MODEL), lambda i: (i // tiles_per_batch, 0, 0)),
                  _const_spec(wq.shape), _const_spec(wkv.shape), _const_spec(wkpe.shape), _const_spec(wz.shape),
                  _const_spec(gq.shape), _const_spec(gkv.shape), _const_spec(sg.shape), _const_spec(sb.shape)],
        out_specs=[row(Q_LORA_RANK), row(KV_LORA_RANK), row(LANES), row(SGU_WIDTH), row(SGU_WIDTH)],
        out_shape=[jax.ShapeDtypeStruct((N_TOK, Q_LORA_RANK), BF16),
                   jax.ShapeDtypeStruct((N_TOK, KV_LORA_RANK), BF16),
                   jax.ShapeDtypeStruct((N_TOK, LANES), F32),
                   jax.ShapeDtypeStruct((N_TOK, SGU_WIDTH), F32),
                   jax.ShapeDtypeStruct((N_TOK, SGU_WIDTH), BF16)],
        compiler_params=_cparams(("parallel",)),
        name="inproj",
    )(x2, mod3, wq, wkv, wkpe, wz, gq, gkv, sg, sb)


def _rope(x, cos, sin):
    w = x.shape[-1]
    lane = lax.broadcasted_iota(jnp.int32, x.shape, 1)
    first_half = (lane % QK_ROPE_DIM) < (QK_ROPE_DIM // 2)
    rot = jnp.where(first_half,
                    -pltpu.roll(x, w - QK_ROPE_DIM // 2, 1),
                    pltpu.roll(x, QK_ROPE_DIM // 2, 1))
    return x * cos + rot * sin


def _qkv_body(cq_ref, ckv_ref, kpe_ref, pos_ref, invf_ref, wuq_ref, wukv_ref, q_ref, k_ref, v_ref):
    ang = pos_ref[...].astype(F32) * invf_ref[...]
    cos1 = jnp.cos(ang)
    sin1 = jnp.sin(ang)
    reps = MLA_HEADS * QK_ROPE_DIM // LANES
    cos = jnp.concatenate([cos1] * reps, axis=1)
    sin = jnp.concatenate([sin1] * reps, axis=1)
    scale = np.float32(QK_DIM ** -0.5)
    q = jnp.dot(cq_ref[...], wuq_ref[...], preferred_element_type=F32) * scale
    q_pe = _rope(q[:, MLA_HEADS * QK_NOPE_DIM:], cos, sin)
    kv = jnp.dot(ckv_ref[...], wukv_ref[...], preferred_element_type=F32)
    k_pe = _rope(kpe_ref[...], cos1, sin1)[:, :QK_ROPE_DIM].astype(BF16)
    for h in range(MLA_HEADS):
        q_ref[0, h, :, 0:QK_NOPE_DIM] = q[:, h * QK_NOPE_DIM:(h + 1) * QK_NOPE_DIM].astype(BF16)
        q_ref[0, h, :, QK_NOPE_DIM:QK_DIM] = q_pe[:, h * QK_ROPE_DIM:(h + 1) * QK_ROPE_DIM].astype(BF16)
        k_ref[0, h, :, 0:QK_NOPE_DIM] = kv[:, h * QK_NOPE_DIM:(h + 1) * QK_NOPE_DIM].astype(BF16)
        k_ref[0, h, :, QK_NOPE_DIM:QK_DIM] = k_pe
        v_ref[0, h, :, :] = kv[:, MLA_WIDTH + h * V_HEAD_DIM:MLA_WIDTH + (h + 1) * V_HEAD_DIM].astype(BF16)


def _qkv(cqn, ckvn, kpe, pos2, invf, wuq, wukv):
    tm = TOK_TM
    tpb = SEQ // tm
    row = lambda w: pl.BlockSpec((tm, w), lambda i: (i, 0))
    head_out = lambda w: pl.BlockSpec((1, MLA_HEADS, tm, w), lambda i: (i // tpb, 0, i % tpb, 0))
    return pl.pallas_call(
        _qkv_body,
        grid=(N_TOK // tm,),
        in_specs=[row(Q_LORA_RANK), row(KV_LORA_RANK), row(LANES), row(1),
                  _const_spec(invf.shape), _const_spec(wuq.shape), _const_spec(wukv.shape)],
        out_specs=[head_out(QK_DIM), head_out(QK_DIM), head_out(V_HEAD_DIM)],
        out_shape=[jax.ShapeDtypeStruct((BATCH, MLA_HEADS, SEQ, QK_DIM), BF16),
                   jax.ShapeDtypeStruct((BATCH, MLA_HEADS, SEQ, QK_DIM), BF16),
                   jax.ShapeDtypeStruct((BATCH, MLA_HEADS, SEQ, V_HEAD_DIM), BF16)],
        compiler_params=_cparams(("parallel",)),
        name="qkv",
    )(cqn, ckvn, kpe, pos2, invf, wuq, wukv)


def _attn_body(q_ref, k_ref, v_ref, o_ref):
    i = pl.program_id(2)
    q = q_ref[0, 0]

    def step(j, carry, masked):
        m, l, acc = carry
        start = pl.multiple_of(j * ATT_TK, ATT_TK)
        k = k_ref[0, 0, pl.ds(start, ATT_TK), :]
        v = v_ref[0, 0, pl.ds(start, ATT_TK), :]
        s = lax.dot_general(q, k, (((1,), (1,)), ((), ())), preferred_element_type=F32)
        if masked:
            r = lax.broadcasted_iota(jnp.int32, s.shape, 0)
            c = lax.broadcasted_iota(jnp.int32, s.shape, 1)
            s = jnp.where(c <= r, s, NEG_BIG)
        m_new = jnp.maximum(m, jnp.max(s, axis=-1, keepdims=True))
        p = jnp.exp(s - m_new)
        a = jnp.exp(m - m_new)
        l = a * l + jnp.sum(p, axis=-1, keepdims=True)
        acc = a * acc + jnp.dot(p.astype(BF16), v, preferred_element_type=F32)
        return m_new, l, acc

    init = (jnp.full((ATT_TQ, 1), NEG_BIG, F32), jnp.zeros((ATT_TQ, 1), F32),
            jnp.zeros((ATT_TQ, V_HEAD_DIM), F32))
    carry = lax.fori_loop(0, i, lambda j, c: step(j, c, False), init)
    m, l, acc = step(i, carry, True)
    o_ref[0] = (acc / l).astype(BF16)


def _attn(q, k, v):
    assert ATT_TQ == ATT_TK
    return pl.pallas_call(
        _attn_body,
        grid=(BATCH, MLA_HEADS, SEQ // ATT_TQ),
        in_specs=[pl.BlockSpec((1, 1, ATT_TQ, QK_DIM), lambda b, h, i: (b, h, i, 0)),
                  pl.BlockSpec((1, 1, SEQ, QK_DIM), lambda b, h, i: (b, h, 0, 0)),
                  pl.BlockSpec((1, 1, SEQ, V_HEAD_DIM), lambda b, h, i: (b, h, 0, 0))],
        out_specs=pl.BlockSpec((1, ATT_TQ, V_HEAD_DIM), lambda b, h, i: (b, i, h)),
        out_shape=jax.ShapeDtypeStruct((BATCH, SEQ, MLA_WIDTH), BF16),
        compiler_params=_cparams(("parallel", "parallel", "arbitrary")),
        name="attn",
    )(q, k, v)


def _mixout_body(x_ref, mod_ref, attn_ref, u_ref, vs_ref, wsp_ref, bsp_ref, woa_ref, wos_ref,
                 g1_ref, b1_ref, wr_ref, br_ref, x1_ref, h2_ref, lg_ref, sgu_scr):
    r = lax.broadcasted_iota(jnp.int32, (SGU_CHUNK, SGU_CHUNK), 0)
    c = lax.broadcasted_iota(jnp.int32, (SGU_CHUNK, SGU_CHUNK), 1)
    causal = c <= r
    for g in range(SGU_GROUPS):
        ws = jnp.where(causal, wsp_ref[g], 0.0).astype(BF16)
        bias = bsp_ref[:, g:g + 1]
        cols = slice(g * SGU_GROUP_DIM, (g + 1) * SGU_GROUP_DIM)
        for ch in range(TOK_TM // SGU_CHUNK):
            rows = slice(ch * SGU_CHUNK, (ch + 1) * SGU_CHUNK)
            mixed = jnp.dot(ws, vs_ref[rows, cols], preferred_element_type=F32) + bias
            sgu_scr[rows, cols] = (u_ref[rows, cols] * mixed).astype(BF16)
    y = (jnp.dot(attn_ref[...], woa_ref[...], preferred_element_type=F32)
         + jnp.dot(sgu_scr[...], wos_ref[...], preferred_element_type=F32))
    gate1 = mod_ref[0, 2:3, :]
    sh2 = mod_ref[0, 3:4, :]
    sc2 = mod_ref[0, 4:5, :]
    x1 = _ln_rows(DEEPNORM_ALPHA * x_ref[...] + gate1 * y) * g1_ref[...] + b1_ref[...]
    x1_ref[...] = x1
    h2 = _ln_rows(x1) * (1.0 + sc2) + sh2
    h2_ref[...] = _pack_halves(h2)
    lg_ref[...] = jnp.dot(h2.astype(BF16), wr_ref[...], preferred_element_type=F32) + br_ref[...]


def _mixout(x2, mod3, attn, u, vs, wsp, bsp_t, woa, wos, g1, b1, wr, br):
    tm = TOK_TM
    tpb = SEQ // tm
    row = lambda w: pl.BlockSpec((tm, w), lambda i: (i, 0))
    return pl.pallas_call(
        _mixout_body,
        grid=(N_TOK // tm,),
        in_specs=[row(D_MODEL),
                  pl.BlockSpec((1, N_MOD, D_MODEL), lambda i: (i // tpb, 0, 0)),
                  row(MLA_WIDTH), row(SGU_WIDTH), row(SGU_WIDTH),
                  _const_spec(wsp.shape), _const_spec(bsp_t.shape), _const_spec(woa.shape), _const_spec(wos.shape),
                  _const_spec(g1.shape), _const_spec(b1.shape), _const_spec(wr.shape), _const_spec(br.shape)],
        out_specs=[row(D_MODEL), row(HALF_D), row(LANES)],
        out_shape=[jax.ShapeDtypeStruct((N_TOK, D_MODEL), F32),
                   jax.ShapeDtypeStruct((N_TOK, HALF_D), U32),
                   jax.ShapeDtypeStruct((N_TOK, LANES), F32)],
        scratch_shapes=[pltpu.VMEM((tm, SGU_WIDTH), BF16)],
        compiler_params=_cparams(("parallel",)),
        name="mix_out",
    )(x2, mod3, attn, u, vs, wsp, bsp_t, woa, wos, g1, b1, wr, br)


def _route_body(lg_ref, o_ref):
    lg = lg_ref[...]
    lane = lax.broadcasted_iota(jnp.int32, lg.shape, 1)
    big = jnp.int32(LANES)

    def top1(vals):
        m = jnp.max(vals, axis=-1, keepdims=True)
        idx = jnp.min(jnp.where(vals == m, lane, big), axis=-1, keepdims=True)
        return m, idx

    is_group = lane < N_GROUPS
    glog = jnp.where(is_group, lg, -jnp.inf)
    gmax, gidx = top1(glog)
    pg_top = 1.0 / jnp.sum(jnp.exp(glog - gmax), axis=-1, keepdims=True)
    eid = lane - N_GROUPS
    sel = (eid >= gidx * EXPERTS_PER_GROUP) & (eid < (gidx + 1) * EXPERTS_PER_GROUP)
    elog = jnp.where(sel, lg, -jnp.inf)
    m1, i1 = top1(elog)
    m2, i2 = top1(jnp.where(lane == i1, -jnp.inf, elog))
    e2 = jnp.exp(m2 - m1)
    w1 = pg_top / (1.0 + e2)
    w2 = pg_top * e2 / (1.0 + e2)
    out = jnp.where(lane == 0, (i1 - N_GROUPS).astype(F32),
                    jnp.where(lane == 1, (i2 - N_GROUPS).astype(F32),
                              jnp.where(lane == 2, w1, jnp.where(lane == 3, w2, 0.0))))
    o_ref[...] = out


def _route(logits):
    tm = 1024
    return pl.pallas_call(
        _route_body,
        grid=(N_TOK // tm,),
        in_specs=[pl.BlockSpec((tm, LANES), lambda i: (i, 0))],
        out_specs=pl.BlockSpec((tm, LANES), lambda i: (i, 0)),
        out_shape=jax.ShapeDtypeStruct((N_TOK, LANES), F32),
        compiler_params=_cparams(("parallel",)),
        name="route",
    )(logits)


def _plan_body(rt_ref, pos_ref, tt_ref, rank_scr, cnt_scr):
    ph = pl.program_id(0)
    i = pl.program_id(1)
    t = PLAN_TM
    lane = lax.broadcasted_iota(jnp.int32, (t, LANES), 1)
    rt = rt_ref[...]
    oh0 = lane.astype(F32) == rt[:, 0:1]
    oh1 = lane.astype(F32) == rt[:, 1:2]
    rows = pl.ds(pl.multiple_of(i * t, t), t)

    @pl.when(ph == 0)
    def _():
        @pl.when(i == 0)
        def _():
            cnt_scr[...] = jnp.zeros_like(cnt_scr)

        s = jnp.where(oh0 | oh1, 1.0, 0.0)
        r = lax.broadcasted_iota(jnp.int32, (t, t), 0)
        c = lax.broadcasted_iota(jnp.int32, (t, t), 1)
        before = jnp.where(c < r, 1.0, 0.0).astype(BF16)
        csum = jnp.dot(before, s.astype(BF16), preferred_element_type=F32) + cnt_scr[...]
        rank0 = jnp.sum(jnp.where(oh0, csum, 0.0), axis=-1, keepdims=True)
        rank1 = jnp.sum(jnp.where(oh1, csum, 0.0), axis=-1, keepdims=True)
        rank_scr[rows, :] = jnp.where(lane == 0, rank0, jnp.where(lane == 1, rank1, 0.0))
        cnt_scr[...] += jnp.sum(s, axis=0, keepdims=True)

    @pl.when(ph == 1)
    def _():
        counts = cnt_scr[...]
        tiles = jnp.floor((counts + (MOE_TM - 1)) * (1.0 / MOE_TM))
        r = lax.broadcasted_iota(jnp.int32, (LANES, LANES), 0)
        c = lax.broadcasted_iota(jnp.int32, (LANES, LANES), 1)
        upto = jnp.where(r <= c, 1.0, 0.0).astype(BF16)
        tiles8 = jnp.broadcast_to(tiles, (8, LANES)).astype(BF16)
        tile_end = jnp.dot(tiles8, upto, preferred_element_type=F32)[0:1]
        offs = (tile_end - tiles) * MOE_TM
        rk = rank_scr[rows, :]
        p0 = jnp.sum(jnp.where(oh0, offs, 0.0), axis=-1, keepdims=True) + rk[:, 0:1]
        p1 = jnp.sum(jnp.where(oh1, offs, 0.0), axis=-1, keepdims=True) + rk[:, 1:2]
        pos_ref[...] = jnp.where(lane == 0, p0, jnp.where(lane == 1, p1, 0.0)).astype(jnp.int32)

        lane_e = lax.broadcasted_iota(jnp.int32, (LANES, LANES), 1)
        tile_id = lax.broadcasted_iota(jnp.int32, (LANES, LANES), 0).astype(F32)
        is_e = lane_e < N_EXPERTS
        total = jnp.max(tile_end, axis=-1, keepdims=True)
        t_exp = jnp.sum(jnp.where(is_e & (tile_end <= tile_id), 1.0, 0.0), axis=-1, keepdims=True)
        t_valid = jnp.where(tile_id[:, 0:1] < total, 1.0, 0.0)
        last_exp = jnp.sum(jnp.where(is_e & (tile_end <= total - 1.0), 1.0, 0.0), axis=-1, keepdims=True)
        t_exp = jnp.where(t_valid > 0, t_exp, last_exp)
        t_first = jnp.sum(jnp.where(is_e & (tiles > 0) & ((tile_end - tiles) == tile_id), 1.0, 0.0),
                          axis=-1, keepdims=True)
        t_last = jnp.sum(jnp.where(is_e & (tiles > 0) & ((tile_end - 1.0) == tile_id), 1.0, 0.0),
                         axis=-1, keepdims=True)
        t_clear = jnp.maximum(t_last, 1.0 - t_valid)
        none = jnp.float32(LANES)
        t_next = jnp.min(jnp.where(is_e & (tiles > 0) & (lane_e.astype(F32) > t_exp), lane_e.astype(F32), none),
                         axis=-1, keepdims=True)
        t_next = jnp.where(t_next >= none, -1.0, t_next)
        tt_ref[...] = jnp.where(lane_e == 0, t_exp, jnp.where(lane_e == 1, t_valid,
                                jnp.where(lane_e == 2, t_first,
                                          jnp.where(lane_e == 3, t_clear,
                                                    jnp.where(lane_e == 4, t_next, 0.0))))).astype(jnp.int32)


def _plan(route):
    t = PLAN_TM
    return pl.pallas_call(
        _plan_body,
        grid=(2, N_TOK // t),
        in_specs=[pl.BlockSpec((t, LANES), lambda ph, i: (i, 0))],
        out_specs=[pl.BlockSpec((t, LANES), lambda ph, i: (i * ph, 0)),
                   pl.BlockSpec((LANES, LANES), lambda ph, i: (0, 0))],
        out_shape=[jax.ShapeDtypeStruct((N_TOK, LANES), jnp.int32),
                   jax.ShapeDtypeStruct((LANES, LANES), jnp.int32)],
        scratch_shapes=[pltpu.VMEM((N_TOK, LANES), F32), pltpu.VMEM((1, LANES), F32)],
        compiler_params=_cparams(("arbitrary", "arbitrary")),
        name="plan",
    )(route)


def _rows_wait(ref, n_rows, sem):
    pltpu.make_async_copy(ref.at[pl.ds(0, n_rows)], ref.at[pl.ds(0, n_rows)], sem).wait()


def _dispatch_body(pos_ref, clear_ref, h_ref, xs_hbm, zbuf, sem_z, sem_s):
    i = pl.program_id(0)

    @pl.when(i == 0)
    def _():
        zbuf[...] = _pack_halves(jnp.zeros((MOE_TM, D_MODEL), F32))

        def zero_copy(tile):
            start = pl.multiple_of(tile * MOE_TM, MOE_TM)
            return pltpu.make_async_copy(zbuf, xs_hbm.at[pl.ds(start, MOE_TM)], sem_z)

        def clear_start(tile, carry):
            @pl.when(clear_ref[tile] > 0)
            def _():
                zero_copy(tile).start()
            return carry

        def clear_wait(tile, carry):
            @pl.when(clear_ref[tile] > 0)
            def _():
                zero_copy(tile).wait()
            return carry

        lax.fori_loop(0, MOE_TILES, clear_start, 0)
        lax.fori_loop(0, MOE_TILES, clear_wait, 0)

    def tok(j, carry):
        t = i * DISPATCH_TM + j
        src = h_ref.at[pl.ds(j, 1)]
        pltpu.make_async_copy(src, xs_hbm.at[pl.ds(pos_ref[2 * t], 1)], sem_s).start()
        pltpu.make_async_copy(src, xs_hbm.at[pl.ds(pos_ref[2 * t + 1], 1)], sem_s).start()
        return carry

    lax.fori_loop(0, DISPATCH_TM, tok, 0, unroll=8)
    _rows_wait(xs_hbm, TOP_K * DISPATCH_TM, sem_s)


def _dispatch(pos_flat, tile_clear, h2p):
    grid_spec = pltpu.PrefetchScalarGridSpec(
        num_scalar_prefetch=2,
        grid=(N_TOK // DISPATCH_TM,),
        in_specs=[pl.BlockSpec((DISPATCH_TM, HALF_D), lambda i, p, c: (i, 0))],
        out_specs=pl.BlockSpec(memory_space=pl.ANY),
        scratch_shapes=[pltpu.VMEM((MOE_TM, HALF_D), U32), pltpu.SemaphoreType.DMA(()), pltpu.SemaphoreType.DMA(())],
    )
    return pl.pallas_call(
        _dispatch_body,
        grid_spec=grid_spec,
        out_shape=jax.ShapeDtypeStruct((MOE_ROWS, HALF_D), U32),
        compiler_params=_cparams(("arbitrary",)),
        name="dispatch",
    )(pos_flat, tile_clear, h2p)


def _moe_body(te_ref, tv_ref, tf_ref, tn_ref, x_ref, wg_hbm, wu_hbm, wd_hbm, y_ref,
              wg_s, wu_s, wd_s, stg_g, stg_u, stg_d, sem):
    i = pl.program_id(0)

    def fetch(e):
        return (pltpu.make_async_copy(wg_hbm.at[e], stg_g, sem.at[0]),
                pltpu.make_async_copy(wu_hbm.at[e], stg_u, sem.at[1]),
                pltpu.make_async_copy(wd_hbm.at[e], stg_d, sem.at[2]))

    @pl.when(i == 0)
    def _():
        for cp in fetch(te_ref[0]):
            cp.start()

    @pl.when(tf_ref[i] > 0)
    def _():
        for cp in fetch(te_ref[i]):
            cp.wait()
        wg_s[...] = stg_g[...].astype(BF16)
        wu_s[...] = stg_u[...].astype(BF16)
        wd_s[...] = stg_d[...].astype(BF16)

        @pl.when(tn_ref[i] >= 0)
        def _():
            for cp in fetch(tn_ref[i]):
                cp.start()

    @pl.when(tv_ref[i] > 0)
    def _():
        lo, hi = _unpack_halves(x_ref[...])
        xa = lo.astype(BF16)
        xb = hi.astype(BF16)
        g = (jnp.dot(xa, wg_s[:HALF_D, :], preferred_element_type=F32)
             + jnp.dot(xb, wg_s[HALF_D:, :], preferred_element_type=F32))
        u = (jnp.dot(xa, wu_s[:HALF_D, :], preferred_element_type=F32)
             + jnp.dot(xb, wu_s[HALF_D:, :], preferred_element_type=F32))
        hid = (g * jax.nn.sigmoid(g) * u).astype(BF16)
        y_ref[...] = _pack_halves(jnp.dot(hid, wd_s[...], preferred_element_type=F32))

    @pl.when(tv_ref[i] == 0)
    def _():
        y_ref[...] = _pack_halves(jnp.zeros((MOE_TM, D_MODEL), F32))


def _moe(tile_expert, tile_valid, tile_first, tile_next, xs, wg, wu, wd):
    tm = MOE_TM
    grid_spec = pltpu.PrefetchScalarGridSpec(
        num_scalar_prefetch=4,
        grid=(MOE_TILES,),
        in_specs=[pl.BlockSpec((tm, HALF_D), lambda i, *_: (i, 0)),
                  pl.BlockSpec(memory_space=pl.ANY), pl.BlockSpec(memory_space=pl.ANY),
                  pl.BlockSpec(memory_space=pl.ANY)],
        out_specs=pl.BlockSpec((tm, HALF_D), lambda i, *_: (i, 0)),
        scratch_shapes=[pltpu.VMEM((D_MODEL, EXPERT_FF), BF16), pltpu.VMEM((D_MODEL, EXPERT_FF), BF16),
                        pltpu.VMEM((EXPERT_FF, D_MODEL), BF16),
                        pltpu.VMEM((D_MODEL, EXPERT_FF), F32), pltpu.VMEM((D_MODEL, EXPERT_FF), F32),
                        pltpu.VMEM((EXPERT_FF, D_MODEL), F32),
                        pltpu.SemaphoreType.DMA((3,))],
    )
    return pl.pallas_call(
        _moe_body,
        grid_spec=grid_spec,
        out_shape=jax.ShapeDtypeStruct((MOE_ROWS, HALF_D), U32),
        compiler_params=_cparams(("arbitrary",)),
        name="moe",
    )(tile_expert, tile_valid, tile_first, tile_next, xs, wg, wu, wd)


def _final_body(pos_ref, x1_ref, mod_ref, rt_ref, g2_ref, b2_ref, ys_hbm, o_ref, buf, sem):
    i = pl.program_id(0)
    n = pl.num_programs(0)
    tm = FINAL_TM

    def issue(tile, slot):
        def tok(j, carry):
            t = tile * tm + j
            pltpu.make_async_copy(ys_hbm.at[pl.ds(pos_ref[2 * t], 1)], buf.at[slot, pl.ds(j, 1)],
                                  sem.at[slot]).start()
            pltpu.make_async_copy(ys_hbm.at[pl.ds(pos_ref[2 * t + 1], 1)], buf.at[slot, pl.ds(tm + j, 1)],
                                  sem.at[slot]).start()
            return carry

        lax.fori_loop(0, tm, tok, 0, unroll=8)

    @pl.when(i == 0)
    def _():
        issue(0, 0)

    @pl.when(i + 1 < n)
    def _():
        issue(i + 1, (i + 1) % 2)

    slot = i % 2
    pltpu.make_async_copy(ys_hbm.at[pl.ds(0, TOP_K * tm)], buf.at[slot], sem.at[slot]).wait()
    a_lo, a_hi = _unpack_halves(buf[slot, 0:tm, :])
    b_lo, b_hi = _unpack_halves(buf[slot, tm:2 * tm, :])
    w0 = rt_ref[:, 2:3]
    w1 = rt_ref[:, 3:4]
    y = jnp.concatenate([w0 * a_lo + w1 * b_lo, w0 * a_hi + w1 * b_hi], axis=1)
    gate2 = mod_ref[0, 5:6, :]
    o_ref[...] = _ln_rows(DEEPNORM_ALPHA * x1_ref[...] + gate2 * y) * g2_ref[...] + b2_ref[...]


def _final(pos_flat, x1, mod3, route, g2, b2, ys):
    tm = FINAL_TM
    tpb = SEQ // tm
    row = lambda w: pl.BlockSpec((tm, w), lambda i, p: (i, 0))
    grid_spec = pltpu.PrefetchScalarGridSpec(
        num_scalar_prefetch=1,
        grid=(N_TOK // tm,),
        in_specs=[row(D_MODEL),
                  pl.BlockSpec((1, N_MOD, D_MODEL), lambda i, p: (i // tpb, 0, 0)),
                  row(LANES),
                  pl.BlockSpec(g2.shape, lambda i, p: (0, 0)),
                  pl.BlockSpec(b2.shape, lambda i, p: (0, 0)),
                  pl.BlockSpec(memory_space=pl.ANY)],
        out_specs=row(D_MODEL),
        scratch_shapes=[pltpu.VMEM((2, TOP_K * tm, HALF_D), U32), pltpu.SemaphoreType.DMA((2,))],
    )
    return pl.pallas_call(
        _final_body,
        grid_spec=grid_spec,
        out_shape=jax.ShapeDtypeStruct((N_TOK, D_MODEL), F32),
        compiler_params=_cparams(("arbitrary",)),
        name="final",
    )(pos_flat, x1, mod3, route, g2, b2, ys)


def kernel(x, c, positions, w_ada, b_ada, w_in, q_norm_g, w_uq, kv_norm_g, w_ukv, sgu_norm_g, sgu_norm_b,
           w_spatial, b_spatial, w_o, ln1_g, ln1_b, w_router_group, b_router_group, w_router_expert,
           b_router_expert, w_gate, w_up, w_down, ln2_g, ln2_b):
    l = 0
    x2 = x.reshape(N_TOK, D_MODEL)
    mod3 = _ada(c, w_ada[l], b_ada[l][None, :]).reshape(BATCH, N_MOD, D_MODEL)

    w_in_l = w_in[l]
    o1, o2, o3 = Q_LORA_RANK, Q_LORA_RANK + KV_LORA_RANK, Q_LORA_RANK + KV_LORA_RANK + QK_ROPE_DIM
    wq = w_in_l[:, :o1].astype(BF16)
    wkv = w_in_l[:, o1:o2].astype(BF16)
    wkpe = jnp.pad(w_in_l[:, o2:o3], ((0, 0), (0, LANES - QK_ROPE_DIM))).astype(BF16)
    wz = w_in_l[:, o3:].astype(BF16)
    wuq3 = w_uq[l].reshape(Q_LORA_RANK, MLA_HEADS, QK_DIM)
    wuq = jnp.concatenate([wuq3[:, :, :QK_NOPE_DIM].reshape(Q_LORA_RANK, -1),
                           wuq3[:, :, QK_NOPE_DIM:].reshape(Q_LORA_RANK, -1)], axis=1).astype(BF16)
    wukv3 = w_ukv[l].reshape(KV_LORA_RANK, MLA_HEADS, QK_NOPE_DIM + V_HEAD_DIM)
    wukv = jnp.concatenate([wukv3[:, :, :QK_NOPE_DIM].reshape(KV_LORA_RANK, -1),
                            wukv3[:, :, QK_NOPE_DIM:].reshape(KV_LORA_RANK, -1)], axis=1).astype(BF16)
    woa = w_o[l][:MLA_WIDTH].astype(BF16)
    wos = w_o[l][MLA_WIDTH:].astype(BF16)
    n_r = N_GROUPS + N_EXPERTS
    wr = jnp.pad(jnp.concatenate([w_router_group[l], w_router_expert[l]], axis=1),
                 ((0, 0), (0, LANES - n_r))).astype(BF16)
    br = jnp.pad(jnp.concatenate([b_router_group[l], b_router_expert[l]]), (0, LANES - n_r))[None, :]
    inv_freq = 1.0 / (ROPE_THETA ** (jnp.arange(0, QK_ROPE_DIM, 2, dtype=F32) / QK_ROPE_DIM))
    invf = jnp.tile(inv_freq, 2 * LANES // QK_ROPE_DIM)[None, :]

    cqn, ckvn, kpe, u, vs = _inproj(x2, mod3, wq, wkv, wkpe, wz, q_norm_g[l][None, :], kv_norm_g[l][None, :],
                                    sgu_norm_g[l][None, :], sgu_norm_b[l][None, :])
    q, k, v = _qkv(cqn, ckvn, kpe, positions.reshape(N_TOK, 1), invf, wuq, wukv)
    attn = _attn(q, k, v).reshape(N_TOK, MLA_WIDTH)
    x1, h2, logits = _mixout(x2, mod3, attn, u, vs, w_spatial[l], b_spatial[l].T, woa, wos,
                             ln1_g[l][None, :], ln1_b[l][None, :], wr, br)
    route = _route(logits)
    pos_tab, tile_tab = _plan(route)
    pos_flat = pos_tab[:, :TOP_K].reshape(-1)
    xs = _dispatch(pos_flat, tile_tab[:MOE_TILES, 3], h2)
    ys = _moe(tile_tab[:MOE_TILES, 0], tile_tab[:MOE_TILES, 1], tile_tab[:MOE_TILES, 2], tile_tab[:MOE_TILES, 4],
              xs, w_gate[l], w_up[l], w_down[l])
    out = _final(pos_flat, x1, mod3, route, ln2_g[l][None, :], ln2_b[l][None, :], ys)
    return out.reshape(BATCH, SEQ, D_MODEL)
```

```python
import functools

import jax
import jax.numpy as jnp
import numpy as np
from jax import lax
from jax.experimental import pallas as pl
from jax.experimental.pallas import tpu as pltpu

D_MODEL = 2048
BATCH = 4
SEQ = 2048
N_TOK = BATCH * SEQ

MLA_HEADS = 8
QK_NOPE_DIM = 128
QK_ROPE_DIM = 64
QK_DIM = QK_NOPE_DIM + QK_ROPE_DIM
V_HEAD_DIM = 128
Q_LORA_RANK = 768
KV_LORA_RANK = 512
ROPE_THETA = 10000.0
MLA_WIDTH = MLA_HEADS * V_HEAD_DIM

SGU_GROUPS = 8
SGU_GROUP_DIM = 128
SGU_CHUNK = 128
SGU_WIDTH = SGU_GROUPS * SGU_GROUP_DIM

N_GROUPS = 4
EXPERTS_PER_GROUP = 8
N_EXPERTS = N_GROUPS * EXPERTS_PER_GROUP
TOP_K = 2
EXPERT_FF = 512

DEEPNORM_ALPHA = 2.0 ** 0.25
EPS = 1e-6
N_MOD = 6
NEG_BIG = -1e30

LANES = 128
VMEM_LIMIT = 56 * 1024 * 1024

ADA_TN = 1024
TOK_TM = 256
ATT_TQ = 512
ATT_TK = 512
ATT_HEADS = 2
MOE_TM = 256
MOE_TILES = (N_TOK * TOP_K + N_EXPERTS * (MOE_TM - 1)) // MOE_TM + 1
MOE_ROWS = MOE_TILES * MOE_TM
PLAN_TM = 512
DISPATCH_TM = 1024
FINAL_TM = 256
assert MOE_TILES <= LANES

F32 = jnp.float32
BF16 = jnp.bfloat16
U32 = jnp.uint32
HALF_D = D_MODEL // 2


def _cparams(sem):
    return pltpu.CompilerParams(dimension_semantics=sem, vmem_limit_bytes=VMEM_LIMIT)


def _const_spec(shape):
    nd = len(shape)
    return pl.BlockSpec(shape, lambda *_: (0,) * nd, pipeline_mode=pl.Buffered(1))


def _ln_rows(x):
    mu = jnp.mean(x, axis=-1, keepdims=True)
    xc = x - mu
    var = jnp.mean(xc * xc, axis=-1, keepdims=True)
    return xc * lax.rsqrt(var + EPS)


def _rms_rows(x):
    return x * lax.rsqrt(jnp.mean(x * x, axis=-1, keepdims=True) + EPS)


def _pack_halves(x):
    half = x.shape[-1] // 2
    return pltpu.pack_elementwise([x[:, :half], x[:, half:]], packed_dtype=BF16)


def _unpack_halves(w):
    lo = pltpu.unpack_elementwise(w, index=0, packed_dtype=BF16, unpacked_dtype=F32)
    hi = pltpu.unpack_elementwise(w, index=1, packed_dtype=BF16, unpacked_dtype=F32)
    return lo, hi


def _gelu_tanh(x):
    c = np.sqrt(2.0 / np.pi).astype(np.float32)
    return 0.5 * x * (1.0 + jnp.tanh(c * (x + 0.044715 * (x * x * x))))


def _ada_body(c_ref, w_ref, b_ref, o_ref):
    o_ref[...] = jnp.dot(c_ref[...].astype(BF16), w_ref[...].astype(BF16),
                         preferred_element_type=F32) + b_ref[...]


def _ada(c, w, b):
    n = w.shape[1]
    return pl.pallas_call(
        _ada_body,
        grid=(n // ADA_TN,),
        in_specs=[pl.BlockSpec((BATCH, D_MODEL), lambda j: (0, 0)),
                  pl.BlockSpec((D_MODEL, ADA_TN), lambda j: (0, j)),
                  pl.BlockSpec((1, ADA_TN), lambda j: (0, j))],
        out_specs=pl.BlockSpec((BATCH, ADA_TN), lambda j: (0, j)),
        out_shape=jax.ShapeDtypeStruct((BATCH, n), F32),
        compiler_params=_cparams(("parallel",)),
        name="ada",
    )(c, w, b)


def _inproj_body(x_ref, mod_ref, wq_ref, wkv_ref, wkpe_ref, wz_ref, gq_ref, gkv_ref, sg_ref, sb_ref,
                 cq_ref, ckv_ref, kpe_ref, u_ref, vs_ref):
    sh = mod_ref[0, 0:1, :]
    sc = mod_ref[0, 1:2, :]
    h = (_ln_rows(x_ref[...]) * (1.0 + sc) + sh).astype(BF16)
    cq = jnp.dot(h, wq_ref[...], preferred_element_type=F32)
    cq_ref[...] = (_rms_rows(cq) * gq_ref[...]).astype(BF16)
    ckv = jnp.dot(h, wkv_ref[...], preferred_element_type=F32)
    ckv_ref[...] = (_rms_rows(ckv) * gkv_ref[...]).astype(BF16)
    kpe_ref[...] = jnp.dot(h, wkpe_ref[...], preferred_element_type=F32)
    gz = _gelu_tanh(jnp.dot(h, wz_ref[...], preferred_element_type=F32))
    u_ref[...] = gz[:, :SGU_WIDTH]
    vs_ref[...] = (_ln_rows(gz[:, SGU_WIDTH:]) * sg_ref[...] + sb_ref[...]).astype(BF16)


def _inproj(x2, mod3, wq, wkv, wkpe, wz, gq, gkv, sg, sb):
    tm = TOK_TM
    tiles_per_batch = SEQ // tm
    row = lambda w: pl.BlockSpec((tm, w), lambda i: (i, 0))
    return pl.pallas_call(
        _inproj_body,
        grid=(N_TOK // tm,),
        in_specs=[row(D_MODEL),
                  pl.BlockSpec((1, N_MOD, D_MODEL), lambda i: (i // tiles_per_batch, 0, 0)),
                  _const_spec(wq.shape), _const_spec(wkv.shape), _const_spec(wkpe.shape), _const_spec(wz.shape),
                  _const_spec(gq.shape), _const_spec(gkv.shape), _const_spec(sg.shape), _const_spec(sb.shape)],
        out_specs=[row(Q_LORA_RANK), row(KV_LORA_RANK), row(LANES), row(SGU_WIDTH), row(SGU_WIDTH)],
        out_shape=[jax.ShapeDtypeStruct((N_TOK, Q_LORA_RANK), BF16),
                   jax.ShapeDtypeStruct((N_TOK, KV_LORA_RANK), BF16),
                   jax.ShapeDtypeStruct((N_TOK, LANES), F32),
                   jax.ShapeDtypeStruct((N_TOK, SGU_WIDTH), F32),
                   jax.ShapeDtypeStruct((N_TOK, SGU_WIDTH), BF16)],
        compiler_params=_cparams(("parallel",)),
        name="inproj",
    )(x2, mod3, wq, wkv, wkpe, wz, gq, gkv, sg, sb)


def _rope(x, cos, sin):
    w = x.shape[-1]
    lane = lax.broadcasted_iota(jnp.int32, x.shape, 1)
    first_half = (lane % QK_ROPE_DIM) < (QK_ROPE_DIM // 2)
    rot = jnp.where(first_half,
                    -pltpu.roll(x, w - QK_ROPE_DIM // 2, 1),
                    pltpu.roll(x, QK_ROPE_DIM // 2, 1))
    return x * cos + rot * sin


def _qkv_body(cq_ref, ckv_ref, kpe_ref, pos_ref, invf_ref, wuq_ref, wukv_ref, q_ref, k_ref, v_ref):
    ang = pos_ref[...].astype(F32) * invf_ref[...]
    cos1 = jnp.cos(ang)
    sin1 = jnp.sin(ang)
    reps = MLA_HEADS * QK_ROPE_DIM // LANES
    cos = jnp.concatenate([cos1] * reps, axis=1)
    sin = jnp.concatenate([sin1] * reps, axis=1)
    scale = np.float32(QK_DIM ** -0.5)
    q = jnp.dot(cq_ref[...], wuq_ref[...], preferred_element_type=F32) * scale
    q_pe = _rope(q[:, MLA_HEADS * QK_NOPE_DIM:], cos, sin)
    kv = jnp.dot(ckv_ref[...], wukv_ref[...], preferred_element_type=F32)
    k_pe = _rope(kpe_ref[...], cos1, sin1)[:, :QK_ROPE_DIM].astype(BF16)
    for h in range(MLA_HEADS):
        q_ref[0, h, :, 0:QK_NOPE_DIM] = q[:, h * QK_NOPE_DIM:(h + 1) * QK_NOPE_DIM].astype(BF16)
        q_ref[0, h, :, QK_NOPE_DIM:QK_DIM] = q_pe[:, h * QK_ROPE_DIM:(h + 1) * QK_ROPE_DIM].astype(BF16)
        k_ref[0, h, :, 0:QK_NOPE_DIM] = kv[:, h * QK_NOPE_DIM:(h + 1) * QK_NOPE_DIM].astype(BF16)
        k_ref[0, h, :, QK_NOPE_DIM:QK_DIM] = k_pe
        v_ref[0, h, :, :] = kv[:, MLA_WIDTH + h * V_HEAD_DIM:MLA_WIDTH + (h + 1) * V_HEAD_DIM].astype(BF16)


def _qkv(cqn, ckvn, kpe, pos2, invf, wuq, wukv):
    tm = TOK_TM
    tpb = SEQ // tm
    row = lambda w: pl.BlockSpec((tm, w), lambda i: (i, 0))
    head_out = lambda w: pl.BlockSpec((1, MLA_HEADS, tm, w), lambda i: (i // tpb, 0, i % tpb, 0))
    return pl.pallas_call(
        _qkv_body,
        grid=(N_TOK // tm,),
        in_specs=[row(Q_LORA_RANK), row(KV_LORA_RANK), row(LANES), row(1),
                  _const_spec(invf.shape), _const_spec(wuq.shape), _const_spec(wukv.shape)],
        out_specs=[head_out(QK_DIM), head_out(QK_DIM), head_out(V_HEAD_DIM)],
        out_shape=[jax.ShapeDtypeStruct((BATCH, MLA_HEADS, SEQ, QK_DIM), BF16),
                   jax.ShapeDtypeStruct((BATCH, MLA_HEADS, SEQ, QK_DIM), BF16),
                   jax.ShapeDtypeStruct((BATCH, MLA_HEADS, SEQ, V_HEAD_DIM), BF16)],
        compiler_params=_cparams(("parallel",)),
        name="qkv",
    )(cqn, ckvn, kpe, pos2, invf, wuq, wukv)


def _attn_body(q_ref, k_ref, v_ref, o_ref):
    i = pl.program_id(2)

    def step(h, j, carry, masked):
        m, l, acc = carry
        start = pl.multiple_of(j * ATT_TK, ATT_TK)
        k = k_ref[0, h, pl.ds(start, ATT_TK), :]
        v = v_ref[0, h, pl.ds(start, ATT_TK), :]
        s = lax.dot_general(q_ref[0, h], k, (((1,), (1,)), ((), ())), preferred_element_type=F32)
        if masked:
            r = lax.broadcasted_iota(jnp.int32, s.shape, 0)
            c = lax.broadcasted_iota(jnp.int32, s.shape, 1)
            s = jnp.where(c <= r, s, NEG_BIG)
        m_new = jnp.maximum(m, jnp.max(s, axis=-1, keepdims=True))
        p = jnp.exp(s - m_new)
        a = jnp.exp(m - m_new)
        l = a * l + jnp.sum(p, axis=-1, keepdims=True)
        acc = a * acc + jnp.dot(p.astype(BF16), v, preferred_element_type=F32)
        return m_new, l, acc

    def steps(j, carries, masked):
        return tuple(step(h, j, carries[h], masked) for h in range(ATT_HEADS))

    init = tuple((jnp.full((ATT_TQ, 1), NEG_BIG, F32), jnp.zeros((ATT_TQ, 1), F32),
                  jnp.zeros((ATT_TQ, V_HEAD_DIM), F32)) for _ in range(ATT_HEADS))
    carries = lax.fori_loop(0, i, lambda j, c: steps(j, c, False), init)
    carries = steps(i, carries, True)
    for h, (m, l, acc) in enumerate(carries):
        o_ref[0, :, h * V_HEAD_DIM:(h + 1) * V_HEAD_DIM] = (acc / l).astype(BF16)


def _attn(q, k, v):
    assert ATT_TQ == ATT_TK
    hb = ATT_HEADS
    return pl.pallas_call(
        _attn_body,
        grid=(BATCH, MLA_HEADS // hb, SEQ // ATT_TQ),
        in_specs=[pl.BlockSpec((1, hb, ATT_TQ, QK_DIM), lambda b, h, i: (b, h, i, 0)),
                  pl.BlockSpec((1, hb, SEQ, QK_DIM), lambda b, h, i: (b, h, 0, 0)),
                  pl.BlockSpec((1, hb, SEQ, V_HEAD_DIM), lambda b, h, i: (b, h, 0, 0))],
        out_specs=pl.BlockSpec((1, ATT_TQ, hb * V_HEAD_DIM), lambda b, h, i: (b, i, h)),
        out_shape=jax.ShapeDtypeStruct((BATCH, SEQ, MLA_WIDTH), BF16),
        compiler_params=_cparams(("parallel", "parallel", "arbitrary")),
        name="attn",
    )(q, k, v)


def _mixout_body(x_ref, mod_ref, attn_ref, u_ref, vs_ref, wsp_ref, bsp_ref, woa_ref, wos_ref,
                 g1_ref, b1_ref, wr_ref, br_ref, x1_ref, h2_ref, lg_ref, sgu_scr):
    r = lax.broadcasted_iota(jnp.int32, (SGU_CHUNK, SGU_CHUNK), 0)
    c = lax.broadcasted_iota(jnp.int32, (SGU_CHUNK, SGU_CHUNK), 1)
    causal = c <= r
    for g in range(SGU_GROUPS):
        ws = jnp.where(causal, wsp_ref[g], 0.0).astype(BF16)
        bias = bsp_ref[:, g:g + 1]
        cols = slice(g * SGU_GROUP_DIM, (g + 1) * SGU_GROUP_DIM)
        for ch in range(TOK_TM // SGU_CHUNK):
            rows = slice(ch * SGU_CHUNK, (ch + 1) * SGU_CHUNK)
            mixed = jnp.dot(ws, vs_ref[rows, cols], preferred_element_type=F32) + bias
            sgu_scr[rows, cols] = (u_ref[rows, cols] * mixed).astype(BF16)
    y = (jnp.dot(attn_ref[...], woa_ref[...], preferred_element_type=F32)
         + jnp.dot(sgu_scr[...], wos_ref[...], preferred_element_type=F32))
    gate1 = mod_ref[0, 2:3, :]
    sh2 = mod_ref[0, 3:4, :]
    sc2 = mod_ref[0, 4:5, :]
    x1 = _ln_rows(DEEPNORM_ALPHA * x_ref[...] + gate1 * y) * g1_ref[...] + b1_ref[...]
    x1_ref[...] = x1
    h2 = _ln_rows(x1) * (1.0 + sc2) + sh2
    h2_ref[...] = _pack_halves(h2)
    lg_ref[...] = jnp.dot(h2.astype(BF16), wr_ref[...], preferred_element_type=F32) + br_ref[...]


def _mixout(x2, mod3, attn, u, vs, wsp, bsp_t, woa, wos, g1, b1, wr, br):
    tm = TOK_TM
    tpb = SEQ // tm
    row = lambda w: pl.BlockSpec((tm, w), lambda i: (i, 0))
    return pl.pallas_call(
        _mixout_body,
        grid=(N_TOK // tm,),
        in_specs=[row(D_MODEL),
                  pl.BlockSpec((1, N_MOD, D_MODEL), lambda i: (i // tpb, 0, 0)),
                  row(MLA_WIDTH), row(SGU_WIDTH), row(SGU_WIDTH),
                  _const_spec(wsp.shape), _const_spec(bsp_t.shape), _const_spec(woa.shape), _const_spec(wos.shape),
                  _const_spec(g1.shape), _const_spec(b1.shape), _const_spec(wr.shape), _const_spec(br.shape)],
        out_specs=[row(D_MODEL), row(HALF_D), row(LANES)],
        out_shape=[jax.ShapeDtypeStruct((N_TOK, D_MODEL), F32),
                   jax.ShapeDtypeStruct((N_TOK, HALF_D), U32),
                   jax.ShapeDtypeStruct((N_TOK, LANES), F32)],
        scratch_shapes=[pltpu.VMEM((tm, SGU_WIDTH), BF16)],
        compiler_params=_cparams(("parallel",)),
        name="mix_out",
    )(x2, mod3, attn, u, vs, wsp, bsp_t, woa, wos, g1, b1, wr, br)


def _route_body(lg_ref, o_ref):
    lg = lg_ref[...]
    lane = lax.broadcasted_iota(jnp.int32, lg.shape, 1)
    big = jnp.int32(LANES)

    def top1(vals):
        m = jnp.max(vals, axis=-1, keepdims=True)
        idx = jnp.min(jnp.where(vals == m, lane, big), axis=-1, keepdims=True)
        return m, idx

    is_group = lane < N_GROUPS
    glog = jnp.where(is_group, lg, -jnp.inf)
    gmax, gidx = top1(glog)
    pg_top = 1.0 / jnp.sum(jnp.exp(glog - gmax), axis=-1, keepdims=True)
    eid = lane - N_GROUPS
    sel = (eid >= gidx * EXPERTS_PER_GROUP) & (eid < (gidx + 1) * EXPERTS_PER_GROUP)
    elog = jnp.where(sel, lg, -jnp.inf)
    m1, i1 = top1(elog)
    m2, i2 = top1(jnp.where(lane == i1, -jnp.inf, elog))
    e2 = jnp.exp(m2 - m1)
    w1 = pg_top / (1.0 + e2)
    w2 = pg_top * e2 / (1.0 + e2)
    out = jnp.where(lane == 0, (i1 - N_GROUPS).astype(F32),
                    jnp.where(lane == 1, (i2 - N_GROUPS).astype(F32),
                              jnp.where(lane == 2, w1, jnp.where(lane == 3, w2, 0.0))))
    o_ref[...] = out


def _route(logits):
    tm = 1024
    return pl.pallas_call(
        _route_body,
        grid=(N_TOK // tm,),
        in_specs=[pl.BlockSpec((tm, LANES), lambda i: (i, 0))],
        out_specs=pl.BlockSpec((tm, LANES), lambda i: (i, 0)),
        out_shape=jax.ShapeDtypeStruct((N_TOK, LANES), F32),
        compiler_params=_cparams(("parallel",)),
        name="route",
    )(logits)


def _plan_body(rt_ref, pos_ref, tt_ref, rank_scr, cnt_scr):
    ph = pl.program_id(0)
    i = pl.program_id(1)
    t = PLAN_TM
    lane = lax.broadcasted_iota(jnp.int32, (t, LANES), 1)
    rt = rt_ref[...]
    oh0 = lane.astype(F32) == rt[:, 0:1]
    oh1 = lane.astype(F32) == rt[:, 1:2]
    rows = pl.ds(pl.multiple_of(i * t, t), t)

    @pl.when(ph == 0)
    def _():
        @pl.when(i == 0)
        def _():
            cnt_scr[...] = jnp.zeros_like(cnt_scr)

        s = jnp.where(oh0 | oh1, 1.0, 0.0)
        r = lax.broadcasted_iota(jnp.int32, (t, t), 0)
        c = lax.broadcasted_iota(jnp.int32, (t, t), 1)
        before = jnp.where(c < r, 1.0, 0.0).astype(BF16)
        csum = jnp.dot(before, s.astype(BF16), preferred_element_type=F32) + cnt_scr[...]
        rank0 = jnp.sum(jnp.where(oh0, csum, 0.0), axis=-1, keepdims=True)
        rank1 = jnp.sum(jnp.where(oh1, csum, 0.0), axis=-1, keepdims=True)
        rank_scr[rows, :] = jnp.where(lane == 0, rank0, jnp.where(lane == 1, rank1, 0.0))
        cnt_scr[...] += jnp.sum(s, axis=0, keepdims=True)

    @pl.when(ph == 1)
    def _():
        counts = cnt_scr[...]
        tiles = jnp.floor((counts + (MOE_TM - 1)) * (1.0 / MOE_TM))
        r = lax.broadcasted_iota(jnp.int32, (LANES, LANES), 0)
        c = lax.broadcasted_iota(jnp.int32, (LANES, LANES), 1)
        upto = jnp.where(r <= c, 1.0, 0.0).astype(BF16)
        tiles8 = jnp.broadcast_to(tiles, (8, LANES)).astype(BF16)
        tile_end = jnp.dot(tiles8, upto, preferred_element_type=F32)[0:1]
        offs = (tile_end - tiles) * MOE_TM
        rk = rank_scr[rows, :]
        p0 = jnp.sum(jnp.where(oh0, offs, 0.0), axis=-1, keepdims=True) + rk[:, 0:1]
        p1 = jnp.sum(jnp.where(oh1, offs, 0.0), axis=-1, keepdims=True) + rk[:, 1:2]
        pos_ref[...] = jnp.where(lane == 0, p0, jnp.where(lane == 1, p1, 0.0)).astype(jnp.int32)

        lane_e = lax.broadcasted_iota(jnp.int32, (LANES, LANES), 1)
        tile_id = lax.broadcasted_iota(jnp.int32, (LANES, LANES), 0).astype(F32)
        is_e = lane_e < N_EXPERTS
        total = jnp.max(tile_end, axis=-1, keepdims=True)
        t_exp = jnp.sum(jnp.where(is_e & (tile_end <= tile_id), 1.0, 0.0), axis=-1, keepdims=True)
        t_valid = jnp.where(tile_id[:, 0:1] < total, 1.0, 0.0)
        last_exp = jnp.sum(jnp.where(is_e & (tile_end <= total - 1.0), 1.0, 0.0), axis=-1, keepdims=True)
        t_exp = jnp.where(t_valid > 0, t_exp, last_exp)
        t_first = jnp.sum(jnp.where(is_e & (tiles > 0) & ((tile_end - tiles) == tile_id), 1.0, 0.0),
                          axis=-1, keepdims=True)
        t_last = jnp.sum(jnp.where(is_e & (tiles > 0) & ((tile_end - 1.0) == tile_id), 1.0, 0.0),
                         axis=-1, keepdims=True)
        t_clear = jnp.maximum(t_last, 1.0 - t_valid)
        none = jnp.float32(LANES)
        owns = is_e & (tiles > 0)
        lane_f = lane_e.astype(F32)
        t_next = jnp.min(jnp.where(owns & (lane_f > t_exp), lane_f, none), axis=-1, keepdims=True)
        t_next2 = jnp.min(jnp.where(owns & (lane_f > t_next), lane_f, none), axis=-1, keepdims=True)
        t_ord = jnp.sum(jnp.where(owns & (lane_f < t_exp), 1.0, 0.0), axis=-1, keepdims=True)
        cols = (t_exp, t_valid, t_first, t_clear, t_next, t_next2, t_ord)
        table = jnp.zeros((LANES, LANES), F32)
        for k, col in enumerate(cols):
            table = jnp.where(lane_e == k, col, table)
        tt_ref[...] = table.astype(jnp.int32)


def _plan(route):
    t = PLAN_TM
    return pl.pallas_call(
        _plan_body,
        grid=(2, N_TOK // t),
        in_specs=[pl.BlockSpec((t, LANES), lambda ph, i: (i, 0))],
        out_specs=[pl.BlockSpec((t, LANES), lambda ph, i: (i * ph, 0)),
                   pl.BlockSpec((LANES, LANES), lambda ph, i: (0, 0))],
        out_shape=[jax.ShapeDtypeStruct((N_TOK, LANES), jnp.int32),
                   jax.ShapeDtypeStruct((LANES, LANES), jnp.int32)],
        scratch_shapes=[pltpu.VMEM((N_TOK, LANES), F32), pltpu.VMEM((1, LANES), F32)],
        compiler_params=_cparams(("arbitrary", "arbitrary")),
        name="plan",
    )(route)


def _rows_wait(ref, n_rows, sem):
    pltpu.make_async_copy(ref.at[pl.ds(0, n_rows)], ref.at[pl.ds(0, n_rows)], sem).wait()


def _dispatch_body(pos_ref, clear_ref, h_ref, xs_hbm, zbuf, sem_z, sem_s):
    i = pl.program_id(0)

    @pl.when(i == 0)
    def _():
        zbuf[...] = _pack_halves(jnp.zeros((MOE_TM, D_MODEL), F32))

        def zero_copy(tile):
            start = pl.multiple_of(tile * MOE_TM, MOE_TM)
            return pltpu.make_async_copy(zbuf, xs_hbm.at[pl.ds(start, MOE_TM)], sem_z)

        def clear_start(tile, carry):
            @pl.when(clear_ref[tile] > 0)
            def _():
                zero_copy(tile).start()
            return carry

        def clear_wait(tile, carry):
            @pl.when(clear_ref[tile] > 0)
            def _():
                zero_copy(tile).wait()
            return carry

        lax.fori_loop(0, MOE_TILES, clear_start, 0)
        lax.fori_loop(0, MOE_TILES, clear_wait, 0)

    def tok(j, carry):
        t = i * DISPATCH_TM + j
        src = h_ref.at[pl.ds(j, 1)]
        pltpu.make_async_copy(src, xs_hbm.at[pl.ds(pos_ref[2 * t], 1)], sem_s).start()
        pltpu.make_async_copy(src, xs_hbm.at[pl.ds(pos_ref[2 * t + 1], 1)], sem_s).start()
        return carry

    lax.fori_loop(0, DISPATCH_TM, tok, 0, unroll=8)
    _rows_wait(xs_hbm, TOP_K * DISPATCH_TM, sem_s)


def _dispatch(pos_flat, tile_clear, h2p):
    grid_spec = pltpu.PrefetchScalarGridSpec(
        num_scalar_prefetch=2,
        grid=(N_TOK // DISPATCH_TM,),
        in_specs=[pl.BlockSpec((DISPATCH_TM, HALF_D), lambda i, p, c: (i, 0))],
        out_specs=pl.BlockSpec(memory_space=pl.ANY),
        scratch_shapes=[pltpu.VMEM((MOE_TM, HALF_D), U32), pltpu.SemaphoreType.DMA(()), pltpu.SemaphoreType.DMA(())],
    )
    return pl.pallas_call(
        _dispatch_body,
        grid_spec=grid_spec,
        out_shape=jax.ShapeDtypeStruct((MOE_ROWS, HALF_D), U32),
        compiler_params=_cparams(("arbitrary",)),
        name="dispatch",
    )(pos_flat, tile_clear, h2p)


def _moe_body(te_ref, tv_ref, tf_ref, tn_ref, tn2_ref, to_ref, x_ref, wg_hbm, wu_hbm, wd_hbm, y_ref,
              wg_s, wu_s, wd_s, stg_g, stg_u, stg_d, sem):
    i = pl.program_id(0)

    def fetch(e, slot):
        return (pltpu.make_async_copy(wg_hbm.at[e], stg_g.at[slot], sem.at[slot, 0]),
                pltpu.make_async_copy(wu_hbm.at[e], stg_u.at[slot], sem.at[slot, 1]),
                pltpu.make_async_copy(wd_hbm.at[e], stg_d.at[slot], sem.at[slot, 2]))

    @pl.when(i == 0)
    def _():
        for cp in fetch(te_ref[0], 0):
            cp.start()

        @pl.when(tn_ref[0] < N_EXPERTS)
        def _():
            for cp in fetch(tn_ref[0], 1):
                cp.start()

    @pl.when(tf_ref[i] > 0)
    def _():
        slot = to_ref[i] % 2
        for cp in fetch(te_ref[i], slot):
            cp.wait()
        wg_s[...] = stg_g[slot].astype(BF16)
        wu_s[...] = stg_u[slot].astype(BF16)
        wd_s[...] = stg_d[slot].astype(BF16)

        @pl.when(tn2_ref[i] < N_EXPERTS)
        def _():
            for cp in fetch(tn2_ref[i], slot):
                cp.start()

    @pl.when(tv_ref[i] > 0)
    def _():
        lo, hi = _unpack_halves(x_ref[...])
        xa = lo.astype(BF16)
        xb = hi.astype(BF16)
        g = (jnp.dot(xa, wg_s[:HALF_D, :], preferred_element_type=F32)
             + jnp.dot(xb, wg_s[HALF_D:, :], preferred_element_type=F32))
        u = (jnp.dot(xa, wu_s[:HALF_D, :], preferred_element_type=F32)
             + jnp.dot(xb, wu_s[HALF_D:, :], preferred_element_type=F32))
        hid = (g * jax.nn.sigmoid(g) * u).astype(BF16)
        y_ref[...] = _pack_halves(jnp.dot(hid, wd_s[...], preferred_element_type=F32))

    @pl.when(tv_ref[i] == 0)
    def _():
        y_ref[...] = _pack_halves(jnp.zeros((MOE_TM, D_MODEL), F32))


def _moe(tile_cols, xs, wg, wu, wd):
    tm = MOE_TM
    grid_spec = pltpu.PrefetchScalarGridSpec(
        num_scalar_prefetch=len(tile_cols),
        grid=(MOE_TILES,),
        in_specs=[pl.BlockSpec((tm, HALF_D), lambda i, *_: (i, 0)),
                  pl.BlockSpec(memory_space=pl.ANY), pl.BlockSpec(memory_space=pl.ANY),
                  pl.BlockSpec(memory_space=pl.ANY)],
        out_specs=pl.BlockSpec((tm, HALF_D), lambda i, *_: (i, 0)),
        scratch_shapes=[pltpu.VMEM((D_MODEL, EXPERT_FF), BF16), pltpu.VMEM((D_MODEL, EXPERT_FF), BF16),
                        pltpu.VMEM((EXPERT_FF, D_MODEL), BF16),
                        pltpu.VMEM((2, D_MODEL, EXPERT_FF), F32), pltpu.VMEM((2, D_MODEL, EXPERT_FF), F32),
                        pltpu.VMEM((2, EXPERT_FF, D_MODEL), F32),
                        pltpu.SemaphoreType.DMA((2, 3))],
    )
    return pl.pallas_call(
        _moe_body,
        grid_spec=grid_spec,
        out_shape=jax.ShapeDtypeStruct((MOE_ROWS, HALF_D), U32),
        compiler_params=_cparams(("arbitrary",)),
        name="moe",
    )(*tile_cols, xs, wg, wu, wd)


def _final_body(pos_ref, x1_ref, mod_ref, rt_ref, g2_ref, b2_ref, ys_hbm, o_ref, buf, sem):
    i = pl.program_id(0)
    n = pl.num_programs(0)
    tm = FINAL_TM

    def issue(tile, slot):
        def tok(j, carry):
            t = tile * tm + j
            pltpu.make_async_copy(ys_hbm.at[pl.ds(pos_ref[2 * t], 1)], buf.at[slot, pl.ds(j, 1)],
                                  sem.at[slot]).start()
            pltpu.make_async_copy(ys_hbm.at[pl.ds(pos_ref[2 * t + 1], 1)], buf.at[slot, pl.ds(tm + j, 1)],
                                  sem.at[slot]).start()
            return carry

        lax.fori_loop(0, tm, tok, 0, unroll=8)

    @pl.when(i == 0)
    def _():
        issue(0, 0)

    @pl.when(i + 1 < n)
    def _():
        issue(i + 1, (i + 1) % 2)

    slot = i % 2
    pltpu.make_async_copy(ys_hbm.at[pl.ds(0, TOP_K * tm)], buf.at[slot], sem.at[slot]).wait()
    a_lo, a_hi = _unpack_halves(buf[slot, 0:tm, :])
    b_lo, b_hi = _unpack_halves(buf[slot, tm:2 * tm, :])
    w0 = rt_ref[:, 2:3]
    w1 = rt_ref[:, 3:4]
    y = jnp.concatenate([w0 * a_lo + w1 * b_lo, w0 * a_hi + w1 * b_hi], axis=1)
    gate2 = mod_ref[0, 5:6, :]
    o_ref[...] = _ln_rows(DEEPNORM_ALPHA * x1_ref[...] + gate2 * y) * g2_ref[...] + b2_ref[...]


def _final(pos_flat, x1, mod3, route, g2, b2, ys):
    tm = FINAL_TM
    tpb = SEQ // tm
    row = lambda w: pl.BlockSpec((tm, w), lambda i, p: (i, 0))
    grid_spec = pltpu.PrefetchScalarGridSpec(
        num_scalar_prefetch=1,
        grid=(N_TOK // tm,),
        in_specs=[row(D_MODEL),
                  pl.BlockSpec((1, N_MOD, D_MODEL), lambda i, p: (i // tpb, 0, 0)),
                  row(LANES),
                  pl.BlockSpec(g2.shape, lambda i, p: (0, 0)),
                  pl.BlockSpec(b2.shape, lambda i, p: (0, 0)),
                  pl.BlockSpec(memory_space=pl.ANY)],
        out_specs=row(D_MODEL),
        scratch_shapes=[pltpu.VMEM((2, TOP_K * tm, HALF_D), U32), pltpu.SemaphoreType.DMA((2,))],
    )
    return pl.pallas_call(
        _final_body,
        grid_spec=grid_spec,
        out_shape=jax.ShapeDtypeStruct((N_TOK, D_MODEL), F32),
        compiler_params=_cparams(("arbitrary",)),
        name="final",
    )(pos_flat, x1, mod3, route, g2, b2, ys)


def kernel(x, c, positions, w_ada, b_ada, w_in, q_norm_g, w_uq, kv_norm_g, w_ukv, sgu_norm_g, sgu_norm_b,
           w_spatial, b_spatial, w_o, ln1_g, ln1_b, w_router_group, b_router_group, w_router_expert,
           b_router_expert, w_gate, w_up, w_down, ln2_g, ln2_b):
    l = 0
    x2 = x.reshape(N_TOK, D_MODEL)
    mod3 = _ada(c, w_ada[l], b_ada[l][None, :]).reshape(BATCH, N_MOD, D_MODEL)

    w_in_l = w_in[l]
    o1, o2, o3 = Q_LORA_RANK, Q_LORA_RANK + KV_LORA_RANK, Q_LORA_RANK + KV_LORA_RANK + QK_ROPE_DIM
    wq = w_in_l[:, :o1].astype(BF16)
    wkv = w_in_l[:, o1:o2].astype(BF16)
    wkpe = jnp.pad(w_in_l[:, o2:o3], ((0, 0), (0, LANES - QK_ROPE_DIM))).astype(BF16)
    wz = w_in_l[:, o3:].astype(BF16)
    wuq3 = w_uq[l].reshape(Q_LORA_RANK, MLA_HEADS, QK_DIM)
    wuq = jnp.concatenate([wuq3[:, :, :QK_NOPE_DIM].reshape(Q_LORA_RANK, -1),
                           wuq3[:, :, QK_NOPE_DIM:].reshape(Q_LORA_RANK, -1)], axis=1).astype(BF16)
    wukv3 = w_ukv[l].reshape(KV_LORA_RANK, MLA_HEADS, QK_NOPE_DIM + V_HEAD_DIM)
    wukv = jnp.concatenate([wukv3[:, :, :QK_NOPE_DIM].reshape(KV_LORA_RANK, -1),
                            wukv3[:, :, QK_NOPE_DIM:].reshape(KV_LORA_RANK, -1)], axis=1).astype(BF16)
    woa = w_o[l][:MLA_WIDTH].astype(BF16)
    wos = w_o[l][MLA_WIDTH:].astype(BF16)
    n_r = N_GROUPS + N_EXPERTS
    wr = jnp.pad(jnp.concatenate([w_router_group[l], w_router_expert[l]], axis=1),
                 ((0, 0), (0, LANES - n_r))).astype(BF16)
    br = jnp.pad(jnp.concatenate([b_router_group[l], b_router_expert[l]]), (0, LANES - n_r))[None, :]
    inv_freq = 1.0 / (ROPE_THETA ** (jnp.arange(0, QK_ROPE_DIM, 2, dtype=F32) / QK_ROPE_DIM))
    invf = jnp.tile(inv_freq, 2 * LANES // QK_ROPE_DIM)[None, :]

    cqn, ckvn, kpe, u, vs = _inproj(x2, mod3, wq, wkv, wkpe, wz, q_norm_g[l][None, :], kv_norm_g[l][None, :],
                                    sgu_norm_g[l][None, :], sgu_norm_b[l][None, :])
    q, k, v = _qkv(cqn, ckvn, kpe, positions.reshape(N_TOK, 1), invf, wuq, wukv)
    attn = _attn(q, k, v).reshape(N_TOK, MLA_WIDTH)
    x1, h2, logits = _mixout(x2, mod3, attn, u, vs, w_spatial[l], b_spatial[l].T, woa, wos,
                             ln1_g[l][None, :], ln1_b[l][None, :], wr, br)
    route = _route(logits)
    pos_tab, tile_tab = _plan(route)
    pos_flat = pos_tab[:, :TOP_K].reshape(-1)
    xs = _dispatch(pos_flat, tile_tab[:MOE_TILES, 3], h2)
    tile_cols = [tile_tab[:MOE_TILES, k] for k in (0, 1, 2, 4, 5, 6)]
    ys = _moe(tile_cols, xs, w_gate[l], w_up[l], w_down[l])
    out = _final(pos_flat, x1, mod3, route, ln2_g[l][None, :], ln2_b[l][None, :], ys)
    return out.reshape(BATCH, SEQ, D_MODEL)
```

```python
import functools

import jax
import jax.numpy as jnp
import numpy as np
from jax import lax
from jax.experimental import pallas as pl
from jax.experimental.pallas import tpu as pltpu

D_MODEL = 2048
BATCH = 4
SEQ = 2048
N_TOK = BATCH * SEQ

MLA_HEADS = 8
QK_NOPE_DIM = 128
QK_ROPE_DIM = 64
QK_DIM = QK_NOPE_DIM + QK_ROPE_DIM
V_HEAD_DIM = 128
Q_LORA_RANK = 768
KV_LORA_RANK = 512
ROPE_THETA = 10000.0
MLA_WIDTH = MLA_HEADS * V_HEAD_DIM

SGU_GROUPS = 8
SGU_GROUP_DIM = 128
SGU_CHUNK = 128
SGU_WIDTH = SGU_GROUPS * SGU_GROUP_DIM

N_GROUPS = 4
EXPERTS_PER_GROUP = 8
N_EXPERTS = N_GROUPS * EXPERTS_PER_GROUP
TOP_K = 2
EXPERT_FF = 512

DEEPNORM_ALPHA = 2.0 ** 0.25
EPS = 1e-6
N_MOD = 6
NEG_BIG = -1e30

LANES = 128
SUBLANES = 8
VMEM_LIMIT = 56 * 1024 * 1024

ADA_TN = 1024
TOK_TM = 256
ATT_TQ = 512
ATT_TK = 512
ATT_HEADS = 2
MOE_TM = 256
MOE_TILES = (N_TOK * TOP_K + N_EXPERTS * (MOE_TM - 1)) // MOE_TM + 1
MOE_ROWS = MOE_TILES * MOE_TM
PLAN_TM = 512
DISPATCH_TM = 1024
FINAL_TM = 256
assert MOE_TILES <= LANES

F32 = jnp.float32
BF16 = jnp.bfloat16
U32 = jnp.uint32
HALF_D = D_MODEL // 2


def _cparams(sem):
    return pltpu.CompilerParams(dimension_semantics=sem, vmem_limit_bytes=VMEM_LIMIT)


def _const_spec(shape):
    nd = len(shape)
    return pl.BlockSpec(shape, lambda *_: (0,) * nd, pipeline_mode=pl.Buffered(1))


def _ln_rows(x):
    mu = jnp.mean(x, axis=-1, keepdims=True)
    xc = x - mu
    var = jnp.mean(xc * xc, axis=-1, keepdims=True)
    return xc * lax.rsqrt(var + EPS)


def _rms_rows(x):
    return x * lax.rsqrt(jnp.mean(x * x, axis=-1, keepdims=True) + EPS)


def _pack_halves(x):
    half = x.shape[-1] // 2
    return pltpu.pack_elementwise([x[:, :half], x[:, half:]], packed_dtype=BF16)


def _unpack_halves(w):
    lo = pltpu.unpack_elementwise(w, index=0, packed_dtype=BF16, unpacked_dtype=F32)
    hi = pltpu.unpack_elementwise(w, index=1, packed_dtype=BF16, unpacked_dtype=F32)
    return lo, hi


def _store_token_tiles(ref, w):
    rows = w.shape[0]
    for s in range(SUBLANES):
        ref[pl.ds(s, rows, stride=SUBLANES), :] = w[:, s * LANES:(s + 1) * LANES]


def _load_token_tiles(ref, start_row, rows):
    return jnp.concatenate([ref[pl.ds(start_row * SUBLANES + s, rows, stride=SUBLANES), :]
                            for s in range(SUBLANES)], axis=1)


def _gelu_tanh(x):
    c = np.sqrt(2.0 / np.pi).astype(np.float32)
    return 0.5 * x * (1.0 + jnp.tanh(c * (x + 0.044715 * (x * x * x))))


def _ada_body(c_ref, w_ref, b_ref, o_ref):
    o_ref[...] = jnp.dot(c_ref[...].astype(BF16), w_ref[...].astype(BF16),
                         preferred_element_type=F32) + b_ref[...]


def _ada(c, w, b):
    n = w.shape[1]
    return pl.pallas_call(
        _ada_body,
        grid=(n // ADA_TN,),
        in_specs=[pl.BlockSpec((BATCH, D_MODEL), lambda j: (0, 0)),
                  pl.BlockSpec((D_MODEL, ADA_TN), lambda j: (0, j)),
                  pl.BlockSpec((1, ADA_TN), lambda j: (0, j))],
        out_specs=pl.BlockSpec((BATCH, ADA_TN), lambda j: (0, j)),
        out_shape=jax.ShapeDtypeStruct((BATCH, n), F32),
        compiler_params=_cparams(("parallel",)),
        name="ada",
    )(c, w, b)


def _inproj_body(x_ref, mod_ref, wq_ref, wkv_ref, wkpe_ref, wz_ref, gq_ref, gkv_ref, sg_ref, sb_ref,
                 cq_ref, ckv_ref, kpe_ref, u_ref, vs_ref):
    sh = mod_ref[0, 0:1, :]
    sc = mod_ref[0, 1:2, :]
    h = (_ln_rows(x_ref[...]) * (1.0 + sc) + sh).astype(BF16)
    cq = jnp.dot(h, wq_ref[...], preferred_element_type=F32)
    cq_ref[...] = (_rms_rows(cq) * gq_ref[...]).astype(BF16)
    ckv = jnp.dot(h, wkv_ref[...], preferred_element_type=F32)
    ckv_ref[...] = (_rms_rows(ckv) * gkv_ref[...]).astype(BF16)
    kpe_ref[...] = jnp.dot(h, wkpe_ref[...], preferred_element_type=F32)
    gz = _gelu_tanh(jnp.dot(h, wz_ref[...], preferred_element_type=F32))
    u_ref[...] = gz[:, :SGU_WIDTH]
    vs_ref[...] = (_ln_rows(gz[:, SGU_WIDTH:]) * sg_ref[...] + sb_ref[...]).astype(BF16)


def _inproj(x2, mod3, wq, wkv, wkpe, wz, gq, gkv, sg, sb):
    tm = TOK_TM
    tiles_per_batch = SEQ // tm
    row = lambda w: pl.BlockSpec((tm, w), lambda i: (i, 0))
    return pl.pallas_call(
        _inproj_body,
        grid=(N_TOK // tm,),
        in_specs=[row(D_MODEL),
                  pl.BlockSpec((1, N_MOD, D_MODEL), lambda i: (i // tiles_per_batch, 0, 0)),
                  _const_spec(wq.shape), _const_spec(wkv.shape), _const_spec(wkpe.shape), _const_spec(wz.shape),
                  _const_spec(gq.shape), _const_spec(gkv.shape), _const_spec(sg.shape), _const_spec(sb.shape)],
        out_specs=[row(Q_LORA_RANK), row(KV_LORA_RANK), row(LANES), row(SGU_WIDTH), row(SGU_WIDTH)],
        out_shape=[jax.ShapeDtypeStruct((N_TOK, Q_LORA_RANK), BF16),
                   jax.ShapeDtypeStruct((N_TOK, KV_LORA_RANK), BF16),
                   jax.ShapeDtypeStruct((N_TOK, LANES), F32),
                   jax.ShapeDtypeStruct((N_TOK, SGU_WIDTH), F32),
                   jax.ShapeDtypeStruct((N_TOK, SGU_WIDTH), BF16)],
        compiler_params=_cparams(("parallel",)),
        name="inproj",
    )(x2, mod3, wq, wkv, wkpe, wz, gq, gkv, sg, sb)


def _rope(x, cos, sin):
    w = x.shape[-1]
    lane = lax.broadcasted_iota(jnp.int32, x.shape, 1)
    first_half = (lane % QK_ROPE_DIM) < (QK_ROPE_DIM // 2)
    rot = jnp.where(first_half,
                    -pltpu.roll(x, w - QK_ROPE_DIM // 2, 1),
                    pltpu.roll(x, QK_ROPE_DIM // 2, 1))
    return x * cos + rot * sin


def _qkv_body(cq_ref, ckv_ref, kpe_ref, pos_ref, invf_ref, wuq_ref, wukv_ref, q_ref, k_ref, v_ref):
    ang = pos_ref[...].astype(F32) * invf_ref[...]
    cos1 = jnp.cos(ang)
    sin1 = jnp.sin(ang)
    reps = MLA_HEADS * QK_ROPE_DIM // LANES
    cos = jnp.concatenate([cos1] * reps, axis=1)
    sin = jnp.concatenate([sin1] * reps, axis=1)
    scale = np.float32(QK_DIM ** -0.5)
    q = jnp.dot(cq_ref[...], wuq_ref[...], preferred_element_type=F32) * scale
    q_pe = _rope(q[:, MLA_HEADS * QK_NOPE_DIM:], cos, sin)
    kv = jnp.dot(ckv_ref[...], wukv_ref[...], preferred_element_type=F32)
    k_pe = _rope(kpe_ref[...], cos1, sin1)[:, :QK_ROPE_DIM].astype(BF16)
    for h in range(MLA_HEADS):
        q_ref[0, h, :, 0:QK_NOPE_DIM] = q[:, h * QK_NOPE_DIM:(h + 1) * QK_NOPE_DIM].astype(BF16)
        q_ref[0, h, :, QK_NOPE_DIM:QK_DIM] = q_pe[:, h * QK_ROPE_DIM:(h + 1) * QK_ROPE_DIM].astype(BF16)
        k_ref[0, h, :, 0:QK_NOPE_DIM] = kv[:, h * QK_NOPE_DIM:(h + 1) * QK_NOPE_DIM].astype(BF16)
        k_ref[0, h, :, QK_NOPE_DIM:QK_DIM] = k_pe
        v_ref[0, h, :, :] = kv[:, MLA_WIDTH + h * V_HEAD_DIM:MLA_WIDTH + (h + 1) * V_HEAD_DIM].astype(BF16)


def _qkv(cqn, ckvn, kpe, pos2, invf, wuq, wukv):
    tm = TOK_TM
    tpb = SEQ // tm
    row = lambda w: pl.BlockSpec((tm, w), lambda i: (i, 0))
    head_out = lambda w: pl.BlockSpec((1, MLA_HEADS, tm, w), lambda i: (i // tpb, 0, i % tpb, 0))
    return pl.pallas_call(
        _qkv_body,
        grid=(N_TOK // tm,),
        in_specs=[row(Q_LORA_RANK), row(KV_LORA_RANK), row(LANES), row(1),
                  _const_spec(invf.shape), _const_spec(wuq.shape), _const_spec(wukv.shape)],
        out_specs=[head_out(QK_DIM), head_out(QK_DIM), head_out(V_HEAD_DIM)],
        out_shape=[jax.ShapeDtypeStruct((BATCH, MLA_HEADS, SEQ, QK_DIM), BF16),
                   jax.ShapeDtypeStruct((BATCH, MLA_HEADS, SEQ, QK_DIM), BF16),
                   jax.ShapeDtypeStruct((BATCH, MLA_HEADS, SEQ, V_HEAD_DIM), BF16)],
        compiler_params=_cparams(("parallel",)),
        name="qkv",
    )(cqn, ckvn, kpe, pos2, invf, wuq, wukv)


def _attn_body(q_ref, k_ref, v_ref, o_ref):
    i = pl.program_id(2)

    def step(h, j, carry, masked):
        m, l, acc = carry
        start = pl.multiple_of(j * ATT_TK, ATT_TK)
        k = k_ref[0, h, pl.ds(start, ATT_TK), :]
        v = v_ref[0, h, pl.ds(start, ATT_TK), :]
        s = lax.dot_general(q_ref[0, h], k, (((1,), (1,)), ((), ())), preferred_element_type=F32)
        if masked:
            r = lax.broadcasted_iota(jnp.int32, s.shape, 0)
            c = lax.broadcasted_iota(jnp.int32, s.shape, 1)
            s = jnp.where(c <= r, s, NEG_BIG)
        m_new = jnp.maximum(m, jnp.max(s, axis=-1, keepdims=True))
        p = jnp.exp(s - m_new)
        a = jnp.exp(m - m_new)
        l = a * l + jnp.sum(p, axis=-1, keepdims=True)
        acc = a * acc + jnp.dot(p.astype(BF16), v, preferred_element_type=F32)
        return m_new, l, acc

    def steps(j, carries, masked):
        return tuple(step(h, j, carries[h], masked) for h in range(ATT_HEADS))

    init = tuple((jnp.full((ATT_TQ, 1), NEG_BIG, F32), jnp.zeros((ATT_TQ, 1), F32),
                  jnp.zeros((ATT_TQ, V_HEAD_DIM), F32)) for _ in range(ATT_HEADS))
    carries = lax.fori_loop(0, i, lambda j, c: steps(j, c, False), init)
    carries = steps(i, carries, True)
    for h, (m, l, acc) in enumerate(carries):
        o_ref[0, :, h * V_HEAD_DIM:(h + 1) * V_HEAD_DIM] = (acc / l).astype(BF16)


def _attn(q, k, v):
    assert ATT_TQ == ATT_TK
    hb = ATT_HEADS
    return pl.pallas_call(
        _attn_body,
        grid=(BATCH, MLA_HEADS // hb, SEQ // ATT_TQ),
        in_specs=[pl.BlockSpec((1, hb, ATT_TQ, QK_DIM), lambda b, h, i: (b, h, i, 0)),
                  pl.BlockSpec((1, hb, SEQ, QK_DIM), lambda b, h, i: (b, h, 0, 0)),
                  pl.BlockSpec((1, hb, SEQ, V_HEAD_DIM), lambda b, h, i: (b, h, 0, 0))],
        out_specs=pl.BlockSpec((1, ATT_TQ, hb * V_HEAD_DIM), lambda b, h, i: (b, i, h)),
        out_shape=jax.ShapeDtypeStruct((BATCH, SEQ, MLA_WIDTH), BF16),
        compiler_params=_cparams(("parallel", "parallel", "arbitrary")),
        name="attn",
    )(q, k, v)


def _mixout_body(x_ref, mod_ref, attn_ref, u_ref, vs_ref, wsp_ref, bsp_ref, woa_ref, wos_ref,
                 g1_ref, b1_ref, wr_ref, br_ref, x1_ref, h2_ref, lg_ref, sgu_scr):
    r = lax.broadcasted_iota(jnp.int32, (SGU_CHUNK, SGU_CHUNK), 0)
    c = lax.broadcasted_iota(jnp.int32, (SGU_CHUNK, SGU_CHUNK), 1)
    causal = c <= r
    for g in range(SGU_GROUPS):
        ws = jnp.where(causal, wsp_ref[g], 0.0).astype(BF16)
        bias = bsp_ref[:, g:g + 1]
        cols = slice(g * SGU_GROUP_DIM, (g + 1) * SGU_GROUP_DIM)
        for ch in range(TOK_TM // SGU_CHUNK):
            rows = slice(ch * SGU_CHUNK, (ch + 1) * SGU_CHUNK)
            mixed = jnp.dot(ws, vs_ref[rows, cols], preferred_element_type=F32) + bias
            sgu_scr[rows, cols] = (u_ref[rows, cols] * mixed).astype(BF16)
    y = (jnp.dot(attn_ref[...], woa_ref[...], preferred_element_type=F32)
         + jnp.dot(sgu_scr[...], wos_ref[...], preferred_element_type=F32))
    gate1 = mod_ref[0, 2:3, :]
    sh2 = mod_ref[0, 3:4, :]
    sc2 = mod_ref[0, 4:5, :]
    x1 = _ln_rows(DEEPNORM_ALPHA * x_ref[...] + gate1 * y) * g1_ref[...] + b1_ref[...]
    x1_ref[...] = x1
    h2 = _ln_rows(x1) * (1.0 + sc2) + sh2
    _store_token_tiles(h2_ref, _pack_halves(h2))
    lg_ref[...] = jnp.dot(h2.astype(BF16), wr_ref[...], preferred_element_type=F32) + br_ref[...]


def _mixout(x2, mod3, attn, u, vs, wsp, bsp_t, woa, wos, g1, b1, wr, br):
    tm = TOK_TM
    tpb = SEQ // tm
    row = lambda w: pl.BlockSpec((tm, w), lambda i: (i, 0))
    return pl.pallas_call(
        _mixout_body,
        grid=(N_TOK // tm,),
        in_specs=[row(D_MODEL),
                  pl.BlockSpec((1, N_MOD, D_MODEL), lambda i: (i // tpb, 0, 0)),
                  row(MLA_WIDTH), row(SGU_WIDTH), row(SGU_WIDTH),
                  _const_spec(wsp.shape), _const_spec(bsp_t.shape), _const_spec(woa.shape), _const_spec(wos.shape),
                  _const_spec(g1.shape), _const_spec(b1.shape), _const_spec(wr.shape), _const_spec(br.shape)],
        out_specs=[row(D_MODEL), pl.BlockSpec((tm * SUBLANES, LANES), lambda i: (i, 0)), row(LANES)],
        out_shape=[jax.ShapeDtypeStruct((N_TOK, D_MODEL), F32),
                   jax.ShapeDtypeStruct((N_TOK * SUBLANES, LANES), U32),
                   jax.ShapeDtypeStruct((N_TOK, LANES), F32)],
        scratch_shapes=[pltpu.VMEM((tm, SGU_WIDTH), BF16)],
        compiler_params=_cparams(("parallel",)),
        name="mix_out",
    )(x2, mod3, attn, u, vs, wsp, bsp_t, woa, wos, g1, b1, wr, br)


def _route_body(lg_ref, o_ref):
    lg = lg_ref[...]
    lane = lax.broadcasted_iota(jnp.int32, lg.shape, 1)
    big = jnp.int32(LANES)

    def top1(vals):
        m = jnp.max(vals, axis=-1, keepdims=True)
        idx = jnp.min(jnp.where(vals == m, lane, big), axis=-1, keepdims=True)
        return m, idx

    is_group = lane < N_GROUPS
    glog = jnp.where(is_group, lg, -jnp.inf)
    gmax, gidx = top1(glog)
    pg_top = 1.0 / jnp.sum(jnp.exp(glog - gmax), axis=-1, keepdims=True)
    eid = lane - N_GROUPS
    sel = (eid >= gidx * EXPERTS_PER_GROUP) & (eid < (gidx + 1) * EXPERTS_PER_GROUP)
    elog = jnp.where(sel, lg, -jnp.inf)
    m1, i1 = top1(elog)
    m2, i2 = top1(jnp.where(lane == i1, -jnp.inf, elog))
    e2 = jnp.exp(m2 - m1)
    w1 = pg_top / (1.0 + e2)
    w2 = pg_top * e2 / (1.0 + e2)
    out = jnp.where(lane == 0, (i1 - N_GROUPS).astype(F32),
                    jnp.where(lane == 1, (i2 - N_GROUPS).astype(F32),
                              jnp.where(lane == 2, w1, jnp.where(lane == 3, w2, 0.0))))
    o_ref[...] = out


def _route(logits):
    tm = 1024
    return pl.pallas_call(
        _route_body,
        grid=(N_TOK // tm,),
        in_specs=[pl.BlockSpec((tm, LANES), lambda i: (i, 0))],
        out_specs=pl.BlockSpec((tm, LANES), lambda i: (i, 0)),
        out_shape=jax.ShapeDtypeStruct((N_TOK, LANES), F32),
        compiler_params=_cparams(("parallel",)),
        name="route",
    )(logits)


def _plan_body(rt_ref, pos_ref, tt_ref, rank_scr, cnt_scr):
    ph = pl.program_id(0)
    i = pl.program_id(1)
    t = PLAN_TM
    lane = lax.broadcasted_iota(jnp.int32, (t, LANES), 1)
    rt = rt_ref[...]
    oh0 = lane.astype(F32) == rt[:, 0:1]
    oh1 = lane.astype(F32) == rt[:, 1:2]
    rows = pl.ds(pl.multiple_of(i * t, t), t)

    @pl.when(ph == 0)
    def _():
        @pl.when(i == 0)
        def _():
            cnt_scr[...] = jnp.zeros_like(cnt_scr)

        s = jnp.where(oh0 | oh1, 1.0, 0.0)
        r = lax.broadcasted_iota(jnp.int32, (t, t), 0)
        c = lax.broadcasted_iota(jnp.int32, (t, t), 1)
        before = jnp.where(c < r, 1.0, 0.0).astype(BF16)
        csum = jnp.dot(before, s.astype(BF16), preferred_element_type=F32) + cnt_scr[...]
        rank0 = jnp.sum(jnp.where(oh0, csum, 0.0), axis=-1, keepdims=True)
        rank1 = jnp.sum(jnp.where(oh1, csum, 0.0), axis=-1, keepdims=True)
        rank_scr[rows, :] = jnp.where(lane == 0, rank0, jnp.where(lane == 1, rank1, 0.0))
        cnt_scr[...] += jnp.sum(s, axis=0, keepdims=True)

    @pl.when(ph == 1)
    def _():
        counts = cnt_scr[...]
        tiles = jnp.floor((counts + (MOE_TM - 1)) * (1.0 / MOE_TM))
        r = lax.broadcasted_iota(jnp.int32, (LANES, LANES), 0)
        c = lax.broadcasted_iota(jnp.int32, (LANES, LANES), 1)
        upto = jnp.where(r <= c, 1.0, 0.0).astype(BF16)
        tiles8 = jnp.broadcast_to(tiles, (8, LANES)).astype(BF16)
        tile_end = jnp.dot(tiles8, upto, preferred_element_type=F32)[0:1]
        offs = (tile_end - tiles) * MOE_TM
        rk = rank_scr[rows, :]
        p0 = jnp.sum(jnp.where(oh0, offs, 0.0), axis=-1, keepdims=True) + rk[:, 0:1]
        p1 = jnp.sum(jnp.where(oh1, offs, 0.0), axis=-1, keepdims=True) + rk[:, 1:2]
        pos_ref[...] = (jnp.where(lane == 0, p0, jnp.where(lane == 1, p1, 0.0)) * SUBLANES).astype(jnp.int32)

        lane_e = lax.broadcasted_iota(jnp.int32, (LANES, LANES), 1)
        tile_id = lax.broadcasted_iota(jnp.int32, (LANES, LANES), 0).astype(F32)
        is_e = lane_e < N_EXPERTS
        total = jnp.max(tile_end, axis=-1, keepdims=True)
        t_exp = jnp.sum(jnp.where(is_e & (tile_end <= tile_id), 1.0, 0.0), axis=-1, keepdims=True)
        t_valid = jnp.where(tile_id[:, 0:1] < total, 1.0, 0.0)
        last_exp = jnp.sum(jnp.where(is_e & (tile_end <= total - 1.0), 1.0, 0.0), axis=-1, keepdims=True)
        t_exp = jnp.where(t_valid > 0, t_exp, last_exp)
        t_first = jnp.sum(jnp.where(is_e & (tiles > 0) & ((tile_end - tiles) == tile_id), 1.0, 0.0),
                          axis=-1, keepdims=True)
        t_last = jnp.sum(jnp.where(is_e & (tiles > 0) & ((tile_end - 1.0) == tile_id), 1.0, 0.0),
                         axis=-1, keepdims=True)
        t_clear = jnp.maximum(t_last, 1.0 - t_valid)
        none = jnp.float32(LANES)
        owns = is_e & (tiles > 0)
        lane_f = lane_e.astype(F32)
        t_next = jnp.min(jnp.where(owns & (lane_f > t_exp), lane_f, none), axis=-1, keepdims=True)
        t_next2 = jnp.min(jnp.where(owns & (lane_f > t_next), lane_f, none), axis=-1, keepdims=True)
        t_ord = jnp.sum(jnp.where(owns & (lane_f < t_exp), 1.0, 0.0), axis=-1, keepdims=True)
        cols = (t_exp, t_valid, t_first, t_clear, t_next, t_next2, t_ord)
        table = jnp.zeros((LANES, LANES), F32)
        for k, col in enumerate(cols):
            table = jnp.where(lane_e == k, col, table)
        tt_ref[...] = table.astype(jnp.int32)


def _plan(route):
    t = PLAN_TM
    return pl.pallas_call(
        _plan_body,
        grid=(2, N_TOK // t),
        in_specs=[pl.BlockSpec((t, LANES), lambda ph, i: (i, 0))],
        out_specs=[pl.BlockSpec((t, LANES), lambda ph, i: (i * ph, 0)),
                   pl.BlockSpec((LANES, LANES), lambda ph, i: (0, 0))],
        out_shape=[jax.ShapeDtypeStruct((N_TOK, LANES), jnp.int32),
                   jax.ShapeDtypeStruct((LANES, LANES), jnp.int32)],
        scratch_shapes=[pltpu.VMEM((N_TOK, LANES), F32), pltpu.VMEM((1, LANES), F32)],
        compiler_params=_cparams(("arbitrary", "arbitrary")),
        name="plan",
    )(route)


def _rows_wait(ref, n_rows, sem):
    pltpu.make_async_copy(ref.at[pl.ds(0, n_rows)], ref.at[pl.ds(0, n_rows)], sem).wait()


def _dispatch_body(pos_ref, clear_ref, h_ref, xs_hbm, zbuf, sem_z, sem_s):
    i = pl.program_id(0)
    tile_rows = MOE_TM * SUBLANES

    @pl.when(i == 0)
    def _():
        zbuf[...] = _pack_halves(jnp.zeros((tile_rows, 2 * LANES), F32))

        def zero_copy(tile):
            start = pl.multiple_of(tile * tile_rows, tile_rows)
            return pltpu.make_async_copy(zbuf, xs_hbm.at[pl.ds(start, tile_rows)], sem_z)

        def clear_start(tile, carry):
            @pl.when(clear_ref[tile] > 0)
            def _():
                zero_copy(tile).start()
            return carry

        def clear_wait(tile, carry):
            @pl.when(clear_ref[tile] > 0)
            def _():
                zero_copy(tile).wait()
            return carry

        lax.fori_loop(0, MOE_TILES, clear_start, 0)
        lax.fori_loop(0, MOE_TILES, clear_wait, 0)

    def tok(j, carry):
        src = h_ref.at[pl.ds(pl.multiple_of(j * SUBLANES, SUBLANES), SUBLANES)]
        pair = TOP_K * (i * DISPATCH_TM + j)
        for k in range(TOP_K):
            dst_row = pl.multiple_of(pos_ref[pair + k], SUBLANES)
            pltpu.make_async_copy(src, xs_hbm.at[pl.ds(dst_row, SUBLANES)], sem_s).start()
        return carry

    lax.fori_loop(0, DISPATCH_TM, tok, 0, unroll=8)
    _rows_wait(xs_hbm, TOP_K * DISPATCH_TM * SUBLANES, sem_s)


def _dispatch(pos_rows, tile_clear, h2p):
    grid_spec = pltpu.PrefetchScalarGridSpec(
        num_scalar_prefetch=2,
        grid=(N_TOK // DISPATCH_TM,),
        in_specs=[pl.BlockSpec((DISPATCH_TM * SUBLANES, LANES), lambda i, *_: (i, 0))],
        out_specs=pl.BlockSpec(memory_space=pl.ANY),
        scratch_shapes=[pltpu.VMEM((MOE_TM * SUBLANES, LANES), U32),
                        pltpu.SemaphoreType.DMA(()), pltpu.SemaphoreType.DMA(())],
    )
    return pl.pallas_call(
        _dispatch_body,
        grid_spec=grid_spec,
        out_shape=jax.ShapeDtypeStruct((MOE_ROWS * SUBLANES, LANES), U32),
        compiler_params=_cparams(("arbitrary",)),
        name="dispatch",
    )(pos_rows, tile_clear, h2p)


def _moe_body(te_ref, tv_ref, tf_ref, tn_ref, tn2_ref, to_ref, x_ref, wg_hbm, wu_hbm, wd_hbm, y_ref,
              wg_s, wu_s, wd_s, stg_g, stg_u, stg_d, sem):
    i = pl.program_id(0)

    def fetch(e, slot):
        return (pltpu.make_async_copy(wg_hbm.at[e], stg_g.at[slot], sem.at[slot, 0]),
                pltpu.make_async_copy(wu_hbm.at[e], stg_u.at[slot], sem.at[slot, 1]),
                pltpu.make_async_copy(wd_hbm.at[e], stg_d.at[slot], sem.at[slot, 2]))

    @pl.when(i == 0)
    def _():
        for cp in fetch(te_ref[0], 0):
            cp.start()

        @pl.when(tn_ref[0] < N_EXPERTS)
        def _():
            for cp in fetch(tn_ref[0], 1):
                cp.start()

    @pl.when(tf_ref[i] > 0)
    def _():
        slot = to_ref[i] % 2
        for cp in fetch(te_ref[i], slot):
            cp.wait()
        wg_s[...] = stg_g[slot].astype(BF16)
        wu_s[...] = stg_u[slot].astype(BF16)
        wd_s[...] = stg_d[slot].astype(BF16)

        @pl.when(tn2_ref[i] < N_EXPERTS)
        def _():
            for cp in fetch(tn2_ref[i], slot):
                cp.start()

    @pl.when(tv_ref[i] > 0)
    def _():
        lo, hi = _unpack_halves(_load_token_tiles(x_ref, 0, MOE_TM))
        xa = lo.astype(BF16)
        xb = hi.astype(BF16)
        g = (jnp.dot(xa, wg_s[:HALF_D, :], preferred_element_type=F32)
             + jnp.dot(xb, wg_s[HALF_D:, :], preferred_element_type=F32))
        u = (jnp.dot(xa, wu_s[:HALF_D, :], preferred_element_type=F32)
             + jnp.dot(xb, wu_s[HALF_D:, :], preferred_element_type=F32))
        hid = (g * jax.nn.sigmoid(g) * u).astype(BF16)
        _store_token_tiles(y_ref, _pack_halves(jnp.dot(hid, wd_s[...], preferred_element_type=F32)))

    @pl.when(tv_ref[i] == 0)
    def _():
        y_ref[...] = _pack_halves(jnp.zeros((MOE_TM * SUBLANES, 2 * LANES), F32))


def _moe(tile_cols, xs, wg, wu, wd):
    tm = MOE_TM
    grid_spec = pltpu.PrefetchScalarGridSpec(
        num_scalar_prefetch=len(tile_cols),
        grid=(MOE_TILES,),
        in_specs=[pl.BlockSpec((tm * SUBLANES, LANES), lambda i, *_: (i, 0)),
                  pl.BlockSpec(memory_space=pl.ANY), pl.BlockSpec(memory_space=pl.ANY),
                  pl.BlockSpec(memory_space=pl.ANY)],
        out_specs=pl.BlockSpec((tm * SUBLANES, LANES), lambda i, *_: (i, 0)),
        scratch_shapes=[pltpu.VMEM((D_MODEL, EXPERT_FF), BF16), pltpu.VMEM((D_MODEL, EXPERT_FF), BF16),
                        pltpu.VMEM((EXPERT_FF, D_MODEL), BF16),
                        pltpu.VMEM((2, D_MODEL, EXPERT_FF), F32), pltpu.VMEM((2, D_MODEL, EXPERT_FF), F32),
                        pltpu.VMEM((2, EXPERT_FF, D_MODEL), F32),
                        pltpu.SemaphoreType.DMA((2, 3))],
    )
    return pl.pallas_call(
        _moe_body,
        grid_spec=grid_spec,
        out_shape=jax.ShapeDtypeStruct((MOE_ROWS * SUBLANES, LANES), U32),
        compiler_params=_cparams(("arbitrary",)),
        name="moe",
    )(*tile_cols, xs, wg, wu, wd)


def _final_body(pos_ref, x1_ref, mod_ref, rt_ref, g2_ref, b2_ref, ys_hbm, o_ref, buf, sem):
    i = pl.program_id(0)
    n = pl.num_programs(0)
    tm = FINAL_TM

    def issue(tile, slot):
        def tok(j, carry):
            pair = TOP_K * (tile * tm + j)
            for k in range(TOP_K):
                src_row = pl.multiple_of(pos_ref[pair + k], SUBLANES)
                dst_row = pl.multiple_of((k * tm + j) * SUBLANES, SUBLANES)
                pltpu.make_async_copy(ys_hbm.at[pl.ds(src_row, SUBLANES)],
                                      buf.at[slot, pl.ds(dst_row, SUBLANES)], sem.at[slot]).start()
            return carry

        lax.fori_loop(0, tm, tok, 0, unroll=8)

    @pl.when(i == 0)
    def _():
        issue(0, 0)

    @pl.when(i + 1 < n)
    def _():
        issue(i + 1, (i + 1) % 2)

    slot = i % 2
    pltpu.make_async_copy(ys_hbm.at[pl.ds(0, TOP_K * tm * SUBLANES)], buf.at[slot], sem.at[slot]).wait()
    a_lo, a_hi = _unpack_halves(_load_token_tiles(buf.at[slot], 0, tm))
    b_lo, b_hi = _unpack_halves(_load_token_tiles(buf.at[slot], tm, tm))
    w0 = rt_ref[:, 2:3]
    w1 = rt_ref[:, 3:4]
    y = jnp.concatenate([w0 * a_lo + w1 * b_lo, w0 * a_hi + w1 * b_hi], axis=1)
    gate2 = mod_ref[0, 5:6, :]
    o_ref[...] = _ln_rows(DEEPNORM_ALPHA * x1_ref[...] + gate2 * y) * g2_ref[...] + b2_ref[...]


def _final(pos_rows, x1, mod3, route, g2, b2, ys):
    tm = FINAL_TM
    tpb = SEQ // tm
    row = lambda w: pl.BlockSpec((tm, w), lambda i, *_: (i, 0))
    grid_spec = pltpu.PrefetchScalarGridSpec(
        num_scalar_prefetch=1,
        grid=(N_TOK // tm,),
        in_specs=[row(D_MODEL),
                  pl.BlockSpec((1, N_MOD, D_MODEL), lambda i, *_: (i // tpb, 0, 0)),
                  row(LANES),
                  pl.BlockSpec(g2.shape, lambda i, *_: (0, 0)),
                  pl.BlockSpec(b2.shape, lambda i, *_: (0, 0)),
                  pl.BlockSpec(memory_space=pl.ANY)],
        out_specs=row(D_MODEL),
        scratch_shapes=[pltpu.VMEM((2, TOP_K * tm * SUBLANES, LANES), U32),
                        pltpu.SemaphoreType.DMA((2,))],
    )
    return pl.pallas_call(
        _final_body,
        grid_spec=grid_spec,
        out_shape=jax.ShapeDtypeStruct((N_TOK, D_MODEL), F32),
        compiler_params=_cparams(("arbitrary",)),
        name="final",
    )(pos_rows, x1, mod3, route, g2, b2, ys)


def kernel(x, c, positions, w_ada, b_ada, w_in, q_norm_g, w_uq, kv_norm_g, w_ukv, sgu_norm_g, sgu_norm_b,
           w_spatial, b_spatial, w_o, ln1_g, ln1_b, w_router_group, b_router_group, w_router_expert,
           b_router_expert, w_gate, w_up, w_down, ln2_g, ln2_b):
    l = 0
    x2 = x.reshape(N_TOK, D_MODEL)
    mod3 = _ada(c, w_ada[l], b_ada[l][None, :]).reshape(BATCH, N_MOD, D_MODEL)

    w_in_l = w_in[l]
    o1, o2, o3 = Q_LORA_RANK, Q_LORA_RANK + KV_LORA_RANK, Q_LORA_RANK + KV_LORA_RANK + QK_ROPE_DIM
    wq = w_in_l[:, :o1].astype(BF16)
    wkv = w_in_l[:, o1:o2].astype(BF16)
    wkpe = jnp.pad(w_in_l[:, o2:o3], ((0, 0), (0, LANES - QK_ROPE_DIM))).astype(BF16)
    wz = w_in_l[:, o3:].astype(BF16)
    wuq3 = w_uq[l].reshape(Q_LORA_RANK, MLA_HEADS, QK_DIM)
    wuq = jnp.concatenate([wuq3[:, :, :QK_NOPE_DIM].reshape(Q_LORA_RANK, -1),
                           wuq3[:, :, QK_NOPE_DIM:].reshape(Q_LORA_RANK, -1)], axis=1).astype(BF16)
    wukv3 = w_ukv[l].reshape(KV_LORA_RANK, MLA_HEADS, QK_NOPE_DIM + V_HEAD_DIM)
    wukv = jnp.concatenate([wukv3[:, :, :QK_NOPE_DIM].reshape(KV_LORA_RANK, -1),
                            wukv3[:, :, QK_NOPE_DIM:].reshape(KV_LORA_RANK, -1)], axis=1).astype(BF16)
    woa = w_o[l][:MLA_WIDTH].astype(BF16)
    wos = w_o[l][MLA_WIDTH:].astype(BF16)
    n_r = N_GROUPS + N_EXPERTS
    wr = jnp.pad(jnp.concatenate([w_router_group[l], w_router_expert[l]], axis=1),
                 ((0, 0), (0, LANES - n_r))).astype(BF16)
    br = jnp.pad(jnp.concatenate([b_router_group[l], b_router_expert[l]]), (0, LANES - n_r))[None, :]
    inv_freq = 1.0 / (ROPE_THETA ** (jnp.arange(0, QK_ROPE_DIM, 2, dtype=F32) / QK_ROPE_DIM))
    invf = jnp.tile(inv_freq, 2 * LANES // QK_ROPE_DIM)[None, :]

    cqn, ckvn, kpe, u, vs = _inproj(x2, mod3, wq, wkv, wkpe, wz, q_norm_g[l][None, :], kv_norm_g[l][None, :],
                                    sgu_norm_g[l][None, :], sgu_norm_b[l][None, :])
    q, k, v = _qkv(cqn, ckvn, kpe, positions.reshape(N_TOK, 1), invf, wuq, wukv)
    attn = _attn(q, k, v).reshape(N_TOK, MLA_WIDTH)
    x1, h2, logits = _mixout(x2, mod3, attn, u, vs, w_spatial[l], b_spatial[l].T, woa, wos,
                             ln1_g[l][None, :], ln1_b[l][None, :], wr, br)
    route = _route(logits)
    pos_tab, tile_tab = _plan(route)
    pos_rows = pos_tab[:, 0:TOP_K].reshape(-1)
    xs = _dispatch(pos_rows, tile_tab[:MOE_TILES, 3], h2)
    tile_cols = [tile_tab[:MOE_TILES, k] for k in (0, 1, 2, 4, 5, 6)]
    ys = _moe(tile_cols, xs, w_gate[l], w_up[l], w_down[l])
    out = _final(pos_rows, x1, mod3, route, ln2_g[l][None, :], ln2_b[l][None, :], ys)
    return out.reshape(BATCH, SEQ, D_MODEL)
```

```python
import functools

import jax
import jax.numpy as jnp
import numpy as np
from jax import lax
from jax.experimental import pallas as pl
from jax.experimental.pallas import tpu as pltpu

D_MODEL = 2048
BATCH = 4
SEQ = 2048
N_TOK = BATCH * SEQ

MLA_HEADS = 8
QK_NOPE_DIM = 128
QK_ROPE_DIM = 64
QK_DIM = QK_NOPE_DIM + QK_ROPE_DIM
V_HEAD_DIM = 128
Q_LORA_RANK = 768
KV_LORA_RANK = 512
ROPE_THETA = 10000.0
MLA_WIDTH = MLA_HEADS * V_HEAD_DIM

SGU_GROUPS = 8
SGU_GROUP_DIM = 128
SGU_CHUNK = 128
SGU_WIDTH = SGU_GROUPS * SGU_GROUP_DIM

N_GROUPS = 4
EXPERTS_PER_GROUP = 8
N_EXPERTS = N_GROUPS * EXPERTS_PER_GROUP
TOP_K = 2
EXPERT_FF = 512

DEEPNORM_ALPHA = 2.0 ** 0.25
EPS = 1e-6
N_MOD = 6
NEG_BIG = -1e30

LANES = 128
SUBLANES = 8
VMEM_LIMIT = 56 * 1024 * 1024

ADA_TN = 1024
TOK_TM = 512
ATT_TQ = 512
ATT_TK = 512
ATT_HEADS = 2
MOE_TM = 256
MOE_TILES = (N_TOK * TOP_K + N_EXPERTS * (MOE_TM - 1)) // MOE_TM + 1
MOE_ROWS = MOE_TILES * MOE_TM
PLAN_TM = 512
DISPATCH_TM = 1024
FINAL_TM = 256
assert MOE_TILES <= LANES

F32 = jnp.float32
BF16 = jnp.bfloat16
U32 = jnp.uint32
HALF_D = D_MODEL // 2


def _cparams(sem):
    return pltpu.CompilerParams(dimension_semantics=sem, vmem_limit_bytes=VMEM_LIMIT)


def _const_spec(shape):
    nd = len(shape)
    return pl.BlockSpec(shape, lambda *_: (0,) * nd, pipeline_mode=pl.Buffered(1))


def _ln_rows(x):
    mu = jnp.mean(x, axis=-1, keepdims=True)
    xc = x - mu
    var = jnp.mean(xc * xc, axis=-1, keepdims=True)
    return xc * lax.rsqrt(var + EPS)


def _rms_rows(x):
    return x * lax.rsqrt(jnp.mean(x * x, axis=-1, keepdims=True) + EPS)


def _pack_halves(x):
    half = x.shape[-1] // 2
    return pltpu.pack_elementwise([x[:, :half], x[:, half:]], packed_dtype=BF16)


def _unpack_halves(w):
    lo = pltpu.unpack_elementwise(w, index=0, packed_dtype=BF16, unpacked_dtype=F32)
    hi = pltpu.unpack_elementwise(w, index=1, packed_dtype=BF16, unpacked_dtype=F32)
    return lo, hi


def _store_token_tiles(ref, w):
    rows = w.shape[0]
    for s in range(SUBLANES):
        ref[pl.ds(s, rows, stride=SUBLANES), :] = w[:, s * LANES:(s + 1) * LANES]


def _load_token_tiles(ref, start_row, rows):
    return jnp.concatenate([ref[pl.ds(start_row * SUBLANES + s, rows, stride=SUBLANES), :]
                            for s in range(SUBLANES)], axis=1)


def _gelu_tanh(x):
    c = np.sqrt(2.0 / np.pi).astype(np.float32)
    return 0.5 * x * (1.0 + jnp.tanh(c * (x + 0.044715 * (x * x * x))))


def _ada_body(c_ref, w_ref, b_ref, o_ref):
    o_ref[...] = jnp.dot(c_ref[...].astype(BF16), w_ref[...].astype(BF16),
                         preferred_element_type=F32) + b_ref[...]


def _ada(c, w, b):
    n = w.shape[1]
    return pl.pallas_call(
        _ada_body,
        grid=(n // ADA_TN,),
        in_specs=[pl.BlockSpec((BATCH, D_MODEL), lambda j: (0, 0)),
                  pl.BlockSpec((D_MODEL, ADA_TN), lambda j: (0, j)),
                  pl.BlockSpec((1, ADA_TN), lambda j: (0, j))],
        out_specs=pl.BlockSpec((BATCH, ADA_TN), lambda j: (0, j)),
        out_shape=jax.ShapeDtypeStruct((BATCH, n), F32),
        compiler_params=_cparams(("parallel",)),
        name="ada",
    )(c, w, b)


def _inproj_body(x_ref, mod_ref, wq_ref, wkv_ref, wkpe_ref, wz_ref, gq_ref, gkv_ref, sg_ref, sb_ref,
                 cq_ref, ckv_ref, kpe_ref, u_ref, vs_ref):
    sh = mod_ref[0, 0:1, :]
    sc = mod_ref[0, 1:2, :]
    h = (_ln_rows(x_ref[...]) * (1.0 + sc) + sh).astype(BF16)
    cq = jnp.dot(h, wq_ref[...], preferred_element_type=F32)
    cq_ref[...] = (_rms_rows(cq) * gq_ref[...]).astype(BF16)
    ckv = jnp.dot(h, wkv_ref[...], preferred_element_type=F32)
    ckv_ref[...] = (_rms_rows(ckv) * gkv_ref[...]).astype(BF16)
    kpe_ref[...] = jnp.dot(h, wkpe_ref[...], preferred_element_type=F32)
    gz = _gelu_tanh(jnp.dot(h, wz_ref[...], preferred_element_type=F32))
    u_ref[...] = gz[:, :SGU_WIDTH]
    vs_ref[...] = (_ln_rows(gz[:, SGU_WIDTH:]) * sg_ref[...] + sb_ref[...]).astype(BF16)


def _inproj(x2, mod3, wq, wkv, wkpe, wz, gq, gkv, sg, sb):
    tm = TOK_TM
    tiles_per_batch = SEQ // tm
    row = lambda w: pl.BlockSpec((tm, w), lambda i: (i, 0))
    return pl.pallas_call(
        _inproj_body,
        grid=(N_TOK // tm,),
        in_specs=[row(D_MODEL),
                  pl.BlockSpec((1, N_MOD, D_MODEL), lambda i: (i // tiles_per_batch, 0, 0)),
                  _const_spec(wq.shape), _const_spec(wkv.shape), _const_spec(wkpe.shape), _const_spec(wz.shape),
                  _const_spec(gq.shape), _const_spec(gkv.shape), _const_spec(sg.shape), _const_spec(sb.shape)],
        out_specs=[row(Q_LORA_RANK), row(KV_LORA_RANK), row(LANES), row(SGU_WIDTH), row(SGU_WIDTH)],
        out_shape=[jax.ShapeDtypeStruct((N_TOK, Q_LORA_RANK), BF16),
                   jax.ShapeDtypeStruct((N_TOK, KV_LORA_RANK), BF16),
                   jax.ShapeDtypeStruct((N_TOK, LANES), F32),
                   jax.ShapeDtypeStruct((N_TOK, SGU_WIDTH), F32),
                   jax.ShapeDtypeStruct((N_TOK, SGU_WIDTH), BF16)],
        compiler_params=_cparams(("parallel",)),
        name="inproj",
    )(x2, mod3, wq, wkv, wkpe, wz, gq, gkv, sg, sb)


def _rope(x, cos, sin):
    w = x.shape[-1]
    lane = lax.broadcasted_iota(jnp.int32, x.shape, 1)
    first_half = (lane % QK_ROPE_DIM) < (QK_ROPE_DIM // 2)
    rot = jnp.where(first_half,
                    -pltpu.roll(x, w - QK_ROPE_DIM // 2, 1),
                    pltpu.roll(x, QK_ROPE_DIM // 2, 1))
    return x * cos + rot * sin


def _qkv_body(cq_ref, ckv_ref, kpe_ref, pos_ref, invf_ref, wuq_ref, wukv_ref, q_ref, k_ref, v_ref):
    ang = pos_ref[...].astype(F32) * invf_ref[...]
    cos1 = jnp.cos(ang)
    sin1 = jnp.sin(ang)
    reps = MLA_HEADS * QK_ROPE_DIM // LANES
    cos = jnp.concatenate([cos1] * reps, axis=1)
    sin = jnp.concatenate([sin1] * reps, axis=1)
    scale = np.float32(QK_DIM ** -0.5)
    q = jnp.dot(cq_ref[...], wuq_ref[...], preferred_element_type=F32) * scale
    q_pe = _rope(q[:, MLA_HEADS * QK_NOPE_DIM:], cos, sin)
    kv = jnp.dot(ckv_ref[...], wukv_ref[...], preferred_element_type=F32)
    k_pe = _rope(kpe_ref[...], cos1, sin1)[:, :QK_ROPE_DIM].astype(BF16)
    for h in range(MLA_HEADS):
        q_ref[0, h, :, 0:QK_NOPE_DIM] = q[:, h * QK_NOPE_DIM:(h + 1) * QK_NOPE_DIM].astype(BF16)
        q_ref[0, h, :, QK_NOPE_DIM:QK_DIM] = q_pe[:, h * QK_ROPE_DIM:(h + 1) * QK_ROPE_DIM].astype(BF16)
        k_ref[0, h, :, 0:QK_NOPE_DIM] = kv[:, h * QK_NOPE_DIM:(h + 1) * QK_NOPE_DIM].astype(BF16)
        k_ref[0, h, :, QK_NOPE_DIM:QK_DIM] = k_pe
        v_ref[0, h, :, :] = kv[:, MLA_WIDTH + h * V_HEAD_DIM:MLA_WIDTH + (h + 1) * V_HEAD_DIM].astype(BF16)


def _qkv(cqn, ckvn, kpe, pos2, invf, wuq, wukv):
    tm = TOK_TM
    tpb = SEQ // tm
    row = lambda w: pl.BlockSpec((tm, w), lambda i: (i, 0))
    head_out = lambda w: pl.BlockSpec((1, MLA_HEADS, tm, w), lambda i: (i // tpb, 0, i % tpb, 0))
    return pl.pallas_call(
        _qkv_body,
        grid=(N_TOK // tm,),
        in_specs=[row(Q_LORA_RANK), row(KV_LORA_RANK), row(LANES), row(1),
                  _const_spec(invf.shape), _const_spec(wuq.shape), _const_spec(wukv.shape)],
        out_specs=[head_out(QK_DIM), head_out(QK_DIM), head_out(V_HEAD_DIM)],
        out_shape=[jax.ShapeDtypeStruct((BATCH, MLA_HEADS, SEQ, QK_DIM), BF16),
                   jax.ShapeDtypeStruct((BATCH, MLA_HEADS, SEQ, QK_DIM), BF16),
                   jax.ShapeDtypeStruct((BATCH, MLA_HEADS, SEQ, V_HEAD_DIM), BF16)],
        compiler_params=_cparams(("parallel",)),
        name="qkv",
    )(cqn, ckvn, kpe, pos2, invf, wuq, wukv)


def _attn_body(q_ref, k_ref, v_ref, o_ref):
    i = pl.program_id(2)

    def step(h, j, carry, masked):
        m, l, acc = carry
        start = pl.multiple_of(j * ATT_TK, ATT_TK)
        k = k_ref[0, h, pl.ds(start, ATT_TK), :]
        v = v_ref[0, h, pl.ds(start, ATT_TK), :]
        s = lax.dot_general(q_ref[0, h], k, (((1,), (1,)), ((), ())), preferred_element_type=F32)
        if masked:
            r = lax.broadcasted_iota(jnp.int32, s.shape, 0)
            c = lax.broadcasted_iota(jnp.int32, s.shape, 1)
            s = jnp.where(c <= r, s, NEG_BIG)
        m_new = jnp.maximum(m, jnp.max(s, axis=-1, keepdims=True))
        p = jnp.exp(s - m_new)
        a = jnp.exp(m - m_new)
        l = a * l + jnp.sum(p, axis=-1, keepdims=True)
        acc = a * acc + jnp.dot(p.astype(BF16), v, preferred_element_type=F32)
        return m_new, l, acc

    def steps(j, carries, masked):
        return tuple(step(h, j, carries[h], masked) for h in range(ATT_HEADS))

    init = tuple((jnp.full((ATT_TQ, 1), NEG_BIG, F32), jnp.zeros((ATT_TQ, 1), F32),
                  jnp.zeros((ATT_TQ, V_HEAD_DIM), F32)) for _ in range(ATT_HEADS))
    carries = lax.fori_loop(0, i, lambda j, c: steps(j, c, False), init)
    carries = steps(i, carries, True)
    for h, (m, l, acc) in enumerate(carries):
        o_ref[0, :, h * V_HEAD_DIM:(h + 1) * V_HEAD_DIM] = (acc / l).astype(BF16)


def _attn(q, k, v):
    assert ATT_TQ == ATT_TK
    hb = ATT_HEADS
    return pl.pallas_call(
        _attn_body,
        grid=(BATCH, MLA_HEADS // hb, SEQ // ATT_TQ),
        in_specs=[pl.BlockSpec((1, hb, ATT_TQ, QK_DIM), lambda b, h, i: (b, h, i, 0)),
                  pl.BlockSpec((1, hb, SEQ, QK_DIM), lambda b, h, i: (b, h, 0, 0)),
                  pl.BlockSpec((1, hb, SEQ, V_HEAD_DIM), lambda b, h, i: (b, h, 0, 0))],
        out_specs=pl.BlockSpec((1, ATT_TQ, hb * V_HEAD_DIM), lambda b, h, i: (b, i, h)),
        out_shape=jax.ShapeDtypeStruct((BATCH, SEQ, MLA_WIDTH), BF16),
        compiler_params=_cparams(("parallel", "parallel", "arbitrary")),
        name="attn",
    )(q, k, v)


def _mixout_body(x_ref, mod_ref, attn_ref, u_ref, vs_ref, wsp_ref, bsp_ref, woa_ref, wos_ref,
                 g1_ref, b1_ref, wr_ref, br_ref, x1_ref, h2_ref, lg_ref, sgu_scr):
    r = lax.broadcasted_iota(jnp.int32, (SGU_CHUNK, SGU_CHUNK), 0)
    c = lax.broadcasted_iota(jnp.int32, (SGU_CHUNK, SGU_CHUNK), 1)
    causal = c <= r
    for g in range(SGU_GROUPS):
        ws = jnp.where(causal, wsp_ref[g], 0.0).astype(BF16)
        bias = bsp_ref[:, g:g + 1]
        cols = slice(g * SGU_GROUP_DIM, (g + 1) * SGU_GROUP_DIM)
        for ch in range(TOK_TM // SGU_CHUNK):
            rows = slice(ch * SGU_CHUNK, (ch + 1) * SGU_CHUNK)
            mixed = jnp.dot(ws, vs_ref[rows, cols], preferred_element_type=F32) + bias
            sgu_scr[rows, cols] = (u_ref[rows, cols] * mixed).astype(BF16)
    y = (jnp.dot(attn_ref[...], woa_ref[...], preferred_element_type=F32)
         + jnp.dot(sgu_scr[...], wos_ref[...], preferred_element_type=F32))
    gate1 = mod_ref[0, 2:3, :]
    sh2 = mod_ref[0, 3:4, :]
    sc2 = mod_ref[0, 4:5, :]
    x1 = _ln_rows(DEEPNORM_ALPHA * x_ref[...] + gate1 * y) * g1_ref[...] + b1_ref[...]
    x1_ref[...] = x1
    h2 = _ln_rows(x1) * (1.0 + sc2) + sh2
    _store_token_tiles(h2_ref, _pack_halves(h2))
    lg_ref[...] = jnp.dot(h2.astype(BF16), wr_ref[...], preferred_element_type=F32) + br_ref[...]


def _mixout(x2, mod3, attn, u, vs, wsp, bsp_t, woa, wos, g1, b1, wr, br):
    tm = TOK_TM
    tpb = SEQ // tm
    row = lambda w: pl.BlockSpec((tm, w), lambda i: (i, 0))
    return pl.pallas_call(
        _mixout_body,
        grid=(N_TOK // tm,),
        in_specs=[row(D_MODEL),
                  pl.BlockSpec((1, N_MOD, D_MODEL), lambda i: (i // tpb, 0, 0)),
                  row(MLA_WIDTH), row(SGU_WIDTH), row(SGU_WIDTH),
                  _const_spec(wsp.shape), _const_spec(bsp_t.shape), _const_spec(woa.shape), _const_spec(wos.shape),
                  _const_spec(g1.shape), _const_spec(b1.shape), _const_spec(wr.shape), _const_spec(br.shape)],
        out_specs=[row(D_MODEL), pl.BlockSpec((tm * SUBLANES, LANES), lambda i: (i, 0)), row(LANES)],
        out_shape=[jax.ShapeDtypeStruct((N_TOK, D_MODEL), F32),
                   jax.ShapeDtypeStruct((N_TOK * SUBLANES, LANES), U32),
                   jax.ShapeDtypeStruct((N_TOK, LANES), F32)],
        scratch_shapes=[pltpu.VMEM((tm, SGU_WIDTH), BF16)],
        compiler_params=_cparams(("parallel",)),
        name="mix_out",
    )(x2, mod3, attn, u, vs, wsp, bsp_t, woa, wos, g1, b1, wr, br)


def _route_body(lg_ref, o_ref):
    lg = lg_ref[...]
    lane = lax.broadcasted_iota(jnp.int32, lg.shape, 1)
    big = jnp.int32(LANES)

    def top1(vals):
        m = jnp.max(vals, axis=-1, keepdims=True)
        idx = jnp.min(jnp.where(vals == m, lane, big), axis=-1, keepdims=True)
        return m, idx

    is_group = lane < N_GROUPS
    glog = jnp.where(is_group, lg, -jnp.inf)
    gmax, gidx = top1(glog)
    pg_top = 1.0 / jnp.sum(jnp.exp(glog - gmax), axis=-1, keepdims=True)
    eid = lane - N_GROUPS
    sel = (eid >= gidx * EXPERTS_PER_GROUP) & (eid < (gidx + 1) * EXPERTS_PER_GROUP)
    elog = jnp.where(sel, lg, -jnp.inf)
    m1, i1 = top1(elog)
    m2, i2 = top1(jnp.where(lane == i1, -jnp.inf, elog))
    e2 = jnp.exp(m2 - m1)
    w1 = pg_top / (1.0 + e2)
    w2 = pg_top * e2 / (1.0 + e2)
    out = jnp.where(lane == 0, (i1 - N_GROUPS).astype(F32),
                    jnp.where(lane == 1, (i2 - N_GROUPS).astype(F32),
                              jnp.where(lane == 2, w1, jnp.where(lane == 3, w2, 0.0))))
    o_ref[...] = out


def _route(logits):
    tm = 1024
    return pl.pallas_call(
        _route_body,
        grid=(N_TOK // tm,),
        in_specs=[pl.BlockSpec((tm, LANES), lambda i: (i, 0))],
        out_specs=pl.BlockSpec((tm, LANES), lambda i: (i, 0)),
        out_shape=jax.ShapeDtypeStruct((N_TOK, LANES), F32),
        compiler_params=_cparams(("parallel",)),
        name="route",
    )(logits)


def _plan_body(rt_ref, pos_ref, tt_ref, rank_scr, cnt_scr):
    ph = pl.program_id(0)
    i = pl.program_id(1)
    t = PLAN_TM
    lane = lax.broadcasted_iota(jnp.int32, (t, LANES), 1)
    rt = rt_ref[...]
    oh0 = lane.astype(F32) == rt[:, 0:1]
    oh1 = lane.astype(F32) == rt[:, 1:2]
    rows = pl.ds(pl.multiple_of(i * t, t), t)

    @pl.when(ph == 0)
    def _():
        @pl.when(i == 0)
        def _():
            cnt_scr[...] = jnp.zeros_like(cnt_scr)

        s = jnp.where(oh0 | oh1, 1.0, 0.0)
        r = lax.broadcasted_iota(jnp.int32, (t, t), 0)
        c = lax.broadcasted_iota(jnp.int32, (t, t), 1)
        before = jnp.where(c < r, 1.0, 0.0).astype(BF16)
        csum = jnp.dot(before, s.astype(BF16), preferred_element_type=F32) + cnt_scr[...]
        rank0 = jnp.sum(jnp.where(oh0, csum, 0.0), axis=-1, keepdims=True)
        rank1 = jnp.sum(jnp.where(oh1, csum, 0.0), axis=-1, keepdims=True)
        rank_scr[rows, :] = jnp.where(lane == 0, rank0, jnp.where(lane == 1, rank1, 0.0))
        cnt_scr[...] += jnp.sum(s, axis=0, keepdims=True)

    @pl.when(ph == 1)
    def _():
        counts = cnt_scr[...]
        tiles = jnp.floor((counts + (MOE_TM - 1)) * (1.0 / MOE_TM))
        r = lax.broadcasted_iota(jnp.int32, (LANES, LANES), 0)
        c = lax.broadcasted_iota(jnp.int32, (LANES, LANES), 1)
        upto = jnp.where(r <= c, 1.0, 0.0).astype(BF16)
        tiles8 = jnp.broadcast_to(tiles, (8, LANES)).astype(BF16)
        tile_end = jnp.dot(tiles8, upto, preferred_element_type=F32)[0:1]
        offs = (tile_end - tiles) * MOE_TM
        rk = rank_scr[rows, :]
        p0 = jnp.sum(jnp.where(oh0, offs, 0.0), axis=-1, keepdims=True) + rk[:, 0:1]
        p1 = jnp.sum(jnp.where(oh1, offs, 0.0), axis=-1, keepdims=True) + rk[:, 1:2]
        pos_ref[...] = (jnp.where(lane == 0, p0, jnp.where(lane == 1, p1, 0.0)) * SUBLANES).astype(jnp.int32)

        lane_e = lax.broadcasted_iota(jnp.int32, (LANES, LANES), 1)
        tile_id = lax.broadcasted_iota(jnp.int32, (LANES, LANES), 0).astype(F32)
        is_e = lane_e < N_EXPERTS
        total = jnp.max(tile_end, axis=-1, keepdims=True)
        t_exp = jnp.sum(jnp.where(is_e & (tile_end <= tile_id), 1.0, 0.0), axis=-1, keepdims=True)
        t_valid = jnp.where(tile_id[:, 0:1] < total, 1.0, 0.0)
        last_exp = jnp.sum(jnp.where(is_e & (tile_end <= total - 1.0), 1.0, 0.0), axis=-1, keepdims=True)
        t_exp = jnp.where(t_valid > 0, t_exp, last_exp)
        t_first = jnp.sum(jnp.where(is_e & (tiles > 0) & ((tile_end - tiles) == tile_id), 1.0, 0.0),
                          axis=-1, keepdims=True)
        t_last = jnp.sum(jnp.where(is_e & (tiles > 0) & ((tile_end - 1.0) == tile_id), 1.0, 0.0),
                         axis=-1, keepdims=True)
        t_clear = jnp.maximum(t_last, 1.0 - t_valid)
        none = jnp.float32(LANES)
        owns = is_e & (tiles > 0)
        lane_f = lane_e.astype(F32)
        t_next = jnp.min(jnp.where(owns & (lane_f > t_exp), lane_f, none), axis=-1, keepdims=True)
        t_next2 = jnp.min(jnp.where(owns & (lane_f > t_next), lane_f, none), axis=-1, keepdims=True)
        t_ord = jnp.sum(jnp.where(owns & (lane_f < t_exp), 1.0, 0.0), axis=-1, keepdims=True)
        cols = (t_exp, t_valid, t_first, t_clear, t_next, t_next2, t_ord)
        table = jnp.zeros((LANES, LANES), F32)
        for k, col in enumerate(cols):
            table = jnp.where(lane_e == k, col, table)
        tt_ref[...] = table.astype(jnp.int32)


def _plan(route):
    t = PLAN_TM
    return pl.pallas_call(
        _plan_body,
        grid=(2, N_TOK // t),
        in_specs=[pl.BlockSpec((t, LANES), lambda ph, i: (i, 0))],
        out_specs=[pl.BlockSpec((t, LANES), lambda ph, i: (i * ph, 0)),
                   pl.BlockSpec((LANES, LANES), lambda ph, i: (0, 0))],
        out_shape=[jax.ShapeDtypeStruct((N_TOK, LANES), jnp.int32),
                   jax.ShapeDtypeStruct((LANES, LANES), jnp.int32)],
        scratch_shapes=[pltpu.VMEM((N_TOK, LANES), F32), pltpu.VMEM((1, LANES), F32)],
        compiler_params=_cparams(("arbitrary", "arbitrary")),
        name="plan",
    )(route)


def _rows_wait(ref, n_rows, sem):
    pltpu.make_async_copy(ref.at[pl.ds(0, n_rows)], ref.at[pl.ds(0, n_rows)], sem).wait()


def _dispatch_body(pos_ref, clear_ref, h_ref, xs_hbm, zbuf, sem_z, sem_s):
    i = pl.program_id(0)
    tile_rows = MOE_TM * SUBLANES

    @pl.when(i == 0)
    def _():
        zbuf[...] = _pack_halves(jnp.zeros((tile_rows, 2 * LANES), F32))

        def zero_copy(tile):
            start = pl.multiple_of(tile * tile_rows, tile_rows)
            return pltpu.make_async_copy(zbuf, xs_hbm.at[pl.ds(start, tile_rows)], sem_z)

        def clear_start(tile, carry):
            @pl.when(clear_ref[tile] > 0)
            def _():
                zero_copy(tile).start()
            return carry

        def clear_wait(tile, carry):
            @pl.when(clear_ref[tile] > 0)
            def _():
                zero_copy(tile).wait()
            return carry

        lax.fori_loop(0, MOE_TILES, clear_start, 0)
        lax.fori_loop(0, MOE_TILES, clear_wait, 0)

    def tok(j, carry):
        src = h_ref.at[pl.ds(pl.multiple_of(j * SUBLANES, SUBLANES), SUBLANES)]
        pair = TOP_K * (i * DISPATCH_TM + j)
        for k in range(TOP_K):
            dst_row = pl.multiple_of(pos_ref[pair + k], SUBLANES)
            pltpu.make_async_copy(src, xs_hbm.at[pl.ds(dst_row, SUBLANES)], sem_s).start(priority=k)
        return carry

    lax.fori_loop(0, DISPATCH_TM, tok, 0, unroll=8)
    _rows_wait(xs_hbm, TOP_K * DISPATCH_TM * SUBLANES, sem_s)


def _dispatch(pos_rows, tile_clear, h2p):
    grid_spec = pltpu.PrefetchScalarGridSpec(
        num_scalar_prefetch=2,
        grid=(N_TOK // DISPATCH_TM,),
        in_specs=[pl.BlockSpec((DISPATCH_TM * SUBLANES, LANES), lambda i, *_: (i, 0))],
        out_specs=pl.BlockSpec(memory_space=pl.ANY),
        scratch_shapes=[pltpu.VMEM((MOE_TM * SUBLANES, LANES), U32),
                        pltpu.SemaphoreType.DMA(()), pltpu.SemaphoreType.DMA(())],
    )
    return pl.pallas_call(
        _dispatch_body,
        grid_spec=grid_spec,
        out_shape=jax.ShapeDtypeStruct((MOE_ROWS * SUBLANES, LANES), U32),
        compiler_params=_cparams(("arbitrary",)),
        name="dispatch",
    )(pos_rows, tile_clear, h2p)


def _moe_body(te_ref, tv_ref, tf_ref, tn_ref, tn2_ref, to_ref, x_ref, wg_hbm, wu_hbm, wd_hbm, y_ref,
              wg_s, wu_s, wd_s, stg_g, stg_u, stg_d, sem):
    i = pl.program_id(0)

    def fetch(e, slot):
        return (pltpu.make_async_copy(wg_hbm.at[e], stg_g.at[slot], sem.at[slot, 0]),
                pltpu.make_async_copy(wu_hbm.at[e], stg_u.at[slot], sem.at[slot, 1]),
                pltpu.make_async_copy(wd_hbm.at[e], stg_d.at[slot], sem.at[slot, 2]))

    @pl.when(i == 0)
    def _():
        for cp in fetch(te_ref[0], 0):
            cp.start()

        @pl.when(tn_ref[0] < N_EXPERTS)
        def _():
            for cp in fetch(tn_ref[0], 1):
                cp.start()

    @pl.when(tf_ref[i] > 0)
    def _():
        slot = to_ref[i] % 2
        for cp in fetch(te_ref[i], slot):
            cp.wait()
        wg_s[...] = stg_g[slot].astype(BF16)
        wu_s[...] = stg_u[slot].astype(BF16)
        wd_s[...] = stg_d[slot].astype(BF16)

        @pl.when(tn2_ref[i] < N_EXPERTS)
        def _():
            for cp in fetch(tn2_ref[i], slot):
                cp.start()

    @pl.when(tv_ref[i] > 0)
    def _():
        lo, hi = _unpack_halves(_load_token_tiles(x_ref, 0, MOE_TM))
        xa = lo.astype(BF16)
        xb = hi.astype(BF16)
        g = (jnp.dot(xa, wg_s[:HALF_D, :], preferred_element_type=F32)
             + jnp.dot(xb, wg_s[HALF_D:, :], preferred_element_type=F32))
        u = (jnp.dot(xa, wu_s[:HALF_D, :], preferred_element_type=F32)
             + jnp.dot(xb, wu_s[HALF_D:, :], preferred_element_type=F32))
        hid = (g * jax.nn.sigmoid(g) * u).astype(BF16)
        _store_token_tiles(y_ref, _pack_halves(jnp.dot(hid, wd_s[...], preferred_element_type=F32)))

    @pl.when(tv_ref[i] == 0)
    def _():
        y_ref[...] = _pack_halves(jnp.zeros((MOE_TM * SUBLANES, 2 * LANES), F32))


def _moe(tile_cols, xs, wg, wu, wd):
    tm = MOE_TM
    grid_spec = pltpu.PrefetchScalarGridSpec(
        num_scalar_prefetch=len(tile_cols),
        grid=(MOE_TILES,),
        in_specs=[pl.BlockSpec((tm * SUBLANES, LANES), lambda i, *_: (i, 0)),
                  pl.BlockSpec(memory_space=pl.ANY), pl.BlockSpec(memory_space=pl.ANY),
                  pl.BlockSpec(memory_space=pl.ANY)],
        out_specs=pl.BlockSpec((tm * SUBLANES, LANES), lambda i, *_: (i, 0)),
        scratch_shapes=[pltpu.VMEM((D_MODEL, EXPERT_FF), BF16), pltpu.VMEM((D_MODEL, EXPERT_FF), BF16),
                        pltpu.VMEM((EXPERT_FF, D_MODEL), BF16),
                        pltpu.VMEM((2, D_MODEL, EXPERT_FF), F32), pltpu.VMEM((2, D_MODEL, EXPERT_FF), F32),
                        pltpu.VMEM((2, EXPERT_FF, D_MODEL), F32),
                        pltpu.SemaphoreType.DMA((2, 3))],
    )
    return pl.pallas_call(
        _moe_body,
        grid_spec=grid_spec,
        out_shape=jax.ShapeDtypeStruct((MOE_ROWS * SUBLANES, LANES), U32),
        compiler_params=_cparams(("arbitrary",)),
        name="moe",
    )(*tile_cols, xs, wg, wu, wd)


def _final_body(pos_ref, x1_ref, mod_ref, rt_ref, g2_ref, b2_ref, ys_hbm, o_ref, buf, sem):
    i = pl.program_id(0)
    n = pl.num_programs(0)
    tm = FINAL_TM

    def issue(tile, slot):
        def tok(j, carry):
            pair = TOP_K * (tile * tm + j)
            for k in range(TOP_K):
                src_row = pl.multiple_of(pos_ref[pair + k], SUBLANES)
                dst_row = pl.multiple_of((k * tm + j) * SUBLANES, SUBLANES)
                pltpu.make_async_copy(ys_hbm.at[pl.ds(src_row, SUBLANES)],
                                      buf.at[slot, pl.ds(dst_row, SUBLANES)], sem.at[slot]).start(priority=k)
            return carry

        lax.fori_loop(0, tm, tok, 0, unroll=8)

    @pl.when(i == 0)
    def _():
        issue(0, 0)

    @pl.when(i + 1 < n)
    def _():
        issue(i + 1, (i + 1) % 2)

    slot = i % 2
    pltpu.make_async_copy(ys_hbm.at[pl.ds(0, TOP_K * tm * SUBLANES)], buf.at[slot], sem.at[slot]).wait()
    a_lo, a_hi = _unpack_halves(_load_token_tiles(buf.at[slot], 0, tm))
    b_lo, b_hi = _unpack_halves(_load_token_tiles(buf.at[slot], tm, tm))
    w0 = rt_ref[:, 2:3]
    w1 = rt_ref[:, 3:4]
    y = jnp.concatenate([w0 * a_lo + w1 * b_lo, w0 * a_hi + w1 * b_hi], axis=1)
    gate2 = mod_ref[0, 5:6, :]
    o_ref[...] = _ln_rows(DEEPNORM_ALPHA * x1_ref[...] + gate2 * y) * g2_ref[...] + b2_ref[...]


def _final(pos_rows, x1, mod3, route, g2, b2, ys):
    tm = FINAL_TM
    tpb = SEQ // tm
    row = lambda w: pl.BlockSpec((tm, w), lambda i, *_: (i, 0))
    grid_spec = pltpu.PrefetchScalarGridSpec(
        num_scalar_prefetch=1,
        grid=(N_TOK // tm,),
        in_specs=[row(D_MODEL),
                  pl.BlockSpec((1, N_MOD, D_MODEL), lambda i, *_: (i // tpb, 0, 0)),
                  row(LANES),
                  pl.BlockSpec(g2.shape, lambda i, *_: (0, 0)),
                  pl.BlockSpec(b2.shape, lambda i, *_: (0, 0)),
                  pl.BlockSpec(memory_space=pl.ANY)],
        out_specs=row(D_MODEL),
        scratch_shapes=[pltpu.VMEM((2, TOP_K * tm * SUBLANES, LANES), U32),
                        pltpu.SemaphoreType.DMA((2,))],
    )
    return pl.pallas_call(
        _final_body,
        grid_spec=grid_spec,
        out_shape=jax.ShapeDtypeStruct((N_TOK, D_MODEL), F32),
        compiler_params=_cparams(("arbitrary",)),
        name="final",
    )(pos_rows, x1, mod3, route, g2, b2, ys)


def kernel(x, c, positions, w_ada, b_ada, w_in, q_norm_g, w_uq, kv_norm_g, w_ukv, sgu_norm_g, sgu_norm_b,
           w_spatial, b_spatial, w_o, ln1_g, ln1_b, w_router_group, b_router_group, w_router_expert,
           b_router_expert, w_gate, w_up, w_down, ln2_g, ln2_b):
    l = 0
    x2 = x.reshape(N_TOK, D_MODEL)
    mod3 = _ada(c, w_ada[l], b_ada[l][None, :]).reshape(BATCH, N_MOD, D_MODEL)

    w_in_l = w_in[l]
    o1, o2, o3 = Q_LORA_RANK, Q_LORA_RANK + KV_LORA_RANK, Q_LORA_RANK + KV_LORA_RANK + QK_ROPE_DIM
    wq = w_in_l[:, :o1].astype(BF16)
    wkv = w_in_l[:, o1:o2].astype(BF16)
    wkpe = jnp.pad(w_in_l[:, o2:o3], ((0, 0), (0, LANES - QK_ROPE_DIM))).astype(BF16)
    wz = w_in_l[:, o3:].astype(BF16)
    wuq3 = w_uq[l].reshape(Q_LORA_RANK, MLA_HEADS, QK_DIM)
    wuq = jnp.concatenate([wuq3[:, :, :QK_NOPE_DIM].reshape(Q_LORA_RANK, -1),
                           wuq3[:, :, QK_NOPE_DIM:].reshape(Q_LORA_RANK, -1)], axis=1).astype(BF16)
    wukv3 = w_ukv[l].reshape(KV_LORA_RANK, MLA_HEADS, QK_NOPE_DIM + V_HEAD_DIM)
    wukv = jnp.concatenate([wukv3[:, :, :QK_NOPE_DIM].reshape(KV_LORA_RANK, -1),
                            wukv3[:, :, QK_NOPE_DIM:].reshape(KV_LORA_RANK, -1)], axis=1).astype(BF16)
    woa = w_o[l][:MLA_WIDTH].astype(BF16)
    wos = w_o[l][MLA_WIDTH:].astype(BF16)
    n_r = N_GROUPS + N_EXPERTS
    wr = jnp.pad(jnp.concatenate([w_router_group[l], w_router_expert[l]], axis=1),
                 ((0, 0), (0, LANES - n_r))).astype(BF16)
    br = jnp.pad(jnp.concatenate([b_router_group[l], b_router_expert[l]]), (0, LANES - n_r))[None, :]
    inv_freq = 1.0 / (ROPE_THETA ** (jnp.arange(0, QK_ROPE_DIM, 2, dtype=F32) / QK_ROPE_DIM))
    invf = jnp.tile(inv_freq, 2 * LANES // QK_ROPE_DIM)[None, :]

    cqn, ckvn, kpe, u, vs = _inproj(x2, mod3, wq, wkv, wkpe, wz, q_norm_g[l][None, :], kv_norm_g[l][None, :],
                                    sgu_norm_g[l][None, :], sgu_norm_b[l][None, :])
    q, k, v = _qkv(cqn, ckvn, kpe, positions.reshape(N_TOK, 1), invf, wuq, wukv)
    attn = _attn(q, k, v).reshape(N_TOK, MLA_WIDTH)
    x1, h2, logits = _mixout(x2, mod3, attn, u, vs, w_spatial[l], b_spatial[l].T, woa, wos,
                             ln1_g[l][None, :], ln1_b[l][None, :], wr, br)
    route = _route(logits)
    pos_tab, tile_tab = _plan(route)
    pos_rows = pos_tab[:, 0:TOP_K].reshape(-1)
    xs = _dispatch(pos_rows, tile_tab[:MOE_TILES, 3], h2)
    tile_cols = [tile_tab[:MOE_TILES, k] for k in (0, 1, 2, 4, 5, 6)]
    ys = _moe(tile_cols, xs, w_gate[l], w_up[l], w_down[l])
    out = _final(pos_rows, x1, mod3, route, ln2_g[l][None, :], ln2_b[l][None, :], ys)
    return out.reshape(BATCH, SEQ, D_MODEL)
```

```python
import functools

import jax
import jax.numpy as jnp
import numpy as np
from jax import lax
from jax.experimental import pallas as pl
from jax.experimental.pallas import tpu as pltpu

D_MODEL = 2048
BATCH = 4
SEQ = 2048
N_TOK = BATCH * SEQ

MLA_HEADS = 8
QK_NOPE_DIM = 128
QK_ROPE_DIM = 64
QK_DIM = QK_NOPE_DIM + QK_ROPE_DIM
V_HEAD_DIM = 128
Q_LORA_RANK = 768
KV_LORA_RANK = 512
ROPE_THETA = 10000.0
MLA_WIDTH = MLA_HEADS * V_HEAD_DIM

SGU_GROUPS = 8
SGU_GROUP_DIM = 128
SGU_CHUNK = 128
SGU_WIDTH = SGU_GROUPS * SGU_GROUP_DIM

N_GROUPS = 4
EXPERTS_PER_GROUP = 8
N_EXPERTS = N_GROUPS * EXPERTS_PER_GROUP
TOP_K = 2
EXPERT_FF = 512

DEEPNORM_ALPHA = 2.0 ** 0.25
EPS = 1e-6
N_MOD = 6
NEG_BIG = -1e30

LANES = 128
SUBLANES = 8
VMEM_LIMIT = 56 * 1024 * 1024

ADA_TN = 1024
TOK_TM = 512
MIX_TM = 256
PREP_STEPS = 8
ATT_TQ = 512
ATT_TK = 512
ATT_HEADS = 2
MOE_TM = 256
MOE_TILES = (N_TOK * TOP_K + N_EXPERTS * (MOE_TM - 1)) // MOE_TM + 1
MOE_ROWS = MOE_TILES * MOE_TM
PLAN_TM = 512
DISPATCH_TM = 1024
FINAL_TM = 256
assert MOE_TILES <= LANES
T_EXP, T_VALID, T_FIRST, T_CLEAR, T_NEXT, T_NEXT2, T_ORD = range(7)

F32 = jnp.float32
BF16 = jnp.bfloat16
U32 = jnp.uint32
HALF_D = D_MODEL // 2


def _cparams(sem):
    return pltpu.CompilerParams(dimension_semantics=sem, vmem_limit_bytes=VMEM_LIMIT)


def _const_spec(shape):
    nd = len(shape)
    return pl.BlockSpec(shape, lambda *_: (0,) * nd, pipeline_mode=pl.Buffered(1))


def _ln_rows(x):
    mu = jnp.mean(x, axis=-1, keepdims=True)
    xc = x - mu
    var = jnp.mean(xc * xc, axis=-1, keepdims=True)
    return xc * lax.rsqrt(var + EPS)


def _rms_rows(x):
    return x * lax.rsqrt(jnp.mean(x * x, axis=-1, keepdims=True) + EPS)


def _pack_halves(x):
    half = x.shape[-1] // 2
    return pltpu.pack_elementwise([x[:, :half], x[:, half:]], packed_dtype=BF16)


def _unpack_halves(w):
    lo = pltpu.unpack_elementwise(w, index=0, packed_dtype=BF16, unpacked_dtype=F32)
    hi = pltpu.unpack_elementwise(w, index=1, packed_dtype=BF16, unpacked_dtype=F32)
    return lo, hi


def _store_token_tiles(ref, w):
    rows = w.shape[0]
    for s in range(SUBLANES):
        ref[pl.ds(s, rows, stride=SUBLANES), :] = w[:, s * LANES:(s + 1) * LANES]


def _load_token_tiles(ref, start_row, rows):
    return jnp.concatenate([ref[pl.ds(start_row * SUBLANES + s, rows, stride=SUBLANES), :]
                            for s in range(SUBLANES)], axis=1)


def _gelu_tanh(x):
    c = np.sqrt(2.0 / np.pi).astype(np.float32)
    return 0.5 * x * (1.0 + jnp.tanh(c * (x + 0.044715 * (x * x * x))))


def _ada_body(c_ref, w_ref, b_ref, o_ref):
    o_ref[...] = jnp.dot(c_ref[...].astype(BF16), w_ref[...].astype(BF16),
                         preferred_element_type=F32) + b_ref[...]


def _ada(c, w, b):
    n = w.shape[1]
    return pl.pallas_call(
        _ada_body,
        grid=(n // ADA_TN,),
        in_specs=[pl.BlockSpec((BATCH, D_MODEL), lambda j: (0, 0)),
                  pl.BlockSpec((D_MODEL, ADA_TN), lambda j: (0, j)),
                  pl.BlockSpec((1, ADA_TN), lambda j: (0, j))],
        out_specs=pl.BlockSpec((BATCH, ADA_TN), lambda j: (0, j)),
        out_shape=jax.ShapeDtypeStruct((BATCH, n), F32),
        compiler_params=_cparams(("parallel",)),
        name="ada",
    )(c, w, b)


def _prep_body(win_ref, wuq_ref, wukv_ref, wo_ref, wq_o, wkv_o, wkpe_o, wz_o, wuq_o, wukv_o, wo_o):
    o1, o2, o3 = Q_LORA_RANK, Q_LORA_RANK + KV_LORA_RANK, Q_LORA_RANK + KV_LORA_RANK + QK_ROPE_DIM
    w = win_ref[...]
    wq_o[...] = w[:, :o1].astype(BF16)
    wkv_o[...] = w[:, o1:o2].astype(BF16)
    kpe = w[:, o2:o3]
    wkpe_o[...] = jnp.concatenate([kpe, jnp.zeros_like(kpe)], axis=1).astype(BF16)
    wz_o[...] = w[:, o3:].astype(BF16)
    u = wuq_ref[...]
    nope = [u[:, h * QK_DIM:h * QK_DIM + QK_NOPE_DIM] for h in range(MLA_HEADS)]
    rope = [u[:, h * QK_DIM + QK_NOPE_DIM:(h + 1) * QK_DIM] for h in range(MLA_HEADS)]
    wuq_o[...] = jnp.concatenate(nope + rope, axis=1).astype(BF16)
    kv = wukv_ref[...]
    hw = QK_NOPE_DIM + V_HEAD_DIM
    kn = [kv[:, h * hw:h * hw + QK_NOPE_DIM] for h in range(MLA_HEADS)]
    vv = [kv[:, h * hw + QK_NOPE_DIM:(h + 1) * hw] for h in range(MLA_HEADS)]
    wukv_o[...] = jnp.concatenate(kn + vv, axis=1).astype(BF16)
    wo_o[...] = wo_ref[...].astype(BF16)


def _prep(w_in, w_uq, w_ukv, w_o):
    steps = PREP_STEPS
    zw = w_in.shape[1] - (Q_LORA_RANK + KV_LORA_RANK + QK_ROPE_DIM)
    blk = lambda a: pl.BlockSpec((a.shape[0] // steps, a.shape[1]), lambda i: (i, 0))
    out_cols = (Q_LORA_RANK, KV_LORA_RANK, LANES, zw)
    outs = ([(D_MODEL, c) for c in out_cols] + [w_uq.shape, w_ukv.shape, w_o.shape])
    return pl.pallas_call(
        _prep_body,
        grid=(steps,),
        in_specs=[blk(w_in), blk(w_uq), blk(w_ukv), blk(w_o)],
        out_specs=[pl.BlockSpec((r // steps, c), lambda i: (i, 0)) for r, c in outs],
        out_shape=[jax.ShapeDtypeStruct(s, BF16) for s in outs],
        compiler_params=_cparams(("parallel",)),
        name="prep",
    )(w_in, w_uq, w_ukv, w_o)


def _inproj_body(x_ref, mod_ref, wq_ref, wkv_ref, wkpe_ref, wz_ref, gq_ref, gkv_ref, sg_ref, sb_ref,
                 cq_ref, ckv_ref, kpe_ref, u_ref, vs_ref):
    sh = mod_ref[0, 0:1, :]
    sc = mod_ref[0, 1:2, :]
    h = (_ln_rows(x_ref[...]) * (1.0 + sc) + sh).astype(BF16)
    cq = jnp.dot(h, wq_ref[...], preferred_element_type=F32)
    cq_ref[...] = (_rms_rows(cq) * gq_ref[...]).astype(BF16)
    ckv = jnp.dot(h, wkv_ref[...], preferred_element_type=F32)
    ckv_ref[...] = (_rms_rows(ckv) * gkv_ref[...]).astype(BF16)
    kpe_ref[...] = jnp.dot(h, wkpe_ref[...], preferred_element_type=F32)
    gz = _gelu_tanh(jnp.dot(h, wz_ref[...], preferred_element_type=F32))
    u_ref[...] = gz[:, :SGU_WIDTH]
    vs_ref[...] = (_ln_rows(gz[:, SGU_WIDTH:]) * sg_ref[...] + sb_ref[...]).astype(BF16)


def _inproj(x2, mod3, wq, wkv, wkpe, wz, gq, gkv, sg, sb):
    tm = TOK_TM
    tiles_per_batch = SEQ // tm
    row = lambda w: pl.BlockSpec((tm, w), lambda i: (i, 0))
    return pl.pallas_call(
        _inproj_body,
        grid=(N_TOK // tm,),
        in_specs=[row(D_MODEL),
                  pl.BlockSpec((1, N_MOD, D_MODEL), lambda i: (i // tiles_per_batch, 0, 0)),
                  _const_spec(wq.shape), _const_spec(wkv.shape), _const_spec(wkpe.shape), _const_spec(wz.shape),
                  _const_spec(gq.shape), _const_spec(gkv.shape), _const_spec(sg.shape), _const_spec(sb.shape)],
        out_specs=[row(Q_LORA_RANK), row(KV_LORA_RANK), row(LANES), row(SGU_WIDTH), row(SGU_WIDTH)],
        out_shape=[jax.ShapeDtypeStruct((N_TOK, Q_LORA_RANK), BF16),
                   jax.ShapeDtypeStruct((N_TOK, KV_LORA_RANK), BF16),
                   jax.ShapeDtypeStruct((N_TOK, LANES), F32),
                   jax.ShapeDtypeStruct((N_TOK, SGU_WIDTH), F32),
                   jax.ShapeDtypeStruct((N_TOK, SGU_WIDTH), BF16)],
        compiler_params=_cparams(("parallel",)),
        name="inproj",
    )(x2, mod3, wq, wkv, wkpe, wz, gq, gkv, sg, sb)


def _rope(x, cos, sin):
    w = x.shape[-1]
    lane = lax.broadcasted_iota(jnp.int32, x.shape, 1)
    first_half = (lane % QK_ROPE_DIM) < (QK_ROPE_DIM // 2)
    rot = jnp.where(first_half,
                    -pltpu.roll(x, w - QK_ROPE_DIM // 2, 1),
                    pltpu.roll(x, QK_ROPE_DIM // 2, 1))
    return x * cos + rot * sin


def _qkv_body(cq_ref, ckv_ref, kpe_ref, pos_ref, invf_ref, wuq_ref, wukv_ref, q_ref, k_ref, v_ref):
    ang = pos_ref[...].astype(F32) * invf_ref[...]
    cos1 = jnp.cos(ang)
    sin1 = jnp.sin(ang)
    reps = MLA_HEADS * QK_ROPE_DIM // LANES
    cos = jnp.concatenate([cos1] * reps, axis=1)
    sin = jnp.concatenate([sin1] * reps, axis=1)
    scale = np.float32(QK_DIM ** -0.5)
    q = jnp.dot(cq_ref[...], wuq_ref[...], preferred_element_type=F32) * scale
    q_pe = _rope(q[:, MLA_HEADS * QK_NOPE_DIM:], cos, sin)
    kv = jnp.dot(ckv_ref[...], wukv_ref[...], preferred_element_type=F32)
    k_pe = _rope(kpe_ref[...], cos1, sin1)[:, :QK_ROPE_DIM].astype(BF16)
    for h in range(MLA_HEADS):
        q_ref[0, h, :, 0:QK_NOPE_DIM] = q[:, h * QK_NOPE_DIM:(h + 1) * QK_NOPE_DIM].astype(BF16)
        q_ref[0, h, :, QK_NOPE_DIM:QK_DIM] = q_pe[:, h * QK_ROPE_DIM:(h + 1) * QK_ROPE_DIM].astype(BF16)
        k_ref[0, h, :, 0:QK_NOPE_DIM] = kv[:, h * QK_NOPE_DIM:(h + 1) * QK_NOPE_DIM].astype(BF16)
        k_ref[0, h, :, QK_NOPE_DIM:QK_DIM] = k_pe
        v_ref[0, h, :, :] = kv[:, MLA_WIDTH + h * V_HEAD_DIM:MLA_WIDTH + (h + 1) * V_HEAD_DIM].astype(BF16)


def _qkv(cqn, ckvn, kpe, pos2, invf, wuq, wukv):
    tm = TOK_TM
    tpb = SEQ // tm
    row = lambda w: pl.BlockSpec((tm, w), lambda i: (i, 0))
    head_out = lambda w: pl.BlockSpec((1, MLA_HEADS, tm, w), lambda i: (i // tpb, 0, i % tpb, 0))
    return pl.pallas_call(
        _qkv_body,
        grid=(N_TOK // tm,),
        in_specs=[row(Q_LORA_RANK), row(KV_LORA_RANK), row(LANES), row(1),
                  _const_spec(invf.shape), _const_spec(wuq.shape), _const_spec(wukv.shape)],
        out_specs=[head_out(QK_DIM), head_out(QK_DIM), head_out(V_HEAD_DIM)],
        out_shape=[jax.ShapeDtypeStruct((BATCH, MLA_HEADS, SEQ, QK_DIM), BF16),
                   jax.ShapeDtypeStruct((BATCH, MLA_HEADS, SEQ, QK_DIM), BF16),
                   jax.ShapeDtypeStruct((BATCH, MLA_HEADS, SEQ, V_HEAD_DIM), BF16)],
        compiler_params=_cparams(("parallel",)),
        name="qkv",
    )(cqn, ckvn, kpe, pos2, invf, wuq, wukv)


def _attn_body(q_ref, k_ref, v_ref, o_ref):
    i = pl.program_id(2)

    def step(h, j, carry, masked):
        m, l, acc = carry
        start = pl.multiple_of(j * ATT_TK, ATT_TK)
        k = k_ref[0, h, pl.ds(start, ATT_TK), :]
        v = v_ref[0, h, pl.ds(start, ATT_TK), :]
        s = lax.dot_general(q_ref[0, h], k, (((1,), (1,)), ((), ())), preferred_element_type=F32)
        if masked:
            r = lax.broadcasted_iota(jnp.int32, s.shape, 0)
            c = lax.broadcasted_iota(jnp.int32, s.shape, 1)
            s = jnp.where(c <= r, s, NEG_BIG)
        m_new = jnp.maximum(m, jnp.max(s, axis=-1, keepdims=True))
        p = jnp.exp(s - m_new)
        a = jnp.exp(m - m_new)
        l = a * l + jnp.sum(p, axis=-1, keepdims=True)
        acc = a * acc + jnp.dot(p.astype(BF16), v, preferred_element_type=F32)
        return m_new, l, acc

    def steps(j, carries, masked):
        return tuple(step(h, j, carries[h], masked) for h in range(ATT_HEADS))

    init = tuple((jnp.full((ATT_TQ, 1), NEG_BIG, F32), jnp.zeros((ATT_TQ, 1), F32),
                  jnp.zeros((ATT_TQ, V_HEAD_DIM), F32)) for _ in range(ATT_HEADS))
    carries = lax.fori_loop(0, i, lambda j, c: steps(j, c, False), init)
    carries = steps(i, carries, True)
    for h, (m, l, acc) in enumerate(carries):
        o_ref[0, :, h * V_HEAD_DIM:(h + 1) * V_HEAD_DIM] = (acc / l).astype(BF16)


def _attn(q, k, v):
    assert ATT_TQ == ATT_TK
    hb = ATT_HEADS
    return pl.pallas_call(
        _attn_body,
        grid=(BATCH, MLA_HEADS // hb, SEQ // ATT_TQ),
        in_specs=[pl.BlockSpec((1, hb, ATT_TQ, QK_DIM), lambda b, h, i: (b, h, i, 0)),
                  pl.BlockSpec((1, hb, SEQ, QK_DIM), lambda b, h, i: (b, h, 0, 0)),
                  pl.BlockSpec((1, hb, SEQ, V_HEAD_DIM), lambda b, h, i: (b, h, 0, 0))],
        out_specs=pl.BlockSpec((1, ATT_TQ, hb * V_HEAD_DIM), lambda b, h, i: (b, i, h)),
        out_shape=jax.ShapeDtypeStruct((BATCH, SEQ, MLA_WIDTH), BF16),
        compiler_params=_cparams(("parallel", "parallel", "arbitrary")),
        name="attn",
    )(q, k, v)


def _mixout_body(x_ref, mod_ref, attn_ref, u_ref, vs_ref, wsp_ref, bsp_ref, woa_ref, wos_ref,
                 g1_ref, b1_ref, wr_ref, br_ref, x1_ref, h2_ref, lg_ref, sgu_scr):
    r = lax.broadcasted_iota(jnp.int32, (SGU_CHUNK, SGU_CHUNK), 0)
    c = lax.broadcasted_iota(jnp.int32, (SGU_CHUNK, SGU_CHUNK), 1)
    causal = c <= r
    for g in range(SGU_GROUPS):
        ws = jnp.where(causal, wsp_ref[g], 0.0).astype(BF16)
        bias = bsp_ref[:, g:g + 1]
        cols = slice(g * SGU_GROUP_DIM, (g + 1) * SGU_GROUP_DIM)
        for ch in range(MIX_TM // SGU_CHUNK):
            rows = slice(ch * SGU_CHUNK, (ch + 1) * SGU_CHUNK)
            mixed = jnp.dot(ws, vs_ref[rows, cols], preferred_element_type=F32) + bias
            sgu_scr[rows, cols] = (u_ref[rows, cols] * mixed).astype(BF16)
    y = (jnp.dot(attn_ref[...], woa_ref[...], preferred_element_type=F32)
         + jnp.dot(sgu_scr[...], wos_ref[...], preferred_element_type=F32))
    gate1 = mod_ref[0, 2:3, :]
    sh2 = mod_ref[0, 3:4, :]
    sc2 = mod_ref[0, 4:5, :]
    x1 = _ln_rows(DEEPNORM_ALPHA * x_ref[...] + gate1 * y) * g1_ref[...] + b1_ref[...]
    x1_ref[...] = x1
    h2 = _ln_rows(x1) * (1.0 + sc2) + sh2
    _store_token_tiles(h2_ref, _pack_halves(h2))
    lg_ref[...] = jnp.dot(h2.astype(BF16), wr_ref[...], preferred_element_type=F32) + br_ref[...]


def _mixout(x2, mod3, attn, u, vs, wsp, bsp_t, wo, g1, b1, wr, br):
    tm = MIX_TM
    tpb = SEQ // tm
    row = lambda w: pl.BlockSpec((tm, w), lambda i: (i, 0))
    wo_half = lambda j: pl.BlockSpec((MLA_WIDTH, D_MODEL), lambda i: (j, 0), pipeline_mode=pl.Buffered(1))
    return pl.pallas_call(
        _mixout_body,
        grid=(N_TOK // tm,),
        in_specs=[row(D_MODEL),
                  pl.BlockSpec((1, N_MOD, D_MODEL), lambda i: (i // tpb, 0, 0)),
                  row(MLA_WIDTH), row(SGU_WIDTH), row(SGU_WIDTH),
                  _const_spec(wsp.shape), _const_spec(bsp_t.shape), wo_half(0), wo_half(1),
                  _const_spec(g1.shape), _const_spec(b1.shape), _const_spec(wr.shape), _const_spec(br.shape)],
        out_specs=[row(D_MODEL), pl.BlockSpec((tm * SUBLANES, LANES), lambda i: (i, 0)), row(LANES)],
        out_shape=[jax.ShapeDtypeStruct((N_TOK, D_MODEL), F32),
                   jax.ShapeDtypeStruct((N_TOK * SUBLANES, LANES), U32),
                   jax.ShapeDtypeStruct((N_TOK, LANES), F32)],
        scratch_shapes=[pltpu.VMEM((tm, SGU_WIDTH), BF16)],
        compiler_params=_cparams(("parallel",)),
        name="mix_out",
    )(x2, mod3, attn, u, vs, wsp, bsp_t, wo, wo, g1, b1, wr, br)


def _route_body(lg_ref, o_ref):
    lg = lg_ref[...]
    lane = lax.broadcasted_iota(jnp.int32, lg.shape, 1)
    big = jnp.int32(LANES)

    def top1(vals):
        m = jnp.max(vals, axis=-1, keepdims=True)
        idx = jnp.min(jnp.where(vals == m, lane, big), axis=-1, keepdims=True)
        return m, idx

    is_group = lane < N_GROUPS
    glog = jnp.where(is_group, lg, -jnp.inf)
    gmax, gidx = top1(glog)
    pg_top = 1.0 / jnp.sum(jnp.exp(glog - gmax), axis=-1, keepdims=True)
    eid = lane - N_GROUPS
    sel = (eid >= gidx * EXPERTS_PER_GROUP) & (eid < (gidx + 1) * EXPERTS_PER_GROUP)
    elog = jnp.where(sel, lg, -jnp.inf)
    m1, i1 = top1(elog)
    m2, i2 = top1(jnp.where(lane == i1, -jnp.inf, elog))
    e2 = jnp.exp(m2 - m1)
    w1 = pg_top / (1.0 + e2)
    w2 = pg_top * e2 / (1.0 + e2)
    out = jnp.where(lane == 0, (i1 - N_GROUPS).astype(F32),
                    jnp.where(lane == 1, (i2 - N_GROUPS).astype(F32),
                              jnp.where(lane == 2, w1, jnp.where(lane == 3, w2, 0.0))))
    o_ref[...] = out


def _route(logits):
    tm = 1024
    return pl.pallas_call(
        _route_body,
        grid=(N_TOK // tm,),
        in_specs=[pl.BlockSpec((tm, LANES), lambda i: (i, 0))],
        out_specs=pl.BlockSpec((tm, LANES), lambda i: (i, 0)),
        out_shape=jax.ShapeDtypeStruct((N_TOK, LANES), F32),
        compiler_params=_cparams(("parallel",)),
        name="route",
    )(logits)


def _plan_body(rt_ref, pos_ref, tt_ref, rank_scr, cnt_scr):
    ph = pl.program_id(0)
    i = pl.program_id(1)
    t = PLAN_TM
    lane = lax.broadcasted_iota(jnp.int32, (t, LANES), 1)
    rt = rt_ref[...]
    oh0 = lane.astype(F32) == rt[:, 0:1]
    oh1 = lane.astype(F32) == rt[:, 1:2]
    rows = pl.ds(pl.multiple_of(i * t, t), t)

    @pl.when(ph == 0)
    def _():
        @pl.when(i == 0)
        def _():
            cnt_scr[...] = jnp.zeros_like(cnt_scr)

        s = jnp.where(oh0 | oh1, 1.0, 0.0)
        r = lax.broadcasted_iota(jnp.int32, (t, t), 0)
        c = lax.broadcasted_iota(jnp.int32, (t, t), 1)
        before = jnp.where(c < r, 1.0, 0.0).astype(BF16)
        csum = jnp.dot(before, s.astype(BF16), preferred_element_type=F32) + cnt_scr[...]
        rank0 = jnp.sum(jnp.where(oh0, csum, 0.0), axis=-1, keepdims=True)
        rank1 = jnp.sum(jnp.where(oh1, csum, 0.0), axis=-1, keepdims=True)
        rank_scr[rows, :] = jnp.where(lane == 0, rank0, jnp.where(lane == 1, rank1, 0.0))
        cnt_scr[...] += jnp.sum(s, axis=0, keepdims=True)

    @pl.when(ph == 1)
    def _():
        counts = cnt_scr[...]
        tiles = jnp.floor((counts + (MOE_TM - 1)) * (1.0 / MOE_TM))
        r = lax.broadcasted_iota(jnp.int32, (LANES, LANES), 0)
        c = lax.broadcasted_iota(jnp.int32, (LANES, LANES), 1)
        upto = jnp.where(r <= c, 1.0, 0.0).astype(BF16)
        tiles8 = jnp.broadcast_to(tiles, (8, LANES)).astype(BF16)
        tile_end = jnp.dot(tiles8, upto, preferred_element_type=F32)[0:1]
        offs = (tile_end - tiles) * MOE_TM
        rk = rank_scr[rows, :]
        p0 = jnp.sum(jnp.where(oh0, offs, 0.0), axis=-1, keepdims=True) + rk[:, 0:1]
        p1 = jnp.sum(jnp.where(oh1, offs, 0.0), axis=-1, keepdims=True) + rk[:, 1:2]
        pos_ref[...] = (jnp.where(lane == 0, p0, jnp.where(lane == 1, p1, 0.0)) * SUBLANES).astype(jnp.int32)

        lane_e = lax.broadcasted_iota(jnp.int32, (LANES, LANES), 1)
        tile_id = lax.broadcasted_iota(jnp.int32, (LANES, LANES), 0).astype(F32)
        is_e = lane_e < N_EXPERTS
        total = jnp.max(tile_end, axis=-1, keepdims=True)
        t_exp = jnp.sum(jnp.where(is_e & (tile_end <= tile_id), 1.0, 0.0), axis=-1, keepdims=True)
        t_valid = jnp.where(tile_id[:, 0:1] < total, 1.0, 0.0)
        last_exp = jnp.sum(jnp.where(is_e & (tile_end <= total - 1.0), 1.0, 0.0), axis=-1, keepdims=True)
        t_exp = jnp.where(t_valid > 0, t_exp, last_exp)
        t_first = jnp.sum(jnp.where(is_e & (tiles > 0) & ((tile_end - tiles) == tile_id), 1.0, 0.0),
                          axis=-1, keepdims=True)
        t_last = jnp.sum(jnp.where(is_e & (tiles > 0) & ((tile_end - 1.0) == tile_id), 1.0, 0.0),
                         axis=-1, keepdims=True)
        t_clear = jnp.maximum(t_last, 1.0 - t_valid)
        none = jnp.float32(LANES)
        owns = is_e & (tiles > 0)
        lane_f = lane_e.astype(F32)
        t_next = jnp.min(jnp.where(owns & (lane_f > t_exp), lane_f, none), axis=-1, keepdims=True)
        t_next2 = jnp.min(jnp.where(owns & (lane_f > t_next), lane_f, none), axis=-1, keepdims=True)
        t_ord = jnp.sum(jnp.where(owns & (lane_f < t_exp), 1.0, 0.0), axis=-1, keepdims=True)
        cols = {T_EXP: t_exp, T_VALID: t_valid, T_FIRST: t_first, T_CLEAR: t_clear,
                T_NEXT: t_next, T_NEXT2: t_next2, T_ORD: t_ord}
        table = jnp.zeros((LANES, LANES), F32)
        for k, col in cols.items():
            table = jnp.where(lane_e == k, col, table)
        tt_ref[...] = table.astype(jnp.int32)


def _plan(route):
    t = PLAN_TM
    return pl.pallas_call(
        _plan_body,
        grid=(2, N_TOK // t),
        in_specs=[pl.BlockSpec((t, LANES), lambda ph, i: (i, 0))],
        out_specs=[pl.BlockSpec((t, LANES), lambda ph, i: (i * ph, 0)),
                   pl.BlockSpec((LANES, LANES), lambda ph, i: (0, 0))],
        out_shape=[jax.ShapeDtypeStruct((N_TOK, LANES), jnp.int32),
                   jax.ShapeDtypeStruct((LANES, LANES), jnp.int32)],
        scratch_shapes=[pltpu.VMEM((N_TOK, LANES), F32), pltpu.VMEM((1, LANES), F32)],
        compiler_params=_cparams(("arbitrary", "arbitrary")),
        name="plan",
    )(route)


def _rows_wait(ref, n_rows, sem):
    pltpu.make_async_copy(ref.at[pl.ds(0, n_rows)], ref.at[pl.ds(0, n_rows)], sem).wait()


def _dispatch_body(pos_ref, tt_ref, h_ref, xs_hbm, zbuf, sem_z, sem_s):
    i = pl.program_id(0)
    tile_rows = MOE_TM * SUBLANES

    @pl.when(i == 0)
    def _():
        zbuf[...] = _pack_halves(jnp.zeros((tile_rows, 2 * LANES), F32))

        def zero_copy(tile):
            start = pl.multiple_of(tile * tile_rows, tile_rows)
            return pltpu.make_async_copy(zbuf, xs_hbm.at[pl.ds(start, tile_rows)], sem_z)

        def clear_start(tile, carry):
            @pl.when(tt_ref[tile, T_CLEAR] > 0)
            def _():
                zero_copy(tile).start()
            return carry

        def clear_wait(tile, carry):
            @pl.when(tt_ref[tile, T_CLEAR] > 0)
            def _():
                zero_copy(tile).wait()
            return carry

        lax.fori_loop(0, MOE_TILES, clear_start, 0)
        lax.fori_loop(0, MOE_TILES, clear_wait, 0)

    def tok(j, carry):
        src = h_ref.at[pl.ds(pl.multiple_of(j * SUBLANES, SUBLANES), SUBLANES)]
        pair = TOP_K * (i * DISPATCH_TM + j)
        for k in range(TOP_K):
            dst_row = pl.multiple_of(pos_ref[pair + k], SUBLANES)
            pltpu.make_async_copy(src, xs_hbm.at[pl.ds(dst_row, SUBLANES)], sem_s).start(priority=k)
        return carry

    lax.fori_loop(0, DISPATCH_TM, tok, 0, unroll=8)
    _rows_wait(xs_hbm, TOP_K * DISPATCH_TM * SUBLANES, sem_s)


def _dispatch(pos_rows, tile_clear, h2p):
    grid_spec = pltpu.PrefetchScalarGridSpec(
        num_scalar_prefetch=2,
        grid=(N_TOK // DISPATCH_TM,),
        in_specs=[pl.BlockSpec((DISPATCH_TM * SUBLANES, LANES), lambda i, *_: (i, 0))],
        out_specs=pl.BlockSpec(memory_space=pl.ANY),
        scratch_shapes=[pltpu.VMEM((MOE_TM * SUBLANES, LANES), U32),
                        pltpu.SemaphoreType.DMA(()), pltpu.SemaphoreType.DMA(())],
    )
    return pl.pallas_call(
        _dispatch_body,
        grid_spec=grid_spec,
        out_shape=jax.ShapeDtypeStruct((MOE_ROWS * SUBLANES, LANES), U32),
        compiler_params=_cparams(("arbitrary",)),
        name="dispatch",
    )(pos_rows, tile_clear, h2p)


def _moe_body(tt_ref, x_ref, wg_hbm, wu_hbm, wd_hbm, y_ref,
              wg_s, wu_s, wd_s, stg_g, stg_u, stg_d, sem):
    i = pl.program_id(0)

    def fetch(e, slot):
        return (pltpu.make_async_copy(wg_hbm.at[e], stg_g.at[slot], sem.at[slot, 0]),
                pltpu.make_async_copy(wu_hbm.at[e], stg_u.at[slot], sem.at[slot, 1]),
                pltpu.make_async_copy(wd_hbm.at[e], stg_d.at[slot], sem.at[slot, 2]))

    @pl.when(i == 0)
    def _():
        for cp in fetch(tt_ref[0, T_EXP], 0):
            cp.start()

        @pl.when(tt_ref[0, T_NEXT] < N_EXPERTS)
        def _():
            for cp in fetch(tt_ref[0, T_NEXT], 1):
                cp.start()

    @pl.when(tt_ref[i, T_FIRST] > 0)
    def _():
        slot = tt_ref[i, T_ORD] % 2
        for cp in fetch(tt_ref[i, T_EXP], slot):
            cp.wait()
        wg_s[...] = stg_g[slot].astype(BF16)
        wu_s[...] = stg_u[slot].astype(BF16)
        wd_s[...] = stg_d[slot].astype(BF16)

        @pl.when(tt_ref[i, T_NEXT2] < N_EXPERTS)
        def _():
            for cp in fetch(tt_ref[i, T_NEXT2], slot):
                cp.start()

    @pl.when(tt_ref[i, T_VALID] > 0)
    def _():
        lo, hi = _unpack_halves(_load_token_tiles(x_ref, 0, MOE_TM))
        xa = lo.astype(BF16)
        xb = hi.astype(BF16)
        g = (jnp.dot(xa, wg_s[:HALF_D, :], preferred_element_type=F32)
             + jnp.dot(xb, wg_s[HALF_D:, :], preferred_element_type=F32))
        u = (jnp.dot(xa, wu_s[:HALF_D, :], preferred_element_type=F32)
             + jnp.dot(xb, wu_s[HALF_D:, :], preferred_element_type=F32))
        hid = (g * jax.nn.sigmoid(g) * u).astype(BF16)
        _store_token_tiles(y_ref, _pack_halves(jnp.dot(hid, wd_s[...], preferred_element_type=F32)))

    @pl.when(tt_ref[i, T_VALID] == 0)
    def _():
        y_ref[...] = _pack_halves(jnp.zeros((MOE_TM * SUBLANES, 2 * LANES), F32))


def _moe(tile_tab, xs, wg, wu, wd):
    tm = MOE_TM
    grid_spec = pltpu.PrefetchScalarGridSpec(
        num_scalar_prefetch=1,
        grid=(MOE_TILES,),
        in_specs=[pl.BlockSpec((tm * SUBLANES, LANES), lambda i, *_: (i, 0)),
                  pl.BlockSpec(memory_space=pl.ANY), pl.BlockSpec(memory_space=pl.ANY),
                  pl.BlockSpec(memory_space=pl.ANY)],
        out_specs=pl.BlockSpec((tm * SUBLANES, LANES), lambda i, *_: (i, 0)),
        scratch_shapes=[pltpu.VMEM((D_MODEL, EXPERT_FF), BF16), pltpu.VMEM((D_MODEL, EXPERT_FF), BF16),
                        pltpu.VMEM((EXPERT_FF, D_MODEL), BF16),
                        pltpu.VMEM((2, D_MODEL, EXPERT_FF), F32), pltpu.VMEM((2, D_MODEL, EXPERT_FF), F32),
                        pltpu.VMEM((2, EXPERT_FF, D_MODEL), F32),
                        pltpu.SemaphoreType.DMA((2, 3))],
    )
    return pl.pallas_call(
        _moe_body,
        grid_spec=grid_spec,
        out_shape=jax.ShapeDtypeStruct((MOE_ROWS * SUBLANES, LANES), U32),
        compiler_params=_cparams(("arbitrary",)),
        name="moe",
    )(tile_tab, xs, wg, wu, wd)


def _final_body(pos_ref, x1_ref, mod_ref, rt_ref, g2_ref, b2_ref, ys_hbm, o_ref, buf, sem):
    i = pl.program_id(0)
    n = pl.num_programs(0)
    tm = FINAL_TM

    def issue(tile, slot):
        def tok(j, carry):
            pair = TOP_K * (tile * tm + j)
            for k in range(TOP_K):
                src_row = pl.multiple_of(pos_ref[pair + k], SUBLANES)
                dst_row = pl.multiple_of((k * tm + j) * SUBLANES, SUBLANES)
                pltpu.make_async_copy(ys_hbm.at[pl.ds(src_row, SUBLANES)],
                                      buf.at[slot, pl.ds(dst_row, SUBLANES)], sem.at[slot]).start(priority=k)
            return carry

        lax.fori_loop(0, tm, tok, 0, unroll=8)

    @pl.when(i == 0)
    def _():
        issue(0, 0)

    @pl.when(i + 1 < n)
    def _():
        issue(i + 1, (i + 1) % 2)

    slot = i % 2
    pltpu.make_async_copy(ys_hbm.at[pl.ds(0, TOP_K * tm * SUBLANES)], buf.at[slot], sem.at[slot]).wait()
    a_lo, a_hi = _unpack_halves(_load_token_tiles(buf.at[slot], 0, tm))
    b_lo, b_hi = _unpack_halves(_load_token_tiles(buf.at[slot], tm, tm))
    w0 = rt_ref[:, 2:3]
    w1 = rt_ref[:, 3:4]
    y = jnp.concatenate([w0 * a_lo + w1 * b_lo, w0 * a_hi + w1 * b_hi], axis=1)
    gate2 = mod_ref[0, 5:6, :]
    o_ref[...] = _ln_rows(DEEPNORM_ALPHA * x1_ref[...] + gate2 * y) * g2_ref[...] + b2_ref[...]


def _final(pos_rows, x1, mod3, route, g2, b2, ys):
    tm = FINAL_TM
    tpb = SEQ // tm
    row = lambda w: pl.BlockSpec((tm, w), lambda i, *_: (i, 0))
    grid_spec = pltpu.PrefetchScalarGridSpec(
        num_scalar_prefetch=1,
        grid=(N_TOK // tm,),
        in_specs=[row(D_MODEL),
                  pl.BlockSpec((1, N_MOD, D_MODEL), lambda i, *_: (i // tpb, 0, 0)),
                  row(LANES),
                  pl.BlockSpec(g2.shape, lambda i, *_: (0, 0)),
                  pl.BlockSpec(b2.shape, lambda i, *_: (0, 0)),
                  pl.BlockSpec(memory_space=pl.ANY)],
        out_specs=row(D_MODEL),
        scratch_shapes=[pltpu.VMEM((2, TOP_K * tm * SUBLANES, LANES), U32),
                        pltpu.SemaphoreType.DMA((2,))],
    )
    return pl.pallas_call(
        _final_body,
        grid_spec=grid_spec,
        out_shape=jax.ShapeDtypeStruct((N_TOK, D_MODEL), F32),
        compiler_params=_cparams(("arbitrary",)),
        name="final",
    )(pos_rows, x1, mod3, route, g2, b2, ys)


def kernel(x, c, positions, w_ada, b_ada, w_in, q_norm_g, w_uq, kv_norm_g, w_ukv, sgu_norm_g, sgu_norm_b,
           w_spatial, b_spatial, w_o, ln1_g, ln1_b, w_router_group, b_router_group, w_router_expert,
           b_router_expert, w_gate, w_up, w_down, ln2_g, ln2_b):
    l = 0
    x2 = x.reshape(N_TOK, D_MODEL)
    mod3 = _ada(c, w_ada[l], b_ada[l][None, :]).reshape(BATCH, N_MOD, D_MODEL)

    wq, wkv, wkpe, wz, wuq, wukv, wo = _prep(w_in[l], w_uq[l], w_ukv[l], w_o[l])
    n_r = N_GROUPS + N_EXPERTS
    wr = jnp.pad(jnp.concatenate([w_router_group[l], w_router_expert[l]], axis=1),
                 ((0, 0), (0, LANES - n_r))).astype(BF16)
    br = jnp.pad(jnp.concatenate([b_router_group[l], b_router_expert[l]]), (0, LANES - n_r))[None, :]
    inv_freq = 1.0 / (ROPE_THETA ** (jnp.arange(0, QK_ROPE_DIM, 2, dtype=F32) / QK_ROPE_DIM))
    invf = jnp.tile(inv_freq, 2 * LANES // QK_ROPE_DIM)[None, :]

    cqn, ckvn, kpe, u, vs = _inproj(x2, mod3, wq, wkv, wkpe, wz, q_norm_g[l][None, :], kv_norm_g[l][None, :],
                                    sgu_norm_g[l][None, :], sgu_norm_b[l][None, :])
    q, k, v = _qkv(cqn, ckvn, kpe, positions.reshape(N_TOK, 1), invf, wuq, wukv)
    attn = _attn(q, k, v).reshape(N_TOK, MLA_WIDTH)
    x1, h2, logits = _mixout(x2, mod3, attn, u, vs, w_spatial[l], b_spatial[l].T, wo,
                             ln1_g[l][None, :], ln1_b[l][None, :], wr, br)
    route = _route(logits)
    pos_tab, tile_tab = _plan(route)
    pos_rows = pos_tab[:, 0:TOP_K].reshape(-1)
    xs = _dispatch(pos_rows, tile_tab, h2)
    ys = _moe(tile_tab, xs, w_gate[l], w_up[l], w_down[l])
    out = _final(pos_rows, x1, mod3, route, ln2_g[l][None, :], ln2_b[l][None, :], ys)
    return out.reshape(BATCH, SEQ, D_MODEL)
```

```python
import functools

import jax
import jax.numpy as jnp
import numpy as np
from jax import lax
from jax.experimental import pallas as pl
from jax.experimental.pallas import tpu as pltpu

D_MODEL = 2048
BATCH = 4
SEQ = 2048
N_TOK = BATCH * SEQ

MLA_HEADS = 8
QK_NOPE_DIM = 128
QK_ROPE_DIM = 64
QK_DIM = QK_NOPE_DIM + QK_ROPE_DIM
V_HEAD_DIM = 128
Q_LORA_RANK = 768
KV_LORA_RANK = 512
ROPE_THETA = 10000.0
MLA_WIDTH = MLA_HEADS * V_HEAD_DIM

SGU_GROUPS = 8
SGU_GROUP_DIM = 128
SGU_CHUNK = 128
SGU_WIDTH = SGU_GROUPS * SGU_GROUP_DIM

N_GROUPS = 4
EXPERTS_PER_GROUP = 8
N_EXPERTS = N_GROUPS * EXPERTS_PER_GROUP
TOP_K = 2
EXPERT_FF = 512

DEEPNORM_ALPHA = 2.0 ** 0.25
EPS = 1e-6
N_MOD = 6
NEG_BIG = -1e30

LANES = 128
SUBLANES = 8
VMEM_LIMIT = 56 * 1024 * 1024

ADA_TN = 1024
TOK_TM = 512
MIX_TM = 256
PREP_STEPS = 8
ATT_TQ = 512
ATT_TK = 512
ATT_HEADS = 2
MOE_TM = 256
MOE_TILES = (N_TOK * TOP_K + N_EXPERTS * (MOE_TM - 1)) // MOE_TM + 1
MOE_ROWS = MOE_TILES * MOE_TM
PLAN_TM = 2048
ROUTE_TM = 1024
RT_RANK = 4
DISPATCH_TM = 1024
FINAL_TM = 256
FINAL_CHUNK = 128
assert MOE_TILES <= LANES
T_EXP, T_VALID, T_FIRST, T_CLEAR, T_NEXT, T_NEXT2, T_ORD = range(7)

F32 = jnp.float32
BF16 = jnp.bfloat16
U32 = jnp.uint32
HALF_D = D_MODEL // 2


def _cparams(sem):
    return pltpu.CompilerParams(dimension_semantics=sem, vmem_limit_bytes=VMEM_LIMIT)


def _const_spec(shape):
    nd = len(shape)
    return pl.BlockSpec(shape, lambda *_: (0,) * nd, pipeline_mode=pl.Buffered(1))


def _ln_rows(x):
    mu = jnp.mean(x, axis=-1, keepdims=True)
    xc = x - mu
    var = jnp.mean(xc * xc, axis=-1, keepdims=True)
    return xc * lax.rsqrt(var + EPS)


def _rms_rows(x):
    return x * lax.rsqrt(jnp.mean(x * x, axis=-1, keepdims=True) + EPS)


def _pack_halves(x):
    half = x.shape[-1] // 2
    return pltpu.pack_elementwise([x[:, :half], x[:, half:]], packed_dtype=BF16)


def _unpack_halves(w):
    lo = pltpu.unpack_elementwise(w, index=0, packed_dtype=BF16, unpacked_dtype=F32)
    hi = pltpu.unpack_elementwise(w, index=1, packed_dtype=BF16, unpacked_dtype=F32)
    return lo, hi


def _store_token_tiles(ref, w):
    rows = w.shape[0]
    for s in range(SUBLANES):
        ref[pl.ds(s, rows, stride=SUBLANES), :] = w[:, s * LANES:(s + 1) * LANES]


def _load_token_tiles(ref, start_row, rows):
    return jnp.concatenate([ref[pl.ds(start_row * SUBLANES + s, rows, stride=SUBLANES), :]
                            for s in range(SUBLANES)], axis=1)


def _gelu_tanh(x):
    c = np.sqrt(2.0 / np.pi).astype(np.float32)
    return 0.5 * x * (1.0 + jnp.tanh(c * (x + 0.044715 * (x * x * x))))


def _ada_body(c_ref, w_ref, b_ref, o_ref):
    o_ref[...] = jnp.dot(c_ref[...].astype(BF16), w_ref[...].astype(BF16),
                         preferred_element_type=F32) + b_ref[...]


def _ada(c, w, b):
    n = w.shape[1]
    return pl.pallas_call(
        _ada_body,
        grid=(n // ADA_TN,),
        in_specs=[pl.BlockSpec((BATCH, D_MODEL), lambda j: (0, 0)),
                  pl.BlockSpec((D_MODEL, ADA_TN), lambda j: (0, j)),
                  pl.BlockSpec((1, ADA_TN), lambda j: (0, j))],
        out_specs=pl.BlockSpec((BATCH, ADA_TN), lambda j: (0, j)),
        out_shape=jax.ShapeDtypeStruct((BATCH, n), F32),
        compiler_params=_cparams(("parallel",)),
        name="ada",
    )(c, w, b)


def _prep_body(win_ref, wuq_ref, wukv_ref, wo_ref, wq_o, wkv_o, wkpe_o, wz_o, wuq_o, wukv_o, wo_o):
    o1, o2, o3 = Q_LORA_RANK, Q_LORA_RANK + KV_LORA_RANK, Q_LORA_RANK + KV_LORA_RANK + QK_ROPE_DIM
    w = win_ref[...]
    wq_o[...] = w[:, :o1].astype(BF16)
    wkv_o[...] = w[:, o1:o2].astype(BF16)
    kpe = w[:, o2:o3]
    wkpe_o[...] = jnp.concatenate([kpe, jnp.zeros_like(kpe)], axis=1).astype(BF16)
    wz_o[...] = w[:, o3:].astype(BF16)
    u = wuq_ref[...]
    nope = [u[:, h * QK_DIM:h * QK_DIM + QK_NOPE_DIM] for h in range(MLA_HEADS)]
    rope = [u[:, h * QK_DIM + QK_NOPE_DIM:(h + 1) * QK_DIM] for h in range(MLA_HEADS)]
    wuq_o[...] = jnp.concatenate(nope + rope, axis=1).astype(BF16)
    kv = wukv_ref[...]
    hw = QK_NOPE_DIM + V_HEAD_DIM
    kn = [kv[:, h * hw:h * hw + QK_NOPE_DIM] for h in range(MLA_HEADS)]
    vv = [kv[:, h * hw + QK_NOPE_DIM:(h + 1) * hw] for h in range(MLA_HEADS)]
    wukv_o[...] = jnp.concatenate(kn + vv, axis=1).astype(BF16)
    wo_o[...] = wo_ref[...].astype(BF16)


def _prep(w_in, w_uq, w_ukv, w_o):
    steps = PREP_STEPS
    zw = w_in.shape[1] - (Q_LORA_RANK + KV_LORA_RANK + QK_ROPE_DIM)
    blk = lambda a: pl.BlockSpec((a.shape[0] // steps, a.shape[1]), lambda i: (i, 0))
    out_cols = (Q_LORA_RANK, KV_LORA_RANK, LANES, zw)
    outs = ([(D_MODEL, c) for c in out_cols] + [w_uq.shape, w_ukv.shape, w_o.shape])
    return pl.pallas_call(
        _prep_body,
        grid=(steps,),
        in_specs=[blk(w_in), blk(w_uq), blk(w_ukv), blk(w_o)],
        out_specs=[pl.BlockSpec((r // steps, c), lambda i: (i, 0)) for r, c in outs],
        out_shape=[jax.ShapeDtypeStruct(s, BF16) for s in outs],
        compiler_params=_cparams(("parallel",)),
        name="prep",
    )(w_in, w_uq, w_ukv, w_o)


def _inproj_body(x_ref, mod_ref, wq_ref, wkv_ref, wkpe_ref, wz_ref, gq_ref, gkv_ref, sg_ref, sb_ref,
                 cq_ref, ckv_ref, kpe_ref, u_ref, vs_ref):
    sh = mod_ref[0, 0:1, :]
    sc = mod_ref[0, 1:2, :]
    h = (_ln_rows(x_ref[...]) * (1.0 + sc) + sh).astype(BF16)
    cq = jnp.dot(h, wq_ref[...], preferred_element_type=F32)
    cq_ref[...] = (_rms_rows(cq) * gq_ref[...]).astype(BF16)
    ckv = jnp.dot(h, wkv_ref[...], preferred_element_type=F32)
    ckv_ref[...] = (_rms_rows(ckv) * gkv_ref[...]).astype(BF16)
    kpe_ref[...] = jnp.dot(h, wkpe_ref[...], preferred_element_type=F32)
    gz = _gelu_tanh(jnp.dot(h, wz_ref[...], preferred_element_type=F32))
    u_ref[...] = gz[:, :SGU_WIDTH]
    vs_ref[...] = (_ln_rows(gz[:, SGU_WIDTH:]) * sg_ref[...] + sb_ref[...]).astype(BF16)


def _inproj(x2, mod3, wq, wkv, wkpe, wz, gq, gkv, sg, sb):
    tm = TOK_TM
    tiles_per_batch = SEQ // tm
    row = lambda w: pl.BlockSpec((tm, w), lambda i: (i, 0))
    return pl.pallas_call(
        _inproj_body,
        grid=(N_TOK // tm,),
        in_specs=[row(D_MODEL),
                  pl.BlockSpec((1, N_MOD, D_MODEL), lambda i: (i // tiles_per_batch, 0, 0)),
                  _const_spec(wq.shape), _const_spec(wkv.shape), _const_spec(wkpe.shape), _const_spec(wz.shape),
                  _const_spec(gq.shape), _const_spec(gkv.shape), _const_spec(sg.shape), _const_spec(sb.shape)],
        out_specs=[row(Q_LORA_RANK), row(KV_LORA_RANK), row(LANES), row(SGU_WIDTH), row(SGU_WIDTH)],
        out_shape=[jax.ShapeDtypeStruct((N_TOK, Q_LORA_RANK), BF16),
                   jax.ShapeDtypeStruct((N_TOK, KV_LORA_RANK), BF16),
                   jax.ShapeDtypeStruct((N_TOK, LANES), F32),
                   jax.ShapeDtypeStruct((N_TOK, SGU_WIDTH), F32),
                   jax.ShapeDtypeStruct((N_TOK, SGU_WIDTH), BF16)],
        compiler_params=_cparams(("parallel",)),
        name="inproj",
    )(x2, mod3, wq, wkv, wkpe, wz, gq, gkv, sg, sb)


def _rope(x, cos, sin):
    w = x.shape[-1]
    lane = lax.broadcasted_iota(jnp.int32, x.shape, 1)
    first_half = (lane % QK_ROPE_DIM) < (QK_ROPE_DIM // 2)
    rot = jnp.where(first_half,
                    -pltpu.roll(x, w - QK_ROPE_DIM // 2, 1),
                    pltpu.roll(x, QK_ROPE_DIM // 2, 1))
    return x * cos + rot * sin


def _qkv_body(cq_ref, ckv_ref, kpe_ref, pos_ref, invf_ref, wuq_ref, wukv_ref, q_ref, k_ref, v_ref):
    ang = pos_ref[...].astype(F32) * invf_ref[...]
    cos1 = jnp.cos(ang)
    sin1 = jnp.sin(ang)
    reps = MLA_HEADS * QK_ROPE_DIM // LANES
    cos = jnp.concatenate([cos1] * reps, axis=1)
    sin = jnp.concatenate([sin1] * reps, axis=1)
    scale = np.float32(QK_DIM ** -0.5)
    q = jnp.dot(cq_ref[...], wuq_ref[...], preferred_element_type=F32) * scale
    q_pe = _rope(q[:, MLA_HEADS * QK_NOPE_DIM:], cos, sin)
    kv = jnp.dot(ckv_ref[...], wukv_ref[...], preferred_element_type=F32)
    k_pe = _rope(kpe_ref[...], cos1, sin1)[:, :QK_ROPE_DIM].astype(BF16)
    for h in range(MLA_HEADS):
        q_ref[0, h, :, 0:QK_NOPE_DIM] = q[:, h * QK_NOPE_DIM:(h + 1) * QK_NOPE_DIM].astype(BF16)
        q_ref[0, h, :, QK_NOPE_DIM:QK_DIM] = q_pe[:, h * QK_ROPE_DIM:(h + 1) * QK_ROPE_DIM].astype(BF16)
        k_ref[0, h, :, 0:QK_NOPE_DIM] = kv[:, h * QK_NOPE_DIM:(h + 1) * QK_NOPE_DIM].astype(BF16)
        k_ref[0, h, :, QK_NOPE_DIM:QK_DIM] = k_pe
        v_ref[0, h, :, :] = kv[:, MLA_WIDTH + h * V_HEAD_DIM:MLA_WIDTH + (h + 1) * V_HEAD_DIM].astype(BF16)


def _qkv(cqn, ckvn, kpe, pos2, invf, wuq, wukv):
    tm = TOK_TM
    tpb = SEQ // tm
    row = lambda w: pl.BlockSpec((tm, w), lambda i: (i, 0))
    head_out = lambda w: pl.BlockSpec((1, MLA_HEADS, tm, w), lambda i: (i // tpb, 0, i % tpb, 0))
    return pl.pallas_call(
        _qkv_body,
        grid=(N_TOK // tm,),
        in_specs=[row(Q_LORA_RANK), row(KV_LORA_RANK), row(LANES), row(1),
                  _const_spec(invf.shape), _const_spec(wuq.shape), _const_spec(wukv.shape)],
        out_specs=[head_out(QK_DIM), head_out(QK_DIM), head_out(V_HEAD_DIM)],
        out_shape=[jax.ShapeDtypeStruct((BATCH, MLA_HEADS, SEQ, QK_DIM), BF16),
                   jax.ShapeDtypeStruct((BATCH, MLA_HEADS, SEQ, QK_DIM), BF16),
                   jax.ShapeDtypeStruct((BATCH, MLA_HEADS, SEQ, V_HEAD_DIM), BF16)],
        compiler_params=_cparams(("parallel",)),
        name="qkv",
    )(cqn, ckvn, kpe, pos2, invf, wuq, wukv)


def _attn_body(q_ref, k_ref, v_ref, o_ref):
    i = pl.program_id(2)

    def step(h, j, carry, masked):
        m, l, acc = carry
        start = pl.multiple_of(j * ATT_TK, ATT_TK)
        k = k_ref[0, h, pl.ds(start, ATT_TK), :]
        v = v_ref[0, h, pl.ds(start, ATT_TK), :]
        s = lax.dot_general(q_ref[0, h], k, (((1,), (1,)), ((), ())), preferred_element_type=F32)
        if masked:
            r = lax.broadcasted_iota(jnp.int32, s.shape, 0)
            c = lax.broadcasted_iota(jnp.int32, s.shape, 1)
            s = jnp.where(c <= r, s, NEG_BIG)
        m_new = jnp.maximum(m, jnp.max(s, axis=-1, keepdims=True))
        p = jnp.exp(s - m_new)
        a = jnp.exp(m - m_new)
        l = a * l + jnp.sum(p, axis=-1, keepdims=True)
        acc = a * acc + jnp.dot(p.astype(BF16), v, preferred_element_type=F32)
        return m_new, l, acc

    def steps(j, carries, masked):
        return tuple(step(h, j, carries[h], masked) for h in range(ATT_HEADS))

    init = tuple((jnp.full((ATT_TQ, 1), NEG_BIG, F32), jnp.zeros((ATT_TQ, 1), F32),
                  jnp.zeros((ATT_TQ, V_HEAD_DIM), F32)) for _ in range(ATT_HEADS))
    carries = lax.fori_loop(0, i, lambda j, c: steps(j, c, False), init)
    carries = steps(i, carries, True)
    for h, (m, l, acc) in enumerate(carries):
        o_ref[0, :, h * V_HEAD_DIM:(h + 1) * V_HEAD_DIM] = (acc / l).astype(BF16)


def _attn(q, k, v):
    assert ATT_TQ == ATT_TK
    hb = ATT_HEADS
    return pl.pallas_call(
        _attn_body,
        grid=(BATCH, MLA_HEADS // hb, SEQ // ATT_TQ),
        in_specs=[pl.BlockSpec((1, hb, ATT_TQ, QK_DIM), lambda b, h, i: (b, h, i, 0)),
                  pl.BlockSpec((1, hb, SEQ, QK_DIM), lambda b, h, i: (b, h, 0, 0)),
                  pl.BlockSpec((1, hb, SEQ, V_HEAD_DIM), lambda b, h, i: (b, h, 0, 0))],
        out_specs=pl.BlockSpec((1, ATT_TQ, hb * V_HEAD_DIM), lambda b, h, i: (b, i, h)),
        out_shape=jax.ShapeDtypeStruct((BATCH, SEQ, MLA_WIDTH), BF16),
        compiler_params=_cparams(("parallel", "parallel", "arbitrary")),
        name="attn",
    )(q, k, v)


def _mixout_body(x_ref, mod_ref, attn_ref, u_ref, vs_ref, wsp_ref, bsp_ref, woa_ref, wos_ref,
                 g1_ref, b1_ref, wr_ref, br_ref, x1_ref, h2_ref, lg_ref, sgu_scr):
    r = lax.broadcasted_iota(jnp.int32, (SGU_CHUNK, SGU_CHUNK), 0)
    c = lax.broadcasted_iota(jnp.int32, (SGU_CHUNK, SGU_CHUNK), 1)
    causal = c <= r
    for g in range(SGU_GROUPS):
        ws = jnp.where(causal, wsp_ref[g], 0.0).astype(BF16)
        bias = bsp_ref[:, g:g + 1]
        cols = slice(g * SGU_GROUP_DIM, (g + 1) * SGU_GROUP_DIM)
        for ch in range(MIX_TM // SGU_CHUNK):
            rows = slice(ch * SGU_CHUNK, (ch + 1) * SGU_CHUNK)
            mixed = jnp.dot(ws, vs_ref[rows, cols], preferred_element_type=F32) + bias
            sgu_scr[rows, cols] = (u_ref[rows, cols] * mixed).astype(BF16)
    y = (jnp.dot(attn_ref[...], woa_ref[...], preferred_element_type=F32)
         + jnp.dot(sgu_scr[...], wos_ref[...], preferred_element_type=F32))
    gate1 = mod_ref[0, 2:3, :]
    sh2 = mod_ref[0, 3:4, :]
    sc2 = mod_ref[0, 4:5, :]
    x1 = _ln_rows(DEEPNORM_ALPHA * x_ref[...] + gate1 * y) * g1_ref[...] + b1_ref[...]
    x1_ref[...] = x1
    h2 = _ln_rows(x1) * (1.0 + sc2) + sh2
    _store_token_tiles(h2_ref, _pack_halves(h2))
    lg_ref[...] = jnp.dot(h2.astype(BF16), wr_ref[...], preferred_element_type=F32) + br_ref[...]


def _mixout(x2, mod3, attn, u, vs, wsp, bsp_t, wo, g1, b1, wr, br):
    tm = MIX_TM
    tpb = SEQ // tm
    row = lambda w: pl.BlockSpec((tm, w), lambda i: (i, 0))
    wo_half = lambda j: pl.BlockSpec((MLA_WIDTH, D_MODEL), lambda i: (j, 0), pipeline_mode=pl.Buffered(1))
    return pl.pallas_call(
        _mixout_body,
        grid=(N_TOK // tm,),
        in_specs=[row(D_MODEL),
                  pl.BlockSpec((1, N_MOD, D_MODEL), lambda i: (i // tpb, 0, 0)),
                  row(MLA_WIDTH), row(SGU_WIDTH), row(SGU_WIDTH),
                  _const_spec(wsp.shape), _const_spec(bsp_t.shape), wo_half(0), wo_half(1),
                  _const_spec(g1.shape), _const_spec(b1.shape), _const_spec(wr.shape), _const_spec(br.shape)],
        out_specs=[row(D_MODEL), pl.BlockSpec((tm * SUBLANES, LANES), lambda i: (i, 0)), row(LANES)],
        out_shape=[jax.ShapeDtypeStruct((N_TOK, D_MODEL), F32),
                   jax.ShapeDtypeStruct((N_TOK * SUBLANES, LANES), U32),
                   jax.ShapeDtypeStruct((N_TOK, LANES), F32)],
        scratch_shapes=[pltpu.VMEM((tm, SGU_WIDTH), BF16)],
        compiler_params=_cparams(("parallel",)),
        name="mix_out",
    )(x2, mod3, attn, u, vs, wsp, bsp_t, wo, wo, g1, b1, wr, br)


def _route_math(lg):
    lane = lax.broadcasted_iota(jnp.int32, lg.shape, 1)
    big = jnp.int32(LANES)

    def top1(vals):
        m = jnp.max(vals, axis=-1, keepdims=True)
        idx = jnp.min(jnp.where(vals == m, lane, big), axis=-1, keepdims=True)
        return m, idx

    is_group = lane < N_GROUPS
    glog = jnp.where(is_group, lg, -jnp.inf)
    gmax, gidx = top1(glog)
    pg_top = 1.0 / jnp.sum(jnp.exp(glog - gmax), axis=-1, keepdims=True)
    eid = lane - N_GROUPS
    sel = (eid >= gidx * EXPERTS_PER_GROUP) & (eid < (gidx + 1) * EXPERTS_PER_GROUP)
    elog = jnp.where(sel, lg, -jnp.inf)
    m1, i1 = top1(elog)
    m2, i2 = top1(jnp.where(lane == i1, -jnp.inf, elog))
    e2 = jnp.exp(m2 - m1)
    w1 = pg_top / (1.0 + e2)
    w2 = pg_top * e2 / (1.0 + e2)
    return jnp.where(lane == 0, (i1 - N_GROUPS).astype(F32),
                     jnp.where(lane == 1, (i2 - N_GROUPS).astype(F32),
                               jnp.where(lane == 2, w1, jnp.where(lane == 3, w2, 0.0))))


def _rank_math(rt, counts):
    t = rt.shape[0]
    lane = lax.broadcasted_iota(jnp.int32, (t, LANES), 1).astype(F32)
    oh0 = lane == rt[:, 0:1]
    oh1 = lane == rt[:, 1:2]
    s = jnp.where(oh0 | oh1, 1.0, 0.0)
    r = lax.broadcasted_iota(jnp.int32, (t, t), 0)
    c = lax.broadcasted_iota(jnp.int32, (t, t), 1)
    before = jnp.where(c < r, 1.0, 0.0).astype(BF16)
    csum = jnp.dot(before, s.astype(BF16), preferred_element_type=F32) + counts
    rank0 = jnp.sum(jnp.where(oh0, csum, 0.0), axis=-1, keepdims=True)
    rank1 = jnp.sum(jnp.where(oh1, csum, 0.0), axis=-1, keepdims=True)
    return rank0, rank1, counts + jnp.sum(s, axis=0, keepdims=True)


def _route_body(lg_ref, rt_ref, cnt_ref, cnt_scr):
    @pl.when(pl.program_id(0) == 0)
    def _():
        cnt_scr[...] = jnp.zeros_like(cnt_scr)

    rt = _route_math(lg_ref[...])
    rank0, rank1, counts = _rank_math(rt, cnt_scr[...])
    cnt_scr[...] = counts
    lane = lax.broadcasted_iota(jnp.int32, rt.shape, 1)
    rt_ref[...] = jnp.where(lane == RT_RANK, rank0, jnp.where(lane == RT_RANK + 1, rank1, rt))
    cnt_ref[...] = jnp.broadcast_to(counts, cnt_ref.shape)


def _route(logits):
    tm = ROUTE_TM
    return pl.pallas_call(
        _route_body,
        grid=(N_TOK // tm,),
        in_specs=[pl.BlockSpec((tm, LANES), lambda i: (i, 0))],
        out_specs=[pl.BlockSpec((tm, LANES), lambda i: (i, 0)),
                   pl.BlockSpec((SUBLANES, LANES), lambda i: (0, 0))],
        out_shape=[jax.ShapeDtypeStruct((N_TOK, LANES), F32),
                   jax.ShapeDtypeStruct((SUBLANES, LANES), F32)],
        scratch_shapes=[pltpu.VMEM((1, LANES), F32)],
        compiler_params=_cparams(("arbitrary",)),
        name="route",
    )(logits)


def _plan_body(rt_ref, cnt_ref, pos_ref, tt_ref):
    t = PLAN_TM
    lane = lax.broadcasted_iota(jnp.int32, (t, LANES), 1)
    rt = rt_ref[...]
    oh0 = lane.astype(F32) == rt[:, 0:1]
    oh1 = lane.astype(F32) == rt[:, 1:2]
    if True:
        counts = cnt_ref[0:1, :]
        tiles = jnp.floor((counts + (MOE_TM - 1)) * (1.0 / MOE_TM))
        r = lax.broadcasted_iota(jnp.int32, (LANES, LANES), 0)
        c = lax.broadcasted_iota(jnp.int32, (LANES, LANES), 1)
        upto = jnp.where(r <= c, 1.0, 0.0).astype(BF16)
        tiles8 = jnp.broadcast_to(tiles, (8, LANES)).astype(BF16)
        tile_end = jnp.dot(tiles8, upto, preferred_element_type=F32)[0:1]
        offs = (tile_end - tiles) * MOE_TM
        p0 = jnp.sum(jnp.where(oh0, offs, 0.0), axis=-1, keepdims=True) + rt[:, RT_RANK:RT_RANK + 1]
        p1 = jnp.sum(jnp.where(oh1, offs, 0.0), axis=-1, keepdims=True) + rt[:, RT_RANK + 1:RT_RANK + 2]
        pos_ref[...] = (jnp.where(lane == 0, p0, jnp.where(lane == 1, p1, 0.0)) * SUBLANES).astype(jnp.int32)

        lane_e = lax.broadcasted_iota(jnp.int32, (LANES, LANES), 1)
        tile_id = lax.broadcasted_iota(jnp.int32, (LANES, LANES), 0).astype(F32)
        is_e = lane_e < N_EXPERTS
        total = jnp.max(tile_end, axis=-1, keepdims=True)
        t_exp = jnp.sum(jnp.where(is_e & (tile_end <= tile_id), 1.0, 0.0), axis=-1, keepdims=True)
        t_valid = jnp.where(tile_id[:, 0:1] < total, 1.0, 0.0)
        last_exp = jnp.sum(jnp.where(is_e & (tile_end <= total - 1.0), 1.0, 0.0), axis=-1, keepdims=True)
        t_exp = jnp.where(t_valid > 0, t_exp, last_exp)
        t_first = jnp.sum(jnp.where(is_e & (tiles > 0) & ((tile_end - tiles) == tile_id), 1.0, 0.0),
                          axis=-1, keepdims=True)
        t_last = jnp.sum(jnp.where(is_e & (tiles > 0) & ((tile_end - 1.0) == tile_id), 1.0, 0.0),
                         axis=-1, keepdims=True)
        t_clear = jnp.maximum(t_last, 1.0 - t_valid)
        none = jnp.float32(LANES)
        owns = is_e & (tiles > 0)
        lane_f = lane_e.astype(F32)
        t_next = jnp.min(jnp.where(owns & (lane_f > t_exp), lane_f, none), axis=-1, keepdims=True)
        t_next2 = jnp.min(jnp.where(owns & (lane_f > t_next), lane_f, none), axis=-1, keepdims=True)
        t_ord = jnp.sum(jnp.where(owns & (lane_f < t_exp), 1.0, 0.0), axis=-1, keepdims=True)
        cols = {T_EXP: t_exp, T_VALID: t_valid, T_FIRST: t_first, T_CLEAR: t_clear,
                T_NEXT: t_next, T_NEXT2: t_next2, T_ORD: t_ord}
        table = jnp.zeros((LANES, LANES), F32)
        for k, col in cols.items():
            table = jnp.where(lane_e == k, col, table)
        tt_ref[...] = table.astype(jnp.int32)


def _plan(route, counts):
    t = PLAN_TM
    return pl.pallas_call(
        _plan_body,
        grid=(N_TOK // t,),
        in_specs=[pl.BlockSpec((t, LANES), lambda i: (i, 0)),
                  pl.BlockSpec((SUBLANES, LANES), lambda i: (0, 0))],
        out_specs=[pl.BlockSpec((t, LANES), lambda i: (i, 0)),
                   pl.BlockSpec((LANES, LANES), lambda i: (0, 0))],
        out_shape=[jax.ShapeDtypeStruct((N_TOK, LANES), jnp.int32),
                   jax.ShapeDtypeStruct((LANES, LANES), jnp.int32)],
        compiler_params=_cparams(("arbitrary",)),
        name="plan",
    )(route, counts)


def _rows_wait(ref, n_rows, sem):
    pltpu.make_async_copy(ref.at[pl.ds(0, n_rows)], ref.at[pl.ds(0, n_rows)], sem).wait()


def _dispatch_body(pos_ref, tt_ref, h_ref, xs_hbm, zbuf, sem_z, sem_s):
    i = pl.program_id(0)
    tile_rows = MOE_TM * SUBLANES

    @pl.when(i == 0)
    def _():
        zbuf[...] = _pack_halves(jnp.zeros((tile_rows, 2 * LANES), F32))

        def zero_copy(tile):
            start = pl.multiple_of(tile * tile_rows, tile_rows)
            return pltpu.make_async_copy(zbuf, xs_hbm.at[pl.ds(start, tile_rows)], sem_z)

        def clear_start(tile, carry):
            @pl.when(tt_ref[tile, T_CLEAR] > 0)
            def _():
                zero_copy(tile).start()
            return carry

        def clear_wait(tile, carry):
            @pl.when(tt_ref[tile, T_CLEAR] > 0)
            def _():
                zero_copy(tile).wait()
            return carry

        lax.fori_loop(0, MOE_TILES, clear_start, 0)
        lax.fori_loop(0, MOE_TILES, clear_wait, 0)

    def tok(j, carry):
        src = h_ref.at[pl.ds(pl.multiple_of(j * SUBLANES, SUBLANES), SUBLANES)]
        pair = TOP_K * (i * DISPATCH_TM + j)
        for k in range(TOP_K):
            dst_row = pl.multiple_of(pos_ref[pair + k], SUBLANES)
            pltpu.make_async_copy(src, xs_hbm.at[pl.ds(dst_row, SUBLANES)], sem_s).start(priority=k)
        return carry

    lax.fori_loop(0, DISPATCH_TM, tok, 0, unroll=8)
    _rows_wait(xs_hbm, TOP_K * DISPATCH_TM * SUBLANES, sem_s)


def _dispatch(pos_rows, tile_clear, h2p):
    grid_spec = pltpu.PrefetchScalarGridSpec(
        num_scalar_prefetch=2,
        grid=(N_TOK // DISPATCH_TM,),
        in_specs=[pl.BlockSpec((DISPATCH_TM * SUBLANES, LANES), lambda i, *_: (i, 0))],
        out_specs=pl.BlockSpec(memory_space=pl.ANY),
        scratch_shapes=[pltpu.VMEM((MOE_TM * SUBLANES, LANES), U32),
                        pltpu.SemaphoreType.DMA(()), pltpu.SemaphoreType.DMA(())],
    )
    return pl.pallas_call(
        _dispatch_body,
        grid_spec=grid_spec,
        out_shape=jax.ShapeDtypeStruct((MOE_ROWS * SUBLANES, LANES), U32),
        compiler_params=_cparams(("arbitrary",)),
        name="dispatch",
    )(pos_rows, tile_clear, h2p)


def _moe_body(tt_ref, x_ref, wg_hbm, wu_hbm, wd_hbm, y_ref,
              wg_s, wu_s, wd_s, stg_g, stg_u, stg_d, sem):
    i = pl.program_id(0)

    def fetch(e, slot):
        return (pltpu.make_async_copy(wg_hbm.at[e], stg_g.at[slot], sem.at[slot, 0]),
                pltpu.make_async_copy(wu_hbm.at[e], stg_u.at[slot], sem.at[slot, 1]),
                pltpu.make_async_copy(wd_hbm.at[e], stg_d.at[slot], sem.at[slot, 2]))

    @pl.when(i == 0)
    def _():
        for cp in fetch(tt_ref[0, T_EXP], 0):
            cp.start()

        @pl.when(tt_ref[0, T_NEXT] < N_EXPERTS)
        def _():
            for cp in fetch(tt_ref[0, T_NEXT], 1):
                cp.start()

    @pl.when(tt_ref[i, T_FIRST] > 0)
    def _():
        slot = tt_ref[i, T_ORD] % 2
        for cp in fetch(tt_ref[i, T_EXP], slot):
            cp.wait()
        wg_s[...] = stg_g[slot].astype(BF16)
        wu_s[...] = stg_u[slot].astype(BF16)
        wd_s[...] = stg_d[slot].astype(BF16)

        @pl.when(tt_ref[i, T_NEXT2] < N_EXPERTS)
        def _():
            for cp in fetch(tt_ref[i, T_NEXT2], slot):
                cp.start()

    @pl.when(tt_ref[i, T_VALID] > 0)
    def _():
        lo, hi = _unpack_halves(_load_token_tiles(x_ref, 0, MOE_TM))
        xa = lo.astype(BF16)
        xb = hi.astype(BF16)
        g = (jnp.dot(xa, wg_s[:HALF_D, :], preferred_element_type=F32)
             + jnp.dot(xb, wg_s[HALF_D:, :], preferred_element_type=F32))
        u = (jnp.dot(xa, wu_s[:HALF_D, :], preferred_element_type=F32)
             + jnp.dot(xb, wu_s[HALF_D:, :], preferred_element_type=F32))
        hid = (g * jax.nn.sigmoid(g) * u).astype(BF16)
        _store_token_tiles(y_ref, _pack_halves(jnp.dot(hid, wd_s[...], preferred_element_type=F32)))

    @pl.when(tt_ref[i, T_VALID] == 0)
    def _():
        y_ref[...] = _pack_halves(jnp.zeros((MOE_TM * SUBLANES, 2 * LANES), F32))


def _moe(tile_tab, xs, wg, wu, wd):
    tm = MOE_TM
    grid_spec = pltpu.PrefetchScalarGridSpec(
        num_scalar_prefetch=1,
        grid=(MOE_TILES,),
        in_specs=[pl.BlockSpec((tm * SUBLANES, LANES), lambda i, *_: (i, 0)),
                  pl.BlockSpec(memory_space=pl.ANY), pl.BlockSpec(memory_space=pl.ANY),
                  pl.BlockSpec(memory_space=pl.ANY)],
        out_specs=pl.BlockSpec((tm * SUBLANES, LANES), lambda i, *_: (i, 0)),
        scratch_shapes=[pltpu.VMEM((D_MODEL, EXPERT_FF), BF16), pltpu.VMEM((D_MODEL, EXPERT_FF), BF16),
                        pltpu.VMEM((EXPERT_FF, D_MODEL), BF16),
                        pltpu.VMEM((2, D_MODEL, EXPERT_FF), F32), pltpu.VMEM((2, D_MODEL, EXPERT_FF), F32),
                        pltpu.VMEM((2, EXPERT_FF, D_MODEL), F32),
                        pltpu.SemaphoreType.DMA((2, 3))],
    )
    return pl.pallas_call(
        _moe_body,
        grid_spec=grid_spec,
        out_shape=jax.ShapeDtypeStruct((MOE_ROWS * SUBLANES, LANES), U32),
        compiler_params=_cparams(("arbitrary",)),
        name="moe",
    )(tile_tab, xs, wg, wu, wd)


def _final_body(pos_ref, x1_ref, mod_ref, rt_ref, g2_ref, b2_ref, ys_hbm, o_ref, buf, sem):
    i = pl.program_id(0)
    n = pl.num_programs(0)
    tm = FINAL_TM
    slot = i % 2
    nxt_slot = (i + 1) % 2
    nxt_tile = jnp.minimum(i + 1, n - 1)

    def issue(tile, dst_slot, j):
        pair = TOP_K * (tile * tm + j)
        for k in range(TOP_K):
            src_row = pl.multiple_of(pos_ref[pair + k], SUBLANES)
            dst_row = pl.multiple_of((k * tm + j) * SUBLANES, SUBLANES)
            pltpu.make_async_copy(ys_hbm.at[pl.ds(src_row, SUBLANES)],
                                  buf.at[dst_slot, pl.ds(dst_row, SUBLANES)], sem.at[dst_slot]).start(priority=k)

    def wait(s):
        pltpu.make_async_copy(ys_hbm.at[pl.ds(0, TOP_K * tm * SUBLANES)], buf.at[s], sem.at[s]).wait()

    @pl.when(i == 0)
    def _():
        def tok(j, carry):
            issue(0, 0, j)
            return carry
        lax.fori_loop(0, tm, tok, 0, unroll=8)

    wait(slot)
    gate2 = mod_ref[0, 5:6, :]
    cur = buf.at[slot]

    def chunk(c, carry):
        r0 = pl.multiple_of(c * FINAL_CHUNK, FINAL_CHUNK)
        rows = pl.ds(r0, FINAL_CHUNK)
        a_lo, a_hi = _unpack_halves(_load_token_tiles(cur, r0, FINAL_CHUNK))
        b_lo, b_hi = _unpack_halves(_load_token_tiles(cur, tm + r0, FINAL_CHUNK))
        x1 = x1_ref[rows, :]
        rt = rt_ref[rows, :]
        for r in range(FINAL_CHUNK):
            issue(nxt_tile, nxt_slot, r0 + r)
        w0 = rt[:, 2:3]
        w1 = rt[:, 3:4]
        y = jnp.concatenate([w0 * a_lo + w1 * b_lo, w0 * a_hi + w1 * b_hi], axis=1)
        o_ref[rows, :] = _ln_rows(DEEPNORM_ALPHA * x1 + gate2 * y) * g2_ref[...] + b2_ref[...]
        return carry

    lax.fori_loop(0, tm // FINAL_CHUNK, chunk, 0)

    @pl.when(i == n - 1)
    def _():
        wait(nxt_slot)


def _final(pos_rows, x1, mod3, route, g2, b2, ys):
    tm = FINAL_TM
    tpb = SEQ // tm
    row = lambda w: pl.BlockSpec((tm, w), lambda i, *_: (i, 0))
    grid_spec = pltpu.PrefetchScalarGridSpec(
        num_scalar_prefetch=1,
        grid=(N_TOK // tm,),
        in_specs=[row(D_MODEL),
                  pl.BlockSpec((1, N_MOD, D_MODEL), lambda i, *_: (i // tpb, 0, 0)),
                  row(LANES),
                  pl.BlockSpec(g2.shape, lambda i, *_: (0, 0)),
                  pl.BlockSpec(b2.shape, lambda i, *_: (0, 0)),
                  pl.BlockSpec(memory_space=pl.ANY)],
        out_specs=row(D_MODEL),
        scratch_shapes=[pltpu.VMEM((2, TOP_K * tm * SUBLANES, LANES), U32),
                        pltpu.SemaphoreType.DMA((2,))],
    )
    return pl.pallas_call(
        _final_body,
        grid_spec=grid_spec,
        out_shape=jax.ShapeDtypeStruct((N_TOK, D_MODEL), F32),
        compiler_params=_cparams(("arbitrary",)),
        name="final",
    )(pos_rows, x1, mod3, route, g2, b2, ys)


def kernel(x, c, positions, w_ada, b_ada, w_in, q_norm_g, w_uq, kv_norm_g, w_ukv, sgu_norm_g, sgu_norm_b,
           w_spatial, b_spatial, w_o, ln1_g, ln1_b, w_router_group, b_router_group, w_router_expert,
           b_router_expert, w_gate, w_up, w_down, ln2_g, ln2_b):
    l = 0
    x2 = x.reshape(N_TOK, D_MODEL)
    mod3 = _ada(c, w_ada[l], b_ada[l][None, :]).reshape(BATCH, N_MOD, D_MODEL)

    wq, wkv, wkpe, wz, wuq, wukv, wo = _prep(w_in[l], w_uq[l], w_ukv[l], w_o[l])
    n_r = N_GROUPS + N_EXPERTS
    wr = jnp.pad(jnp.concatenate([w_router_group[l], w_router_expert[l]], axis=1),
                 ((0, 0), (0, LANES - n_r))).astype(BF16)
    br = jnp.pad(jnp.concatenate([b_router_group[l], b_router_expert[l]]), (0, LANES - n_r))[None, :]
    inv_freq = 1.0 / (ROPE_THETA ** (jnp.arange(0, QK_ROPE_DIM, 2, dtype=F32) / QK_ROPE_DIM))
    invf = jnp.tile(inv_freq, 2 * LANES // QK_ROPE_DIM)[None, :]

    cqn, ckvn, kpe, u, vs = _inproj(x2, mod3, wq, wkv, wkpe, wz, q_norm_g[l][None, :], kv_norm_g[l][None, :],
                                    sgu_norm_g[l][None, :], sgu_norm_b[l][None, :])
    q, k, v = _qkv(cqn, ckvn, kpe, positions.reshape(N_TOK, 1), invf, wuq, wukv)
    attn = _attn(q, k, v).reshape(N_TOK, MLA_WIDTH)
    x1, h2, logits = _mixout(x2, mod3, attn, u, vs, w_spatial[l], b_spatial[l].T, wo,
                             ln1_g[l][None, :], ln1_b[l][None, :], wr, br)
    route, counts = _route(logits)
    pos_tab, tile_tab = _plan(route, counts)
    pos_rows = pos_tab[:, 0:TOP_K].reshape(-1)
    xs = _dispatch(pos_rows, tile_tab, h2)
    ys = _moe(tile_tab, xs, w_gate[l], w_up[l], w_down[l])
    out = _final(pos_rows, x1, mod3, route, ln2_g[l][None, :], ln2_b[l][None, :], ys)
    return out.reshape(BATCH, SEQ, D_MODEL)
```

```python
import functools

import jax
import jax.numpy as jnp
import numpy as np
from jax import lax
from jax.experimental import pallas as pl
from jax.experimental.pallas import tpu as pltpu

D_MODEL = 2048
BATCH = 4
SEQ = 2048
N_TOK = BATCH * SEQ

MLA_HEADS = 8
QK_NOPE_DIM = 128
QK_ROPE_DIM = 64
QK_DIM = QK_NOPE_DIM + QK_ROPE_DIM
V_HEAD_DIM = 128
Q_LORA_RANK = 768
KV_LORA_RANK = 512
ROPE_THETA = 10000.0
MLA_WIDTH = MLA_HEADS * V_HEAD_DIM

SGU_GROUPS = 8
SGU_GROUP_DIM = 128
SGU_CHUNK = 128
SGU_WIDTH = SGU_GROUPS * SGU_GROUP_DIM

N_GROUPS = 4
EXPERTS_PER_GROUP = 8
N_EXPERTS = N_GROUPS * EXPERTS_PER_GROUP
TOP_K = 2
EXPERT_FF = 512

DEEPNORM_ALPHA = 2.0 ** 0.25
EPS = 1e-6
N_MOD = 6
NEG_BIG = -1e30

LANES = 128
SUBLANES = 8
VMEM_LIMIT = 56 * 1024 * 1024

ADA_TN = 1024
TOK_TM = 512
MIX_TM = 256
PREP_STEPS = 8
ATT_TQ = 512
ATT_TK = 512
ATT_HEADS = 2
MOE_TM = 256
MOE_TILES = (N_TOK * TOP_K + N_EXPERTS * (MOE_TM - 1)) // MOE_TM + 1
MOE_ROWS = MOE_TILES * MOE_TM
PLAN_TM = 2048
ROUTE_TM = 1024
RT_RANK = 4
DISPATCH_TM = 1024
FINAL_TM = 256
FINAL_CHUNK = 128
FINAL_SLOTS = 3
assert MOE_TILES <= LANES
T_EXP, T_VALID, T_FIRST, T_CLEAR, T_NEXT, T_NEXT2, T_ORD = range(7)

F32 = jnp.float32
BF16 = jnp.bfloat16
U32 = jnp.uint32
HALF_D = D_MODEL // 2


def _cparams(sem):
    return pltpu.CompilerParams(dimension_semantics=sem, vmem_limit_bytes=VMEM_LIMIT)


def _const_spec(shape):
    nd = len(shape)
    return pl.BlockSpec(shape, lambda *_: (0,) * nd, pipeline_mode=pl.Buffered(1))


def _ln_rows(x):
    mu = jnp.mean(x, axis=-1, keepdims=True)
    xc = x - mu
    var = jnp.mean(xc * xc, axis=-1, keepdims=True)
    return xc * lax.rsqrt(var + EPS)


def _rms_rows(x):
    return x * lax.rsqrt(jnp.mean(x * x, axis=-1, keepdims=True) + EPS)


def _pack_halves(x):
    half = x.shape[-1] // 2
    return pltpu.pack_elementwise([x[:, :half], x[:, half:]], packed_dtype=BF16)


def _unpack_halves(w):
    lo = pltpu.unpack_elementwise(w, index=0, packed_dtype=BF16, unpacked_dtype=F32)
    hi = pltpu.unpack_elementwise(w, index=1, packed_dtype=BF16, unpacked_dtype=F32)
    return lo, hi


def _store_token_tiles(ref, w):
    rows = w.shape[0]
    for s in range(SUBLANES):
        ref[pl.ds(s, rows, stride=SUBLANES), :] = w[:, s * LANES:(s + 1) * LANES]


def _load_token_tiles(ref, start_row, rows):
    return jnp.concatenate([ref[pl.ds(start_row * SUBLANES + s, rows, stride=SUBLANES), :]
                            for s in range(SUBLANES)], axis=1)


def _gelu_tanh(x):
    c = np.sqrt(2.0 / np.pi).astype(np.float32)
    return 0.5 * x * (1.0 + jnp.tanh(c * (x + 0.044715 * (x * x * x))))


def _ada_body(c_ref, w_ref, b_ref, o_ref):
    o_ref[...] = jnp.dot(c_ref[...].astype(BF16), w_ref[...].astype(BF16),
                         preferred_element_type=F32) + b_ref[...]


def _ada(c, w, b):
    n = w.shape[1]
    return pl.pallas_call(
        _ada_body,
        grid=(n // ADA_TN,),
        in_specs=[pl.BlockSpec((BATCH, D_MODEL), lambda j: (0, 0)),
                  pl.BlockSpec((D_MODEL, ADA_TN), lambda j: (0, j)),
                  pl.BlockSpec((1, ADA_TN), lambda j: (0, j))],
        out_specs=pl.BlockSpec((BATCH, ADA_TN), lambda j: (0, j)),
        out_shape=jax.ShapeDtypeStruct((BATCH, n), F32),
        compiler_params=_cparams(("parallel",)),
        name="ada",
    )(c, w, b)


def _prep_body(win_ref, wuq_ref, wukv_ref, wo_ref, wq_o, wkv_o, wkpe_o, wz_o, wuq_o, wukv_o, wo_o):
    o1, o2, o3 = Q_LORA_RANK, Q_LORA_RANK + KV_LORA_RANK, Q_LORA_RANK + KV_LORA_RANK + QK_ROPE_DIM
    w = win_ref[...]
    wq_o[...] = w[:, :o1].astype(BF16)
    wkv_o[...] = w[:, o1:o2].astype(BF16)
    kpe = w[:, o2:o3]
    wkpe_o[...] = jnp.concatenate([kpe, jnp.zeros_like(kpe)], axis=1).astype(BF16)
    wz_o[...] = w[:, o3:].astype(BF16)
    u = wuq_ref[...]
    nope = [u[:, h * QK_DIM:h * QK_DIM + QK_NOPE_DIM] for h in range(MLA_HEADS)]
    rope = [u[:, h * QK_DIM + QK_NOPE_DIM:(h + 1) * QK_DIM] for h in range(MLA_HEADS)]
    wuq_o[...] = jnp.concatenate(nope + rope, axis=1).astype(BF16)
    kv = wukv_ref[...]
    hw = QK_NOPE_DIM + V_HEAD_DIM
    kn = [kv[:, h * hw:h * hw + QK_NOPE_DIM] for h in range(MLA_HEADS)]
    vv = [kv[:, h * hw + QK_NOPE_DIM:(h + 1) * hw] for h in range(MLA_HEADS)]
    wukv_o[...] = jnp.concatenate(kn + vv, axis=1).astype(BF16)
    wo_o[...] = wo_ref[...].astype(BF16)


def _prep(w_in, w_uq, w_ukv, w_o):
    steps = PREP_STEPS
    zw = w_in.shape[1] - (Q_LORA_RANK + KV_LORA_RANK + QK_ROPE_DIM)
    blk = lambda a: pl.BlockSpec((a.shape[0] // steps, a.shape[1]), lambda i: (i, 0))
    out_cols = (Q_LORA_RANK, KV_LORA_RANK, LANES, zw)
    outs = ([(D_MODEL, c) for c in out_cols] + [w_uq.shape, w_ukv.shape, w_o.shape])
    return pl.pallas_call(
        _prep_body,
        grid=(steps,),
        in_specs=[blk(w_in), blk(w_uq), blk(w_ukv), blk(w_o)],
        out_specs=[pl.BlockSpec((r // steps, c), lambda i: (i, 0)) for r, c in outs],
        out_shape=[jax.ShapeDtypeStruct(s, BF16) for s in outs],
        compiler_params=_cparams(("parallel",)),
        name="prep",
    )(w_in, w_uq, w_ukv, w_o)


def _inproj_body(x_ref, mod_ref, wq_ref, wkv_ref, wkpe_ref, wz_ref, gq_ref, gkv_ref, sg_ref, sb_ref,
                 cq_ref, ckv_ref, kpe_ref, u_ref, vs_ref):
    sh = mod_ref[0, 0:1, :]
    sc = mod_ref[0, 1:2, :]
    h = (_ln_rows(x_ref[...]) * (1.0 + sc) + sh).astype(BF16)
    cq = jnp.dot(h, wq_ref[...], preferred_element_type=F32)
    cq_ref[...] = (_rms_rows(cq) * gq_ref[...]).astype(BF16)
    ckv = jnp.dot(h, wkv_ref[...], preferred_element_type=F32)
    ckv_ref[...] = (_rms_rows(ckv) * gkv_ref[...]).astype(BF16)
    kpe_ref[...] = jnp.dot(h, wkpe_ref[...], preferred_element_type=F32)
    gz = _gelu_tanh(jnp.dot(h, wz_ref[...], preferred_element_type=F32))
    u_ref[...] = gz[:, :SGU_WIDTH]
    vs_ref[...] = (_ln_rows(gz[:, SGU_WIDTH:]) * sg_ref[...] + sb_ref[...]).astype(BF16)


def _inproj(x2, mod3, wq, wkv, wkpe, wz, gq, gkv, sg, sb):
    tm = TOK_TM
    tiles_per_batch = SEQ // tm
    row = lambda w: pl.BlockSpec((tm, w), lambda i: (i, 0))
    return pl.pallas_call(
        _inproj_body,
        grid=(N_TOK // tm,),
        in_specs=[row(D_MODEL),
                  pl.BlockSpec((1, N_MOD, D_MODEL), lambda i: (i // tiles_per_batch, 0, 0)),
                  _const_spec(wq.shape), _const_spec(wkv.shape), _const_spec(wkpe.shape), _const_spec(wz.shape),
                  _const_spec(gq.shape), _const_spec(gkv.shape), _const_spec(sg.shape), _const_spec(sb.shape)],
        out_specs=[row(Q_LORA_RANK), row(KV_LORA_RANK), row(LANES), row(SGU_WIDTH), row(SGU_WIDTH)],
        out_shape=[jax.ShapeDtypeStruct((N_TOK, Q_LORA_RANK), BF16),
                   jax.ShapeDtypeStruct((N_TOK, KV_LORA_RANK), BF16),
                   jax.ShapeDtypeStruct((N_TOK, LANES), F32),
                   jax.ShapeDtypeStruct((N_TOK, SGU_WIDTH), F32),
                   jax.ShapeDtypeStruct((N_TOK, SGU_WIDTH), BF16)],
        compiler_params=_cparams(("parallel",)),
        name="inproj",
    )(x2, mod3, wq, wkv, wkpe, wz, gq, gkv, sg, sb)


def _rope(x, cos, sin):
    w = x.shape[-1]
    lane = lax.broadcasted_iota(jnp.int32, x.shape, 1)
    first_half = (lane % QK_ROPE_DIM) < (QK_ROPE_DIM // 2)
    rot = jnp.where(first_half,
                    -pltpu.roll(x, w - QK_ROPE_DIM // 2, 1),
                    pltpu.roll(x, QK_ROPE_DIM // 2, 1))
    return x * cos + rot * sin


def _qkv_body(cq_ref, ckv_ref, kpe_ref, pos_ref, invf_ref, wuq_ref, wukv_ref, q_ref, k_ref, v_ref):
    ang = pos_ref[...].astype(F32) * invf_ref[...]
    cos1 = jnp.cos(ang)
    sin1 = jnp.sin(ang)
    reps = MLA_HEADS * QK_ROPE_DIM // LANES
    cos = jnp.concatenate([cos1] * reps, axis=1)
    sin = jnp.concatenate([sin1] * reps, axis=1)
    scale = np.float32(QK_DIM ** -0.5)
    q = jnp.dot(cq_ref[...], wuq_ref[...], preferred_element_type=F32) * scale
    q_pe = _rope(q[:, MLA_HEADS * QK_NOPE_DIM:], cos, sin)
    kv = jnp.dot(ckv_ref[...], wukv_ref[...], preferred_element_type=F32)
    k_pe = _rope(kpe_ref[...], cos1, sin1)[:, :QK_ROPE_DIM].astype(BF16)
    for h in range(MLA_HEADS):
        q_ref[0, h, :, 0:QK_NOPE_DIM] = q[:, h * QK_NOPE_DIM:(h + 1) * QK_NOPE_DIM].astype(BF16)
        q_ref[0, h, :, QK_NOPE_DIM:QK_DIM] = q_pe[:, h * QK_ROPE_DIM:(h + 1) * QK_ROPE_DIM].astype(BF16)
        k_ref[0, h, :, 0:QK_NOPE_DIM] = kv[:, h * QK_NOPE_DIM:(h + 1) * QK_NOPE_DIM].astype(BF16)
        k_ref[0, h, :, QK_NOPE_DIM:QK_DIM] = k_pe
        v_ref[0, h, :, :] = kv[:, MLA_WIDTH + h * V_HEAD_DIM:MLA_WIDTH + (h + 1) * V_HEAD_DIM].astype(BF16)


def _qkv(cqn, ckvn, kpe, pos2, invf, wuq, wukv):
    tm = TOK_TM
    tpb = SEQ // tm
    row = lambda w: pl.BlockSpec((tm, w), lambda i: (i, 0))
    head_out = lambda w: pl.BlockSpec((1, MLA_HEADS, tm, w), lambda i: (i // tpb, 0, i % tpb, 0))
    return pl.pallas_call(
        _qkv_body,
        grid=(N_TOK // tm,),
        in_specs=[row(Q_LORA_RANK), row(KV_LORA_RANK), row(LANES), row(1),
                  _const_spec(invf.shape), _const_spec(wuq.shape), _const_spec(wukv.shape)],
        out_specs=[head_out(QK_DIM), head_out(QK_DIM), head_out(V_HEAD_DIM)],
        out_shape=[jax.ShapeDtypeStruct((BATCH, MLA_HEADS, SEQ, QK_DIM), BF16),
                   jax.ShapeDtypeStruct((BATCH, MLA_HEADS, SEQ, QK_DIM), BF16),
                   jax.ShapeDtypeStruct((BATCH, MLA_HEADS, SEQ, V_HEAD_DIM), BF16)],
        compiler_params=_cparams(("parallel",)),
        name="qkv",
    )(cqn, ckvn, kpe, pos2, invf, wuq, wukv)


def _attn_body(q_ref, k_ref, v_ref, o_ref):
    i = pl.program_id(2)

    def step(h, j, carry, masked):
        m, l, acc = carry
        start = pl.multiple_of(j * ATT_TK, ATT_TK)
        k = k_ref[0, h, pl.ds(start, ATT_TK), :]
        v = v_ref[0, h, pl.ds(start, ATT_TK), :]
        s = lax.dot_general(q_ref[0, h], k, (((1,), (1,)), ((), ())), preferred_element_type=F32)
        if masked:
            r = lax.broadcasted_iota(jnp.int32, s.shape, 0)
            c = lax.broadcasted_iota(jnp.int32, s.shape, 1)
            s = jnp.where(c <= r, s, NEG_BIG)
        m_new = jnp.maximum(m, jnp.max(s, axis=-1, keepdims=True))
        p = jnp.exp(s - m_new)
        a = jnp.exp(m - m_new)
        l = a * l + jnp.sum(p, axis=-1, keepdims=True)
        acc = a * acc + jnp.dot(p.astype(BF16), v, preferred_element_type=F32)
        return m_new, l, acc

    def steps(j, carries, masked):
        return tuple(step(h, j, carries[h], masked) for h in range(ATT_HEADS))

    init = tuple((jnp.full((ATT_TQ, 1), NEG_BIG, F32), jnp.zeros((ATT_TQ, 1), F32),
                  jnp.zeros((ATT_TQ, V_HEAD_DIM), F32)) for _ in range(ATT_HEADS))
    carries = lax.fori_loop(0, i, lambda j, c: steps(j, c, False), init)
    carries = steps(i, carries, True)
    for h, (m, l, acc) in enumerate(carries):
        o_ref[0, :, h * V_HEAD_DIM:(h + 1) * V_HEAD_DIM] = (acc / l).astype(BF16)


def _attn(q, k, v):
    assert ATT_TQ == ATT_TK
    hb = ATT_HEADS
    return pl.pallas_call(
        _attn_body,
        grid=(BATCH, MLA_HEADS // hb, SEQ // ATT_TQ),
        in_specs=[pl.BlockSpec((1, hb, ATT_TQ, QK_DIM), lambda b, h, i: (b, h, i, 0)),
                  pl.BlockSpec((1, hb, SEQ, QK_DIM), lambda b, h, i: (b, h, 0, 0)),
                  pl.BlockSpec((1, hb, SEQ, V_HEAD_DIM), lambda b, h, i: (b, h, 0, 0))],
        out_specs=pl.BlockSpec((1, ATT_TQ, hb * V_HEAD_DIM), lambda b, h, i: (b, i, h)),
        out_shape=jax.ShapeDtypeStruct((BATCH, SEQ, MLA_WIDTH), BF16),
        compiler_params=_cparams(("parallel", "parallel", "arbitrary")),
        name="attn",
    )(q, k, v)


def _mixout_body(x_ref, mod_ref, attn_ref, u_ref, vs_ref, wsp_ref, bsp_ref, woa_ref, wos_ref,
                 g1_ref, b1_ref, wr_ref, br_ref, x1_ref, h2_ref, lg_ref, sgu_scr):
    r = lax.broadcasted_iota(jnp.int32, (SGU_CHUNK, SGU_CHUNK), 0)
    c = lax.broadcasted_iota(jnp.int32, (SGU_CHUNK, SGU_CHUNK), 1)
    causal = c <= r
    for g in range(SGU_GROUPS):
        ws = jnp.where(causal, wsp_ref[g], 0.0).astype(BF16)
        bias = bsp_ref[:, g:g + 1]
        cols = slice(g * SGU_GROUP_DIM, (g + 1) * SGU_GROUP_DIM)
        for ch in range(MIX_TM // SGU_CHUNK):
            rows = slice(ch * SGU_CHUNK, (ch + 1) * SGU_CHUNK)
            mixed = jnp.dot(ws, vs_ref[rows, cols], preferred_element_type=F32) + bias
            sgu_scr[rows, cols] = (u_ref[rows, cols] * mixed).astype(BF16)
    y = (jnp.dot(attn_ref[...], woa_ref[...], preferred_element_type=F32)
         + jnp.dot(sgu_scr[...], wos_ref[...], preferred_element_type=F32))
    gate1 = mod_ref[0, 2:3, :]
    sh2 = mod_ref[0, 3:4, :]
    sc2 = mod_ref[0, 4:5, :]
    x1 = _ln_rows(DEEPNORM_ALPHA * x_ref[...] + gate1 * y) * g1_ref[...] + b1_ref[...]
    x1_ref[...] = x1
    h2 = _ln_rows(x1) * (1.0 + sc2) + sh2
    _store_token_tiles(h2_ref, _pack_halves(h2))
    lg_ref[...] = jnp.dot(h2.astype(BF16), wr_ref[...], preferred_element_type=F32) + br_ref[...]


def _mixout(x2, mod3, attn, u, vs, wsp, bsp_t, wo, g1, b1, wr, br):
    tm = MIX_TM
    tpb = SEQ // tm
    row = lambda w: pl.BlockSpec((tm, w), lambda i: (i, 0))
    wo_half = lambda j: pl.BlockSpec((MLA_WIDTH, D_MODEL), lambda i: (j, 0), pipeline_mode=pl.Buffered(1))
    return pl.pallas_call(
        _mixout_body,
        grid=(N_TOK // tm,),
        in_specs=[row(D_MODEL),
                  pl.BlockSpec((1, N_MOD, D_MODEL), lambda i: (i // tpb, 0, 0)),
                  row(MLA_WIDTH), row(SGU_WIDTH), row(SGU_WIDTH),
                  _const_spec(wsp.shape), _const_spec(bsp_t.shape), wo_half(0), wo_half(1),
                  _const_spec(g1.shape), _const_spec(b1.shape), _const_spec(wr.shape), _const_spec(br.shape)],
        out_specs=[row(D_MODEL), pl.BlockSpec((tm * SUBLANES, LANES), lambda i: (i, 0)), row(LANES)],
        out_shape=[jax.ShapeDtypeStruct((N_TOK, D_MODEL), F32),
                   jax.ShapeDtypeStruct((N_TOK * SUBLANES, LANES), U32),
                   jax.ShapeDtypeStruct((N_TOK, LANES), F32)],
        scratch_shapes=[pltpu.VMEM((tm, SGU_WIDTH), BF16)],
        compiler_params=_cparams(("parallel",)),
        name="mix_out",
    )(x2, mod3, attn, u, vs, wsp, bsp_t, wo, wo, g1, b1, wr, br)


def _route_math(lg):
    lane = lax.broadcasted_iota(jnp.int32, lg.shape, 1)
    big = jnp.int32(LANES)

    def top1(vals):
        m = jnp.max(vals, axis=-1, keepdims=True)
        idx = jnp.min(jnp.where(vals == m, lane, big), axis=-1, keepdims=True)
        return m, idx

    is_group = lane < N_GROUPS
    glog = jnp.where(is_group, lg, -jnp.inf)
    gmax, gidx = top1(glog)
    pg_top = 1.0 / jnp.sum(jnp.exp(glog - gmax), axis=-1, keepdims=True)
    eid = lane - N_GROUPS
    sel = (eid >= gidx * EXPERTS_PER_GROUP) & (eid < (gidx + 1) * EXPERTS_PER_GROUP)
    elog = jnp.where(sel, lg, -jnp.inf)
    m1, i1 = top1(elog)
    m2, i2 = top1(jnp.where(lane == i1, -jnp.inf, elog))
    e2 = jnp.exp(m2 - m1)
    w1 = pg_top / (1.0 + e2)
    w2 = pg_top * e2 / (1.0 + e2)
    return jnp.where(lane == 0, (i1 - N_GROUPS).astype(F32),
                     jnp.where(lane == 1, (i2 - N_GROUPS).astype(F32),
                               jnp.where(lane == 2, w1, jnp.where(lane == 3, w2, 0.0))))


def _rank_math(rt, counts):
    t = rt.shape[0]
    lane = lax.broadcasted_iota(jnp.int32, (t, LANES), 1).astype(F32)
    oh0 = lane == rt[:, 0:1]
    oh1 = lane == rt[:, 1:2]
    s = jnp.where(oh0 | oh1, 1.0, 0.0)
    r = lax.broadcasted_iota(jnp.int32, (t, t), 0)
    c = lax.broadcasted_iota(jnp.int32, (t, t), 1)
    before = jnp.where(c < r, 1.0, 0.0).astype(BF16)
    csum = jnp.dot(before, s.astype(BF16), preferred_element_type=F32) + counts
    rank0 = jnp.sum(jnp.where(oh0, csum, 0.0), axis=-1, keepdims=True)
    rank1 = jnp.sum(jnp.where(oh1, csum, 0.0), axis=-1, keepdims=True)
    return rank0, rank1, counts + jnp.sum(s, axis=0, keepdims=True)


def _route_body(lg_ref, rt_ref, cnt_ref, cnt_scr):
    @pl.when(pl.program_id(0) == 0)
    def _():
        cnt_scr[...] = jnp.zeros_like(cnt_scr)

    rt = _route_math(lg_ref[...])
    rank0, rank1, counts = _rank_math(rt, cnt_scr[...])
    cnt_scr[...] = counts
    lane = lax.broadcasted_iota(jnp.int32, rt.shape, 1)
    rt_ref[...] = jnp.where(lane == RT_RANK, rank0, jnp.where(lane == RT_RANK + 1, rank1, rt))
    cnt_ref[...] = jnp.broadcast_to(counts, cnt_ref.shape)


def _route(logits):
    tm = ROUTE_TM
    return pl.pallas_call(
        _route_body,
        grid=(N_TOK // tm,),
        in_specs=[pl.BlockSpec((tm, LANES), lambda i: (i, 0))],
        out_specs=[pl.BlockSpec((tm, LANES), lambda i: (i, 0)),
                   pl.BlockSpec((SUBLANES, LANES), lambda i: (0, 0))],
        out_shape=[jax.ShapeDtypeStruct((N_TOK, LANES), F32),
                   jax.ShapeDtypeStruct((SUBLANES, LANES), F32)],
        scratch_shapes=[pltpu.VMEM((1, LANES), F32)],
        compiler_params=_cparams(("arbitrary",)),
        name="route",
    )(logits)


def _plan_body(rt_ref, cnt_ref, pos_ref, tt_ref):
    t = PLAN_TM
    lane = lax.broadcasted_iota(jnp.int32, (t, LANES), 1)
    rt = rt_ref[...]
    oh0 = lane.astype(F32) == rt[:, 0:1]
    oh1 = lane.astype(F32) == rt[:, 1:2]
    if True:
        counts = cnt_ref[0:1, :]
        tiles = jnp.floor((counts + (MOE_TM - 1)) * (1.0 / MOE_TM))
        r = lax.broadcasted_iota(jnp.int32, (LANES, LANES), 0)
        c = lax.broadcasted_iota(jnp.int32, (LANES, LANES), 1)
        upto = jnp.where(r <= c, 1.0, 0.0).astype(BF16)
        tiles8 = jnp.broadcast_to(tiles, (8, LANES)).astype(BF16)
        tile_end = jnp.dot(tiles8, upto, preferred_element_type=F32)[0:1]
        offs = (tile_end - tiles) * MOE_TM
        p0 = jnp.sum(jnp.where(oh0, offs, 0.0), axis=-1, keepdims=True) + rt[:, RT_RANK:RT_RANK + 1]
        p1 = jnp.sum(jnp.where(oh1, offs, 0.0), axis=-1, keepdims=True) + rt[:, RT_RANK + 1:RT_RANK + 2]
        pos_ref[...] = (jnp.where(lane == 0, p0, jnp.where(lane == 1, p1, 0.0)) * SUBLANES).astype(jnp.int32)

        lane_e = lax.broadcasted_iota(jnp.int32, (LANES, LANES), 1)
        tile_id = lax.broadcasted_iota(jnp.int32, (LANES, LANES), 0).astype(F32)
        is_e = lane_e < N_EXPERTS
        total = jnp.max(tile_end, axis=-1, keepdims=True)
        t_exp = jnp.sum(jnp.where(is_e & (tile_end <= tile_id), 1.0, 0.0), axis=-1, keepdims=True)
        t_valid = jnp.where(tile_id[:, 0:1] < total, 1.0, 0.0)
        last_exp = jnp.sum(jnp.where(is_e & (tile_end <= total - 1.0), 1.0, 0.0), axis=-1, keepdims=True)
        t_exp = jnp.where(t_valid > 0, t_exp, last_exp)
        t_first = jnp.sum(jnp.where(is_e & (tiles > 0) & ((tile_end - tiles) == tile_id), 1.0, 0.0),
                          axis=-1, keepdims=True)
        t_last = jnp.sum(jnp.where(is_e & (tiles > 0) & ((tile_end - 1.0) == tile_id), 1.0, 0.0),
                         axis=-1, keepdims=True)
        t_clear = jnp.maximum(t_last, 1.0 - t_valid)
        none = jnp.float32(LANES)
        owns = is_e & (tiles > 0)
        lane_f = lane_e.astype(F32)
        t_next = jnp.min(jnp.where(owns & (lane_f > t_exp), lane_f, none), axis=-1, keepdims=True)
        t_next2 = jnp.min(jnp.where(owns & (lane_f > t_next), lane_f, none), axis=-1, keepdims=True)
        t_ord = jnp.sum(jnp.where(owns & (lane_f < t_exp), 1.0, 0.0), axis=-1, keepdims=True)
        cols = {T_EXP: t_exp, T_VALID: t_valid, T_FIRST: t_first, T_CLEAR: t_clear,
                T_NEXT: t_next, T_NEXT2: t_next2, T_ORD: t_ord}
        table = jnp.zeros((LANES, LANES), F32)
        for k, col in cols.items():
            table = jnp.where(lane_e == k, col, table)
        tt_ref[...] = table.astype(jnp.int32)


def _plan(route, counts):
    t = PLAN_TM
    return pl.pallas_call(
        _plan_body,
        grid=(N_TOK // t,),
        in_specs=[pl.BlockSpec((t, LANES), lambda i: (i, 0)),
                  pl.BlockSpec((SUBLANES, LANES), lambda i: (0, 0))],
        out_specs=[pl.BlockSpec((t, LANES), lambda i: (i, 0)),
                   pl.BlockSpec((LANES, LANES), lambda i: (0, 0))],
        out_shape=[jax.ShapeDtypeStruct((N_TOK, LANES), jnp.int32),
                   jax.ShapeDtypeStruct((LANES, LANES), jnp.int32)],
        compiler_params=_cparams(("arbitrary",)),
        name="plan",
    )(route, counts)


def _rows_wait(ref, n_rows, sem):
    pltpu.make_async_copy(ref.at[pl.ds(0, n_rows)], ref.at[pl.ds(0, n_rows)], sem).wait()


def _dispatch_body(pos_ref, tt_ref, h_ref, xs_hbm, zbuf, sem_z, sem_s):
    i = pl.program_id(0)
    tile_rows = MOE_TM * SUBLANES

    @pl.when(i == 0)
    def _():
        zbuf[...] = _pack_halves(jnp.zeros((tile_rows, 2 * LANES), F32))

        def zero_copy(tile):
            start = pl.multiple_of(tile * tile_rows, tile_rows)
            return pltpu.make_async_copy(zbuf, xs_hbm.at[pl.ds(start, tile_rows)], sem_z)

        def clear_start(tile, carry):
            @pl.when(tt_ref[tile, T_CLEAR] > 0)
            def _():
                zero_copy(tile).start()
            return carry

        def clear_wait(tile, carry):
            @pl.when(tt_ref[tile, T_CLEAR] > 0)
            def _():
                zero_copy(tile).wait()
            return carry

        lax.fori_loop(0, MOE_TILES, clear_start, 0)
        lax.fori_loop(0, MOE_TILES, clear_wait, 0)

    def tok(j, carry):
        src = h_ref.at[pl.ds(pl.multiple_of(j * SUBLANES, SUBLANES), SUBLANES)]
        pair = TOP_K * (i * DISPATCH_TM + j)
        for k in range(TOP_K):
            dst_row = pl.multiple_of(pos_ref[pair + k], SUBLANES)
            pltpu.make_async_copy(src, xs_hbm.at[pl.ds(dst_row, SUBLANES)], sem_s).start(priority=k)
        return carry

    lax.fori_loop(0, DISPATCH_TM, tok, 0, unroll=8)
    _rows_wait(xs_hbm, TOP_K * DISPATCH_TM * SUBLANES, sem_s)


def _dispatch(pos_rows, tile_clear, h2p):
    grid_spec = pltpu.PrefetchScalarGridSpec(
        num_scalar_prefetch=2,
        grid=(N_TOK // DISPATCH_TM,),
        in_specs=[pl.BlockSpec((DISPATCH_TM * SUBLANES, LANES), lambda i, *_: (i, 0))],
        out_specs=pl.BlockSpec(memory_space=pl.ANY),
        scratch_shapes=[pltpu.VMEM((MOE_TM * SUBLANES, LANES), U32),
                        pltpu.SemaphoreType.DMA(()), pltpu.SemaphoreType.DMA(())],
    )
    return pl.pallas_call(
        _dispatch_body,
        grid_spec=grid_spec,
        out_shape=jax.ShapeDtypeStruct((MOE_ROWS * SUBLANES, LANES), U32),
        compiler_params=_cparams(("arbitrary",)),
        name="dispatch",
    )(pos_rows, tile_clear, h2p)


def _moe_body(tt_ref, x_ref, wg_hbm, wu_hbm, wd_hbm, y_ref,
              wg_s, wu_s, wd_s, stg_g, stg_u, stg_d, sem):
    i = pl.program_id(0)

    def fetch(e, slot):
        return (pltpu.make_async_copy(wg_hbm.at[e], stg_g.at[slot], sem.at[slot, 0]),
                pltpu.make_async_copy(wu_hbm.at[e], stg_u.at[slot], sem.at[slot, 1]),
                pltpu.make_async_copy(wd_hbm.at[e], stg_d.at[slot], sem.at[slot, 2]))

    @pl.when(i == 0)
    def _():
        for cp in fetch(tt_ref[0, T_EXP], 0):
            cp.start()

        @pl.when(tt_ref[0, T_NEXT] < N_EXPERTS)
        def _():
            for cp in fetch(tt_ref[0, T_NEXT], 1):
                cp.start()

    @pl.when(tt_ref[i, T_FIRST] > 0)
    def _():
        slot = tt_ref[i, T_ORD] % 2
        for cp in fetch(tt_ref[i, T_EXP], slot):
            cp.wait()
        wg_s[...] = stg_g[slot].astype(BF16)
        wu_s[...] = stg_u[slot].astype(BF16)
        wd_s[...] = stg_d[slot].astype(BF16)

        @pl.when(tt_ref[i, T_NEXT2] < N_EXPERTS)
        def _():
            for cp in fetch(tt_ref[i, T_NEXT2], slot):
                cp.start()

    @pl.when(tt_ref[i, T_VALID] > 0)
    def _():
        lo, hi = _unpack_halves(_load_token_tiles(x_ref, 0, MOE_TM))
        xa = lo.astype(BF16)
        xb = hi.astype(BF16)
        g = (jnp.dot(xa, wg_s[:HALF_D, :], preferred_element_type=F32)
             + jnp.dot(xb, wg_s[HALF_D:, :], preferred_element_type=F32))
        u = (jnp.dot(xa, wu_s[:HALF_D, :], preferred_element_type=F32)
             + jnp.dot(xb, wu_s[HALF_D:, :], preferred_element_type=F32))
        hid = (g * jax.nn.sigmoid(g) * u).astype(BF16)
        _store_token_tiles(y_ref, _pack_halves(jnp.dot(hid, wd_s[...], preferred_element_type=F32)))

    @pl.when(tt_ref[i, T_VALID] == 0)
    def _():
        y_ref[...] = _pack_halves(jnp.zeros((MOE_TM * SUBLANES, 2 * LANES), F32))


def _moe(tile_tab, xs, wg, wu, wd):
    tm = MOE_TM
    grid_spec = pltpu.PrefetchScalarGridSpec(
        num_scalar_prefetch=1,
        grid=(MOE_TILES,),
        in_specs=[pl.BlockSpec((tm * SUBLANES, LANES), lambda i, *_: (i, 0)),
                  pl.BlockSpec(memory_space=pl.ANY), pl.BlockSpec(memory_space=pl.ANY),
                  pl.BlockSpec(memory_space=pl.ANY)],
        out_specs=pl.BlockSpec((tm * SUBLANES, LANES), lambda i, *_: (i, 0)),
        scratch_shapes=[pltpu.VMEM((D_MODEL, EXPERT_FF), BF16), pltpu.VMEM((D_MODEL, EXPERT_FF), BF16),
                        pltpu.VMEM((EXPERT_FF, D_MODEL), BF16),
                        pltpu.VMEM((2, D_MODEL, EXPERT_FF), F32), pltpu.VMEM((2, D_MODEL, EXPERT_FF), F32),
                        pltpu.VMEM((2, EXPERT_FF, D_MODEL), F32),
                        pltpu.SemaphoreType.DMA((2, 3))],
    )
    return pl.pallas_call(
        _moe_body,
        grid_spec=grid_spec,
        out_shape=jax.ShapeDtypeStruct((MOE_ROWS * SUBLANES, LANES), U32),
        compiler_params=_cparams(("arbitrary",)),
        name="moe",
    )(tile_tab, xs, wg, wu, wd)


def _final_body(pos_ref, x1_ref, mod_ref, rt_ref, g2_ref, b2_ref, ys_hbm, o_ref, buf, sem):
    i = pl.program_id(0)
    n = pl.num_programs(0)
    tm = FINAL_TM
    slot = i % FINAL_SLOTS
    nxt_slot = (i + 2) % FINAL_SLOTS
    nxt_tile = jnp.minimum(i + 2, n - 1)

    def issue(tile, dst_slot, j):
        pair = TOP_K * (tile * tm + j)
        for k in range(TOP_K):
            src_row = pl.multiple_of(pos_ref[pair + k], SUBLANES)
            dst_row = pl.multiple_of((k * tm + j) * SUBLANES, SUBLANES)
            pltpu.make_async_copy(ys_hbm.at[pl.ds(src_row, SUBLANES)],
                                  buf.at[dst_slot, pl.ds(dst_row, SUBLANES)], sem.at[dst_slot]).start(priority=k)

    def wait(s):
        pltpu.make_async_copy(ys_hbm.at[pl.ds(0, TOP_K * tm * SUBLANES)], buf.at[s], sem.at[s]).wait()

    @pl.when(i == 0)
    def _():
        def tok(j, carry):
            issue(0, 0, j)
            issue(jnp.minimum(1, n - 1), 1, j)
            return carry
        lax.fori_loop(0, tm, tok, 0, unroll=8)

    wait(slot)
    gate2 = mod_ref[0, 5:6, :]
    cur = buf.at[slot]

    def chunk(c, carry):
        r0 = pl.multiple_of(c * FINAL_CHUNK, FINAL_CHUNK)
        rows = pl.ds(r0, FINAL_CHUNK)
        a_lo, a_hi = _unpack_halves(_load_token_tiles(cur, r0, FINAL_CHUNK))
        b_lo, b_hi = _unpack_halves(_load_token_tiles(cur, tm + r0, FINAL_CHUNK))
        x1 = x1_ref[rows, :]
        rt = rt_ref[rows, :]
        for r in range(FINAL_CHUNK):
            issue(nxt_tile, nxt_slot, r0 + r)
        w0 = rt[:, 2:3]
        w1 = rt[:, 3:4]
        y = jnp.concatenate([w0 * a_lo + w1 * b_lo, w0 * a_hi + w1 * b_hi], axis=1)
        o_ref[rows, :] = _ln_rows(DEEPNORM_ALPHA * x1 + gate2 * y) * g2_ref[...] + b2_ref[...]
        return carry

    lax.fori_loop(0, tm // FINAL_CHUNK, chunk, 0)

    @pl.when(i == n - 1)
    def _():
        wait((i + 1) % FINAL_SLOTS)
        wait(nxt_slot)


def _final(pos_rows, x1, mod3, route, g2, b2, ys):
    tm = FINAL_TM
    tpb = SEQ // tm
    row = lambda w: pl.BlockSpec((tm, w), lambda i, *_: (i, 0))
    grid_spec = pltpu.PrefetchScalarGridSpec(
        num_scalar_prefetch=1,
        grid=(N_TOK // tm,),
        in_specs=[row(D_MODEL),
                  pl.BlockSpec((1, N_MOD, D_MODEL), lambda i, *_: (i // tpb, 0, 0)),
                  row(LANES),
                  pl.BlockSpec(g2.shape, lambda i, *_: (0, 0)),
                  pl.BlockSpec(b2.shape, lambda i, *_: (0, 0)),
                  pl.BlockSpec(memory_space=pl.ANY)],
        out_specs=row(D_MODEL),
        scratch_shapes=[pltpu.VMEM((FINAL_SLOTS, TOP_K * tm * SUBLANES, LANES), U32),
                        pltpu.SemaphoreType.DMA((FINAL_SLOTS,))],
    )
    return pl.pallas_call(
        _final_body,
        grid_spec=grid_spec,
        out_shape=jax.ShapeDtypeStruct((N_TOK, D_MODEL), F32),
        compiler_params=_cparams(("arbitrary",)),
        name="final",
    )(pos_rows, x1, mod3, route, g2, b2, ys)


def kernel(x, c, positions, w_ada, b_ada, w_in, q_norm_g, w_uq, kv_norm_g, w_ukv, sgu_norm_g, sgu_norm_b,
           w_spatial, b_spatial, w_o, ln1_g, ln1_b, w_router_group, b_router_group, w_router_expert,
           b_router_expert, w_gate, w_up, w_down, ln2_g, ln2_b):
    l = 0
    x2 = x.reshape(N_TOK, D_MODEL)
    mod3 = _ada(c, w_ada[l], b_ada[l][None, :]).reshape(BATCH, N_MOD, D_MODEL)

    wq, wkv, wkpe, wz, wuq, wukv, wo = _prep(w_in[l], w_uq[l], w_ukv[l], w_o[l])
    n_r = N_GROUPS + N_EXPERTS
    wr = jnp.pad(jnp.concatenate([w_router_group[l], w_router_expert[l]], axis=1),
                 ((0, 0), (0, LANES - n_r))).astype(BF16)
    br = jnp.pad(jnp.concatenate([b_router_group[l], b_router_expert[l]]), (0, LANES - n_r))[None, :]
    inv_freq = 1.0 / (ROPE_THETA ** (jnp.arange(0, QK_ROPE_DIM, 2, dtype=F32) / QK_ROPE_DIM))
    invf = jnp.tile(inv_freq, 2 * LANES // QK_ROPE_DIM)[None, :]

    cqn, ckvn, kpe, u, vs = _inproj(x2, mod3, wq, wkv, wkpe, wz, q_norm_g[l][None, :], kv_norm_g[l][None, :],
                                    sgu_norm_g[l][None, :], sgu_norm_b[l][None, :])
    q, k, v = _qkv(cqn, ckvn, kpe, positions.reshape(N_TOK, 1), invf, wuq, wukv)
    attn = _attn(q, k, v).reshape(N_TOK, MLA_WIDTH)
    x1, h2, logits = _mixout(x2, mod3, attn, u, vs, w_spatial[l], b_spatial[l].T, wo,
                             ln1_g[l][None, :], ln1_b[l][None, :], wr, br)
    route, counts = _route(logits)
    pos_tab, tile_tab = _plan(route, counts)
    pos_rows = pos_tab[:, 0:TOP_K].reshape(-1)
    xs = _dispatch(pos_rows, tile_tab, h2)
    ys = _moe(tile_tab, xs, w_gate[l], w_up[l], w_down[l])
    out = _final(pos_rows, x1, mod3, route, ln2_g[l][None, :], ln2_b[l][None, :], ys)
    return out.reshape(BATCH, SEQ, D_MODEL)
```

```python
import functools

import jax
import jax.numpy as jnp
import numpy as np
from jax import lax
from jax.experimental import pallas as pl
from jax.experimental.pallas import tpu as pltpu

D_MODEL = 2048
BATCH = 4
SEQ = 2048
N_TOK = BATCH * SEQ

MLA_HEADS = 8
QK_NOPE_DIM = 128
QK_ROPE_DIM = 64
QK_DIM = QK_NOPE_DIM + QK_ROPE_DIM
V_HEAD_DIM = 128
Q_LORA_RANK = 768
KV_LORA_RANK = 512
ROPE_THETA = 10000.0
MLA_WIDTH = MLA_HEADS * V_HEAD_DIM

SGU_GROUPS = 8
SGU_GROUP_DIM = 128
SGU_CHUNK = 128
SGU_WIDTH = SGU_GROUPS * SGU_GROUP_DIM

N_GROUPS = 4
EXPERTS_PER_GROUP = 8
N_EXPERTS = N_GROUPS * EXPERTS_PER_GROUP
TOP_K = 2
EXPERT_FF = 512

DEEPNORM_ALPHA = 2.0 ** 0.25
EPS = 1e-6
N_MOD = 6
NEG_BIG = -1e30

LANES = 128
SUBLANES = 8
VMEM_LIMIT = 56 * 1024 * 1024

ADA_TN = 1024
TOK_TM = 512
MIX_TM = 256
PREP_STEPS = 8
ATT_TQ = 512
ATT_TK = 512
ATT_HEADS = 2
MOE_TM = 256
MOE_TILES = (N_TOK * TOP_K + N_EXPERTS * (MOE_TM - 1)) // MOE_TM + 1
MOE_ROWS = MOE_TILES * MOE_TM
PLAN_TM = 2048
ROUTE_TM = 1024
RT_RANK = 4
DISPATCH_TM = 1024
FINAL_TM = 256
FINAL_CHUNK = 128
FINAL_SLOTS = 3
assert MOE_TILES <= LANES
T_EXP, T_VALID, T_FIRST, T_CLEAR, T_NEXT, T_NEXT2, T_NEXT3, T_ORD = range(8)

F32 = jnp.float32
BF16 = jnp.bfloat16
U32 = jnp.uint32
HALF_D = D_MODEL // 2


def _cparams(sem):
    return pltpu.CompilerParams(dimension_semantics=sem, vmem_limit_bytes=VMEM_LIMIT)


def _const_spec(shape):
    nd = len(shape)
    return pl.BlockSpec(shape, lambda *_: (0,) * nd, pipeline_mode=pl.Buffered(1))


def _ln_rows(x):
    mu = jnp.mean(x, axis=-1, keepdims=True)
    xc = x - mu
    var = jnp.mean(xc * xc, axis=-1, keepdims=True)
    return xc * lax.rsqrt(var + EPS)


def _rms_rows(x):
    return x * lax.rsqrt(jnp.mean(x * x, axis=-1, keepdims=True) + EPS)


def _pack_halves(x):
    half = x.shape[-1] // 2
    return pltpu.pack_elementwise([x[:, :half], x[:, half:]], packed_dtype=BF16)


def _unpack_halves(w):
    lo = pltpu.unpack_elementwise(w, index=0, packed_dtype=BF16, unpacked_dtype=F32)
    hi = pltpu.unpack_elementwise(w, index=1, packed_dtype=BF16, unpacked_dtype=F32)
    return lo, hi


def _store_token_tiles(ref, w):
    rows = w.shape[0]
    for s in range(SUBLANES):
        ref[pl.ds(s, rows, stride=SUBLANES), :] = w[:, s * LANES:(s + 1) * LANES]


def _load_token_tiles(ref, start_row, rows):
    return jnp.concatenate([ref[pl.ds(start_row * SUBLANES + s, rows, stride=SUBLANES), :]
                            for s in range(SUBLANES)], axis=1)


def _gelu_tanh(x):
    c = np.sqrt(2.0 / np.pi).astype(np.float32)
    return 0.5 * x * (1.0 + jnp.tanh(c * (x + 0.044715 * (x * x * x))))


def _ada_body(c_ref, w_ref, b_ref, o_ref):
    o_ref[...] = jnp.dot(c_ref[...].astype(BF16), w_ref[...].astype(BF16),
                         preferred_element_type=F32) + b_ref[...]


def _ada(c, w, b):
    n = w.shape[1]
    return pl.pallas_call(
        _ada_body,
        grid=(n // ADA_TN,),
        in_specs=[pl.BlockSpec((BATCH, D_MODEL), lambda j: (0, 0)),
                  pl.BlockSpec((D_MODEL, ADA_TN), lambda j: (0, j)),
                  pl.BlockSpec((1, ADA_TN), lambda j: (0, j))],
        out_specs=pl.BlockSpec((BATCH, ADA_TN), lambda j: (0, j)),
        out_shape=jax.ShapeDtypeStruct((BATCH, n), F32),
        compiler_params=_cparams(("parallel",)),
        name="ada",
    )(c, w, b)


def _prep_body(win_ref, wuq_ref, wukv_ref, wo_ref, wq_o, wkv_o, wkpe_o, wz_o, wuq_o, wukv_o, wo_o):
    o1, o2, o3 = Q_LORA_RANK, Q_LORA_RANK + KV_LORA_RANK, Q_LORA_RANK + KV_LORA_RANK + QK_ROPE_DIM
    w = win_ref[...]
    wq_o[...] = w[:, :o1].astype(BF16)
    wkv_o[...] = w[:, o1:o2].astype(BF16)
    kpe = w[:, o2:o3]
    wkpe_o[...] = jnp.concatenate([kpe, jnp.zeros_like(kpe)], axis=1).astype(BF16)
    wz_o[...] = w[:, o3:].astype(BF16)
    u = wuq_ref[...]
    nope = [u[:, h * QK_DIM:h * QK_DIM + QK_NOPE_DIM] for h in range(MLA_HEADS)]
    rope = [u[:, h * QK_DIM + QK_NOPE_DIM:(h + 1) * QK_DIM] for h in range(MLA_HEADS)]
    wuq_o[...] = jnp.concatenate(nope + rope, axis=1).astype(BF16)
    kv = wukv_ref[...]
    hw = QK_NOPE_DIM + V_HEAD_DIM
    kn = [kv[:, h * hw:h * hw + QK_NOPE_DIM] for h in range(MLA_HEADS)]
    vv = [kv[:, h * hw + QK_NOPE_DIM:(h + 1) * hw] for h in range(MLA_HEADS)]
    wukv_o[...] = jnp.concatenate(kn + vv, axis=1).astype(BF16)
    wo_o[...] = wo_ref[...].astype(BF16)


def _prep(w_in, w_uq, w_ukv, w_o):
    steps = PREP_STEPS
    zw = w_in.shape[1] - (Q_LORA_RANK + KV_LORA_RANK + QK_ROPE_DIM)
    blk = lambda a: pl.BlockSpec((a.shape[0] // steps, a.shape[1]), lambda i: (i, 0))
    out_cols = (Q_LORA_RANK, KV_LORA_RANK, LANES, zw)
    outs = ([(D_MODEL, c) for c in out_cols] + [w_uq.shape, w_ukv.shape, w_o.shape])
    return pl.pallas_call(
        _prep_body,
        grid=(steps,),
        in_specs=[blk(w_in), blk(w_uq), blk(w_ukv), blk(w_o)],
        out_specs=[pl.BlockSpec((r // steps, c), lambda i: (i, 0)) for r, c in outs],
        out_shape=[jax.ShapeDtypeStruct(s, BF16) for s in outs],
        compiler_params=_cparams(("parallel",)),
        name="prep",
    )(w_in, w_uq, w_ukv, w_o)


def _inproj_body(x_ref, mod_ref, wq_ref, wkv_ref, wkpe_ref, wz_ref, gq_ref, gkv_ref, sg_ref, sb_ref,
                 cq_ref, ckv_ref, kpe_ref, u_ref, vs_ref):
    sh = mod_ref[0, 0:1, :]
    sc = mod_ref[0, 1:2, :]
    h = (_ln_rows(x_ref[...]) * (1.0 + sc) + sh).astype(BF16)
    cq = jnp.dot(h, wq_ref[...], preferred_element_type=F32)
    cq_ref[...] = (_rms_rows(cq) * gq_ref[...]).astype(BF16)
    ckv = jnp.dot(h, wkv_ref[...], preferred_element_type=F32)
    ckv_ref[...] = (_rms_rows(ckv) * gkv_ref[...]).astype(BF16)
    kpe_ref[...] = jnp.dot(h, wkpe_ref[...], preferred_element_type=F32)
    gz = _gelu_tanh(jnp.dot(h, wz_ref[...], preferred_element_type=F32))
    u_ref[...] = gz[:, :SGU_WIDTH]
    vs_ref[...] = (_ln_rows(gz[:, SGU_WIDTH:]) * sg_ref[...] + sb_ref[...]).astype(BF16)


def _inproj(x2, mod3, wq, wkv, wkpe, wz, gq, gkv, sg, sb):
    tm = TOK_TM
    tiles_per_batch = SEQ // tm
    row = lambda w: pl.BlockSpec((tm, w), lambda i: (i, 0))
    return pl.pallas_call(
        _inproj_body,
        grid=(N_TOK // tm,),
        in_specs=[row(D_MODEL),
                  pl.BlockSpec((1, N_MOD, D_MODEL), lambda i: (i // tiles_per_batch, 0, 0)),
                  _const_spec(wq.shape), _const_spec(wkv.shape), _const_spec(wkpe.shape), _const_spec(wz.shape),
                  _const_spec(gq.shape), _const_spec(gkv.shape), _const_spec(sg.shape), _const_spec(sb.shape)],
        out_specs=[row(Q_LORA_RANK), row(KV_LORA_RANK), row(LANES), row(SGU_WIDTH), row(SGU_WIDTH)],
        out_shape=[jax.ShapeDtypeStruct((N_TOK, Q_LORA_RANK), BF16),
                   jax.ShapeDtypeStruct((N_TOK, KV_LORA_RANK), BF16),
                   jax.ShapeDtypeStruct((N_TOK, LANES), F32),
                   jax.ShapeDtypeStruct((N_TOK, SGU_WIDTH), F32),
                   jax.ShapeDtypeStruct((N_TOK, SGU_WIDTH), BF16)],
        compiler_params=_cparams(("parallel",)),
        name="inproj",
    )(x2, mod3, wq, wkv, wkpe, wz, gq, gkv, sg, sb)


def _rope(x, cos, sin):
    w = x.shape[-1]
    lane = lax.broadcasted_iota(jnp.int32, x.shape, 1)
    first_half = (lane % QK_ROPE_DIM) < (QK_ROPE_DIM // 2)
    rot = jnp.where(first_half,
                    -pltpu.roll(x, w - QK_ROPE_DIM // 2, 1),
                    pltpu.roll(x, QK_ROPE_DIM // 2, 1))
    return x * cos + rot * sin


def _qkv_body(cq_ref, ckv_ref, kpe_ref, pos_ref, invf_ref, wuq_ref, wukv_ref, q_ref, k_ref, v_ref):
    ang = pos_ref[...].astype(F32) * invf_ref[...]
    cos1 = jnp.cos(ang)
    sin1 = jnp.sin(ang)
    reps = MLA_HEADS * QK_ROPE_DIM // LANES
    cos = jnp.concatenate([cos1] * reps, axis=1)
    sin = jnp.concatenate([sin1] * reps, axis=1)
    scale = np.float32(QK_DIM ** -0.5)
    q = jnp.dot(cq_ref[...], wuq_ref[...], preferred_element_type=F32) * scale
    q_pe = _rope(q[:, MLA_HEADS * QK_NOPE_DIM:], cos, sin)
    kv = jnp.dot(ckv_ref[...], wukv_ref[...], preferred_element_type=F32)
    k_pe = _rope(kpe_ref[...], cos1, sin1)[:, :QK_ROPE_DIM].astype(BF16)
    for h in range(MLA_HEADS):
        q_ref[0, h, :, 0:QK_NOPE_DIM] = q[:, h * QK_NOPE_DIM:(h + 1) * QK_NOPE_DIM].astype(BF16)
        q_ref[0, h, :, QK_NOPE_DIM:QK_DIM] = q_pe[:, h * QK_ROPE_DIM:(h + 1) * QK_ROPE_DIM].astype(BF16)
        k_ref[0, h, :, 0:QK_NOPE_DIM] = kv[:, h * QK_NOPE_DIM:(h + 1) * QK_NOPE_DIM].astype(BF16)
        k_ref[0, h, :, QK_NOPE_DIM:QK_DIM] = k_pe
        v_ref[0, h, :, :] = kv[:, MLA_WIDTH + h * V_HEAD_DIM:MLA_WIDTH + (h + 1) * V_HEAD_DIM].astype(BF16)


def _qkv(cqn, ckvn, kpe, pos2, invf, wuq, wukv):
    tm = TOK_TM
    tpb = SEQ // tm
    row = lambda w: pl.BlockSpec((tm, w), lambda i: (i, 0))
    head_out = lambda w: pl.BlockSpec((1, MLA_HEADS, tm, w), lambda i: (i // tpb, 0, i % tpb, 0))
    return pl.pallas_call(
        _qkv_body,
        grid=(N_TOK // tm,),
        in_specs=[row(Q_LORA_RANK), row(KV_LORA_RANK), row(LANES), row(1),
                  _const_spec(invf.shape), _const_spec(wuq.shape), _const_spec(wukv.shape)],
        out_specs=[head_out(QK_DIM), head_out(QK_DIM), head_out(V_HEAD_DIM)],
        out_shape=[jax.ShapeDtypeStruct((BATCH, MLA_HEADS, SEQ, QK_DIM), BF16),
                   jax.ShapeDtypeStruct((BATCH, MLA_HEADS, SEQ, QK_DIM), BF16),
                   jax.ShapeDtypeStruct((BATCH, MLA_HEADS, SEQ, V_HEAD_DIM), BF16)],
        compiler_params=_cparams(("parallel",)),
        name="qkv",
    )(cqn, ckvn, kpe, pos2, invf, wuq, wukv)


def _attn_body(q_ref, k_ref, v_ref, o_ref):
    i = pl.program_id(2)

    def step(h, j, carry, masked):
        m, l, acc = carry
        start = pl.multiple_of(j * ATT_TK, ATT_TK)
        k = k_ref[0, h, pl.ds(start, ATT_TK), :]
        v = v_ref[0, h, pl.ds(start, ATT_TK), :]
        s = lax.dot_general(q_ref[0, h], k, (((1,), (1,)), ((), ())), preferred_element_type=F32)
        if masked:
            r = lax.broadcasted_iota(jnp.int32, s.shape, 0)
            c = lax.broadcasted_iota(jnp.int32, s.shape, 1)
            s = jnp.where(c <= r, s, NEG_BIG)
        m_new = jnp.maximum(m, jnp.max(s, axis=-1, keepdims=True))
        p = jnp.exp(s - m_new)
        a = jnp.exp(m - m_new)
        l = a * l + jnp.sum(p, axis=-1, keepdims=True)
        acc = a * acc + jnp.dot(p.astype(BF16), v, preferred_element_type=F32)
        return m_new, l, acc

    def steps(j, carries, masked):
        return tuple(step(h, j, carries[h], masked) for h in range(ATT_HEADS))

    init = tuple((jnp.full((ATT_TQ, 1), NEG_BIG, F32), jnp.zeros((ATT_TQ, 1), F32),
                  jnp.zeros((ATT_TQ, V_HEAD_DIM), F32)) for _ in range(ATT_HEADS))
    carries = lax.fori_loop(0, i, lambda j, c: steps(j, c, False), init)
    carries = steps(i, carries, True)
    for h, (m, l, acc) in enumerate(carries):
        o_ref[0, :, h * V_HEAD_DIM:(h + 1) * V_HEAD_DIM] = (acc / l).astype(BF16)


def _attn(q, k, v):
    assert ATT_TQ == ATT_TK
    hb = ATT_HEADS
    return pl.pallas_call(
        _attn_body,
        grid=(BATCH, MLA_HEADS // hb, SEQ // ATT_TQ),
        in_specs=[pl.BlockSpec((1, hb, ATT_TQ, QK_DIM), lambda b, h, i: (b, h, i, 0)),
                  pl.BlockSpec((1, hb, SEQ, QK_DIM), lambda b, h, i: (b, h, 0, 0)),
                  pl.BlockSpec((1, hb, SEQ, V_HEAD_DIM), lambda b, h, i: (b, h, 0, 0))],
        out_specs=pl.BlockSpec((1, ATT_TQ, hb * V_HEAD_DIM), lambda b, h, i: (b, i, h)),
        out_shape=jax.ShapeDtypeStruct((BATCH, SEQ, MLA_WIDTH), BF16),
        compiler_params=_cparams(("parallel", "parallel", "arbitrary")),
        name="attn",
    )(q, k, v)


def _mixout_body(x_ref, mod_ref, attn_ref, u_ref, vs_ref, wsp_ref, bsp_ref, woa_ref, wos_ref,
                 g1_ref, b1_ref, wr_ref, br_ref, x1_ref, h2_ref, lg_ref, sgu_scr):
    r = lax.broadcasted_iota(jnp.int32, (SGU_CHUNK, SGU_CHUNK), 0)
    c = lax.broadcasted_iota(jnp.int32, (SGU_CHUNK, SGU_CHUNK), 1)
    causal = c <= r
    for g in range(SGU_GROUPS):
        ws = jnp.where(causal, wsp_ref[g], 0.0).astype(BF16)
        bias = bsp_ref[:, g:g + 1]
        cols = slice(g * SGU_GROUP_DIM, (g + 1) * SGU_GROUP_DIM)
        for ch in range(MIX_TM // SGU_CHUNK):
            rows = slice(ch * SGU_CHUNK, (ch + 1) * SGU_CHUNK)
            mixed = jnp.dot(ws, vs_ref[rows, cols], preferred_element_type=F32) + bias
            sgu_scr[rows, cols] = (u_ref[rows, cols] * mixed).astype(BF16)
    y = (jnp.dot(attn_ref[...], woa_ref[...], preferred_element_type=F32)
         + jnp.dot(sgu_scr[...], wos_ref[...], preferred_element_type=F32))
    gate1 = mod_ref[0, 2:3, :]
    sh2 = mod_ref[0, 3:4, :]
    sc2 = mod_ref[0, 4:5, :]
    x1 = _ln_rows(DEEPNORM_ALPHA * x_ref[...] + gate1 * y) * g1_ref[...] + b1_ref[...]
    x1_ref[...] = x1
    h2 = _ln_rows(x1) * (1.0 + sc2) + sh2
    _store_token_tiles(h2_ref, _pack_halves(h2))
    lg_ref[...] = jnp.dot(h2.astype(BF16), wr_ref[...], preferred_element_type=F32) + br_ref[...]


def _mixout(x2, mod3, attn, u, vs, wsp, bsp_t, wo, g1, b1, wr, br):
    tm = MIX_TM
    tpb = SEQ // tm
    row = lambda w: pl.BlockSpec((tm, w), lambda i: (i, 0))
    wo_half = lambda j: pl.BlockSpec((MLA_WIDTH, D_MODEL), lambda i: (j, 0), pipeline_mode=pl.Buffered(1))
    return pl.pallas_call(
        _mixout_body,
        grid=(N_TOK // tm,),
        in_specs=[row(D_MODEL),
                  pl.BlockSpec((1, N_MOD, D_MODEL), lambda i: (i // tpb, 0, 0)),
                  row(MLA_WIDTH), row(SGU_WIDTH), row(SGU_WIDTH),
                  _const_spec(wsp.shape), _const_spec(bsp_t.shape), wo_half(0), wo_half(1),
                  _const_spec(g1.shape), _const_spec(b1.shape), _const_spec(wr.shape), _const_spec(br.shape)],
        out_specs=[row(D_MODEL), pl.BlockSpec((tm * SUBLANES, LANES), lambda i: (i, 0)), row(LANES)],
        out_shape=[jax.ShapeDtypeStruct((N_TOK, D_MODEL), F32),
                   jax.ShapeDtypeStruct((N_TOK * SUBLANES, LANES), U32),
                   jax.ShapeDtypeStruct((N_TOK, LANES), F32)],
        scratch_shapes=[pltpu.VMEM((tm, SGU_WIDTH), BF16)],
        compiler_params=_cparams(("parallel",)),
        name="mix_out",
    )(x2, mod3, attn, u, vs, wsp, bsp_t, wo, wo, g1, b1, wr, br)


def _route_math(lg):
    lane = lax.broadcasted_iota(jnp.int32, lg.shape, 1)
    big = jnp.int32(LANES)

    def top1(vals):
        m = jnp.max(vals, axis=-1, keepdims=True)
        idx = jnp.min(jnp.where(vals == m, lane, big), axis=-1, keepdims=True)
        return m, idx

    is_group = lane < N_GROUPS
    glog = jnp.where(is_group, lg, -jnp.inf)
    gmax, gidx = top1(glog)
    pg_top = 1.0 / jnp.sum(jnp.exp(glog - gmax), axis=-1, keepdims=True)
    eid = lane - N_GROUPS
    sel = (eid >= gidx * EXPERTS_PER_GROUP) & (eid < (gidx + 1) * EXPERTS_PER_GROUP)
    elog = jnp.where(sel, lg, -jnp.inf)
    m1, i1 = top1(elog)
    m2, i2 = top1(jnp.where(lane == i1, -jnp.inf, elog))
    e2 = jnp.exp(m2 - m1)
    w1 = pg_top / (1.0 + e2)
    w2 = pg_top * e2 / (1.0 + e2)
    return jnp.where(lane == 0, (i1 - N_GROUPS).astype(F32),
                     jnp.where(lane == 1, (i2 - N_GROUPS).astype(F32),
                               jnp.where(lane == 2, w1, jnp.where(lane == 3, w2, 0.0))))


def _rank_math(rt, counts):
    t = rt.shape[0]
    lane = lax.broadcasted_iota(jnp.int32, (t, LANES), 1).astype(F32)
    oh0 = lane == rt[:, 0:1]
    oh1 = lane == rt[:, 1:2]
    s = jnp.where(oh0 | oh1, 1.0, 0.0)
    r = lax.broadcasted_iota(jnp.int32, (t, t), 0)
    c = lax.broadcasted_iota(jnp.int32, (t, t), 1)
    before = jnp.where(c < r, 1.0, 0.0).astype(BF16)
    csum = jnp.dot(before, s.astype(BF16), preferred_element_type=F32) + counts
    rank0 = jnp.sum(jnp.where(oh0, csum, 0.0), axis=-1, keepdims=True)
    rank1 = jnp.sum(jnp.where(oh1, csum, 0.0), axis=-1, keepdims=True)
    return rank0, rank1, counts + jnp.sum(s, axis=0, keepdims=True)


def _route_body(lg_ref, rt_ref, cnt_ref, cnt_scr):
    @pl.when(pl.program_id(0) == 0)
    def _():
        cnt_scr[...] = jnp.zeros_like(cnt_scr)

    rt = _route_math(lg_ref[...])
    rank0, rank1, counts = _rank_math(rt, cnt_scr[...])
    cnt_scr[...] = counts
    lane = lax.broadcasted_iota(jnp.int32, rt.shape, 1)
    rt_ref[...] = jnp.where(lane == RT_RANK, rank0, jnp.where(lane == RT_RANK + 1, rank1, rt))
    cnt_ref[...] = jnp.broadcast_to(counts, cnt_ref.shape)


def _route(logits):
    tm = ROUTE_TM
    return pl.pallas_call(
        _route_body,
        grid=(N_TOK // tm,),
        in_specs=[pl.BlockSpec((tm, LANES), lambda i: (i, 0))],
        out_specs=[pl.BlockSpec((tm, LANES), lambda i: (i, 0)),
                   pl.BlockSpec((SUBLANES, LANES), lambda i: (0, 0))],
        out_shape=[jax.ShapeDtypeStruct((N_TOK, LANES), F32),
                   jax.ShapeDtypeStruct((SUBLANES, LANES), F32)],
        scratch_shapes=[pltpu.VMEM((1, LANES), F32)],
        compiler_params=_cparams(("arbitrary",)),
        name="route",
    )(logits)


def _plan_body(rt_ref, cnt_ref, pos_ref, tt_ref):
    t = PLAN_TM
    lane = lax.broadcasted_iota(jnp.int32, (t, LANES), 1)
    rt = rt_ref[...]
    oh0 = lane.astype(F32) == rt[:, 0:1]
    oh1 = lane.astype(F32) == rt[:, 1:2]
    if True:
        counts = cnt_ref[0:1, :]
        tiles = jnp.floor((counts + (MOE_TM - 1)) * (1.0 / MOE_TM))
        r = lax.broadcasted_iota(jnp.int32, (LANES, LANES), 0)
        c = lax.broadcasted_iota(jnp.int32, (LANES, LANES), 1)
        upto = jnp.where(r <= c, 1.0, 0.0).astype(BF16)
        tiles8 = jnp.broadcast_to(tiles, (8, LANES)).astype(BF16)
        tile_end = jnp.dot(tiles8, upto, preferred_element_type=F32)[0:1]
        offs = (tile_end - tiles) * MOE_TM
        p0 = jnp.sum(jnp.where(oh0, offs, 0.0), axis=-1, keepdims=True) + rt[:, RT_RANK:RT_RANK + 1]
        p1 = jnp.sum(jnp.where(oh1, offs, 0.0), axis=-1, keepdims=True) + rt[:, RT_RANK + 1:RT_RANK + 2]
        pos_ref[...] = (jnp.where(lane == 0, p0, jnp.where(lane == 1, p1, 0.0)) * SUBLANES).astype(jnp.int32)

        lane_e = lax.broadcasted_iota(jnp.int32, (LANES, LANES), 1)
        tile_id = lax.broadcasted_iota(jnp.int32, (LANES, LANES), 0).astype(F32)
        is_e = lane_e < N_EXPERTS
        total = jnp.max(tile_end, axis=-1, keepdims=True)
        t_exp = jnp.sum(jnp.where(is_e & (tile_end <= tile_id), 1.0, 0.0), axis=-1, keepdims=True)
        t_valid = jnp.where(tile_id[:, 0:1] < total, 1.0, 0.0)
        last_exp = jnp.sum(jnp.where(is_e & (tile_end <= total - 1.0), 1.0, 0.0), axis=-1, keepdims=True)
        t_exp = jnp.where(t_valid > 0, t_exp, last_exp)
        t_first = jnp.sum(jnp.where(is_e & (tiles > 0) & ((tile_end - tiles) == tile_id), 1.0, 0.0),
                          axis=-1, keepdims=True)
        t_last = jnp.sum(jnp.where(is_e & (tiles > 0) & ((tile_end - 1.0) == tile_id), 1.0, 0.0),
                         axis=-1, keepdims=True)
        t_clear = jnp.maximum(t_last, 1.0 - t_valid)
        none = jnp.float32(LANES)
        owns = is_e & (tiles > 0)
        lane_f = lane_e.astype(F32)
        t_next = jnp.min(jnp.where(owns & (lane_f > t_exp), lane_f, none), axis=-1, keepdims=True)
        t_next2 = jnp.min(jnp.where(owns & (lane_f > t_next), lane_f, none), axis=-1, keepdims=True)
        t_next3 = jnp.min(jnp.where(owns & (lane_f > t_next2), lane_f, none), axis=-1, keepdims=True)
        t_ord = jnp.sum(jnp.where(owns & (lane_f < t_exp), 1.0, 0.0), axis=-1, keepdims=True)
        cols = {T_EXP: t_exp, T_VALID: t_valid, T_FIRST: t_first, T_CLEAR: t_clear,
                T_NEXT: t_next, T_NEXT2: t_next2, T_NEXT3: t_next3, T_ORD: t_ord}
        table = jnp.zeros((LANES, LANES), F32)
        for k, col in cols.items():
            table = jnp.where(lane_e == k, col, table)
        tt_ref[...] = table.astype(jnp.int32)


def _plan(route, counts):
    t = PLAN_TM
    return pl.pallas_call(
        _plan_body,
        grid=(N_TOK // t,),
        in_specs=[pl.BlockSpec((t, LANES), lambda i: (i, 0)),
                  pl.BlockSpec((SUBLANES, LANES), lambda i: (0, 0))],
        out_specs=[pl.BlockSpec((t, LANES), lambda i: (i, 0)),
                   pl.BlockSpec((LANES, LANES), lambda i: (0, 0))],
        out_shape=[jax.ShapeDtypeStruct((N_TOK, LANES), jnp.int32),
                   jax.ShapeDtypeStruct((LANES, LANES), jnp.int32)],
        compiler_params=_cparams(("arbitrary",)),
        name="plan",
    )(route, counts)


def _rows_wait(ref, n_rows, sem):
    pltpu.make_async_copy(ref.at[pl.ds(0, n_rows)], ref.at[pl.ds(0, n_rows)], sem).wait()


def _dispatch_body(pos_ref, tt_ref, h_ref, xs_hbm, zbuf, sem_z, sem_s):
    i = pl.program_id(0)
    tile_rows = MOE_TM * SUBLANES

    @pl.when(i == 0)
    def _():
        zbuf[...] = _pack_halves(jnp.zeros((tile_rows, 2 * LANES), F32))

        def zero_copy(tile):
            start = pl.multiple_of(tile * tile_rows, tile_rows)
            return pltpu.make_async_copy(zbuf, xs_hbm.at[pl.ds(start, tile_rows)], sem_z)

        def clear_start(tile, carry):
            @pl.when(tt_ref[tile, T_CLEAR] > 0)
            def _():
                zero_copy(tile).start()
            return carry

        def clear_wait(tile, carry):
            @pl.when(tt_ref[tile, T_CLEAR] > 0)
            def _():
                zero_copy(tile).wait()
            return carry

        lax.fori_loop(0, MOE_TILES, clear_start, 0)
        lax.fori_loop(0, MOE_TILES, clear_wait, 0)

    def tok(j, carry):
        src = h_ref.at[pl.ds(pl.multiple_of(j * SUBLANES, SUBLANES), SUBLANES)]
        pair = TOP_K * (i * DISPATCH_TM + j)
        for k in range(TOP_K):
            dst_row = pl.multiple_of(pos_ref[pair + k], SUBLANES)
            pltpu.make_async_copy(src, xs_hbm.at[pl.ds(dst_row, SUBLANES)], sem_s).start(priority=k)
        return carry

    lax.fori_loop(0, DISPATCH_TM, tok, 0, unroll=8)
    _rows_wait(xs_hbm, TOP_K * DISPATCH_TM * SUBLANES, sem_s)


def _dispatch(pos_rows, tile_clear, h2p):
    grid_spec = pltpu.PrefetchScalarGridSpec(
        num_scalar_prefetch=2,
        grid=(N_TOK // DISPATCH_TM,),
        in_specs=[pl.BlockSpec((DISPATCH_TM * SUBLANES, LANES), lambda i, *_: (i, 0))],
        out_specs=pl.BlockSpec(memory_space=pl.ANY),
        scratch_shapes=[pltpu.VMEM((MOE_TM * SUBLANES, LANES), U32),
                        pltpu.SemaphoreType.DMA(()), pltpu.SemaphoreType.DMA(())],
    )
    return pl.pallas_call(
        _dispatch_body,
        grid_spec=grid_spec,
        out_shape=jax.ShapeDtypeStruct((MOE_ROWS * SUBLANES, LANES), U32),
        compiler_params=_cparams(("arbitrary",)),
        name="dispatch",
    )(pos_rows, tile_clear, h2p)


def _moe_body(tt_ref, x_ref, wg_hbm, wu_hbm, wd_hbm, y_ref, wg_s, wu_s, wd_s, stg_g, stg_u, stg_d, sem):
    i = pl.program_id(0)
    parity = tt_ref[i, T_ORD] % 2

    def fetch(e, slot):
        return (pltpu.make_async_copy(wg_hbm.at[e], stg_g.at[slot], sem.at[slot, 0]),
                pltpu.make_async_copy(wu_hbm.at[e], stg_u.at[slot], sem.at[slot, 1]),
                pltpu.make_async_copy(wd_hbm.at[e], stg_d.at[slot], sem.at[slot, 2]))

    def cast(slot):
        wg_s[slot] = stg_g[slot].astype(BF16)
        wu_s[slot] = stg_u[slot].astype(BF16)
        wd_s[slot] = stg_d[slot].astype(BF16)

    def compute(par):
        lo, hi = _unpack_halves(_load_token_tiles(x_ref, 0, MOE_TM))
        xa = lo.astype(BF16)
        xb = hi.astype(BF16)
        g = (jnp.dot(xa, wg_s[par, :HALF_D, :], preferred_element_type=F32)
             + jnp.dot(xb, wg_s[par, HALF_D:, :], preferred_element_type=F32))
        u = (jnp.dot(xa, wu_s[par, :HALF_D, :], preferred_element_type=F32)
             + jnp.dot(xb, wu_s[par, HALF_D:, :], preferred_element_type=F32))
        hid = (g * jax.nn.sigmoid(g) * u).astype(BF16)
        _store_token_tiles(y_ref, _pack_halves(jnp.dot(hid, wd_s[par], preferred_element_type=F32)))

    @pl.when(i == 0)
    def _():
        for cp in fetch(tt_ref[0, T_EXP], 0):
            cp.start()

        @pl.when(tt_ref[0, T_NEXT] < N_EXPERTS)
        def _():
            for cp in fetch(tt_ref[0, T_NEXT], 1):
                cp.start()

        for cp in fetch(tt_ref[0, T_EXP], 0):
            cp.wait()
        cast(0)

        @pl.when(tt_ref[0, T_NEXT2] < N_EXPERTS)
        def _():
            for cp in fetch(tt_ref[0, T_NEXT2], 0):
                cp.start()

    valid = tt_ref[i, T_VALID] > 0
    with_cast = (tt_ref[i, T_FIRST] > 0) & (tt_ref[i, T_NEXT] < N_EXPERTS)

    for par in range(2):
        oth = 1 - par

        @pl.when(with_cast & (parity == par))
        def _():
            for cp in fetch(tt_ref[i, T_NEXT], oth):
                cp.wait()
            compute(par)
            cast(oth)

            @pl.when(tt_ref[i, T_NEXT3] < N_EXPERTS)
            def _():
                for cp in fetch(tt_ref[i, T_NEXT3], oth):
                    cp.start()

        @pl.when(valid & jnp.logical_not(with_cast) & (parity == par))
        def _():
            compute(par)

    @pl.when(jnp.logical_not(valid))
    def _():
        y_ref[...] = _pack_halves(jnp.zeros((MOE_TM * SUBLANES, 2 * LANES), F32))


def _moe(tile_tab, xs, wg, wu, wd):
    tm = MOE_TM
    grid_spec = pltpu.PrefetchScalarGridSpec(
        num_scalar_prefetch=1,
        grid=(MOE_TILES,),
        in_specs=[pl.BlockSpec((tm * SUBLANES, LANES), lambda i, *_: (i, 0)),
                  pl.BlockSpec(memory_space=pl.ANY), pl.BlockSpec(memory_space=pl.ANY),
                  pl.BlockSpec(memory_space=pl.ANY)],
        out_specs=pl.BlockSpec((tm * SUBLANES, LANES), lambda i, *_: (i, 0)),
        scratch_shapes=[pltpu.VMEM((2, D_MODEL, EXPERT_FF), BF16), pltpu.VMEM((2, D_MODEL, EXPERT_FF), BF16),
                        pltpu.VMEM((2, EXPERT_FF, D_MODEL), BF16),
                        pltpu.VMEM((2, D_MODEL, EXPERT_FF), F32), pltpu.VMEM((2, D_MODEL, EXPERT_FF), F32),
                        pltpu.VMEM((2, EXPERT_FF, D_MODEL), F32),
                        pltpu.SemaphoreType.DMA((2, 3))],
    )
    return pl.pallas_call(
        _moe_body,
        grid_spec=grid_spec,
        out_shape=jax.ShapeDtypeStruct((MOE_ROWS * SUBLANES, LANES), U32),
        compiler_params=_cparams(("arbitrary",)),
        name="moe",
    )(tile_tab, xs, wg, wu, wd)


def _final_body(pos_ref, x1_ref, mod_ref, rt_ref, g2_ref, b2_ref, ys_hbm, o_ref, buf, sem):
    i = pl.program_id(0)
    n = pl.num_programs(0)
    tm = FINAL_TM
    slot = i % FINAL_SLOTS
    nxt_slot = (i + 2) % FINAL_SLOTS
    nxt_tile = jnp.minimum(i + 2, n - 1)

    def issue(tile, dst_slot, j):
        pair = TOP_K * (tile * tm + j)
        for k in range(TOP_K):
            src_row = pl.multiple_of(pos_ref[pair + k], SUBLANES)
            dst_row = pl.multiple_of((k * tm + j) * SUBLANES, SUBLANES)
            pltpu.make_async_copy(ys_hbm.at[pl.ds(src_row, SUBLANES)],
                                  buf.at[dst_slot, pl.ds(dst_row, SUBLANES)], sem.at[dst_slot]).start(priority=k)

    def wait(s):
        pltpu.make_async_copy(ys_hbm.at[pl.ds(0, TOP_K * tm * SUBLANES)], buf.at[s], sem.at[s]).wait()

    @pl.when(i == 0)
    def _():
        def tok(j, carry):
            issue(0, 0, j)
            issue(jnp.minimum(1, n - 1), 1, j)
            return carry
        lax.fori_loop(0, tm, tok, 0, unroll=8)

    wait(slot)
    gate2 = mod_ref[0, 5:6, :]
    cur = buf.at[slot]

    def chunk(c, carry):
        r0 = pl.multiple_of(c * FINAL_CHUNK, FINAL_CHUNK)
        rows = pl.ds(r0, FINAL_CHUNK)
        a_lo, a_hi = _unpack_halves(_load_token_tiles(cur, r0, FINAL_CHUNK))
        b_lo, b_hi = _unpack_halves(_load_token_tiles(cur, tm + r0, FINAL_CHUNK))
        x1 = x1_ref[rows, :]
        rt = rt_ref[rows, :]
        for r in range(FINAL_CHUNK):
            issue(nxt_tile, nxt_slot, r0 + r)
        w0 = rt[:, 2:3]
        w1 = rt[:, 3:4]
        y = jnp.concatenate([w0 * a_lo + w1 * b_lo, w0 * a_hi + w1 * b_hi], axis=1)
        o_ref[rows, :] = _ln_rows(DEEPNORM_ALPHA * x1 + gate2 * y) * g2_ref[...] + b2_ref[...]
        return carry

    lax.fori_loop(0, tm // FINAL_CHUNK, chunk, 0)

    @pl.when(i == n - 1)
    def _():
        wait((i + 1) % FINAL_SLOTS)
        wait(nxt_slot)


def _final(pos_rows, x1, mod3, route, g2, b2, ys):
    tm = FINAL_TM
    tpb = SEQ // tm
    row = lambda w: pl.BlockSpec((tm, w), lambda i, *_: (i, 0))
    grid_spec = pltpu.PrefetchScalarGridSpec(
        num_scalar_prefetch=1,
        grid=(N_TOK // tm,),
        in_specs=[row(D_MODEL),
                  pl.BlockSpec((1, N_MOD, D_MODEL), lambda i, *_: (i // tpb, 0, 0)),
                  row(LANES),
                  pl.BlockSpec(g2.shape, lambda i, *_: (0, 0)),
                  pl.BlockSpec(b2.shape, lambda i, *_: (0, 0)),
                  pl.BlockSpec(memory_space=pl.ANY)],
        out_specs=row(D_MODEL),
        scratch_shapes=[pltpu.VMEM((FINAL_SLOTS, TOP_K * tm * SUBLANES, LANES), U32),
                        pltpu.SemaphoreType.DMA((FINAL_SLOTS,))],
    )
    return pl.pallas_call(
        _final_body,
        grid_spec=grid_spec,
        out_shape=jax.ShapeDtypeStruct((N_TOK, D_MODEL), F32),
        compiler_params=_cparams(("arbitrary",)),
        name="final",
    )(pos_rows, x1, mod3, route, g2, b2, ys)


def kernel(x, c, positions, w_ada, b_ada, w_in, q_norm_g, w_uq, kv_norm_g, w_ukv, sgu_norm_g, sgu_norm_b,
           w_spatial, b_spatial, w_o, ln1_g, ln1_b, w_router_group, b_router_group, w_router_expert,
           b_router_expert, w_gate, w_up, w_down, ln2_g, ln2_b):
    l = 0
    x2 = x.reshape(N_TOK, D_MODEL)
    mod3 = _ada(c, w_ada[l], b_ada[l][None, :]).reshape(BATCH, N_MOD, D_MODEL)

    wq, wkv, wkpe, wz, wuq, wukv, wo = _prep(w_in[l], w_uq[l], w_ukv[l], w_o[l])
    n_r = N_GROUPS + N_EXPERTS
    wr = jnp.pad(jnp.concatenate([w_router_group[l], w_router_expert[l]], axis=1),
                 ((0, 0), (0, LANES - n_r))).astype(BF16)
    br = jnp.pad(jnp.concatenate([b_router_group[l], b_router_expert[l]]), (0, LANES - n_r))[None, :]
    inv_freq = 1.0 / (ROPE_THETA ** (jnp.arange(0, QK_ROPE_DIM, 2, dtype=F32) / QK_ROPE_DIM))
    invf = jnp.tile(inv_freq, 2 * LANES // QK_ROPE_DIM)[None, :]

    cqn, ckvn, kpe, u, vs = _inproj(x2, mod3, wq, wkv, wkpe, wz, q_norm_g[l][None, :], kv_norm_g[l][None, :],
                                    sgu_norm_g[l][None, :], sgu_norm_b[l][None, :])
    q, k, v = _qkv(cqn, ckvn, kpe, positions.reshape(N_TOK, 1), invf, wuq, wukv)
    attn = _attn(q, k, v).reshape(N_TOK, MLA_WIDTH)
    x1, h2, logits = _mixout(x2, mod3, attn, u, vs, w_spatial[l], b_spatial[l].T, wo,
                             ln1_g[l][None, :], ln1_b[l][None, :], wr, br)
    route, counts = _route(logits)
    pos_tab, tile_tab = _plan(route, counts)
    pos_rows = pos_tab[:, 0:TOP_K].reshape(-1)
    xs = _dispatch(pos_rows, tile_tab, h2)
    ys = _moe(tile_tab, xs, w_gate[l], w_up[l], w_down[l])
    out = _final(pos_rows, x1, mod3, route, ln2_g[l][None, :], ln2_b[l][None, :], ys)
    return out.reshape(BATCH, SEQ, D_MODEL)
```

```python
import functools

import jax
import jax.numpy as jnp
import numpy as np
from jax import lax
from jax.experimental import pallas as pl
from jax.experimental.pallas import tpu as pltpu

D_MODEL = 2048
BATCH = 4
SEQ = 2048
N_TOK = BATCH * SEQ

MLA_HEADS = 8
QK_NOPE_DIM = 128
QK_ROPE_DIM = 64
QK_DIM = QK_NOPE_DIM + QK_ROPE_DIM
V_HEAD_DIM = 128
Q_LORA_RANK = 768
KV_LORA_RANK = 512
ROPE_THETA = 10000.0
MLA_WIDTH = MLA_HEADS * V_HEAD_DIM

SGU_GROUPS = 8
SGU_GROUP_DIM = 128
SGU_CHUNK = 128
SGU_WIDTH = SGU_GROUPS * SGU_GROUP_DIM

N_GROUPS = 4
EXPERTS_PER_GROUP = 8
N_EXPERTS = N_GROUPS * EXPERTS_PER_GROUP
TOP_K = 2
EXPERT_FF = 512

DEEPNORM_ALPHA = 2.0 ** 0.25
EPS = 1e-6
N_MOD = 6
NEG_BIG = -1e30

LANES = 128
SUBLANES = 8
VMEM_LIMIT = 56 * 1024 * 1024

ADA_TN = 1024
TOK_TM = 512
MIX_TM = 256
PREP_STEPS = 4
ATT_TQ = 512
ATT_TK = 512
ATT_HEADS = 4
MOE_TM = 256
MOE_TILES = (N_TOK * TOP_K + N_EXPERTS * (MOE_TM - 1)) // MOE_TM + 1
MOE_ROWS = MOE_TILES * MOE_TM
PLAN_TM = 2048
ROUTE_TM = 1024
RT_RANK = 4
DISPATCH_TM = 1024
FINAL_TM = 256
FINAL_CHUNK = 128
FINAL_SLOTS = 3
assert MOE_TILES <= LANES
T_EXP, T_VALID, T_FIRST, T_CLEAR, T_NEXT, T_NEXT2, T_ORD = range(7)

F32 = jnp.float32
BF16 = jnp.bfloat16
U32 = jnp.uint32
HALF_D = D_MODEL // 2


def _cparams(sem):
    return pltpu.CompilerParams(dimension_semantics=sem, vmem_limit_bytes=VMEM_LIMIT)


def _const_spec(shape):
    nd = len(shape)
    return pl.BlockSpec(shape, lambda *_: (0,) * nd, pipeline_mode=pl.Buffered(1))


def _ln_rows(x):
    mu = jnp.mean(x, axis=-1, keepdims=True)
    xc = x - mu
    var = jnp.mean(xc * xc, axis=-1, keepdims=True)
    return xc * lax.rsqrt(var + EPS)


def _rms_rows(x):
    return x * lax.rsqrt(jnp.mean(x * x, axis=-1, keepdims=True) + EPS)


def _pack_halves(x):
    half = x.shape[-1] // 2
    return pltpu.pack_elementwise([x[:, :half], x[:, half:]], packed_dtype=BF16)


def _unpack_halves(w):
    lo = pltpu.unpack_elementwise(w, index=0, packed_dtype=BF16, unpacked_dtype=F32)
    hi = pltpu.unpack_elementwise(w, index=1, packed_dtype=BF16, unpacked_dtype=F32)
    return lo, hi


def _store_token_tiles(ref, w):
    rows = w.shape[0]
    for s in range(SUBLANES):
        ref[pl.ds(s, rows, stride=SUBLANES), :] = w[:, s * LANES:(s + 1) * LANES]


def _load_token_tiles(ref, start_row, rows):
    return jnp.concatenate([ref[pl.ds(start_row * SUBLANES + s, rows, stride=SUBLANES), :]
                            for s in range(SUBLANES)], axis=1)


def _gelu_tanh(x):
    c = np.sqrt(2.0 / np.pi).astype(np.float32)
    return 0.5 * x * (1.0 + jnp.tanh(c * (x + 0.044715 * (x * x * x))))


def _ada_body(c_ref, w_ref, b_ref, o_ref):
    o_ref[...] = jnp.dot(c_ref[...].astype(BF16), w_ref[...].astype(BF16),
                         preferred_element_type=F32) + b_ref[...]


def _ada(c, w, b):
    n = w.shape[1]
    return pl.pallas_call(
        _ada_body,
        grid=(n // ADA_TN,),
        in_specs=[pl.BlockSpec((BATCH, D_MODEL), lambda j: (0, 0)),
                  pl.BlockSpec((D_MODEL, ADA_TN), lambda j: (0, j)),
                  pl.BlockSpec((1, ADA_TN), lambda j: (0, j))],
        out_specs=pl.BlockSpec((BATCH, ADA_TN), lambda j: (0, j)),
        out_shape=jax.ShapeDtypeStruct((BATCH, n), F32),
        compiler_params=_cparams(("parallel",)),
        name="ada",
    )(c, w, b)


def _prep_body(win_ref, wuq_ref, wukv_ref, wo_ref, win_o, wuq_o, wukv_o, wo_o):
    win_o[...] = win_ref[...].astype(BF16)
    u = wuq_ref[...]
    nope = [u[:, h * QK_DIM:h * QK_DIM + QK_NOPE_DIM] for h in range(MLA_HEADS)]
    rope = [u[:, h * QK_DIM + QK_NOPE_DIM:(h + 1) * QK_DIM] for h in range(MLA_HEADS)]
    wuq_o[...] = jnp.concatenate(nope + rope, axis=1).astype(BF16)
    kv = wukv_ref[...]
    hw = QK_NOPE_DIM + V_HEAD_DIM
    kn = [kv[:, h * hw:h * hw + QK_NOPE_DIM] for h in range(MLA_HEADS)]
    vv = [kv[:, h * hw + QK_NOPE_DIM:(h + 1) * hw] for h in range(MLA_HEADS)]
    wukv_o[...] = jnp.concatenate(kn + vv, axis=1).astype(BF16)
    wo_o[...] = wo_ref[...].astype(BF16)


def _prep(w_in_t, w_uq, w_ukv, w_o):
    steps = PREP_STEPS
    blk = lambda a: pl.BlockSpec((a.shape[0] // steps, a.shape[1]), lambda i: (i, 0))
    ins = (w_in_t, w_uq, w_ukv, w_o)
    return pl.pallas_call(
        _prep_body,
        grid=(steps,),
        in_specs=[blk(a) for a in ins],
        out_specs=[blk(a) for a in ins],
        out_shape=[jax.ShapeDtypeStruct(a.shape, BF16) for a in ins],
        compiler_params=_cparams(("parallel",)),
        name="prep",
    )(*ins)


def _inproj_body(x_ref, mod_ref, wt_ref, gq_ref, gkv_ref, sg_ref, sb_ref,
                 cq_ref, ckv_ref, kpe_ref, u_ref, vs_ref):
    o1, o2, o3 = Q_LORA_RANK, Q_LORA_RANK + KV_LORA_RANK, Q_LORA_RANK + KV_LORA_RANK + QK_ROPE_DIM

    def proj(lo, hi):
        return lax.dot_general(h, wt_ref[lo:hi, :], (((1,), (1,)), ((), ())), preferred_element_type=F32)

    sh = mod_ref[0, 0:1, :]
    sc = mod_ref[0, 1:2, :]
    h = (_ln_rows(x_ref[...]) * (1.0 + sc) + sh).astype(BF16)
    cq_ref[...] = (_rms_rows(proj(0, o1)) * gq_ref[...]).astype(BF16)
    ckv_ref[...] = (_rms_rows(proj(o1, o2)) * gkv_ref[...]).astype(BF16)
    kpe_ref[...] = proj(o2, o2 + LANES)
    gz = _gelu_tanh(proj(o3, o3 + 2 * SGU_WIDTH))
    u_ref[...] = gz[:, :SGU_WIDTH]
    vs_ref[...] = (_ln_rows(gz[:, SGU_WIDTH:]) * sg_ref[...] + sb_ref[...]).astype(BF16)


def _inproj(x2, mod3, wt, gq, gkv, sg, sb):
    tm = TOK_TM
    tiles_per_batch = SEQ // tm
    row = lambda w: pl.BlockSpec((tm, w), lambda i: (i, 0))
    return pl.pallas_call(
        _inproj_body,
        grid=(N_TOK // tm,),
        in_specs=[row(D_MODEL),
                  pl.BlockSpec((1, N_MOD, D_MODEL), lambda i: (i // tiles_per_batch, 0, 0)),
                  _const_spec(wt.shape),
                  _const_spec(gq.shape), _const_spec(gkv.shape), _const_spec(sg.shape), _const_spec(sb.shape)],
        out_specs=[row(Q_LORA_RANK), row(KV_LORA_RANK), row(LANES), row(SGU_WIDTH), row(SGU_WIDTH)],
        out_shape=[jax.ShapeDtypeStruct((N_TOK, Q_LORA_RANK), BF16),
                   jax.ShapeDtypeStruct((N_TOK, KV_LORA_RANK), BF16),
                   jax.ShapeDtypeStruct((N_TOK, LANES), F32),
                   jax.ShapeDtypeStruct((N_TOK, SGU_WIDTH), F32),
                   jax.ShapeDtypeStruct((N_TOK, SGU_WIDTH), BF16)],
        compiler_params=_cparams(("parallel",)),
        name="inproj",
    )(x2, mod3, wt, gq, gkv, sg, sb)


def _rope(x, cos, sin):
    w = x.shape[-1]
    lane = lax.broadcasted_iota(jnp.int32, x.shape, 1)
    first_half = (lane % QK_ROPE_DIM) < (QK_ROPE_DIM // 2)
    rot = jnp.where(first_half,
                    -pltpu.roll(x, w - QK_ROPE_DIM // 2, 1),
                    pltpu.roll(x, QK_ROPE_DIM // 2, 1))
    return x * cos + rot * sin


def _qkv_body(cq_ref, ckv_ref, kpe_ref, pos_ref, invf_ref, wuq_ref, wukv_ref, q_ref, k_ref, v_ref):
    ang = pos_ref[...].astype(F32) * invf_ref[...]
    cos1 = jnp.cos(ang)
    sin1 = jnp.sin(ang)
    reps = MLA_HEADS * QK_ROPE_DIM // LANES
    cos = jnp.concatenate([cos1] * reps, axis=1)
    sin = jnp.concatenate([sin1] * reps, axis=1)
    scale = np.float32(QK_DIM ** -0.5)
    q = jnp.dot(cq_ref[...], wuq_ref[...], preferred_element_type=F32) * scale
    q_pe = _rope(q[:, MLA_HEADS * QK_NOPE_DIM:], cos, sin)
    kv = jnp.dot(ckv_ref[...], wukv_ref[...], preferred_element_type=F32)
    k_pe = _rope(kpe_ref[...], cos1, sin1)[:, :QK_ROPE_DIM].astype(BF16)
    for h in range(MLA_HEADS):
        q_ref[0, h, :, 0:QK_NOPE_DIM] = q[:, h * QK_NOPE_DIM:(h + 1) * QK_NOPE_DIM].astype(BF16)
        q_ref[0, h, :, QK_NOPE_DIM:QK_DIM] = q_pe[:, h * QK_ROPE_DIM:(h + 1) * QK_ROPE_DIM].astype(BF16)
        k_ref[0, h, :, 0:QK_NOPE_DIM] = kv[:, h * QK_NOPE_DIM:(h + 1) * QK_NOPE_DIM].astype(BF16)
        k_ref[0, h, :, QK_NOPE_DIM:QK_DIM] = k_pe
        v_ref[0, h, :, :] = kv[:, MLA_WIDTH + h * V_HEAD_DIM:MLA_WIDTH + (h + 1) * V_HEAD_DIM].astype(BF16)


def _qkv(cqn, ckvn, kpe, pos2, invf, wuq, wukv):
    tm = TOK_TM
    tpb = SEQ // tm
    row = lambda w: pl.BlockSpec((tm, w), lambda i: (i, 0))
    head_out = lambda w: pl.BlockSpec((1, MLA_HEADS, tm, w), lambda i: (i // tpb, 0, i % tpb, 0))
    return pl.pallas_call(
        _qkv_body,
        grid=(N_TOK // tm,),
        in_specs=[row(Q_LORA_RANK), row(KV_LORA_RANK), row(LANES), row(1),
                  _const_spec(invf.shape), _const_spec(wuq.shape), _const_spec(wukv.shape)],
        out_specs=[head_out(QK_DIM), head_out(QK_DIM), head_out(V_HEAD_DIM)],
        out_shape=[jax.ShapeDtypeStruct((BATCH, MLA_HEADS, SEQ, QK_DIM), BF16),
                   jax.ShapeDtypeStruct((BATCH, MLA_HEADS, SEQ, QK_DIM), BF16),
                   jax.ShapeDtypeStruct((BATCH, MLA_HEADS, SEQ, V_HEAD_DIM), BF16)],
        compiler_params=_cparams(("parallel",)),
        name="qkv",
    )(cqn, ckvn, kpe, pos2, invf, wuq, wukv)


def _attn_body(q_ref, k_ref, v_ref, o_ref):
    i = pl.program_id(2)

    def step(h, j, carry, masked):
        m, l, acc = carry
        start = pl.multiple_of(j * ATT_TK, ATT_TK)
        k = k_ref[0, h, pl.ds(start, ATT_TK), :]
        v = v_ref[0, h, pl.ds(start, ATT_TK), :]
        s = lax.dot_general(q_ref[0, h], k, (((1,), (1,)), ((), ())), preferred_element_type=F32)
        if masked:
            r = lax.broadcasted_iota(jnp.int32, s.shape, 0)
            c = lax.broadcasted_iota(jnp.int32, s.shape, 1)
            s = jnp.where(c <= r, s, NEG_BIG)
        m_new = jnp.maximum(m, jnp.max(s, axis=-1, keepdims=True))
        p = jnp.exp(s - m_new)
        a = jnp.exp(m - m_new)
        l = a * l + jnp.sum(p, axis=-1, keepdims=True)
        acc = a * acc + jnp.dot(p.astype(BF16), v, preferred_element_type=F32)
        return m_new, l, acc

    def steps(j, carries, masked):
        return tuple(step(h, j, carries[h], masked) for h in range(ATT_HEADS))

    init = tuple((jnp.full((ATT_TQ, 1), NEG_BIG, F32), jnp.zeros((ATT_TQ, 1), F32),
                  jnp.zeros((ATT_TQ, V_HEAD_DIM), F32)) for _ in range(ATT_HEADS))
    carries = lax.fori_loop(0, i, lambda j, c: steps(j, c, False), init)
    carries = steps(i, carries, True)
    for h, (m, l, acc) in enumerate(carries):
        o_ref[0, :, h * V_HEAD_DIM:(h + 1) * V_HEAD_DIM] = (acc / l).astype(BF16)


def _attn(q, k, v):
    assert ATT_TQ == ATT_TK
    hb = ATT_HEADS
    return pl.pallas_call(
        _attn_body,
        grid=(BATCH, MLA_HEADS // hb, SEQ // ATT_TQ),
        in_specs=[pl.BlockSpec((1, hb, ATT_TQ, QK_DIM), lambda b, h, i: (b, h, i, 0)),
                  pl.BlockSpec((1, hb, SEQ, QK_DIM), lambda b, h, i: (b, h, 0, 0)),
                  pl.BlockSpec((1, hb, SEQ, V_HEAD_DIM), lambda b, h, i: (b, h, 0, 0))],
        out_specs=pl.BlockSpec((1, ATT_TQ, hb * V_HEAD_DIM), lambda b, h, i: (b, i, h)),
        out_shape=jax.ShapeDtypeStruct((BATCH, SEQ, MLA_WIDTH), BF16),
        compiler_params=_cparams(("parallel", "parallel", "arbitrary")),
        name="attn",
    )(q, k, v)


def _mixout_body(x_ref, mod_ref, attn_ref, u_ref, vs_ref, wsp_ref, bsp_ref, woa_ref, wos_ref,
                 g1_ref, b1_ref, wr_ref, br_ref, x1_ref, h2_ref, lg_ref, sgu_scr):
    r = lax.broadcasted_iota(jnp.int32, (SGU_CHUNK, SGU_CHUNK), 0)
    c = lax.broadcasted_iota(jnp.int32, (SGU_CHUNK, SGU_CHUNK), 1)
    causal = c <= r
    for g in range(SGU_GROUPS):
        ws = jnp.where(causal, wsp_ref[g], 0.0).astype(BF16)
        bias = bsp_ref[:, g:g + 1]
        cols = slice(g * SGU_GROUP_DIM, (g + 1) * SGU_GROUP_DIM)
        for ch in range(MIX_TM // SGU_CHUNK):
            rows = slice(ch * SGU_CHUNK, (ch + 1) * SGU_CHUNK)
            mixed = jnp.dot(ws, vs_ref[rows, cols], preferred_element_type=F32) + bias
            sgu_scr[rows, cols] = (u_ref[rows, cols] * mixed).astype(BF16)
    y = (jnp.dot(attn_ref[...], woa_ref[...], preferred_element_type=F32)
         + jnp.dot(sgu_scr[...], wos_ref[...], preferred_element_type=F32))
    gate1 = mod_ref[0, 2:3, :]
    sh2 = mod_ref[0, 3:4, :]
    sc2 = mod_ref[0, 4:5, :]
    x1 = _ln_rows(DEEPNORM_ALPHA * x_ref[...] + gate1 * y) * g1_ref[...] + b1_ref[...]
    x1_ref[...] = x1
    h2 = _ln_rows(x1) * (1.0 + sc2) + sh2
    _store_token_tiles(h2_ref, _pack_halves(h2))
    lg_ref[...] = jnp.dot(h2.astype(BF16), wr_ref[...], preferred_element_type=F32) + br_ref[...]


def _mixout(x2, mod3, attn, u, vs, wsp, bsp_t, wo, g1, b1, wr, br):
    tm = MIX_TM
    tpb = SEQ // tm
    row = lambda w: pl.BlockSpec((tm, w), lambda i: (i, 0))
    wo_half = lambda j: pl.BlockSpec((MLA_WIDTH, D_MODEL), lambda i: (j, 0), pipeline_mode=pl.Buffered(1))
    return pl.pallas_call(
        _mixout_body,
        grid=(N_TOK // tm,),
        in_specs=[row(D_MODEL),
                  pl.BlockSpec((1, N_MOD, D_MODEL), lambda i: (i // tpb, 0, 0)),
                  row(MLA_WIDTH), row(SGU_WIDTH), row(SGU_WIDTH),
                  _const_spec(wsp.shape), _const_spec(bsp_t.shape), wo_half(0), wo_half(1),
                  _const_spec(g1.shape), _const_spec(b1.shape), _const_spec(wr.shape), _const_spec(br.shape)],
        out_specs=[row(D_MODEL), pl.BlockSpec((tm * SUBLANES, LANES), lambda i: (i, 0)), row(LANES)],
        out_shape=[jax.ShapeDtypeStruct((N_TOK, D_MODEL), F32),
                   jax.ShapeDtypeStruct((N_TOK * SUBLANES, LANES), U32),
                   jax.ShapeDtypeStruct((N_TOK, LANES), F32)],
        scratch_shapes=[pltpu.VMEM((tm, SGU_WIDTH), BF16)],
        compiler_params=_cparams(("parallel",)),
        name="mix_out",
    )(x2, mod3, attn, u, vs, wsp, bsp_t, wo, wo, g1, b1, wr, br)


def _route_math(lg):
    lane = lax.broadcasted_iota(jnp.int32, lg.shape, 1)
    big = jnp.int32(LANES)

    def top1(vals):
        m = jnp.max(vals, axis=-1, keepdims=True)
        idx = jnp.min(jnp.where(vals == m, lane, big), axis=-1, keepdims=True)
        return m, idx

    is_group = lane < N_GROUPS
    glog = jnp.where(is_group, lg, -jnp.inf)
    gmax, gidx = top1(glog)
    pg_top = 1.0 / jnp.sum(jnp.exp(glog - gmax), axis=-1, keepdims=True)
    eid = lane - N_GROUPS
    sel = (eid >= gidx * EXPERTS_PER_GROUP) & (eid < (gidx + 1) * EXPERTS_PER_GROUP)
    elog = jnp.where(sel, lg, -jnp.inf)
    m1, i1 = top1(elog)
    m2, i2 = top1(jnp.where(lane == i1, -jnp.inf, elog))
    e2 = jnp.exp(m2 - m1)
    w1 = pg_top / (1.0 + e2)
    w2 = pg_top * e2 / (1.0 + e2)
    return jnp.where(lane == 0, (i1 - N_GROUPS).astype(F32),
                     jnp.where(lane == 1, (i2 - N_GROUPS).astype(F32),
                               jnp.where(lane == 2, w1, jnp.where(lane == 3, w2, 0.0))))


def _rank_math(rt, counts):
    t = rt.shape[0]
    lane = lax.broadcasted_iota(jnp.int32, (t, LANES), 1).astype(F32)
    oh0 = lane == rt[:, 0:1]
    oh1 = lane == rt[:, 1:2]
    s = jnp.where(oh0 | oh1, 1.0, 0.0)
    r = lax.broadcasted_iota(jnp.int32, (t, t), 0)
    c = lax.broadcasted_iota(jnp.int32, (t, t), 1)
    before = jnp.where(c < r, 1.0, 0.0).astype(BF16)
    csum = jnp.dot(before, s.astype(BF16), preferred_element_type=F32) + counts
    rank0 = jnp.sum(jnp.where(oh0, csum, 0.0), axis=-1, keepdims=True)
    rank1 = jnp.sum(jnp.where(oh1, csum, 0.0), axis=-1, keepdims=True)
    return rank0, rank1, counts + jnp.sum(s, axis=0, keepdims=True)


def _route_body(lg_ref, rt_ref, cnt_ref, cnt_scr):
    @pl.when(pl.program_id(0) == 0)
    def _():
        cnt_scr[...] = jnp.zeros_like(cnt_scr)

    rt = _route_math(lg_ref[...])
    rank0, rank1, counts = _rank_math(rt, cnt_scr[...])
    cnt_scr[...] = counts
    lane = lax.broadcasted_iota(jnp.int32, rt.shape, 1)
    rt_ref[...] = jnp.where(lane == RT_RANK, rank0, jnp.where(lane == RT_RANK + 1, rank1, rt))
    cnt_ref[...] = jnp.broadcast_to(counts, cnt_ref.shape)


def _route(logits):
    tm = ROUTE_TM
    return pl.pallas_call(
        _route_body,
        grid=(N_TOK // tm,),
        in_specs=[pl.BlockSpec((tm, LANES), lambda i: (i, 0))],
        out_specs=[pl.BlockSpec((tm, LANES), lambda i: (i, 0)),
                   pl.BlockSpec((SUBLANES, LANES), lambda i: (0, 0))],
        out_shape=[jax.ShapeDtypeStruct((N_TOK, LANES), F32),
                   jax.ShapeDtypeStruct((SUBLANES, LANES), F32)],
        scratch_shapes=[pltpu.VMEM((1, LANES), F32)],
        compiler_params=_cparams(("arbitrary",)),
        name="route",
    )(logits)


def _plan_body(rt_ref, cnt_ref, pos_ref, tt_ref):
    t = PLAN_TM
    lane = lax.broadcasted_iota(jnp.int32, (t, LANES), 1)
    rt = rt_ref[...]
    oh0 = lane.astype(F32) == rt[:, 0:1]
    oh1 = lane.astype(F32) == rt[:, 1:2]
    if True:
        counts = cnt_ref[0:1, :]
        tiles = jnp.floor((counts + (MOE_TM - 1)) * (1.0 / MOE_TM))
        r = lax.broadcasted_iota(jnp.int32, (LANES, LANES), 0)
        c = lax.broadcasted_iota(jnp.int32, (LANES, LANES), 1)
        upto = jnp.where(r <= c, 1.0, 0.0).astype(BF16)
        tiles8 = jnp.broadcast_to(tiles, (8, LANES)).astype(BF16)
        tile_end = jnp.dot(tiles8, upto, preferred_element_type=F32)[0:1]
        offs = (tile_end - tiles) * MOE_TM
        p0 = jnp.sum(jnp.where(oh0, offs, 0.0), axis=-1, keepdims=True) + rt[:, RT_RANK:RT_RANK + 1]
        p1 = jnp.sum(jnp.where(oh1, offs, 0.0), axis=-1, keepdims=True) + rt[:, RT_RANK + 1:RT_RANK + 2]
        pos_ref[...] = (jnp.where(lane == 0, p0, jnp.where(lane == 1, p1, 0.0)) * SUBLANES).astype(jnp.int32)

        lane_e = lax.broadcasted_iota(jnp.int32, (LANES, LANES), 1)
        tile_id = lax.broadcasted_iota(jnp.int32, (LANES, LANES), 0).astype(F32)
        is_e = lane_e < N_EXPERTS
        total = jnp.max(tile_end, axis=-1, keepdims=True)
        t_exp = jnp.sum(jnp.where(is_e & (tile_end <= tile_id), 1.0, 0.0), axis=-1, keepdims=True)
        t_valid = jnp.where(tile_id[:, 0:1] < total, 1.0, 0.0)
        last_exp = jnp.sum(jnp.where(is_e & (tile_end <= total - 1.0), 1.0, 0.0), axis=-1, keepdims=True)
        t_exp = jnp.where(t_valid > 0, t_exp, last_exp)
        t_first = jnp.sum(jnp.where(is_e & (tiles > 0) & ((tile_end - tiles) == tile_id), 1.0, 0.0),
                          axis=-1, keepdims=True)
        t_last = jnp.sum(jnp.where(is_e & (tiles > 0) & ((tile_end - 1.0) == tile_id), 1.0, 0.0),
                         axis=-1, keepdims=True)
        t_clear = jnp.maximum(t_last, 1.0 - t_valid)
        none = jnp.float32(LANES)
        owns = is_e & (tiles > 0)
        lane_f = lane_e.astype(F32)
        t_next = jnp.min(jnp.where(owns & (lane_f > t_exp), lane_f, none), axis=-1, keepdims=True)
        t_next2 = jnp.min(jnp.where(owns & (lane_f > t_next), lane_f, none), axis=-1, keepdims=True)
        t_ord = jnp.sum(jnp.where(owns & (lane_f < t_exp), 1.0, 0.0), axis=-1, keepdims=True)
        cols = {T_EXP: t_exp, T_VALID: t_valid, T_FIRST: t_first, T_CLEAR: t_clear,
                T_NEXT: t_next, T_NEXT2: t_next2, T_ORD: t_ord}
        table = jnp.zeros((LANES, LANES), F32)
        for k, col in cols.items():
            table = jnp.where(lane_e == k, col, table)
        tt_ref[...] = table.astype(jnp.int32)


def _plan(route, counts):
    t = PLAN_TM
    return pl.pallas_call(
        _plan_body,
        grid=(N_TOK // t,),
        in_specs=[pl.BlockSpec((t, LANES), lambda i: (i, 0)),
                  pl.BlockSpec((SUBLANES, LANES), lambda i: (0, 0))],
        out_specs=[pl.BlockSpec((t, LANES), lambda i: (i, 0)),
                   pl.BlockSpec((LANES, LANES), lambda i: (0, 0))],
        out_shape=[jax.ShapeDtypeStruct((N_TOK, LANES), jnp.int32),
                   jax.ShapeDtypeStruct((LANES, LANES), jnp.int32)],
        compiler_params=_cparams(("arbitrary",)),
        name="plan",
    )(route, counts)


def _rows_wait(ref, n_rows, sem):
    pltpu.make_async_copy(ref.at[pl.ds(0, n_rows)], ref.at[pl.ds(0, n_rows)], sem).wait()


def _dispatch_body(pos_ref, tt_ref, h_ref, xs_hbm, zbuf, sem_z, sem_s):
    i = pl.program_id(0)
    tile_rows = MOE_TM * SUBLANES

    @pl.when(i == 0)
    def _():
        zbuf[...] = _pack_halves(jnp.zeros((tile_rows, 2 * LANES), F32))

        def zero_copy(tile):
            start = pl.multiple_of(tile * tile_rows, tile_rows)
            return pltpu.make_async_copy(zbuf, xs_hbm.at[pl.ds(start, tile_rows)], sem_z)

        def clear_start(tile, carry):
            @pl.when(tt_ref[tile, T_CLEAR] > 0)
            def _():
                zero_copy(tile).start()
            return carry

        def clear_wait(tile, carry):
            @pl.when(tt_ref[tile, T_CLEAR] > 0)
            def _():
                zero_copy(tile).wait()
            return carry

        lax.fori_loop(0, MOE_TILES, clear_start, 0)
        lax.fori_loop(0, MOE_TILES, clear_wait, 0)

    def tok(j, carry):
        src = h_ref.at[pl.ds(pl.multiple_of(j * SUBLANES, SUBLANES), SUBLANES)]
        pair = TOP_K * (i * DISPATCH_TM + j)
        for k in range(TOP_K):
            dst_row = pl.multiple_of(pos_ref[pair + k], SUBLANES)
            pltpu.make_async_copy(src, xs_hbm.at[pl.ds(dst_row, SUBLANES)], sem_s).start(priority=k)
        return carry

    lax.fori_loop(0, DISPATCH_TM, tok, 0, unroll=8)
    _rows_wait(xs_hbm, TOP_K * DISPATCH_TM * SUBLANES, sem_s)


def _dispatch(pos_rows, tile_clear, h2p):
    grid_spec = pltpu.PrefetchScalarGridSpec(
        num_scalar_prefetch=2,
        grid=(N_TOK // DISPATCH_TM,),
        in_specs=[pl.BlockSpec((DISPATCH_TM * SUBLANES, LANES), lambda i, *_: (i, 0))],
        out_specs=pl.BlockSpec(memory_space=pl.ANY),
        scratch_shapes=[pltpu.VMEM((MOE_TM * SUBLANES, LANES), U32),
                        pltpu.SemaphoreType.DMA(()), pltpu.SemaphoreType.DMA(())],
    )
    return pl.pallas_call(
        _dispatch_body,
        grid_spec=grid_spec,
        out_shape=jax.ShapeDtypeStruct((MOE_ROWS * SUBLANES, LANES), U32),
        compiler_params=_cparams(("arbitrary",)),
        name="dispatch",
    )(pos_rows, tile_clear, h2p)


def _moe_body(tt_ref, x_ref, wg_hbm, wu_hbm, wd_hbm, y_ref,
              wg_s, wu_s, wd_s, stg_g, stg_u, stg_d, sem):
    i = pl.program_id(0)

    def fetch(e, slot):
        return (pltpu.make_async_copy(wg_hbm.at[e], stg_g.at[slot], sem.at[slot, 0]),
                pltpu.make_async_copy(wu_hbm.at[e], stg_u.at[slot], sem.at[slot, 1]),
                pltpu.make_async_copy(wd_hbm.at[e], stg_d.at[slot], sem.at[slot, 2]))

    @pl.when(i == 0)
    def _():
        for cp in fetch(tt_ref[0, T_EXP], 0):
            cp.start()

        @pl.when(tt_ref[0, T_NEXT] < N_EXPERTS)
        def _():
            for cp in fetch(tt_ref[0, T_NEXT], 1):
                cp.start()

    @pl.when(tt_ref[i, T_FIRST] > 0)
    def _():
        slot = tt_ref[i, T_ORD] % 2
        for cp in fetch(tt_ref[i, T_EXP], slot):
            cp.wait()
        wg_s[...] = stg_g[slot].astype(BF16)
        wu_s[...] = stg_u[slot].astype(BF16)
        wd_s[...] = stg_d[slot].astype(BF16)

        @pl.when(tt_ref[i, T_NEXT2] < N_EXPERTS)
        def _():
            for cp in fetch(tt_ref[i, T_NEXT2], slot):
                cp.start()

    @pl.when(tt_ref[i, T_VALID] > 0)
    def _():
        lo, hi = _unpack_halves(_load_token_tiles(x_ref, 0, MOE_TM))
        xa = lo.astype(BF16)
        xb = hi.astype(BF16)
        g = (jnp.dot(xa, wg_s[:HALF_D, :], preferred_element_type=F32)
             + jnp.dot(xb, wg_s[HALF_D:, :], preferred_element_type=F32))
        u = (jnp.dot(xa, wu_s[:HALF_D, :], preferred_element_type=F32)
             + jnp.dot(xb, wu_s[HALF_D:, :], preferred_element_type=F32))
        hid = (g * jax.nn.sigmoid(g) * u).astype(BF16)
        _store_token_tiles(y_ref, _pack_halves(jnp.dot(hid, wd_s[...], preferred_element_type=F32)))

    @pl.when(tt_ref[i, T_VALID] == 0)
    def _():
        y_ref[...] = _pack_halves(jnp.zeros((MOE_TM * SUBLANES, 2 * LANES), F32))


def _moe(tile_tab, xs, wg, wu, wd):
    tm = MOE_TM
    grid_spec = pltpu.PrefetchScalarGridSpec(
        num_scalar_prefetch=1,
        grid=(MOE_TILES,),
        in_specs=[pl.BlockSpec((tm * SUBLANES, LANES), lambda i, *_: (i, 0)),
                  pl.BlockSpec(memory_space=pl.ANY), pl.BlockSpec(memory_space=pl.ANY),
                  pl.BlockSpec(memory_space=pl.ANY)],
        out_specs=pl.BlockSpec((tm * SUBLANES, LANES), lambda i, *_: (i, 0)),
        scratch_shapes=[pltpu.VMEM((D_MODEL, EXPERT_FF), BF16), pltpu.VMEM((D_MODEL, EXPERT_FF), BF16),
                        pltpu.VMEM((EXPERT_FF, D_MODEL), BF16),
                        pltpu.VMEM((2, D_MODEL, EXPERT_FF), F32), pltpu.VMEM((2, D_MODEL, EXPERT_FF), F32),
                        pltpu.VMEM((2, EXPERT_FF, D_MODEL), F32),
                        pltpu.SemaphoreType.DMA((2, 3))],
    )
    return pl.pallas_call(
        _moe_body,
        grid_spec=grid_spec,
        out_shape=jax.ShapeDtypeStruct((MOE_ROWS * SUBLANES, LANES), U32),
        compiler_params=_cparams(("arbitrary",)),
        name="moe",
    )(tile_tab, xs, wg, wu, wd)


def _final_body(pos_ref, x1_ref, mod_ref, rt_ref, g2_ref, b2_ref, ys_hbm, o_ref, buf, sem):
    i = pl.program_id(0)
    n = pl.num_programs(0)
    tm = FINAL_TM
    slot = i % FINAL_SLOTS
    nxt_slot = (i + 2) % FINAL_SLOTS
    nxt_tile = jnp.minimum(i + 2, n - 1)

    def issue(tile, dst_slot, j):
        pair = TOP_K * (tile * tm + j)
        for k in range(TOP_K):
            src_row = pl.multiple_of(pos_ref[pair + k], SUBLANES)
            dst_row = pl.multiple_of((k * tm + j) * SUBLANES, SUBLANES)
            pltpu.make_async_copy(ys_hbm.at[pl.ds(src_row, SUBLANES)],
                                  buf.at[dst_slot, pl.ds(dst_row, SUBLANES)], sem.at[dst_slot]).start(priority=k)

    def wait(s):
        pltpu.make_async_copy(ys_hbm.at[pl.ds(0, TOP_K * tm * SUBLANES)], buf.at[s], sem.at[s]).wait()

    @pl.when(i == 0)
    def _():
        def tok(j, carry):
            issue(0, 0, j)
            issue(jnp.minimum(1, n - 1), 1, j)
            return carry
        lax.fori_loop(0, tm, tok, 0, unroll=8)

    wait(slot)
    gate2 = mod_ref[0, 5:6, :]
    cur = buf.at[slot]

    def chunk(c, carry):
        r0 = pl.multiple_of(c * FINAL_CHUNK, FINAL_CHUNK)
        rows = pl.ds(r0, FINAL_CHUNK)
        a_lo, a_hi = _unpack_halves(_load_token_tiles(cur, r0, FINAL_CHUNK))
        b_lo, b_hi = _unpack_halves(_load_token_tiles(cur, tm + r0, FINAL_CHUNK))
        x1 = x1_ref[rows, :]
        rt = rt_ref[rows, :]
        for r in range(FINAL_CHUNK):
            issue(nxt_tile, nxt_slot, r0 + r)
        w0 = rt[:, 2:3]
        w1 = rt[:, 3:4]
        y = jnp.concatenate([w0 * a_lo + w1 * b_lo, w0 * a_hi + w1 * b_hi], axis=1)
        o_ref[rows, :] = _ln_rows(DEEPNORM_ALPHA * x1 + gate2 * y) * g2_ref[...] + b2_ref[...]
        return carry

    lax.fori_loop(0, tm // FINAL_CHUNK, chunk, 0)

    @pl.when(i == n - 1)
    def _():
        wait((i + 1) % FINAL_SLOTS)
        wait(nxt_slot)


def _final(pos_rows, x1, mod3, route, g2, b2, ys):
    tm = FINAL_TM
    tpb = SEQ // tm
    row = lambda w: pl.BlockSpec((tm, w), lambda i, *_: (i, 0))
    grid_spec = pltpu.PrefetchScalarGridSpec(
        num_scalar_prefetch=1,
        grid=(N_TOK // tm,),
        in_specs=[row(D_MODEL),
                  pl.BlockSpec((1, N_MOD, D_MODEL), lambda i, *_: (i // tpb, 0, 0)),
                  row(LANES),
                  pl.BlockSpec(g2.shape, lambda i, *_: (0, 0)),
                  pl.BlockSpec(b2.shape, lambda i, *_: (0, 0)),
                  pl.BlockSpec(memory_space=pl.ANY)],
        out_specs=row(D_MODEL),
        scratch_shapes=[pltpu.VMEM((FINAL_SLOTS, TOP_K * tm * SUBLANES, LANES), U32),
                        pltpu.SemaphoreType.DMA((FINAL_SLOTS,))],
    )
    return pl.pallas_call(
        _final_body,
        grid_spec=grid_spec,
        out_shape=jax.ShapeDtypeStruct((N_TOK, D_MODEL), F32),
        compiler_params=_cparams(("arbitrary",)),
        name="final",
    )(pos_rows, x1, mod3, route, g2, b2, ys)


def kernel(x, c, positions, w_ada, b_ada, w_in, q_norm_g, w_uq, kv_norm_g, w_ukv, sgu_norm_g, sgu_norm_b,
           w_spatial, b_spatial, w_o, ln1_g, ln1_b, w_router_group, b_router_group, w_router_expert,
           b_router_expert, w_gate, w_up, w_down, ln2_g, ln2_b):
    l = 0
    x2 = x.reshape(N_TOK, D_MODEL)
    mod3 = _ada(c, w_ada[l], b_ada[l][None, :]).reshape(BATCH, N_MOD, D_MODEL)

    wt, wuq, wukv, wo = _prep(w_in[l].T, w_uq[l], w_ukv[l], w_o[l])
    n_r = N_GROUPS + N_EXPERTS
    wr = jnp.pad(jnp.concatenate([w_router_group[l], w_router_expert[l]], axis=1),
                 ((0, 0), (0, LANES - n_r))).astype(BF16)
    br = jnp.pad(jnp.concatenate([b_router_group[l], b_router_expert[l]]), (0, LANES - n_r))[None, :]
    inv_freq = 1.0 / (ROPE_THETA ** (jnp.arange(0, QK_ROPE_DIM, 2, dtype=F32) / QK_ROPE_DIM))
    invf = jnp.tile(inv_freq, 2 * LANES // QK_ROPE_DIM)[None, :]

    cqn, ckvn, kpe, u, vs = _inproj(x2, mod3, wt, q_norm_g[l][None, :], kv_norm_g[l][None, :],
                                    sgu_norm_g[l][None, :], sgu_norm_b[l][None, :])
    q, k, v = _qkv(cqn, ckvn, kpe, positions.reshape(N_TOK, 1), invf, wuq, wukv)
    attn = _attn(q, k, v).reshape(N_TOK, MLA_WIDTH)
    x1, h2, logits = _mixout(x2, mod3, attn, u, vs, w_spatial[l], b_spatial[l].T, wo,
                             ln1_g[l][None, :], ln1_b[l][None, :], wr, br)
    route, counts = _route(logits)
    pos_tab, tile_tab = _plan(route, counts)
    pos_rows = pos_tab[:, 0:TOP_K].reshape(-1)
    xs = _dispatch(pos_rows, tile_tab, h2)
    ys = _moe(tile_tab, xs, w_gate[l], w_up[l], w_down[l])
    out = _final(pos_rows, x1, mod3, route, ln2_g[l][None, :], ln2_b[l][None, :], ys)
    return out.reshape(BATCH, SEQ, D_MODEL)
```

```python
import functools

import jax
import jax.numpy as jnp
import numpy as np
from jax import lax
from jax.experimental import pallas as pl
from jax.experimental.pallas import tpu as pltpu

D_MODEL = 2048
BATCH = 4
SEQ = 2048
N_TOK = BATCH * SEQ

MLA_HEADS = 8
QK_NOPE_DIM = 128
QK_ROPE_DIM = 64
QK_DIM = QK_NOPE_DIM + QK_ROPE_DIM
V_HEAD_DIM = 128
Q_LORA_RANK = 768
KV_LORA_RANK = 512
ROPE_THETA = 10000.0
MLA_WIDTH = MLA_HEADS * V_HEAD_DIM

SGU_GROUPS = 8
SGU_GROUP_DIM = 128
SGU_CHUNK = 128
SGU_WIDTH = SGU_GROUPS * SGU_GROUP_DIM

N_GROUPS = 4
EXPERTS_PER_GROUP = 8
N_EXPERTS = N_GROUPS * EXPERTS_PER_GROUP
TOP_K = 2
EXPERT_FF = 512

DEEPNORM_ALPHA = 2.0 ** 0.25
EPS = 1e-6
N_MOD = 6
NEG_BIG = -1e30

LANES = 128
SUBLANES = 8
VMEM_LIMIT = 56 * 1024 * 1024

ADA_TN = 2048
TOK_TM = 512
MIX_TM = 256
PREP_STEPS = 4
ATT_TQ = 512
ATT_TK = 512
ATT_HEADS = 4
MOE_TM = 256
MOE_TILES = (N_TOK * TOP_K + N_EXPERTS * (MOE_TM - 1)) // MOE_TM + 1
MOE_ROWS = MOE_TILES * MOE_TM
PLAN_TM = 2048
ROUTE_TM = 1024
RT_RANK = 4
DISPATCH_TM = 1024
FINAL_TM = 256
FINAL_CHUNK = 128
FINAL_SLOTS = 3
assert MOE_TILES <= LANES
T_EXP, T_VALID, T_FIRST, T_CLEAR, T_NEXT, T_NEXT2, T_ORD, T_LAST = range(8)

F32 = jnp.float32
BF16 = jnp.bfloat16
U32 = jnp.uint32
HALF_D = D_MODEL // 2


def _cparams(sem):
    return pltpu.CompilerParams(dimension_semantics=sem, vmem_limit_bytes=VMEM_LIMIT)


def _const_spec(shape):
    nd = len(shape)
    return pl.BlockSpec(shape, lambda *_: (0,) * nd, pipeline_mode=pl.Buffered(1))


def _ln_rows(x):
    mu = jnp.mean(x, axis=-1, keepdims=True)
    xc = x - mu
    var = jnp.mean(xc * xc, axis=-1, keepdims=True)
    return xc * lax.rsqrt(var + EPS)


def _rms_rows(x):
    return x * lax.rsqrt(jnp.mean(x * x, axis=-1, keepdims=True) + EPS)


def _pack_halves(x):
    half = x.shape[-1] // 2
    return pltpu.pack_elementwise([x[:, :half], x[:, half:]], packed_dtype=BF16)


def _unpack_halves(w):
    lo = pltpu.unpack_elementwise(w, index=0, packed_dtype=BF16, unpacked_dtype=F32)
    hi = pltpu.unpack_elementwise(w, index=1, packed_dtype=BF16, unpacked_dtype=F32)
    return lo, hi


def _store_token_tiles(ref, w):
    rows = w.shape[0]
    for s in range(SUBLANES):
        ref[pl.ds(s, rows, stride=SUBLANES), :] = w[:, s * LANES:(s + 1) * LANES]


def _load_token_tiles(ref, start_row, rows):
    return jnp.concatenate([ref[pl.ds(start_row * SUBLANES + s, rows, stride=SUBLANES), :]
                            for s in range(SUBLANES)], axis=1)


def _gelu_tanh(x):
    c = np.sqrt(2.0 / np.pi).astype(np.float32)
    return 0.5 * x * (1.0 + jnp.tanh(c * (x + 0.044715 * (x * x * x))))


def _ada_body(c_ref, w_ref, b_ref, o_ref):
    o_ref[...] = jnp.dot(c_ref[...].astype(BF16), w_ref[...].astype(BF16),
                         preferred_element_type=F32) + b_ref[...]


def _ada(c, w, b):
    n = w.shape[1]
    return pl.pallas_call(
        _ada_body,
        grid=(n // ADA_TN,),
        in_specs=[pl.BlockSpec((BATCH, D_MODEL), lambda j: (0, 0)),
                  pl.BlockSpec((D_MODEL, ADA_TN), lambda j: (0, j)),
                  pl.BlockSpec((1, ADA_TN), lambda j: (0, j))],
        out_specs=pl.BlockSpec((BATCH, ADA_TN), lambda j: (0, j)),
        out_shape=jax.ShapeDtypeStruct((BATCH, n), F32),
        compiler_params=_cparams(("parallel",)),
        name="ada",
    )(c, w, b)


def _prep_body(win_ref, wuq_ref, wukv_ref, wo_ref, win_o, wuq_o, wukv_o, wo_o):
    win_o[...] = win_ref[...].astype(BF16)
    u = wuq_ref[...]
    nope = [u[:, h * QK_DIM:h * QK_DIM + QK_NOPE_DIM] for h in range(MLA_HEADS)]
    rope = [u[:, h * QK_DIM + QK_NOPE_DIM:(h + 1) * QK_DIM] for h in range(MLA_HEADS)]
    wuq_o[...] = jnp.concatenate(nope + rope, axis=1).astype(BF16)
    kv = wukv_ref[...]
    hw = QK_NOPE_DIM + V_HEAD_DIM
    kn = [kv[:, h * hw:h * hw + QK_NOPE_DIM] for h in range(MLA_HEADS)]
    vv = [kv[:, h * hw + QK_NOPE_DIM:(h + 1) * hw] for h in range(MLA_HEADS)]
    wukv_o[...] = jnp.concatenate(kn + vv, axis=1).astype(BF16)
    wo_o[...] = wo_ref[...].astype(BF16)


def _prep(w_in_t, w_uq, w_ukv, w_o):
    steps = PREP_STEPS
    blk = lambda a: pl.BlockSpec((a.shape[0] // steps, a.shape[1]), lambda i: (i, 0))
    ins = (w_in_t, w_uq, w_ukv, w_o)
    return pl.pallas_call(
        _prep_body,
        grid=(steps,),
        in_specs=[blk(a) for a in ins],
        out_specs=[blk(a) for a in ins],
        out_shape=[jax.ShapeDtypeStruct(a.shape, BF16) for a in ins],
        compiler_params=_cparams(("parallel",)),
        name="prep",
    )(*ins)


def _inproj_body(x_ref, mod_ref, wt_ref, gq_ref, gkv_ref, sg_ref, sb_ref,
                 cq_ref, ckv_ref, kpe_ref, u_ref, vs_ref):
    o1, o2, o3 = Q_LORA_RANK, Q_LORA_RANK + KV_LORA_RANK, Q_LORA_RANK + KV_LORA_RANK + QK_ROPE_DIM

    def proj(lo, hi):
        return lax.dot_general(h, wt_ref[lo:hi, :], (((1,), (1,)), ((), ())), preferred_element_type=F32)

    sh = mod_ref[0, 0:1, :]
    sc = mod_ref[0, 1:2, :]
    h = (_ln_rows(x_ref[...]) * (1.0 + sc) + sh).astype(BF16)
    cq_ref[...] = (_rms_rows(proj(0, o1)) * gq_ref[...]).astype(BF16)
    ckv_ref[...] = (_rms_rows(proj(o1, o2)) * gkv_ref[...]).astype(BF16)
    kpe_ref[...] = proj(o2, o2 + LANES)
    gz = _gelu_tanh(proj(o3, o3 + 2 * SGU_WIDTH))
    u_ref[...] = gz[:, :SGU_WIDTH]
    vs_ref[...] = (_ln_rows(gz[:, SGU_WIDTH:]) * sg_ref[...] + sb_ref[...]).astype(BF16)


def _inproj(x2, mod3, wt, gq, gkv, sg, sb):
    tm = TOK_TM
    tiles_per_batch = SEQ // tm
    row = lambda w: pl.BlockSpec((tm, w), lambda i: (i, 0))
    return pl.pallas_call(
        _inproj_body,
        grid=(N_TOK // tm,),
        in_specs=[row(D_MODEL),
                  pl.BlockSpec((1, N_MOD, D_MODEL), lambda i: (i // tiles_per_batch, 0, 0)),
                  _const_spec(wt.shape),
                  _const_spec(gq.shape), _const_spec(gkv.shape), _const_spec(sg.shape), _const_spec(sb.shape)],
        out_specs=[row(Q_LORA_RANK), row(KV_LORA_RANK), row(LANES), row(SGU_WIDTH), row(SGU_WIDTH)],
        out_shape=[jax.ShapeDtypeStruct((N_TOK, Q_LORA_RANK), BF16),
                   jax.ShapeDtypeStruct((N_TOK, KV_LORA_RANK), BF16),
                   jax.ShapeDtypeStruct((N_TOK, LANES), F32),
                   jax.ShapeDtypeStruct((N_TOK, SGU_WIDTH), F32),
                   jax.ShapeDtypeStruct((N_TOK, SGU_WIDTH), BF16)],
        compiler_params=_cparams(("parallel",)),
        name="inproj",
    )(x2, mod3, wt, gq, gkv, sg, sb)


def _rope(x, cos, sin):
    w = x.shape[-1]
    lane = lax.broadcasted_iota(jnp.int32, x.shape, 1)
    first_half = (lane % QK_ROPE_DIM) < (QK_ROPE_DIM // 2)
    rot = jnp.where(first_half,
                    -pltpu.roll(x, w - QK_ROPE_DIM // 2, 1),
                    pltpu.roll(x, QK_ROPE_DIM // 2, 1))
    return x * cos + rot * sin


def _qkv_body(cq_ref, ckv_ref, kpe_ref, pos_ref, invf_ref, wuq_ref, wukv_ref, q_ref, k_ref, v_ref):
    ang = pos_ref[...].astype(F32) * invf_ref[...]
    cos1 = jnp.cos(ang)
    sin1 = jnp.sin(ang)
    reps = MLA_HEADS * QK_ROPE_DIM // LANES
    cos = jnp.concatenate([cos1] * reps, axis=1)
    sin = jnp.concatenate([sin1] * reps, axis=1)
    scale = np.float32(QK_DIM ** -0.5)
    q = jnp.dot(cq_ref[...], wuq_ref[...], preferred_element_type=F32) * scale
    q_pe = _rope(q[:, MLA_HEADS * QK_NOPE_DIM:], cos, sin)
    kv = jnp.dot(ckv_ref[...], wukv_ref[...], preferred_element_type=F32)
    k_pe = _rope(kpe_ref[...], cos1, sin1)[:, :QK_ROPE_DIM].astype(BF16)
    for h in range(MLA_HEADS):
        q_ref[0, h, :, 0:QK_NOPE_DIM] = q[:, h * QK_NOPE_DIM:(h + 1) * QK_NOPE_DIM].astype(BF16)
        q_ref[0, h, :, QK_NOPE_DIM:QK_DIM] = q_pe[:, h * QK_ROPE_DIM:(h + 1) * QK_ROPE_DIM].astype(BF16)
        k_ref[0, h, :, 0:QK_NOPE_DIM] = kv[:, h * QK_NOPE_DIM:(h + 1) * QK_NOPE_DIM].astype(BF16)
        k_ref[0, h, :, QK_NOPE_DIM:QK_DIM] = k_pe
        v_ref[0, h, :, :] = kv[:, MLA_WIDTH + h * V_HEAD_DIM:MLA_WIDTH + (h + 1) * V_HEAD_DIM].astype(BF16)


def _qkv(cqn, ckvn, kpe, pos2, invf, wuq, wukv):
    tm = TOK_TM
    tpb = SEQ // tm
    row = lambda w: pl.BlockSpec((tm, w), lambda i: (i, 0))
    head_out = lambda w: pl.BlockSpec((1, MLA_HEADS, tm, w), lambda i: (i // tpb, 0, i % tpb, 0))
    return pl.pallas_call(
        _qkv_body,
        grid=(N_TOK // tm,),
        in_specs=[row(Q_LORA_RANK), row(KV_LORA_RANK), row(LANES), row(1),
                  _const_spec(invf.shape), _const_spec(wuq.shape), _const_spec(wukv.shape)],
        out_specs=[head_out(QK_DIM), head_out(QK_DIM), head_out(V_HEAD_DIM)],
        out_shape=[jax.ShapeDtypeStruct((BATCH, MLA_HEADS, SEQ, QK_DIM), BF16),
                   jax.ShapeDtypeStruct((BATCH, MLA_HEADS, SEQ, QK_DIM), BF16),
                   jax.ShapeDtypeStruct((BATCH, MLA_HEADS, SEQ, V_HEAD_DIM), BF16)],
        compiler_params=_cparams(("parallel",)),
        name="qkv",
    )(cqn, ckvn, kpe, pos2, invf, wuq, wukv)


def _attn_body(q_ref, k_ref, v_ref, o_ref):
    i = pl.program_id(2)

    def step(h, j, carry, masked):
        m, l, acc = carry
        start = pl.multiple_of(j * ATT_TK, ATT_TK)
        k = k_ref[0, h, pl.ds(start, ATT_TK), :]
        v = v_ref[0, h, pl.ds(start, ATT_TK), :]
        s = lax.dot_general(q_ref[0, h], k, (((1,), (1,)), ((), ())), preferred_element_type=F32)
        if masked:
            r = lax.broadcasted_iota(jnp.int32, s.shape, 0)
            c = lax.broadcasted_iota(jnp.int32, s.shape, 1)
            s = jnp.where(c <= r, s, NEG_BIG)
        m_new = jnp.maximum(m, jnp.max(s, axis=-1, keepdims=True))
        p = jnp.exp(s - m_new)
        a = jnp.exp(m - m_new)
        l = a * l + jnp.sum(p, axis=-1, keepdims=True)
        acc = a * acc + jnp.dot(p.astype(BF16), v, preferred_element_type=F32)
        return m_new, l, acc

    def steps(j, carries, masked):
        return tuple(step(h, j, carries[h], masked) for h in range(ATT_HEADS))

    init = tuple((jnp.full((ATT_TQ, 1), NEG_BIG, F32), jnp.zeros((ATT_TQ, 1), F32),
                  jnp.zeros((ATT_TQ, V_HEAD_DIM), F32)) for _ in range(ATT_HEADS))
    carries = lax.fori_loop(0, i, lambda j, c: steps(j, c, False), init)
    carries = steps(i, carries, True)
    for h, (m, l, acc) in enumerate(carries):
        o_ref[0, :, h * V_HEAD_DIM:(h + 1) * V_HEAD_DIM] = (acc / l).astype(BF16)


def _attn(q, k, v):
    assert ATT_TQ == ATT_TK
    hb = ATT_HEADS
    return pl.pallas_call(
        _attn_body,
        grid=(BATCH, MLA_HEADS // hb, SEQ // ATT_TQ),
        in_specs=[pl.BlockSpec((1, hb, ATT_TQ, QK_DIM), lambda b, h, i: (b, h, i, 0)),
                  pl.BlockSpec((1, hb, SEQ, QK_DIM), lambda b, h, i: (b, h, 0, 0)),
                  pl.BlockSpec((1, hb, SEQ, V_HEAD_DIM), lambda b, h, i: (b, h, 0, 0))],
        out_specs=pl.BlockSpec((1, ATT_TQ, hb * V_HEAD_DIM), lambda b, h, i: (b, i, h)),
        out_shape=jax.ShapeDtypeStruct((BATCH, SEQ, MLA_WIDTH), BF16),
        compiler_params=_cparams(("parallel", "parallel", "arbitrary")),
        name="attn",
    )(q, k, v)


def _mixout_body(x_ref, mod_ref, attn_ref, u_ref, vs_ref, wsp_ref, bsp_ref, woa_ref, wos_ref,
                 g1_ref, b1_ref, wr_ref, br_ref, x1_ref, h2_ref, lg_ref, sgu_scr):
    r = lax.broadcasted_iota(jnp.int32, (SGU_CHUNK, SGU_CHUNK), 0)
    c = lax.broadcasted_iota(jnp.int32, (SGU_CHUNK, SGU_CHUNK), 1)
    causal = c <= r
    for g in range(SGU_GROUPS):
        ws = jnp.where(causal, wsp_ref[g], 0.0).astype(BF16)
        bias = bsp_ref[:, g:g + 1]
        cols = slice(g * SGU_GROUP_DIM, (g + 1) * SGU_GROUP_DIM)
        for ch in range(MIX_TM // SGU_CHUNK):
            rows = slice(ch * SGU_CHUNK, (ch + 1) * SGU_CHUNK)
            mixed = jnp.dot(ws, vs_ref[rows, cols], preferred_element_type=F32) + bias
            sgu_scr[rows, cols] = (u_ref[rows, cols] * mixed).astype(BF16)
    y = (jnp.dot(attn_ref[...], woa_ref[...], preferred_element_type=F32)
         + jnp.dot(sgu_scr[...], wos_ref[...], preferred_element_type=F32))
    gate1 = mod_ref[0, 2:3, :]
    sh2 = mod_ref[0, 3:4, :]
    sc2 = mod_ref[0, 4:5, :]
    x1 = _ln_rows(DEEPNORM_ALPHA * x_ref[...] + gate1 * y) * g1_ref[...] + b1_ref[...]
    x1_ref[...] = x1
    h2 = _ln_rows(x1) * (1.0 + sc2) + sh2
    _store_token_tiles(h2_ref, _pack_halves(h2))
    lg_ref[...] = jnp.dot(h2.astype(BF16), wr_ref[...], preferred_element_type=F32) + br_ref[...]


def _mixout(x2, mod3, attn, u, vs, wsp, bsp_t, wo, g1, b1, wr, br):
    tm = MIX_TM
    tpb = SEQ // tm
    row = lambda w: pl.BlockSpec((tm, w), lambda i: (i, 0))
    wo_half = lambda j: pl.BlockSpec((MLA_WIDTH, D_MODEL), lambda i: (j, 0), pipeline_mode=pl.Buffered(1))
    return pl.pallas_call(
        _mixout_body,
        grid=(N_TOK // tm,),
        in_specs=[row(D_MODEL),
                  pl.BlockSpec((1, N_MOD, D_MODEL), lambda i: (i // tpb, 0, 0)),
                  row(MLA_WIDTH), row(SGU_WIDTH), row(SGU_WIDTH),
                  _const_spec(wsp.shape), _const_spec(bsp_t.shape), wo_half(0), wo_half(1),
                  _const_spec(g1.shape), _const_spec(b1.shape), _const_spec(wr.shape), _const_spec(br.shape)],
        out_specs=[row(D_MODEL), pl.BlockSpec((tm * SUBLANES, LANES), lambda i: (i, 0)), row(LANES)],
        out_shape=[jax.ShapeDtypeStruct((N_TOK, D_MODEL), F32),
                   jax.ShapeDtypeStruct((N_TOK * SUBLANES, LANES), U32),
                   jax.ShapeDtypeStruct((N_TOK, LANES), F32)],
        scratch_shapes=[pltpu.VMEM((tm, SGU_WIDTH), BF16)],
        compiler_params=_cparams(("parallel",)),
        name="mix_out",
    )(x2, mod3, attn, u, vs, wsp, bsp_t, wo, wo, g1, b1, wr, br)


def _route_math(lg):
    lane = lax.broadcasted_iota(jnp.int32, lg.shape, 1)
    big = jnp.int32(LANES)

    def top1(vals):
        m = jnp.max(vals, axis=-1, keepdims=True)
        idx = jnp.min(jnp.where(vals == m, lane, big), axis=-1, keepdims=True)
        return m, idx

    is_group = lane < N_GROUPS
    glog = jnp.where(is_group, lg, -jnp.inf)
    gmax, gidx = top1(glog)
    pg_top = 1.0 / jnp.sum(jnp.exp(glog - gmax), axis=-1, keepdims=True)
    eid = lane - N_GROUPS
    sel = (eid >= gidx * EXPERTS_PER_GROUP) & (eid < (gidx + 1) * EXPERTS_PER_GROUP)
    elog = jnp.where(sel, lg, -jnp.inf)
    m1, i1 = top1(elog)
    m2, i2 = top1(jnp.where(lane == i1, -jnp.inf, elog))
    e2 = jnp.exp(m2 - m1)
    w1 = pg_top / (1.0 + e2)
    w2 = pg_top * e2 / (1.0 + e2)
    return jnp.where(lane == 0, (i1 - N_GROUPS).astype(F32),
                     jnp.where(lane == 1, (i2 - N_GROUPS).astype(F32),
                               jnp.where(lane == 2, w1, jnp.where(lane == 3, w2, 0.0))))


def _rank_math(rt, counts):
    t = rt.shape[0]
    lane = lax.broadcasted_iota(jnp.int32, (t, LANES), 1).astype(F32)
    oh0 = lane == rt[:, 0:1]
    oh1 = lane == rt[:, 1:2]
    s = jnp.where(oh0 | oh1, 1.0, 0.0)
    r = lax.broadcasted_iota(jnp.int32, (t, t), 0)
    c = lax.broadcasted_iota(jnp.int32, (t, t), 1)
    before = jnp.where(c < r, 1.0, 0.0).astype(BF16)
    csum = jnp.dot(before, s.astype(BF16), preferred_element_type=F32) + counts
    rank0 = jnp.sum(jnp.where(oh0, csum, 0.0), axis=-1, keepdims=True)
    rank1 = jnp.sum(jnp.where(oh1, csum, 0.0), axis=-1, keepdims=True)
    return rank0, rank1, counts + jnp.sum(s, axis=0, keepdims=True)


def _route_body(lg_ref, rt_ref, cnt_ref, cnt_scr):
    @pl.when(pl.program_id(0) == 0)
    def _():
        cnt_scr[...] = jnp.zeros_like(cnt_scr)

    rt = _route_math(lg_ref[...])
    rank0, rank1, counts = _rank_math(rt, cnt_scr[...])
    cnt_scr[...] = counts
    lane = lax.broadcasted_iota(jnp.int32, rt.shape, 1)
    rt_ref[...] = jnp.where(lane == RT_RANK, rank0, jnp.where(lane == RT_RANK + 1, rank1, rt))
    cnt_ref[...] = jnp.broadcast_to(counts, cnt_ref.shape)


def _route(logits):
    tm = ROUTE_TM
    return pl.pallas_call(
        _route_body,
        grid=(N_TOK // tm,),
        in_specs=[pl.BlockSpec((tm, LANES), lambda i: (i, 0))],
        out_specs=[pl.BlockSpec((tm, LANES), lambda i: (i, 0)),
                   pl.BlockSpec((SUBLANES, LANES), lambda i: (0, 0))],
        out_shape=[jax.ShapeDtypeStruct((N_TOK, LANES), F32),
                   jax.ShapeDtypeStruct((SUBLANES, LANES), F32)],
        scratch_shapes=[pltpu.VMEM((1, LANES), F32)],
        compiler_params=_cparams(("arbitrary",)),
        name="route",
    )(logits)


def _plan_body(rt_ref, cnt_ref, pos_ref, tt_ref):
    t = PLAN_TM
    lane = lax.broadcasted_iota(jnp.int32, (t, LANES), 1)
    rt = rt_ref[...]
    oh0 = lane.astype(F32) == rt[:, 0:1]
    oh1 = lane.astype(F32) == rt[:, 1:2]
    if True:
        counts = cnt_ref[0:1, :]
        tiles = jnp.floor((counts + (MOE_TM - 1)) * (1.0 / MOE_TM))
        r = lax.broadcasted_iota(jnp.int32, (LANES, LANES), 0)
        c = lax.broadcasted_iota(jnp.int32, (LANES, LANES), 1)
        upto = jnp.where(r <= c, 1.0, 0.0).astype(BF16)
        tiles8 = jnp.broadcast_to(tiles, (8, LANES)).astype(BF16)
        tile_end = jnp.dot(tiles8, upto, preferred_element_type=F32)[0:1]
        offs = (tile_end - tiles) * MOE_TM
        p0 = jnp.sum(jnp.where(oh0, offs, 0.0), axis=-1, keepdims=True) + rt[:, RT_RANK:RT_RANK + 1]
        p1 = jnp.sum(jnp.where(oh1, offs, 0.0), axis=-1, keepdims=True) + rt[:, RT_RANK + 1:RT_RANK + 2]
        pos_ref[...] = (jnp.where(lane == 0, p0, jnp.where(lane == 1, p1, 0.0)) * SUBLANES).astype(jnp.int32)

        lane_e = lax.broadcasted_iota(jnp.int32, (LANES, LANES), 1)
        tile_id = lax.broadcasted_iota(jnp.int32, (LANES, LANES), 0).astype(F32)
        is_e = lane_e < N_EXPERTS
        total = jnp.max(tile_end, axis=-1, keepdims=True)
        t_exp = jnp.sum(jnp.where(is_e & (tile_end <= tile_id), 1.0, 0.0), axis=-1, keepdims=True)
        t_valid = jnp.where(tile_id[:, 0:1] < total, 1.0, 0.0)
        last_exp = jnp.sum(jnp.where(is_e & (tile_end <= total - 1.0), 1.0, 0.0), axis=-1, keepdims=True)
        t_exp = jnp.where(t_valid > 0, t_exp, last_exp)
        t_first = jnp.sum(jnp.where(is_e & (tiles > 0) & ((tile_end - tiles) == tile_id), 1.0, 0.0),
                          axis=-1, keepdims=True)
        t_last = jnp.sum(jnp.where(is_e & (tiles > 0) & ((tile_end - 1.0) == tile_id), 1.0, 0.0),
                         axis=-1, keepdims=True)
        t_clear = jnp.maximum(t_last, 1.0 - t_valid)
        none = jnp.float32(LANES)
        owns = is_e & (tiles > 0)
        lane_f = lane_e.astype(F32)
        t_next = jnp.min(jnp.where(owns & (lane_f > t_exp), lane_f, none), axis=-1, keepdims=True)
        t_next2 = jnp.min(jnp.where(owns & (lane_f > t_next), lane_f, none), axis=-1, keepdims=True)
        t_ord = jnp.sum(jnp.where(owns & (lane_f < t_exp), 1.0, 0.0), axis=-1, keepdims=True)
        cols = {T_EXP: t_exp, T_VALID: t_valid, T_FIRST: t_first, T_CLEAR: t_clear,
                T_NEXT: t_next, T_NEXT2: t_next2, T_ORD: t_ord, T_LAST: total - 1.0}
        table = jnp.zeros((LANES, LANES), F32)
        for k, col in cols.items():
            table = jnp.where(lane_e == k, col, table)
        tt_ref[...] = table.astype(jnp.int32)


def _plan(route, counts):
    t = PLAN_TM
    return pl.pallas_call(
        _plan_body,
        grid=(N_TOK // t,),
        in_specs=[pl.BlockSpec((t, LANES), lambda i: (i, 0)),
                  pl.BlockSpec((SUBLANES, LANES), lambda i: (0, 0))],
        out_specs=[pl.BlockSpec((t, LANES), lambda i: (i, 0)),
                   pl.BlockSpec((LANES, LANES), lambda i: (0, 0))],
        out_shape=[jax.ShapeDtypeStruct((N_TOK, LANES), jnp.int32),
                   jax.ShapeDtypeStruct((LANES, LANES), jnp.int32)],
        compiler_params=_cparams(("arbitrary",)),
        name="plan",
    )(route, counts)


def _rows_wait(ref, n_rows, sem):
    pltpu.make_async_copy(ref.at[pl.ds(0, n_rows)], ref.at[pl.ds(0, n_rows)], sem).wait()


def _dispatch_body(pos_ref, tt_ref, h_ref, xs_hbm, zbuf, sem_z, sem_s):
    i = pl.program_id(0)
    tile_rows = MOE_TM * SUBLANES

    @pl.when(i == 0)
    def _():
        zbuf[...] = _pack_halves(jnp.zeros((tile_rows, 2 * LANES), F32))

        def zero_copy(tile):
            start = pl.multiple_of(tile * tile_rows, tile_rows)
            return pltpu.make_async_copy(zbuf, xs_hbm.at[pl.ds(start, tile_rows)], sem_z)

        def clear_start(tile, carry):
            @pl.when(tt_ref[tile, T_CLEAR] > 0)
            def _():
                zero_copy(tile).start()
            return carry

        def clear_wait(tile, carry):
            @pl.when(tt_ref[tile, T_CLEAR] > 0)
            def _():
                zero_copy(tile).wait()
            return carry

        lax.fori_loop(0, MOE_TILES, clear_start, 0)
        lax.fori_loop(0, MOE_TILES, clear_wait, 0)

    def tok(j, carry):
        src = h_ref.at[pl.ds(pl.multiple_of(j * SUBLANES, SUBLANES), SUBLANES)]
        pair = TOP_K * (i * DISPATCH_TM + j)
        for k in range(TOP_K):
            dst_row = pl.multiple_of(pos_ref[pair + k], SUBLANES)
            pltpu.make_async_copy(src, xs_hbm.at[pl.ds(dst_row, SUBLANES)], sem_s).start(priority=k)
        return carry

    lax.fori_loop(0, DISPATCH_TM, tok, 0, unroll=8)
    _rows_wait(xs_hbm, TOP_K * DISPATCH_TM * SUBLANES, sem_s)


def _dispatch(pos_rows, tile_clear, h2p):
    grid_spec = pltpu.PrefetchScalarGridSpec(
        num_scalar_prefetch=2,
        grid=(N_TOK // DISPATCH_TM,),
        in_specs=[pl.BlockSpec((DISPATCH_TM * SUBLANES, LANES), lambda i, *_: (i, 0))],
        out_specs=pl.BlockSpec(memory_space=pl.ANY),
        scratch_shapes=[pltpu.VMEM((MOE_TM * SUBLANES, LANES), U32),
                        pltpu.SemaphoreType.DMA(()), pltpu.SemaphoreType.DMA(())],
    )
    return pl.pallas_call(
        _dispatch_body,
        grid_spec=grid_spec,
        out_shape=jax.ShapeDtypeStruct((MOE_ROWS * SUBLANES, LANES), U32),
        compiler_params=_cparams(("arbitrary",)),
        name="dispatch",
    )(pos_rows, tile_clear, h2p)


def _moe_body(tt_ref, x_ref, wg_hbm, wu_hbm, wd_hbm, y_ref,
              wg_s, wu_s, wd_s, stg_g, stg_u, stg_d, sem):
    i = pl.program_id(0)

    def fetch(e, slot):
        return (pltpu.make_async_copy(wg_hbm.at[e], stg_g.at[slot], sem.at[slot, 0]),
                pltpu.make_async_copy(wu_hbm.at[e], stg_u.at[slot], sem.at[slot, 1]),
                pltpu.make_async_copy(wd_hbm.at[e], stg_d.at[slot], sem.at[slot, 2]))

    @pl.when(i == 0)
    def _():
        for cp in fetch(tt_ref[0, T_EXP], 0):
            cp.start()

        @pl.when(tt_ref[0, T_NEXT] < N_EXPERTS)
        def _():
            for cp in fetch(tt_ref[0, T_NEXT], 1):
                cp.start()

    @pl.when(tt_ref[i, T_FIRST] > 0)
    def _():
        slot = tt_ref[i, T_ORD] % 2
        for cp in fetch(tt_ref[i, T_EXP], slot):
            cp.wait()
        wg_s[...] = stg_g[slot].astype(BF16)
        wu_s[...] = stg_u[slot].astype(BF16)
        wd_s[...] = stg_d[slot].astype(BF16)

        @pl.when(tt_ref[i, T_NEXT2] < N_EXPERTS)
        def _():
            for cp in fetch(tt_ref[i, T_NEXT2], slot):
                cp.start()

    @pl.when(tt_ref[i, T_VALID] > 0)
    def _():
        lo, hi = _unpack_halves(_load_token_tiles(x_ref, 0, MOE_TM))
        xa = lo.astype(BF16)
        xb = hi.astype(BF16)
        g = (jnp.dot(xa, wg_s[:HALF_D, :], preferred_element_type=F32)
             + jnp.dot(xb, wg_s[HALF_D:, :], preferred_element_type=F32))
        u = (jnp.dot(xa, wu_s[:HALF_D, :], preferred_element_type=F32)
             + jnp.dot(xb, wu_s[HALF_D:, :], preferred_element_type=F32))
        hid = (g * jax.nn.sigmoid(g) * u).astype(BF16)
        _store_token_tiles(y_ref, _pack_halves(jnp.dot(hid, wd_s[...], preferred_element_type=F32)))

    @pl.when(tt_ref[i, T_VALID] == 0)
    def _():
        y_ref[...] = _pack_halves(jnp.zeros((MOE_TM * SUBLANES, 2 * LANES), F32))


def _moe(tile_tab, xs, wg, wu, wd):
    tm = MOE_TM
    grid_spec = pltpu.PrefetchScalarGridSpec(
        num_scalar_prefetch=1,
        grid=(MOE_TILES,),
        in_specs=[pl.BlockSpec((tm * SUBLANES, LANES), lambda i, tt: (jnp.minimum(i, tt[0, T_LAST]), 0)),
                  pl.BlockSpec(memory_space=pl.ANY), pl.BlockSpec(memory_space=pl.ANY),
                  pl.BlockSpec(memory_space=pl.ANY)],
        out_specs=pl.BlockSpec((tm * SUBLANES, LANES), lambda i, *_: (i, 0)),
        scratch_shapes=[pltpu.VMEM((D_MODEL, EXPERT_FF), BF16), pltpu.VMEM((D_MODEL, EXPERT_FF), BF16),
                        pltpu.VMEM((EXPERT_FF, D_MODEL), BF16),
                        pltpu.VMEM((2, D_MODEL, EXPERT_FF), F32), pltpu.VMEM((2, D_MODEL, EXPERT_FF), F32),
                        pltpu.VMEM((2, EXPERT_FF, D_MODEL), F32),
                        pltpu.SemaphoreType.DMA((2, 3))],
    )
    return pl.pallas_call(
        _moe_body,
        grid_spec=grid_spec,
        out_shape=jax.ShapeDtypeStruct((MOE_ROWS * SUBLANES, LANES), U32),
        compiler_params=_cparams(("arbitrary",)),
        name="moe",
    )(tile_tab, xs, wg, wu, wd)


def _final_body(pos_ref, x1_ref, mod_ref, rt_ref, g2_ref, b2_ref, ys_hbm, o_ref, buf, sem):
    i = pl.program_id(0)
    n = pl.num_programs(0)
    tm = FINAL_TM
    slot = i % FINAL_SLOTS
    nxt_slot = (i + 2) % FINAL_SLOTS
    nxt_tile = jnp.minimum(i + 2, n - 1)

    def issue(tile, dst_slot, j):
        pair = TOP_K * (tile * tm + j)
        for k in range(TOP_K):
            src_row = pl.multiple_of(pos_ref[pair + k], SUBLANES)
            dst_row = pl.multiple_of((k * tm + j) * SUBLANES, SUBLANES)
            pltpu.make_async_copy(ys_hbm.at[pl.ds(src_row, SUBLANES)],
                                  buf.at[dst_slot, pl.ds(dst_row, SUBLANES)], sem.at[dst_slot]).start(priority=k)

    def wait(s):
        pltpu.make_async_copy(ys_hbm.at[pl.ds(0, TOP_K * tm * SUBLANES)], buf.at[s], sem.at[s]).wait()

    @pl.when(i == 0)
    def _():
        def tok(j, carry):
            issue(0, 0, j)
            issue(jnp.minimum(1, n - 1), 1, j)
            return carry
        lax.fori_loop(0, tm, tok, 0, unroll=8)

    wait(slot)
    gate2 = mod_ref[0, 5:6, :]
    cur = buf.at[slot]

    def chunk(c, carry):
        r0 = pl.multiple_of(c * FINAL_CHUNK, FINAL_CHUNK)
        rows = pl.ds(r0, FINAL_CHUNK)
        a_lo, a_hi = _unpack_halves(_load_token_tiles(cur, r0, FINAL_CHUNK))
        b_lo, b_hi = _unpack_halves(_load_token_tiles(cur, tm + r0, FINAL_CHUNK))
        x1 = x1_ref[rows, :]
        rt = rt_ref[rows, :]
        for r in range(FINAL_CHUNK):
            issue(nxt_tile, nxt_slot, r0 + r)
        w0 = rt[:, 2:3]
        w1 = rt[:, 3:4]
        y = jnp.concatenate([w0 * a_lo + w1 * b_lo, w0 * a_hi + w1 * b_hi], axis=1)
        o_ref[rows, :] = _ln_rows(DEEPNORM_ALPHA * x1 + gate2 * y) * g2_ref[...] + b2_ref[...]
        return carry

    lax.fori_loop(0, tm // FINAL_CHUNK, chunk, 0)

    @pl.when(i == n - 1)
    def _():
        wait((i + 1) % FINAL_SLOTS)
        wait(nxt_slot)


def _final(pos_rows, x1, mod3, route, g2, b2, ys):
    tm = FINAL_TM
    tpb = SEQ // tm
    row = lambda w: pl.BlockSpec((tm, w), lambda i, *_: (i, 0))
    grid_spec = pltpu.PrefetchScalarGridSpec(
        num_scalar_prefetch=1,
        grid=(N_TOK // tm,),
        in_specs=[row(D_MODEL),
                  pl.BlockSpec((1, N_MOD, D_MODEL), lambda i, *_: (i // tpb, 0, 0)),
                  row(LANES),
                  pl.BlockSpec(g2.shape, lambda i, *_: (0, 0)),
                  pl.BlockSpec(b2.shape, lambda i, *_: (0, 0)),
                  pl.BlockSpec(memory_space=pl.ANY)],
        out_specs=row(D_MODEL),
        scratch_shapes=[pltpu.VMEM((FINAL_SLOTS, TOP_K * tm * SUBLANES, LANES), U32),
                        pltpu.SemaphoreType.DMA((FINAL_SLOTS,))],
    )
    return pl.pallas_call(
        _final_body,
        grid_spec=grid_spec,
        out_shape=jax.ShapeDtypeStruct((N_TOK, D_MODEL), F32),
        compiler_params=_cparams(("arbitrary",)),
        name="final",
    )(pos_rows, x1, mod3, route, g2, b2, ys)


def kernel(x, c, positions, w_ada, b_ada, w_in, q_norm_g, w_uq, kv_norm_g, w_ukv, sgu_norm_g, sgu_norm_b,
           w_spatial, b_spatial, w_o, ln1_g, ln1_b, w_router_group, b_router_group, w_router_expert,
           b_router_expert, w_gate, w_up, w_down, ln2_g, ln2_b):
    l = 0
    x2 = x.reshape(N_TOK, D_MODEL)
    mod3 = _ada(c, w_ada[l], b_ada[l][None, :]).reshape(BATCH, N_MOD, D_MODEL)

    wt, wuq, wukv, wo = _prep(w_in[l].T, w_uq[l], w_ukv[l], w_o[l])
    n_r = N_GROUPS + N_EXPERTS
    wr = jnp.pad(jnp.concatenate([w_router_group[l], w_router_expert[l]], axis=1),
                 ((0, 0), (0, LANES - n_r))).astype(BF16)
    br = jnp.pad(jnp.concatenate([b_router_group[l], b_router_expert[l]]), (0, LANES - n_r))[None, :]
    inv_freq = 1.0 / (ROPE_THETA ** (jnp.arange(0, QK_ROPE_DIM, 2, dtype=F32) / QK_ROPE_DIM))
    invf = jnp.tile(inv_freq, 2 * LANES // QK_ROPE_DIM)[None, :]

    cqn, ckvn, kpe, u, vs = _inproj(x2, mod3, wt, q_norm_g[l][None, :], kv_norm_g[l][None, :],
                                    sgu_norm_g[l][None, :], sgu_norm_b[l][None, :])
    q, k, v = _qkv(cqn, ckvn, kpe, positions.reshape(N_TOK, 1), invf, wuq, wukv)
    attn = _attn(q, k, v).reshape(N_TOK, MLA_WIDTH)
    x1, h2, logits = _mixout(x2, mod3, attn, u, vs, w_spatial[l], b_spatial[l].T, wo,
                             ln1_g[l][None, :], ln1_b[l][None, :], wr, br)
    route, counts = _route(logits)
    pos_tab, tile_tab = _plan(route, counts)
    pos_rows = pos_tab[:, 0:TOP_K].reshape(-1)
    xs = _dispatch(pos_rows, tile_tab, h2)
    ys = _moe(tile_tab, xs, w_gate[l], w_up[l], w_down[l])
    out = _final(pos_rows, x1, mod3, route, ln2_g[l][None, :], ln2_b[l][None, :], ys)
    return out.reshape(BATCH, SEQ, D_MODEL)
```

```python
import functools

import jax
import jax.numpy as jnp
import numpy as np
from jax import lax
from jax.experimental import pallas as pl
from jax.experimental.pallas import tpu as pltpu

D_MODEL = 2048
BATCH = 4
SEQ = 2048
N_TOK = BATCH * SEQ

MLA_HEADS = 8
QK_NOPE_DIM = 128
QK_ROPE_DIM = 64
QK_DIM = QK_NOPE_DIM + QK_ROPE_DIM
V_HEAD_DIM = 128
Q_LORA_RANK = 768
KV_LORA_RANK = 512
ROPE_THETA = 10000.0
MLA_WIDTH = MLA_HEADS * V_HEAD_DIM

SGU_GROUPS = 8
SGU_GROUP_DIM = 128
SGU_CHUNK = 128
SGU_WIDTH = SGU_GROUPS * SGU_GROUP_DIM

N_GROUPS = 4
EXPERTS_PER_GROUP = 8
N_EXPERTS = N_GROUPS * EXPERTS_PER_GROUP
TOP_K = 2
EXPERT_FF = 512

DEEPNORM_ALPHA = 2.0 ** 0.25
EPS = 1e-6
N_MOD = 6
NEG_BIG = -1e30

LANES = 128
SUBLANES = 8
VMEM_LIMIT = 56 * 1024 * 1024

ADA_TN = 1024
TOK_TM = 512
MIX_TM = 256
PREP_STEPS = 4
ATT_TQ = 1024
ATT_TK = 1024
ATT_HEADS = 4
MOE_TM = 256
MOE_TILES = (N_TOK * TOP_K + N_EXPERTS * (MOE_TM - 1)) // MOE_TM + 1
MOE_ROWS = MOE_TILES * MOE_TM
PLAN_TM = 2048
ROUTE_TM = 1024
RT_RANK = 4
DISPATCH_TM = 1024
FINAL_TM = 256
FINAL_CHUNK = 128
FINAL_SLOTS = 3
assert MOE_TILES <= LANES
T_EXP, T_VALID, T_FIRST, T_CLEAR, T_NEXT, T_NEXT2, T_ORD, T_LAST = range(8)

F32 = jnp.float32
BF16 = jnp.bfloat16
U32 = jnp.uint32
HALF_D = D_MODEL // 2


def _cparams(sem):
    return pltpu.CompilerParams(dimension_semantics=sem, vmem_limit_bytes=VMEM_LIMIT)


def _const_spec(shape):
    nd = len(shape)
    return pl.BlockSpec(shape, lambda *_: (0,) * nd, pipeline_mode=pl.Buffered(1))


def _ln_rows(x):
    mu = jnp.mean(x, axis=-1, keepdims=True)
    xc = x - mu
    var = jnp.mean(xc * xc, axis=-1, keepdims=True)
    return xc * lax.rsqrt(var + EPS)


def _rms_rows(x):
    return x * lax.rsqrt(jnp.mean(x * x, axis=-1, keepdims=True) + EPS)


def _pack_halves(x):
    half = x.shape[-1] // 2
    return pltpu.pack_elementwise([x[:, :half], x[:, half:]], packed_dtype=BF16)


def _unpack_halves(w):
    lo = pltpu.unpack_elementwise(w, index=0, packed_dtype=BF16, unpacked_dtype=F32)
    hi = pltpu.unpack_elementwise(w, index=1, packed_dtype=BF16, unpacked_dtype=F32)
    return lo, hi


def _store_token_tiles(ref, w):
    rows = w.shape[0]
    for s in range(SUBLANES):
        ref[pl.ds(s, rows, stride=SUBLANES), :] = w[:, s * LANES:(s + 1) * LANES]


def _load_token_tiles(ref, start_row, rows):
    return jnp.concatenate([ref[pl.ds(start_row * SUBLANES + s, rows, stride=SUBLANES), :]
                            for s in range(SUBLANES)], axis=1)


def _gelu_tanh(x):
    c = np.sqrt(2.0 / np.pi).astype(np.float32)
    return 0.5 * x * (1.0 + jnp.tanh(c * (x + 0.044715 * (x * x * x))))


def _ada_body(c_ref, w_ref, b_ref, o_ref):
    o_ref[...] = jnp.dot(c_ref[...].astype(BF16), w_ref[...].astype(BF16),
                         preferred_element_type=F32) + b_ref[...]


def _ada(c, w, b):
    n = w.shape[1]
    return pl.pallas_call(
        _ada_body,
        grid=(n // ADA_TN,),
        in_specs=[pl.BlockSpec((BATCH, D_MODEL), lambda j: (0, 0)),
                  pl.BlockSpec((D_MODEL, ADA_TN), lambda j: (0, j)),
                  pl.BlockSpec((1, ADA_TN), lambda j: (0, j))],
        out_specs=pl.BlockSpec((BATCH, ADA_TN), lambda j: (0, j)),
        out_shape=jax.ShapeDtypeStruct((BATCH, n), F32),
        compiler_params=_cparams(("parallel",)),
        name="ada",
    )(c, w, b)


def _prep_body(win_ref, wuq_ref, wukv_ref, wo_ref, win_o, wuq_o, wukv_o, wo_o):
    win_o[...] = win_ref[...].astype(BF16)
    u = wuq_ref[...]
    nope = [u[:, h * QK_DIM:h * QK_DIM + QK_NOPE_DIM] for h in range(MLA_HEADS)]
    rope = [u[:, h * QK_DIM + QK_NOPE_DIM:(h + 1) * QK_DIM] for h in range(MLA_HEADS)]
    wuq_o[...] = jnp.concatenate(nope + rope, axis=1).astype(BF16)
    kv = wukv_ref[...]
    hw = QK_NOPE_DIM + V_HEAD_DIM
    kn = [kv[:, h * hw:h * hw + QK_NOPE_DIM] for h in range(MLA_HEADS)]
    vv = [kv[:, h * hw + QK_NOPE_DIM:(h + 1) * hw] for h in range(MLA_HEADS)]
    wukv_o[...] = jnp.concatenate(kn + vv, axis=1).astype(BF16)
    wo_o[...] = wo_ref[...].astype(BF16)


def _prep(w_in_t, w_uq, w_ukv, w_o):
    steps = PREP_STEPS
    blk = lambda a: pl.BlockSpec((a.shape[0] // steps, a.shape[1]), lambda i: (i, 0))
    ins = (w_in_t, w_uq, w_ukv, w_o)
    return pl.pallas_call(
        _prep_body,
        grid=(steps,),
        in_specs=[blk(a) for a in ins],
        out_specs=[blk(a) for a in ins],
        out_shape=[jax.ShapeDtypeStruct(a.shape, BF16) for a in ins],
        compiler_params=_cparams(("parallel",)),
        name="prep",
    )(*ins)


def _inproj_body(x_ref, mod_ref, wt_ref, gq_ref, gkv_ref, sg_ref, sb_ref,
                 cq_ref, ckv_ref, kpe_ref, u_ref, vs_ref):
    o1, o2, o3 = Q_LORA_RANK, Q_LORA_RANK + KV_LORA_RANK, Q_LORA_RANK + KV_LORA_RANK + QK_ROPE_DIM

    def proj(lo, hi):
        return lax.dot_general(h, wt_ref[lo:hi, :], (((1,), (1,)), ((), ())), preferred_element_type=F32)

    sh = mod_ref[0, 0:1, :]
    sc = mod_ref[0, 1:2, :]
    h = (_ln_rows(x_ref[...]) * (1.0 + sc) + sh).astype(BF16)
    cq_ref[...] = (_rms_rows(proj(0, o1)) * gq_ref[...]).astype(BF16)
    ckv_ref[...] = (_rms_rows(proj(o1, o2)) * gkv_ref[...]).astype(BF16)
    kpe_ref[...] = proj(o2, o2 + LANES)
    gz = _gelu_tanh(proj(o3, o3 + 2 * SGU_WIDTH))
    u_ref[...] = gz[:, :SGU_WIDTH]
    vs_ref[...] = (_ln_rows(gz[:, SGU_WIDTH:]) * sg_ref[...] + sb_ref[...]).astype(BF16)


def _inproj(x2, mod3, wt, gq, gkv, sg, sb):
    tm = TOK_TM
    tiles_per_batch = SEQ // tm
    row = lambda w: pl.BlockSpec((tm, w), lambda i: (i, 0))
    return pl.pallas_call(
        _inproj_body,
        grid=(N_TOK // tm,),
        in_specs=[row(D_MODEL),
                  pl.BlockSpec((1, N_MOD, D_MODEL), lambda i: (i // tiles_per_batch, 0, 0)),
                  _const_spec(wt.shape),
                  _const_spec(gq.shape), _const_spec(gkv.shape), _const_spec(sg.shape), _const_spec(sb.shape)],
        out_specs=[row(Q_LORA_RANK), row(KV_LORA_RANK), row(LANES), row(SGU_WIDTH), row(SGU_WIDTH)],
        out_shape=[jax.ShapeDtypeStruct((N_TOK, Q_LORA_RANK), BF16),
                   jax.ShapeDtypeStruct((N_TOK, KV_LORA_RANK), BF16),
                   jax.ShapeDtypeStruct((N_TOK, LANES), F32),
                   jax.ShapeDtypeStruct((N_TOK, SGU_WIDTH), F32),
                   jax.ShapeDtypeStruct((N_TOK, SGU_WIDTH), BF16)],
        compiler_params=_cparams(("parallel",)),
        name="inproj",
    )(x2, mod3, wt, gq, gkv, sg, sb)


def _rope(x, cos, sin):
    w = x.shape[-1]
    lane = lax.broadcasted_iota(jnp.int32, x.shape, 1)
    first_half = (lane % QK_ROPE_DIM) < (QK_ROPE_DIM // 2)
    rot = jnp.where(first_half,
                    -pltpu.roll(x, w - QK_ROPE_DIM // 2, 1),
                    pltpu.roll(x, QK_ROPE_DIM // 2, 1))
    return x * cos + rot * sin


def _qkv_body(cq_ref, ckv_ref, kpe_ref, pos_ref, invf_ref, wuq_ref, wukv_ref, q_ref, k_ref, v_ref):
    ang = pos_ref[...].astype(F32) * invf_ref[...]
    cos1 = jnp.cos(ang)
    sin1 = jnp.sin(ang)
    reps = MLA_HEADS * QK_ROPE_DIM // LANES
    cos = jnp.concatenate([cos1] * reps, axis=1)
    sin = jnp.concatenate([sin1] * reps, axis=1)
    scale = np.float32(QK_DIM ** -0.5)
    q = jnp.dot(cq_ref[...], wuq_ref[...], preferred_element_type=F32) * scale
    q_pe = _rope(q[:, MLA_HEADS * QK_NOPE_DIM:], cos, sin)
    kv = jnp.dot(ckv_ref[...], wukv_ref[...], preferred_element_type=F32)
    k_pe = _rope(kpe_ref[...], cos1, sin1)[:, :QK_ROPE_DIM].astype(BF16)
    for h in range(MLA_HEADS):
        q_ref[0, h, :, 0:QK_NOPE_DIM] = q[:, h * QK_NOPE_DIM:(h + 1) * QK_NOPE_DIM].astype(BF16)
        q_ref[0, h, :, QK_NOPE_DIM:QK_DIM] = q_pe[:, h * QK_ROPE_DIM:(h + 1) * QK_ROPE_DIM].astype(BF16)
        k_ref[0, h, :, 0:QK_NOPE_DIM] = kv[:, h * QK_NOPE_DIM:(h + 1) * QK_NOPE_DIM].astype(BF16)
        k_ref[0, h, :, QK_NOPE_DIM:QK_DIM] = k_pe
        v_ref[0, h, :, :] = kv[:, MLA_WIDTH + h * V_HEAD_DIM:MLA_WIDTH + (h + 1) * V_HEAD_DIM].astype(BF16)


def _qkv(cqn, ckvn, kpe, pos2, invf, wuq, wukv):
    tm = TOK_TM
    tpb = SEQ // tm
    row = lambda w: pl.BlockSpec((tm, w), lambda i: (i, 0))
    head_out = lambda w: pl.BlockSpec((1, MLA_HEADS, tm, w), lambda i: (i // tpb, 0, i % tpb, 0))
    return pl.pallas_call(
        _qkv_body,
        grid=(N_TOK // tm,),
        in_specs=[row(Q_LORA_RANK), row(KV_LORA_RANK), row(LANES), row(1),
                  _const_spec(invf.shape), _const_spec(wuq.shape), _const_spec(wukv.shape)],
        out_specs=[head_out(QK_DIM), head_out(QK_DIM), head_out(V_HEAD_DIM)],
        out_shape=[jax.ShapeDtypeStruct((BATCH, MLA_HEADS, SEQ, QK_DIM), BF16),
                   jax.ShapeDtypeStruct((BATCH, MLA_HEADS, SEQ, QK_DIM), BF16),
                   jax.ShapeDtypeStruct((BATCH, MLA_HEADS, SEQ, V_HEAD_DIM), BF16)],
        compiler_params=_cparams(("parallel",)),
        name="qkv",
    )(cqn, ckvn, kpe, pos2, invf, wuq, wukv)


def _attn_body(q_ref, k_ref, v_ref, o_ref):
    i = pl.program_id(2)

    def step(h, j, carry, masked):
        m, l, acc = carry
        start = pl.multiple_of(j * ATT_TK, ATT_TK)
        k = k_ref[0, h, pl.ds(start, ATT_TK), :]
        v = v_ref[0, h, pl.ds(start, ATT_TK), :]
        s = lax.dot_general(q_ref[0, h], k, (((1,), (1,)), ((), ())), preferred_element_type=F32)
        if masked:
            r = lax.broadcasted_iota(jnp.int32, s.shape, 0)
            c = lax.broadcasted_iota(jnp.int32, s.shape, 1)
            s = jnp.where(c <= r, s, NEG_BIG)
        m_new = jnp.maximum(m, jnp.max(s, axis=-1, keepdims=True))
        p = jnp.exp(s - m_new)
        a = jnp.exp(m - m_new)
        l = a * l + jnp.sum(p, axis=-1, keepdims=True)
        acc = a * acc + jnp.dot(p.astype(BF16), v, preferred_element_type=F32)
        return m_new, l, acc

    def steps(j, carries, masked):
        return tuple(step(h, j, carries[h], masked) for h in range(ATT_HEADS))

    init = tuple((jnp.full((ATT_TQ, 1), NEG_BIG, F32), jnp.zeros((ATT_TQ, 1), F32),
                  jnp.zeros((ATT_TQ, V_HEAD_DIM), F32)) for _ in range(ATT_HEADS))
    carries = lax.fori_loop(0, i, lambda j, c: steps(j, c, False), init)
    carries = steps(i, carries, True)
    for h, (m, l, acc) in enumerate(carries):
        o_ref[0, :, h * V_HEAD_DIM:(h + 1) * V_HEAD_DIM] = (acc / l).astype(BF16)


def _attn(q, k, v):
    assert ATT_TQ == ATT_TK
    hb = ATT_HEADS
    return pl.pallas_call(
        _attn_body,
        grid=(BATCH, MLA_HEADS // hb, SEQ // ATT_TQ),
        in_specs=[pl.BlockSpec((1, hb, ATT_TQ, QK_DIM), lambda b, h, i: (b, h, i, 0)),
                  pl.BlockSpec((1, hb, SEQ, QK_DIM), lambda b, h, i: (b, h, 0, 0)),
                  pl.BlockSpec((1, hb, SEQ, V_HEAD_DIM), lambda b, h, i: (b, h, 0, 0))],
        out_specs=pl.BlockSpec((1, ATT_TQ, hb * V_HEAD_DIM), lambda b, h, i: (b, i, h)),
        out_shape=jax.ShapeDtypeStruct((BATCH, SEQ, MLA_WIDTH), BF16),
        compiler_params=_cparams(("parallel", "parallel", "arbitrary")),
        name="attn",
    )(q, k, v)


def _mixout_body(x_ref, mod_ref, attn_ref, u_ref, vs_ref, wsp_ref, bsp_ref, woa_ref, wos_ref,
                 g1_ref, b1_ref, wr_ref, br_ref, x1_ref, h2_ref, lg_ref, sgu_scr):
    r = lax.broadcasted_iota(jnp.int32, (SGU_CHUNK, SGU_CHUNK), 0)
    c = lax.broadcasted_iota(jnp.int32, (SGU_CHUNK, SGU_CHUNK), 1)
    causal = c <= r
    for g in range(SGU_GROUPS):
        ws = jnp.where(causal, wsp_ref[g], 0.0).astype(BF16)
        bias = bsp_ref[:, g:g + 1]
        cols = slice(g * SGU_GROUP_DIM, (g + 1) * SGU_GROUP_DIM)
        for ch in range(MIX_TM // SGU_CHUNK):
            rows = slice(ch * SGU_CHUNK, (ch + 1) * SGU_CHUNK)
            mixed = jnp.dot(ws, vs_ref[rows, cols], preferred_element_type=F32) + bias
            sgu_scr[rows, cols] = (u_ref[rows, cols] * mixed).astype(BF16)
    y = (jnp.dot(attn_ref[...], woa_ref[...], preferred_element_type=F32)
         + jnp.dot(sgu_scr[...], wos_ref[...], preferred_element_type=F32))
    gate1 = mod_ref[0, 2:3, :]
    sh2 = mod_ref[0, 3:4, :]
    sc2 = mod_ref[0, 4:5, :]
    x1 = _ln_rows(DEEPNORM_ALPHA * x_ref[...] + gate1 * y) * g1_ref[...] + b1_ref[...]
    x1_ref[...] = x1
    h2 = _ln_rows(x1) * (1.0 + sc2) + sh2
    _store_token_tiles(h2_ref, _pack_halves(h2))
    lg_ref[...] = jnp.dot(h2.astype(BF16), wr_ref[...], preferred_element_type=F32) + br_ref[...]


def _mixout(x2, mod3, attn, u, vs, wsp, bsp_t, wo, g1, b1, wr, br):
    tm = MIX_TM
    tpb = SEQ // tm
    row = lambda w: pl.BlockSpec((tm, w), lambda i: (i, 0))
    wo_half = lambda j: pl.BlockSpec((MLA_WIDTH, D_MODEL), lambda i: (j, 0), pipeline_mode=pl.Buffered(1))
    return pl.pallas_call(
        _mixout_body,
        grid=(N_TOK // tm,),
        in_specs=[row(D_MODEL),
                  pl.BlockSpec((1, N_MOD, D_MODEL), lambda i: (i // tpb, 0, 0)),
                  row(MLA_WIDTH), row(SGU_WIDTH), row(SGU_WIDTH),
                  _const_spec(wsp.shape), _const_spec(bsp_t.shape), wo_half(0), wo_half(1),
                  _const_spec(g1.shape), _const_spec(b1.shape), _const_spec(wr.shape), _const_spec(br.shape)],
        out_specs=[row(D_MODEL), pl.BlockSpec((tm * SUBLANES, LANES), lambda i: (i, 0)), row(LANES)],
        out_shape=[jax.ShapeDtypeStruct((N_TOK, D_MODEL), F32),
                   jax.ShapeDtypeStruct((N_TOK * SUBLANES, LANES), U32),
                   jax.ShapeDtypeStruct((N_TOK, LANES), F32)],
        scratch_shapes=[pltpu.VMEM((tm, SGU_WIDTH), BF16)],
        compiler_params=_cparams(("parallel",)),
        name="mix_out",
    )(x2, mod3, attn, u, vs, wsp, bsp_t, wo, wo, g1, b1, wr, br)


def _route_math(lg):
    lane = lax.broadcasted_iota(jnp.int32, lg.shape, 1)
    big = jnp.int32(LANES)

    def top1(vals):
        m = jnp.max(vals, axis=-1, keepdims=True)
        idx = jnp.min(jnp.where(vals == m, lane, big), axis=-1, keepdims=True)
        return m, idx

    is_group = lane < N_GROUPS
    glog = jnp.where(is_group, lg, -jnp.inf)
    gmax, gidx = top1(glog)
    pg_top = 1.0 / jnp.sum(jnp.exp(glog - gmax), axis=-1, keepdims=True)
    eid = lane - N_GROUPS
    sel = (eid >= gidx * EXPERTS_PER_GROUP) & (eid < (gidx + 1) * EXPERTS_PER_GROUP)
    elog = jnp.where(sel, lg, -jnp.inf)
    m1, i1 = top1(elog)
    m2, i2 = top1(jnp.where(lane == i1, -jnp.inf, elog))
    e2 = jnp.exp(m2 - m1)
    w1 = pg_top / (1.0 + e2)
    w2 = pg_top * e2 / (1.0 + e2)
    return jnp.where(lane == 0, (i1 - N_GROUPS).astype(F32),
                     jnp.where(lane == 1, (i2 - N_GROUPS).astype(F32),
                               jnp.where(lane == 2, w1, jnp.where(lane == 3, w2, 0.0))))


def _rank_math(rt, counts):
    t = rt.shape[0]
    lane = lax.broadcasted_iota(jnp.int32, (t, LANES), 1).astype(F32)
    oh0 = lane == rt[:, 0:1]
    oh1 = lane == rt[:, 1:2]
    s = jnp.where(oh0 | oh1, 1.0, 0.0)
    r = lax.broadcasted_iota(jnp.int32, (t, t), 0)
    c = lax.broadcasted_iota(jnp.int32, (t, t), 1)
    before = jnp.where(c < r, 1.0, 0.0).astype(BF16)
    csum = jnp.dot(before, s.astype(BF16), preferred_element_type=F32) + counts
    rank0 = jnp.sum(jnp.where(oh0, csum, 0.0), axis=-1, keepdims=True)
    rank1 = jnp.sum(jnp.where(oh1, csum, 0.0), axis=-1, keepdims=True)
    return rank0, rank1, counts + jnp.sum(s, axis=0, keepdims=True)


def _route_body(lg_ref, rt_ref, cnt_ref, cnt_scr):
    @pl.when(pl.program_id(0) == 0)
    def _():
        cnt_scr[...] = jnp.zeros_like(cnt_scr)

    rt = _route_math(lg_ref[...])
    rank0, rank1, counts = _rank_math(rt, cnt_scr[...])
    cnt_scr[...] = counts
    lane = lax.broadcasted_iota(jnp.int32, rt.shape, 1)
    rt_ref[...] = jnp.where(lane == RT_RANK, rank0, jnp.where(lane == RT_RANK + 1, rank1, rt))
    cnt_ref[...] = jnp.broadcast_to(counts, cnt_ref.shape)


def _route(logits):
    tm = ROUTE_TM
    return pl.pallas_call(
        _route_body,
        grid=(N_TOK // tm,),
        in_specs=[pl.BlockSpec((tm, LANES), lambda i: (i, 0))],
        out_specs=[pl.BlockSpec((tm, LANES), lambda i: (i, 0)),
                   pl.BlockSpec((SUBLANES, LANES), lambda i: (0, 0))],
        out_shape=[jax.ShapeDtypeStruct((N_TOK, LANES), F32),
                   jax.ShapeDtypeStruct((SUBLANES, LANES), F32)],
        scratch_shapes=[pltpu.VMEM((1, LANES), F32)],
        compiler_params=_cparams(("arbitrary",)),
        name="route",
    )(logits)


def _plan_body(rt_ref, cnt_ref, pos_ref, tt_ref):
    t = PLAN_TM
    lane = lax.broadcasted_iota(jnp.int32, (t, LANES), 1)
    rt = rt_ref[...]
    oh0 = lane.astype(F32) == rt[:, 0:1]
    oh1 = lane.astype(F32) == rt[:, 1:2]
    if True:
        counts = cnt_ref[0:1, :]
        tiles = jnp.floor((counts + (MOE_TM - 1)) * (1.0 / MOE_TM))
        r = lax.broadcasted_iota(jnp.int32, (LANES, LANES), 0)
        c = lax.broadcasted_iota(jnp.int32, (LANES, LANES), 1)
        upto = jnp.where(r <= c, 1.0, 0.0).astype(BF16)
        tiles8 = jnp.broadcast_to(tiles, (8, LANES)).astype(BF16)
        tile_end = jnp.dot(tiles8, upto, preferred_element_type=F32)[0:1]
        offs = (tile_end - tiles) * MOE_TM
        p0 = jnp.sum(jnp.where(oh0, offs, 0.0), axis=-1, keepdims=True) + rt[:, RT_RANK:RT_RANK + 1]
        p1 = jnp.sum(jnp.where(oh1, offs, 0.0), axis=-1, keepdims=True) + rt[:, RT_RANK + 1:RT_RANK + 2]
        pos_ref[...] = (jnp.where(lane == 0, p0, jnp.where(lane == 1, p1, 0.0)) * SUBLANES).astype(jnp.int32)

        lane_e = lax.broadcasted_iota(jnp.int32, (LANES, LANES), 1)
        tile_id = lax.broadcasted_iota(jnp.int32, (LANES, LANES), 0).astype(F32)
        is_e = lane_e < N_EXPERTS
        total = jnp.max(tile_end, axis=-1, keepdims=True)
        t_exp = jnp.sum(jnp.where(is_e & (tile_end <= tile_id), 1.0, 0.0), axis=-1, keepdims=True)
        t_valid = jnp.where(tile_id[:, 0:1] < total, 1.0, 0.0)
        last_exp = jnp.sum(jnp.where(is_e & (tile_end <= total - 1.0), 1.0, 0.0), axis=-1, keepdims=True)
        t_exp = jnp.where(t_valid > 0, t_exp, last_exp)
        t_first = jnp.sum(jnp.where(is_e & (tiles > 0) & ((tile_end - tiles) == tile_id), 1.0, 0.0),
                          axis=-1, keepdims=True)
        t_last = jnp.sum(jnp.where(is_e & (tiles > 0) & ((tile_end - 1.0) == tile_id), 1.0, 0.0),
                         axis=-1, keepdims=True)
        t_clear = jnp.maximum(t_last, 1.0 - t_valid)
        none = jnp.float32(LANES)
        owns = is_e & (tiles > 0)
        lane_f = lane_e.astype(F32)
        t_next = jnp.min(jnp.where(owns & (lane_f > t_exp), lane_f, none), axis=-1, keepdims=True)
        t_next2 = jnp.min(jnp.where(owns & (lane_f > t_next), lane_f, none), axis=-1, keepdims=True)
        t_ord = jnp.sum(jnp.where(owns & (lane_f < t_exp), 1.0, 0.0), axis=-1, keepdims=True)
        cols = {T_EXP: t_exp, T_VALID: t_valid, T_FIRST: t_first, T_CLEAR: t_clear,
                T_NEXT: t_next, T_NEXT2: t_next2, T_ORD: t_ord, T_LAST: total - 1.0}
        table = jnp.zeros((LANES, LANES), F32)
        for k, col in cols.items():
            table = jnp.where(lane_e == k, col, table)
        tt_ref[...] = table.astype(jnp.int32)


def _plan(route, counts):
    t = PLAN_TM
    return pl.pallas_call(
        _plan_body,
        grid=(N_TOK // t,),
        in_specs=[pl.BlockSpec((t, LANES), lambda i: (i, 0)),
                  pl.BlockSpec((SUBLANES, LANES), lambda i: (0, 0))],
        out_specs=[pl.BlockSpec((t, LANES), lambda i: (i, 0)),
                   pl.BlockSpec((LANES, LANES), lambda i: (0, 0))],
        out_shape=[jax.ShapeDtypeStruct((N_TOK, LANES), jnp.int32),
                   jax.ShapeDtypeStruct((LANES, LANES), jnp.int32)],
        compiler_params=_cparams(("arbitrary",)),
        name="plan",
    )(route, counts)


def _rows_wait(ref, n_rows, sem):
    pltpu.make_async_copy(ref.at[pl.ds(0, n_rows)], ref.at[pl.ds(0, n_rows)], sem).wait()


def _dispatch_body(pos_ref, tt_ref, h_ref, xs_hbm, zbuf, sem_z, sem_s):
    i = pl.program_id(0)
    tile_rows = MOE_TM * SUBLANES

    @pl.when(i == 0)
    def _():
        zbuf[...] = _pack_halves(jnp.zeros((tile_rows, 2 * LANES), F32))

        def zero_copy(tile):
            start = pl.multiple_of(tile * tile_rows, tile_rows)
            return pltpu.make_async_copy(zbuf, xs_hbm.at[pl.ds(start, tile_rows)], sem_z)

        def clear_start(tile, carry):
            @pl.when(tt_ref[tile, T_CLEAR] > 0)
            def _():
                zero_copy(tile).start()
            return carry

        def clear_wait(tile, carry):
            @pl.when(tt_ref[tile, T_CLEAR] > 0)
            def _():
                zero_copy(tile).wait()
            return carry

        lax.fori_loop(0, MOE_TILES, clear_start, 0)
        lax.fori_loop(0, MOE_TILES, clear_wait, 0)

    def tok(j, carry):
        src = h_ref.at[pl.ds(pl.multiple_of(j * SUBLANES, SUBLANES), SUBLANES)]
        pair = TOP_K * (i * DISPATCH_TM + j)
        for k in range(TOP_K):
            dst_row = pl.multiple_of(pos_ref[pair + k], SUBLANES)
            pltpu.make_async_copy(src, xs_hbm.at[pl.ds(dst_row, SUBLANES)], sem_s).start(priority=k)
        return carry

    lax.fori_loop(0, DISPATCH_TM, tok, 0, unroll=8)
    _rows_wait(xs_hbm, TOP_K * DISPATCH_TM * SUBLANES, sem_s)


def _dispatch(pos_rows, tile_clear, h2p):
    grid_spec = pltpu.PrefetchScalarGridSpec(
        num_scalar_prefetch=2,
        grid=(N_TOK // DISPATCH_TM,),
        in_specs=[pl.BlockSpec((DISPATCH_TM * SUBLANES, LANES), lambda i, *_: (i, 0))],
        out_specs=pl.BlockSpec(memory_space=pl.ANY),
        scratch_shapes=[pltpu.VMEM((MOE_TM * SUBLANES, LANES), U32),
                        pltpu.SemaphoreType.DMA(()), pltpu.SemaphoreType.DMA(())],
    )
    return pl.pallas_call(
        _dispatch_body,
        grid_spec=grid_spec,
        out_shape=jax.ShapeDtypeStruct((MOE_ROWS * SUBLANES, LANES), U32),
        compiler_params=_cparams(("arbitrary",)),
        name="dispatch",
    )(pos_rows, tile_clear, h2p)


def _moe_body(tt_ref, x_ref, wg_hbm, wu_hbm, wd_hbm, y_ref,
              wg_s, wu_s, wd_s, stg_g, stg_u, stg_d, sem):
    i = pl.program_id(0)

    def fetch(e, slot):
        return (pltpu.make_async_copy(wg_hbm.at[e], stg_g.at[slot], sem.at[slot, 0]),
                pltpu.make_async_copy(wu_hbm.at[e], stg_u.at[slot], sem.at[slot, 1]),
                pltpu.make_async_copy(wd_hbm.at[e], stg_d.at[slot], sem.at[slot, 2]))

    @pl.when(i == 0)
    def _():
        for cp in fetch(tt_ref[0, T_EXP], 0):
            cp.start()

        @pl.when(tt_ref[0, T_NEXT] < N_EXPERTS)
        def _():
            for cp in fetch(tt_ref[0, T_NEXT], 1):
                cp.start()

    @pl.when(tt_ref[i, T_FIRST] > 0)
    def _():
        slot = tt_ref[i, T_ORD] % 2
        for cp in fetch(tt_ref[i, T_EXP], slot):
            cp.wait()
        wg_s[...] = stg_g[slot].astype(BF16)
        wu_s[...] = stg_u[slot].astype(BF16)
        wd_s[...] = stg_d[slot].astype(BF16)

        @pl.when(tt_ref[i, T_NEXT2] < N_EXPERTS)
        def _():
            for cp in fetch(tt_ref[i, T_NEXT2], slot):
                cp.start()

    @pl.when(tt_ref[i, T_VALID] > 0)
    def _():
        lo, hi = _unpack_halves(_load_token_tiles(x_ref, 0, MOE_TM))
        xa = lo.astype(BF16)
        xb = hi.astype(BF16)
        g = (jnp.dot(xa, wg_s[:HALF_D, :], preferred_element_type=F32)
             + jnp.dot(xb, wg_s[HALF_D:, :], preferred_element_type=F32))
        u = (jnp.dot(xa, wu_s[:HALF_D, :], preferred_element_type=F32)
             + jnp.dot(xb, wu_s[HALF_D:, :], preferred_element_type=F32))
        hid = (g * jax.nn.sigmoid(g) * u).astype(BF16)
        _store_token_tiles(y_ref, _pack_halves(jnp.dot(hid, wd_s[...], preferred_element_type=F32)))

    @pl.when(tt_ref[i, T_VALID] == 0)
    def _():
        y_ref[...] = _pack_halves(jnp.zeros((MOE_TM * SUBLANES, 2 * LANES), F32))


def _moe(tile_tab, xs, wg, wu, wd):
    tm = MOE_TM
    grid_spec = pltpu.PrefetchScalarGridSpec(
        num_scalar_prefetch=1,
        grid=(MOE_TILES,),
        in_specs=[pl.BlockSpec((tm * SUBLANES, LANES), lambda i, tt: (jnp.minimum(i, tt[0, T_LAST]), 0)),
                  pl.BlockSpec(memory_space=pl.ANY), pl.BlockSpec(memory_space=pl.ANY),
                  pl.BlockSpec(memory_space=pl.ANY)],
        out_specs=pl.BlockSpec((tm * SUBLANES, LANES), lambda i, *_: (i, 0)),
        scratch_shapes=[pltpu.VMEM((D_MODEL, EXPERT_FF), BF16), pltpu.VMEM((D_MODEL, EXPERT_FF), BF16),
                        pltpu.VMEM((EXPERT_FF, D_MODEL), BF16),
                        pltpu.VMEM((2, D_MODEL, EXPERT_FF), F32), pltpu.VMEM((2, D_MODEL, EXPERT_FF), F32),
                        pltpu.VMEM((2, EXPERT_FF, D_MODEL), F32),
                        pltpu.SemaphoreType.DMA((2, 3))],
    )
    return pl.pallas_call(
        _moe_body,
        grid_spec=grid_spec,
        out_shape=jax.ShapeDtypeStruct((MOE_ROWS * SUBLANES, LANES), U32),
        compiler_params=_cparams(("arbitrary",)),
        name="moe",
    )(tile_tab, xs, wg, wu, wd)


def _final_body(pos_ref, x1_ref, mod_ref, rt_ref, g2_ref, b2_ref, ys_hbm, o_ref, buf, sem):
    i = pl.program_id(0)
    n = pl.num_programs(0)
    tm = FINAL_TM
    slot = i % FINAL_SLOTS
    nxt_slot = (i + 2) % FINAL_SLOTS
    nxt_tile = jnp.minimum(i + 2, n - 1)

    def issue(tile, dst_slot, j):
        pair = TOP_K * (tile * tm + j)
        for k in range(TOP_K):
            src_row = pl.multiple_of(pos_ref[pair + k], SUBLANES)
            dst_row = pl.multiple_of((k * tm + j) * SUBLANES, SUBLANES)
            pltpu.make_async_copy(ys_hbm.at[pl.ds(src_row, SUBLANES)],
                                  buf.at[dst_slot, pl.ds(dst_row, SUBLANES)], sem.at[dst_slot]).start(priority=k)

    def wait(s):
        pltpu.make_async_copy(ys_hbm.at[pl.ds(0, TOP_K * tm * SUBLANES)], buf.at[s], sem.at[s]).wait()

    @pl.when(i == 0)
    def _():
        def tok(j, carry):
            issue(0, 0, j)
            issue(jnp.minimum(1, n - 1), 1, j)
            return carry
        lax.fori_loop(0, tm, tok, 0, unroll=8)

    wait(slot)
    gate2 = mod_ref[0, 5:6, :]
    cur = buf.at[slot]

    def chunk(c, carry):
        r0 = pl.multiple_of(c * FINAL_CHUNK, FINAL_CHUNK)
        rows = pl.ds(r0, FINAL_CHUNK)
        a_lo, a_hi = _unpack_halves(_load_token_tiles(cur, r0, FINAL_CHUNK))
        b_lo, b_hi = _unpack_halves(_load_token_tiles(cur, tm + r0, FINAL_CHUNK))
        x1 = x1_ref[rows, :]
        rt = rt_ref[rows, :]
        for r in range(FINAL_CHUNK):
            issue(nxt_tile, nxt_slot, r0 + r)
        w0 = rt[:, 2:3]
        w1 = rt[:, 3:4]
        y = jnp.concatenate([w0 * a_lo + w1 * b_lo, w0 * a_hi + w1 * b_hi], axis=1)
        o_ref[rows, :] = _ln_rows(DEEPNORM_ALPHA * x1 + gate2 * y) * g2_ref[...] + b2_ref[...]
        return carry

    lax.fori_loop(0, tm // FINAL_CHUNK, chunk, 0)

    @pl.when(i == n - 1)
    def _():
        wait((i + 1) % FINAL_SLOTS)
        wait(nxt_slot)


def _final(pos_rows, x1, mod3, route, g2, b2, ys):
    tm = FINAL_TM
    tpb = SEQ // tm
    row = lambda w: pl.BlockSpec((tm, w), lambda i, *_: (i, 0))
    grid_spec = pltpu.PrefetchScalarGridSpec(
        num_scalar_prefetch=1,
        grid=(N_TOK // tm,),
        in_specs=[row(D_MODEL),
                  pl.BlockSpec((1, N_MOD, D_MODEL), lambda i, *_: (i // tpb, 0, 0)),
                  row(LANES),
                  pl.BlockSpec(g2.shape, lambda i, *_: (0, 0)),
                  pl.BlockSpec(b2.shape, lambda i, *_: (0, 0)),
                  pl.BlockSpec(memory_space=pl.ANY)],
        out_specs=row(D_MODEL),
        scratch_shapes=[pltpu.VMEM((FINAL_SLOTS, TOP_K * tm * SUBLANES, LANES), U32),
                        pltpu.SemaphoreType.DMA((FINAL_SLOTS,))],
    )
    return pl.pallas_call(
        _final_body,
        grid_spec=grid_spec,
        out_shape=jax.ShapeDtypeStruct((N_TOK, D_MODEL), F32),
        compiler_params=_cparams(("arbitrary",)),
        name="final",
    )(pos_rows, x1, mod3, route, g2, b2, ys)


def kernel(x, c, positions, w_ada, b_ada, w_in, q_norm_g, w_uq, kv_norm_g, w_ukv, sgu_norm_g, sgu_norm_b,
           w_spatial, b_spatial, w_o, ln1_g, ln1_b, w_router_group, b_router_group, w_router_expert,
           b_router_expert, w_gate, w_up, w_down, ln2_g, ln2_b):
    l = 0
    x2 = x.reshape(N_TOK, D_MODEL)
    mod3 = _ada(c, w_ada[l], b_ada[l][None, :]).reshape(BATCH, N_MOD, D_MODEL)

    wt, wuq, wukv, wo = _prep(w_in[l].T, w_uq[l], w_ukv[l], w_o[l])
    n_r = N_GROUPS + N_EXPERTS
    wr = jnp.pad(jnp.concatenate([w_router_group[l], w_router_expert[l]], axis=1),
                 ((0, 0), (0, LANES - n_r))).astype(BF16)
    br = jnp.pad(jnp.concatenate([b_router_group[l], b_router_expert[l]]), (0, LANES - n_r))[None, :]
    inv_freq = 1.0 / (ROPE_THETA ** (jnp.arange(0, QK_ROPE_DIM, 2, dtype=F32) / QK_ROPE_DIM))
    invf = jnp.tile(inv_freq, 2 * LANES // QK_ROPE_DIM)[None, :]

    cqn, ckvn, kpe, u, vs = _inproj(x2, mod3, wt, q_norm_g[l][None, :], kv_norm_g[l][None, :],
                                    sgu_norm_g[l][None, :], sgu_norm_b[l][None, :])
    q, k, v = _qkv(cqn, ckvn, kpe, positions.reshape(N_TOK, 1), invf, wuq, wukv)
    attn = _attn(q, k, v).reshape(N_TOK, MLA_WIDTH)
    x1, h2, logits = _mixout(x2, mod3, attn, u, vs, w_spatial[l], b_spatial[l].T, wo,
                             ln1_g[l][None, :], ln1_b[l][None, :], wr, br)
    route, counts = _route(logits)
    pos_tab, tile_tab = _plan(route, counts)
    pos_rows = pos_tab[:, 0:TOP_K].reshape(-1)
    xs = _dispatch(pos_rows, tile_tab, h2)
    ys = _moe(tile_tab, xs, w_gate[l], w_up[l], w_down[l])
    out = _final(pos_rows, x1, mod3, route, ln2_g[l][None, :], ln2_b[l][None, :], ys)
    return out.reshape(BATCH, SEQ, D_MODEL)
```

```python
import functools

import jax
import jax.numpy as jnp
import numpy as np
from jax import lax
from jax.experimental import pallas as pl
from jax.experimental.pallas import tpu as pltpu

D_MODEL = 2048
BATCH = 4
SEQ = 2048
N_TOK = BATCH * SEQ

MLA_HEADS = 8
QK_NOPE_DIM = 128
QK_ROPE_DIM = 64
QK_DIM = QK_NOPE_DIM + QK_ROPE_DIM
V_HEAD_DIM = 128
Q_LORA_RANK = 768
KV_LORA_RANK = 512
ROPE_THETA = 10000.0
MLA_WIDTH = MLA_HEADS * V_HEAD_DIM

SGU_GROUPS = 8
SGU_GROUP_DIM = 128
SGU_CHUNK = 128
SGU_WIDTH = SGU_GROUPS * SGU_GROUP_DIM

N_GROUPS = 4
EXPERTS_PER_GROUP = 8
N_EXPERTS = N_GROUPS * EXPERTS_PER_GROUP
TOP_K = 2
EXPERT_FF = 512

DEEPNORM_ALPHA = 2.0 ** 0.25
EPS = 1e-6
N_MOD = 6
NEG_BIG = -1e30

LANES = 128
SUBLANES = 8
VMEM_LIMIT = 56 * 1024 * 1024

ADA_TN = 1024
TOK_TM = 512
MIX_TM = 256
PREP_STEPS = 4
ATT_TQ = 1024
ATT_TK = 1024
ATT_HEADS = 4
MOE_TM = 256
MOE_TILES = (N_TOK * TOP_K + N_EXPERTS * (MOE_TM - 1)) // MOE_TM + 1
MOE_ROWS = MOE_TILES * MOE_TM
PLAN_TM = 2048
ROUTE_TM = 1024
RT_RANK = 4
DISPATCH_TM = 1024
FINAL_TM = 256
FINAL_CHUNK = 128
FINAL_SLOTS = 3
assert MOE_TILES <= LANES
T_EXP, T_VALID, T_FIRST, T_CLEAR, T_NEXT, T_NEXT2, T_NEXT3, T_ORD, T_LAST = range(9)
MOE_SLOTS = 3

F32 = jnp.float32
BF16 = jnp.bfloat16
U32 = jnp.uint32
HALF_D = D_MODEL // 2


def _cparams(sem):
    return pltpu.CompilerParams(dimension_semantics=sem, vmem_limit_bytes=VMEM_LIMIT)


def _const_spec(shape):
    nd = len(shape)
    return pl.BlockSpec(shape, lambda *_: (0,) * nd, pipeline_mode=pl.Buffered(1))


def _ln_rows(x):
    mu = jnp.mean(x, axis=-1, keepdims=True)
    xc = x - mu
    var = jnp.mean(xc * xc, axis=-1, keepdims=True)
    return xc * lax.rsqrt(var + EPS)


def _rms_rows(x):
    return x * lax.rsqrt(jnp.mean(x * x, axis=-1, keepdims=True) + EPS)


def _pack_halves(x):
    half = x.shape[-1] // 2
    return pltpu.pack_elementwise([x[:, :half], x[:, half:]], packed_dtype=BF16)


def _unpack_halves(w):
    lo = pltpu.unpack_elementwise(w, index=0, packed_dtype=BF16, unpacked_dtype=F32)
    hi = pltpu.unpack_elementwise(w, index=1, packed_dtype=BF16, unpacked_dtype=F32)
    return lo, hi


def _store_token_tiles(ref, w):
    rows = w.shape[0]
    for s in range(SUBLANES):
        ref[pl.ds(s, rows, stride=SUBLANES), :] = w[:, s * LANES:(s + 1) * LANES]


def _load_token_tiles(ref, start_row, rows):
    return jnp.concatenate([ref[pl.ds(start_row * SUBLANES + s, rows, stride=SUBLANES), :]
                            for s in range(SUBLANES)], axis=1)


def _gelu_tanh(x):
    c = np.sqrt(2.0 / np.pi).astype(np.float32)
    return 0.5 * x * (1.0 + jnp.tanh(c * (x + 0.044715 * (x * x * x))))


def _ada_body(c_ref, w_ref, b_ref, o_ref):
    o_ref[...] = jnp.dot(c_ref[...].astype(BF16), w_ref[...].astype(BF16),
                         preferred_element_type=F32) + b_ref[...]


def _ada(c, w, b):
    n = w.shape[1]
    return pl.pallas_call(
        _ada_body,
        grid=(n // ADA_TN,),
        in_specs=[pl.BlockSpec((BATCH, D_MODEL), lambda j: (0, 0)),
                  pl.BlockSpec((D_MODEL, ADA_TN), lambda j: (0, j)),
                  pl.BlockSpec((1, ADA_TN), lambda j: (0, j))],
        out_specs=pl.BlockSpec((BATCH, ADA_TN), lambda j: (0, j)),
        out_shape=jax.ShapeDtypeStruct((BATCH, n), F32),
        compiler_params=_cparams(("parallel",)),
        name="ada",
    )(c, w, b)


def _prep_body(win_ref, wuq_ref, wukv_ref, wo_ref, win_o, wuq_o, wukv_o, wo_o):
    win_o[...] = win_ref[...].astype(BF16)
    u = wuq_ref[...]
    nope = [u[:, h * QK_DIM:h * QK_DIM + QK_NOPE_DIM] for h in range(MLA_HEADS)]
    rope = [u[:, h * QK_DIM + QK_NOPE_DIM:(h + 1) * QK_DIM] for h in range(MLA_HEADS)]
    wuq_o[...] = jnp.concatenate(nope + rope, axis=1).astype(BF16)
    kv = wukv_ref[...]
    hw = QK_NOPE_DIM + V_HEAD_DIM
    kn = [kv[:, h * hw:h * hw + QK_NOPE_DIM] for h in range(MLA_HEADS)]
    vv = [kv[:, h * hw + QK_NOPE_DIM:(h + 1) * hw] for h in range(MLA_HEADS)]
    wukv_o[...] = jnp.concatenate(kn + vv, axis=1).astype(BF16)
    wo_o[...] = wo_ref[...].astype(BF16)


def _prep(w_in_t, w_uq, w_ukv, w_o):
    steps = PREP_STEPS
    blk = lambda a: pl.BlockSpec((a.shape[0] // steps, a.shape[1]), lambda i: (i, 0))
    ins = (w_in_t, w_uq, w_ukv, w_o)
    return pl.pallas_call(
        _prep_body,
        grid=(steps,),
        in_specs=[blk(a) for a in ins],
        out_specs=[blk(a) for a in ins],
        out_shape=[jax.ShapeDtypeStruct(a.shape, BF16) for a in ins],
        compiler_params=_cparams(("parallel",)),
        name="prep",
    )(*ins)


def _inproj_body(x_ref, mod_ref, wt_ref, gq_ref, gkv_ref, sg_ref, sb_ref,
                 cq_ref, ckv_ref, kpe_ref, u_ref, vs_ref):
    o1, o2, o3 = Q_LORA_RANK, Q_LORA_RANK + KV_LORA_RANK, Q_LORA_RANK + KV_LORA_RANK + QK_ROPE_DIM

    def proj(lo, hi):
        return lax.dot_general(h, wt_ref[lo:hi, :], (((1,), (1,)), ((), ())), preferred_element_type=F32)

    sh = mod_ref[0, 0:1, :]
    sc = mod_ref[0, 1:2, :]
    h = (_ln_rows(x_ref[...]) * (1.0 + sc) + sh).astype(BF16)
    cq_ref[...] = (_rms_rows(proj(0, o1)) * gq_ref[...]).astype(BF16)
    ckv_ref[...] = (_rms_rows(proj(o1, o2)) * gkv_ref[...]).astype(BF16)
    kpe_ref[...] = proj(o2, o2 + LANES)
    gz = _gelu_tanh(proj(o3, o3 + 2 * SGU_WIDTH))
    u_ref[...] = gz[:, :SGU_WIDTH]
    vs_ref[...] = (_ln_rows(gz[:, SGU_WIDTH:]) * sg_ref[...] + sb_ref[...]).astype(BF16)


def _inproj(x2, mod3, wt, gq, gkv, sg, sb):
    tm = TOK_TM
    tiles_per_batch = SEQ // tm
    row = lambda w: pl.BlockSpec((tm, w), lambda i: (i, 0))
    return pl.pallas_call(
        _inproj_body,
        grid=(N_TOK // tm,),
        in_specs=[row(D_MODEL),
                  pl.BlockSpec((1, N_MOD, D_MODEL), lambda i: (i // tiles_per_batch, 0, 0)),
                  _const_spec(wt.shape),
                  _const_spec(gq.shape), _const_spec(gkv.shape), _const_spec(sg.shape), _const_spec(sb.shape)],
        out_specs=[row(Q_LORA_RANK), row(KV_LORA_RANK), row(LANES), row(SGU_WIDTH), row(SGU_WIDTH)],
        out_shape=[jax.ShapeDtypeStruct((N_TOK, Q_LORA_RANK), BF16),
                   jax.ShapeDtypeStruct((N_TOK, KV_LORA_RANK), BF16),
                   jax.ShapeDtypeStruct((N_TOK, LANES), F32),
                   jax.ShapeDtypeStruct((N_TOK, SGU_WIDTH), F32),
                   jax.ShapeDtypeStruct((N_TOK, SGU_WIDTH), BF16)],
        compiler_params=_cparams(("parallel",)),
        name="inproj",
    )(x2, mod3, wt, gq, gkv, sg, sb)


def _rope(x, cos, sin):
    w = x.shape[-1]
    lane = lax.broadcasted_iota(jnp.int32, x.shape, 1)
    first_half = (lane % QK_ROPE_DIM) < (QK_ROPE_DIM // 2)
    rot = jnp.where(first_half,
                    -pltpu.roll(x, w - QK_ROPE_DIM // 2, 1),
                    pltpu.roll(x, QK_ROPE_DIM // 2, 1))
    return x * cos + rot * sin


def _qkv_body(cq_ref, ckv_ref, kpe_ref, pos_ref, invf_ref, wuq_ref, wukv_ref, q_ref, k_ref, v_ref):
    ang = pos_ref[...].astype(F32) * invf_ref[...]
    cos1 = jnp.cos(ang)
    sin1 = jnp.sin(ang)
    reps = MLA_HEADS * QK_ROPE_DIM // LANES
    cos = jnp.concatenate([cos1] * reps, axis=1)
    sin = jnp.concatenate([sin1] * reps, axis=1)
    scale = np.float32(QK_DIM ** -0.5)
    q = jnp.dot(cq_ref[...], wuq_ref[...], preferred_element_type=F32) * scale
    q_pe = _rope(q[:, MLA_HEADS * QK_NOPE_DIM:], cos, sin)
    kv = jnp.dot(ckv_ref[...], wukv_ref[...], preferred_element_type=F32)
    k_pe = _rope(kpe_ref[...], cos1, sin1)[:, :QK_ROPE_DIM].astype(BF16)
    for h in range(MLA_HEADS):
        q_ref[0, h, :, 0:QK_NOPE_DIM] = q[:, h * QK_NOPE_DIM:(h + 1) * QK_NOPE_DIM].astype(BF16)
        q_ref[0, h, :, QK_NOPE_DIM:QK_DIM] = q_pe[:, h * QK_ROPE_DIM:(h + 1) * QK_ROPE_DIM].astype(BF16)
        k_ref[0, h, :, 0:QK_NOPE_DIM] = kv[:, h * QK_NOPE_DIM:(h + 1) * QK_NOPE_DIM].astype(BF16)
        k_ref[0, h, :, QK_NOPE_DIM:QK_DIM] = k_pe
        v_ref[0, h, :, :] = kv[:, MLA_WIDTH + h * V_HEAD_DIM:MLA_WIDTH + (h + 1) * V_HEAD_DIM].astype(BF16)


def _qkv(cqn, ckvn, kpe, pos2, invf, wuq, wukv):
    tm = TOK_TM
    tpb = SEQ // tm
    row = lambda w: pl.BlockSpec((tm, w), lambda i: (i, 0))
    head_out = lambda w: pl.BlockSpec((1, MLA_HEADS, tm, w), lambda i: (i // tpb, 0, i % tpb, 0))
    return pl.pallas_call(
        _qkv_body,
        grid=(N_TOK // tm,),
        in_specs=[row(Q_LORA_RANK), row(KV_LORA_RANK), row(LANES), row(1),
                  _const_spec(invf.shape), _const_spec(wuq.shape), _const_spec(wukv.shape)],
        out_specs=[head_out(QK_DIM), head_out(QK_DIM), head_out(V_HEAD_DIM)],
        out_shape=[jax.ShapeDtypeStruct((BATCH, MLA_HEADS, SEQ, QK_DIM), BF16),
                   jax.ShapeDtypeStruct((BATCH, MLA_HEADS, SEQ, QK_DIM), BF16),
                   jax.ShapeDtypeStruct((BATCH, MLA_HEADS, SEQ, V_HEAD_DIM), BF16)],
        compiler_params=_cparams(("parallel",)),
        name="qkv",
    )(cqn, ckvn, kpe, pos2, invf, wuq, wukv)


def _attn_body(q_ref, k_ref, v_ref, o_ref):
    i = pl.program_id(2)

    def step(h, j, carry, masked):
        m, l, acc = carry
        start = pl.multiple_of(j * ATT_TK, ATT_TK)
        k = k_ref[0, h, pl.ds(start, ATT_TK), :]
        v = v_ref[0, h, pl.ds(start, ATT_TK), :]
        s = lax.dot_general(q_ref[0, h], k, (((1,), (1,)), ((), ())), preferred_element_type=F32)
        if masked:
            r = lax.broadcasted_iota(jnp.int32, s.shape, 0)
            c = lax.broadcasted_iota(jnp.int32, s.shape, 1)
            s = jnp.where(c <= r, s, NEG_BIG)
        m_new = jnp.maximum(m, jnp.max(s, axis=-1, keepdims=True))
        p = jnp.exp(s - m_new)
        a = jnp.exp(m - m_new)
        l = a * l + jnp.sum(p, axis=-1, keepdims=True)
        acc = a * acc + jnp.dot(p.astype(BF16), v, preferred_element_type=F32)
        return m_new, l, acc

    def steps(j, carries, masked):
        return tuple(step(h, j, carries[h], masked) for h in range(ATT_HEADS))

    init = tuple((jnp.full((ATT_TQ, 1), NEG_BIG, F32), jnp.zeros((ATT_TQ, 1), F32),
                  jnp.zeros((ATT_TQ, V_HEAD_DIM), F32)) for _ in range(ATT_HEADS))
    carries = lax.fori_loop(0, i, lambda j, c: steps(j, c, False), init)
    carries = steps(i, carries, True)
    for h, (m, l, acc) in enumerate(carries):
        o_ref[0, :, h * V_HEAD_DIM:(h + 1) * V_HEAD_DIM] = (acc / l).astype(BF16)


def _attn(q, k, v):
    assert ATT_TQ == ATT_TK
    hb = ATT_HEADS
    return pl.pallas_call(
        _attn_body,
        grid=(BATCH, MLA_HEADS // hb, SEQ // ATT_TQ),
        in_specs=[pl.BlockSpec((1, hb, ATT_TQ, QK_DIM), lambda b, h, i: (b, h, i, 0)),
                  pl.BlockSpec((1, hb, SEQ, QK_DIM), lambda b, h, i: (b, h, 0, 0)),
                  pl.BlockSpec((1, hb, SEQ, V_HEAD_DIM), lambda b, h, i: (b, h, 0, 0))],
        out_specs=pl.BlockSpec((1, ATT_TQ, hb * V_HEAD_DIM), lambda b, h, i: (b, i, h)),
        out_shape=jax.ShapeDtypeStruct((BATCH, SEQ, MLA_WIDTH), BF16),
        compiler_params=_cparams(("parallel", "parallel", "arbitrary")),
        name="attn",
    )(q, k, v)


def _mixout_body(x_ref, mod_ref, attn_ref, u_ref, vs_ref, wsp_ref, bsp_ref, woa_ref, wos_ref,
                 g1_ref, b1_ref, wr_ref, br_ref, x1_ref, h2_ref, lg_ref, sgu_scr):
    r = lax.broadcasted_iota(jnp.int32, (SGU_CHUNK, SGU_CHUNK), 0)
    c = lax.broadcasted_iota(jnp.int32, (SGU_CHUNK, SGU_CHUNK), 1)
    causal = c <= r
    for g in range(SGU_GROUPS):
        ws = jnp.where(causal, wsp_ref[g], 0.0).astype(BF16)
        bias = bsp_ref[:, g:g + 1]
        cols = slice(g * SGU_GROUP_DIM, (g + 1) * SGU_GROUP_DIM)
        for ch in range(MIX_TM // SGU_CHUNK):
            rows = slice(ch * SGU_CHUNK, (ch + 1) * SGU_CHUNK)
            mixed = jnp.dot(ws, vs_ref[rows, cols], preferred_element_type=F32) + bias
            sgu_scr[rows, cols] = (u_ref[rows, cols] * mixed).astype(BF16)
    y = (jnp.dot(attn_ref[...], woa_ref[...], preferred_element_type=F32)
         + jnp.dot(sgu_scr[...], wos_ref[...], preferred_element_type=F32))
    gate1 = mod_ref[0, 2:3, :]
    sh2 = mod_ref[0, 3:4, :]
    sc2 = mod_ref[0, 4:5, :]
    x1 = _ln_rows(DEEPNORM_ALPHA * x_ref[...] + gate1 * y) * g1_ref[...] + b1_ref[...]
    x1_ref[...] = x1
    h2 = _ln_rows(x1) * (1.0 + sc2) + sh2
    _store_token_tiles(h2_ref, _pack_halves(h2))
    lg_ref[...] = jnp.dot(h2.astype(BF16), wr_ref[...], preferred_element_type=F32) + br_ref[...]


def _mixout(x2, mod3, attn, u, vs, wsp, bsp_t, wo, g1, b1, wr, br):
    tm = MIX_TM
    tpb = SEQ // tm
    row = lambda w: pl.BlockSpec((tm, w), lambda i: (i, 0))
    wo_half = lambda j: pl.BlockSpec((MLA_WIDTH, D_MODEL), lambda i: (j, 0), pipeline_mode=pl.Buffered(1))
    return pl.pallas_call(
        _mixout_body,
        grid=(N_TOK // tm,),
        in_specs=[row(D_MODEL),
                  pl.BlockSpec((1, N_MOD, D_MODEL), lambda i: (i // tpb, 0, 0)),
                  row(MLA_WIDTH), row(SGU_WIDTH), row(SGU_WIDTH),
                  _const_spec(wsp.shape), _const_spec(bsp_t.shape), wo_half(0), wo_half(1),
                  _const_spec(g1.shape), _const_spec(b1.shape), _const_spec(wr.shape), _const_spec(br.shape)],
        out_specs=[row(D_MODEL), pl.BlockSpec((tm * SUBLANES, LANES), lambda i: (i, 0)), row(LANES)],
        out_shape=[jax.ShapeDtypeStruct((N_TOK, D_MODEL), F32),
                   jax.ShapeDtypeStruct((N_TOK * SUBLANES, LANES), U32),
                   jax.ShapeDtypeStruct((N_TOK, LANES), F32)],
        scratch_shapes=[pltpu.VMEM((tm, SGU_WIDTH), BF16)],
        compiler_params=_cparams(("parallel",)),
        name="mix_out",
    )(x2, mod3, attn, u, vs, wsp, bsp_t, wo, wo, g1, b1, wr, br)


def _route_math(lg):
    lane = lax.broadcasted_iota(jnp.int32, lg.shape, 1)
    big = jnp.int32(LANES)

    def top1(vals):
        m = jnp.max(vals, axis=-1, keepdims=True)
        idx = jnp.min(jnp.where(vals == m, lane, big), axis=-1, keepdims=True)
        return m, idx

    is_group = lane < N_GROUPS
    glog = jnp.where(is_group, lg, -jnp.inf)
    gmax, gidx = top1(glog)
    pg_top = 1.0 / jnp.sum(jnp.exp(glog - gmax), axis=-1, keepdims=True)
    eid = lane - N_GROUPS
    sel = (eid >= gidx * EXPERTS_PER_GROUP) & (eid < (gidx + 1) * EXPERTS_PER_GROUP)
    elog = jnp.where(sel, lg, -jnp.inf)
    m1, i1 = top1(elog)
    m2, i2 = top1(jnp.where(lane == i1, -jnp.inf, elog))
    e2 = jnp.exp(m2 - m1)
    w1 = pg_top / (1.0 + e2)
    w2 = pg_top * e2 / (1.0 + e2)
    return jnp.where(lane == 0, (i1 - N_GROUPS).astype(F32),
                     jnp.where(lane == 1, (i2 - N_GROUPS).astype(F32),
                               jnp.where(lane == 2, w1, jnp.where(lane == 3, w2, 0.0))))


def _rank_math(rt, counts):
    t = rt.shape[0]
    lane = lax.broadcasted_iota(jnp.int32, (t, LANES), 1).astype(F32)
    oh0 = lane == rt[:, 0:1]
    oh1 = lane == rt[:, 1:2]
    s = jnp.where(oh0 | oh1, 1.0, 0.0)
    r = lax.broadcasted_iota(jnp.int32, (t, t), 0)
    c = lax.broadcasted_iota(jnp.int32, (t, t), 1)
    before = jnp.where(c < r, 1.0, 0.0).astype(BF16)
    csum = jnp.dot(before, s.astype(BF16), preferred_element_type=F32) + counts
    rank0 = jnp.sum(jnp.where(oh0, csum, 0.0), axis=-1, keepdims=True)
    rank1 = jnp.sum(jnp.where(oh1, csum, 0.0), axis=-1, keepdims=True)
    return rank0, rank1, counts + jnp.sum(s, axis=0, keepdims=True)


def _route_body(lg_ref, rt_ref, cnt_ref, cnt_scr):
    @pl.when(pl.program_id(0) == 0)
    def _():
        cnt_scr[...] = jnp.zeros_like(cnt_scr)

    rt = _route_math(lg_ref[...])
    rank0, rank1, counts = _rank_math(rt, cnt_scr[...])
    cnt_scr[...] = counts
    lane = lax.broadcasted_iota(jnp.int32, rt.shape, 1)
    rt_ref[...] = jnp.where(lane == RT_RANK, rank0, jnp.where(lane == RT_RANK + 1, rank1, rt))
    cnt_ref[...] = jnp.broadcast_to(counts, cnt_ref.shape)


def _route(logits):
    tm = ROUTE_TM
    return pl.pallas_call(
        _route_body,
        grid=(N_TOK // tm,),
        in_specs=[pl.BlockSpec((tm, LANES), lambda i: (i, 0))],
        out_specs=[pl.BlockSpec((tm, LANES), lambda i: (i, 0)),
                   pl.BlockSpec((SUBLANES, LANES), lambda i: (0, 0))],
        out_shape=[jax.ShapeDtypeStruct((N_TOK, LANES), F32),
                   jax.ShapeDtypeStruct((SUBLANES, LANES), F32)],
        scratch_shapes=[pltpu.VMEM((1, LANES), F32)],
        compiler_params=_cparams(("arbitrary",)),
        name="route",
    )(logits)


def _plan_body(rt_ref, cnt_ref, pos_ref, tt_ref):
    t = PLAN_TM
    lane = lax.broadcasted_iota(jnp.int32, (t, LANES), 1)
    rt = rt_ref[...]
    oh0 = lane.astype(F32) == rt[:, 0:1]
    oh1 = lane.astype(F32) == rt[:, 1:2]
    if True:
        counts = cnt_ref[0:1, :]
        tiles = jnp.floor((counts + (MOE_TM - 1)) * (1.0 / MOE_TM))
        r = lax.broadcasted_iota(jnp.int32, (LANES, LANES), 0)
        c = lax.broadcasted_iota(jnp.int32, (LANES, LANES), 1)
        upto = jnp.where(r <= c, 1.0, 0.0).astype(BF16)
        tiles8 = jnp.broadcast_to(tiles, (8, LANES)).astype(BF16)
        tile_end = jnp.dot(tiles8, upto, preferred_element_type=F32)[0:1]
        offs = (tile_end - tiles) * MOE_TM
        p0 = jnp.sum(jnp.where(oh0, offs, 0.0), axis=-1, keepdims=True) + rt[:, RT_RANK:RT_RANK + 1]
        p1 = jnp.sum(jnp.where(oh1, offs, 0.0), axis=-1, keepdims=True) + rt[:, RT_RANK + 1:RT_RANK + 2]
        pos_ref[...] = (jnp.where(lane == 0, p0, jnp.where(lane == 1, p1, 0.0)) * SUBLANES).astype(jnp.int32)

        lane_e = lax.broadcasted_iota(jnp.int32, (LANES, LANES), 1)
        tile_id = lax.broadcasted_iota(jnp.int32, (LANES, LANES), 0).astype(F32)
        is_e = lane_e < N_EXPERTS
        total = jnp.max(tile_end, axis=-1, keepdims=True)
        t_exp = jnp.sum(jnp.where(is_e & (tile_end <= tile_id), 1.0, 0.0), axis=-1, keepdims=True)
        t_valid = jnp.where(tile_id[:, 0:1] < total, 1.0, 0.0)
        last_exp = jnp.sum(jnp.where(is_e & (tile_end <= total - 1.0), 1.0, 0.0), axis=-1, keepdims=True)
        t_exp = jnp.where(t_valid > 0, t_exp, last_exp)
        t_first = jnp.sum(jnp.where(is_e & (tiles > 0) & ((tile_end - tiles) == tile_id), 1.0, 0.0),
                          axis=-1, keepdims=True)
        t_last = jnp.sum(jnp.where(is_e & (tiles > 0) & ((tile_end - 1.0) == tile_id), 1.0, 0.0),
                         axis=-1, keepdims=True)
        t_clear = jnp.maximum(t_last, 1.0 - t_valid)
        none = jnp.float32(LANES)
        owns = is_e & (tiles > 0)
        lane_f = lane_e.astype(F32)
        t_next = jnp.min(jnp.where(owns & (lane_f > t_exp), lane_f, none), axis=-1, keepdims=True)
        t_next2 = jnp.min(jnp.where(owns & (lane_f > t_next), lane_f, none), axis=-1, keepdims=True)
        t_next3 = jnp.min(jnp.where(owns & (lane_f > t_next2), lane_f, none), axis=-1, keepdims=True)
        t_ord = jnp.sum(jnp.where(owns & (lane_f < t_exp), 1.0, 0.0), axis=-1, keepdims=True)
        cols = {T_EXP: t_exp, T_VALID: t_valid, T_FIRST: t_first, T_CLEAR: t_clear, T_NEXT: t_next,
                T_NEXT2: t_next2, T_NEXT3: t_next3, T_ORD: t_ord, T_LAST: total - 1.0}
        table = jnp.zeros((LANES, LANES), F32)
        for k, col in cols.items():
            table = jnp.where(lane_e == k, col, table)
        tt_ref[...] = table.astype(jnp.int32)


def _plan(route, counts):
    t = PLAN_TM
    return pl.pallas_call(
        _plan_body,
        grid=(N_TOK // t,),
        in_specs=[pl.BlockSpec((t, LANES), lambda i: (i, 0)),
                  pl.BlockSpec((SUBLANES, LANES), lambda i: (0, 0))],
        out_specs=[pl.BlockSpec((t, LANES), lambda i: (i, 0)),
                   pl.BlockSpec((LANES, LANES), lambda i: (0, 0))],
        out_shape=[jax.ShapeDtypeStruct((N_TOK, LANES), jnp.int32),
                   jax.ShapeDtypeStruct((LANES, LANES), jnp.int32)],
        compiler_params=_cparams(("arbitrary",)),
        name="plan",
    )(route, counts)


def _rows_wait(ref, n_rows, sem):
    pltpu.make_async_copy(ref.at[pl.ds(0, n_rows)], ref.at[pl.ds(0, n_rows)], sem).wait()


def _dispatch_body(pos_ref, tt_ref, h_ref, xs_hbm, zbuf, sem_z, sem_s):
    i = pl.program_id(0)
    tile_rows = MOE_TM * SUBLANES

    @pl.when(i == 0)
    def _():
        zbuf[...] = _pack_halves(jnp.zeros((tile_rows, 2 * LANES), F32))

        def zero_copy(tile):
            start = pl.multiple_of(tile * tile_rows, tile_rows)
            return pltpu.make_async_copy(zbuf, xs_hbm.at[pl.ds(start, tile_rows)], sem_z)

        def clear_start(tile, carry):
            @pl.when(tt_ref[tile, T_CLEAR] > 0)
            def _():
                zero_copy(tile).start()
            return carry

        def clear_wait(tile, carry):
            @pl.when(tt_ref[tile, T_CLEAR] > 0)
            def _():
                zero_copy(tile).wait()
            return carry

        lax.fori_loop(0, MOE_TILES, clear_start, 0)
        lax.fori_loop(0, MOE_TILES, clear_wait, 0)

    def tok(j, carry):
        src = h_ref.at[pl.ds(pl.multiple_of(j * SUBLANES, SUBLANES), SUBLANES)]
        pair = TOP_K * (i * DISPATCH_TM + j)
        for k in range(TOP_K):
            dst_row = pl.multiple_of(pos_ref[pair + k], SUBLANES)
            pltpu.make_async_copy(src, xs_hbm.at[pl.ds(dst_row, SUBLANES)], sem_s).start(priority=k)
        return carry

    lax.fori_loop(0, DISPATCH_TM, tok, 0, unroll=8)
    _rows_wait(xs_hbm, TOP_K * DISPATCH_TM * SUBLANES, sem_s)


def _dispatch(pos_rows, tile_clear, h2p):
    grid_spec = pltpu.PrefetchScalarGridSpec(
        num_scalar_prefetch=2,
        grid=(N_TOK // DISPATCH_TM,),
        in_specs=[pl.BlockSpec((DISPATCH_TM * SUBLANES, LANES), lambda i, *_: (i, 0))],
        out_specs=pl.BlockSpec(memory_space=pl.ANY),
        scratch_shapes=[pltpu.VMEM((MOE_TM * SUBLANES, LANES), U32),
                        pltpu.SemaphoreType.DMA(()), pltpu.SemaphoreType.DMA(())],
    )
    return pl.pallas_call(
        _dispatch_body,
        grid_spec=grid_spec,
        out_shape=jax.ShapeDtypeStruct((MOE_ROWS * SUBLANES, LANES), U32),
        compiler_params=_cparams(("arbitrary",)),
        name="dispatch",
    )(pos_rows, tile_clear, h2p)


def _moe_body(tt_ref, x_ref, wg_hbm, wu_hbm, wd_hbm, y_ref,
              wg_s, wu_s, wd_s, stg_g, stg_u, stg_d, sem):
    i = pl.program_id(0)
    ahead = (T_EXP, T_NEXT, T_NEXT2, T_NEXT3)

    def fetch(e, slot):
        return (pltpu.make_async_copy(wg_hbm.at[e], stg_g.at[slot], sem.at[slot, 0]),
                pltpu.make_async_copy(wu_hbm.at[e], stg_u.at[slot], sem.at[slot, 1]),
                pltpu.make_async_copy(wd_hbm.at[e], stg_d.at[slot], sem.at[slot, 2]))

    @pl.when(i == 0)
    def _():
        for d in range(MOE_SLOTS):
            @pl.when(tt_ref[0, ahead[d]] < N_EXPERTS)
            def _():
                for cp in fetch(tt_ref[0, ahead[d]], d):
                    cp.start()

    @pl.when(tt_ref[i, T_FIRST] > 0)
    def _():
        slot = tt_ref[i, T_ORD] % MOE_SLOTS
        for cp in fetch(tt_ref[i, T_EXP], slot):
            cp.wait()
        wg_s[...] = stg_g[slot].astype(BF16)
        wu_s[...] = stg_u[slot].astype(BF16)
        wd_s[...] = stg_d[slot].astype(BF16)

        @pl.when(tt_ref[i, ahead[MOE_SLOTS]] < N_EXPERTS)
        def _():
            for cp in fetch(tt_ref[i, ahead[MOE_SLOTS]], slot):
                cp.start()

    @pl.when(tt_ref[i, T_VALID] > 0)
    def _():
        lo, hi = _unpack_halves(_load_token_tiles(x_ref, 0, MOE_TM))
        xa = lo.astype(BF16)
        xb = hi.astype(BF16)
        g = (jnp.dot(xa, wg_s[:HALF_D, :], preferred_element_type=F32)
             + jnp.dot(xb, wg_s[HALF_D:, :], preferred_element_type=F32))
        u = (jnp.dot(xa, wu_s[:HALF_D, :], preferred_element_type=F32)
             + jnp.dot(xb, wu_s[HALF_D:, :], preferred_element_type=F32))
        hid = (g * jax.nn.sigmoid(g) * u).astype(BF16)
        _store_token_tiles(y_ref, _pack_halves(jnp.dot(hid, wd_s[...], preferred_element_type=F32)))

    @pl.when(tt_ref[i, T_VALID] == 0)
    def _():
        y_ref[...] = _pack_halves(jnp.zeros((MOE_TM * SUBLANES, 2 * LANES), F32))


def _moe(tile_tab, xs, wg, wu, wd):
    tm = MOE_TM
    grid_spec = pltpu.PrefetchScalarGridSpec(
        num_scalar_prefetch=1,
        grid=(MOE_TILES,),
        in_specs=[pl.BlockSpec((tm * SUBLANES, LANES), lambda i, tt: (jnp.minimum(i, tt[0, T_LAST]), 0)),
                  pl.BlockSpec(memory_space=pl.ANY), pl.BlockSpec(memory_space=pl.ANY),
                  pl.BlockSpec(memory_space=pl.ANY)],
        out_specs=pl.BlockSpec((tm * SUBLANES, LANES), lambda i, *_: (i, 0)),
        scratch_shapes=[pltpu.VMEM((D_MODEL, EXPERT_FF), BF16), pltpu.VMEM((D_MODEL, EXPERT_FF), BF16),
                        pltpu.VMEM((EXPERT_FF, D_MODEL), BF16),
                        pltpu.VMEM((MOE_SLOTS, D_MODEL, EXPERT_FF), F32),
                        pltpu.VMEM((MOE_SLOTS, D_MODEL, EXPERT_FF), F32),
                        pltpu.VMEM((MOE_SLOTS, EXPERT_FF, D_MODEL), F32),
                        pltpu.SemaphoreType.DMA((MOE_SLOTS, 3))],
    )
    return pl.pallas_call(
        _moe_body,
        grid_spec=grid_spec,
        out_shape=jax.ShapeDtypeStruct((MOE_ROWS * SUBLANES, LANES), U32),
        compiler_params=_cparams(("arbitrary",)),
        name="moe",
    )(tile_tab, xs, wg, wu, wd)


def _final_body(pos_ref, x1_ref, mod_ref, rt_ref, g2_ref, b2_ref, ys_hbm, o_ref, buf, sem):
    i = pl.program_id(0)
    n = pl.num_programs(0)
    tm = FINAL_TM
    slot = i % FINAL_SLOTS
    nxt_slot = (i + 2) % FINAL_SLOTS
    nxt_tile = jnp.minimum(i + 2, n - 1)

    def issue(tile, dst_slot, j):
        pair = TOP_K * (tile * tm + j)
        for k in range(TOP_K):
            src_row = pl.multiple_of(pos_ref[pair + k], SUBLANES)
            dst_row = pl.multiple_of((k * tm + j) * SUBLANES, SUBLANES)
            pltpu.make_async_copy(ys_hbm.at[pl.ds(src_row, SUBLANES)],
                                  buf.at[dst_slot, pl.ds(dst_row, SUBLANES)], sem.at[dst_slot]).start(priority=k)

    def wait(s):
        pltpu.make_async_copy(ys_hbm.at[pl.ds(0, TOP_K * tm * SUBLANES)], buf.at[s], sem.at[s]).wait()

    @pl.when(i == 0)
    def _():
        def tok(j, carry):
            issue(0, 0, j)
            issue(jnp.minimum(1, n - 1), 1, j)
            return carry
        lax.fori_loop(0, tm, tok, 0, unroll=8)

    wait(slot)
    gate2 = mod_ref[0, 5:6, :]
    cur = buf.at[slot]

    def chunk(c, carry):
        r0 = pl.multiple_of(c * FINAL_CHUNK, FINAL_CHUNK)
        rows = pl.ds(r0, FINAL_CHUNK)
        a_lo, a_hi = _unpack_halves(_load_token_tiles(cur, r0, FINAL_CHUNK))
        b_lo, b_hi = _unpack_halves(_load_token_tiles(cur, tm + r0, FINAL_CHUNK))
        x1 = x1_ref[rows, :]
        rt = rt_ref[rows, :]
        for r in range(FINAL_CHUNK):
            issue(nxt_tile, nxt_slot, r0 + r)
        w0 = rt[:, 2:3]
        w1 = rt[:, 3:4]
        y = jnp.concatenate([w0 * a_lo + w1 * b_lo, w0 * a_hi + w1 * b_hi], axis=1)
        o_ref[rows, :] = _ln_rows(DEEPNORM_ALPHA * x1 + gate2 * y) * g2_ref[...] + b2_ref[...]
        return carry

    lax.fori_loop(0, tm // FINAL_CHUNK, chunk, 0)

    @pl.when(i == n - 1)
    def _():
        wait((i + 1) % FINAL_SLOTS)
        wait(nxt_slot)


def _final(pos_rows, x1, mod3, route, g2, b2, ys):
    tm = FINAL_TM
    tpb = SEQ // tm
    row = lambda w: pl.BlockSpec((tm, w), lambda i, *_: (i, 0))
    grid_spec = pltpu.PrefetchScalarGridSpec(
        num_scalar_prefetch=1,
        grid=(N_TOK // tm,),
        in_specs=[row(D_MODEL),
                  pl.BlockSpec((1, N_MOD, D_MODEL), lambda i, *_: (i // tpb, 0, 0)),
                  row(LANES),
                  pl.BlockSpec(g2.shape, lambda i, *_: (0, 0)),
                  pl.BlockSpec(b2.shape, lambda i, *_: (0, 0)),
                  pl.BlockSpec(memory_space=pl.ANY)],
        out_specs=row(D_MODEL),
        scratch_shapes=[pltpu.VMEM((FINAL_SLOTS, TOP_K * tm * SUBLANES, LANES), U32),
                        pltpu.SemaphoreType.DMA((FINAL_SLOTS,))],
    )
    return pl.pallas_call(
        _final_body,
        grid_spec=grid_spec,
        out_shape=jax.ShapeDtypeStruct((N_TOK, D_MODEL), F32),
        compiler_params=_cparams(("arbitrary",)),
        name="final",
    )(pos_rows, x1, mod3, route, g2, b2, ys)


def kernel(x, c, positions, w_ada, b_ada, w_in, q_norm_g, w_uq, kv_norm_g, w_ukv, sgu_norm_g, sgu_norm_b,
           w_spatial, b_spatial, w_o, ln1_g, ln1_b, w_router_group, b_router_group, w_router_expert,
           b_router_expert, w_gate, w_up, w_down, ln2_g, ln2_b):
    l = 0
    x2 = x.reshape(N_TOK, D_MODEL)
    mod3 = _ada(c, w_ada[l], b_ada[l][None, :]).reshape(BATCH, N_MOD, D_MODEL)

    wt, wuq, wukv, wo = _prep(w_in[l].T, w_uq[l], w_ukv[l], w_o[l])
    n_r = N_GROUPS + N_EXPERTS
    wr = jnp.pad(jnp.concatenate([w_router_group[l], w_router_expert[l]], axis=1),
                 ((0, 0), (0, LANES - n_r))).astype(BF16)
    br = jnp.pad(jnp.concatenate([b_router_group[l], b_router_expert[l]]), (0, LANES - n_r))[None, :]
    inv_freq = 1.0 / (ROPE_THETA ** (jnp.arange(0, QK_ROPE_DIM, 2, dtype=F32) / QK_ROPE_DIM))
    invf = jnp.tile(inv_freq, 2 * LANES // QK_ROPE_DIM)[None, :]

    cqn, ckvn, kpe, u, vs = _inproj(x2, mod3, wt, q_norm_g[l][None, :], kv_norm_g[l][None, :],
                                    sgu_norm_g[l][None, :], sgu_norm_b[l][None, :])
    q, k, v = _qkv(cqn, ckvn, kpe, positions.reshape(N_TOK, 1), invf, wuq, wukv)
    attn = _attn(q, k, v).reshape(N_TOK, MLA_WIDTH)
    x1, h2, logits = _mixout(x2, mod3, attn, u, vs, w_spatial[l], b_spatial[l].T, wo,
                             ln1_g[l][None, :], ln1_b[l][None, :], wr, br)
    route, counts = _route(logits)
    pos_tab, tile_tab = _plan(route, counts)
    pos_rows = pos_tab[:, 0:TOP_K].reshape(-1)
    xs = _dispatch(pos_rows, tile_tab, h2)
    ys = _moe(tile_tab, xs, w_gate[l], w_up[l], w_down[l])
    out = _final(pos_rows, x1, mod3, route, ln2_g[l][None, :], ln2_b[l][None, :], ys)
    return out.reshape(BATCH, SEQ, D_MODEL)
```

```python
import jax
import jax.numpy as jnp
import numpy as np
from jax import lax
from jax.experimental import pallas as pl
from jax.experimental.pallas import tpu as pltpu

D_MODEL = 2048
BATCH = 4
SEQ = 2048
N_TOK = BATCH * SEQ

MLA_HEADS = 8
QK_NOPE_DIM = 128
QK_ROPE_DIM = 64
QK_DIM = QK_NOPE_DIM + QK_ROPE_DIM
V_HEAD_DIM = 128
Q_LORA_RANK = 768
KV_LORA_RANK = 512
ROPE_THETA = 10000.0
MLA_WIDTH = MLA_HEADS * V_HEAD_DIM

SGU_GROUPS = 8
SGU_GROUP_DIM = 128
SGU_CHUNK = 128
SGU_WIDTH = SGU_GROUPS * SGU_GROUP_DIM

N_GROUPS = 4
EXPERTS_PER_GROUP = 8
N_EXPERTS = N_GROUPS * EXPERTS_PER_GROUP
TOP_K = 2
EXPERT_FF = 512

DEEPNORM_ALPHA = 2.0 ** 0.25
EPS = 1e-6
N_MOD = 6
NEG_BIG = -1e30

LANES = 128
SUBLANES = 8
VMEM_LIMIT = 56 * 1024 * 1024

ADA_TN = 1024
TOK_TM = 512
MIX_TM = 256
PREP_STEPS = 4
ATT_TQ = 1024
ATT_TK = 1024
ATT_HEADS = 4
MOE_TM = 256
MOE_TILES = (N_TOK * TOP_K + N_EXPERTS * (MOE_TM - 1)) // MOE_TM + 1
MOE_ROWS = MOE_TILES * MOE_TM
PLAN_TM = 2048
ROUTE_TM = 1024
RT_RANK = 4
DISPATCH_TM = 2048
FINAL_TM = 512
FINAL_CHUNK = 128
FINAL_SLOTS = 3
assert MOE_TILES <= LANES
T_EXP, T_VALID, T_FIRST, T_CLEAR, T_NEXT, T_NEXT2, T_NEXT3, T_ORD, T_LAST = range(9)
MOE_SLOTS = 2

F32 = jnp.float32
BF16 = jnp.bfloat16
U32 = jnp.uint32
HALF_D = D_MODEL // 2


def _cparams(sem):
    return pltpu.CompilerParams(dimension_semantics=sem, vmem_limit_bytes=VMEM_LIMIT)


def _const_spec(shape):
    nd = len(shape)
    return pl.BlockSpec(shape, lambda *_: (0,) * nd, pipeline_mode=pl.Buffered(1))


def _ln_rows(x):
    mu = jnp.mean(x, axis=-1, keepdims=True)
    xc = x - mu
    var = jnp.mean(xc * xc, axis=-1, keepdims=True)
    return xc * lax.rsqrt(var + EPS)


def _rms_rows(x):
    return x * lax.rsqrt(jnp.mean(x * x, axis=-1, keepdims=True) + EPS)


def _pack_halves(x):
    half = x.shape[-1] // 2
    return pltpu.pack_elementwise([x[:, :half], x[:, half:]], packed_dtype=BF16)


def _unpack_halves(w):
    lo = pltpu.unpack_elementwise(w, index=0, packed_dtype=BF16, unpacked_dtype=F32)
    hi = pltpu.unpack_elementwise(w, index=1, packed_dtype=BF16, unpacked_dtype=F32)
    return lo, hi


def _store_token_tiles(ref, w):
    rows = w.shape[0]
    for s in range(SUBLANES):
        ref[pl.ds(s, rows, stride=SUBLANES), :] = w[:, s * LANES:(s + 1) * LANES]


def _load_token_tiles(ref, start_row, rows):
    return jnp.concatenate([ref[pl.ds(start_row * SUBLANES + s, rows, stride=SUBLANES), :]
                            for s in range(SUBLANES)], axis=1)


def _gelu_tanh(x):
    c = np.sqrt(2.0 / np.pi).astype(np.float32)
    return 0.5 * x * (1.0 + jnp.tanh(c * (x + 0.044715 * (x * x * x))))


def _ada_body(c_ref, w_ref, b_ref, o_ref):
    o_ref[...] = jnp.dot(c_ref[...].astype(BF16), w_ref[...].astype(BF16),
                         preferred_element_type=F32) + b_ref[...]


def _ada(c, w, b):
    n = w.shape[1]
    return pl.pallas_call(
        _ada_body,
        grid=(n // ADA_TN,),
        in_specs=[pl.BlockSpec((BATCH, D_MODEL), lambda j: (0, 0)),
                  pl.BlockSpec((D_MODEL, ADA_TN), lambda j: (0, j)),
                  pl.BlockSpec((1, ADA_TN), lambda j: (0, j))],
        out_specs=pl.BlockSpec((BATCH, ADA_TN), lambda j: (0, j)),
        out_shape=jax.ShapeDtypeStruct((BATCH, n), F32),
        compiler_params=_cparams(("parallel",)),
        name="ada",
    )(c, w, b)


def _prep_body(win_ref, wuq_ref, wukv_ref, wo_ref, win_o, wuq_o, wukv_o, wo_o):
    win_o[...] = win_ref[...].astype(BF16)
    u = wuq_ref[...]
    nope = [u[:, h * QK_DIM:h * QK_DIM + QK_NOPE_DIM] for h in range(MLA_HEADS)]
    rope = [u[:, h * QK_DIM + QK_NOPE_DIM:(h + 1) * QK_DIM] for h in range(MLA_HEADS)]
    wuq_o[...] = jnp.concatenate(nope + rope, axis=1).astype(BF16)
    kv = wukv_ref[...]
    hw = QK_NOPE_DIM + V_HEAD_DIM
    kn = [kv[:, h * hw:h * hw + QK_NOPE_DIM] for h in range(MLA_HEADS)]
    vv = [kv[:, h * hw + QK_NOPE_DIM:(h + 1) * hw] for h in range(MLA_HEADS)]
    wukv_o[...] = jnp.concatenate(kn + vv, axis=1).astype(BF16)
    wo_o[...] = wo_ref[...].astype(BF16)


def _prep(w_in_t, w_uq, w_ukv, w_o):
    steps = PREP_STEPS
    blk = lambda a: pl.BlockSpec((a.shape[0] // steps, a.shape[1]), lambda i: (i, 0))
    ins = (w_in_t, w_uq, w_ukv, w_o)
    return pl.pallas_call(
        _prep_body,
        grid=(steps,),
        in_specs=[blk(a) for a in ins],
        out_specs=[blk(a) for a in ins],
        out_shape=[jax.ShapeDtypeStruct(a.shape, BF16) for a in ins],
        compiler_params=_cparams(("parallel",)),
        name="prep",
    )(*ins)


def _inproj_body(x_ref, mod_ref, wt_ref, gq_ref, gkv_ref, sg_ref, sb_ref,
                 cq_ref, ckv_ref, kpe_ref, u_ref, vs_ref):
    o1, o2, o3 = Q_LORA_RANK, Q_LORA_RANK + KV_LORA_RANK, Q_LORA_RANK + KV_LORA_RANK + QK_ROPE_DIM

    def proj(lo, hi):
        return lax.dot_general(h, wt_ref[lo:hi, :], (((1,), (1,)), ((), ())), preferred_element_type=F32)

    sh = mod_ref[0, 0:1, :]
    sc = mod_ref[0, 1:2, :]
    h = (_ln_rows(x_ref[...]) * (1.0 + sc) + sh).astype(BF16)
    cq_ref[...] = (_rms_rows(proj(0, o1)) * gq_ref[...]).astype(BF16)
    ckv_ref[...] = (_rms_rows(proj(o1, o2)) * gkv_ref[...]).astype(BF16)
    kpe_ref[...] = proj(o2, o2 + LANES)
    gz = _gelu_tanh(proj(o3, o3 + 2 * SGU_WIDTH))
    u_ref[...] = gz[:, :SGU_WIDTH]
    vs_ref[...] = (_ln_rows(gz[:, SGU_WIDTH:]) * sg_ref[...] + sb_ref[...]).astype(BF16)


def _inproj(x2, mod3, wt, gq, gkv, sg, sb):
    tm = TOK_TM
    tiles_per_batch = SEQ // tm
    row = lambda w: pl.BlockSpec((tm, w), lambda i: (i, 0))
    return pl.pallas_call(
        _inproj_body,
        grid=(N_TOK // tm,),
        in_specs=[row(D_MODEL),
                  pl.BlockSpec((1, N_MOD, D_MODEL), lambda i: (i // tiles_per_batch, 0, 0)),
                  _const_spec(wt.shape),
                  _const_spec(gq.shape), _const_spec(gkv.shape), _const_spec(sg.shape), _const_spec(sb.shape)],
        out_specs=[row(Q_LORA_RANK), row(KV_LORA_RANK), row(LANES), row(SGU_WIDTH), row(SGU_WIDTH)],
        out_shape=[jax.ShapeDtypeStruct((N_TOK, Q_LORA_RANK), BF16),
                   jax.ShapeDtypeStruct((N_TOK, KV_LORA_RANK), BF16),
                   jax.ShapeDtypeStruct((N_TOK, LANES), F32),
                   jax.ShapeDtypeStruct((N_TOK, SGU_WIDTH), F32),
                   jax.ShapeDtypeStruct((N_TOK, SGU_WIDTH), BF16)],
        compiler_params=_cparams(("parallel",)),
        name="inproj",
    )(x2, mod3, wt, gq, gkv, sg, sb)


def _rope(x, cos, sin):
    w = x.shape[-1]
    lane = lax.broadcasted_iota(jnp.int32, x.shape, 1)
    first_half = (lane % QK_ROPE_DIM) < (QK_ROPE_DIM // 2)
    rot = jnp.where(first_half,
                    -pltpu.roll(x, w - QK_ROPE_DIM // 2, 1),
                    pltpu.roll(x, QK_ROPE_DIM // 2, 1))
    return x * cos + rot * sin


def _qkv_body(cq_ref, ckv_ref, kpe_ref, pos_ref, invf_ref, wuq_ref, wukv_ref, q_ref, k_ref, v_ref):
    ang = pos_ref[...].astype(F32) * invf_ref[...]
    cos1 = jnp.cos(ang)
    sin1 = jnp.sin(ang)
    reps = MLA_HEADS * QK_ROPE_DIM // LANES
    cos = jnp.concatenate([cos1] * reps, axis=1)
    sin = jnp.concatenate([sin1] * reps, axis=1)
    scale = np.float32(QK_DIM ** -0.5)
    q = jnp.dot(cq_ref[...], wuq_ref[...], preferred_element_type=F32) * scale
    q_pe = _rope(q[:, MLA_HEADS * QK_NOPE_DIM:], cos, sin)
    kv = jnp.dot(ckv_ref[...], wukv_ref[...], preferred_element_type=F32)
    k_pe = _rope(kpe_ref[...], cos1, sin1)[:, :QK_ROPE_DIM].astype(BF16)
    for h in range(MLA_HEADS):
        q_ref[0, h, :, 0:QK_NOPE_DIM] = q[:, h * QK_NOPE_DIM:(h + 1) * QK_NOPE_DIM].astype(BF16)
        q_ref[0, h, :, QK_NOPE_DIM:QK_DIM] = q_pe[:, h * QK_ROPE_DIM:(h + 1) * QK_ROPE_DIM].astype(BF16)
        k_ref[0, h, :, 0:QK_NOPE_DIM] = kv[:, h * QK_NOPE_DIM:(h + 1) * QK_NOPE_DIM].astype(BF16)
        k_ref[0, h, :, QK_NOPE_DIM:QK_DIM] = k_pe
        v_ref[0, h, :, :] = kv[:, MLA_WIDTH + h * V_HEAD_DIM:MLA_WIDTH + (h + 1) * V_HEAD_DIM].astype(BF16)


def _qkv(cqn, ckvn, kpe, pos2, invf, wuq, wukv):
    tm = TOK_TM
    tpb = SEQ // tm
    row = lambda w: pl.BlockSpec((tm, w), lambda i: (i, 0))
    head_out = lambda w: pl.BlockSpec((1, MLA_HEADS, tm, w), lambda i: (i // tpb, 0, i % tpb, 0))
    return pl.pallas_call(
        _qkv_body,
        grid=(N_TOK // tm,),
        in_specs=[row(Q_LORA_RANK), row(KV_LORA_RANK), row(LANES), row(1),
                  _const_spec(invf.shape), _const_spec(wuq.shape), _const_spec(wukv.shape)],
        out_specs=[head_out(QK_DIM), head_out(QK_DIM), head_out(V_HEAD_DIM)],
        out_shape=[jax.ShapeDtypeStruct((BATCH, MLA_HEADS, SEQ, QK_DIM), BF16),
                   jax.ShapeDtypeStruct((BATCH, MLA_HEADS, SEQ, QK_DIM), BF16),
                   jax.ShapeDtypeStruct((BATCH, MLA_HEADS, SEQ, V_HEAD_DIM), BF16)],
        compiler_params=_cparams(("parallel",)),
        name="qkv",
    )(cqn, ckvn, kpe, pos2, invf, wuq, wukv)


def _attn_body(q_ref, k_ref, v_ref, o_ref):
    i = pl.program_id(2)

    def step(h, j, carry, masked):
        m, l, acc = carry
        start = pl.multiple_of(j * ATT_TK, ATT_TK)
        k = k_ref[0, h, pl.ds(start, ATT_TK), :]
        v = v_ref[0, h, pl.ds(start, ATT_TK), :]
        s = lax.dot_general(q_ref[0, h], k, (((1,), (1,)), ((), ())), preferred_element_type=F32)
        if masked:
            r = lax.broadcasted_iota(jnp.int32, s.shape, 0)
            c = lax.broadcasted_iota(jnp.int32, s.shape, 1)
            s = jnp.where(c <= r, s, NEG_BIG)
        m_new = jnp.maximum(m, jnp.max(s, axis=-1, keepdims=True))
        p = jnp.exp(s - m_new)
        a = jnp.exp(m - m_new)
        l = a * l + jnp.sum(p, axis=-1, keepdims=True)
        acc = a * acc + jnp.dot(p.astype(BF16), v, preferred_element_type=F32)
        return m_new, l, acc

    def steps(j, carries, masked):
        return tuple(step(h, j, carries[h], masked) for h in range(ATT_HEADS))

    init = tuple((jnp.full((ATT_TQ, 1), NEG_BIG, F32), jnp.zeros((ATT_TQ, 1), F32),
                  jnp.zeros((ATT_TQ, V_HEAD_DIM), F32)) for _ in range(ATT_HEADS))
    carries = lax.fori_loop(0, i, lambda j, c: steps(j, c, False), init)
    carries = steps(i, carries, True)
    for h, (m, l, acc) in enumerate(carries):
        o_ref[0, :, h * V_HEAD_DIM:(h + 1) * V_HEAD_DIM] = (acc / l).astype(BF16)


def _attn(q, k, v):
    assert ATT_TQ == ATT_TK
    hb = ATT_HEADS
    return pl.pallas_call(
        _attn_body,
        grid=(BATCH, MLA_HEADS // hb, SEQ // ATT_TQ),
        in_specs=[pl.BlockSpec((1, hb, ATT_TQ, QK_DIM), lambda b, h, i: (b, h, i, 0)),
                  pl.BlockSpec((1, hb, SEQ, QK_DIM), lambda b, h, i: (b, h, 0, 0)),
                  pl.BlockSpec((1, hb, SEQ, V_HEAD_DIM), lambda b, h, i: (b, h, 0, 0))],
        out_specs=pl.BlockSpec((1, ATT_TQ, hb * V_HEAD_DIM), lambda b, h, i: (b, i, h)),
        out_shape=jax.ShapeDtypeStruct((BATCH, SEQ, MLA_WIDTH), BF16),
        compiler_params=_cparams(("parallel", "parallel", "arbitrary")),
        name="attn",
    )(q, k, v)


def _mixout_body(x_ref, mod_ref, attn_ref, u_ref, vs_ref, wsp_ref, bsp_ref, woa_ref, wos_ref,
                 g1_ref, b1_ref, wr_ref, br_ref, x1_ref, h2_ref, lg_ref, sgu_scr):
    r = lax.broadcasted_iota(jnp.int32, (SGU_CHUNK, SGU_CHUNK), 0)
    c = lax.broadcasted_iota(jnp.int32, (SGU_CHUNK, SGU_CHUNK), 1)
    causal = c <= r
    for g in range(SGU_GROUPS):
        ws = jnp.where(causal, wsp_ref[g], 0.0).astype(BF16)
        bias = bsp_ref[:, g:g + 1]
        cols = slice(g * SGU_GROUP_DIM, (g + 1) * SGU_GROUP_DIM)
        for ch in range(MIX_TM // SGU_CHUNK):
            rows = slice(ch * SGU_CHUNK, (ch + 1) * SGU_CHUNK)
            mixed = jnp.dot(ws, vs_ref[rows, cols], preferred_element_type=F32) + bias
            sgu_scr[rows, cols] = (u_ref[rows, cols] * mixed).astype(BF16)
    y = (jnp.dot(attn_ref[...], woa_ref[...], preferred_element_type=F32)
         + jnp.dot(sgu_scr[...], wos_ref[...], preferred_element_type=F32))
    gate1 = mod_ref[0, 2:3, :]
    sh2 = mod_ref[0, 3:4, :]
    sc2 = mod_ref[0, 4:5, :]
    x1 = _ln_rows(DEEPNORM_ALPHA * x_ref[...] + gate1 * y) * g1_ref[...] + b1_ref[...]
    x1_ref[...] = x1
    h2 = _ln_rows(x1) * (1.0 + sc2) + sh2
    _store_token_tiles(h2_ref, _pack_halves(h2))
    lg_ref[...] = jnp.dot(h2.astype(BF16), wr_ref[...], preferred_element_type=F32) + br_ref[...]


def _mixout(x2, mod3, attn, u, vs, wsp, bsp_t, wo, g1, b1, wr, br):
    tm = MIX_TM
    tpb = SEQ // tm
    row = lambda w: pl.BlockSpec((tm, w), lambda i: (i, 0))
    wo_half = lambda j: pl.BlockSpec((MLA_WIDTH, D_MODEL), lambda i: (j, 0), pipeline_mode=pl.Buffered(1))
    return pl.pallas_call(
        _mixout_body,
        grid=(N_TOK // tm,),
        in_specs=[row(D_MODEL),
                  pl.BlockSpec((1, N_MOD, D_MODEL), lambda i: (i // tpb, 0, 0)),
                  row(MLA_WIDTH), row(SGU_WIDTH), row(SGU_WIDTH),
                  _const_spec(wsp.shape), _const_spec(bsp_t.shape), wo_half(0), wo_half(1),
                  _const_spec(g1.shape), _const_spec(b1.shape), _const_spec(wr.shape), _const_spec(br.shape)],
        out_specs=[row(D_MODEL), pl.BlockSpec((tm * SUBLANES, LANES), lambda i: (i, 0)), row(LANES)],
        out_shape=[jax.ShapeDtypeStruct((N_TOK, D_MODEL), F32),
                   jax.ShapeDtypeStruct((N_TOK * SUBLANES, LANES), U32),
                   jax.ShapeDtypeStruct((N_TOK, LANES), F32)],
        scratch_shapes=[pltpu.VMEM((tm, SGU_WIDTH), BF16)],
        compiler_params=_cparams(("parallel",)),
        name="mix_out",
    )(x2, mod3, attn, u, vs, wsp, bsp_t, wo, wo, g1, b1, wr, br)


def _route_math(lg):
    lane = lax.broadcasted_iota(jnp.int32, lg.shape, 1)
    big = jnp.int32(LANES)

    def top1(vals):
        m = jnp.max(vals, axis=-1, keepdims=True)
        idx = jnp.min(jnp.where(vals == m, lane, big), axis=-1, keepdims=True)
        return m, idx

    is_group = lane < N_GROUPS
    glog = jnp.where(is_group, lg, -jnp.inf)
    gmax, gidx = top1(glog)
    pg_top = 1.0 / jnp.sum(jnp.exp(glog - gmax), axis=-1, keepdims=True)
    eid = lane - N_GROUPS
    sel = (eid >= gidx * EXPERTS_PER_GROUP) & (eid < (gidx + 1) * EXPERTS_PER_GROUP)
    elog = jnp.where(sel, lg, -jnp.inf)
    m1, i1 = top1(elog)
    m2, i2 = top1(jnp.where(lane == i1, -jnp.inf, elog))
    e2 = jnp.exp(m2 - m1)
    w1 = pg_top / (1.0 + e2)
    w2 = pg_top * e2 / (1.0 + e2)
    return jnp.where(lane == 0, (i1 - N_GROUPS).astype(F32),
                     jnp.where(lane == 1, (i2 - N_GROUPS).astype(F32),
                               jnp.where(lane == 2, w1, jnp.where(lane == 3, w2, 0.0))))


def _rank_math(rt, counts):
    t = rt.shape[0]
    lane = lax.broadcasted_iota(jnp.int32, (t, LANES), 1).astype(F32)
    oh0 = lane == rt[:, 0:1]
    oh1 = lane == rt[:, 1:2]
    s = jnp.where(oh0 | oh1, 1.0, 0.0)
    r = lax.broadcasted_iota(jnp.int32, (t, t), 0)
    c = lax.broadcasted_iota(jnp.int32, (t, t), 1)
    before = jnp.where(c < r, 1.0, 0.0).astype(BF16)
    csum = jnp.dot(before, s.astype(BF16), preferred_element_type=F32) + counts
    rank0 = jnp.sum(jnp.where(oh0, csum, 0.0), axis=-1, keepdims=True)
    rank1 = jnp.sum(jnp.where(oh1, csum, 0.0), axis=-1, keepdims=True)
    return rank0, rank1, counts + jnp.sum(s, axis=0, keepdims=True)


def _route_body(lg_ref, rt_ref, cnt_ref, cnt_scr):
    @pl.when(pl.program_id(0) == 0)
    def _():
        cnt_scr[...] = jnp.zeros_like(cnt_scr)

    rt = _route_math(lg_ref[...])
    rank0, rank1, counts = _rank_math(rt, cnt_scr[...])
    cnt_scr[...] = counts
    lane = lax.broadcasted_iota(jnp.int32, rt.shape, 1)
    rt_ref[...] = jnp.where(lane == RT_RANK, rank0, jnp.where(lane == RT_RANK + 1, rank1, rt))
    cnt_ref[...] = jnp.broadcast_to(counts, cnt_ref.shape)


def _route(logits):
    tm = ROUTE_TM
    return pl.pallas_call(
        _route_body,
        grid=(N_TOK // tm,),
        in_specs=[pl.BlockSpec((tm, LANES), lambda i: (i, 0))],
        out_specs=[pl.BlockSpec((tm, LANES), lambda i: (i, 0)),
                   pl.BlockSpec((SUBLANES, LANES), lambda i: (0, 0))],
        out_shape=[jax.ShapeDtypeStruct((N_TOK, LANES), F32),
                   jax.ShapeDtypeStruct((SUBLANES, LANES), F32)],
        scratch_shapes=[pltpu.VMEM((1, LANES), F32)],
        compiler_params=_cparams(("arbitrary",)),
        name="route",
    )(logits)


def _plan_body(rt_ref, cnt_ref, pos_ref, tt_ref):
    t = PLAN_TM
    lane = lax.broadcasted_iota(jnp.int32, (t, LANES), 1)
    rt = rt_ref[...]
    oh0 = lane.astype(F32) == rt[:, 0:1]
    oh1 = lane.astype(F32) == rt[:, 1:2]
    counts = cnt_ref[0:1, :]
    tiles = jnp.floor((counts + (MOE_TM - 1)) * (1.0 / MOE_TM))
    r = lax.broadcasted_iota(jnp.int32, (LANES, LANES), 0)
    c = lax.broadcasted_iota(jnp.int32, (LANES, LANES), 1)
    upto = jnp.where(r <= c, 1.0, 0.0).astype(BF16)
    tiles8 = jnp.broadcast_to(tiles, (SUBLANES, LANES)).astype(BF16)
    tile_end = jnp.dot(tiles8, upto, preferred_element_type=F32)[0:1]
    offs = (tile_end - tiles) * MOE_TM
    p0 = jnp.sum(jnp.where(oh0, offs, 0.0), axis=-1, keepdims=True) + rt[:, RT_RANK:RT_RANK + 1]
    p1 = jnp.sum(jnp.where(oh1, offs, 0.0), axis=-1, keepdims=True) + rt[:, RT_RANK + 1:RT_RANK + 2]
    pos_ref[...] = (jnp.where(lane == 0, p0, jnp.where(lane == 1, p1, 0.0)) * SUBLANES).astype(jnp.int32)
    tt_ref[...] = _tile_table(tiles, tile_end)


def _tile_table(tiles, tile_end):
    lane_e = lax.broadcasted_iota(jnp.int32, (LANES, LANES), 1)
    lane_f = lane_e.astype(F32)
    tile_id = lax.broadcasted_iota(jnp.int32, (LANES, LANES), 0).astype(F32)
    is_e = lane_e < N_EXPERTS
    owns = is_e & (tiles > 0)

    def count(cond):
        return jnp.sum(jnp.where(cond, 1.0, 0.0), axis=-1, keepdims=True)

    def first_owner_after(e):
        return jnp.min(jnp.where(owns & (lane_f > e), lane_f, jnp.float32(LANES)), axis=-1, keepdims=True)

    total = jnp.max(tile_end, axis=-1, keepdims=True)
    t_valid = jnp.where(tile_id[:, 0:1] < total, 1.0, 0.0)
    t_exp = jnp.where(t_valid > 0, count(is_e & (tile_end <= tile_id)), count(is_e & (tile_end <= total - 1.0)))
    t_first = count(owns & ((tile_end - tiles) == tile_id))
    t_clear = jnp.maximum(count(owns & ((tile_end - 1.0) == tile_id)), 1.0 - t_valid)
    t_next = first_owner_after(t_exp)
    t_next2 = first_owner_after(t_next)
    t_next3 = first_owner_after(t_next2)
    cols = {T_EXP: t_exp, T_VALID: t_valid, T_FIRST: t_first, T_CLEAR: t_clear, T_NEXT: t_next,
            T_NEXT2: t_next2, T_NEXT3: t_next3, T_ORD: count(owns & (lane_f < t_exp)), T_LAST: total - 1.0}
    table = jnp.zeros((LANES, LANES), F32)
    for k, col in cols.items():
        table = jnp.where(lane_e == k, col, table)
    return table.astype(jnp.int32)


def _plan(route, counts):
    t = PLAN_TM
    return pl.pallas_call(
        _plan_body,
        grid=(N_TOK // t,),
        in_specs=[pl.BlockSpec((t, LANES), lambda i: (i, 0)),
                  pl.BlockSpec((SUBLANES, LANES), lambda i: (0, 0))],
        out_specs=[pl.BlockSpec((t, LANES), lambda i: (i, 0)),
                   pl.BlockSpec((LANES, LANES), lambda i: (0, 0))],
        out_shape=[jax.ShapeDtypeStruct((N_TOK, LANES), jnp.int32),
                   jax.ShapeDtypeStruct((LANES, LANES), jnp.int32)],
        compiler_params=_cparams(("arbitrary",)),
        name="plan",
    )(route, counts)


def _rows_wait(ref, n_rows, sem):
    pltpu.make_async_copy(ref.at[pl.ds(0, n_rows)], ref.at[pl.ds(0, n_rows)], sem).wait()


def _dispatch_body(pos_ref, tt_ref, h_ref, xs_hbm, zbuf, sem_z, sem_s):
    i = pl.program_id(0)
    tile_rows = MOE_TM * SUBLANES

    @pl.when(i == 0)
    def _():
        zbuf[...] = _pack_halves(jnp.zeros((tile_rows, 2 * LANES), F32))

        def zero_copy(tile):
            start = pl.multiple_of(tile * tile_rows, tile_rows)
            return pltpu.make_async_copy(zbuf, xs_hbm.at[pl.ds(start, tile_rows)], sem_z)

        def clear_start(tile, carry):
            @pl.when(tt_ref[tile, T_CLEAR] > 0)
            def _():
                zero_copy(tile).start()
            return carry

        def clear_wait(tile, carry):
            @pl.when(tt_ref[tile, T_CLEAR] > 0)
            def _():
                zero_copy(tile).wait()
            return carry

        lax.fori_loop(0, MOE_TILES, clear_start, 0)
        lax.fori_loop(0, MOE_TILES, clear_wait, 0)

    def tok(j, carry):
        src = h_ref.at[pl.ds(pl.multiple_of(j * SUBLANES, SUBLANES), SUBLANES)]
        pair = TOP_K * (i * DISPATCH_TM + j)
        for k in range(TOP_K):
            dst_row = pl.multiple_of(pos_ref[pair + k], SUBLANES)
            pltpu.make_async_copy(src, xs_hbm.at[pl.ds(dst_row, SUBLANES)], sem_s).start(priority=k)
        return carry

    lax.fori_loop(0, DISPATCH_TM, tok, 0, unroll=8)
    _rows_wait(xs_hbm, TOP_K * DISPATCH_TM * SUBLANES, sem_s)


def _dispatch(pos_rows, tile_clear, h2p):
    grid_spec = pltpu.PrefetchScalarGridSpec(
        num_scalar_prefetch=2,
        grid=(N_TOK // DISPATCH_TM,),
        in_specs=[pl.BlockSpec((DISPATCH_TM * SUBLANES, LANES), lambda i, *_: (i, 0))],
        out_specs=pl.BlockSpec(memory_space=pl.ANY),
        scratch_shapes=[pltpu.VMEM((MOE_TM * SUBLANES, LANES), U32),
                        pltpu.SemaphoreType.DMA(()), pltpu.SemaphoreType.DMA(())],
    )
    return pl.pallas_call(
        _dispatch_body,
        grid_spec=grid_spec,
        out_shape=jax.ShapeDtypeStruct((MOE_ROWS * SUBLANES, LANES), U32),
        compiler_params=_cparams(("arbitrary",)),
        name="dispatch",
    )(pos_rows, tile_clear, h2p)


def _moe_body(tt_ref, x_ref, wg_hbm, wu_hbm, wd_hbm, y_ref,
              wg_s, wu_s, wd_s, stg_g, stg_u, stg_d, sem):
    i = pl.program_id(0)
    ahead = (T_EXP, T_NEXT, T_NEXT2, T_NEXT3)

    def fetch(e, slot):
        return (pltpu.make_async_copy(wg_hbm.at[e], stg_g.at[slot], sem.at[slot, 0]),
                pltpu.make_async_copy(wu_hbm.at[e], stg_u.at[slot], sem.at[slot, 1]),
                pltpu.make_async_copy(wd_hbm.at[e], stg_d.at[slot], sem.at[slot, 2]))

    @pl.when(i == 0)
    def _():
        for d in range(MOE_SLOTS):
            @pl.when(tt_ref[0, ahead[d]] < N_EXPERTS)
            def _():
                for cp in fetch(tt_ref[0, ahead[d]], d):
                    cp.start()

    @pl.when(tt_ref[i, T_FIRST] > 0)
    def _():
        slot = tt_ref[i, T_ORD] % MOE_SLOTS
        for cp in fetch(tt_ref[i, T_EXP], slot):
            cp.wait()
        wg_s[...] = stg_g[slot].astype(BF16)
        wu_s[...] = stg_u[slot].astype(BF16)
        wd_s[...] = stg_d[slot].astype(BF16)

        @pl.when(tt_ref[i, ahead[MOE_SLOTS]] < N_EXPERTS)
        def _():
            for cp in fetch(tt_ref[i, ahead[MOE_SLOTS]], slot):
                cp.start()

    @pl.when(tt_ref[i, T_VALID] > 0)
    def _():
        lo, hi = _unpack_halves(_load_token_tiles(x_ref, 0, MOE_TM))
        xa = lo.astype(BF16)
        xb = hi.astype(BF16)
        g = (jnp.dot(xa, wg_s[:HALF_D, :], preferred_element_type=F32)
             + jnp.dot(xb, wg_s[HALF_D:, :], preferred_element_type=F32))
        u = (jnp.dot(xa, wu_s[:HALF_D, :], preferred_element_type=F32)
             + jnp.dot(xb, wu_s[HALF_D:, :], preferred_element_type=F32))
        hid = (g * jax.nn.sigmoid(g) * u).astype(BF16)
        _store_token_tiles(y_ref, _pack_halves(jnp.dot(hid, wd_s[...], preferred_element_type=F32)))

    @pl.when(tt_ref[i, T_VALID] == 0)
    def _():
        y_ref[...] = _pack_halves(jnp.zeros((MOE_TM * SUBLANES, 2 * LANES), F32))


def _moe(tile_tab, xs, wg, wu, wd):
    tm = MOE_TM
    grid_spec = pltpu.PrefetchScalarGridSpec(
        num_scalar_prefetch=1,
        grid=(MOE_TILES,),
        in_specs=[pl.BlockSpec((tm * SUBLANES, LANES), lambda i, tt: (jnp.minimum(i, tt[0, T_LAST]), 0)),
                  pl.BlockSpec(memory_space=pl.ANY), pl.BlockSpec(memory_space=pl.ANY),
                  pl.BlockSpec(memory_space=pl.ANY)],
        out_specs=pl.BlockSpec((tm * SUBLANES, LANES), lambda i, *_: (i, 0)),
        scratch_shapes=[pltpu.VMEM((D_MODEL, EXPERT_FF), BF16), pltpu.VMEM((D_MODEL, EXPERT_FF), BF16),
                        pltpu.VMEM((EXPERT_FF, D_MODEL), BF16),
                        pltpu.VMEM((MOE_SLOTS, D_MODEL, EXPERT_FF), F32),
                        pltpu.VMEM((MOE_SLOTS, D_MODEL, EXPERT_FF), F32),
                        pltpu.VMEM((MOE_SLOTS, EXPERT_FF, D_MODEL), F32),
                        pltpu.SemaphoreType.DMA((MOE_SLOTS, 3))],
    )
    return pl.pallas_call(
        _moe_body,
        grid_spec=grid_spec,
        out_shape=jax.ShapeDtypeStruct((MOE_ROWS * SUBLANES, LANES), U32),
        compiler_params=_cparams(("arbitrary",)),
        name="moe",
    )(tile_tab, xs, wg, wu, wd)


def _final_body(pos_ref, x1_ref, mod_ref, rt_ref, g2_ref, b2_ref, ys_hbm, o_ref, buf, sem):
    i = pl.program_id(0)
    n = pl.num_programs(0)
    tm = FINAL_TM
    slot = i % FINAL_SLOTS
    nxt_slot = (i + 2) % FINAL_SLOTS
    nxt_tile = jnp.minimum(i + 2, n - 1)

    def issue(tile, dst_slot, j):
        pair = TOP_K * (tile * tm + j)
        for k in range(TOP_K):
            src_row = pl.multiple_of(pos_ref[pair + k], SUBLANES)
            dst_row = pl.multiple_of((k * tm + j) * SUBLANES, SUBLANES)
            pltpu.make_async_copy(ys_hbm.at[pl.ds(src_row, SUBLANES)],
                                  buf.at[dst_slot, pl.ds(dst_row, SUBLANES)], sem.at[dst_slot]).start(priority=k)

    def wait(s):
        pltpu.make_async_copy(ys_hbm.at[pl.ds(0, TOP_K * tm * SUBLANES)], buf.at[s], sem.at[s]).wait()

    @pl.when(i == 0)
    def _():
        def tok(j, carry):
            issue(0, 0, j)
            issue(jnp.minimum(1, n - 1), 1, j)
            return carry
        lax.fori_loop(0, tm, tok, 0, unroll=8)

    wait(slot)
    gate2 = mod_ref[0, 5:6, :]
    cur = buf.at[slot]

    def chunk(c, carry):
        r0 = pl.multiple_of(c * FINAL_CHUNK, FINAL_CHUNK)
        rows = pl.ds(r0, FINAL_CHUNK)
        a_lo, a_hi = _unpack_halves(_load_token_tiles(cur, r0, FINAL_CHUNK))
        b_lo, b_hi = _unpack_halves(_load_token_tiles(cur, tm + r0, FINAL_CHUNK))
        x1 = x1_ref[rows, :]
        rt = rt_ref[rows, :]
        for r in range(FINAL_CHUNK):
            issue(nxt_tile, nxt_slot, r0 + r)
        w0 = rt[:, 2:3]
        w1 = rt[:, 3:4]
        y = jnp.concatenate([w0 * a_lo + w1 * b_lo, w0 * a_hi + w1 * b_hi], axis=1)
        o_ref[rows, :] = _ln_rows(DEEPNORM_ALPHA * x1 + gate2 * y) * g2_ref[...] + b2_ref[...]
        return carry

    lax.fori_loop(0, tm // FINAL_CHUNK, chunk, 0)

    @pl.when(i == n - 1)
    def _():
        wait((i + 1) % FINAL_SLOTS)
        wait(nxt_slot)


def _final(pos_rows, x1, mod3, route, g2, b2, ys):
    tm = FINAL_TM
    tpb = SEQ // tm
    row = lambda w: pl.BlockSpec((tm, w), lambda i, *_: (i, 0))
    grid_spec = pltpu.PrefetchScalarGridSpec(
        num_scalar_prefetch=1,
        grid=(N_TOK // tm,),
        in_specs=[row(D_MODEL),
                  pl.BlockSpec((1, N_MOD, D_MODEL), lambda i, *_: (i // tpb, 0, 0)),
                  row(LANES),
                  pl.BlockSpec(g2.shape, lambda i, *_: (0, 0)),
                  pl.BlockSpec(b2.shape, lambda i, *_: (0, 0)),
                  pl.BlockSpec(memory_space=pl.ANY)],
        out_specs=row(D_MODEL),
        scratch_shapes=[pltpu.VMEM((FINAL_SLOTS, TOP_K * tm * SUBLANES, LANES), U32),
                        pltpu.SemaphoreType.DMA((FINAL_SLOTS,))],
    )
    return pl.pallas_call(
        _final_body,
        grid_spec=grid_spec,
        out_shape=jax.ShapeDtypeStruct((N_TOK, D_MODEL), F32),
        compiler_params=_cparams(("arbitrary",)),
        name="final",
    )(pos_rows, x1, mod3, route, g2, b2, ys)


def kernel(x, c, positions, w_ada, b_ada, w_in, q_norm_g, w_uq, kv_norm_g, w_ukv, sgu_norm_g, sgu_norm_b,
           w_spatial, b_spatial, w_o, ln1_g, ln1_b, w_router_group, b_router_group, w_router_expert,
           b_router_expert, w_gate, w_up, w_down, ln2_g, ln2_b):
    l = 0
    x2 = x.reshape(N_TOK, D_MODEL)
    mod3 = _ada(c, w_ada[l], b_ada[l][None, :]).reshape(BATCH, N_MOD, D_MODEL)

    wt, wuq, wukv, wo = _prep(w_in[l].T, w_uq[l], w_ukv[l], w_o[l])
    n_r = N_GROUPS + N_EXPERTS
    wr = jnp.pad(jnp.concatenate([w_router_group[l], w_router_expert[l]], axis=1),
                 ((0, 0), (0, LANES - n_r))).astype(BF16)
    br = jnp.pad(jnp.concatenate([b_router_group[l], b_router_expert[l]]), (0, LANES - n_r))[None, :]
    inv_freq = 1.0 / (ROPE_THETA ** (jnp.arange(0, QK_ROPE_DIM, 2, dtype=F32) / QK_ROPE_DIM))
    invf = jnp.tile(inv_freq, 2 * LANES // QK_ROPE_DIM)[None, :]

    cqn, ckvn, kpe, u, vs = _inproj(x2, mod3, wt, q_norm_g[l][None, :], kv_norm_g[l][None, :],
                                    sgu_norm_g[l][None, :], sgu_norm_b[l][None, :])
    q, k, v = _qkv(cqn, ckvn, kpe, positions.reshape(N_TOK, 1), invf, wuq, wukv)
    attn = _attn(q, k, v).reshape(N_TOK, MLA_WIDTH)
    x1, h2, logits = _mixout(x2, mod3, attn, u, vs, w_spatial[l], b_spatial[l].T, wo,
                             ln1_g[l][None, :], ln1_b[l][None, :], wr, br)
    route, counts = _route(logits)
    pos_tab, tile_tab = _plan(route, counts)
    pos_rows = pos_tab[:, 0:TOP_K].reshape(-1)
    xs = _dispatch(pos_rows, tile_tab, h2)
    ys = _moe(tile_tab, xs, w_gate[l], w_up[l], w_down[l])
    out = _final(pos_rows, x1, mod3, route, ln2_g[l][None, :], ln2_b[l][None, :], ys)
    return out.reshape(BATCH, SEQ, D_MODEL)
```

```python
import jax
import jax.numpy as jnp
import numpy as np
from jax import lax
from jax.experimental import pallas as pl
from jax.experimental.pallas import tpu as pltpu

D_MODEL = 2048
BATCH = 4
SEQ = 2048
N_TOK = BATCH * SEQ

MLA_HEADS = 8
QK_NOPE_DIM = 128
QK_ROPE_DIM = 64
QK_DIM = QK_NOPE_DIM + QK_ROPE_DIM
V_HEAD_DIM = 128
Q_LORA_RANK = 768
KV_LORA_RANK = 512
ROPE_THETA = 10000.0
MLA_WIDTH = MLA_HEADS * V_HEAD_DIM

SGU_GROUPS = 8
SGU_GROUP_DIM = 128
SGU_CHUNK = 128
SGU_WIDTH = SGU_GROUPS * SGU_GROUP_DIM

N_GROUPS = 4
EXPERTS_PER_GROUP = 8
N_EXPERTS = N_GROUPS * EXPERTS_PER_GROUP
TOP_K = 2
EXPERT_FF = 512

DEEPNORM_ALPHA = 2.0 ** 0.25
EPS = 1e-6
N_MOD = 6
NEG_BIG = -1e30

LANES = 128
SUBLANES = 8
VMEM_LIMIT = 56 * 1024 * 1024

ADA_TN = 1024
TOK_TM = 512
MIX_TM = 256
PREP_STEPS = 4
ATT_TQ = 1024
ATT_TK = 1024
ATT_HEADS = 4
MOE_TM = 256
MOE_TILES = (N_TOK * TOP_K + N_EXPERTS * (MOE_TM - 1)) // MOE_TM + 1
MOE_ROWS = MOE_TILES * MOE_TM
PLAN_TM = 2048
ROUTE_TM = 1024
RT_RANK = 4
DISPATCH_TM = 1024
FINAL_TM = 256
FINAL_CHUNK = 128
FINAL_SLOTS = 3
assert MOE_TILES <= LANES
T_EXP, T_VALID, T_FIRST, T_CLEAR, T_NEXT, T_NEXT2, T_NEXT3, T_ORD, T_LAST = range(9)
MOE_SLOTS = 2
WEIGHT_DMA_PRIORITY = 1

F32 = jnp.float32
BF16 = jnp.bfloat16
U32 = jnp.uint32
HALF_D = D_MODEL // 2


def _cparams(sem):
    return pltpu.CompilerParams(dimension_semantics=sem, vmem_limit_bytes=VMEM_LIMIT)


def _const_spec(shape):
    nd = len(shape)
    return pl.BlockSpec(shape, lambda *_: (0,) * nd, pipeline_mode=pl.Buffered(1))


def _ln_rows(x):
    mu = jnp.mean(x, axis=-1, keepdims=True)
    xc = x - mu
    var = jnp.mean(xc * xc, axis=-1, keepdims=True)
    return xc * lax.rsqrt(var + EPS)


def _rms_rows(x):
    return x * lax.rsqrt(jnp.mean(x * x, axis=-1, keepdims=True) + EPS)


def _pack_halves(x):
    half = x.shape[-1] // 2
    return pltpu.pack_elementwise([x[:, :half], x[:, half:]], packed_dtype=BF16)


def _unpack_halves(w):
    lo = pltpu.unpack_elementwise(w, index=0, packed_dtype=BF16, unpacked_dtype=F32)
    hi = pltpu.unpack_elementwise(w, index=1, packed_dtype=BF16, unpacked_dtype=F32)
    return lo, hi


def _store_token_tiles(ref, w):
    rows = w.shape[0]
    for s in range(SUBLANES):
        ref[pl.ds(s, rows, stride=SUBLANES), :] = w[:, s * LANES:(s + 1) * LANES]


def _load_token_tiles(ref, start_row, rows):
    return jnp.concatenate([ref[pl.ds(start_row * SUBLANES + s, rows, stride=SUBLANES), :]
                            for s in range(SUBLANES)], axis=1)


def _gelu_tanh(x):
    c = np.sqrt(2.0 / np.pi).astype(np.float32)
    return 0.5 * x * (1.0 + jnp.tanh(c * (x + 0.044715 * (x * x * x))))


def _ada_body(c_ref, w_ref, b_ref, o_ref):
    o_ref[...] = jnp.dot(c_ref[...].astype(BF16), w_ref[...].astype(BF16),
                         preferred_element_type=F32) + b_ref[...]


def _ada(c, w, b):
    n = w.shape[1]
    return pl.pallas_call(
        _ada_body,
        grid=(n // ADA_TN,),
        in_specs=[pl.BlockSpec((BATCH, D_MODEL), lambda j: (0, 0)),
                  pl.BlockSpec((D_MODEL, ADA_TN), lambda j: (0, j)),
                  pl.BlockSpec((1, ADA_TN), lambda j: (0, j))],
        out_specs=pl.BlockSpec((BATCH, ADA_TN), lambda j: (0, j)),
        out_shape=jax.ShapeDtypeStruct((BATCH, n), F32),
        compiler_params=_cparams(("parallel",)),
        name="ada",
    )(c, w, b)


def _prep_body(win_ref, wuq_ref, wukv_ref, wo_ref, win_o, wuq_o, wukv_o, wo_o):
    win_o[...] = win_ref[...].astype(BF16)
    u = wuq_ref[...]
    nope = [u[:, h * QK_DIM:h * QK_DIM + QK_NOPE_DIM] for h in range(MLA_HEADS)]
    rope = [u[:, h * QK_DIM + QK_NOPE_DIM:(h + 1) * QK_DIM] for h in range(MLA_HEADS)]
    wuq_o[...] = jnp.concatenate(nope + rope, axis=1).astype(BF16)
    kv = wukv_ref[...]
    hw = QK_NOPE_DIM + V_HEAD_DIM
    kn = [kv[:, h * hw:h * hw + QK_NOPE_DIM] for h in range(MLA_HEADS)]
    vv = [kv[:, h * hw + QK_NOPE_DIM:(h + 1) * hw] for h in range(MLA_HEADS)]
    wukv_o[...] = jnp.concatenate(kn + vv, axis=1).astype(BF16)
    wo_o[...] = wo_ref[...].astype(BF16)


def _prep(w_in_t, w_uq, w_ukv, w_o):
    steps = PREP_STEPS
    blk = lambda a: pl.BlockSpec((a.shape[0] // steps, a.shape[1]), lambda i: (i, 0))
    ins = (w_in_t, w_uq, w_ukv, w_o)
    return pl.pallas_call(
        _prep_body,
        grid=(steps,),
        in_specs=[blk(a) for a in ins],
        out_specs=[blk(a) for a in ins],
        out_shape=[jax.ShapeDtypeStruct(a.shape, BF16) for a in ins],
        compiler_params=_cparams(("parallel",)),
        name="prep",
    )(*ins)


def _inproj_body(x_ref, mod_ref, wt_ref, gq_ref, gkv_ref, sg_ref, sb_ref,
                 cq_ref, ckv_ref, kpe_ref, u_ref, vs_ref):
    o1, o2, o3 = Q_LORA_RANK, Q_LORA_RANK + KV_LORA_RANK, Q_LORA_RANK + KV_LORA_RANK + QK_ROPE_DIM

    def proj(lo, hi):
        return lax.dot_general(h, wt_ref[lo:hi, :], (((1,), (1,)), ((), ())), preferred_element_type=F32)

    sh = mod_ref[0, 0:1, :]
    sc = mod_ref[0, 1:2, :]
    h = (_ln_rows(x_ref[...]) * (1.0 + sc) + sh).astype(BF16)
    cq_ref[...] = (_rms_rows(proj(0, o1)) * gq_ref[...]).astype(BF16)
    ckv_ref[...] = (_rms_rows(proj(o1, o2)) * gkv_ref[...]).astype(BF16)
    kpe_ref[...] = proj(o2, o2 + LANES)
    gz = _gelu_tanh(proj(o3, o3 + 2 * SGU_WIDTH))
    u_ref[...] = gz[:, :SGU_WIDTH]
    vs_ref[...] = (_ln_rows(gz[:, SGU_WIDTH:]) * sg_ref[...] + sb_ref[...]).astype(BF16)


def _inproj(x2, mod3, wt, gq, gkv, sg, sb):
    tm = TOK_TM
    tiles_per_batch = SEQ // tm
    row = lambda w: pl.BlockSpec((tm, w), lambda i: (i, 0))
    return pl.pallas_call(
        _inproj_body,
        grid=(N_TOK // tm,),
        in_specs=[row(D_MODEL),
                  pl.BlockSpec((1, N_MOD, D_MODEL), lambda i: (i // tiles_per_batch, 0, 0)),
                  _const_spec(wt.shape),
                  _const_spec(gq.shape), _const_spec(gkv.shape), _const_spec(sg.shape), _const_spec(sb.shape)],
        out_specs=[row(Q_LORA_RANK), row(KV_LORA_RANK), row(LANES), row(SGU_WIDTH), row(SGU_WIDTH)],
        out_shape=[jax.ShapeDtypeStruct((N_TOK, Q_LORA_RANK), BF16),
                   jax.ShapeDtypeStruct((N_TOK, KV_LORA_RANK), BF16),
                   jax.ShapeDtypeStruct((N_TOK, LANES), F32),
                   jax.ShapeDtypeStruct((N_TOK, SGU_WIDTH), F32),
                   jax.ShapeDtypeStruct((N_TOK, SGU_WIDTH), BF16)],
        compiler_params=_cparams(("parallel",)),
        name="inproj",
    )(x2, mod3, wt, gq, gkv, sg, sb)


def _rope(x, cos, sin):
    w = x.shape[-1]
    lane = lax.broadcasted_iota(jnp.int32, x.shape, 1)
    first_half = (lane % QK_ROPE_DIM) < (QK_ROPE_DIM // 2)
    rot = jnp.where(first_half,
                    -pltpu.roll(x, w - QK_ROPE_DIM // 2, 1),
                    pltpu.roll(x, QK_ROPE_DIM // 2, 1))
    return x * cos + rot * sin


def _qkv_body(cq_ref, ckv_ref, kpe_ref, pos_ref, invf_ref, wuq_ref, wukv_ref, q_ref, k_ref, v_ref):
    ang = pos_ref[...].astype(F32) * invf_ref[...]
    cos1 = jnp.cos(ang)
    sin1 = jnp.sin(ang)
    reps = MLA_HEADS * QK_ROPE_DIM // LANES
    cos = jnp.concatenate([cos1] * reps, axis=1)
    sin = jnp.concatenate([sin1] * reps, axis=1)
    scale = np.float32(QK_DIM ** -0.5)
    q = jnp.dot(cq_ref[...], wuq_ref[...], preferred_element_type=F32) * scale
    q_pe = _rope(q[:, MLA_HEADS * QK_NOPE_DIM:], cos, sin)
    kv = jnp.dot(ckv_ref[...], wukv_ref[...], preferred_element_type=F32)
    k_pe = _rope(kpe_ref[...], cos1, sin1)[:, :QK_ROPE_DIM].astype(BF16)
    for h in range(MLA_HEADS):
        q_ref[0, h, :, 0:QK_NOPE_DIM] = q[:, h * QK_NOPE_DIM:(h + 1) * QK_NOPE_DIM].astype(BF16)
        q_ref[0, h, :, QK_NOPE_DIM:QK_DIM] = q_pe[:, h * QK_ROPE_DIM:(h + 1) * QK_ROPE_DIM].astype(BF16)
        k_ref[0, h, :, 0:QK_NOPE_DIM] = kv[:, h * QK_NOPE_DIM:(h + 1) * QK_NOPE_DIM].astype(BF16)
        k_ref[0, h, :, QK_NOPE_DIM:QK_DIM] = k_pe
        v_ref[0, h, :, :] = kv[:, MLA_WIDTH + h * V_HEAD_DIM:MLA_WIDTH + (h + 1) * V_HEAD_DIM].astype(BF16)


def _qkv(cqn, ckvn, kpe, pos2, invf, wuq, wukv):
    tm = TOK_TM
    tpb = SEQ // tm
    row = lambda w: pl.BlockSpec((tm, w), lambda i: (i, 0))
    head_out = lambda w: pl.BlockSpec((1, MLA_HEADS, tm, w), lambda i: (i // tpb, 0, i % tpb, 0))
    return pl.pallas_call(
        _qkv_body,
        grid=(N_TOK // tm,),
        in_specs=[row(Q_LORA_RANK), row(KV_LORA_RANK), row(LANES), row(1),
                  _const_spec(invf.shape), _const_spec(wuq.shape), _const_spec(wukv.shape)],
        out_specs=[head_out(QK_DIM), head_out(QK_DIM), head_out(V_HEAD_DIM)],
        out_shape=[jax.ShapeDtypeStruct((BATCH, MLA_HEADS, SEQ, QK_DIM), BF16),
                   jax.ShapeDtypeStruct((BATCH, MLA_HEADS, SEQ, QK_DIM), BF16),
                   jax.ShapeDtypeStruct((BATCH, MLA_HEADS, SEQ, V_HEAD_DIM), BF16)],
        compiler_params=_cparams(("parallel",)),
        name="qkv",
    )(cqn, ckvn, kpe, pos2, invf, wuq, wukv)


def _attn_body(q_ref, k_ref, v_ref, o_ref):
    i = pl.program_id(2)

    def step(h, j, carry, masked):
        m, l, acc = carry
        start = pl.multiple_of(j * ATT_TK, ATT_TK)
        k = k_ref[0, h, pl.ds(start, ATT_TK), :]
        v = v_ref[0, h, pl.ds(start, ATT_TK), :]
        s = lax.dot_general(q_ref[0, h], k, (((1,), (1,)), ((), ())), preferred_element_type=F32)
        if masked:
            r = lax.broadcasted_iota(jnp.int32, s.shape, 0)
            c = lax.broadcasted_iota(jnp.int32, s.shape, 1)
            s = jnp.where(c <= r, s, NEG_BIG)
        m_new = jnp.maximum(m, jnp.max(s, axis=-1, keepdims=True))
        p = jnp.exp(s - m_new)
        a = jnp.exp(m - m_new)
        l = a * l + jnp.sum(p, axis=-1, keepdims=True)
        acc = a * acc + jnp.dot(p.astype(BF16), v, preferred_element_type=F32)
        return m_new, l, acc

    def steps(j, carries, masked):
        return tuple(step(h, j, carries[h], masked) for h in range(ATT_HEADS))

    init = tuple((jnp.full((ATT_TQ, 1), NEG_BIG, F32), jnp.zeros((ATT_TQ, 1), F32),
                  jnp.zeros((ATT_TQ, V_HEAD_DIM), F32)) for _ in range(ATT_HEADS))
    carries = lax.fori_loop(0, i, lambda j, c: steps(j, c, False), init)
    carries = steps(i, carries, True)
    for h, (m, l, acc) in enumerate(carries):
        o_ref[0, :, h * V_HEAD_DIM:(h + 1) * V_HEAD_DIM] = (acc / l).astype(BF16)


def _attn(q, k, v):
    assert ATT_TQ == ATT_TK
    hb = ATT_HEADS
    return pl.pallas_call(
        _attn_body,
        grid=(BATCH, MLA_HEADS // hb, SEQ // ATT_TQ),
        in_specs=[pl.BlockSpec((1, hb, ATT_TQ, QK_DIM), lambda b, h, i: (b, h, i, 0)),
                  pl.BlockSpec((1, hb, SEQ, QK_DIM), lambda b, h, i: (b, h, 0, 0)),
                  pl.BlockSpec((1, hb, SEQ, V_HEAD_DIM), lambda b, h, i: (b, h, 0, 0))],
        out_specs=pl.BlockSpec((1, ATT_TQ, hb * V_HEAD_DIM), lambda b, h, i: (b, i, h)),
        out_shape=jax.ShapeDtypeStruct((BATCH, SEQ, MLA_WIDTH), BF16),
        compiler_params=_cparams(("parallel", "parallel", "arbitrary")),
        name="attn",
    )(q, k, v)


def _mixout_body(x_ref, mod_ref, attn_ref, u_ref, vs_ref, wsp_ref, bsp_ref, woa_ref, wos_ref,
                 g1_ref, b1_ref, wr_ref, br_ref, x1_ref, h2_ref, lg_ref, sgu_scr):
    r = lax.broadcasted_iota(jnp.int32, (SGU_CHUNK, SGU_CHUNK), 0)
    c = lax.broadcasted_iota(jnp.int32, (SGU_CHUNK, SGU_CHUNK), 1)
    causal = c <= r
    for g in range(SGU_GROUPS):
        ws = jnp.where(causal, wsp_ref[g], 0.0).astype(BF16)
        bias = bsp_ref[:, g:g + 1]
        cols = slice(g * SGU_GROUP_DIM, (g + 1) * SGU_GROUP_DIM)
        for ch in range(MIX_TM // SGU_CHUNK):
            rows = slice(ch * SGU_CHUNK, (ch + 1) * SGU_CHUNK)
            mixed = jnp.dot(ws, vs_ref[rows, cols], preferred_element_type=F32) + bias
            sgu_scr[rows, cols] = (u_ref[rows, cols] * mixed).astype(BF16)
    y = (jnp.dot(attn_ref[...], woa_ref[...], preferred_element_type=F32)
         + jnp.dot(sgu_scr[...], wos_ref[...], preferred_element_type=F32))
    gate1 = mod_ref[0, 2:3, :]
    sh2 = mod_ref[0, 3:4, :]
    sc2 = mod_ref[0, 4:5, :]
    x1 = _ln_rows(DEEPNORM_ALPHA * x_ref[...] + gate1 * y) * g1_ref[...] + b1_ref[...]
    x1_ref[...] = x1
    h2 = _ln_rows(x1) * (1.0 + sc2) + sh2
    _store_token_tiles(h2_ref, _pack_halves(h2))
    lg_ref[...] = jnp.dot(h2.astype(BF16), wr_ref[...], preferred_element_type=F32) + br_ref[...]


def _mixout(x2, mod3, attn, u, vs, wsp, bsp_t, wo, g1, b1, wr, br):
    tm = MIX_TM
    tpb = SEQ // tm
    row = lambda w: pl.BlockSpec((tm, w), lambda i: (i, 0))
    wo_half = lambda j: pl.BlockSpec((MLA_WIDTH, D_MODEL), lambda i: (j, 0), pipeline_mode=pl.Buffered(1))
    return pl.pallas_call(
        _mixout_body,
        grid=(N_TOK // tm,),
        in_specs=[row(D_MODEL),
                  pl.BlockSpec((1, N_MOD, D_MODEL), lambda i: (i // tpb, 0, 0)),
                  row(MLA_WIDTH), row(SGU_WIDTH), row(SGU_WIDTH),
                  _const_spec(wsp.shape), _const_spec(bsp_t.shape), wo_half(0), wo_half(1),
                  _const_spec(g1.shape), _const_spec(b1.shape), _const_spec(wr.shape), _const_spec(br.shape)],
        out_specs=[row(D_MODEL), pl.BlockSpec((tm * SUBLANES, LANES), lambda i: (i, 0)), row(LANES)],
        out_shape=[jax.ShapeDtypeStruct((N_TOK, D_MODEL), F32),
                   jax.ShapeDtypeStruct((N_TOK * SUBLANES, LANES), U32),
                   jax.ShapeDtypeStruct((N_TOK, LANES), F32)],
        scratch_shapes=[pltpu.VMEM((tm, SGU_WIDTH), BF16)],
        compiler_params=_cparams(("parallel",)),
        name="mix_out",
    )(x2, mod3, attn, u, vs, wsp, bsp_t, wo, wo, g1, b1, wr, br)


def _route_math(lg):
    lane = lax.broadcasted_iota(jnp.int32, lg.shape, 1)
    big = jnp.int32(LANES)

    def top1(vals):
        m = jnp.max(vals, axis=-1, keepdims=True)
        idx = jnp.min(jnp.where(vals == m, lane, big), axis=-1, keepdims=True)
        return m, idx

    is_group = lane < N_GROUPS
    glog = jnp.where(is_group, lg, -jnp.inf)
    gmax, gidx = top1(glog)
    pg_top = 1.0 / jnp.sum(jnp.exp(glog - gmax), axis=-1, keepdims=True)
    eid = lane - N_GROUPS
    sel = (eid >= gidx * EXPERTS_PER_GROUP) & (eid < (gidx + 1) * EXPERTS_PER_GROUP)
    elog = jnp.where(sel, lg, -jnp.inf)
    m1, i1 = top1(elog)
    m2, i2 = top1(jnp.where(lane == i1, -jnp.inf, elog))
    e2 = jnp.exp(m2 - m1)
    w1 = pg_top / (1.0 + e2)
    w2 = pg_top * e2 / (1.0 + e2)
    return jnp.where(lane == 0, (i1 - N_GROUPS).astype(F32),
                     jnp.where(lane == 1, (i2 - N_GROUPS).astype(F32),
                               jnp.where(lane == 2, w1, jnp.where(lane == 3, w2, 0.0))))


def _rank_math(rt, counts):
    t = rt.shape[0]
    lane = lax.broadcasted_iota(jnp.int32, (t, LANES), 1).astype(F32)
    oh0 = lane == rt[:, 0:1]
    oh1 = lane == rt[:, 1:2]
    s = jnp.where(oh0 | oh1, 1.0, 0.0)
    r = lax.broadcasted_iota(jnp.int32, (t, t), 0)
    c = lax.broadcasted_iota(jnp.int32, (t, t), 1)
    before = jnp.where(c < r, 1.0, 0.0).astype(BF16)
    csum = jnp.dot(before, s.astype(BF16), preferred_element_type=F32) + counts
    rank0 = jnp.sum(jnp.where(oh0, csum, 0.0), axis=-1, keepdims=True)
    rank1 = jnp.sum(jnp.where(oh1, csum, 0.0), axis=-1, keepdims=True)
    return rank0, rank1, counts + jnp.sum(s, axis=0, keepdims=True)


def _route_body(lg_ref, rt_ref, cnt_ref, cnt_scr):
    @pl.when(pl.program_id(0) == 0)
    def _():
        cnt_scr[...] = jnp.zeros_like(cnt_scr)

    rt = _route_math(lg_ref[...])
    rank0, rank1, counts = _rank_math(rt, cnt_scr[...])
    cnt_scr[...] = counts
    lane = lax.broadcasted_iota(jnp.int32, rt.shape, 1)
    rt_ref[...] = jnp.where(lane == RT_RANK, rank0, jnp.where(lane == RT_RANK + 1, rank1, rt))
    cnt_ref[...] = jnp.broadcast_to(counts, cnt_ref.shape)


def _route(logits):
    tm = ROUTE_TM
    return pl.pallas_call(
        _route_body,
        grid=(N_TOK // tm,),
        in_specs=[pl.BlockSpec((tm, LANES), lambda i: (i, 0))],
        out_specs=[pl.BlockSpec((tm, LANES), lambda i: (i, 0)),
                   pl.BlockSpec((SUBLANES, LANES), lambda i: (0, 0))],
        out_shape=[jax.ShapeDtypeStruct((N_TOK, LANES), F32),
                   jax.ShapeDtypeStruct((SUBLANES, LANES), F32)],
        scratch_shapes=[pltpu.VMEM((1, LANES), F32)],
        compiler_params=_cparams(("arbitrary",)),
        name="route",
    )(logits)


def _plan_body(rt_ref, cnt_ref, pos_ref, tt_ref):
    t = PLAN_TM
    lane = lax.broadcasted_iota(jnp.int32, (t, LANES), 1)
    rt = rt_ref[...]
    oh0 = lane.astype(F32) == rt[:, 0:1]
    oh1 = lane.astype(F32) == rt[:, 1:2]
    counts = cnt_ref[0:1, :]
    tiles = jnp.floor((counts + (MOE_TM - 1)) * (1.0 / MOE_TM))
    r = lax.broadcasted_iota(jnp.int32, (LANES, LANES), 0)
    c = lax.broadcasted_iota(jnp.int32, (LANES, LANES), 1)
    upto = jnp.where(r <= c, 1.0, 0.0).astype(BF16)
    tiles8 = jnp.broadcast_to(tiles, (SUBLANES, LANES)).astype(BF16)
    tile_end = jnp.dot(tiles8, upto, preferred_element_type=F32)[0:1]
    offs = (tile_end - tiles) * MOE_TM
    p0 = jnp.sum(jnp.where(oh0, offs, 0.0), axis=-1, keepdims=True) + rt[:, RT_RANK:RT_RANK + 1]
    p1 = jnp.sum(jnp.where(oh1, offs, 0.0), axis=-1, keepdims=True) + rt[:, RT_RANK + 1:RT_RANK + 2]
    pos_ref[...] = (jnp.where(lane == 0, p0, jnp.where(lane == 1, p1, 0.0)) * SUBLANES).astype(jnp.int32)
    tt_ref[...] = _tile_table(tiles, tile_end)


def _tile_table(tiles, tile_end):
    lane_e = lax.broadcasted_iota(jnp.int32, (LANES, LANES), 1)
    lane_f = lane_e.astype(F32)
    tile_id = lax.broadcasted_iota(jnp.int32, (LANES, LANES), 0).astype(F32)
    is_e = lane_e < N_EXPERTS
    owns = is_e & (tiles > 0)

    def count(cond):
        return jnp.sum(jnp.where(cond, 1.0, 0.0), axis=-1, keepdims=True)

    def first_owner_after(e):
        return jnp.min(jnp.where(owns & (lane_f > e), lane_f, jnp.float32(LANES)), axis=-1, keepdims=True)

    total = jnp.max(tile_end, axis=-1, keepdims=True)
    t_valid = jnp.where(tile_id[:, 0:1] < total, 1.0, 0.0)
    t_exp = jnp.where(t_valid > 0, count(is_e & (tile_end <= tile_id)), count(is_e & (tile_end <= total - 1.0)))
    t_first = count(owns & ((tile_end - tiles) == tile_id))
    t_clear = jnp.maximum(count(owns & ((tile_end - 1.0) == tile_id)), 1.0 - t_valid)
    t_next = first_owner_after(t_exp)
    t_next2 = first_owner_after(t_next)
    t_next3 = first_owner_after(t_next2)
    cols = {T_EXP: t_exp, T_VALID: t_valid, T_FIRST: t_first, T_CLEAR: t_clear, T_NEXT: t_next,
            T_NEXT2: t_next2, T_NEXT3: t_next3, T_ORD: count(owns & (lane_f < t_exp)), T_LAST: total - 1.0}
    table = jnp.zeros((LANES, LANES), F32)
    for k, col in cols.items():
        table = jnp.where(lane_e == k, col, table)
    return table.astype(jnp.int32)


def _plan(route, counts):
    t = PLAN_TM
    return pl.pallas_call(
        _plan_body,
        grid=(N_TOK // t,),
        in_specs=[pl.BlockSpec((t, LANES), lambda i: (i, 0)),
                  pl.BlockSpec((SUBLANES, LANES), lambda i: (0, 0))],
        out_specs=[pl.BlockSpec((t, LANES), lambda i: (i, 0)),
                   pl.BlockSpec((LANES, LANES), lambda i: (0, 0))],
        out_shape=[jax.ShapeDtypeStruct((N_TOK, LANES), jnp.int32),
                   jax.ShapeDtypeStruct((LANES, LANES), jnp.int32)],
        compiler_params=_cparams(("arbitrary",)),
        name="plan",
    )(route, counts)


def _rows_wait(ref, n_rows, sem):
    pltpu.make_async_copy(ref.at[pl.ds(0, n_rows)], ref.at[pl.ds(0, n_rows)], sem).wait()


def _dispatch_body(pos_ref, tt_ref, h_ref, xs_hbm, zbuf, sem_z, sem_s):
    i = pl.program_id(0)
    tile_rows = MOE_TM * SUBLANES

    @pl.when(i == 0)
    def _():
        zbuf[...] = _pack_halves(jnp.zeros((tile_rows, 2 * LANES), F32))

        def zero_copy(tile):
            start = pl.multiple_of(tile * tile_rows, tile_rows)
            return pltpu.make_async_copy(zbuf, xs_hbm.at[pl.ds(start, tile_rows)], sem_z)

        def clear_start(tile, carry):
            @pl.when(tt_ref[tile, T_CLEAR] > 0)
            def _():
                zero_copy(tile).start()
            return carry

        def clear_wait(tile, carry):
            @pl.when(tt_ref[tile, T_CLEAR] > 0)
            def _():
                zero_copy(tile).wait()
            return carry

        lax.fori_loop(0, MOE_TILES, clear_start, 0)
        lax.fori_loop(0, MOE_TILES, clear_wait, 0)

    def tok(j, carry):
        src = h_ref.at[pl.ds(pl.multiple_of(j * SUBLANES, SUBLANES), SUBLANES)]
        pair = TOP_K * (i * DISPATCH_TM + j)
        for k in range(TOP_K):
            dst_row = pl.multiple_of(pos_ref[pair + k], SUBLANES)
            pltpu.make_async_copy(src, xs_hbm.at[pl.ds(dst_row, SUBLANES)], sem_s).start(priority=k)
        return carry

    lax.fori_loop(0, DISPATCH_TM, tok, 0, unroll=8)
    _rows_wait(xs_hbm, TOP_K * DISPATCH_TM * SUBLANES, sem_s)


def _dispatch(pos_rows, tile_clear, h2p):
    grid_spec = pltpu.PrefetchScalarGridSpec(
        num_scalar_prefetch=2,
        grid=(N_TOK // DISPATCH_TM,),
        in_specs=[pl.BlockSpec((DISPATCH_TM * SUBLANES, LANES), lambda i, *_: (i, 0))],
        out_specs=pl.BlockSpec(memory_space=pl.ANY),
        scratch_shapes=[pltpu.VMEM((MOE_TM * SUBLANES, LANES), U32),
                        pltpu.SemaphoreType.DMA(()), pltpu.SemaphoreType.DMA(())],
    )
    return pl.pallas_call(
        _dispatch_body,
        grid_spec=grid_spec,
        out_shape=jax.ShapeDtypeStruct((MOE_ROWS * SUBLANES, LANES), U32),
        compiler_params=_cparams(("arbitrary",)),
        name="dispatch",
    )(pos_rows, tile_clear, h2p)


def _moe_body(tt_ref, x_ref, wg_hbm, wu_hbm, wd_hbm, y_ref,
              wg_s, wu_s, wd_s, stg_g, stg_u, stg_d, sem):
    i = pl.program_id(0)
    ahead = (T_EXP, T_NEXT, T_NEXT2, T_NEXT3)

    def fetch(e, slot):
        return (pltpu.make_async_copy(wg_hbm.at[e], stg_g.at[slot], sem.at[slot, 0]),
                pltpu.make_async_copy(wu_hbm.at[e], stg_u.at[slot], sem.at[slot, 1]),
                pltpu.make_async_copy(wd_hbm.at[e], stg_d.at[slot], sem.at[slot, 2]))

    @pl.when(i == 0)
    def _():
        for d in range(MOE_SLOTS):
            @pl.when(tt_ref[0, ahead[d]] < N_EXPERTS)
            def _():
                for cp in fetch(tt_ref[0, ahead[d]], d):
                    cp.start(priority=WEIGHT_DMA_PRIORITY)

    @pl.when(tt_ref[i, T_FIRST] > 0)
    def _():
        slot = tt_ref[i, T_ORD] % MOE_SLOTS
        for cp in fetch(tt_ref[i, T_EXP], slot):
            cp.wait()
        wg_s[...] = stg_g[slot].astype(BF16)
        wu_s[...] = stg_u[slot].astype(BF16)
        wd_s[...] = stg_d[slot].astype(BF16)

        @pl.when(tt_ref[i, ahead[MOE_SLOTS]] < N_EXPERTS)
        def _():
            for cp in fetch(tt_ref[i, ahead[MOE_SLOTS]], slot):
                cp.start(priority=WEIGHT_DMA_PRIORITY)

    @pl.when(tt_ref[i, T_VALID] > 0)
    def _():
        lo, hi = _unpack_halves(_load_token_tiles(x_ref, 0, MOE_TM))
        xa = lo.astype(BF16)
        xb = hi.astype(BF16)
        g = (jnp.dot(xa, wg_s[:HALF_D, :], preferred_element_type=F32)
             + jnp.dot(xb, wg_s[HALF_D:, :], preferred_element_type=F32))
        u = (jnp.dot(xa, wu_s[:HALF_D, :], preferred_element_type=F32)
             + jnp.dot(xb, wu_s[HALF_D:, :], preferred_element_type=F32))
        hid = (g * jax.nn.sigmoid(g) * u).astype(BF16)
        _store_token_tiles(y_ref, _pack_halves(jnp.dot(hid, wd_s[...], preferred_element_type=F32)))

    @pl.when(tt_ref[i, T_VALID] == 0)
    def _():
        y_ref[...] = _pack_halves(jnp.zeros((MOE_TM * SUBLANES, 2 * LANES), F32))


def _moe(tile_tab, xs, wg, wu, wd):
    tm = MOE_TM
    grid_spec = pltpu.PrefetchScalarGridSpec(
        num_scalar_prefetch=1,
        grid=(MOE_TILES,),
        in_specs=[pl.BlockSpec((tm * SUBLANES, LANES), lambda i, tt: (jnp.minimum(i, tt[0, T_LAST]), 0)),
                  pl.BlockSpec(memory_space=pl.ANY), pl.BlockSpec(memory_space=pl.ANY),
                  pl.BlockSpec(memory_space=pl.ANY)],
        out_specs=pl.BlockSpec((tm * SUBLANES, LANES), lambda i, *_: (i, 0)),
        scratch_shapes=[pltpu.VMEM((D_MODEL, EXPERT_FF), BF16), pltpu.VMEM((D_MODEL, EXPERT_FF), BF16),
                        pltpu.VMEM((EXPERT_FF, D_MODEL), BF16),
                        pltpu.VMEM((MOE_SLOTS, D_MODEL, EXPERT_FF), F32),
                        pltpu.VMEM((MOE_SLOTS, D_MODEL, EXPERT_FF), F32),
                        pltpu.VMEM((MOE_SLOTS, EXPERT_FF, D_MODEL), F32),
                        pltpu.SemaphoreType.DMA((MOE_SLOTS, 3))],
    )
    return pl.pallas_call(
        _moe_body,
        grid_spec=grid_spec,
        out_shape=jax.ShapeDtypeStruct((MOE_ROWS * SUBLANES, LANES), U32),
        compiler_params=_cparams(("arbitrary",)),
        name="moe",
    )(tile_tab, xs, wg, wu, wd)


def _final_body(pos_ref, x1_ref, mod_ref, rt_ref, g2_ref, b2_ref, ys_hbm, o_ref, buf, sem):
    i = pl.program_id(0)
    n = pl.num_programs(0)
    tm = FINAL_TM
    slot = i % FINAL_SLOTS
    nxt_slot = (i + 2) % FINAL_SLOTS
    nxt_tile = jnp.minimum(i + 2, n - 1)

    def issue(tile, dst_slot, j):
        pair = TOP_K * (tile * tm + j)
        for k in range(TOP_K):
            src_row = pl.multiple_of(pos_ref[pair + k], SUBLANES)
            dst_row = pl.multiple_of((k * tm + j) * SUBLANES, SUBLANES)
            pltpu.make_async_copy(ys_hbm.at[pl.ds(src_row, SUBLANES)],
                                  buf.at[dst_slot, pl.ds(dst_row, SUBLANES)], sem.at[dst_slot]).start(priority=k)

    def wait(s):
        pltpu.make_async_copy(ys_hbm.at[pl.ds(0, TOP_K * tm * SUBLANES)], buf.at[s], sem.at[s]).wait()

    @pl.when(i == 0)
    def _():
        def tok(j, carry):
            issue(0, 0, j)
            issue(jnp.minimum(1, n - 1), 1, j)
            return carry
        lax.fori_loop(0, tm, tok, 0, unroll=8)

    wait(slot)
    gate2 = mod_ref[0, 5:6, :]
    cur = buf.at[slot]

    def chunk(c, carry):
        r0 = pl.multiple_of(c * FINAL_CHUNK, FINAL_CHUNK)
        rows = pl.ds(r0, FINAL_CHUNK)
        a_lo, a_hi = _unpack_halves(_load_token_tiles(cur, r0, FINAL_CHUNK))
        b_lo, b_hi = _unpack_halves(_load_token_tiles(cur, tm + r0, FINAL_CHUNK))
        x1 = x1_ref[rows, :]
        rt = rt_ref[rows, :]
        for r in range(FINAL_CHUNK):
            issue(nxt_tile, nxt_slot, r0 + r)
        w0 = rt[:, 2:3]
        w1 = rt[:, 3:4]
        y = jnp.concatenate([w0 * a_lo + w1 * b_lo, w0 * a_hi + w1 * b_hi], axis=1)
        o_ref[rows, :] = _ln_rows(DEEPNORM_ALPHA * x1 + gate2 * y) * g2_ref[...] + b2_ref[...]
        return carry

    lax.fori_loop(0, tm // FINAL_CHUNK, chunk, 0)

    @pl.when(i == n - 1)
    def _():
        wait((i + 1) % FINAL_SLOTS)
        wait(nxt_slot)


def _final(pos_rows, x1, mod3, route, g2, b2, ys):
    tm = FINAL_TM
    tpb = SEQ // tm
    row = lambda w: pl.BlockSpec((tm, w), lambda i, *_: (i, 0))
    grid_spec = pltpu.PrefetchScalarGridSpec(
        num_scalar_prefetch=1,
        grid=(N_TOK // tm,),
        in_specs=[row(D_MODEL),
                  pl.BlockSpec((1, N_MOD, D_MODEL), lambda i, *_: (i // tpb, 0, 0)),
                  row(LANES),
                  pl.BlockSpec(g2.shape, lambda i, *_: (0, 0)),
                  pl.BlockSpec(b2.shape, lambda i, *_: (0, 0)),
                  pl.BlockSpec(memory_space=pl.ANY)],
        out_specs=row(D_MODEL),
        scratch_shapes=[pltpu.VMEM((FINAL_SLOTS, TOP_K * tm * SUBLANES, LANES), U32),
                        pltpu.SemaphoreType.DMA((FINAL_SLOTS,))],
    )
    return pl.pallas_call(
        _final_body,
        grid_spec=grid_spec,
        out_shape=jax.ShapeDtypeStruct((N_TOK, D_MODEL), F32),
        compiler_params=_cparams(("arbitrary",)),
        name="final",
    )(pos_rows, x1, mod3, route, g2, b2, ys)


def kernel(x, c, positions, w_ada, b_ada, w_in, q_norm_g, w_uq, kv_norm_g, w_ukv, sgu_norm_g, sgu_norm_b,
           w_spatial, b_spatial, w_o, ln1_g, ln1_b, w_router_group, b_router_group, w_router_expert,
           b_router_expert, w_gate, w_up, w_down, ln2_g, ln2_b):
    l = 0
    x2 = x.reshape(N_TOK, D_MODEL)
    mod3 = _ada(c, w_ada[l], b_ada[l][None, :]).reshape(BATCH, N_MOD, D_MODEL)

    wt, wuq, wukv, wo = _prep(w_in[l].T, w_uq[l], w_ukv[l], w_o[l])
    n_r = N_GROUPS + N_EXPERTS
    wr = jnp.pad(jnp.concatenate([w_router_group[l], w_router_expert[l]], axis=1),
                 ((0, 0), (0, LANES - n_r))).astype(BF16)
    br = jnp.pad(jnp.concatenate([b_router_group[l], b_router_expert[l]]), (0, LANES - n_r))[None, :]
    inv_freq = 1.0 / (ROPE_THETA ** (jnp.arange(0, QK_ROPE_DIM, 2, dtype=F32) / QK_ROPE_DIM))
    invf = jnp.tile(inv_freq, 2 * LANES // QK_ROPE_DIM)[None, :]

    cqn, ckvn, kpe, u, vs = _inproj(x2, mod3, wt, q_norm_g[l][None, :], kv_norm_g[l][None, :],
                                    sgu_norm_g[l][None, :], sgu_norm_b[l][None, :])
    q, k, v = _qkv(cqn, ckvn, kpe, positions.reshape(N_TOK, 1), invf, wuq, wukv)
    attn = _attn(q, k, v).reshape(N_TOK, MLA_WIDTH)
    x1, h2, logits = _mixout(x2, mod3, attn, u, vs, w_spatial[l], b_spatial[l].T, wo,
                             ln1_g[l][None, :], ln1_b[l][None, :], wr, br)
    route, counts = _route(logits)
    pos_tab, tile_tab = _plan(route, counts)
    pos_rows = pos_tab[:, 0:TOP_K].reshape(-1)
    xs = _dispatch(pos_rows, tile_tab, h2)
    ys = _moe(tile_tab, xs, w_gate[l], w_up[l], w_down[l])
    out = _final(pos_rows, x1, mod3, route, ln2_g[l][None, :], ln2_b[l][None, :], ys)
    return out.reshape(BATCH, SEQ, D_MODEL)
```

```python
import jax
import jax.numpy as jnp
import numpy as np
from jax import lax
from jax.experimental import pallas as pl
from jax.experimental.pallas import tpu as pltpu

D_MODEL = 2048
BATCH = 4
SEQ = 2048
N_TOK = BATCH * SEQ

MLA_HEADS = 8
QK_NOPE_DIM = 128
QK_ROPE_DIM = 64
QK_DIM = QK_NOPE_DIM + QK_ROPE_DIM
V_HEAD_DIM = 128
Q_LORA_RANK = 768
KV_LORA_RANK = 512
ROPE_THETA = 10000.0
MLA_WIDTH = MLA_HEADS * V_HEAD_DIM

SGU_GROUPS = 8
SGU_GROUP_DIM = 128
SGU_CHUNK = 128
SGU_WIDTH = SGU_GROUPS * SGU_GROUP_DIM

N_GROUPS = 4
EXPERTS_PER_GROUP = 8
N_EXPERTS = N_GROUPS * EXPERTS_PER_GROUP
TOP_K = 2
EXPERT_FF = 512

DEEPNORM_ALPHA = 2.0 ** 0.25
EPS = 1e-6
N_MOD = 6
NEG_BIG = -1e30

LANES = 128
SUBLANES = 8
VMEM_LIMIT = 56 * 1024 * 1024

ADA_TN = 1024
TOK_TM = 512
MIX_TM = 256
PREP_STEPS = 4
ATT_TQ = 1024
ATT_TK = 1024
ATT_HEADS = 4
MOE_TM = 256
MOE_TILES = (N_TOK * TOP_K + N_EXPERTS * (MOE_TM - 1)) // MOE_TM + 1
MOE_ROWS = MOE_TILES * MOE_TM
PLAN_TM = 2048
ROUTE_TM = 1024
RT_RANK = 4
DISPATCH_TM = 1024
FINAL_TM = 256
FINAL_CHUNK = 128
FINAL_SLOTS = 3
assert MOE_TILES <= LANES
T_EXP, T_VALID, T_FIRST, T_CLEAR, T_NEXT, T_NEXT2, T_NEXT3, T_ORD, T_LAST = range(9)
MOE_SLOTS = 3
WEIGHT_DMA_PRIORITY = 1

F32 = jnp.float32
BF16 = jnp.bfloat16
U32 = jnp.uint32
HALF_D = D_MODEL // 2


def _cparams(sem):
    return pltpu.CompilerParams(dimension_semantics=sem, vmem_limit_bytes=VMEM_LIMIT)


def _const_spec(shape):
    nd = len(shape)
    return pl.BlockSpec(shape, lambda *_: (0,) * nd, pipeline_mode=pl.Buffered(1))


def _ln_rows(x):
    mu = jnp.mean(x, axis=-1, keepdims=True)
    xc = x - mu
    var = jnp.mean(xc * xc, axis=-1, keepdims=True)
    return xc * lax.rsqrt(var + EPS)


def _rms_rows(x):
    return x * lax.rsqrt(jnp.mean(x * x, axis=-1, keepdims=True) + EPS)


def _pack_halves(x):
    half = x.shape[-1] // 2
    return pltpu.pack_elementwise([x[:, :half], x[:, half:]], packed_dtype=BF16)


def _unpack_halves(w):
    lo = pltpu.unpack_elementwise(w, index=0, packed_dtype=BF16, unpacked_dtype=F32)
    hi = pltpu.unpack_elementwise(w, index=1, packed_dtype=BF16, unpacked_dtype=F32)
    return lo, hi


def _store_token_tiles(ref, w):
    rows = w.shape[0]
    for s in range(SUBLANES):
        ref[pl.ds(s, rows, stride=SUBLANES), :] = w[:, s * LANES:(s + 1) * LANES]


def _load_token_tiles(ref, start_row, rows):
    return jnp.concatenate([ref[pl.ds(start_row * SUBLANES + s, rows, stride=SUBLANES), :]
                            for s in range(SUBLANES)], axis=1)


def _gelu_tanh(x):
    c = np.sqrt(2.0 / np.pi).astype(np.float32)
    return 0.5 * x * (1.0 + jnp.tanh(c * (x + 0.044715 * (x * x * x))))


def _ada_body(c_ref, w_ref, b_ref, o_ref):
    o_ref[...] = jnp.dot(c_ref[...].astype(BF16), w_ref[...].astype(BF16),
                         preferred_element_type=F32) + b_ref[...]


def _ada(c, w, b):
    n = w.shape[1]
    return pl.pallas_call(
        _ada_body,
        grid=(n // ADA_TN,),
        in_specs=[pl.BlockSpec((BATCH, D_MODEL), lambda j: (0, 0)),
                  pl.BlockSpec((D_MODEL, ADA_TN), lambda j: (0, j)),
                  pl.BlockSpec((1, ADA_TN), lambda j: (0, j))],
        out_specs=pl.BlockSpec((BATCH, ADA_TN), lambda j: (0, j)),
        out_shape=jax.ShapeDtypeStruct((BATCH, n), F32),
        compiler_params=_cparams(("parallel",)),
        name="ada",
    )(c, w, b)


def _prep_body(win_ref, wuq_ref, wukv_ref, wo_ref, win_o, wuq_o, wukv_o, wo_o):
    win_o[...] = win_ref[...].astype(BF16)
    u = wuq_ref[...]
    nope = [u[:, h * QK_DIM:h * QK_DIM + QK_NOPE_DIM] for h in range(MLA_HEADS)]
    rope = [u[:, h * QK_DIM + QK_NOPE_DIM:(h + 1) * QK_DIM] for h in range(MLA_HEADS)]
    wuq_o[...] = jnp.concatenate(nope + rope, axis=1).astype(BF16)
    kv = wukv_ref[...]
    hw = QK_NOPE_DIM + V_HEAD_DIM
    kn = [kv[:, h * hw:h * hw + QK_NOPE_DIM] for h in range(MLA_HEADS)]
    vv = [kv[:, h * hw + QK_NOPE_DIM:(h + 1) * hw] for h in range(MLA_HEADS)]
    wukv_o[...] = jnp.concatenate(kn + vv, axis=1).astype(BF16)
    wo_o[...] = wo_ref[...].astype(BF16)


def _prep(w_in_t, w_uq, w_ukv, w_o):
    steps = PREP_STEPS
    blk = lambda a: pl.BlockSpec((a.shape[0] // steps, a.shape[1]), lambda i: (i, 0))
    ins = (w_in_t, w_uq, w_ukv, w_o)
    return pl.pallas_call(
        _prep_body,
        grid=(steps,),
        in_specs=[blk(a) for a in ins],
        out_specs=[blk(a) for a in ins],
        out_shape=[jax.ShapeDtypeStruct(a.shape, BF16) for a in ins],
        compiler_params=_cparams(("parallel",)),
        name="prep",
    )(*ins)


def _inproj_body(x_ref, mod_ref, wt_ref, gq_ref, gkv_ref, sg_ref, sb_ref,
                 cq_ref, ckv_ref, kpe_ref, u_ref, vs_ref):
    o1, o2, o3 = Q_LORA_RANK, Q_LORA_RANK + KV_LORA_RANK, Q_LORA_RANK + KV_LORA_RANK + QK_ROPE_DIM

    def proj(lo, hi):
        return lax.dot_general(h, wt_ref[lo:hi, :], (((1,), (1,)), ((), ())), preferred_element_type=F32)

    sh = mod_ref[0, 0:1, :]
    sc = mod_ref[0, 1:2, :]
    h = (_ln_rows(x_ref[...]) * (1.0 + sc) + sh).astype(BF16)
    cq_ref[...] = (_rms_rows(proj(0, o1)) * gq_ref[...]).astype(BF16)
    ckv_ref[...] = (_rms_rows(proj(o1, o2)) * gkv_ref[...]).astype(BF16)
    kpe_ref[...] = proj(o2, o2 + LANES)
    gz = _gelu_tanh(proj(o3, o3 + 2 * SGU_WIDTH))
    u_ref[...] = gz[:, :SGU_WIDTH]
    vs_ref[...] = (_ln_rows(gz[:, SGU_WIDTH:]) * sg_ref[...] + sb_ref[...]).astype(BF16)


def _inproj(x2, mod3, wt, gq, gkv, sg, sb):
    tm = TOK_TM
    tiles_per_batch = SEQ // tm
    row = lambda w: pl.BlockSpec((tm, w), lambda i: (i, 0))
    return pl.pallas_call(
        _inproj_body,
        grid=(N_TOK // tm,),
        in_specs=[row(D_MODEL),
                  pl.BlockSpec((1, N_MOD, D_MODEL), lambda i: (i // tiles_per_batch, 0, 0)),
                  _const_spec(wt.shape),
                  _const_spec(gq.shape), _const_spec(gkv.shape), _const_spec(sg.shape), _const_spec(sb.shape)],
        out_specs=[row(Q_LORA_RANK), row(KV_LORA_RANK), row(LANES), row(SGU_WIDTH), row(SGU_WIDTH)],
        out_shape=[jax.ShapeDtypeStruct((N_TOK, Q_LORA_RANK), BF16),
                   jax.ShapeDtypeStruct((N_TOK, KV_LORA_RANK), BF16),
                   jax.ShapeDtypeStruct((N_TOK, LANES), F32),
                   jax.ShapeDtypeStruct((N_TOK, SGU_WIDTH), F32),
                   jax.ShapeDtypeStruct((N_TOK, SGU_WIDTH), BF16)],
        compiler_params=_cparams(("parallel",)),
        name="inproj",
    )(x2, mod3, wt, gq, gkv, sg, sb)


def _rope(x, cos, sin):
    w = x.shape[-1]
    lane = lax.broadcasted_iota(jnp.int32, x.shape, 1)
    first_half = (lane % QK_ROPE_DIM) < (QK_ROPE_DIM // 2)
    rot = jnp.where(first_half,
                    -pltpu.roll(x, w - QK_ROPE_DIM // 2, 1),
                    pltpu.roll(x, QK_ROPE_DIM // 2, 1))
    return x * cos + rot * sin


def _qkv_body(cq_ref, ckv_ref, kpe_ref, pos_ref, invf_ref, wuq_ref, wukv_ref, q_ref, k_ref, v_ref):
    ang = pos_ref[...].astype(F32) * invf_ref[...]
    cos1 = jnp.cos(ang)
    sin1 = jnp.sin(ang)
    reps = MLA_HEADS * QK_ROPE_DIM // LANES
    cos = jnp.concatenate([cos1] * reps, axis=1)
    sin = jnp.concatenate([sin1] * reps, axis=1)
    scale = np.float32(QK_DIM ** -0.5)
    q = jnp.dot(cq_ref[...], wuq_ref[...], preferred_element_type=F32) * scale
    q_pe = _rope(q[:, MLA_HEADS * QK_NOPE_DIM:], cos, sin)
    kv = jnp.dot(ckv_ref[...], wukv_ref[...], preferred_element_type=F32)
    k_pe = _rope(kpe_ref[...], cos1, sin1)[:, :QK_ROPE_DIM].astype(BF16)
    for h in range(MLA_HEADS):
        q_ref[0, h, :, 0:QK_NOPE_DIM] = q[:, h * QK_NOPE_DIM:(h + 1) * QK_NOPE_DIM].astype(BF16)
        q_ref[0, h, :, QK_NOPE_DIM:QK_DIM] = q_pe[:, h * QK_ROPE_DIM:(h + 1) * QK_ROPE_DIM].astype(BF16)
        k_ref[0, h, :, 0:QK_NOPE_DIM] = kv[:, h * QK_NOPE_DIM:(h + 1) * QK_NOPE_DIM].astype(BF16)
        k_ref[0, h, :, QK_NOPE_DIM:QK_DIM] = k_pe
        v_ref[0, h, :, :] = kv[:, MLA_WIDTH + h * V_HEAD_DIM:MLA_WIDTH + (h + 1) * V_HEAD_DIM].astype(BF16)


def _qkv(cqn, ckvn, kpe, pos2, invf, wuq, wukv):
    tm = TOK_TM
    tpb = SEQ // tm
    row = lambda w: pl.BlockSpec((tm, w), lambda i: (i, 0))
    head_out = lambda w: pl.BlockSpec((1, MLA_HEADS, tm, w), lambda i: (i // tpb, 0, i % tpb, 0))
    return pl.pallas_call(
        _qkv_body,
        grid=(N_TOK // tm,),
        in_specs=[row(Q_LORA_RANK), row(KV_LORA_RANK), row(LANES), row(1),
                  _const_spec(invf.shape), _const_spec(wuq.shape), _const_spec(wukv.shape)],
        out_specs=[head_out(QK_DIM), head_out(QK_DIM), head_out(V_HEAD_DIM)],
        out_shape=[jax.ShapeDtypeStruct((BATCH, MLA_HEADS, SEQ, QK_DIM), BF16),
                   jax.ShapeDtypeStruct((BATCH, MLA_HEADS, SEQ, QK_DIM), BF16),
                   jax.ShapeDtypeStruct((BATCH, MLA_HEADS, SEQ, V_HEAD_DIM), BF16)],
        compiler_params=_cparams(("parallel",)),
        name="qkv",
    )(cqn, ckvn, kpe, pos2, invf, wuq, wukv)


def _attn_body(q_ref, k_ref, v_ref, o_ref):
    i = pl.program_id(2)

    def step(h, j, carry, masked):
        m, l, acc = carry
        start = pl.multiple_of(j * ATT_TK, ATT_TK)
        k = k_ref[0, h, pl.ds(start, ATT_TK), :]
        v = v_ref[0, h, pl.ds(start, ATT_TK), :]
        s = lax.dot_general(q_ref[0, h], k, (((1,), (1,)), ((), ())), preferred_element_type=F32)
        if masked:
            r = lax.broadcasted_iota(jnp.int32, s.shape, 0)
            c = lax.broadcasted_iota(jnp.int32, s.shape, 1)
            s = jnp.where(c <= r, s, NEG_BIG)
        m_new = jnp.maximum(m, jnp.max(s, axis=-1, keepdims=True))
        p = jnp.exp(s - m_new)
        a = jnp.exp(m - m_new)
        l = a * l + jnp.sum(p, axis=-1, keepdims=True)
        acc = a * acc + jnp.dot(p.astype(BF16), v, preferred_element_type=F32)
        return m_new, l, acc

    def steps(j, carries, masked):
        return tuple(step(h, j, carries[h], masked) for h in range(ATT_HEADS))

    init = tuple((jnp.full((ATT_TQ, 1), NEG_BIG, F32), jnp.zeros((ATT_TQ, 1), F32),
                  jnp.zeros((ATT_TQ, V_HEAD_DIM), F32)) for _ in range(ATT_HEADS))
    carries = lax.fori_loop(0, i, lambda j, c: steps(j, c, False), init)
    carries = steps(i, carries, True)
    for h, (m, l, acc) in enumerate(carries):
        o_ref[0, :, h * V_HEAD_DIM:(h + 1) * V_HEAD_DIM] = (acc / l).astype(BF16)


def _attn(q, k, v):
    assert ATT_TQ == ATT_TK
    hb = ATT_HEADS
    return pl.pallas_call(
        _attn_body,
        grid=(BATCH, MLA_HEADS // hb, SEQ // ATT_TQ),
        in_specs=[pl.BlockSpec((1, hb, ATT_TQ, QK_DIM), lambda b, h, i: (b, h, i, 0)),
                  pl.BlockSpec((1, hb, SEQ, QK_DIM), lambda b, h, i: (b, h, 0, 0)),
                  pl.BlockSpec((1, hb, SEQ, V_HEAD_DIM), lambda b, h, i: (b, h, 0, 0))],
        out_specs=pl.BlockSpec((1, ATT_TQ, hb * V_HEAD_DIM), lambda b, h, i: (b, i, h)),
        out_shape=jax.ShapeDtypeStruct((BATCH, SEQ, MLA_WIDTH), BF16),
        compiler_params=_cparams(("parallel", "parallel", "arbitrary")),
        name="attn",
    )(q, k, v)


def _mixout_body(x_ref, mod_ref, attn_ref, u_ref, vs_ref, wsp_ref, bsp_ref, woa_ref, wos_ref,
                 g1_ref, b1_ref, wr_ref, br_ref, x1_ref, h2_ref, lg_ref, sgu_scr):
    r = lax.broadcasted_iota(jnp.int32, (SGU_CHUNK, SGU_CHUNK), 0)
    c = lax.broadcasted_iota(jnp.int32, (SGU_CHUNK, SGU_CHUNK), 1)
    causal = c <= r
    for g in range(SGU_GROUPS):
        ws = jnp.where(causal, wsp_ref[g], 0.0).astype(BF16)
        bias = bsp_ref[:, g:g + 1]
        cols = slice(g * SGU_GROUP_DIM, (g + 1) * SGU_GROUP_DIM)
        for ch in range(MIX_TM // SGU_CHUNK):
            rows = slice(ch * SGU_CHUNK, (ch + 1) * SGU_CHUNK)
            mixed = jnp.dot(ws, vs_ref[rows, cols], preferred_element_type=F32) + bias
            sgu_scr[rows, cols] = (u_ref[rows, cols] * mixed).astype(BF16)
    y = (jnp.dot(attn_ref[...], woa_ref[...], preferred_element_type=F32)
         + jnp.dot(sgu_scr[...], wos_ref[...], preferred_element_type=F32))
    gate1 = mod_ref[0, 2:3, :]
    sh2 = mod_ref[0, 3:4, :]
    sc2 = mod_ref[0, 4:5, :]
    x1 = _ln_rows(DEEPNORM_ALPHA * x_ref[...] + gate1 * y) * g1_ref[...] + b1_ref[...]
    x1_ref[...] = x1
    h2 = _ln_rows(x1) * (1.0 + sc2) + sh2
    _store_token_tiles(h2_ref, _pack_halves(h2))
    lg_ref[...] = jnp.dot(h2.astype(BF16), wr_ref[...], preferred_element_type=F32) + br_ref[...]


def _mixout(x2, mod3, attn, u, vs, wsp, bsp_t, wo, g1, b1, wr, br):
    tm = MIX_TM
    tpb = SEQ // tm
    row = lambda w: pl.BlockSpec((tm, w), lambda i: (i, 0))
    wo_half = lambda j: pl.BlockSpec((MLA_WIDTH, D_MODEL), lambda i: (j, 0), pipeline_mode=pl.Buffered(1))
    return pl.pallas_call(
        _mixout_body,
        grid=(N_TOK // tm,),
        in_specs=[row(D_MODEL),
                  pl.BlockSpec((1, N_MOD, D_MODEL), lambda i: (i // tpb, 0, 0)),
                  row(MLA_WIDTH), row(SGU_WIDTH), row(SGU_WIDTH),
                  _const_spec(wsp.shape), _const_spec(bsp_t.shape), wo_half(0), wo_half(1),
                  _const_spec(g1.shape), _const_spec(b1.shape), _const_spec(wr.shape), _const_spec(br.shape)],
        out_specs=[row(D_MODEL), pl.BlockSpec((tm * SUBLANES, LANES), lambda i: (i, 0)), row(LANES)],
        out_shape=[jax.ShapeDtypeStruct((N_TOK, D_MODEL), F32),
                   jax.ShapeDtypeStruct((N_TOK * SUBLANES, LANES), U32),
                   jax.ShapeDtypeStruct((N_TOK, LANES), F32)],
        scratch_shapes=[pltpu.VMEM((tm, SGU_WIDTH), BF16)],
        compiler_params=_cparams(("parallel",)),
        name="mix_out",
    )(x2, mod3, attn, u, vs, wsp, bsp_t, wo, wo, g1, b1, wr, br)


def _route_math(lg):
    lane = lax.broadcasted_iota(jnp.int32, lg.shape, 1)
    big = jnp.int32(LANES)

    def top1(vals):
        m = jnp.max(vals, axis=-1, keepdims=True)
        idx = jnp.min(jnp.where(vals == m, lane, big), axis=-1, keepdims=True)
        return m, idx

    is_group = lane < N_GROUPS
    glog = jnp.where(is_group, lg, -jnp.inf)
    gmax, gidx = top1(glog)
    pg_top = 1.0 / jnp.sum(jnp.exp(glog - gmax), axis=-1, keepdims=True)
    eid = lane - N_GROUPS
    sel = (eid >= gidx * EXPERTS_PER_GROUP) & (eid < (gidx + 1) * EXPERTS_PER_GROUP)
    elog = jnp.where(sel, lg, -jnp.inf)
    m1, i1 = top1(elog)
    m2, i2 = top1(jnp.where(lane == i1, -jnp.inf, elog))
    e2 = jnp.exp(m2 - m1)
    w1 = pg_top / (1.0 + e2)
    w2 = pg_top * e2 / (1.0 + e2)
    return jnp.where(lane == 0, (i1 - N_GROUPS).astype(F32),
                     jnp.where(lane == 1, (i2 - N_GROUPS).astype(F32),
                               jnp.where(lane == 2, w1, jnp.where(lane == 3, w2, 0.0))))


def _rank_math(rt, counts):
    t = rt.shape[0]
    lane = lax.broadcasted_iota(jnp.int32, (t, LANES), 1).astype(F32)
    oh0 = lane == rt[:, 0:1]
    oh1 = lane == rt[:, 1:2]
    s = jnp.where(oh0 | oh1, 1.0, 0.0)
    r = lax.broadcasted_iota(jnp.int32, (t, t), 0)
    c = lax.broadcasted_iota(jnp.int32, (t, t), 1)
    before = jnp.where(c < r, 1.0, 0.0).astype(BF16)
    csum = jnp.dot(before, s.astype(BF16), preferred_element_type=F32) + counts
    rank0 = jnp.sum(jnp.where(oh0, csum, 0.0), axis=-1, keepdims=True)
    rank1 = jnp.sum(jnp.where(oh1, csum, 0.0), axis=-1, keepdims=True)
    return rank0, rank1, counts + jnp.sum(s, axis=0, keepdims=True)


def _route_body(lg_ref, rt_ref, cnt_ref, cnt_scr):
    @pl.when(pl.program_id(0) == 0)
    def _():
        cnt_scr[...] = jnp.zeros_like(cnt_scr)

    rt = _route_math(lg_ref[...])
    rank0, rank1, counts = _rank_math(rt, cnt_scr[...])
    cnt_scr[...] = counts
    lane = lax.broadcasted_iota(jnp.int32, rt.shape, 1)
    rt_ref[...] = jnp.where(lane == RT_RANK, rank0, jnp.where(lane == RT_RANK + 1, rank1, rt))
    cnt_ref[...] = jnp.broadcast_to(counts, cnt_ref.shape)


def _route(logits):
    tm = ROUTE_TM
    return pl.pallas_call(
        _route_body,
        grid=(N_TOK // tm,),
        in_specs=[pl.BlockSpec((tm, LANES), lambda i: (i, 0))],
        out_specs=[pl.BlockSpec((tm, LANES), lambda i: (i, 0)),
                   pl.BlockSpec((SUBLANES, LANES), lambda i: (0, 0))],
        out_shape=[jax.ShapeDtypeStruct((N_TOK, LANES), F32),
                   jax.ShapeDtypeStruct((SUBLANES, LANES), F32)],
        scratch_shapes=[pltpu.VMEM((1, LANES), F32)],
        compiler_params=_cparams(("arbitrary",)),
        name="route",
    )(logits)


def _plan_body(rt_ref, cnt_ref, pos_ref, tt_ref):
    t = PLAN_TM
    lane = lax.broadcasted_iota(jnp.int32, (t, LANES), 1)
    rt = rt_ref[...]
    oh0 = lane.astype(F32) == rt[:, 0:1]
    oh1 = lane.astype(F32) == rt[:, 1:2]
    counts = cnt_ref[0:1, :]
    tiles = jnp.floor((counts + (MOE_TM - 1)) * (1.0 / MOE_TM))
    r = lax.broadcasted_iota(jnp.int32, (LANES, LANES), 0)
    c = lax.broadcasted_iota(jnp.int32, (LANES, LANES), 1)
    upto = jnp.where(r <= c, 1.0, 0.0).astype(BF16)
    tiles8 = jnp.broadcast_to(tiles, (SUBLANES, LANES)).astype(BF16)
    tile_end = jnp.dot(tiles8, upto, preferred_element_type=F32)[0:1]
    offs = (tile_end - tiles) * MOE_TM
    p0 = jnp.sum(jnp.where(oh0, offs, 0.0), axis=-1, keepdims=True) + rt[:, RT_RANK:RT_RANK + 1]
    p1 = jnp.sum(jnp.where(oh1, offs, 0.0), axis=-1, keepdims=True) + rt[:, RT_RANK + 1:RT_RANK + 2]
    pos_ref[...] = (jnp.where(lane == 0, p0, jnp.where(lane == 1, p1, 0.0)) * SUBLANES).astype(jnp.int32)
    tt_ref[...] = _tile_table(tiles, tile_end)


def _tile_table(tiles, tile_end):
    lane_e = lax.broadcasted_iota(jnp.int32, (LANES, LANES), 1)
    lane_f = lane_e.astype(F32)
    tile_id = lax.broadcasted_iota(jnp.int32, (LANES, LANES), 0).astype(F32)
    is_e = lane_e < N_EXPERTS
    owns = is_e & (tiles > 0)

    def count(cond):
        return jnp.sum(jnp.where(cond, 1.0, 0.0), axis=-1, keepdims=True)

    def first_owner_after(e):
        return jnp.min(jnp.where(owns & (lane_f > e), lane_f, jnp.float32(LANES)), axis=-1, keepdims=True)

    total = jnp.max(tile_end, axis=-1, keepdims=True)
    t_valid = jnp.where(tile_id[:, 0:1] < total, 1.0, 0.0)
    t_exp = jnp.where(t_valid > 0, count(is_e & (tile_end <= tile_id)), count(is_e & (tile_end <= total - 1.0)))
    t_first = count(owns & ((tile_end - tiles) == tile_id))
    t_clear = jnp.maximum(count(owns & ((tile_end - 1.0) == tile_id)), 1.0 - t_valid)
    t_next = first_owner_after(t_exp)
    t_next2 = first_owner_after(t_next)
    t_next3 = first_owner_after(t_next2)
    cols = {T_EXP: t_exp, T_VALID: t_valid, T_FIRST: t_first, T_CLEAR: t_clear, T_NEXT: t_next,
            T_NEXT2: t_next2, T_NEXT3: t_next3, T_ORD: count(owns & (lane_f < t_exp)), T_LAST: total - 1.0}
    table = jnp.zeros((LANES, LANES), F32)
    for k, col in cols.items():
        table = jnp.where(lane_e == k, col, table)
    return table.astype(jnp.int32)


def _plan(route, counts):
    t = PLAN_TM
    return pl.pallas_call(
        _plan_body,
        grid=(N_TOK // t,),
        in_specs=[pl.BlockSpec((t, LANES), lambda i: (i, 0)),
                  pl.BlockSpec((SUBLANES, LANES), lambda i: (0, 0))],
        out_specs=[pl.BlockSpec((t, LANES), lambda i: (i, 0)),
                   pl.BlockSpec((LANES, LANES), lambda i: (0, 0))],
        out_shape=[jax.ShapeDtypeStruct((N_TOK, LANES), jnp.int32),
                   jax.ShapeDtypeStruct((LANES, LANES), jnp.int32)],
        compiler_params=_cparams(("arbitrary",)),
        name="plan",
    )(route, counts)


def _rows_wait(ref, n_rows, sem):
    pltpu.make_async_copy(ref.at[pl.ds(0, n_rows)], ref.at[pl.ds(0, n_rows)], sem).wait()


def _dispatch_body(pos_ref, tt_ref, h_ref, xs_hbm, zbuf, sem_z, sem_s):
    i = pl.program_id(0)
    tile_rows = MOE_TM * SUBLANES

    @pl.when(i == 0)
    def _():
        zbuf[...] = _pack_halves(jnp.zeros((tile_rows, 2 * LANES), F32))

        def zero_copy(tile):
            start = pl.multiple_of(tile * tile_rows, tile_rows)
            return pltpu.make_async_copy(zbuf, xs_hbm.at[pl.ds(start, tile_rows)], sem_z)

        def clear_start(tile, carry):
            @pl.when(tt_ref[tile, T_CLEAR] > 0)
            def _():
                zero_copy(tile).start()
            return carry

        def clear_wait(tile, carry):
            @pl.when(tt_ref[tile, T_CLEAR] > 0)
            def _():
                zero_copy(tile).wait()
            return carry

        lax.fori_loop(0, MOE_TILES, clear_start, 0)
        lax.fori_loop(0, MOE_TILES, clear_wait, 0)

    def tok(j, carry):
        src = h_ref.at[pl.ds(pl.multiple_of(j * SUBLANES, SUBLANES), SUBLANES)]
        pair = TOP_K * (i * DISPATCH_TM + j)
        for k in range(TOP_K):
            dst_row = pl.multiple_of(pos_ref[pair + k], SUBLANES)
            pltpu.make_async_copy(src, xs_hbm.at[pl.ds(dst_row, SUBLANES)], sem_s).start(priority=k)
        return carry

    lax.fori_loop(0, DISPATCH_TM, tok, 0, unroll=8)
    _rows_wait(xs_hbm, TOP_K * DISPATCH_TM * SUBLANES, sem_s)


def _dispatch(pos_rows, tile_clear, h2p):
    grid_spec = pltpu.PrefetchScalarGridSpec(
        num_scalar_prefetch=2,
        grid=(N_TOK // DISPATCH_TM,),
        in_specs=[pl.BlockSpec((DISPATCH_TM * SUBLANES, LANES), lambda i, *_: (i, 0))],
        out_specs=pl.BlockSpec(memory_space=pl.ANY),
        scratch_shapes=[pltpu.VMEM((MOE_TM * SUBLANES, LANES), U32),
                        pltpu.SemaphoreType.DMA(()), pltpu.SemaphoreType.DMA(())],
    )
    return pl.pallas_call(
        _dispatch_body,
        grid_spec=grid_spec,
        out_shape=jax.ShapeDtypeStruct((MOE_ROWS * SUBLANES, LANES), U32),
        compiler_params=_cparams(("arbitrary",)),
        name="dispatch",
    )(pos_rows, tile_clear, h2p)


def _moe_body(tt_ref, x_ref, wg_hbm, wu_hbm, wd_hbm, y_ref,
              wg_s, wu_s, wd_s, stg_g, stg_u, stg_d, sem):
    i = pl.program_id(0)
    ahead = (T_EXP, T_NEXT, T_NEXT2, T_NEXT3)

    def fetch(e, slot):
        return (pltpu.make_async_copy(wg_hbm.at[e], stg_g.at[slot], sem.at[slot, 0]),
                pltpu.make_async_copy(wu_hbm.at[e], stg_u.at[slot], sem.at[slot, 1]),
                pltpu.make_async_copy(wd_hbm.at[e], stg_d.at[slot], sem.at[slot, 2]))

    @pl.when(i == 0)
    def _():
        for d in range(MOE_SLOTS):
            @pl.when(tt_ref[0, ahead[d]] < N_EXPERTS)
            def _():
                for cp in fetch(tt_ref[0, ahead[d]], d):
                    cp.start(priority=WEIGHT_DMA_PRIORITY)

    @pl.when(tt_ref[i, T_FIRST] > 0)
    def _():
        slot = tt_ref[i, T_ORD] % MOE_SLOTS
        for cp in fetch(tt_ref[i, T_EXP], slot):
            cp.wait()
        wg_s[...] = stg_g[slot].astype(BF16)
        wu_s[...] = stg_u[slot].astype(BF16)
        wd_s[...] = stg_d[slot].astype(BF16)

        @pl.when(tt_ref[i, ahead[MOE_SLOTS]] < N_EXPERTS)
        def _():
            for cp in fetch(tt_ref[i, ahead[MOE_SLOTS]], slot):
                cp.start(priority=WEIGHT_DMA_PRIORITY)

    @pl.when(tt_ref[i, T_VALID] > 0)
    def _():
        lo, hi = _unpack_halves(_load_token_tiles(x_ref, 0, MOE_TM))
        xa = lo.astype(BF16)
        xb = hi.astype(BF16)
        g = (jnp.dot(xa, wg_s[:HALF_D, :], preferred_element_type=F32)
             + jnp.dot(xb, wg_s[HALF_D:, :], preferred_element_type=F32))
        u = (jnp.dot(xa, wu_s[:HALF_D, :], preferred_element_type=F32)
             + jnp.dot(xb, wu_s[HALF_D:, :], preferred_element_type=F32))
        hid = (g * jax.nn.sigmoid(g) * u).astype(BF16)
        _store_token_tiles(y_ref, _pack_halves(jnp.dot(hid, wd_s[...], preferred_element_type=F32)))

    @pl.when(tt_ref[i, T_VALID] == 0)
    def _():
        y_ref[...] = _pack_halves(jnp.zeros((MOE_TM * SUBLANES, 2 * LANES), F32))


def _moe(tile_tab, xs, wg, wu, wd):
    tm = MOE_TM
    grid_spec = pltpu.PrefetchScalarGridSpec(
        num_scalar_prefetch=1,
        grid=(MOE_TILES,),
        in_specs=[pl.BlockSpec((tm * SUBLANES, LANES), lambda i, tt: (jnp.minimum(i, tt[0, T_LAST]), 0)),
                  pl.BlockSpec(memory_space=pl.ANY), pl.BlockSpec(memory_space=pl.ANY),
                  pl.BlockSpec(memory_space=pl.ANY)],
        out_specs=pl.BlockSpec((tm * SUBLANES, LANES), lambda i, *_: (i, 0)),
        scratch_shapes=[pltpu.VMEM((D_MODEL, EXPERT_FF), BF16), pltpu.VMEM((D_MODEL, EXPERT_FF), BF16),
                        pltpu.VMEM((EXPERT_FF, D_MODEL), BF16),
                        pltpu.VMEM((MOE_SLOTS, D_MODEL, EXPERT_FF), F32),
                        pltpu.VMEM((MOE_SLOTS, D_MODEL, EXPERT_FF), F32),
                        pltpu.VMEM((MOE_SLOTS, EXPERT_FF, D_MODEL), F32),
                        pltpu.SemaphoreType.DMA((MOE_SLOTS, 3))],
    )
    return pl.pallas_call(
        _moe_body,
        grid_spec=grid_spec,
        out_shape=jax.ShapeDtypeStruct((MOE_ROWS * SUBLANES, LANES), U32),
        compiler_params=_cparams(("arbitrary",)),
        name="moe",
    )(tile_tab, xs, wg, wu, wd)


def _final_body(pos_ref, x1_ref, mod_ref, rt_ref, g2_ref, b2_ref, ys_hbm, o_ref, buf, sem):
    i = pl.program_id(0)
    n = pl.num_programs(0)
    tm = FINAL_TM
    slot = i % FINAL_SLOTS
    nxt_slot = (i + 2) % FINAL_SLOTS
    nxt_tile = jnp.minimum(i + 2, n - 1)

    def issue(tile, dst_slot, j):
        pair = TOP_K * (tile * tm + j)
        for k in range(TOP_K):
            src_row = pl.multiple_of(pos_ref[pair + k], SUBLANES)
            dst_row = pl.multiple_of((k * tm + j) * SUBLANES, SUBLANES)
            pltpu.make_async_copy(ys_hbm.at[pl.ds(src_row, SUBLANES)],
                                  buf.at[dst_slot, pl.ds(dst_row, SUBLANES)], sem.at[dst_slot]).start(priority=k)

    def wait(s):
        pltpu.make_async_copy(ys_hbm.at[pl.ds(0, TOP_K * tm * SUBLANES)], buf.at[s], sem.at[s]).wait()

    @pl.when(i == 0)
    def _():
        def tok(j, carry):
            issue(0, 0, j)
            issue(jnp.minimum(1, n - 1), 1, j)
            return carry
        lax.fori_loop(0, tm, tok, 0, unroll=8)

    wait(slot)
    gate2 = mod_ref[0, 5:6, :]
    cur = buf.at[slot]

    def chunk(c, carry):
        r0 = pl.multiple_of(c * FINAL_CHUNK, FINAL_CHUNK)
        rows = pl.ds(r0, FINAL_CHUNK)
        a_lo, a_hi = _unpack_halves(_load_token_tiles(cur, r0, FINAL_CHUNK))
        b_lo, b_hi = _unpack_halves(_load_token_tiles(cur, tm + r0, FINAL_CHUNK))
        x1 = x1_ref[rows, :]
        rt = rt_ref[rows, :]
        for r in range(FINAL_CHUNK):
            issue(nxt_tile, nxt_slot, r0 + r)
        w0 = rt[:, 2:3]
        w1 = rt[:, 3:4]
        y = jnp.concatenate([w0 * a_lo + w1 * b_lo, w0 * a_hi + w1 * b_hi], axis=1)
        o_ref[rows, :] = _ln_rows(DEEPNORM_ALPHA * x1 + gate2 * y) * g2_ref[...] + b2_ref[...]
        return carry

    lax.fori_loop(0, tm // FINAL_CHUNK, chunk, 0)

    @pl.when(i == n - 1)
    def _():
        wait((i + 1) % FINAL_SLOTS)
        wait(nxt_slot)


def _final(pos_rows, x1, mod3, route, g2, b2, ys):
    tm = FINAL_TM
    tpb = SEQ // tm
    row = lambda w: pl.BlockSpec((tm, w), lambda i, *_: (i, 0))
    grid_spec = pltpu.PrefetchScalarGridSpec(
        num_scalar_prefetch=1,
        grid=(N_TOK // tm,),
        in_specs=[row(D_MODEL),
                  pl.BlockSpec((1, N_MOD, D_MODEL), lambda i, *_: (i // tpb, 0, 0)),
                  row(LANES),
                  pl.BlockSpec(g2.shape, lambda i, *_: (0, 0)),
                  pl.BlockSpec(b2.shape, lambda i, *_: (0, 0)),
                  pl.BlockSpec(memory_space=pl.ANY)],
        out_specs=row(D_MODEL),
        scratch_shapes=[pltpu.VMEM((FINAL_SLOTS, TOP_K * tm * SUBLANES, LANES), U32),
                        pltpu.SemaphoreType.DMA((FINAL_SLOTS,))],
    )
    return pl.pallas_call(
        _final_body,
        grid_spec=grid_spec,
        out_shape=jax.ShapeDtypeStruct((N_TOK, D_MODEL), F32),
        compiler_params=_cparams(("arbitrary",)),
        name="final",
    )(pos_rows, x1, mod3, route, g2, b2, ys)


def kernel(x, c, positions, w_ada, b_ada, w_in, q_norm_g, w_uq, kv_norm_g, w_ukv, sgu_norm_g, sgu_norm_b,
           w_spatial, b_spatial, w_o, ln1_g, ln1_b, w_router_group, b_router_group, w_router_expert,
           b_router_expert, w_gate, w_up, w_down, ln2_g, ln2_b):
    l = 0
    x2 = x.reshape(N_TOK, D_MODEL)
    mod3 = _ada(c, w_ada[l], b_ada[l][None, :]).reshape(BATCH, N_MOD, D_MODEL)

    wt, wuq, wukv, wo = _prep(w_in[l].T, w_uq[l], w_ukv[l], w_o[l])
    n_r = N_GROUPS + N_EXPERTS
    wr = jnp.pad(jnp.concatenate([w_router_group[l], w_router_expert[l]], axis=1),
                 ((0, 0), (0, LANES - n_r))).astype(BF16)
    br = jnp.pad(jnp.concatenate([b_router_group[l], b_router_expert[l]]), (0, LANES - n_r))[None, :]
    inv_freq = 1.0 / (ROPE_THETA ** (jnp.arange(0, QK_ROPE_DIM, 2, dtype=F32) / QK_ROPE_DIM))
    invf = jnp.tile(inv_freq, 2 * LANES // QK_ROPE_DIM)[None, :]

    cqn, ckvn, kpe, u, vs = _inproj(x2, mod3, wt, q_norm_g[l][None, :], kv_norm_g[l][None, :],
                                    sgu_norm_g[l][None, :], sgu_norm_b[l][None, :])
    q, k, v = _qkv(cqn, ckvn, kpe, positions.reshape(N_TOK, 1), invf, wuq, wukv)
    attn = _attn(q, k, v).reshape(N_TOK, MLA_WIDTH)
    x1, h2, logits = _mixout(x2, mod3, attn, u, vs, w_spatial[l], b_spatial[l].T, wo,
                             ln1_g[l][None, :], ln1_b[l][None, :], wr, br)
    route, counts = _route(logits)
    pos_tab, tile_tab = _plan(route, counts)
    pos_rows = pos_tab[:, 0:TOP_K].reshape(-1)
    xs = _dispatch(pos_rows, tile_tab, h2)
    ys = _moe(tile_tab, xs, w_gate[l], w_up[l], w_down[l])
    out = _final(pos_rows, x1, mod3, route, ln2_g[l][None, :], ln2_b[l][None, :], ys)
    return out.reshape(BATCH, SEQ, D_MODEL)
```

```python
import jax
import jax.numpy as jnp
import numpy as np
from jax import lax
from jax.experimental import pallas as pl
from jax.experimental.pallas import tpu as pltpu

D_MODEL = 2048
BATCH = 4
SEQ = 2048
N_TOK = BATCH * SEQ

MLA_HEADS = 8
QK_NOPE_DIM = 128
QK_ROPE_DIM = 64
QK_DIM = QK_NOPE_DIM + QK_ROPE_DIM
V_HEAD_DIM = 128
Q_LORA_RANK = 768
KV_LORA_RANK = 512
ROPE_THETA = 10000.0
MLA_WIDTH = MLA_HEADS * V_HEAD_DIM

SGU_GROUPS = 8
SGU_GROUP_DIM = 128
SGU_CHUNK = 128
SGU_WIDTH = SGU_GROUPS * SGU_GROUP_DIM

N_GROUPS = 4
EXPERTS_PER_GROUP = 8
N_EXPERTS = N_GROUPS * EXPERTS_PER_GROUP
TOP_K = 2
EXPERT_FF = 512

DEEPNORM_ALPHA = 2.0 ** 0.25
EPS = 1e-6
N_MOD = 6
NEG_BIG = -1e30

LANES = 128
SUBLANES = 8
VMEM_LIMIT = 56 * 1024 * 1024

ADA_TN = 1024
TOK_TM = 512
MIX_TM = 256
PREP_STEPS = 4
ATT_TQ = 1024
ATT_TK = 1024
ATT_HEADS = 4
MOE_TM = 256
MOE_TILES = (N_TOK * TOP_K + N_EXPERTS * (MOE_TM - 1)) // MOE_TM + 1
MOE_ROWS = MOE_TILES * MOE_TM
PLAN_TM = 2048
ROUTE_TM = 1024
RT_RANK = 4
DISPATCH_TM = 1024
FINAL_TM = 256
FINAL_CHUNK = 128
FINAL_SLOTS = 3
assert MOE_TILES <= LANES
T_EXP, T_VALID, T_FIRST, T_CLEAR, T_NEXT, T_NEXT2, T_NEXT3, T_ORD, T_LAST = range(9)
MOE_SLOTS = 2
WEIGHT_DMA_PRIORITY = 1

F32 = jnp.float32
BF16 = jnp.bfloat16
U32 = jnp.uint32
HALF_D = D_MODEL // 2


def _cparams(sem):
    return pltpu.CompilerParams(dimension_semantics=sem, vmem_limit_bytes=VMEM_LIMIT)


def _const_spec(shape):
    nd = len(shape)
    return pl.BlockSpec(shape, lambda *_: (0,) * nd, pipeline_mode=pl.Buffered(1))


def _ln_rows(x):
    mu = jnp.mean(x, axis=-1, keepdims=True)
    xc = x - mu
    var = jnp.mean(xc * xc, axis=-1, keepdims=True)
    return xc * lax.rsqrt(var + EPS)


def _rms_rows(x):
    return x * lax.rsqrt(jnp.mean(x * x, axis=-1, keepdims=True) + EPS)


def _pack_halves(x):
    half = x.shape[-1] // 2
    return pltpu.pack_elementwise([x[:, :half], x[:, half:]], packed_dtype=BF16)


def _unpack_halves(w):
    lo = pltpu.unpack_elementwise(w, index=0, packed_dtype=BF16, unpacked_dtype=F32)
    hi = pltpu.unpack_elementwise(w, index=1, packed_dtype=BF16, unpacked_dtype=F32)
    return lo, hi


def _store_token_tiles(ref, w):
    rows = w.shape[0]
    for s in range(SUBLANES):
        ref[pl.ds(s, rows, stride=SUBLANES), :] = w[:, s * LANES:(s + 1) * LANES]


def _load_token_tiles(ref, start_row, rows):
    return jnp.concatenate([ref[pl.ds(start_row * SUBLANES + s, rows, stride=SUBLANES), :]
                            for s in range(SUBLANES)], axis=1)


def _gelu_tanh(x):
    c = np.sqrt(2.0 / np.pi).astype(np.float32)
    return 0.5 * x * (1.0 + jnp.tanh(c * (x + 0.044715 * (x * x * x))))


def _ada_body(c_ref, w_ref, b_ref, o_ref):
    o_ref[...] = jnp.dot(c_ref[...].astype(BF16), w_ref[...].astype(BF16),
                         preferred_element_type=F32) + b_ref[...]


def _ada(c, w, b):
    n = w.shape[1]
    return pl.pallas_call(
        _ada_body,
        grid=(n // ADA_TN,),
        in_specs=[pl.BlockSpec((BATCH, D_MODEL), lambda j: (0, 0)),
                  pl.BlockSpec((D_MODEL, ADA_TN), lambda j: (0, j)),
                  pl.BlockSpec((1, ADA_TN), lambda j: (0, j))],
        out_specs=pl.BlockSpec((BATCH, ADA_TN), lambda j: (0, j)),
        out_shape=jax.ShapeDtypeStruct((BATCH, n), F32),
        compiler_params=_cparams(("parallel",)),
        name="ada",
    )(c, w, b)


def _prep_body(win_ref, wuq_ref, wukv_ref, wo_ref, win_o, wuq_o, wukv_o, wo_o):
    win_o[...] = win_ref[...].astype(BF16)
    u = wuq_ref[...]
    nope = [u[:, h * QK_DIM:h * QK_DIM + QK_NOPE_DIM] for h in range(MLA_HEADS)]
    rope = [u[:, h * QK_DIM + QK_NOPE_DIM:(h + 1) * QK_DIM] for h in range(MLA_HEADS)]
    wuq_o[...] = jnp.concatenate(nope + rope, axis=1).astype(BF16)
    kv = wukv_ref[...]
    hw = QK_NOPE_DIM + V_HEAD_DIM
    kn = [kv[:, h * hw:h * hw + QK_NOPE_DIM] for h in range(MLA_HEADS)]
    vv = [kv[:, h * hw + QK_NOPE_DIM:(h + 1) * hw] for h in range(MLA_HEADS)]
    wukv_o[...] = jnp.concatenate(kn + vv, axis=1).astype(BF16)
    wo_o[...] = wo_ref[...].astype(BF16)


def _prep(w_in_t, w_uq, w_ukv, w_o):
    steps = PREP_STEPS
    blk = lambda a: pl.BlockSpec((a.shape[0] // steps, a.shape[1]), lambda i: (i, 0))
    ins = (w_in_t, w_uq, w_ukv, w_o)
    return pl.pallas_call(
        _prep_body,
        grid=(steps,),
        in_specs=[blk(a) for a in ins],
        out_specs=[blk(a) for a in ins],
        out_shape=[jax.ShapeDtypeStruct(a.shape, BF16) for a in ins],
        compiler_params=_cparams(("parallel",)),
        name="prep",
    )(*ins)


def _inproj_body(x_ref, mod_ref, wt_ref, gq_ref, gkv_ref, sg_ref, sb_ref,
                 cq_ref, ckv_ref, kpe_ref, u_ref, vs_ref):
    o1, o2, o3 = Q_LORA_RANK, Q_LORA_RANK + KV_LORA_RANK, Q_LORA_RANK + KV_LORA_RANK + QK_ROPE_DIM

    def proj(lo, hi):
        return lax.dot_general(h, wt_ref[lo:hi, :], (((1,), (1,)), ((), ())), preferred_element_type=F32)

    sh = mod_ref[0, 0:1, :]
    sc = mod_ref[0, 1:2, :]
    h = (_ln_rows(x_ref[...]) * (1.0 + sc) + sh).astype(BF16)
    cq_ref[...] = (_rms_rows(proj(0, o1)) * gq_ref[...]).astype(BF16)
    ckv_ref[...] = (_rms_rows(proj(o1, o2)) * gkv_ref[...]).astype(BF16)
    kpe_ref[...] = proj(o2, o2 + LANES)
    gz = _gelu_tanh(proj(o3, o3 + 2 * SGU_WIDTH))
    u_ref[...] = gz[:, :SGU_WIDTH]
    vs_ref[...] = (_ln_rows(gz[:, SGU_WIDTH:]) * sg_ref[...] + sb_ref[...]).astype(BF16)


def _inproj(x2, mod3, wt, gq, gkv, sg, sb):
    tm = TOK_TM
    tiles_per_batch = SEQ // tm
    row = lambda w: pl.BlockSpec((tm, w), lambda i: (i, 0))
    return pl.pallas_call(
        _inproj_body,
        grid=(N_TOK // tm,),
        in_specs=[row(D_MODEL),
                  pl.BlockSpec((1, N_MOD, D_MODEL), lambda i: (i // tiles_per_batch, 0, 0)),
                  _const_spec(wt.shape),
                  _const_spec(gq.shape), _const_spec(gkv.shape), _const_spec(sg.shape), _const_spec(sb.shape)],
        out_specs=[row(Q_LORA_RANK), row(KV_LORA_RANK), row(LANES), row(SGU_WIDTH), row(SGU_WIDTH)],
        out_shape=[jax.ShapeDtypeStruct((N_TOK, Q_LORA_RANK), BF16),
                   jax.ShapeDtypeStruct((N_TOK, KV_LORA_RANK), BF16),
                   jax.ShapeDtypeStruct((N_TOK, LANES), F32),
                   jax.ShapeDtypeStruct((N_TOK, SGU_WIDTH), F32),
                   jax.ShapeDtypeStruct((N_TOK, SGU_WIDTH), BF16)],
        compiler_params=_cparams(("parallel",)),
        name="inproj",
    )(x2, mod3, wt, gq, gkv, sg, sb)


def _rope(x, cos, sin):
    w = x.shape[-1]
    lane = lax.broadcasted_iota(jnp.int32, x.shape, 1)
    first_half = (lane % QK_ROPE_DIM) < (QK_ROPE_DIM // 2)
    rot = jnp.where(first_half,
                    -pltpu.roll(x, w - QK_ROPE_DIM // 2, 1),
                    pltpu.roll(x, QK_ROPE_DIM // 2, 1))
    return x * cos + rot * sin


def _qkv_body(cq_ref, ckv_ref, kpe_ref, pos_ref, invf_ref, wuq_ref, wukv_ref, q_ref, k_ref, v_ref):
    ang = pos_ref[...].astype(F32) * invf_ref[...]
    cos1 = jnp.cos(ang)
    sin1 = jnp.sin(ang)
    reps = MLA_HEADS * QK_ROPE_DIM // LANES
    cos = jnp.concatenate([cos1] * reps, axis=1)
    sin = jnp.concatenate([sin1] * reps, axis=1)
    scale = np.float32(QK_DIM ** -0.5)
    q = jnp.dot(cq_ref[...], wuq_ref[...], preferred_element_type=F32) * scale
    q_pe = _rope(q[:, MLA_HEADS * QK_NOPE_DIM:], cos, sin)
    kv = jnp.dot(ckv_ref[...], wukv_ref[...], preferred_element_type=F32)
    k_pe = _rope(kpe_ref[...], cos1, sin1)[:, :QK_ROPE_DIM].astype(BF16)
    for h in range(MLA_HEADS):
        q_ref[0, h, :, 0:QK_NOPE_DIM] = q[:, h * QK_NOPE_DIM:(h + 1) * QK_NOPE_DIM].astype(BF16)
        q_ref[0, h, :, QK_NOPE_DIM:QK_DIM] = q_pe[:, h * QK_ROPE_DIM:(h + 1) * QK_ROPE_DIM].astype(BF16)
        k_ref[0, h, :, 0:QK_NOPE_DIM] = kv[:, h * QK_NOPE_DIM:(h + 1) * QK_NOPE_DIM].astype(BF16)
        k_ref[0, h, :, QK_NOPE_DIM:QK_DIM] = k_pe
        v_ref[0, h, :, :] = kv[:, MLA_WIDTH + h * V_HEAD_DIM:MLA_WIDTH + (h + 1) * V_HEAD_DIM].T.astype(BF16)


def _qkv(cqn, ckvn, kpe, pos2, invf, wuq, wukv):
    tm = TOK_TM
    tpb = SEQ // tm
    row = lambda w: pl.BlockSpec((tm, w), lambda i: (i, 0))
    head_out = lambda w: pl.BlockSpec((1, MLA_HEADS, tm, w), lambda i: (i // tpb, 0, i % tpb, 0))
    return pl.pallas_call(
        _qkv_body,
        grid=(N_TOK // tm,),
        in_specs=[row(Q_LORA_RANK), row(KV_LORA_RANK), row(LANES), row(1),
                  _const_spec(invf.shape), _const_spec(wuq.shape), _const_spec(wukv.shape)],
        out_specs=[head_out(QK_DIM), head_out(QK_DIM),
                   pl.BlockSpec((1, MLA_HEADS, V_HEAD_DIM, tm), lambda i: (i // tpb, 0, 0, i % tpb))],
        out_shape=[jax.ShapeDtypeStruct((BATCH, MLA_HEADS, SEQ, QK_DIM), BF16),
                   jax.ShapeDtypeStruct((BATCH, MLA_HEADS, SEQ, QK_DIM), BF16),
                   jax.ShapeDtypeStruct((BATCH, MLA_HEADS, V_HEAD_DIM, SEQ), BF16)],
        compiler_params=_cparams(("parallel",)),
        name="qkv",
    )(cqn, ckvn, kpe, pos2, invf, wuq, wukv)


def _attn_body(q_ref, k_ref, v_ref, o_ref):
    i = pl.program_id(2)

    def step(h, j, carry, masked):
        m, l, acc = carry
        start = pl.multiple_of(j * ATT_TK, ATT_TK)
        k = k_ref[0, h, pl.ds(start, ATT_TK), :]
        vt = v_ref[0, h, :, pl.ds(start, ATT_TK)]
        st = lax.dot_general(k, q_ref[0, h], (((1,), (1,)), ((), ())), preferred_element_type=F32)
        if masked:
            kpos = lax.broadcasted_iota(jnp.int32, st.shape, 0)
            qpos = lax.broadcasted_iota(jnp.int32, st.shape, 1)
            st = jnp.where(kpos <= qpos, st, NEG_BIG)
        m_new = jnp.maximum(m, jnp.max(st, axis=0, keepdims=True))
        p = jnp.exp(st - m_new)
        a = jnp.exp(m - m_new)
        l = a * l + jnp.sum(p, axis=0, keepdims=True)
        pv = jnp.dot(vt, p.astype(BF16), preferred_element_type=F32)
        return m_new, l, a * acc + pv

    def steps(j, carries, masked):
        return tuple(step(h, j, carries[h], masked) for h in range(ATT_HEADS))

    init = tuple((jnp.full((1, ATT_TQ), NEG_BIG, F32), jnp.zeros((1, ATT_TQ), F32),
                  jnp.zeros((V_HEAD_DIM, ATT_TQ), F32)) for _ in range(ATT_HEADS))
    carries = lax.fori_loop(0, i, lambda j, c: steps(j, c, False), init)
    carries = steps(i, carries, True)
    for h, (m, l, acc) in enumerate(carries):
        o_ref[0, :, h * V_HEAD_DIM:(h + 1) * V_HEAD_DIM] = (acc / l).T.astype(BF16)


def _attn(q, k, v):
    assert ATT_TQ == ATT_TK
    hb = ATT_HEADS
    return pl.pallas_call(
        _attn_body,
        grid=(BATCH, MLA_HEADS // hb, SEQ // ATT_TQ),
        in_specs=[pl.BlockSpec((1, hb, ATT_TQ, QK_DIM), lambda b, h, i: (b, h, i, 0)),
                  pl.BlockSpec((1, hb, SEQ, QK_DIM), lambda b, h, i: (b, h, 0, 0)),
                  pl.BlockSpec((1, hb, V_HEAD_DIM, SEQ), lambda b, h, i: (b, h, 0, 0))],
        out_specs=pl.BlockSpec((1, ATT_TQ, hb * V_HEAD_DIM), lambda b, h, i: (b, i, h)),
        out_shape=jax.ShapeDtypeStruct((BATCH, SEQ, MLA_WIDTH), BF16),
        compiler_params=_cparams(("parallel", "parallel", "arbitrary")),
        name="attn",
    )(q, k, v)


def _mixout_body(x_ref, mod_ref, attn_ref, u_ref, vs_ref, wsp_ref, bsp_ref, woa_ref, wos_ref,
                 g1_ref, b1_ref, wr_ref, br_ref, x1_ref, h2_ref, lg_ref, sgu_scr):
    r = lax.broadcasted_iota(jnp.int32, (SGU_CHUNK, SGU_CHUNK), 0)
    c = lax.broadcasted_iota(jnp.int32, (SGU_CHUNK, SGU_CHUNK), 1)
    causal = c <= r
    for g in range(SGU_GROUPS):
        ws = jnp.where(causal, wsp_ref[g], 0.0).astype(BF16)
        bias = bsp_ref[:, g:g + 1]
        cols = slice(g * SGU_GROUP_DIM, (g + 1) * SGU_GROUP_DIM)
        for ch in range(MIX_TM // SGU_CHUNK):
            rows = slice(ch * SGU_CHUNK, (ch + 1) * SGU_CHUNK)
            mixed = jnp.dot(ws, vs_ref[rows, cols], preferred_element_type=F32) + bias
            sgu_scr[rows, cols] = (u_ref[rows, cols] * mixed).astype(BF16)
    y = (jnp.dot(attn_ref[...], woa_ref[...], preferred_element_type=F32)
         + jnp.dot(sgu_scr[...], wos_ref[...], preferred_element_type=F32))
    gate1 = mod_ref[0, 2:3, :]
    sh2 = mod_ref[0, 3:4, :]
    sc2 = mod_ref[0, 4:5, :]
    x1 = _ln_rows(DEEPNORM_ALPHA * x_ref[...] + gate1 * y) * g1_ref[...] + b1_ref[...]
    x1_ref[...] = x1
    h2 = _ln_rows(x1) * (1.0 + sc2) + sh2
    _store_token_tiles(h2_ref, _pack_halves(h2))
    lg_ref[...] = jnp.dot(h2.astype(BF16), wr_ref[...], preferred_element_type=F32) + br_ref[...]


def _mixout(x2, mod3, attn, u, vs, wsp, bsp_t, wo, g1, b1, wr, br):
    tm = MIX_TM
    tpb = SEQ // tm
    row = lambda w: pl.BlockSpec((tm, w), lambda i: (i, 0))
    wo_half = lambda j: pl.BlockSpec((MLA_WIDTH, D_MODEL), lambda i: (j, 0), pipeline_mode=pl.Buffered(1))
    return pl.pallas_call(
        _mixout_body,
        grid=(N_TOK // tm,),
        in_specs=[row(D_MODEL),
                  pl.BlockSpec((1, N_MOD, D_MODEL), lambda i: (i // tpb, 0, 0)),
                  row(MLA_WIDTH), row(SGU_WIDTH), row(SGU_WIDTH),
                  _const_spec(wsp.shape), _const_spec(bsp_t.shape), wo_half(0), wo_half(1),
                  _const_spec(g1.shape), _const_spec(b1.shape), _const_spec(wr.shape), _const_spec(br.shape)],
        out_specs=[row(D_MODEL), pl.BlockSpec((tm * SUBLANES, LANES), lambda i: (i, 0)), row(LANES)],
        out_shape=[jax.ShapeDtypeStruct((N_TOK, D_MODEL), F32),
                   jax.ShapeDtypeStruct((N_TOK * SUBLANES, LANES), U32),
                   jax.ShapeDtypeStruct((N_TOK, LANES), F32)],
        scratch_shapes=[pltpu.VMEM((tm, SGU_WIDTH), BF16)],
        compiler_params=_cparams(("parallel",)),
        name="mix_out",
    )(x2, mod3, attn, u, vs, wsp, bsp_t, wo, wo, g1, b1, wr, br)


def _route_math(lg):
    lane = lax.broadcasted_iota(jnp.int32, lg.shape, 1)
    big = jnp.int32(LANES)

    def top1(vals):
        m = jnp.max(vals, axis=-1, keepdims=True)
        idx = jnp.min(jnp.where(vals == m, lane, big), axis=-1, keepdims=True)
        return m, idx

    is_group = lane < N_GROUPS
    glog = jnp.where(is_group, lg, -jnp.inf)
    gmax, gidx = top1(glog)
    pg_top = 1.0 / jnp.sum(jnp.exp(glog - gmax), axis=-1, keepdims=True)
    eid = lane - N_GROUPS
    sel = (eid >= gidx * EXPERTS_PER_GROUP) & (eid < (gidx + 1) * EXPERTS_PER_GROUP)
    elog = jnp.where(sel, lg, -jnp.inf)
    m1, i1 = top1(elog)
    m2, i2 = top1(jnp.where(lane == i1, -jnp.inf, elog))
    e2 = jnp.exp(m2 - m1)
    w1 = pg_top / (1.0 + e2)
    w2 = pg_top * e2 / (1.0 + e2)
    return jnp.where(lane == 0, (i1 - N_GROUPS).astype(F32),
                     jnp.where(lane == 1, (i2 - N_GROUPS).astype(F32),
                               jnp.where(lane == 2, w1, jnp.where(lane == 3, w2, 0.0))))


def _rank_math(rt, counts):
    t = rt.shape[0]
    lane = lax.broadcasted_iota(jnp.int32, (t, LANES), 1).astype(F32)
    oh0 = lane == rt[:, 0:1]
    oh1 = lane == rt[:, 1:2]
    s = jnp.where(oh0 | oh1, 1.0, 0.0)
    r = lax.broadcasted_iota(jnp.int32, (t, t), 0)
    c = lax.broadcasted_iota(jnp.int32, (t, t), 1)
    before = jnp.where(c < r, 1.0, 0.0).astype(BF16)
    csum = jnp.dot(before, s.astype(BF16), preferred_element_type=F32) + counts
    rank0 = jnp.sum(jnp.where(oh0, csum, 0.0), axis=-1, keepdims=True)
    rank1 = jnp.sum(jnp.where(oh1, csum, 0.0), axis=-1, keepdims=True)
    return rank0, rank1, counts + jnp.sum(s, axis=0, keepdims=True)


def _route_body(lg_ref, rt_ref, cnt_ref, cnt_scr):
    @pl.when(pl.program_id(0) == 0)
    def _():
        cnt_scr[...] = jnp.zeros_like(cnt_scr)

    rt = _route_math(lg_ref[...])
    rank0, rank1, counts = _rank_math(rt, cnt_scr[...])
    cnt_scr[...] = counts
    lane = lax.broadcasted_iota(jnp.int32, rt.shape, 1)
    rt_ref[...] = jnp.where(lane == RT_RANK, rank0, jnp.where(lane == RT_RANK + 1, rank1, rt))
    cnt_ref[...] = jnp.broadcast_to(counts, cnt_ref.shape)


def _route(logits):
    tm = ROUTE_TM
    return pl.pallas_call(
        _route_body,
        grid=(N_TOK // tm,),
        in_specs=[pl.BlockSpec((tm, LANES), lambda i: (i, 0))],
        out_specs=[pl.BlockSpec((tm, LANES), lambda i: (i, 0)),
                   pl.BlockSpec((SUBLANES, LANES), lambda i: (0, 0))],
        out_shape=[jax.ShapeDtypeStruct((N_TOK, LANES), F32),
                   jax.ShapeDtypeStruct((SUBLANES, LANES), F32)],
        scratch_shapes=[pltpu.VMEM((1, LANES), F32)],
        compiler_params=_cparams(("arbitrary",)),
        name="route",
    )(logits)


def _plan_body(rt_ref, cnt_ref, pos_ref, tt_ref):
    t = PLAN_TM
    lane = lax.broadcasted_iota(jnp.int32, (t, LANES), 1)
    rt = rt_ref[...]
    oh0 = lane.astype(F32) == rt[:, 0:1]
    oh1 = lane.astype(F32) == rt[:, 1:2]
    counts = cnt_ref[0:1, :]
    tiles = jnp.floor((counts + (MOE_TM - 1)) * (1.0 / MOE_TM))
    r = lax.broadcasted_iota(jnp.int32, (LANES, LANES), 0)
    c = lax.broadcasted_iota(jnp.int32, (LANES, LANES), 1)
    upto = jnp.where(r <= c, 1.0, 0.0).astype(BF16)
    tiles8 = jnp.broadcast_to(tiles, (SUBLANES, LANES)).astype(BF16)
    tile_end = jnp.dot(tiles8, upto, preferred_element_type=F32)[0:1]
    offs = (tile_end - tiles) * MOE_TM
    p0 = jnp.sum(jnp.where(oh0, offs, 0.0), axis=-1, keepdims=True) + rt[:, RT_RANK:RT_RANK + 1]
    p1 = jnp.sum(jnp.where(oh1, offs, 0.0), axis=-1, keepdims=True) + rt[:, RT_RANK + 1:RT_RANK + 2]
    pos_ref[...] = (jnp.where(lane == 0, p0, jnp.where(lane == 1, p1, 0.0)) * SUBLANES).astype(jnp.int32)
    tt_ref[...] = _tile_table(tiles, tile_end)


def _tile_table(tiles, tile_end):
    lane_e = lax.broadcasted_iota(jnp.int32, (LANES, LANES), 1)
    lane_f = lane_e.astype(F32)
    tile_id = lax.broadcasted_iota(jnp.int32, (LANES, LANES), 0).astype(F32)
    is_e = lane_e < N_EXPERTS
    owns = is_e & (tiles > 0)

    def count(cond):
        return jnp.sum(jnp.where(cond, 1.0, 0.0), axis=-1, keepdims=True)

    def first_owner_after(e):
        return jnp.min(jnp.where(owns & (lane_f > e), lane_f, jnp.float32(LANES)), axis=-1, keepdims=True)

    total = jnp.max(tile_end, axis=-1, keepdims=True)
    t_valid = jnp.where(tile_id[:, 0:1] < total, 1.0, 0.0)
    t_exp = jnp.where(t_valid > 0, count(is_e & (tile_end <= tile_id)), count(is_e & (tile_end <= total - 1.0)))
    t_first = count(owns & ((tile_end - tiles) == tile_id))
    t_clear = jnp.maximum(count(owns & ((tile_end - 1.0) == tile_id)), 1.0 - t_valid)
    t_next = first_owner_after(t_exp)
    t_next2 = first_owner_after(t_next)
    t_next3 = first_owner_after(t_next2)
    cols = {T_EXP: t_exp, T_VALID: t_valid, T_FIRST: t_first, T_CLEAR: t_clear, T_NEXT: t_next,
            T_NEXT2: t_next2, T_NEXT3: t_next3, T_ORD: count(owns & (lane_f < t_exp)), T_LAST: total - 1.0}
    table = jnp.zeros((LANES, LANES), F32)
    for k, col in cols.items():
        table = jnp.where(lane_e == k, col, table)
    return table.astype(jnp.int32)


def _plan(route, counts):
    t = PLAN_TM
    return pl.pallas_call(
        _plan_body,
        grid=(N_TOK // t,),
        in_specs=[pl.BlockSpec((t, LANES), lambda i: (i, 0)),
                  pl.BlockSpec((SUBLANES, LANES), lambda i: (0, 0))],
        out_specs=[pl.BlockSpec((t, LANES), lambda i: (i, 0)),
                   pl.BlockSpec((LANES, LANES), lambda i: (0, 0))],
        out_shape=[jax.ShapeDtypeStruct((N_TOK, LANES), jnp.int32),
                   jax.ShapeDtypeStruct((LANES, LANES), jnp.int32)],
        compiler_params=_cparams(("arbitrary",)),
        name="plan",
    )(route, counts)


def _rows_wait(ref, n_rows, sem):
    pltpu.make_async_copy(ref.at[pl.ds(0, n_rows)], ref.at[pl.ds(0, n_rows)], sem).wait()


def _dispatch_body(pos_ref, tt_ref, h_ref, xs_hbm, zbuf, sem_z, sem_s):
    i = pl.program_id(0)
    tile_rows = MOE_TM * SUBLANES

    @pl.when(i == 0)
    def _():
        zbuf[...] = _pack_halves(jnp.zeros((tile_rows, 2 * LANES), F32))

        def zero_copy(tile):
            start = pl.multiple_of(tile * tile_rows, tile_rows)
            return pltpu.make_async_copy(zbuf, xs_hbm.at[pl.ds(start, tile_rows)], sem_z)

        def clear_start(tile, carry):
            @pl.when(tt_ref[tile, T_CLEAR] > 0)
            def _():
                zero_copy(tile).start()
            return carry

        def clear_wait(tile, carry):
            @pl.when(tt_ref[tile, T_CLEAR] > 0)
            def _():
                zero_copy(tile).wait()
            return carry

        lax.fori_loop(0, MOE_TILES, clear_start, 0)
        lax.fori_loop(0, MOE_TILES, clear_wait, 0)

    def tok(j, carry):
        src = h_ref.at[pl.ds(pl.multiple_of(j * SUBLANES, SUBLANES), SUBLANES)]
        pair = TOP_K * (i * DISPATCH_TM + j)
        for k in range(TOP_K):
            dst_row = pl.multiple_of(pos_ref[pair + k], SUBLANES)
            pltpu.make_async_copy(src, xs_hbm.at[pl.ds(dst_row, SUBLANES)], sem_s).start(priority=k)
        return carry

    lax.fori_loop(0, DISPATCH_TM, tok, 0, unroll=8)
    _rows_wait(xs_hbm, TOP_K * DISPATCH_TM * SUBLANES, sem_s)


def _dispatch(pos_rows, tile_clear, h2p):
    grid_spec = pltpu.PrefetchScalarGridSpec(
        num_scalar_prefetch=2,
        grid=(N_TOK // DISPATCH_TM,),
        in_specs=[pl.BlockSpec((DISPATCH_TM * SUBLANES, LANES), lambda i, *_: (i, 0))],
        out_specs=pl.BlockSpec(memory_space=pl.ANY),
        scratch_shapes=[pltpu.VMEM((MOE_TM * SUBLANES, LANES), U32),
                        pltpu.SemaphoreType.DMA(()), pltpu.SemaphoreType.DMA(())],
    )
    return pl.pallas_call(
        _dispatch_body,
        grid_spec=grid_spec,
        out_shape=jax.ShapeDtypeStruct((MOE_ROWS * SUBLANES, LANES), U32),
        compiler_params=_cparams(("arbitrary",)),
        name="dispatch",
    )(pos_rows, tile_clear, h2p)


def _moe_body(tt_ref, x_ref, wg_hbm, wu_hbm, wd_hbm, y_ref,
              wg_s, wu_s, wd_s, stg_g, stg_u, stg_d, sem):
    i = pl.program_id(0)
    ahead = (T_EXP, T_NEXT, T_NEXT2, T_NEXT3)

    def fetch(e, slot):
        return (pltpu.make_async_copy(wg_hbm.at[e], stg_g.at[slot], sem.at[slot, 0]),
                pltpu.make_async_copy(wu_hbm.at[e], stg_u.at[slot], sem.at[slot, 1]),
                pltpu.make_async_copy(wd_hbm.at[e], stg_d.at[slot], sem.at[slot, 2]))

    @pl.when(i == 0)
    def _():
        for d in range(MOE_SLOTS):
            @pl.when(tt_ref[0, ahead[d]] < N_EXPERTS)
            def _():
                for cp in fetch(tt_ref[0, ahead[d]], d):
                    cp.start(priority=WEIGHT_DMA_PRIORITY)

    @pl.when(tt_ref[i, T_FIRST] > 0)
    def _():
        slot = tt_ref[i, T_ORD] % MOE_SLOTS
        for cp in fetch(tt_ref[i, T_EXP], slot):
            cp.wait()
        wg_s[...] = stg_g[slot].astype(BF16)
        wu_s[...] = stg_u[slot].astype(BF16)
        wd_s[...] = stg_d[slot].astype(BF16)

        @pl.when(tt_ref[i, ahead[MOE_SLOTS]] < N_EXPERTS)
        def _():
            for cp in fetch(tt_ref[i, ahead[MOE_SLOTS]], slot):
                cp.start(priority=WEIGHT_DMA_PRIORITY)

    @pl.when(tt_ref[i, T_VALID] > 0)
    def _():
        lo, hi = _unpack_halves(_load_token_tiles(x_ref, 0, MOE_TM))
        xa = lo.astype(BF16)
        xb = hi.astype(BF16)
        g = (jnp.dot(xa, wg_s[:HALF_D, :], preferred_element_type=F32)
             + jnp.dot(xb, wg_s[HALF_D:, :], preferred_element_type=F32))
        u = (jnp.dot(xa, wu_s[:HALF_D, :], preferred_element_type=F32)
             + jnp.dot(xb, wu_s[HALF_D:, :], preferred_element_type=F32))
        hid = (g * jax.nn.sigmoid(g) * u).astype(BF16)
        _store_token_tiles(y_ref, _pack_halves(jnp.dot(hid, wd_s[...], preferred_element_type=F32)))

    @pl.when(tt_ref[i, T_VALID] == 0)
    def _():
        y_ref[...] = _pack_halves(jnp.zeros((MOE_TM * SUBLANES, 2 * LANES), F32))


def _moe(tile_tab, xs, wg, wu, wd):
    tm = MOE_TM
    grid_spec = pltpu.PrefetchScalarGridSpec(
        num_scalar_prefetch=1,
        grid=(MOE_TILES,),
        in_specs=[pl.BlockSpec((tm * SUBLANES, LANES), lambda i, tt: (jnp.minimum(i, tt[0, T_LAST]), 0)),
                  pl.BlockSpec(memory_space=pl.ANY), pl.BlockSpec(memory_space=pl.ANY),
                  pl.BlockSpec(memory_space=pl.ANY)],
        out_specs=pl.BlockSpec((tm * SUBLANES, LANES), lambda i, *_: (i, 0)),
        scratch_shapes=[pltpu.VMEM((D_MODEL, EXPERT_FF), BF16), pltpu.VMEM((D_MODEL, EXPERT_FF), BF16),
                        pltpu.VMEM((EXPERT_FF, D_MODEL), BF16),
                        pltpu.VMEM((MOE_SLOTS, D_MODEL, EXPERT_FF), F32),
                        pltpu.VMEM((MOE_SLOTS, D_MODEL, EXPERT_FF), F32),
                        pltpu.VMEM((MOE_SLOTS, EXPERT_FF, D_MODEL), F32),
                        pltpu.SemaphoreType.DMA((MOE_SLOTS, 3))],
    )
    return pl.pallas_call(
        _moe_body,
        grid_spec=grid_spec,
        out_shape=jax.ShapeDtypeStruct((MOE_ROWS * SUBLANES, LANES), U32),
        compiler_params=_cparams(("arbitrary",)),
        name="moe",
    )(tile_tab, xs, wg, wu, wd)


def _final_body(pos_ref, x1_ref, mod_ref, rt_ref, g2_ref, b2_ref, ys_hbm, o_ref, buf, sem):
    i = pl.program_id(0)
    n = pl.num_programs(0)
    tm = FINAL_TM
    slot = i % FINAL_SLOTS
    nxt_slot = (i + 2) % FINAL_SLOTS
    nxt_tile = jnp.minimum(i + 2, n - 1)

    def issue(tile, dst_slot, j):
        pair = TOP_K * (tile * tm + j)
        for k in range(TOP_K):
            src_row = pl.multiple_of(pos_ref[pair + k], SUBLANES)
            dst_row = pl.multiple_of((k * tm + j) * SUBLANES, SUBLANES)
            pltpu.make_async_copy(ys_hbm.at[pl.ds(src_row, SUBLANES)],
                                  buf.at[dst_slot, pl.ds(dst_row, SUBLANES)], sem.at[dst_slot]).start(priority=k)

    def wait(s):
        pltpu.make_async_copy(ys_hbm.at[pl.ds(0, TOP_K * tm * SUBLANES)], buf.at[s], sem.at[s]).wait()

    @pl.when(i == 0)
    def _():
        def tok(j, carry):
            issue(0, 0, j)
            issue(jnp.minimum(1, n - 1), 1, j)
            return carry
        lax.fori_loop(0, tm, tok, 0, unroll=8)

    wait(slot)
    gate2 = mod_ref[0, 5:6, :]
    cur = buf.at[slot]

    def chunk(c, carry):
        r0 = pl.multiple_of(c * FINAL_CHUNK, FINAL_CHUNK)
        rows = pl.ds(r0, FINAL_CHUNK)
        a_lo, a_hi = _unpack_halves(_load_token_tiles(cur, r0, FINAL_CHUNK))
        b_lo, b_hi = _unpack_halves(_load_token_tiles(cur, tm + r0, FINAL_CHUNK))
        x1 = x1_ref[rows, :]
        rt = rt_ref[rows, :]
        for r in range(FINAL_CHUNK):
            issue(nxt_tile, nxt_slot, r0 + r)
        w0 = rt[:, 2:3]
        w1 = rt[:, 3:4]
        y = jnp.concatenate([w0 * a_lo + w1 * b_lo, w0 * a_hi + w1 * b_hi], axis=1)
        o_ref[rows, :] = _ln_rows(DEEPNORM_ALPHA * x1 + gate2 * y) * g2_ref[...] + b2_ref[...]
        return carry

    lax.fori_loop(0, tm // FINAL_CHUNK, chunk, 0)

    @pl.when(i == n - 1)
    def _():
        wait((i + 1) % FINAL_SLOTS)
        wait(nxt_slot)


def _final(pos_rows, x1, mod3, route, g2, b2, ys):
    tm = FINAL_TM
    tpb = SEQ // tm
    row = lambda w: pl.BlockSpec((tm, w), lambda i, *_: (i, 0))
    grid_spec = pltpu.PrefetchScalarGridSpec(
        num_scalar_prefetch=1,
        grid=(N_TOK // tm,),
        in_specs=[row(D_MODEL),
                  pl.BlockSpec((1, N_MOD, D_MODEL), lambda i, *_: (i // tpb, 0, 0)),
                  row(LANES),
                  pl.BlockSpec(g2.shape, lambda i, *_: (0, 0)),
                  pl.BlockSpec(b2.shape, lambda i, *_: (0, 0)),
                  pl.BlockSpec(memory_space=pl.ANY)],
        out_specs=row(D_MODEL),
        scratch_shapes=[pltpu.VMEM((FINAL_SLOTS, TOP_K * tm * SUBLANES, LANES), U32),
                        pltpu.SemaphoreType.DMA((FINAL_SLOTS,))],
    )
    return pl.pallas_call(
        _final_body,
        grid_spec=grid_spec,
        out_shape=jax.ShapeDtypeStruct((N_TOK, D_MODEL), F32),
        compiler_params=_cparams(("arbitrary",)),
        name="final",
    )(pos_rows, x1, mod3, route, g2, b2, ys)


def kernel(x, c, positions, w_ada, b_ada, w_in, q_norm_g, w_uq, kv_norm_g, w_ukv, sgu_norm_g, sgu_norm_b,
           w_spatial, b_spatial, w_o, ln1_g, ln1_b, w_router_group, b_router_group, w_router_expert,
           b_router_expert, w_gate, w_up, w_down, ln2_g, ln2_b):
    l = 0
    x2 = x.reshape(N_TOK, D_MODEL)
    mod3 = _ada(c, w_ada[l], b_ada[l][None, :]).reshape(BATCH, N_MOD, D_MODEL)

    wt, wuq, wukv, wo = _prep(w_in[l].T, w_uq[l], w_ukv[l], w_o[l])
    n_r = N_GROUPS + N_EXPERTS
    wr = jnp.pad(jnp.concatenate([w_router_group[l], w_router_expert[l]], axis=1),
                 ((0, 0), (0, LANES - n_r))).astype(BF16)
    br = jnp.pad(jnp.concatenate([b_router_group[l], b_router_expert[l]]), (0, LANES - n_r))[None, :]
    inv_freq = 1.0 / (ROPE_THETA ** (jnp.arange(0, QK_ROPE_DIM, 2, dtype=F32) / QK_ROPE_DIM))
    invf = jnp.tile(inv_freq, 2 * LANES // QK_ROPE_DIM)[None, :]

    cqn, ckvn, kpe, u, vs = _inproj(x2, mod3, wt, q_norm_g[l][None, :], kv_norm_g[l][None, :],
                                    sgu_norm_g[l][None, :], sgu_norm_b[l][None, :])
    q, k, v = _qkv(cqn, ckvn, kpe, positions.reshape(N_TOK, 1), invf, wuq, wukv)
    attn = _attn(q, k, v).reshape(N_TOK, MLA_WIDTH)
    x1, h2, logits = _mixout(x2, mod3, attn, u, vs, w_spatial[l], b_spatial[l].T, wo,
                             ln1_g[l][None, :], ln1_b[l][None, :], wr, br)
    route, counts = _route(logits)
    pos_tab, tile_tab = _plan(route, counts)
    pos_rows = pos_tab[:, 0:TOP_K].reshape(-1)
    xs = _dispatch(pos_rows, tile_tab, h2)
    ys = _moe(tile_tab, xs, w_gate[l], w_up[l], w_down[l])
    out = _final(pos_rows, x1, mod3, route, ln2_g[l][None, :], ln2_b[l][None, :], ys)
    return out.reshape(BATCH, SEQ, D_MODEL)
```

```python
import jax
import jax.numpy as jnp
import numpy as np
from jax import lax
from jax.experimental import pallas as pl
from jax.experimental.pallas import tpu as pltpu

D_MODEL = 2048
BATCH = 4
SEQ = 2048
N_TOK = BATCH * SEQ

MLA_HEADS = 8
QK_NOPE_DIM = 128
QK_ROPE_DIM = 64
QK_DIM = QK_NOPE_DIM + QK_ROPE_DIM
V_HEAD_DIM = 128
Q_LORA_RANK = 768
KV_LORA_RANK = 512
ROPE_THETA = 10000.0
MLA_WIDTH = MLA_HEADS * V_HEAD_DIM

SGU_GROUPS = 8
SGU_GROUP_DIM = 128
SGU_CHUNK = 128
SGU_WIDTH = SGU_GROUPS * SGU_GROUP_DIM

N_GROUPS = 4
EXPERTS_PER_GROUP = 8
N_EXPERTS = N_GROUPS * EXPERTS_PER_GROUP
TOP_K = 2
EXPERT_FF = 512

DEEPNORM_ALPHA = 2.0 ** 0.25
EPS = 1e-6
N_MOD = 6
NEG_BIG = -1e30

LANES = 128
SUBLANES = 8
V7X_VMEM_BYTES = 64 * 1024 * 1024
VMEM_LIMIT = V7X_VMEM_BYTES * 7 // 8

ADA_TN = 1024
TOK_TM = 512
MIX_TM = 256
PREP_STEPS = 4
ATT_TQ = 1024
ATT_TK = 1024
ATT_HEADS = 8
MOE_TM = 256
MOE_TILES = (N_TOK * TOP_K + N_EXPERTS * (MOE_TM - 1)) // MOE_TM + 1
MOE_ROWS = MOE_TILES * MOE_TM
PLAN_TM = 2048
ROUTE_TM = 1024
RT_RANK = 4
DISPATCH_TM = 1024
FINAL_TM = 256
FINAL_CHUNK = 128
FINAL_SLOTS = 3
assert MOE_TILES <= LANES
T_EXP, T_VALID, T_FIRST, T_CLEAR, T_NEXT, T_NEXT2, T_NEXT3, T_ORD, T_LAST = range(9)
MOE_SLOTS = 2
WEIGHT_DMA_PRIORITY = 1

F32 = jnp.float32
BF16 = jnp.bfloat16
U32 = jnp.uint32
HALF_D = D_MODEL // 2


def _cparams(sem):
    return pltpu.CompilerParams(dimension_semantics=sem, vmem_limit_bytes=VMEM_LIMIT)


def _const_spec(shape):
    nd = len(shape)
    return pl.BlockSpec(shape, lambda *_: (0,) * nd, pipeline_mode=pl.Buffered(1))


def _ln_rows(x):
    mu = jnp.mean(x, axis=-1, keepdims=True)
    xc = x - mu
    var = jnp.mean(xc * xc, axis=-1, keepdims=True)
    return xc * lax.rsqrt(var + EPS)


def _rms_rows(x):
    return x * lax.rsqrt(jnp.mean(x * x, axis=-1, keepdims=True) + EPS)


def _pack_halves(x):
    half = x.shape[-1] // 2
    return pltpu.pack_elementwise([x[:, :half], x[:, half:]], packed_dtype=BF16)


def _unpack_halves(w):
    lo = pltpu.unpack_elementwise(w, index=0, packed_dtype=BF16, unpacked_dtype=F32)
    hi = pltpu.unpack_elementwise(w, index=1, packed_dtype=BF16, unpacked_dtype=F32)
    return lo, hi


def _store_token_tiles(ref, w):
    rows = w.shape[0]
    for s in range(SUBLANES):
        ref[pl.ds(s, rows, stride=SUBLANES), :] = w[:, s * LANES:(s + 1) * LANES]


def _load_token_tiles(ref, start_row, rows):
    return jnp.concatenate([ref[pl.ds(start_row * SUBLANES + s, rows, stride=SUBLANES), :]
                            for s in range(SUBLANES)], axis=1)


def _gelu_tanh(x):
    c = np.sqrt(2.0 / np.pi).astype(np.float32)
    return 0.5 * x * (1.0 + jnp.tanh(c * (x + 0.044715 * (x * x * x))))


def _ada_body(c_ref, w_ref, b_ref, o_ref):
    o_ref[...] = jnp.dot(c_ref[...].astype(BF16), w_ref[...].astype(BF16),
                         preferred_element_type=F32) + b_ref[...]


def _ada(c, w, b):
    n = w.shape[1]
    return pl.pallas_call(
        _ada_body,
        grid=(n // ADA_TN,),
        in_specs=[pl.BlockSpec((BATCH, D_MODEL), lambda j: (0, 0)),
                  pl.BlockSpec((D_MODEL, ADA_TN), lambda j: (0, j)),
                  pl.BlockSpec((1, ADA_TN), lambda j: (0, j))],
        out_specs=pl.BlockSpec((BATCH, ADA_TN), lambda j: (0, j)),
        out_shape=jax.ShapeDtypeStruct((BATCH, n), F32),
        compiler_params=_cparams(("parallel",)),
        name="ada",
    )(c, w, b)


def _prep_body(win_ref, wuq_ref, wukv_ref, wo_ref, win_o, wuq_o, wukv_o, wo_o):
    win_o[...] = win_ref[...].astype(BF16)
    u = wuq_ref[...]
    nope = [u[:, h * QK_DIM:h * QK_DIM + QK_NOPE_DIM] for h in range(MLA_HEADS)]
    rope = [u[:, h * QK_DIM + QK_NOPE_DIM:(h + 1) * QK_DIM] for h in range(MLA_HEADS)]
    wuq_o[...] = jnp.concatenate(nope + rope, axis=1).astype(BF16)
    kv = wukv_ref[...]
    hw = QK_NOPE_DIM + V_HEAD_DIM
    kn = [kv[:, h * hw:h * hw + QK_NOPE_DIM] for h in range(MLA_HEADS)]
    vv = [kv[:, h * hw + QK_NOPE_DIM:(h + 1) * hw] for h in range(MLA_HEADS)]
    wukv_o[...] = jnp.concatenate(kn + vv, axis=1).astype(BF16)
    wo_o[...] = wo_ref[...].astype(BF16)


def _prep(w_in_t, w_uq, w_ukv, w_o):
    steps = PREP_STEPS
    blk = lambda a: pl.BlockSpec((a.shape[0] // steps, a.shape[1]), lambda i: (i, 0))
    ins = (w_in_t, w_uq, w_ukv, w_o)
    return pl.pallas_call(
        _prep_body,
        grid=(steps,),
        in_specs=[blk(a) for a in ins],
        out_specs=[blk(a) for a in ins],
        out_shape=[jax.ShapeDtypeStruct(a.shape, BF16) for a in ins],
        compiler_params=_cparams(("parallel",)),
        name="prep",
    )(*ins)


def _inproj_body(x_ref, mod_ref, wt_ref, gq_ref, gkv_ref, sg_ref, sb_ref,
                 cq_ref, ckv_ref, kpe_ref, u_ref, vs_ref):
    o1, o2, o3 = Q_LORA_RANK, Q_LORA_RANK + KV_LORA_RANK, Q_LORA_RANK + KV_LORA_RANK + QK_ROPE_DIM

    def proj(lo, hi):
        return lax.dot_general(h, wt_ref[lo:hi, :], (((1,), (1,)), ((), ())), preferred_element_type=F32)

    sh = mod_ref[0, 0:1, :]
    sc = mod_ref[0, 1:2, :]
    h = (_ln_rows(x_ref[...]) * (1.0 + sc) + sh).astype(BF16)
    cq_ref[...] = (_rms_rows(proj(0, o1)) * gq_ref[...]).astype(BF16)
    ckv_ref[...] = (_rms_rows(proj(o1, o2)) * gkv_ref[...]).astype(BF16)
    kpe_ref[...] = proj(o2, o2 + LANES)
    gz = _gelu_tanh(proj(o3, o3 + 2 * SGU_WIDTH))
    u_ref[...] = gz[:, :SGU_WIDTH]
    vs_ref[...] = (_ln_rows(gz[:, SGU_WIDTH:]) * sg_ref[...] + sb_ref[...]).astype(BF16)


def _inproj(x2, mod3, wt, gq, gkv, sg, sb):
    tm = TOK_TM
    tiles_per_batch = SEQ // tm
    row = lambda w: pl.BlockSpec((tm, w), lambda i: (i, 0))
    return pl.pallas_call(
        _inproj_body,
        grid=(N_TOK // tm,),
        in_specs=[row(D_MODEL),
                  pl.BlockSpec((1, N_MOD, D_MODEL), lambda i: (i // tiles_per_batch, 0, 0)),
                  _const_spec(wt.shape),
                  _const_spec(gq.shape), _const_spec(gkv.shape), _const_spec(sg.shape), _const_spec(sb.shape)],
        out_specs=[row(Q_LORA_RANK), row(KV_LORA_RANK), row(LANES), row(SGU_WIDTH), row(SGU_WIDTH)],
        out_shape=[jax.ShapeDtypeStruct((N_TOK, Q_LORA_RANK), BF16),
                   jax.ShapeDtypeStruct((N_TOK, KV_LORA_RANK), BF16),
                   jax.ShapeDtypeStruct((N_TOK, LANES), F32),
                   jax.ShapeDtypeStruct((N_TOK, SGU_WIDTH), F32),
                   jax.ShapeDtypeStruct((N_TOK, SGU_WIDTH), BF16)],
        compiler_params=_cparams(("parallel",)),
        name="inproj",
    )(x2, mod3, wt, gq, gkv, sg, sb)


def _rope(x, cos, sin):
    w = x.shape[-1]
    lane = lax.broadcasted_iota(jnp.int32, x.shape, 1)
    first_half = (lane % QK_ROPE_DIM) < (QK_ROPE_DIM // 2)
    rot = jnp.where(first_half,
                    -pltpu.roll(x, w - QK_ROPE_DIM // 2, 1),
                    pltpu.roll(x, QK_ROPE_DIM // 2, 1))
    return x * cos + rot * sin


def _qkv_body(cq_ref, ckv_ref, kpe_ref, pos_ref, invf_ref, wuq_ref, wukv_ref, q_ref, k_ref, v_ref):
    ang = pos_ref[...].astype(F32) * invf_ref[...]
    cos1 = jnp.cos(ang)
    sin1 = jnp.sin(ang)
    reps = MLA_HEADS * QK_ROPE_DIM // LANES
    cos = jnp.concatenate([cos1] * reps, axis=1)
    sin = jnp.concatenate([sin1] * reps, axis=1)
    scale = np.float32(QK_DIM ** -0.5)
    q = jnp.dot(cq_ref[...], wuq_ref[...], preferred_element_type=F32) * scale
    q_pe = _rope(q[:, MLA_HEADS * QK_NOPE_DIM:], cos, sin)
    kv = jnp.dot(ckv_ref[...], wukv_ref[...], preferred_element_type=F32)
    k_pe = _rope(kpe_ref[...], cos1, sin1)[:, :QK_ROPE_DIM].astype(BF16)
    for h in range(MLA_HEADS):
        q_ref[0, h, :, 0:QK_NOPE_DIM] = q[:, h * QK_NOPE_DIM:(h + 1) * QK_NOPE_DIM].astype(BF16)
        q_ref[0, h, :, QK_NOPE_DIM:QK_DIM] = q_pe[:, h * QK_ROPE_DIM:(h + 1) * QK_ROPE_DIM].astype(BF16)
        k_ref[0, h, :, 0:QK_NOPE_DIM] = kv[:, h * QK_NOPE_DIM:(h + 1) * QK_NOPE_DIM].astype(BF16)
        k_ref[0, h, :, QK_NOPE_DIM:QK_DIM] = k_pe
        v_ref[0, h, :, :] = kv[:, MLA_WIDTH + h * V_HEAD_DIM:MLA_WIDTH + (h + 1) * V_HEAD_DIM].T.astype(BF16)


def _qkv(cqn, ckvn, kpe, pos2, invf, wuq, wukv):
    tm = TOK_TM
    tpb = SEQ // tm
    row = lambda w: pl.BlockSpec((tm, w), lambda i: (i, 0))
    head_out = lambda w: pl.BlockSpec((1, MLA_HEADS, tm, w), lambda i: (i // tpb, 0, i % tpb, 0))
    return pl.pallas_call(
        _qkv_body,
        grid=(N_TOK // tm,),
        in_specs=[row(Q_LORA_RANK), row(KV_LORA_RANK), row(LANES), row(1),
                  _const_spec(invf.shape), _const_spec(wuq.shape), _const_spec(wukv.shape)],
        out_specs=[head_out(QK_DIM), head_out(QK_DIM),
                   pl.BlockSpec((1, MLA_HEADS, V_HEAD_DIM, tm), lambda i: (i // tpb, 0, 0, i % tpb))],
        out_shape=[jax.ShapeDtypeStruct((BATCH, MLA_HEADS, SEQ, QK_DIM), BF16),
                   jax.ShapeDtypeStruct((BATCH, MLA_HEADS, SEQ, QK_DIM), BF16),
                   jax.ShapeDtypeStruct((BATCH, MLA_HEADS, V_HEAD_DIM, SEQ), BF16)],
        compiler_params=_cparams(("parallel",)),
        name="qkv",
    )(cqn, ckvn, kpe, pos2, invf, wuq, wukv)


def _attn_body(q_ref, k_ref, v_ref, o_ref):
    i = pl.program_id(2)

    def step(h, j, carry, masked):
        m, l, acc = carry
        start = pl.multiple_of(j * ATT_TK, ATT_TK)
        k = k_ref[0, h, pl.ds(start, ATT_TK), :]
        vt = v_ref[0, h, :, pl.ds(start, ATT_TK)]
        st = lax.dot_general(k, q_ref[0, h], (((1,), (1,)), ((), ())), preferred_element_type=F32)
        if masked:
            kpos = lax.broadcasted_iota(jnp.int32, st.shape, 0)
            qpos = lax.broadcasted_iota(jnp.int32, st.shape, 1)
            st = jnp.where(kpos <= qpos, st, NEG_BIG)
        m_new = jnp.maximum(m, jnp.max(st, axis=0, keepdims=True))
        p = jnp.exp(st - m_new)
        a = jnp.exp(m - m_new)
        l = a * l + jnp.sum(p, axis=0, keepdims=True)
        pv = jnp.dot(vt, p.astype(BF16), preferred_element_type=F32)
        return m_new, l, a * acc + pv

    def steps(j, carries, masked):
        return tuple(step(h, j, carries[h], masked) for h in range(ATT_HEADS))

    init = tuple((jnp.full((1, ATT_TQ), NEG_BIG, F32), jnp.zeros((1, ATT_TQ), F32),
                  jnp.zeros((V_HEAD_DIM, ATT_TQ), F32)) for _ in range(ATT_HEADS))
    carries = lax.fori_loop(0, i, lambda j, c: steps(j, c, False), init)
    carries = steps(i, carries, True)
    for h, (m, l, acc) in enumerate(carries):
        o_ref[0, :, h * V_HEAD_DIM:(h + 1) * V_HEAD_DIM] = (acc / l).T.astype(BF16)


def _attn(q, k, v):
    assert ATT_TQ == ATT_TK
    hb = ATT_HEADS
    return pl.pallas_call(
        _attn_body,
        grid=(BATCH, MLA_HEADS // hb, SEQ // ATT_TQ),
        in_specs=[pl.BlockSpec((1, hb, ATT_TQ, QK_DIM), lambda b, h, i: (b, h, i, 0)),
                  pl.BlockSpec((1, hb, SEQ, QK_DIM), lambda b, h, i: (b, h, 0, 0)),
                  pl.BlockSpec((1, hb, V_HEAD_DIM, SEQ), lambda b, h, i: (b, h, 0, 0))],
        out_specs=pl.BlockSpec((1, ATT_TQ, hb * V_HEAD_DIM), lambda b, h, i: (b, i, h)),
        out_shape=jax.ShapeDtypeStruct((BATCH, SEQ, MLA_WIDTH), BF16),
        compiler_params=_cparams(("parallel", "parallel", "arbitrary")),
        name="attn",
    )(q, k, v)


def _mixout_body(x_ref, mod_ref, attn_ref, u_ref, vs_ref, wsp_ref, bsp_ref, woa_ref, wos_ref,
                 g1_ref, b1_ref, wr_ref, br_ref, x1_ref, h2_ref, lg_ref, sgu_scr):
    r = lax.broadcasted_iota(jnp.int32, (SGU_CHUNK, SGU_CHUNK), 0)
    c = lax.broadcasted_iota(jnp.int32, (SGU_CHUNK, SGU_CHUNK), 1)
    causal = c <= r
    for g in range(SGU_GROUPS):
        ws = jnp.where(causal, wsp_ref[g], 0.0).astype(BF16)
        bias = bsp_ref[:, g:g + 1]
        cols = slice(g * SGU_GROUP_DIM, (g + 1) * SGU_GROUP_DIM)
        for ch in range(MIX_TM // SGU_CHUNK):
            rows = slice(ch * SGU_CHUNK, (ch + 1) * SGU_CHUNK)
            mixed = jnp.dot(ws, vs_ref[rows, cols], preferred_element_type=F32) + bias
            sgu_scr[rows, cols] = (u_ref[rows, cols] * mixed).astype(BF16)
    y = (jnp.dot(attn_ref[...], woa_ref[...], preferred_element_type=F32)
         + jnp.dot(sgu_scr[...], wos_ref[...], preferred_element_type=F32))
    gate1 = mod_ref[0, 2:3, :]
    sh2 = mod_ref[0, 3:4, :]
    sc2 = mod_ref[0, 4:5, :]
    x1 = _ln_rows(DEEPNORM_ALPHA * x_ref[...] + gate1 * y) * g1_ref[...] + b1_ref[...]
    x1_ref[...] = x1
    h2 = _ln_rows(x1) * (1.0 + sc2) + sh2
    _store_token_tiles(h2_ref, _pack_halves(h2))
    lg_ref[...] = jnp.dot(h2.astype(BF16), wr_ref[...], preferred_element_type=F32) + br_ref[...]


def _mixout(x2, mod3, attn, u, vs, wsp, bsp_t, wo, g1, b1, wr, br):
    tm = MIX_TM
    tpb = SEQ // tm
    row = lambda w: pl.BlockSpec((tm, w), lambda i: (i, 0))
    wo_half = lambda j: pl.BlockSpec((MLA_WIDTH, D_MODEL), lambda i: (j, 0), pipeline_mode=pl.Buffered(1))
    return pl.pallas_call(
        _mixout_body,
        grid=(N_TOK // tm,),
        in_specs=[row(D_MODEL),
                  pl.BlockSpec((1, N_MOD, D_MODEL), lambda i: (i // tpb, 0, 0)),
                  row(MLA_WIDTH), row(SGU_WIDTH), row(SGU_WIDTH),
                  _const_spec(wsp.shape), _const_spec(bsp_t.shape), wo_half(0), wo_half(1),
                  _const_spec(g1.shape), _const_spec(b1.shape), _const_spec(wr.shape), _const_spec(br.shape)],
        out_specs=[row(D_MODEL), pl.BlockSpec((tm * SUBLANES, LANES), lambda i: (i, 0)), row(LANES)],
        out_shape=[jax.ShapeDtypeStruct((N_TOK, D_MODEL), F32),
                   jax.ShapeDtypeStruct((N_TOK * SUBLANES, LANES), U32),
                   jax.ShapeDtypeStruct((N_TOK, LANES), F32)],
        scratch_shapes=[pltpu.VMEM((tm, SGU_WIDTH), BF16)],
        compiler_params=_cparams(("parallel",)),
        name="mix_out",
    )(x2, mod3, attn, u, vs, wsp, bsp_t, wo, wo, g1, b1, wr, br)


def _route_math(lg):
    lane = lax.broadcasted_iota(jnp.int32, lg.shape, 1)
    big = jnp.int32(LANES)

    def top1(vals):
        m = jnp.max(vals, axis=-1, keepdims=True)
        idx = jnp.min(jnp.where(vals == m, lane, big), axis=-1, keepdims=True)
        return m, idx

    is_group = lane < N_GROUPS
    glog = jnp.where(is_group, lg, -jnp.inf)
    gmax, gidx = top1(glog)
    pg_top = 1.0 / jnp.sum(jnp.exp(glog - gmax), axis=-1, keepdims=True)
    eid = lane - N_GROUPS
    sel = (eid >= gidx * EXPERTS_PER_GROUP) & (eid < (gidx + 1) * EXPERTS_PER_GROUP)
    elog = jnp.where(sel, lg, -jnp.inf)
    m1, i1 = top1(elog)
    m2, i2 = top1(jnp.where(lane == i1, -jnp.inf, elog))
    e2 = jnp.exp(m2 - m1)
    w1 = pg_top / (1.0 + e2)
    w2 = pg_top * e2 / (1.0 + e2)
    return jnp.where(lane == 0, (i1 - N_GROUPS).astype(F32),
                     jnp.where(lane == 1, (i2 - N_GROUPS).astype(F32),
                               jnp.where(lane == 2, w1, jnp.where(lane == 3, w2, 0.0))))


def _rank_math(rt, counts):
    t = rt.shape[0]
    lane = lax.broadcasted_iota(jnp.int32, (t, LANES), 1).astype(F32)
    oh0 = lane == rt[:, 0:1]
    oh1 = lane == rt[:, 1:2]
    s = jnp.where(oh0 | oh1, 1.0, 0.0)
    r = lax.broadcasted_iota(jnp.int32, (t, t), 0)
    c = lax.broadcasted_iota(jnp.int32, (t, t), 1)
    before = jnp.where(c < r, 1.0, 0.0).astype(BF16)
    csum = jnp.dot(before, s.astype(BF16), preferred_element_type=F32) + counts
    rank0 = jnp.sum(jnp.where(oh0, csum, 0.0), axis=-1, keepdims=True)
    rank1 = jnp.sum(jnp.where(oh1, csum, 0.0), axis=-1, keepdims=True)
    return rank0, rank1, counts + jnp.sum(s, axis=0, keepdims=True)


def _route_body(lg_ref, rt_ref, cnt_ref, cnt_scr):
    @pl.when(pl.program_id(0) == 0)
    def _():
        cnt_scr[...] = jnp.zeros_like(cnt_scr)

    rt = _route_math(lg_ref[...])
    rank0, rank1, counts = _rank_math(rt, cnt_scr[...])
    cnt_scr[...] = counts
    lane = lax.broadcasted_iota(jnp.int32, rt.shape, 1)
    rt_ref[...] = jnp.where(lane == RT_RANK, rank0, jnp.where(lane == RT_RANK + 1, rank1, rt))
    cnt_ref[...] = jnp.broadcast_to(counts, cnt_ref.shape)


def _route(logits):
    tm = ROUTE_TM
    return pl.pallas_call(
        _route_body,
        grid=(N_TOK // tm,),
        in_specs=[pl.BlockSpec((tm, LANES), lambda i: (i, 0))],
        out_specs=[pl.BlockSpec((tm, LANES), lambda i: (i, 0)),
                   pl.BlockSpec((SUBLANES, LANES), lambda i: (0, 0))],
        out_shape=[jax.ShapeDtypeStruct((N_TOK, LANES), F32),
                   jax.ShapeDtypeStruct((SUBLANES, LANES), F32)],
        scratch_shapes=[pltpu.VMEM((1, LANES), F32)],
        compiler_params=_cparams(("arbitrary",)),
        name="route",
    )(logits)


def _plan_body(rt_ref, cnt_ref, pos_ref, tt_ref):
    t = PLAN_TM
    lane = lax.broadcasted_iota(jnp.int32, (t, LANES), 1)
    rt = rt_ref[...]
    oh0 = lane.astype(F32) == rt[:, 0:1]
    oh1 = lane.astype(F32) == rt[:, 1:2]
    counts = cnt_ref[0:1, :]
    tiles = jnp.floor((counts + (MOE_TM - 1)) * (1.0 / MOE_TM))
    r = lax.broadcasted_iota(jnp.int32, (LANES, LANES), 0)
    c = lax.broadcasted_iota(jnp.int32, (LANES, LANES), 1)
    upto = jnp.where(r <= c, 1.0, 0.0).astype(BF16)
    tiles8 = jnp.broadcast_to(tiles, (SUBLANES, LANES)).astype(BF16)
    tile_end = jnp.dot(tiles8, upto, preferred_element_type=F32)[0:1]
    offs = (tile_end - tiles) * MOE_TM
    p0 = jnp.sum(jnp.where(oh0, offs, 0.0), axis=-1, keepdims=True) + rt[:, RT_RANK:RT_RANK + 1]
    p1 = jnp.sum(jnp.where(oh1, offs, 0.0), axis=-1, keepdims=True) + rt[:, RT_RANK + 1:RT_RANK + 2]
    pos_ref[...] = (jnp.where(lane == 0, p0, jnp.where(lane == 1, p1, 0.0)) * SUBLANES).astype(jnp.int32)
    tt_ref[...] = _tile_table(tiles, tile_end)


def _tile_table(tiles, tile_end):
    lane_e = lax.broadcasted_iota(jnp.int32, (LANES, LANES), 1)
    lane_f = lane_e.astype(F32)
    tile_id = lax.broadcasted_iota(jnp.int32, (LANES, LANES), 0).astype(F32)
    is_e = lane_e < N_EXPERTS
    owns = is_e & (tiles > 0)

    def count(cond):
        return jnp.sum(jnp.where(cond, 1.0, 0.0), axis=-1, keepdims=True)

    def first_owner_after(e):
        return jnp.min(jnp.where(owns & (lane_f > e), lane_f, jnp.float32(LANES)), axis=-1, keepdims=True)

    total = jnp.max(tile_end, axis=-1, keepdims=True)
    t_valid = jnp.where(tile_id[:, 0:1] < total, 1.0, 0.0)
    t_exp = jnp.where(t_valid > 0, count(is_e & (tile_end <= tile_id)), count(is_e & (tile_end <= total - 1.0)))
    t_first = count(owns & ((tile_end - tiles) == tile_id))
    t_clear = jnp.maximum(count(owns & ((tile_end - 1.0) == tile_id)), 1.0 - t_valid)
    t_next = first_owner_after(t_exp)
    t_next2 = first_owner_after(t_next)
    t_next3 = first_owner_after(t_next2)
    cols = {T_EXP: t_exp, T_VALID: t_valid, T_FIRST: t_first, T_CLEAR: t_clear, T_NEXT: t_next,
            T_NEXT2: t_next2, T_NEXT3: t_next3, T_ORD: count(owns & (lane_f < t_exp)), T_LAST: total - 1.0}
    table = jnp.zeros((LANES, LANES), F32)
    for k, col in cols.items():
        table = jnp.where(lane_e == k, col, table)
    return table.astype(jnp.int32)


def _plan(route, counts):
    t = PLAN_TM
    return pl.pallas_call(
        _plan_body,
        grid=(N_TOK // t,),
        in_specs=[pl.BlockSpec((t, LANES), lambda i: (i, 0)),
                  pl.BlockSpec((SUBLANES, LANES), lambda i: (0, 0))],
        out_specs=[pl.BlockSpec((t, LANES), lambda i: (i, 0)),
                   pl.BlockSpec((LANES, LANES), lambda i: (0, 0))],
        out_shape=[jax.ShapeDtypeStruct((N_TOK, LANES), jnp.int32),
                   jax.ShapeDtypeStruct((LANES, LANES), jnp.int32)],
        compiler_params=_cparams(("arbitrary",)),
        name="plan",
    )(route, counts)


def _rows_wait(ref, n_rows, sem):
    pltpu.make_async_copy(ref.at[pl.ds(0, n_rows)], ref.at[pl.ds(0, n_rows)], sem).wait()


def _dispatch_body(pos_ref, tt_ref, h_ref, xs_hbm, zbuf, sem_z, sem_s):
    i = pl.program_id(0)
    tile_rows = MOE_TM * SUBLANES

    @pl.when(i == 0)
    def _():
        zbuf[...] = _pack_halves(jnp.zeros((tile_rows, 2 * LANES), F32))

        def zero_copy(tile):
            start = pl.multiple_of(tile * tile_rows, tile_rows)
            return pltpu.make_async_copy(zbuf, xs_hbm.at[pl.ds(start, tile_rows)], sem_z)

        def clear_start(tile, carry):
            @pl.when(tt_ref[tile, T_CLEAR] > 0)
            def _():
                zero_copy(tile).start()
            return carry

        def clear_wait(tile, carry):
            @pl.when(tt_ref[tile, T_CLEAR] > 0)
            def _():
                zero_copy(tile).wait()
            return carry

        lax.fori_loop(0, MOE_TILES, clear_start, 0)
        lax.fori_loop(0, MOE_TILES, clear_wait, 0)

    def tok(j, carry):
        src = h_ref.at[pl.ds(pl.multiple_of(j * SUBLANES, SUBLANES), SUBLANES)]
        pair = TOP_K * (i * DISPATCH_TM + j)
        for k in range(TOP_K):
            dst_row = pl.multiple_of(pos_ref[pair + k], SUBLANES)
            pltpu.make_async_copy(src, xs_hbm.at[pl.ds(dst_row, SUBLANES)], sem_s).start(priority=k)
        return carry

    lax.fori_loop(0, DISPATCH_TM, tok, 0, unroll=8)
    _rows_wait(xs_hbm, TOP_K * DISPATCH_TM * SUBLANES, sem_s)


def _dispatch(pos_rows, tile_clear, h2p):
    grid_spec = pltpu.PrefetchScalarGridSpec(
        num_scalar_prefetch=2,
        grid=(N_TOK // DISPATCH_TM,),
        in_specs=[pl.BlockSpec((DISPATCH_TM * SUBLANES, LANES), lambda i, *_: (i, 0))],
        out_specs=pl.BlockSpec(memory_space=pl.ANY),
        scratch_shapes=[pltpu.VMEM((MOE_TM * SUBLANES, LANES), U32),
                        pltpu.SemaphoreType.DMA(()), pltpu.SemaphoreType.DMA(())],
    )
    return pl.pallas_call(
        _dispatch_body,
        grid_spec=grid_spec,
        out_shape=jax.ShapeDtypeStruct((MOE_ROWS * SUBLANES, LANES), U32),
        compiler_params=_cparams(("arbitrary",)),
        name="dispatch",
    )(pos_rows, tile_clear, h2p)


def _moe_body(tt_ref, x_ref, wg_hbm, wu_hbm, wd_hbm, y_ref,
              wg_s, wu_s, wd_s, stg_g, stg_u, stg_d, sem):
    i = pl.program_id(0)
    ahead = (T_EXP, T_NEXT, T_NEXT2, T_NEXT3)

    def fetch(e, slot):
        return (pltpu.make_async_copy(wg_hbm.at[e], stg_g.at[slot], sem.at[slot, 0]),
                pltpu.make_async_copy(wu_hbm.at[e], stg_u.at[slot], sem.at[slot, 1]),
                pltpu.make_async_copy(wd_hbm.at[e], stg_d.at[slot], sem.at[slot, 2]))

    @pl.when(i == 0)
    def _():
        for d in range(MOE_SLOTS):
            @pl.when(tt_ref[0, ahead[d]] < N_EXPERTS)
            def _():
                for cp in fetch(tt_ref[0, ahead[d]], d):
                    cp.start(priority=WEIGHT_DMA_PRIORITY)

    @pl.when(tt_ref[i, T_FIRST] > 0)
    def _():
        slot = tt_ref[i, T_ORD] % MOE_SLOTS
        for cp in fetch(tt_ref[i, T_EXP], slot):
            cp.wait()
        wg_s[...] = stg_g[slot].astype(BF16)
        wu_s[...] = stg_u[slot].astype(BF16)
        wd_s[...] = stg_d[slot].astype(BF16)

        @pl.when(tt_ref[i, ahead[MOE_SLOTS]] < N_EXPERTS)
        def _():
            for cp in fetch(tt_ref[i, ahead[MOE_SLOTS]], slot):
                cp.start(priority=WEIGHT_DMA_PRIORITY)

    @pl.when(tt_ref[i, T_VALID] > 0)
    def _():
        lo, hi = _unpack_halves(_load_token_tiles(x_ref, 0, MOE_TM))
        xa = lo.astype(BF16)
        xb = hi.astype(BF16)
        g = (jnp.dot(xa, wg_s[:HALF_D, :], preferred_element_type=F32)
             + jnp.dot(xb, wg_s[HALF_D:, :], preferred_element_type=F32))
        u = (jnp.dot(xa, wu_s[:HALF_D, :], preferred_element_type=F32)
             + jnp.dot(xb, wu_s[HALF_D:, :], preferred_element_type=F32))
        hid = (g * jax.nn.sigmoid(g) * u).astype(BF16)
        _store_token_tiles(y_ref, _pack_halves(jnp.dot(hid, wd_s[...], preferred_element_type=F32)))

    @pl.when(tt_ref[i, T_VALID] == 0)
    def _():
        y_ref[...] = _pack_halves(jnp.zeros((MOE_TM * SUBLANES, 2 * LANES), F32))


def _moe(tile_tab, xs, wg, wu, wd):
    tm = MOE_TM
    grid_spec = pltpu.PrefetchScalarGridSpec(
        num_scalar_prefetch=1,
        grid=(MOE_TILES,),
        in_specs=[pl.BlockSpec((tm * SUBLANES, LANES), lambda i, tt: (jnp.minimum(i, tt[0, T_LAST]), 0)),
                  pl.BlockSpec(memory_space=pl.ANY), pl.BlockSpec(memory_space=pl.ANY),
                  pl.BlockSpec(memory_space=pl.ANY)],
        out_specs=pl.BlockSpec((tm * SUBLANES, LANES), lambda i, *_: (i, 0)),
        scratch_shapes=[pltpu.VMEM((D_MODEL, EXPERT_FF), BF16), pltpu.VMEM((D_MODEL, EXPERT_FF), BF16),
                        pltpu.VMEM((EXPERT_FF, D_MODEL), BF16),
                        pltpu.VMEM((MOE_SLOTS, D_MODEL, EXPERT_FF), F32),
                        pltpu.VMEM((MOE_SLOTS, D_MODEL, EXPERT_FF), F32),
                        pltpu.VMEM((MOE_SLOTS, EXPERT_FF, D_MODEL), F32),
                        pltpu.SemaphoreType.DMA((MOE_SLOTS, 3))],
    )
    return pl.pallas_call(
        _moe_body,
        grid_spec=grid_spec,
        out_shape=jax.ShapeDtypeStruct((MOE_ROWS * SUBLANES, LANES), U32),
        compiler_params=_cparams(("arbitrary",)),
        name="moe",
    )(tile_tab, xs, wg, wu, wd)


def _final_body(pos_ref, x1_ref, mod_ref, rt_ref, g2_ref, b2_ref, ys_hbm, o_ref, buf, sem):
    i = pl.program_id(0)
    n = pl.num_programs(0)
    tm = FINAL_TM
    slot = i % FINAL_SLOTS
    nxt_slot = (i + 2) % FINAL_SLOTS
    nxt_tile = jnp.minimum(i + 2, n - 1)

    def issue(tile, dst_slot, j):
        pair = TOP_K * (tile * tm + j)
        for k in range(TOP_K):
            src_row = pl.multiple_of(pos_ref[pair + k], SUBLANES)
            dst_row = pl.multiple_of((k * tm + j) * SUBLANES, SUBLANES)
            pltpu.make_async_copy(ys_hbm.at[pl.ds(src_row, SUBLANES)],
                                  buf.at[dst_slot, pl.ds(dst_row, SUBLANES)], sem.at[dst_slot]).start(priority=k)

    def wait(s):
        pltpu.make_async_copy(ys_hbm.at[pl.ds(0, TOP_K * tm * SUBLANES)], buf.at[s], sem.at[s]).wait()

    @pl.when(i == 0)
    def _():
        def tok(j, carry):
            issue(0, 0, j)
            issue(jnp.minimum(1, n - 1), 1, j)
            return carry
        lax.fori_loop(0, tm, tok, 0, unroll=8)

    wait(slot)
    gate2 = mod_ref[0, 5:6, :]
    cur = buf.at[slot]

    def chunk(c, carry):
        r0 = pl.multiple_of(c * FINAL_CHUNK, FINAL_CHUNK)
        rows = pl.ds(r0, FINAL_CHUNK)
        a_lo, a_hi = _unpack_halves(_load_token_tiles(cur, r0, FINAL_CHUNK))
        b_lo, b_hi = _unpack_halves(_load_token_tiles(cur, tm + r0, FINAL_CHUNK))
        x1 = x1_ref[rows, :]
        rt = rt_ref[rows, :]
        for r in range(FINAL_CHUNK):
            issue(nxt_tile, nxt_slot, r0 + r)
        w0 = rt[:, 2:3]
        w1 = rt[:, 3:4]
        y = jnp.concatenate([w0 * a_lo + w1 * b_lo, w0 * a_hi + w1 * b_hi], axis=1)
        o_ref[rows, :] = _ln_rows(DEEPNORM_ALPHA * x1 + gate2 * y) * g2_ref[...] + b2_ref[...]
        return carry

    lax.fori_loop(0, tm // FINAL_CHUNK, chunk, 0)

    @pl.when(i == n - 1)
    def _():
        wait((i + 1) % FINAL_SLOTS)
        wait(nxt_slot)


def _final(pos_rows, x1, mod3, route, g2, b2, ys):
    tm = FINAL_TM
    tpb = SEQ // tm
    row = lambda w: pl.BlockSpec((tm, w), lambda i, *_: (i, 0))
    grid_spec = pltpu.PrefetchScalarGridSpec(
        num_scalar_prefetch=1,
        grid=(N_TOK // tm,),
        in_specs=[row(D_MODEL),
                  pl.BlockSpec((1, N_MOD, D_MODEL), lambda i, *_: (i // tpb, 0, 0)),
                  row(LANES),
                  pl.BlockSpec(g2.shape, lambda i, *_: (0, 0)),
                  pl.BlockSpec(b2.shape, lambda i, *_: (0, 0)),
                  pl.BlockSpec(memory_space=pl.ANY)],
        out_specs=row(D_MODEL),
        scratch_shapes=[pltpu.VMEM((FINAL_SLOTS, TOP_K * tm * SUBLANES, LANES), U32),
                        pltpu.SemaphoreType.DMA((FINAL_SLOTS,))],
    )
    return pl.pallas_call(
        _final_body,
        grid_spec=grid_spec,
        out_shape=jax.ShapeDtypeStruct((N_TOK, D_MODEL), F32),
        compiler_params=_cparams(("arbitrary",)),
        name="final",
    )(pos_rows, x1, mod3, route, g2, b2, ys)


def kernel(x, c, positions, w_ada, b_ada, w_in, q_norm_g, w_uq, kv_norm_g, w_ukv, sgu_norm_g, sgu_norm_b,
           w_spatial, b_spatial, w_o, ln1_g, ln1_b, w_router_group, b_router_group, w_router_expert,
           b_router_expert, w_gate, w_up, w_down, ln2_g, ln2_b):
    l = 0
    x2 = x.reshape(N_TOK, D_MODEL)
    mod3 = _ada(c, w_ada[l], b_ada[l][None, :]).reshape(BATCH, N_MOD, D_MODEL)

    wt, wuq, wukv, wo = _prep(w_in[l].T, w_uq[l], w_ukv[l], w_o[l])
    n_r = N_GROUPS + N_EXPERTS
    wr = jnp.pad(jnp.concatenate([w_router_group[l], w_router_expert[l]], axis=1),
                 ((0, 0), (0, LANES - n_r))).astype(BF16)
    br = jnp.pad(jnp.concatenate([b_router_group[l], b_router_expert[l]]), (0, LANES - n_r))[None, :]
    inv_freq = 1.0 / (ROPE_THETA ** (jnp.arange(0, QK_ROPE_DIM, 2, dtype=F32) / QK_ROPE_DIM))
    invf = jnp.tile(inv_freq, 2 * LANES // QK_ROPE_DIM)[None, :]

    cqn, ckvn, kpe, u, vs = _inproj(x2, mod3, wt, q_norm_g[l][None, :], kv_norm_g[l][None, :],
                                    sgu_norm_g[l][None, :], sgu_norm_b[l][None, :])
    q, k, v = _qkv(cqn, ckvn, kpe, positions.reshape(N_TOK, 1), invf, wuq, wukv)
    attn = _attn(q, k, v).reshape(N_TOK, MLA_WIDTH)
    x1, h2, logits = _mixout(x2, mod3, attn, u, vs, w_spatial[l], b_spatial[l].T, wo,
                             ln1_g[l][None, :], ln1_b[l][None, :], wr, br)
    route, counts = _route(logits)
    pos_tab, tile_tab = _plan(route, counts)
    pos_rows = pos_tab[:, 0:TOP_K].reshape(-1)
    xs = _dispatch(pos_rows, tile_tab, h2)
    ys = _moe(tile_tab, xs, w_gate[l], w_up[l], w_down[l])
    out = _final(pos_rows, x1, mod3, route, ln2_g[l][None, :], ln2_b[l][None, :], ys)
    return out.reshape(BATCH, SEQ, D_MODEL)
```

```python
import jax
import jax.numpy as jnp
import numpy as np
from jax import lax
from jax.experimental import pallas as pl
from jax.experimental.pallas import tpu as pltpu

D_MODEL = 2048
BATCH = 4
SEQ = 2048
N_TOK = BATCH * SEQ

MLA_HEADS = 8
QK_NOPE_DIM = 128
QK_ROPE_DIM = 64
QK_DIM = QK_NOPE_DIM + QK_ROPE_DIM
V_HEAD_DIM = 128
Q_LORA_RANK = 768
KV_LORA_RANK = 512
ROPE_THETA = 10000.0
MLA_WIDTH = MLA_HEADS * V_HEAD_DIM

SGU_GROUPS = 8
SGU_GROUP_DIM = 128
SGU_CHUNK = 128
SGU_WIDTH = SGU_GROUPS * SGU_GROUP_DIM

N_GROUPS = 4
EXPERTS_PER_GROUP = 8
N_EXPERTS = N_GROUPS * EXPERTS_PER_GROUP
TOP_K = 2
EXPERT_FF = 512

DEEPNORM_ALPHA = 2.0 ** 0.25
EPS = 1e-6
N_MOD = 6
NEG_BIG = -1e30

LANES = 128
SUBLANES = 8
V7X_VMEM_BYTES = 64 * 1024 * 1024
VMEM_LIMIT = V7X_VMEM_BYTES * 7 // 8

ADA_TN = 1024
TOK_TM = 512
MIX_TM = 256
PREP_STEPS = 4
ATT_TQ = 1024
ATT_TK = 1024
ATT_HEADS = 2
MOE_TM = 256
MOE_TILES = (N_TOK * TOP_K + N_EXPERTS * (MOE_TM - 1)) // MOE_TM + 1
MOE_ROWS = MOE_TILES * MOE_TM
PLAN_TM = 2048
ROUTE_TM = 1024
RT_RANK = 4
DISPATCH_TM = 1024
FINAL_TM = 256
FINAL_CHUNK = 128
FINAL_SLOTS = 3
assert MOE_TILES <= LANES
T_EXP, T_VALID, T_FIRST, T_CLEAR, T_NEXT, T_NEXT2, T_NEXT3, T_ORD, T_LAST = range(9)
MOE_SLOTS = 2
WEIGHT_DMA_PRIORITY = 1

F32 = jnp.float32
BF16 = jnp.bfloat16
U32 = jnp.uint32
HALF_D = D_MODEL // 2


def _cparams(sem):
    return pltpu.CompilerParams(dimension_semantics=sem, vmem_limit_bytes=VMEM_LIMIT)


def _const_spec(shape):
    nd = len(shape)
    return pl.BlockSpec(shape, lambda *_: (0,) * nd, pipeline_mode=pl.Buffered(1))


def _ln_rows(x):
    mu = jnp.mean(x, axis=-1, keepdims=True)
    xc = x - mu
    var = jnp.mean(xc * xc, axis=-1, keepdims=True)
    return xc * lax.rsqrt(var + EPS)


def _rms_rows(x):
    return x * lax.rsqrt(jnp.mean(x * x, axis=-1, keepdims=True) + EPS)


def _pack_halves(x):
    half = x.shape[-1] // 2
    return pltpu.pack_elementwise([x[:, :half], x[:, half:]], packed_dtype=BF16)


def _unpack_halves(w):
    lo = pltpu.unpack_elementwise(w, index=0, packed_dtype=BF16, unpacked_dtype=F32)
    hi = pltpu.unpack_elementwise(w, index=1, packed_dtype=BF16, unpacked_dtype=F32)
    return lo, hi


def _store_token_tiles(ref, w):
    rows = w.shape[0]
    for s in range(SUBLANES):
        ref[pl.ds(s, rows, stride=SUBLANES), :] = w[:, s * LANES:(s + 1) * LANES]


def _load_token_tiles(ref, start_row, rows):
    return jnp.concatenate([ref[pl.ds(start_row * SUBLANES + s, rows, stride=SUBLANES), :]
                            for s in range(SUBLANES)], axis=1)


def _gelu_tanh(x):
    c = np.sqrt(2.0 / np.pi).astype(np.float32)
    return 0.5 * x * (1.0 + jnp.tanh(c * (x + 0.044715 * (x * x * x))))


def _ada_body(c_ref, w_ref, b_ref, o_ref):
    o_ref[...] = jnp.dot(c_ref[...].astype(BF16), w_ref[...].astype(BF16),
                         preferred_element_type=F32) + b_ref[...]


def _ada(c, w, b):
    n = w.shape[1]
    return pl.pallas_call(
        _ada_body,
        grid=(n // ADA_TN,),
        in_specs=[pl.BlockSpec((BATCH, D_MODEL), lambda j: (0, 0)),
                  pl.BlockSpec((D_MODEL, ADA_TN), lambda j: (0, j)),
                  pl.BlockSpec((1, ADA_TN), lambda j: (0, j))],
        out_specs=pl.BlockSpec((BATCH, ADA_TN), lambda j: (0, j)),
        out_shape=jax.ShapeDtypeStruct((BATCH, n), F32),
        compiler_params=_cparams(("parallel",)),
        name="ada",
    )(c, w, b)


def _prep_body(win_ref, wuq_ref, wukv_ref, wo_ref, win_o, wuq_o, wukv_o, wo_o):
    win_o[...] = win_ref[...].astype(BF16)
    u = wuq_ref[...]
    nope = [u[:, h * QK_DIM:h * QK_DIM + QK_NOPE_DIM] for h in range(MLA_HEADS)]
    rope = [u[:, h * QK_DIM + QK_NOPE_DIM:(h + 1) * QK_DIM] for h in range(MLA_HEADS)]
    wuq_o[...] = jnp.concatenate(nope + rope, axis=1).astype(BF16)
    kv = wukv_ref[...]
    hw = QK_NOPE_DIM + V_HEAD_DIM
    kn = [kv[:, h * hw:h * hw + QK_NOPE_DIM] for h in range(MLA_HEADS)]
    vv = [kv[:, h * hw + QK_NOPE_DIM:(h + 1) * hw] for h in range(MLA_HEADS)]
    wukv_o[...] = jnp.concatenate(kn + vv, axis=1).astype(BF16)
    wo_o[...] = wo_ref[...].astype(BF16)


def _prep(w_in_t, w_uq, w_ukv, w_o):
    steps = PREP_STEPS
    blk = lambda a: pl.BlockSpec((a.shape[0] // steps, a.shape[1]), lambda i: (i, 0))
    ins = (w_in_t, w_uq, w_ukv, w_o)
    return pl.pallas_call(
        _prep_body,
        grid=(steps,),
        in_specs=[blk(a) for a in ins],
        out_specs=[blk(a) for a in ins],
        out_shape=[jax.ShapeDtypeStruct(a.shape, BF16) for a in ins],
        compiler_params=_cparams(("parallel",)),
        name="prep",
    )(*ins)


def _inproj_body(x_ref, mod_ref, wt_ref, gq_ref, gkv_ref, sg_ref, sb_ref,
                 cq_ref, ckv_ref, kpe_ref, u_ref, vs_ref):
    o1, o2, o3 = Q_LORA_RANK, Q_LORA_RANK + KV_LORA_RANK, Q_LORA_RANK + KV_LORA_RANK + QK_ROPE_DIM

    def proj(lo, hi):
        return lax.dot_general(h, wt_ref[lo:hi, :], (((1,), (1,)), ((), ())), preferred_element_type=F32)

    sh = mod_ref[0, 0:1, :]
    sc = mod_ref[0, 1:2, :]
    h = (_ln_rows(x_ref[...]) * (1.0 + sc) + sh).astype(BF16)
    cq_ref[...] = (_rms_rows(proj(0, o1)) * gq_ref[...]).astype(BF16)
    ckv_ref[...] = (_rms_rows(proj(o1, o2)) * gkv_ref[...]).astype(BF16)
    kpe_ref[...] = proj(o2, o2 + LANES)
    gz = _gelu_tanh(proj(o3, o3 + 2 * SGU_WIDTH))
    u_ref[...] = gz[:, :SGU_WIDTH]
    vs_ref[...] = (_ln_rows(gz[:, SGU_WIDTH:]) * sg_ref[...] + sb_ref[...]).astype(BF16)


def _inproj(x2, mod3, wt, gq, gkv, sg, sb):
    tm = TOK_TM
    tiles_per_batch = SEQ // tm
    row = lambda w: pl.BlockSpec((tm, w), lambda i: (i, 0))
    return pl.pallas_call(
        _inproj_body,
        grid=(N_TOK // tm,),
        in_specs=[row(D_MODEL),
                  pl.BlockSpec((1, N_MOD, D_MODEL), lambda i: (i // tiles_per_batch, 0, 0)),
                  _const_spec(wt.shape),
                  _const_spec(gq.shape), _const_spec(gkv.shape), _const_spec(sg.shape), _const_spec(sb.shape)],
        out_specs=[row(Q_LORA_RANK), row(KV_LORA_RANK), row(LANES), row(SGU_WIDTH), row(SGU_WIDTH)],
        out_shape=[jax.ShapeDtypeStruct((N_TOK, Q_LORA_RANK), BF16),
                   jax.ShapeDtypeStruct((N_TOK, KV_LORA_RANK), BF16),
                   jax.ShapeDtypeStruct((N_TOK, LANES), F32),
                   jax.ShapeDtypeStruct((N_TOK, SGU_WIDTH), F32),
                   jax.ShapeDtypeStruct((N_TOK, SGU_WIDTH), BF16)],
        compiler_params=_cparams(("parallel",)),
        name="inproj",
    )(x2, mod3, wt, gq, gkv, sg, sb)


def _rope(x, cos, sin):
    w = x.shape[-1]
    lane = lax.broadcasted_iota(jnp.int32, x.shape, 1)
    first_half = (lane % QK_ROPE_DIM) < (QK_ROPE_DIM // 2)
    rot = jnp.where(first_half,
                    -pltpu.roll(x, w - QK_ROPE_DIM // 2, 1),
                    pltpu.roll(x, QK_ROPE_DIM // 2, 1))
    return x * cos + rot * sin


def _qkv_body(cq_ref, ckv_ref, kpe_ref, pos_ref, invf_ref, wuq_ref, wukv_ref, q_ref, k_ref, v_ref):
    ang = pos_ref[...].astype(F32) * invf_ref[...]
    cos1 = jnp.cos(ang)
    sin1 = jnp.sin(ang)
    reps = MLA_HEADS * QK_ROPE_DIM // LANES
    cos = jnp.concatenate([cos1] * reps, axis=1)
    sin = jnp.concatenate([sin1] * reps, axis=1)
    scale = np.float32(QK_DIM ** -0.5)
    q = jnp.dot(cq_ref[...], wuq_ref[...], preferred_element_type=F32) * scale
    q_pe = _rope(q[:, MLA_HEADS * QK_NOPE_DIM:], cos, sin)
    kv = jnp.dot(ckv_ref[...], wukv_ref[...], preferred_element_type=F32)
    k_pe = _rope(kpe_ref[...], cos1, sin1)[:, :QK_ROPE_DIM].astype(BF16)
    for h in range(MLA_HEADS):
        q_ref[0, h, :, 0:QK_NOPE_DIM] = q[:, h * QK_NOPE_DIM:(h + 1) * QK_NOPE_DIM].astype(BF16)
        q_ref[0, h, :, QK_NOPE_DIM:QK_DIM] = q_pe[:, h * QK_ROPE_DIM:(h + 1) * QK_ROPE_DIM].astype(BF16)
        k_ref[0, h, :, 0:QK_NOPE_DIM] = kv[:, h * QK_NOPE_DIM:(h + 1) * QK_NOPE_DIM].astype(BF16)
        k_ref[0, h, :, QK_NOPE_DIM:QK_DIM] = k_pe
        v_ref[0, h, :, :] = kv[:, MLA_WIDTH + h * V_HEAD_DIM:MLA_WIDTH + (h + 1) * V_HEAD_DIM].T.astype(BF16)


def _qkv(cqn, ckvn, kpe, pos2, invf, wuq, wukv):
    tm = TOK_TM
    tpb = SEQ // tm
    row = lambda w: pl.BlockSpec((tm, w), lambda i: (i, 0))
    head_out = lambda w: pl.BlockSpec((1, MLA_HEADS, tm, w), lambda i: (i // tpb, 0, i % tpb, 0))
    return pl.pallas_call(
        _qkv_body,
        grid=(N_TOK // tm,),
        in_specs=[row(Q_LORA_RANK), row(KV_LORA_RANK), row(LANES), row(1),
                  _const_spec(invf.shape), _const_spec(wuq.shape), _const_spec(wukv.shape)],
        out_specs=[head_out(QK_DIM), head_out(QK_DIM),
                   pl.BlockSpec((1, MLA_HEADS, V_HEAD_DIM, tm), lambda i: (i // tpb, 0, 0, i % tpb))],
        out_shape=[jax.ShapeDtypeStruct((BATCH, MLA_HEADS, SEQ, QK_DIM), BF16),
                   jax.ShapeDtypeStruct((BATCH, MLA_HEADS, SEQ, QK_DIM), BF16),
                   jax.ShapeDtypeStruct((BATCH, MLA_HEADS, V_HEAD_DIM, SEQ), BF16)],
        compiler_params=_cparams(("parallel",)),
        name="qkv",
    )(cqn, ckvn, kpe, pos2, invf, wuq, wukv)


def _attn_body(q_ref, k_ref, v_ref, o_ref):
    i = pl.program_id(2)

    def step(h, j, carry, masked):
        m, l, acc = carry
        start = pl.multiple_of(j * ATT_TK, ATT_TK)
        k = k_ref[0, h, pl.ds(start, ATT_TK), :]
        vt = v_ref[0, h, :, pl.ds(start, ATT_TK)]
        st = lax.dot_general(k, q_ref[0, h], (((1,), (1,)), ((), ())), preferred_element_type=F32)
        if masked:
            kpos = lax.broadcasted_iota(jnp.int32, st.shape, 0)
            qpos = lax.broadcasted_iota(jnp.int32, st.shape, 1)
            st = jnp.where(kpos <= qpos, st, NEG_BIG)
        m_new = jnp.maximum(m, jnp.max(st, axis=0, keepdims=True))
        p = jnp.exp(st - m_new)
        a = jnp.exp(m - m_new)
        l = a * l + jnp.sum(p, axis=0, keepdims=True)
        pv = jnp.dot(vt, p.astype(BF16), preferred_element_type=F32)
        return m_new, l, a * acc + pv

    def steps(j, carries, masked):
        return tuple(step(h, j, carries[h], masked) for h in range(ATT_HEADS))

    init = tuple((jnp.full((1, ATT_TQ), NEG_BIG, F32), jnp.zeros((1, ATT_TQ), F32),
                  jnp.zeros((V_HEAD_DIM, ATT_TQ), F32)) for _ in range(ATT_HEADS))
    carries = lax.fori_loop(0, i, lambda j, c: steps(j, c, False), init)
    carries = steps(i, carries, True)
    for h, (m, l, acc) in enumerate(carries):
        o_ref[0, :, h * V_HEAD_DIM:(h + 1) * V_HEAD_DIM] = (acc / l).T.astype(BF16)


def _attn(q, k, v):
    assert ATT_TQ == ATT_TK
    hb = ATT_HEADS
    return pl.pallas_call(
        _attn_body,
        grid=(BATCH, MLA_HEADS // hb, SEQ // ATT_TQ),
        in_specs=[pl.BlockSpec((1, hb, ATT_TQ, QK_DIM), lambda b, h, i: (b, h, i, 0)),
                  pl.BlockSpec((1, hb, SEQ, QK_DIM), lambda b, h, i: (b, h, 0, 0)),
                  pl.BlockSpec((1, hb, V_HEAD_DIM, SEQ), lambda b, h, i: (b, h, 0, 0))],
        out_specs=pl.BlockSpec((1, ATT_TQ, hb * V_HEAD_DIM), lambda b, h, i: (b, i, h)),
        out_shape=jax.ShapeDtypeStruct((BATCH, SEQ, MLA_WIDTH), BF16),
        compiler_params=_cparams(("parallel", "parallel", "arbitrary")),
        name="attn",
    )(q, k, v)


def _mixout_body(x_ref, mod_ref, attn_ref, u_ref, vs_ref, wsp_ref, bsp_ref, woa_ref, wos_ref,
                 g1_ref, b1_ref, wr_ref, br_ref, x1_ref, h2_ref, lg_ref, sgu_scr):
    r = lax.broadcasted_iota(jnp.int32, (SGU_CHUNK, SGU_CHUNK), 0)
    c = lax.broadcasted_iota(jnp.int32, (SGU_CHUNK, SGU_CHUNK), 1)
    causal = c <= r
    for g in range(SGU_GROUPS):
        ws = jnp.where(causal, wsp_ref[g], 0.0).astype(BF16)
        bias = bsp_ref[:, g:g + 1]
        cols = slice(g * SGU_GROUP_DIM, (g + 1) * SGU_GROUP_DIM)
        for ch in range(MIX_TM // SGU_CHUNK):
            rows = slice(ch * SGU_CHUNK, (ch + 1) * SGU_CHUNK)
            mixed = jnp.dot(ws, vs_ref[rows, cols], preferred_element_type=F32) + bias
            sgu_scr[rows, cols] = (u_ref[rows, cols] * mixed).astype(BF16)
    y = (jnp.dot(attn_ref[...], woa_ref[...], preferred_element_type=F32)
         + jnp.dot(sgu_scr[...], wos_ref[...], preferred_element_type=F32))
    gate1 = mod_ref[0, 2:3, :]
    sh2 = mod_ref[0, 3:4, :]
    sc2 = mod_ref[0, 4:5, :]
    x1 = _ln_rows(DEEPNORM_ALPHA * x_ref[...] + gate1 * y) * g1_ref[...] + b1_ref[...]
    x1_ref[...] = x1
    h2 = _ln_rows(x1) * (1.0 + sc2) + sh2
    _store_token_tiles(h2_ref, _pack_halves(h2))
    lg_ref[...] = jnp.dot(h2.astype(BF16), wr_ref[...], preferred_element_type=F32) + br_ref[...]


def _mixout(x2, mod3, attn, u, vs, wsp, bsp_t, wo, g1, b1, wr, br):
    tm = MIX_TM
    tpb = SEQ // tm
    row = lambda w: pl.BlockSpec((tm, w), lambda i: (i, 0))
    wo_half = lambda j: pl.BlockSpec((MLA_WIDTH, D_MODEL), lambda i: (j, 0), pipeline_mode=pl.Buffered(1))
    return pl.pallas_call(
        _mixout_body,
        grid=(N_TOK // tm,),
        in_specs=[row(D_MODEL),
                  pl.BlockSpec((1, N_MOD, D_MODEL), lambda i: (i // tpb, 0, 0)),
                  row(MLA_WIDTH), row(SGU_WIDTH), row(SGU_WIDTH),
                  _const_spec(wsp.shape), _const_spec(bsp_t.shape), wo_half(0), wo_half(1),
                  _const_spec(g1.shape), _const_spec(b1.shape), _const_spec(wr.shape), _const_spec(br.shape)],
        out_specs=[row(D_MODEL), pl.BlockSpec((tm * SUBLANES, LANES), lambda i: (i, 0)), row(LANES)],
        out_shape=[jax.ShapeDtypeStruct((N_TOK, D_MODEL), F32),
                   jax.ShapeDtypeStruct((N_TOK * SUBLANES, LANES), U32),
                   jax.ShapeDtypeStruct((N_TOK, LANES), F32)],
        scratch_shapes=[pltpu.VMEM((tm, SGU_WIDTH), BF16)],
        compiler_params=_cparams(("parallel",)),
        name="mix_out",
    )(x2, mod3, attn, u, vs, wsp, bsp_t, wo, wo, g1, b1, wr, br)


def _route_math(lg):
    lane = lax.broadcasted_iota(jnp.int32, lg.shape, 1)
    big = jnp.int32(LANES)

    def top1(vals):
        m = jnp.max(vals, axis=-1, keepdims=True)
        idx = jnp.min(jnp.where(vals == m, lane, big), axis=-1, keepdims=True)
        return m, idx

    is_group = lane < N_GROUPS
    glog = jnp.where(is_group, lg, -jnp.inf)
    gmax, gidx = top1(glog)
    pg_top = 1.0 / jnp.sum(jnp.exp(glog - gmax), axis=-1, keepdims=True)
    eid = lane - N_GROUPS
    sel = (eid >= gidx * EXPERTS_PER_GROUP) & (eid < (gidx + 1) * EXPERTS_PER_GROUP)
    elog = jnp.where(sel, lg, -jnp.inf)
    m1, i1 = top1(elog)
    m2, i2 = top1(jnp.where(lane == i1, -jnp.inf, elog))
    e2 = jnp.exp(m2 - m1)
    w1 = pg_top / (1.0 + e2)
    w2 = pg_top * e2 / (1.0 + e2)
    return jnp.where(lane == 0, (i1 - N_GROUPS).astype(F32),
                     jnp.where(lane == 1, (i2 - N_GROUPS).astype(F32),
                               jnp.where(lane == 2, w1, jnp.where(lane == 3, w2, 0.0))))


def _rank_math(rt, counts):
    t = rt.shape[0]
    lane = lax.broadcasted_iota(jnp.int32, (t, LANES), 1).astype(F32)
    oh0 = lane == rt[:, 0:1]
    oh1 = lane == rt[:, 1:2]
    s = jnp.where(oh0 | oh1, 1.0, 0.0)
    r = lax.broadcasted_iota(jnp.int32, (t, t), 0)
    c = lax.broadcasted_iota(jnp.int32, (t, t), 1)
    before = jnp.where(c < r, 1.0, 0.0).astype(BF16)
    csum = jnp.dot(before, s.astype(BF16), preferred_element_type=F32) + counts
    rank0 = jnp.sum(jnp.where(oh0, csum, 0.0), axis=-1, keepdims=True)
    rank1 = jnp.sum(jnp.where(oh1, csum, 0.0), axis=-1, keepdims=True)
    return rank0, rank1, counts + jnp.sum(s, axis=0, keepdims=True)


def _route_body(lg_ref, rt_ref, cnt_ref, cnt_scr):
    @pl.when(pl.program_id(0) == 0)
    def _():
        cnt_scr[...] = jnp.zeros_like(cnt_scr)

    rt = _route_math(lg_ref[...])
    rank0, rank1, counts = _rank_math(rt, cnt_scr[...])
    cnt_scr[...] = counts
    lane = lax.broadcasted_iota(jnp.int32, rt.shape, 1)
    rt_ref[...] = jnp.where(lane == RT_RANK, rank0, jnp.where(lane == RT_RANK + 1, rank1, rt))
    cnt_ref[...] = jnp.broadcast_to(counts, cnt_ref.shape)


def _route(logits):
    tm = ROUTE_TM
    return pl.pallas_call(
        _route_body,
        grid=(N_TOK // tm,),
        in_specs=[pl.BlockSpec((tm, LANES), lambda i: (i, 0))],
        out_specs=[pl.BlockSpec((tm, LANES), lambda i: (i, 0)),
                   pl.BlockSpec((SUBLANES, LANES), lambda i: (0, 0))],
        out_shape=[jax.ShapeDtypeStruct((N_TOK, LANES), F32),
                   jax.ShapeDtypeStruct((SUBLANES, LANES), F32)],
        scratch_shapes=[pltpu.VMEM((1, LANES), F32)],
        compiler_params=_cparams(("arbitrary",)),
        name="route",
    )(logits)


def _plan_body(rt_ref, cnt_ref, pos_ref, tt_ref):
    t = PLAN_TM
    lane = lax.broadcasted_iota(jnp.int32, (t, LANES), 1)
    rt = rt_ref[...]
    oh0 = lane.astype(F32) == rt[:, 0:1]
    oh1 = lane.astype(F32) == rt[:, 1:2]
    counts = cnt_ref[0:1, :]
    tiles = jnp.floor((counts + (MOE_TM - 1)) * (1.0 / MOE_TM))
    r = lax.broadcasted_iota(jnp.int32, (LANES, LANES), 0)
    c = lax.broadcasted_iota(jnp.int32, (LANES, LANES), 1)
    upto = jnp.where(r <= c, 1.0, 0.0).astype(BF16)
    tiles8 = jnp.broadcast_to(tiles, (SUBLANES, LANES)).astype(BF16)
    tile_end = jnp.dot(tiles8, upto, preferred_element_type=F32)[0:1]
    offs = (tile_end - tiles) * MOE_TM
    p0 = jnp.sum(jnp.where(oh0, offs, 0.0), axis=-1, keepdims=True) + rt[:, RT_RANK:RT_RANK + 1]
    p1 = jnp.sum(jnp.where(oh1, offs, 0.0), axis=-1, keepdims=True) + rt[:, RT_RANK + 1:RT_RANK + 2]
    pos_ref[...] = (jnp.where(lane == 0, p0, jnp.where(lane == 1, p1, 0.0)) * SUBLANES).astype(jnp.int32)
    tt_ref[...] = _tile_table(tiles, tile_end)


def _tile_table(tiles, tile_end):
    lane_e = lax.broadcasted_iota(jnp.int32, (LANES, LANES), 1)
    lane_f = lane_e.astype(F32)
    tile_id = lax.broadcasted_iota(jnp.int32, (LANES, LANES), 0).astype(F32)
    is_e = lane_e < N_EXPERTS
    owns = is_e & (tiles > 0)

    def count(cond):
        return jnp.sum(jnp.where(cond, 1.0, 0.0), axis=-1, keepdims=True)

    def first_owner_after(e):
        return jnp.min(jnp.where(owns & (lane_f > e), lane_f, jnp.float32(LANES)), axis=-1, keepdims=True)

    total = jnp.max(tile_end, axis=-1, keepdims=True)
    t_valid = jnp.where(tile_id[:, 0:1] < total, 1.0, 0.0)
    t_exp = jnp.where(t_valid > 0, count(is_e & (tile_end <= tile_id)), count(is_e & (tile_end <= total - 1.0)))
    t_first = count(owns & ((tile_end - tiles) == tile_id))
    t_clear = jnp.maximum(count(owns & ((tile_end - 1.0) == tile_id)), 1.0 - t_valid)
    t_next = first_owner_after(t_exp)
    t_next2 = first_owner_after(t_next)
    t_next3 = first_owner_after(t_next2)
    cols = {T_EXP: t_exp, T_VALID: t_valid, T_FIRST: t_first, T_CLEAR: t_clear, T_NEXT: t_next,
            T_NEXT2: t_next2, T_NEXT3: t_next3, T_ORD: count(owns & (lane_f < t_exp)), T_LAST: total - 1.0}
    table = jnp.zeros((LANES, LANES), F32)
    for k, col in cols.items():
        table = jnp.where(lane_e == k, col, table)
    return table.astype(jnp.int32)


def _plan(route, counts):
    t = PLAN_TM
    return pl.pallas_call(
        _plan_body,
        grid=(N_TOK // t,),
        in_specs=[pl.BlockSpec((t, LANES), lambda i: (i, 0)),
                  pl.BlockSpec((SUBLANES, LANES), lambda i: (0, 0))],
        out_specs=[pl.BlockSpec((t, LANES), lambda i: (i, 0)),
                   pl.BlockSpec((LANES, LANES), lambda i: (0, 0))],
        out_shape=[jax.ShapeDtypeStruct((N_TOK, LANES), jnp.int32),
                   jax.ShapeDtypeStruct((LANES, LANES), jnp.int32)],
        compiler_params=_cparams(("arbitrary",)),
        name="plan",
    )(route, counts)


def _rows_wait(ref, n_rows, sem):
    pltpu.make_async_copy(ref.at[pl.ds(0, n_rows)], ref.at[pl.ds(0, n_rows)], sem).wait()


def _dispatch_body(pos_ref, tt_ref, h_ref, xs_hbm, zbuf, sem_z, sem_s):
    i = pl.program_id(0)
    tile_rows = MOE_TM * SUBLANES

    @pl.when(i == 0)
    def _():
        zbuf[...] = _pack_halves(jnp.zeros((tile_rows, 2 * LANES), F32))

        def zero_copy(tile):
            start = pl.multiple_of(tile * tile_rows, tile_rows)
            return pltpu.make_async_copy(zbuf, xs_hbm.at[pl.ds(start, tile_rows)], sem_z)

        def clear_start(tile, carry):
            @pl.when(tt_ref[tile, T_CLEAR] > 0)
            def _():
                zero_copy(tile).start()
            return carry

        def clear_wait(tile, carry):
            @pl.when(tt_ref[tile, T_CLEAR] > 0)
            def _():
                zero_copy(tile).wait()
            return carry

        lax.fori_loop(0, MOE_TILES, clear_start, 0)
        lax.fori_loop(0, MOE_TILES, clear_wait, 0)

    def tok(j, carry):
        src = h_ref.at[pl.ds(pl.multiple_of(j * SUBLANES, SUBLANES), SUBLANES)]
        pair = TOP_K * (i * DISPATCH_TM + j)
        for k in range(TOP_K):
            dst_row = pl.multiple_of(pos_ref[pair + k], SUBLANES)
            pltpu.make_async_copy(src, xs_hbm.at[pl.ds(dst_row, SUBLANES)], sem_s).start(priority=k)
        return carry

    lax.fori_loop(0, DISPATCH_TM, tok, 0, unroll=8)
    _rows_wait(xs_hbm, TOP_K * DISPATCH_TM * SUBLANES, sem_s)


def _dispatch(pos_rows, tile_clear, h2p):
    grid_spec = pltpu.PrefetchScalarGridSpec(
        num_scalar_prefetch=2,
        grid=(N_TOK // DISPATCH_TM,),
        in_specs=[pl.BlockSpec((DISPATCH_TM * SUBLANES, LANES), lambda i, *_: (i, 0))],
        out_specs=pl.BlockSpec(memory_space=pl.ANY),
        scratch_shapes=[pltpu.VMEM((MOE_TM * SUBLANES, LANES), U32),
                        pltpu.SemaphoreType.DMA(()), pltpu.SemaphoreType.DMA(())],
    )
    return pl.pallas_call(
        _dispatch_body,
        grid_spec=grid_spec,
        out_shape=jax.ShapeDtypeStruct((MOE_ROWS * SUBLANES, LANES), U32),
        compiler_params=_cparams(("arbitrary",)),
        name="dispatch",
    )(pos_rows, tile_clear, h2p)


def _moe_body(tt_ref, x_ref, wg_hbm, wu_hbm, wd_hbm, y_ref,
              wg_s, wu_s, wd_s, stg_g, stg_u, stg_d, sem):
    i = pl.program_id(0)
    ahead = (T_EXP, T_NEXT, T_NEXT2, T_NEXT3)

    def fetch(e, slot):
        return (pltpu.make_async_copy(wg_hbm.at[e], stg_g.at[slot], sem.at[slot, 0]),
                pltpu.make_async_copy(wu_hbm.at[e], stg_u.at[slot], sem.at[slot, 1]),
                pltpu.make_async_copy(wd_hbm.at[e], stg_d.at[slot], sem.at[slot, 2]))

    @pl.when(i == 0)
    def _():
        for d in range(MOE_SLOTS):
            @pl.when(tt_ref[0, ahead[d]] < N_EXPERTS)
            def _():
                for cp in fetch(tt_ref[0, ahead[d]], d):
                    cp.start(priority=WEIGHT_DMA_PRIORITY)

    @pl.when(tt_ref[i, T_FIRST] > 0)
    def _():
        slot = tt_ref[i, T_ORD] % MOE_SLOTS
        for cp in fetch(tt_ref[i, T_EXP], slot):
            cp.wait()
        wg_s[...] = stg_g[slot].astype(BF16)
        wu_s[...] = stg_u[slot].astype(BF16)
        wd_s[...] = stg_d[slot].astype(BF16)

        @pl.when(tt_ref[i, ahead[MOE_SLOTS]] < N_EXPERTS)
        def _():
            for cp in fetch(tt_ref[i, ahead[MOE_SLOTS]], slot):
                cp.start(priority=WEIGHT_DMA_PRIORITY)

    @pl.when(tt_ref[i, T_VALID] > 0)
    def _():
        lo, hi = _unpack_halves(_load_token_tiles(x_ref, 0, MOE_TM))
        xa = lo.astype(BF16)
        xb = hi.astype(BF16)
        g = (jnp.dot(xa, wg_s[:HALF_D, :], preferred_element_type=F32)
             + jnp.dot(xb, wg_s[HALF_D:, :], preferred_element_type=F32))
        u = (jnp.dot(xa, wu_s[:HALF_D, :], preferred_element_type=F32)
             + jnp.dot(xb, wu_s[HALF_D:, :], preferred_element_type=F32))
        hid = (g * jax.nn.sigmoid(g) * u).astype(BF16)
        _store_token_tiles(y_ref, _pack_halves(jnp.dot(hid, wd_s[...], preferred_element_type=F32)))

    @pl.when(tt_ref[i, T_VALID] == 0)
    def _():
        y_ref[...] = _pack_halves(jnp.zeros((MOE_TM * SUBLANES, 2 * LANES), F32))


def _moe(tile_tab, xs, wg, wu, wd):
    tm = MOE_TM
    grid_spec = pltpu.PrefetchScalarGridSpec(
        num_scalar_prefetch=1,
        grid=(MOE_TILES,),
        in_specs=[pl.BlockSpec((tm * SUBLANES, LANES), lambda i, tt: (jnp.minimum(i, tt[0, T_LAST]), 0)),
                  pl.BlockSpec(memory_space=pl.ANY), pl.BlockSpec(memory_space=pl.ANY),
                  pl.BlockSpec(memory_space=pl.ANY)],
        out_specs=pl.BlockSpec((tm * SUBLANES, LANES), lambda i, *_: (i, 0)),
        scratch_shapes=[pltpu.VMEM((D_MODEL, EXPERT_FF), BF16), pltpu.VMEM((D_MODEL, EXPERT_FF), BF16),
                        pltpu.VMEM((EXPERT_FF, D_MODEL), BF16),
                        pltpu.VMEM((MOE_SLOTS, D_MODEL, EXPERT_FF), F32),
                        pltpu.VMEM((MOE_SLOTS, D_MODEL, EXPERT_FF), F32),
                        pltpu.VMEM((MOE_SLOTS, EXPERT_FF, D_MODEL), F32),
                        pltpu.SemaphoreType.DMA((MOE_SLOTS, 3))],
    )
    return pl.pallas_call(
        _moe_body,
        grid_spec=grid_spec,
        out_shape=jax.ShapeDtypeStruct((MOE_ROWS * SUBLANES, LANES), U32),
        compiler_params=_cparams(("arbitrary",)),
        name="moe",
    )(tile_tab, xs, wg, wu, wd)


def _final_body(pos_ref, x1_ref, mod_ref, rt_ref, g2_ref, b2_ref, ys_hbm, o_ref, buf, sem):
    i = pl.program_id(0)
    n = pl.num_programs(0)
    tm = FINAL_TM
    slot = i % FINAL_SLOTS
    nxt_slot = (i + 2) % FINAL_SLOTS
    nxt_tile = jnp.minimum(i + 2, n - 1)

    def issue(tile, dst_slot, j):
        pair = TOP_K * (tile * tm + j)
        for k in range(TOP_K):
            src_row = pl.multiple_of(pos_ref[pair + k], SUBLANES)
            dst_row = pl.multiple_of((k * tm + j) * SUBLANES, SUBLANES)
            pltpu.make_async_copy(ys_hbm.at[pl.ds(src_row, SUBLANES)],
                                  buf.at[dst_slot, pl.ds(dst_row, SUBLANES)], sem.at[dst_slot]).start(priority=k)

    def wait(s):
        pltpu.make_async_copy(ys_hbm.at[pl.ds(0, TOP_K * tm * SUBLANES)], buf.at[s], sem.at[s]).wait()

    @pl.when(i == 0)
    def _():
        def tok(j, carry):
            issue(0, 0, j)
            issue(jnp.minimum(1, n - 1), 1, j)
            return carry
        lax.fori_loop(0, tm, tok, 0, unroll=8)

    wait(slot)
    gate2 = mod_ref[0, 5:6, :]
    cur = buf.at[slot]

    def chunk(c, carry):
        r0 = pl.multiple_of(c * FINAL_CHUNK, FINAL_CHUNK)
        rows = pl.ds(r0, FINAL_CHUNK)
        a_lo, a_hi = _unpack_halves(_load_token_tiles(cur, r0, FINAL_CHUNK))
        b_lo, b_hi = _unpack_halves(_load_token_tiles(cur, tm + r0, FINAL_CHUNK))
        x1 = x1_ref[rows, :]
        rt = rt_ref[rows, :]
        for r in range(FINAL_CHUNK):
            issue(nxt_tile, nxt_slot, r0 + r)
        w0 = rt[:, 2:3]
        w1 = rt[:, 3:4]
        y = jnp.concatenate([w0 * a_lo + w1 * b_lo, w0 * a_hi + w1 * b_hi], axis=1)
        o_ref[rows, :] = _ln_rows(DEEPNORM_ALPHA * x1 + gate2 * y) * g2_ref[...] + b2_ref[...]
        return carry

    lax.fori_loop(0, tm // FINAL_CHUNK, chunk, 0)

    @pl.when(i == n - 1)
    def _():
        wait((i + 1) % FINAL_SLOTS)
        wait(nxt_slot)


def _final(pos_rows, x1, mod3, route, g2, b2, ys):
    tm = FINAL_TM
    tpb = SEQ // tm
    row = lambda w: pl.BlockSpec((tm, w), lambda i, *_: (i, 0))
    grid_spec = pltpu.PrefetchScalarGridSpec(
        num_scalar_prefetch=1,
        grid=(N_TOK // tm,),
        in_specs=[row(D_MODEL),
                  pl.BlockSpec((1, N_MOD, D_MODEL), lambda i, *_: (i // tpb, 0, 0)),
                  row(LANES),
                  pl.BlockSpec(g2.shape, lambda i, *_: (0, 0)),
                  pl.BlockSpec(b2.shape, lambda i, *_: (0, 0)),
                  pl.BlockSpec(memory_space=pl.ANY)],
        out_specs=row(D_MODEL),
        scratch_shapes=[pltpu.VMEM((FINAL_SLOTS, TOP_K * tm * SUBLANES, LANES), U32),
                        pltpu.SemaphoreType.DMA((FINAL_SLOTS,))],
    )
    return pl.pallas_call(
        _final_body,
        grid_spec=grid_spec,
        out_shape=jax.ShapeDtypeStruct((N_TOK, D_MODEL), F32),
        compiler_params=_cparams(("arbitrary",)),
        name="final",
    )(pos_rows, x1, mod3, route, g2, b2, ys)


def kernel(x, c, positions, w_ada, b_ada, w_in, q_norm_g, w_uq, kv_norm_g, w_ukv, sgu_norm_g, sgu_norm_b,
           w_spatial, b_spatial, w_o, ln1_g, ln1_b, w_router_group, b_router_group, w_router_expert,
           b_router_expert, w_gate, w_up, w_down, ln2_g, ln2_b):
    l = 0
    x2 = x.reshape(N_TOK, D_MODEL)
    mod3 = _ada(c, w_ada[l], b_ada[l][None, :]).reshape(BATCH, N_MOD, D_MODEL)

    wt, wuq, wukv, wo = _prep(w_in[l].T, w_uq[l], w_ukv[l], w_o[l])
    n_r = N_GROUPS + N_EXPERTS
    wr = jnp.pad(jnp.concatenate([w_router_group[l], w_router_expert[l]], axis=1),
                 ((0, 0), (0, LANES - n_r))).astype(BF16)
    br = jnp.pad(jnp.concatenate([b_router_group[l], b_router_expert[l]]), (0, LANES - n_r))[None, :]
    inv_freq = 1.0 / (ROPE_THETA ** (jnp.arange(0, QK_ROPE_DIM, 2, dtype=F32) / QK_ROPE_DIM))
    invf = jnp.tile(inv_freq, 2 * LANES // QK_ROPE_DIM)[None, :]

    cqn, ckvn, kpe, u, vs = _inproj(x2, mod3, wt, q_norm_g[l][None, :], kv_norm_g[l][None, :],
                                    sgu_norm_g[l][None, :], sgu_norm_b[l][None, :])
    q, k, v = _qkv(cqn, ckvn, kpe, positions.reshape(N_TOK, 1), invf, wuq, wukv)
    attn = _attn(q, k, v).reshape(N_TOK, MLA_WIDTH)
    x1, h2, logits = _mixout(x2, mod3, attn, u, vs, w_spatial[l], b_spatial[l].T, wo,
                             ln1_g[l][None, :], ln1_b[l][None, :], wr, br)
    route, counts = _route(logits)
    pos_tab, tile_tab = _plan(route, counts)
    pos_rows = pos_tab[:, 0:TOP_K].reshape(-1)
    xs = _dispatch(pos_rows, tile_tab, h2)
    ys = _moe(tile_tab, xs, w_gate[l], w_up[l], w_down[l])
    out = _final(pos_rows, x1, mod3, route, ln2_g[l][None, :], ln2_b[l][None, :], ys)
    return out.reshape(BATCH, SEQ, D_MODEL)
```

```python
import jax
import jax.numpy as jnp
import numpy as np
from jax import lax
from jax.experimental import pallas as pl
from jax.experimental.pallas import tpu as pltpu

D_MODEL = 2048
BATCH = 4
SEQ = 2048
N_TOK = BATCH * SEQ

MLA_HEADS = 8
QK_NOPE_DIM = 128
QK_ROPE_DIM = 64
QK_DIM = QK_NOPE_DIM + QK_ROPE_DIM
V_HEAD_DIM = 128
Q_LORA_RANK = 768
KV_LORA_RANK = 512
ROPE_THETA = 10000.0
MLA_WIDTH = MLA_HEADS * V_HEAD_DIM

SGU_GROUPS = 8
SGU_GROUP_DIM = 128
SGU_CHUNK = 128
SGU_WIDTH = SGU_GROUPS * SGU_GROUP_DIM

N_GROUPS = 4
EXPERTS_PER_GROUP = 8
N_EXPERTS = N_GROUPS * EXPERTS_PER_GROUP
TOP_K = 2
EXPERT_FF = 512

DEEPNORM_ALPHA = 2.0 ** 0.25
EPS = 1e-6
N_MOD = 6
NEG_BIG = -1e30

LANES = 128
SUBLANES = 8
V7X_VMEM_BYTES = 64 * 1024 * 1024
VMEM_LIMIT = V7X_VMEM_BYTES * 7 // 8

ADA_TN = 1024
TOK_TM = 512
MIX_TM = 256
PREP_STEPS = 4
ATT_TQ = 1024
ATT_TK = 1024
ATT_HEADS = 4
MOE_TM = 256
MOE_TILES = (N_TOK * TOP_K + N_EXPERTS * (MOE_TM - 1)) // MOE_TM + 1
MOE_ROWS = MOE_TILES * MOE_TM
PLAN_TM = 2048
ROUTE_TM = 1024
RT_RANK = 4
DISPATCH_TM = 1024
FINAL_TM = 256
FINAL_CHUNK = 128
FINAL_SLOTS = 3
assert MOE_TILES <= LANES
T_EXP, T_VALID, T_FIRST, T_CLEAR, T_NEXT, T_NEXT2, T_NEXT3, T_ORD, T_LAST = range(9)
MOE_SLOTS = 2
WEIGHT_DMA_PRIORITY = 1

F32 = jnp.float32
BF16 = jnp.bfloat16
U32 = jnp.uint32
HALF_D = D_MODEL // 2


def _cparams(sem):
    return pltpu.CompilerParams(dimension_semantics=sem, vmem_limit_bytes=VMEM_LIMIT)


def _const_spec(shape):
    nd = len(shape)
    return pl.BlockSpec(shape, lambda *_: (0,) * nd, pipeline_mode=pl.Buffered(1))


def _ln_rows(x):
    mu = jnp.mean(x, axis=-1, keepdims=True)
    xc = x - mu
    var = jnp.mean(xc * xc, axis=-1, keepdims=True)
    return xc * lax.rsqrt(var + EPS)


def _rms_rows(x):
    return x * lax.rsqrt(jnp.mean(x * x, axis=-1, keepdims=True) + EPS)


def _pack_halves(x):
    half = x.shape[-1] // 2
    return pltpu.pack_elementwise([x[:, :half], x[:, half:]], packed_dtype=BF16)


def _unpack_halves(w):
    lo = pltpu.unpack_elementwise(w, index=0, packed_dtype=BF16, unpacked_dtype=F32)
    hi = pltpu.unpack_elementwise(w, index=1, packed_dtype=BF16, unpacked_dtype=F32)
    return lo, hi


def _store_token_tiles(ref, w):
    rows = w.shape[0]
    for s in range(SUBLANES):
        ref[pl.ds(s, rows, stride=SUBLANES), :] = w[:, s * LANES:(s + 1) * LANES]


def _load_token_tiles(ref, start_row, rows):
    return jnp.concatenate([ref[pl.ds(start_row * SUBLANES + s, rows, stride=SUBLANES), :]
                            for s in range(SUBLANES)], axis=1)


def _gelu_tanh(x):
    c = np.sqrt(2.0 / np.pi).astype(np.float32)
    return 0.5 * x * (1.0 + jnp.tanh(c * (x + 0.044715 * (x * x * x))))


def _ada_body(c_ref, w_ref, b_ref, o_ref):
    o_ref[...] = jnp.dot(c_ref[...].astype(BF16), w_ref[...].astype(BF16),
                         preferred_element_type=F32) + b_ref[...]


def _ada(c, w, b):
    n = w.shape[1]
    return pl.pallas_call(
        _ada_body,
        grid=(n // ADA_TN,),
        in_specs=[pl.BlockSpec((BATCH, D_MODEL), lambda j: (0, 0)),
                  pl.BlockSpec((D_MODEL, ADA_TN), lambda j: (0, j)),
                  pl.BlockSpec((1, ADA_TN), lambda j: (0, j))],
        out_specs=pl.BlockSpec((BATCH, ADA_TN), lambda j: (0, j)),
        out_shape=jax.ShapeDtypeStruct((BATCH, n), F32),
        compiler_params=_cparams(("parallel",)),
        name="ada",
    )(c, w, b)


def _prep_body(win_ref, wuq_ref, wukv_ref, wo_ref, win_o, wuq_o, wukv_o, wo_o):
    win_o[...] = win_ref[...].astype(BF16)
    u = wuq_ref[...]
    nope = [u[:, h * QK_DIM:h * QK_DIM + QK_NOPE_DIM] for h in range(MLA_HEADS)]
    rope = [u[:, h * QK_DIM + QK_NOPE_DIM:(h + 1) * QK_DIM] for h in range(MLA_HEADS)]
    wuq_o[...] = jnp.concatenate(nope + rope, axis=1).astype(BF16)
    kv = wukv_ref[...]
    hw = QK_NOPE_DIM + V_HEAD_DIM
    kn = [kv[:, h * hw:h * hw + QK_NOPE_DIM] for h in range(MLA_HEADS)]
    vv = [kv[:, h * hw + QK_NOPE_DIM:(h + 1) * hw] for h in range(MLA_HEADS)]
    wukv_o[...] = jnp.concatenate(kn + vv, axis=1).astype(BF16)
    wo_o[...] = wo_ref[...].astype(BF16)


def _prep(w_in_t, w_uq, w_ukv, w_o):
    steps = PREP_STEPS
    blk = lambda a: pl.BlockSpec((a.shape[0] // steps, a.shape[1]), lambda i: (i, 0))
    ins = (w_in_t, w_uq, w_ukv, w_o)
    return pl.pallas_call(
        _prep_body,
        grid=(steps,),
        in_specs=[blk(a) for a in ins],
        out_specs=[blk(a) for a in ins],
        out_shape=[jax.ShapeDtypeStruct(a.shape, BF16) for a in ins],
        compiler_params=_cparams(("parallel",)),
        name="prep",
    )(*ins)


def _inproj_body(x_ref, mod_ref, wt_ref, gq_ref, gkv_ref, sg_ref, sb_ref,
                 cq_ref, ckv_ref, kpe_ref, u_ref, vs_ref):
    o1, o2, o3 = Q_LORA_RANK, Q_LORA_RANK + KV_LORA_RANK, Q_LORA_RANK + KV_LORA_RANK + QK_ROPE_DIM

    def proj(lo, hi):
        return lax.dot_general(h, wt_ref[lo:hi, :], (((1,), (1,)), ((), ())), preferred_element_type=F32)

    sh = mod_ref[0, 0:1, :]
    sc = mod_ref[0, 1:2, :]
    h = (_ln_rows(x_ref[...]) * (1.0 + sc) + sh).astype(BF16)
    cq_ref[...] = (_rms_rows(proj(0, o1)) * gq_ref[...]).astype(BF16)
    ckv_ref[...] = (_rms_rows(proj(o1, o2)) * gkv_ref[...]).astype(BF16)
    kpe_ref[...] = proj(o2, o2 + LANES)
    gz = _gelu_tanh(proj(o3, o3 + 2 * SGU_WIDTH))
    u_ref[...] = gz[:, :SGU_WIDTH]
    vs_ref[...] = (_ln_rows(gz[:, SGU_WIDTH:]) * sg_ref[...] + sb_ref[...]).astype(BF16)


def _inproj(x2, mod3, wt, gq, gkv, sg, sb):
    tm = TOK_TM
    tiles_per_batch = SEQ // tm
    row = lambda w: pl.BlockSpec((tm, w), lambda i: (i, 0))
    return pl.pallas_call(
        _inproj_body,
        grid=(N_TOK // tm,),
        in_specs=[row(D_MODEL),
                  pl.BlockSpec((1, N_MOD, D_MODEL), lambda i: (i // tiles_per_batch, 0, 0)),
                  _const_spec(wt.shape),
                  _const_spec(gq.shape), _const_spec(gkv.shape), _const_spec(sg.shape), _const_spec(sb.shape)],
        out_specs=[row(Q_LORA_RANK), row(KV_LORA_RANK), row(LANES), row(SGU_WIDTH), row(SGU_WIDTH)],
        out_shape=[jax.ShapeDtypeStruct((N_TOK, Q_LORA_RANK), BF16),
                   jax.ShapeDtypeStruct((N_TOK, KV_LORA_RANK), BF16),
                   jax.ShapeDtypeStruct((N_TOK, LANES), F32),
                   jax.ShapeDtypeStruct((N_TOK, SGU_WIDTH), F32),
                   jax.ShapeDtypeStruct((N_TOK, SGU_WIDTH), BF16)],
        compiler_params=_cparams(("parallel",)),
        name="inproj",
    )(x2, mod3, wt, gq, gkv, sg, sb)


def _rope(x, cos, sin):
    w = x.shape[-1]
    lane = lax.broadcasted_iota(jnp.int32, x.shape, 1)
    first_half = (lane % QK_ROPE_DIM) < (QK_ROPE_DIM // 2)
    rot = jnp.where(first_half,
                    -pltpu.roll(x, w - QK_ROPE_DIM // 2, 1),
                    pltpu.roll(x, QK_ROPE_DIM // 2, 1))
    return x * cos + rot * sin


def _qkv_body(cq_ref, ckv_ref, kpe_ref, pos_ref, invf_ref, wuq_ref, wukv_ref, q_ref, k_ref, v_ref):
    ang = pos_ref[...].astype(F32) * invf_ref[...]
    cos1 = jnp.cos(ang)
    sin1 = jnp.sin(ang)
    reps = MLA_HEADS * QK_ROPE_DIM // LANES
    cos = jnp.concatenate([cos1] * reps, axis=1)
    sin = jnp.concatenate([sin1] * reps, axis=1)
    scale = np.float32(QK_DIM ** -0.5)
    q = jnp.dot(cq_ref[...], wuq_ref[...], preferred_element_type=F32) * scale
    q_pe = _rope(q[:, MLA_HEADS * QK_NOPE_DIM:], cos, sin)
    kv = jnp.dot(ckv_ref[...], wukv_ref[...], preferred_element_type=F32)
    k_pe = _rope(kpe_ref[...], cos1, sin1)[:, :QK_ROPE_DIM].astype(BF16)
    for h in range(MLA_HEADS):
        q_ref[0, h, :, 0:QK_NOPE_DIM] = q[:, h * QK_NOPE_DIM:(h + 1) * QK_NOPE_DIM].astype(BF16)
        q_ref[0, h, :, QK_NOPE_DIM:QK_DIM] = q_pe[:, h * QK_ROPE_DIM:(h + 1) * QK_ROPE_DIM].astype(BF16)
        k_ref[0, h, :, 0:QK_NOPE_DIM] = kv[:, h * QK_NOPE_DIM:(h + 1) * QK_NOPE_DIM].astype(BF16)
        k_ref[0, h, :, QK_NOPE_DIM:QK_DIM] = k_pe
        v_ref[0, h, :, :] = kv[:, MLA_WIDTH + h * V_HEAD_DIM:MLA_WIDTH + (h + 1) * V_HEAD_DIM].T.astype(BF16)


def _qkv(cqn, ckvn, kpe, pos2, invf, wuq, wukv):
    tm = TOK_TM
    tpb = SEQ // tm
    row = lambda w: pl.BlockSpec((tm, w), lambda i: (i, 0))
    head_out = lambda w: pl.BlockSpec((1, MLA_HEADS, tm, w), lambda i: (i // tpb, 0, i % tpb, 0))
    return pl.pallas_call(
        _qkv_body,
        grid=(N_TOK // tm,),
        in_specs=[row(Q_LORA_RANK), row(KV_LORA_RANK), row(LANES), row(1),
                  _const_spec(invf.shape), _const_spec(wuq.shape), _const_spec(wukv.shape)],
        out_specs=[head_out(QK_DIM), head_out(QK_DIM),
                   pl.BlockSpec((1, MLA_HEADS, V_HEAD_DIM, tm), lambda i: (i // tpb, 0, 0, i % tpb))],
        out_shape=[jax.ShapeDtypeStruct((BATCH, MLA_HEADS, SEQ, QK_DIM), BF16),
                   jax.ShapeDtypeStruct((BATCH, MLA_HEADS, SEQ, QK_DIM), BF16),
                   jax.ShapeDtypeStruct((BATCH, MLA_HEADS, V_HEAD_DIM, SEQ), BF16)],
        compiler_params=_cparams(("parallel",)),
        name="qkv",
    )(cqn, ckvn, kpe, pos2, invf, wuq, wukv)


def _attn_body(q_ref, k_ref, v_ref, o_ref):
    i = pl.program_id(2)

    def step(h, j, carry, masked):
        m, l, acc = carry
        start = pl.multiple_of(j * ATT_TK, ATT_TK)
        k = k_ref[0, h, pl.ds(start, ATT_TK), :]
        vt = v_ref[0, h, :, pl.ds(start, ATT_TK)]
        st = lax.dot_general(k, q_ref[0, h], (((1,), (1,)), ((), ())), preferred_element_type=F32)
        if masked:
            kpos = lax.broadcasted_iota(jnp.int32, st.shape, 0)
            qpos = lax.broadcasted_iota(jnp.int32, st.shape, 1)
            st = jnp.where(kpos <= qpos, st, NEG_BIG)
        m_new = jnp.maximum(m, jnp.max(st, axis=0, keepdims=True))
        p = jnp.exp(st - m_new)
        a = jnp.exp(m - m_new)
        l = a * l + jnp.sum(p, axis=0, keepdims=True)
        pv = jnp.dot(vt, p.astype(BF16), preferred_element_type=F32)
        return m_new, l, a * acc + pv

    def steps(j, carries, masked):
        return tuple(step(h, j, carries[h], masked) for h in range(ATT_HEADS))

    init = tuple((jnp.full((1, ATT_TQ), NEG_BIG, F32), jnp.zeros((1, ATT_TQ), F32),
                  jnp.zeros((V_HEAD_DIM, ATT_TQ), F32)) for _ in range(ATT_HEADS))
    carries = lax.fori_loop(0, i, lambda j, c: steps(j, c, False), init)
    carries = steps(i, carries, True)
    for h, (m, l, acc) in enumerate(carries):
        o_ref[0, :, h * V_HEAD_DIM:(h + 1) * V_HEAD_DIM] = (acc / l).T.astype(BF16)


def _attn(q, k, v):
    assert ATT_TQ == ATT_TK
    hb = ATT_HEADS
    return pl.pallas_call(
        _attn_body,
        grid=(BATCH, MLA_HEADS // hb, SEQ // ATT_TQ),
        in_specs=[pl.BlockSpec((1, hb, ATT_TQ, QK_DIM), lambda b, h, i: (b, h, i, 0)),
                  pl.BlockSpec((1, hb, SEQ, QK_DIM), lambda b, h, i: (b, h, 0, 0)),
                  pl.BlockSpec((1, hb, V_HEAD_DIM, SEQ), lambda b, h, i: (b, h, 0, 0))],
        out_specs=pl.BlockSpec((1, ATT_TQ, hb * V_HEAD_DIM), lambda b, h, i: (b, i, h)),
        out_shape=jax.ShapeDtypeStruct((BATCH, SEQ, MLA_WIDTH), BF16),
        compiler_params=_cparams(("parallel", "parallel", "arbitrary")),
        name="attn",
    )(q, k, v)


def _mixout_body(x_ref, mod_ref, attn_ref, u_ref, vs_ref, wsp_ref, bsp_ref, woa_ref, wos_ref,
                 g1_ref, b1_ref, wr_ref, br_ref, x1_ref, h2_ref, lg_ref, sgu_scr):
    r = lax.broadcasted_iota(jnp.int32, (SGU_CHUNK, SGU_CHUNK), 0)
    c = lax.broadcasted_iota(jnp.int32, (SGU_CHUNK, SGU_CHUNK), 1)
    causal = c <= r
    for g in range(SGU_GROUPS):
        ws = jnp.where(causal, wsp_ref[g], 0.0).astype(BF16)
        bias = bsp_ref[:, g:g + 1]
        cols = slice(g * SGU_GROUP_DIM, (g + 1) * SGU_GROUP_DIM)
        for ch in range(MIX_TM // SGU_CHUNK):
            rows = slice(ch * SGU_CHUNK, (ch + 1) * SGU_CHUNK)
            mixed = jnp.dot(ws, vs_ref[rows, cols], preferred_element_type=F32) + bias
            sgu_scr[rows, cols] = (u_ref[rows, cols] * mixed).astype(BF16)
    y = (jnp.dot(attn_ref[...], woa_ref[...], preferred_element_type=F32)
         + jnp.dot(sgu_scr[...], wos_ref[...], preferred_element_type=F32))
    gate1 = mod_ref[0, 2:3, :]
    sh2 = mod_ref[0, 3:4, :]
    sc2 = mod_ref[0, 4:5, :]
    x1 = _ln_rows(DEEPNORM_ALPHA * x_ref[...] + gate1 * y) * g1_ref[...] + b1_ref[...]
    x1_ref[...] = x1
    h2 = _ln_rows(x1) * (1.0 + sc2) + sh2
    _store_token_tiles(h2_ref, _pack_halves(h2))
    lg_ref[...] = jnp.dot(h2.astype(BF16), wr_ref[...], preferred_element_type=F32) + br_ref[...]


def _mixout(x2, mod3, attn, u, vs, wsp, bsp_t, wo, g1, b1, wr, br):
    tm = MIX_TM
    tpb = SEQ // tm
    row = lambda w: pl.BlockSpec((tm, w), lambda i: (i, 0))
    wo_half = lambda j: pl.BlockSpec((MLA_WIDTH, D_MODEL), lambda i: (j, 0), pipeline_mode=pl.Buffered(1))
    return pl.pallas_call(
        _mixout_body,
        grid=(N_TOK // tm,),
        in_specs=[row(D_MODEL),
                  pl.BlockSpec((1, N_MOD, D_MODEL), lambda i: (i // tpb, 0, 0)),
                  row(MLA_WIDTH), row(SGU_WIDTH), row(SGU_WIDTH),
                  _const_spec(wsp.shape), _const_spec(bsp_t.shape), wo_half(0), wo_half(1),
                  _const_spec(g1.shape), _const_spec(b1.shape), _const_spec(wr.shape), _const_spec(br.shape)],
        out_specs=[row(D_MODEL), pl.BlockSpec((tm * SUBLANES, LANES), lambda i: (i, 0)), row(LANES)],
        out_shape=[jax.ShapeDtypeStruct((N_TOK, D_MODEL), F32),
                   jax.ShapeDtypeStruct((N_TOK * SUBLANES, LANES), U32),
                   jax.ShapeDtypeStruct((N_TOK, LANES), F32)],
        scratch_shapes=[pltpu.VMEM((tm, SGU_WIDTH), BF16)],
        compiler_params=_cparams(("parallel",)),
        name="mix_out",
    )(x2, mod3, attn, u, vs, wsp, bsp_t, wo, wo, g1, b1, wr, br)


def _route_math(lg):
    lane = lax.broadcasted_iota(jnp.int32, lg.shape, 1)
    big = jnp.int32(LANES)

    def top1(vals):
        m = jnp.max(vals, axis=-1, keepdims=True)
        idx = jnp.min(jnp.where(vals == m, lane, big), axis=-1, keepdims=True)
        return m, idx

    is_group = lane < N_GROUPS
    glog = jnp.where(is_group, lg, -jnp.inf)
    gmax, gidx = top1(glog)
    pg_top = 1.0 / jnp.sum(jnp.exp(glog - gmax), axis=-1, keepdims=True)
    eid = lane - N_GROUPS
    sel = (eid >= gidx * EXPERTS_PER_GROUP) & (eid < (gidx + 1) * EXPERTS_PER_GROUP)
    elog = jnp.where(sel, lg, -jnp.inf)
    m1, i1 = top1(elog)
    m2, i2 = top1(jnp.where(lane == i1, -jnp.inf, elog))
    e2 = jnp.exp(m2 - m1)
    w1 = pg_top / (1.0 + e2)
    w2 = pg_top * e2 / (1.0 + e2)
    return jnp.where(lane == 0, (i1 - N_GROUPS).astype(F32),
                     jnp.where(lane == 1, (i2 - N_GROUPS).astype(F32),
                               jnp.where(lane == 2, w1, jnp.where(lane == 3, w2, 0.0))))


def _rank_math(rt, counts):
    t = rt.shape[0]
    lane = lax.broadcasted_iota(jnp.int32, (t, LANES), 1).astype(F32)
    oh0 = lane == rt[:, 0:1]
    oh1 = lane == rt[:, 1:2]
    s = jnp.where(oh0 | oh1, 1.0, 0.0)
    r = lax.broadcasted_iota(jnp.int32, (t, t), 0)
    c = lax.broadcasted_iota(jnp.int32, (t, t), 1)
    before = jnp.where(c < r, 1.0, 0.0).astype(BF16)
    csum = jnp.dot(before, s.astype(BF16), preferred_element_type=F32) + counts
    rank0 = jnp.sum(jnp.where(oh0, csum, 0.0), axis=-1, keepdims=True)
    rank1 = jnp.sum(jnp.where(oh1, csum, 0.0), axis=-1, keepdims=True)
    return rank0, rank1, counts + jnp.sum(s, axis=0, keepdims=True)


def _route_body(lg_ref, rt_ref, cnt_ref, cnt_scr):
    @pl.when(pl.program_id(0) == 0)
    def _():
        cnt_scr[...] = jnp.zeros_like(cnt_scr)

    rt = _route_math(lg_ref[...])
    rank0, rank1, counts = _rank_math(rt, cnt_scr[...])
    cnt_scr[...] = counts
    lane = lax.broadcasted_iota(jnp.int32, rt.shape, 1)
    rt_ref[...] = jnp.where(lane == RT_RANK, rank0, jnp.where(lane == RT_RANK + 1, rank1, rt))
    cnt_ref[...] = jnp.broadcast_to(counts, cnt_ref.shape)


def _route(logits):
    tm = ROUTE_TM
    return pl.pallas_call(
        _route_body,
        grid=(N_TOK // tm,),
        in_specs=[pl.BlockSpec((tm, LANES), lambda i: (i, 0))],
        out_specs=[pl.BlockSpec((tm, LANES), lambda i: (i, 0)),
                   pl.BlockSpec((SUBLANES, LANES), lambda i: (0, 0))],
        out_shape=[jax.ShapeDtypeStruct((N_TOK, LANES), F32),
                   jax.ShapeDtypeStruct((SUBLANES, LANES), F32)],
        scratch_shapes=[pltpu.VMEM((1, LANES), F32)],
        compiler_params=_cparams(("arbitrary",)),
        name="route",
    )(logits)


def _plan_body(rt_ref, cnt_ref, pos_ref, tt_ref):
    t = PLAN_TM
    lane = lax.broadcasted_iota(jnp.int32, (t, LANES), 1)
    rt = rt_ref[...]
    oh0 = lane.astype(F32) == rt[:, 0:1]
    oh1 = lane.astype(F32) == rt[:, 1:2]
    counts = cnt_ref[0:1, :]
    tiles = jnp.floor((counts + (MOE_TM - 1)) * (1.0 / MOE_TM))
    r = lax.broadcasted_iota(jnp.int32, (LANES, LANES), 0)
    c = lax.broadcasted_iota(jnp.int32, (LANES, LANES), 1)
    upto = jnp.where(r <= c, 1.0, 0.0).astype(BF16)
    tiles8 = jnp.broadcast_to(tiles, (SUBLANES, LANES)).astype(BF16)
    tile_end = jnp.dot(tiles8, upto, preferred_element_type=F32)[0:1]
    offs = (tile_end - tiles) * MOE_TM
    p0 = jnp.sum(jnp.where(oh0, offs, 0.0), axis=-1, keepdims=True) + rt[:, RT_RANK:RT_RANK + 1]
    p1 = jnp.sum(jnp.where(oh1, offs, 0.0), axis=-1, keepdims=True) + rt[:, RT_RANK + 1:RT_RANK + 2]
    pos_ref[...] = (jnp.where(lane == 0, p0, jnp.where(lane == 1, p1, 0.0)) * SUBLANES).astype(jnp.int32)
    tt_ref[...] = _tile_table(tiles, tile_end)


def _tile_table(tiles, tile_end):
    lane_e = lax.broadcasted_iota(jnp.int32, (LANES, LANES), 1)
    lane_f = lane_e.astype(F32)
    tile_id = lax.broadcasted_iota(jnp.int32, (LANES, LANES), 0).astype(F32)
    is_e = lane_e < N_EXPERTS
    owns = is_e & (tiles > 0)

    def count(cond):
        return jnp.sum(jnp.where(cond, 1.0, 0.0), axis=-1, keepdims=True)

    def first_owner_after(e):
        return jnp.min(jnp.where(owns & (lane_f > e), lane_f, jnp.float32(LANES)), axis=-1, keepdims=True)

    total = jnp.max(tile_end, axis=-1, keepdims=True)
    t_valid = jnp.where(tile_id[:, 0:1] < total, 1.0, 0.0)
    t_exp = jnp.where(t_valid > 0, count(is_e & (tile_end <= tile_id)), count(is_e & (tile_end <= total - 1.0)))
    t_first = count(owns & ((tile_end - tiles) == tile_id))
    t_clear = jnp.maximum(count(owns & ((tile_end - 1.0) == tile_id)), 1.0 - t_valid)
    t_next = first_owner_after(t_exp)
    t_next2 = first_owner_after(t_next)
    t_next3 = first_owner_after(t_next2)
    cols = {T_EXP: t_exp, T_VALID: t_valid, T_FIRST: t_first, T_CLEAR: t_clear, T_NEXT: t_next,
            T_NEXT2: t_next2, T_NEXT3: t_next3, T_ORD: count(owns & (lane_f < t_exp)), T_LAST: total - 1.0}
    table = jnp.zeros((LANES, LANES), F32)
    for k, col in cols.items():
        table = jnp.where(lane_e == k, col, table)
    return table.astype(jnp.int32)


def _plan(route, counts):
    t = PLAN_TM
    return pl.pallas_call(
        _plan_body,
        grid=(N_TOK // t,),
        in_specs=[pl.BlockSpec((t, LANES), lambda i: (i, 0)),
                  pl.BlockSpec((SUBLANES, LANES), lambda i: (0, 0))],
        out_specs=[pl.BlockSpec((t, LANES), lambda i: (i, 0)),
                   pl.BlockSpec((LANES, LANES), lambda i: (0, 0))],
        out_shape=[jax.ShapeDtypeStruct((N_TOK, LANES), jnp.int32),
                   jax.ShapeDtypeStruct((LANES, LANES), jnp.int32)],
        compiler_params=_cparams(("arbitrary",)),
        name="plan",
    )(route, counts)


def _rows_wait(ref, n_rows, sem):
    pltpu.make_async_copy(ref.at[pl.ds(0, n_rows)], ref.at[pl.ds(0, n_rows)], sem).wait()


def _dispatch_body(pos_ref, tt_ref, h_ref, xs_hbm, zbuf, sem_z, sem_s):
    i = pl.program_id(0)
    tile_rows = MOE_TM * SUBLANES

    @pl.when(i == 0)
    def _():
        zbuf[...] = _pack_halves(jnp.zeros((tile_rows, 2 * LANES), F32))

        def zero_copy(tile):
            start = pl.multiple_of(tile * tile_rows, tile_rows)
            return pltpu.make_async_copy(zbuf, xs_hbm.at[pl.ds(start, tile_rows)], sem_z)

        def clear_start(tile, carry):
            @pl.when(tt_ref[tile, T_CLEAR] > 0)
            def _():
                zero_copy(tile).start()
            return carry

        def clear_wait(tile, carry):
            @pl.when(tt_ref[tile, T_CLEAR] > 0)
            def _():
                zero_copy(tile).wait()
            return carry

        lax.fori_loop(0, MOE_TILES, clear_start, 0)
        lax.fori_loop(0, MOE_TILES, clear_wait, 0)

    def tok(j, carry):
        src = h_ref.at[pl.ds(pl.multiple_of(j * SUBLANES, SUBLANES), SUBLANES)]
        pair = TOP_K * (i * DISPATCH_TM + j)
        for k in range(TOP_K):
            dst_row = pl.multiple_of(pos_ref[pair + k], SUBLANES)
            pltpu.make_async_copy(src, xs_hbm.at[pl.ds(dst_row, SUBLANES)], sem_s).start(priority=k)
        return carry

    lax.fori_loop(0, DISPATCH_TM, tok, 0, unroll=8)
    _rows_wait(xs_hbm, TOP_K * DISPATCH_TM * SUBLANES, sem_s)


def _dispatch(pos_rows, tile_clear, h2p):
    grid_spec = pltpu.PrefetchScalarGridSpec(
        num_scalar_prefetch=2,
        grid=(N_TOK // DISPATCH_TM,),
        in_specs=[pl.BlockSpec((DISPATCH_TM * SUBLANES, LANES), lambda i, *_: (i, 0))],
        out_specs=pl.BlockSpec(memory_space=pl.ANY),
        scratch_shapes=[pltpu.VMEM((MOE_TM * SUBLANES, LANES), U32),
                        pltpu.SemaphoreType.DMA(()), pltpu.SemaphoreType.DMA(())],
    )
    return pl.pallas_call(
        _dispatch_body,
        grid_spec=grid_spec,
        out_shape=jax.ShapeDtypeStruct((MOE_ROWS * SUBLANES, LANES), U32),
        compiler_params=_cparams(("arbitrary",)),
        name="dispatch",
    )(pos_rows, tile_clear, h2p)


def _moe_body(tt_ref, x_ref, wg_hbm, wu_hbm, wd_hbm, y_ref,
              wg_s, wu_s, wd_s, stg_g, stg_u, stg_d, sem):
    i = pl.program_id(0)
    ahead = (T_EXP, T_NEXT, T_NEXT2, T_NEXT3)

    def fetch(e, slot):
        return (pltpu.make_async_copy(wg_hbm.at[e], stg_g.at[slot], sem.at[slot, 0]),
                pltpu.make_async_copy(wu_hbm.at[e], stg_u.at[slot], sem.at[slot, 1]),
                pltpu.make_async_copy(wd_hbm.at[e], stg_d.at[slot], sem.at[slot, 2]))

    @pl.when(i == 0)
    def _():
        for d in range(MOE_SLOTS):
            @pl.when(tt_ref[0, ahead[d]] < N_EXPERTS)
            def _():
                for cp in fetch(tt_ref[0, ahead[d]], d):
                    cp.start(priority=WEIGHT_DMA_PRIORITY)

    @pl.when(tt_ref[i, T_FIRST] > 0)
    def _():
        slot = tt_ref[i, T_ORD] % MOE_SLOTS
        for cp in fetch(tt_ref[i, T_EXP], slot):
            cp.wait()
        wg_s[...] = stg_g[slot].astype(BF16)
        wu_s[...] = stg_u[slot].astype(BF16)
        wd_s[...] = stg_d[slot].astype(BF16)

        @pl.when(tt_ref[i, ahead[MOE_SLOTS]] < N_EXPERTS)
        def _():
            for cp in fetch(tt_ref[i, ahead[MOE_SLOTS]], slot):
                cp.start(priority=WEIGHT_DMA_PRIORITY)

    @pl.when(tt_ref[i, T_VALID] > 0)
    def _():
        lo, hi = _unpack_halves(_load_token_tiles(x_ref, 0, MOE_TM))
        xa = lo.astype(BF16)
        xb = hi.astype(BF16)
        g = (jnp.dot(xa, wg_s[:HALF_D, :], preferred_element_type=F32)
             + jnp.dot(xb, wg_s[HALF_D:, :], preferred_element_type=F32))
        u = (jnp.dot(xa, wu_s[:HALF_D, :], preferred_element_type=F32)
             + jnp.dot(xb, wu_s[HALF_D:, :], preferred_element_type=F32))
        hid = (g * jax.nn.sigmoid(g) * u).astype(BF16)
        _store_token_tiles(y_ref, _pack_halves(jnp.dot(hid, wd_s[...], preferred_element_type=F32)))

    @pl.when(tt_ref[i, T_VALID] == 0)
    def _():
        y_ref[...] = _pack_halves(jnp.zeros((MOE_TM * SUBLANES, 2 * LANES), F32))


def _moe(tile_tab, xs, wg, wu, wd):
    tm = MOE_TM
    grid_spec = pltpu.PrefetchScalarGridSpec(
        num_scalar_prefetch=1,
        grid=(MOE_TILES,),
        in_specs=[pl.BlockSpec((tm * SUBLANES, LANES), lambda i, tt: (jnp.minimum(i, tt[0, T_LAST]), 0)),
                  pl.BlockSpec(memory_space=pl.ANY), pl.BlockSpec(memory_space=pl.ANY),
                  pl.BlockSpec(memory_space=pl.ANY)],
        out_specs=pl.BlockSpec((tm * SUBLANES, LANES), lambda i, *_: (i, 0)),
        scratch_shapes=[pltpu.VMEM((D_MODEL, EXPERT_FF), BF16), pltpu.VMEM((D_MODEL, EXPERT_FF), BF16),
                        pltpu.VMEM((EXPERT_FF, D_MODEL), BF16),
                        pltpu.VMEM((MOE_SLOTS, D_MODEL, EXPERT_FF), F32),
                        pltpu.VMEM((MOE_SLOTS, D_MODEL, EXPERT_FF), F32),
                        pltpu.VMEM((MOE_SLOTS, EXPERT_FF, D_MODEL), F32),
                        pltpu.SemaphoreType.DMA((MOE_SLOTS, 3))],
    )
    return pl.pallas_call(
        _moe_body,
        grid_spec=grid_spec,
        out_shape=jax.ShapeDtypeStruct((MOE_ROWS * SUBLANES, LANES), U32),
        compiler_params=_cparams(("arbitrary",)),
        name="moe",
    )(tile_tab, xs, wg, wu, wd)


def _final_body(pos_ref, x1_ref, mod_ref, rt_ref, g2_ref, b2_ref, ys_hbm, o_ref, buf, sem):
    i = pl.program_id(0)
    n = pl.num_programs(0)
    tm = FINAL_TM
    slot = i % FINAL_SLOTS
    nxt_slot = (i + 2) % FINAL_SLOTS
    nxt_tile = jnp.minimum(i + 2, n - 1)

    def issue(tile, dst_slot, j):
        pair = TOP_K * (tile * tm + j)
        for k in range(TOP_K):
            src_row = pl.multiple_of(pos_ref[pair + k], SUBLANES)
            dst_row = pl.multiple_of((k * tm + j) * SUBLANES, SUBLANES)
            pltpu.make_async_copy(ys_hbm.at[pl.ds(src_row, SUBLANES)],
                                  buf.at[dst_slot, pl.ds(dst_row, SUBLANES)], sem.at[dst_slot]).start(priority=k)

    def wait(s):
        pltpu.make_async_copy(ys_hbm.at[pl.ds(0, TOP_K * tm * SUBLANES)], buf.at[s], sem.at[s]).wait()

    @pl.when(i == 0)
    def _():
        def tok(j, carry):
            issue(0, 0, j)
            issue(jnp.minimum(1, n - 1), 1, j)
            return carry
        lax.fori_loop(0, tm, tok, 0, unroll=8)

    wait(slot)
    gate2 = mod_ref[0, 5:6, :]
    cur = buf.at[slot]

    def chunk(c, carry):
        r0 = pl.multiple_of(c * FINAL_CHUNK, FINAL_CHUNK)
        rows = pl.ds(r0, FINAL_CHUNK)
        a_lo, a_hi = _unpack_halves(_load_token_tiles(cur, r0, FINAL_CHUNK))
        b_lo, b_hi = _unpack_halves(_load_token_tiles(cur, tm + r0, FINAL_CHUNK))
        x1 = x1_ref[rows, :]
        rt = rt_ref[rows, :]
        for r in range(FINAL_CHUNK):
            issue(nxt_tile, nxt_slot, r0 + r)
        w0 = rt[:, 2:3]
        w1 = rt[:, 3:4]
        y = jnp.concatenate([w0 * a_lo + w1 * b_lo, w0 * a_hi + w1 * b_hi], axis=1)
        o_ref[rows, :] = _ln_rows(DEEPNORM_ALPHA * x1 + gate2 * y) * g2_ref[...] + b2_ref[...]
        return carry

    lax.fori_loop(0, tm // FINAL_CHUNK, chunk, 0)

    @pl.when(i == n - 1)
    def _():
        wait((i + 1) % FINAL_SLOTS)
        wait(nxt_slot)


def _final(pos_rows, x1, mod3, route, g2, b2, ys):
    tm = FINAL_TM
    tpb = SEQ // tm
    row = lambda w: pl.BlockSpec((tm, w), lambda i, *_: (i, 0))
    grid_spec = pltpu.PrefetchScalarGridSpec(
        num_scalar_prefetch=1,
        grid=(N_TOK // tm,),
        in_specs=[row(D_MODEL),
                  pl.BlockSpec((1, N_MOD, D_MODEL), lambda i, *_: (i // tpb, 0, 0)),
                  row(LANES),
                  pl.BlockSpec(g2.shape, lambda i, *_: (0, 0)),
                  pl.BlockSpec(b2.shape, lambda i, *_: (0, 0)),
                  pl.BlockSpec(memory_space=pl.ANY)],
        out_specs=row(D_MODEL),
        scratch_shapes=[pltpu.VMEM((FINAL_SLOTS, TOP_K * tm * SUBLANES, LANES), U32),
                        pltpu.SemaphoreType.DMA((FINAL_SLOTS,))],
    )
    return pl.pallas_call(
        _final_body,
        grid_spec=grid_spec,
        out_shape=jax.ShapeDtypeStruct((N_TOK, D_MODEL), F32),
        compiler_params=_cparams(("arbitrary",)),
        name="final",
    )(pos_rows, x1, mod3, route, g2, b2, ys)


def kernel(x, c, positions, w_ada, b_ada, w_in, q_norm_g, w_uq, kv_norm_g, w_ukv, sgu_norm_g, sgu_norm_b,
           w_spatial, b_spatial, w_o, ln1_g, ln1_b, w_router_group, b_router_group, w_router_expert,
           b_router_expert, w_gate, w_up, w_down, ln2_g, ln2_b):
    l = 0
    x2 = x.reshape(N_TOK, D_MODEL)
    mod3 = _ada(c, w_ada[l], b_ada[l][None, :]).reshape(BATCH, N_MOD, D_MODEL)

    wt, wuq, wukv, wo = _prep(w_in[l].T, w_uq[l], w_ukv[l], w_o[l])
    n_r = N_GROUPS + N_EXPERTS
    wr = jnp.pad(jnp.concatenate([w_router_group[l], w_router_expert[l]], axis=1),
                 ((0, 0), (0, LANES - n_r))).astype(BF16)
    br = jnp.pad(jnp.concatenate([b_router_group[l], b_router_expert[l]]), (0, LANES - n_r))[None, :]
    inv_freq = 1.0 / (ROPE_THETA ** (jnp.arange(0, QK_ROPE_DIM, 2, dtype=F32) / QK_ROPE_DIM))
    invf = jnp.tile(inv_freq, 2 * LANES // QK_ROPE_DIM)[None, :]

    cqn, ckvn, kpe, u, vs = _inproj(x2, mod3, wt, q_norm_g[l][None, :], kv_norm_g[l][None, :],
                                    sgu_norm_g[l][None, :], sgu_norm_b[l][None, :])
    q, k, v = _qkv(cqn, ckvn, kpe, positions.reshape(N_TOK, 1), invf, wuq, wukv)
    attn = _attn(q, k, v).reshape(N_TOK, MLA_WIDTH)
    x1, h2, logits = _mixout(x2, mod3, attn, u, vs, w_spatial[l], b_spatial[l].T, wo,
                             ln1_g[l][None, :], ln1_b[l][None, :], wr, br)
    route, counts = _route(logits)
    pos_tab, tile_tab = _plan(route, counts)
    pos_rows = pos_tab[:, 0:TOP_K].reshape(-1)
    xs = _dispatch(pos_rows, tile_tab, h2)
    ys = _moe(tile_tab, xs, w_gate[l], w_up[l], w_down[l])
    out = _final(pos_rows, x1, mod3, route, ln2_g[l][None, :], ln2_b[l][None, :], ys)
    return out.reshape(BATCH, SEQ, D_MODEL)
```

```python
import jax
import jax.numpy as jnp
import numpy as np
from jax import lax
from jax.experimental import pallas as pl
from jax.experimental.pallas import tpu as pltpu

D_MODEL = 2048
BATCH = 4
SEQ = 2048
N_TOK = BATCH * SEQ

MLA_HEADS = 8
QK_NOPE_DIM = 128
QK_ROPE_DIM = 64
QK_DIM = QK_NOPE_DIM + QK_ROPE_DIM
V_HEAD_DIM = 128
Q_LORA_RANK = 768
KV_LORA_RANK = 512
ROPE_THETA = 10000.0
MLA_WIDTH = MLA_HEADS * V_HEAD_DIM

SGU_GROUPS = 8
SGU_GROUP_DIM = 128
SGU_CHUNK = 128
SGU_WIDTH = SGU_GROUPS * SGU_GROUP_DIM

N_GROUPS = 4
EXPERTS_PER_GROUP = 8
N_EXPERTS = N_GROUPS * EXPERTS_PER_GROUP
TOP_K = 2
EXPERT_FF = 512

DEEPNORM_ALPHA = 2.0 ** 0.25
EPS = 1e-6
N_MOD = 6
NEG_BIG = -1e30

LANES = 128
SUBLANES = 8
V7X_VMEM_BYTES = 64 * 1024 * 1024
VMEM_LIMIT = V7X_VMEM_BYTES * 7 // 8

ADA_TN = 1024
TOK_TM = 512
MIX_TM = 256
PREP_STEPS = 4
ATT_TQ = 1024
ATT_TK = 1024
ATT_HEADS = 4
MOE_TM = 256
MOE_TILES = (N_TOK * TOP_K + N_EXPERTS * (MOE_TM - 1)) // MOE_TM + 1
MOE_ROWS = MOE_TILES * MOE_TM
PLAN_TM = 2048
ROUTE_TM = 1024
RT_RANK = 4
DISPATCH_TM = 1024
FINAL_TM = 256
FINAL_CHUNK = 128
FINAL_SLOTS = 3
assert MOE_TILES <= LANES
T_EXP, T_VALID, T_FIRST, T_CLEAR, T_NEXT, T_NEXT2, T_NEXT3, T_ORD, T_LAST = range(9)
MOE_SLOTS = 2
WEIGHT_DMA_PRIORITY = 1

F32 = jnp.float32
BF16 = jnp.bfloat16
U32 = jnp.uint32
HALF_D = D_MODEL // 2


def _cparams(sem):
    return pltpu.CompilerParams(dimension_semantics=sem, vmem_limit_bytes=VMEM_LIMIT)


def _const_spec(shape):
    nd = len(shape)
    return pl.BlockSpec(shape, lambda *_: (0,) * nd, pipeline_mode=pl.Buffered(1))


def _ln_rows(x):
    mu = jnp.mean(x, axis=-1, keepdims=True)
    xc = x - mu
    var = jnp.mean(xc * xc, axis=-1, keepdims=True)
    return xc * lax.rsqrt(var + EPS)


def _rms_rows(x):
    return x * lax.rsqrt(jnp.mean(x * x, axis=-1, keepdims=True) + EPS)


def _pack_halves(x):
    half = x.shape[-1] // 2
    return pltpu.pack_elementwise([x[:, :half], x[:, half:]], packed_dtype=BF16)


def _unpack_halves(w):
    lo = pltpu.unpack_elementwise(w, index=0, packed_dtype=BF16, unpacked_dtype=F32)
    hi = pltpu.unpack_elementwise(w, index=1, packed_dtype=BF16, unpacked_dtype=F32)
    return lo, hi


def _store_token_tiles(ref, w):
    rows = w.shape[0]
    for s in range(SUBLANES):
        ref[pl.ds(s, rows, stride=SUBLANES), :] = w[:, s * LANES:(s + 1) * LANES]


def _load_token_tiles(ref, start_row, rows):
    return jnp.concatenate([ref[pl.ds(start_row * SUBLANES + s, rows, stride=SUBLANES), :]
                            for s in range(SUBLANES)], axis=1)


def _gelu_tanh(x):
    c = np.sqrt(2.0 / np.pi).astype(np.float32)
    return 0.5 * x * (1.0 + jnp.tanh(c * (x + 0.044715 * (x * x * x))))


def _ada_body(c_ref, w_ref, b_ref, o_ref):
    o_ref[...] = jnp.dot(c_ref[...].astype(BF16), w_ref[...].astype(BF16),
                         preferred_element_type=F32) + b_ref[...]


def _ada(c, w, b):
    n = w.shape[1]
    return pl.pallas_call(
        _ada_body,
        grid=(n // ADA_TN,),
        in_specs=[pl.BlockSpec((BATCH, D_MODEL), lambda j: (0, 0)),
                  pl.BlockSpec((D_MODEL, ADA_TN), lambda j: (0, j)),
                  pl.BlockSpec((1, ADA_TN), lambda j: (0, j))],
        out_specs=pl.BlockSpec((BATCH, ADA_TN), lambda j: (0, j)),
        out_shape=jax.ShapeDtypeStruct((BATCH, n), F32),
        compiler_params=_cparams(("parallel",)),
        name="ada",
    )(c, w, b)


def _prep_body(win_ref, wuq_ref, wukv_ref, wo_ref, win_o, wuq_o, wukv_o, wo_o):
    win_o[...] = win_ref[...].astype(BF16)
    u = wuq_ref[...]
    nope = [u[:, h * QK_DIM:h * QK_DIM + QK_NOPE_DIM] for h in range(MLA_HEADS)]
    rope = [u[:, h * QK_DIM + QK_NOPE_DIM:(h + 1) * QK_DIM] for h in range(MLA_HEADS)]
    wuq_o[...] = jnp.concatenate(nope + rope, axis=1).astype(BF16)
    kv = wukv_ref[...]
    hw = QK_NOPE_DIM + V_HEAD_DIM
    kn = [kv[:, h * hw:h * hw + QK_NOPE_DIM] for h in range(MLA_HEADS)]
    vv = [kv[:, h * hw + QK_NOPE_DIM:(h + 1) * hw] for h in range(MLA_HEADS)]
    wukv_o[...] = jnp.concatenate(kn + vv, axis=1).astype(BF16)
    wo_o[...] = wo_ref[...].astype(BF16)


def _prep(w_in_t, w_uq, w_ukv, w_o):
    steps = PREP_STEPS
    blk = lambda a: pl.BlockSpec((a.shape[0] // steps, a.shape[1]), lambda i: (i, 0))
    ins = (w_in_t, w_uq, w_ukv, w_o)
    return pl.pallas_call(
        _prep_body,
        grid=(steps,),
        in_specs=[blk(a) for a in ins],
        out_specs=[blk(a) for a in ins],
        out_shape=[jax.ShapeDtypeStruct(a.shape, BF16) for a in ins],
        compiler_params=_cparams(("parallel",)),
        name="prep",
    )(*ins)


def _inproj_body(x_ref, mod_ref, wt_ref, gq_ref, gkv_ref, sg_ref, sb_ref,
                 cq_ref, ckv_ref, kpe_ref, u_ref, vs_ref):
    o1, o2, o3 = Q_LORA_RANK, Q_LORA_RANK + KV_LORA_RANK, Q_LORA_RANK + KV_LORA_RANK + QK_ROPE_DIM

    def proj(lo, hi):
        return lax.dot_general(h, wt_ref[lo:hi, :], (((1,), (1,)), ((), ())), preferred_element_type=F32)

    sh = mod_ref[0, 0:1, :]
    sc = mod_ref[0, 1:2, :]
    h = (_ln_rows(x_ref[...]) * (1.0 + sc) + sh).astype(BF16)
    cq_ref[...] = (_rms_rows(proj(0, o1)) * gq_ref[...]).astype(BF16)
    ckv_ref[...] = (_rms_rows(proj(o1, o2)) * gkv_ref[...]).astype(BF16)
    kpe_ref[...] = proj(o2, o2 + LANES)
    gz = _gelu_tanh(proj(o3, o3 + 2 * SGU_WIDTH))
    u_ref[...] = gz[:, :SGU_WIDTH]
    vs_ref[...] = (_ln_rows(gz[:, SGU_WIDTH:]) * sg_ref[...] + sb_ref[...]).astype(BF16)


def _inproj(x2, mod3, wt, gq, gkv, sg, sb):
    tm = TOK_TM
    tiles_per_batch = SEQ // tm
    row = lambda w: pl.BlockSpec((tm, w), lambda i: (i, 0))
    return pl.pallas_call(
        _inproj_body,
        grid=(N_TOK // tm,),
        in_specs=[row(D_MODEL),
                  pl.BlockSpec((1, N_MOD, D_MODEL), lambda i: (i // tiles_per_batch, 0, 0)),
                  _const_spec(wt.shape),
                  _const_spec(gq.shape), _const_spec(gkv.shape), _const_spec(sg.shape), _const_spec(sb.shape)],
        out_specs=[row(Q_LORA_RANK), row(KV_LORA_RANK), row(LANES), row(SGU_WIDTH), row(SGU_WIDTH)],
        out_shape=[jax.ShapeDtypeStruct((N_TOK, Q_LORA_RANK), BF16),
                   jax.ShapeDtypeStruct((N_TOK, KV_LORA_RANK), BF16),
                   jax.ShapeDtypeStruct((N_TOK, LANES), F32),
                   jax.ShapeDtypeStruct((N_TOK, SGU_WIDTH), F32),
                   jax.ShapeDtypeStruct((N_TOK, SGU_WIDTH), BF16)],
        compiler_params=_cparams(("parallel",)),
        name="inproj",
    )(x2, mod3, wt, gq, gkv, sg, sb)


def _rope(x, cos, sin):
    w = x.shape[-1]
    lane = lax.broadcasted_iota(jnp.int32, x.shape, 1)
    first_half = (lane % QK_ROPE_DIM) < (QK_ROPE_DIM // 2)
    rot = jnp.where(first_half,
                    -pltpu.roll(x, w - QK_ROPE_DIM // 2, 1),
                    pltpu.roll(x, QK_ROPE_DIM // 2, 1))
    return x * cos + rot * sin


def _qkv_body(cq_ref, ckv_ref, kpe_ref, pos_ref, invf_ref, wuq_ref, wukv_ref, q_ref, k_ref, v_ref):
    ang = pos_ref[...].astype(F32) * invf_ref[...]
    cos1 = jnp.cos(ang)
    sin1 = jnp.sin(ang)
    reps = MLA_HEADS * QK_ROPE_DIM // LANES
    cos = jnp.concatenate([cos1] * reps, axis=1)
    sin = jnp.concatenate([sin1] * reps, axis=1)
    scale = np.float32(QK_DIM ** -0.5)
    q = jnp.dot(cq_ref[...], wuq_ref[...], preferred_element_type=F32) * scale
    q_pe = _rope(q[:, MLA_HEADS * QK_NOPE_DIM:], cos, sin)
    kv = jnp.dot(ckv_ref[...], wukv_ref[...], preferred_element_type=F32)
    k_pe = _rope(kpe_ref[...], cos1, sin1)[:, :QK_ROPE_DIM].astype(BF16)
    for h in range(MLA_HEADS):
        q_ref[0, h, :, 0:QK_NOPE_DIM] = q[:, h * QK_NOPE_DIM:(h + 1) * QK_NOPE_DIM].astype(BF16)
        q_ref[0, h, :, QK_NOPE_DIM:QK_DIM] = q_pe[:, h * QK_ROPE_DIM:(h + 1) * QK_ROPE_DIM].astype(BF16)
        k_ref[0, h, :, 0:QK_NOPE_DIM] = kv[:, h * QK_NOPE_DIM:(h + 1) * QK_NOPE_DIM].astype(BF16)
        k_ref[0, h, :, QK_NOPE_DIM:QK_DIM] = k_pe
        v_ref[0, h, :, :] = kv[:, MLA_WIDTH + h * V_HEAD_DIM:MLA_WIDTH + (h + 1) * V_HEAD_DIM].T.astype(BF16)


def _qkv(cqn, ckvn, kpe, pos2, invf, wuq, wukv):
    tm = TOK_TM
    tpb = SEQ // tm
    row = lambda w: pl.BlockSpec((tm, w), lambda i: (i, 0))
    head_out = lambda w: pl.BlockSpec((1, MLA_HEADS, tm, w), lambda i: (i // tpb, 0, i % tpb, 0))
    return pl.pallas_call(
        _qkv_body,
        grid=(N_TOK // tm,),
        in_specs=[row(Q_LORA_RANK), row(KV_LORA_RANK), row(LANES), row(1),
                  _const_spec(invf.shape), _const_spec(wuq.shape), _const_spec(wukv.shape)],
        out_specs=[head_out(QK_DIM), head_out(QK_DIM),
                   pl.BlockSpec((1, MLA_HEADS, V_HEAD_DIM, tm), lambda i: (i // tpb, 0, 0, i % tpb))],
        out_shape=[jax.ShapeDtypeStruct((BATCH, MLA_HEADS, SEQ, QK_DIM), BF16),
                   jax.ShapeDtypeStruct((BATCH, MLA_HEADS, SEQ, QK_DIM), BF16),
                   jax.ShapeDtypeStruct((BATCH, MLA_HEADS, V_HEAD_DIM, SEQ), BF16)],
        compiler_params=_cparams(("parallel",)),
        name="qkv",
    )(cqn, ckvn, kpe, pos2, invf, wuq, wukv)


def _attn_body(q_ref, k_ref, v_ref, o_ref):
    i = pl.program_id(2)

    def step(h, j, carry, masked):
        m, l, acc = carry
        start = pl.multiple_of(j * ATT_TK, ATT_TK)
        k = k_ref[0, h, pl.ds(start, ATT_TK), :]
        vt = v_ref[0, h, :, pl.ds(start, ATT_TK)]
        st = lax.dot_general(k, q_ref[0, h], (((1,), (1,)), ((), ())), preferred_element_type=F32)
        if masked:
            kpos = lax.broadcasted_iota(jnp.int32, st.shape, 0)
            qpos = lax.broadcasted_iota(jnp.int32, st.shape, 1)
            st = jnp.where(kpos <= qpos, st, NEG_BIG)
        m_new = jnp.maximum(m, jnp.max(st, axis=0, keepdims=True))
        p = jnp.exp(st - m_new)
        a = jnp.exp(m - m_new)
        l = a * l + jnp.sum(p, axis=0, keepdims=True)
        pv = jnp.dot(vt, p.astype(BF16), preferred_element_type=F32)
        return m_new, l, a * acc + pv

    def steps(j, carries, masked):
        return tuple(step(h, j, carries[h], masked) for h in range(ATT_HEADS))

    init = tuple((jnp.full((1, ATT_TQ), NEG_BIG, F32), jnp.zeros((1, ATT_TQ), F32),
                  jnp.zeros((V_HEAD_DIM, ATT_TQ), F32)) for _ in range(ATT_HEADS))
    carries = lax.fori_loop(0, i, lambda j, c: steps(j, c, False), init)
    carries = steps(i, carries, True)
    for h, (m, l, acc) in enumerate(carries):
        o_ref[0, :, h * V_HEAD_DIM:(h + 1) * V_HEAD_DIM] = (acc / l).T.astype(BF16)


def _attn(q, k, v):
    assert ATT_TQ == ATT_TK
    hb = ATT_HEADS
    return pl.pallas_call(
        _attn_body,
        grid=(BATCH, MLA_HEADS // hb, SEQ // ATT_TQ),
        in_specs=[pl.BlockSpec((1, hb, ATT_TQ, QK_DIM), lambda b, h, i: (b, h, i, 0)),
                  pl.BlockSpec((1, hb, SEQ, QK_DIM), lambda b, h, i: (b, h, 0, 0)),
                  pl.BlockSpec((1, hb, V_HEAD_DIM, SEQ), lambda b, h, i: (b, h, 0, 0))],
        out_specs=pl.BlockSpec((1, ATT_TQ, hb * V_HEAD_DIM), lambda b, h, i: (b, i, h)),
        out_shape=jax.ShapeDtypeStruct((BATCH, SEQ, MLA_WIDTH), BF16),
        compiler_params=_cparams(("parallel", "parallel", "arbitrary")),
        name="attn",
    )(q, k, v)


def _mixout_body(x_ref, mod_ref, attn_ref, u_ref, vs_ref, wsp_ref, bsp_ref, woa_ref, wos_ref,
                 g1_ref, b1_ref, wr_ref, br_ref, x1_ref, h2_ref, lg_ref, sgu_scr):
    r = lax.broadcasted_iota(jnp.int32, (SGU_CHUNK, SGU_CHUNK), 0)
    c = lax.broadcasted_iota(jnp.int32, (SGU_CHUNK, SGU_CHUNK), 1)
    causal = c <= r
    for g in range(SGU_GROUPS):
        ws = jnp.where(causal, wsp_ref[g], 0.0).astype(BF16)
        bias = bsp_ref[:, g:g + 1]
        cols = slice(g * SGU_GROUP_DIM, (g + 1) * SGU_GROUP_DIM)
        for ch in range(MIX_TM // SGU_CHUNK):
            rows = slice(ch * SGU_CHUNK, (ch + 1) * SGU_CHUNK)
            mixed = jnp.dot(ws, vs_ref[rows, cols], preferred_element_type=F32) + bias
            sgu_scr[rows, cols] = (u_ref[rows, cols] * mixed).astype(BF16)
    y = (jnp.dot(attn_ref[...], woa_ref[...], preferred_element_type=F32)
         + jnp.dot(sgu_scr[...], wos_ref[...], preferred_element_type=F32))
    gate1 = mod_ref[0, 2:3, :]
    sh2 = mod_ref[0, 3:4, :]
    sc2 = mod_ref[0, 4:5, :]
    x1 = _ln_rows(DEEPNORM_ALPHA * x_ref[...] + gate1 * y) * g1_ref[...] + b1_ref[...]
    x1_ref[...] = x1
    h2 = _ln_rows(x1) * (1.0 + sc2) + sh2
    _store_token_tiles(h2_ref, _pack_halves(h2))
    lg_ref[...] = jnp.dot(h2.astype(BF16), wr_ref[...], preferred_element_type=F32) + br_ref[...]


def _mixout(x2, mod3, attn, u, vs, wsp, bsp_t, wo, g1, b1, wr, br):
    tm = MIX_TM
    tpb = SEQ // tm
    row = lambda w: pl.BlockSpec((tm, w), lambda i: (i, 0))
    wo_half = lambda j: pl.BlockSpec((MLA_WIDTH, D_MODEL), lambda i: (j, 0), pipeline_mode=pl.Buffered(1))
    return pl.pallas_call(
        _mixout_body,
        grid=(N_TOK // tm,),
        in_specs=[row(D_MODEL),
                  pl.BlockSpec((1, N_MOD, D_MODEL), lambda i: (i // tpb, 0, 0)),
                  row(MLA_WIDTH), row(SGU_WIDTH), row(SGU_WIDTH),
                  _const_spec(wsp.shape), _const_spec(bsp_t.shape), wo_half(0), wo_half(1),
                  _const_spec(g1.shape), _const_spec(b1.shape), _const_spec(wr.shape), _const_spec(br.shape)],
        out_specs=[row(D_MODEL), pl.BlockSpec((tm * SUBLANES, LANES), lambda i: (i, 0)), row(LANES)],
        out_shape=[jax.ShapeDtypeStruct((N_TOK, D_MODEL), F32),
                   jax.ShapeDtypeStruct((N_TOK * SUBLANES, LANES), U32),
                   jax.ShapeDtypeStruct((N_TOK, LANES), F32)],
        scratch_shapes=[pltpu.VMEM((tm, SGU_WIDTH), BF16)],
        compiler_params=_cparams(("parallel",)),
        name="mix_out",
    )(x2, mod3, attn, u, vs, wsp, bsp_t, wo, wo, g1, b1, wr, br)


def _route_math(lg):
    lane = lax.broadcasted_iota(jnp.int32, lg.shape, 1)
    big = jnp.int32(LANES)

    def top1(vals):
        m = jnp.max(vals, axis=-1, keepdims=True)
        idx = jnp.min(jnp.where(vals == m, lane, big), axis=-1, keepdims=True)
        return m, idx

    is_group = lane < N_GROUPS
    glog = jnp.where(is_group, lg, -jnp.inf)
    gmax, gidx = top1(glog)
    pg_top = 1.0 / jnp.sum(jnp.exp(glog - gmax), axis=-1, keepdims=True)
    eid = lane - N_GROUPS
    sel = (eid >= gidx * EXPERTS_PER_GROUP) & (eid < (gidx + 1) * EXPERTS_PER_GROUP)
    elog = jnp.where(sel, lg, -jnp.inf)
    m1, i1 = top1(elog)
    m2, i2 = top1(jnp.where(lane == i1, -jnp.inf, elog))
    e2 = jnp.exp(m2 - m1)
    w1 = pg_top / (1.0 + e2)
    w2 = pg_top * e2 / (1.0 + e2)
    return jnp.where(lane == 0, (i1 - N_GROUPS).astype(F32),
                     jnp.where(lane == 1, (i2 - N_GROUPS).astype(F32),
                               jnp.where(lane == 2, w1, jnp.where(lane == 3, w2, 0.0))))


def _rank_math(rt, counts):
    t = rt.shape[0]
    lane = lax.broadcasted_iota(jnp.int32, (t, LANES), 1).astype(F32)
    oh0 = lane == rt[:, 0:1]
    oh1 = lane == rt[:, 1:2]
    s = jnp.where(oh0 | oh1, 1.0, 0.0)
    r = lax.broadcasted_iota(jnp.int32, (t, t), 0)
    c = lax.broadcasted_iota(jnp.int32, (t, t), 1)
    before = jnp.where(c < r, 1.0, 0.0).astype(BF16)
    csum = jnp.dot(before, s.astype(BF16), preferred_element_type=F32) + counts
    rank0 = jnp.sum(jnp.where(oh0, csum, 0.0), axis=-1, keepdims=True)
    rank1 = jnp.sum(jnp.where(oh1, csum, 0.0), axis=-1, keepdims=True)
    return rank0, rank1, counts + jnp.sum(s, axis=0, keepdims=True)


def _route_body(lg_ref, rt_ref, cnt_ref, cnt_scr):
    @pl.when(pl.program_id(0) == 0)
    def _():
        cnt_scr[...] = jnp.zeros_like(cnt_scr)

    rt = _route_math(lg_ref[...])
    rank0, rank1, counts = _rank_math(rt, cnt_scr[...])
    cnt_scr[...] = counts
    lane = lax.broadcasted_iota(jnp.int32, rt.shape, 1)
    rt_ref[...] = jnp.where(lane == RT_RANK, rank0, jnp.where(lane == RT_RANK + 1, rank1, rt))
    cnt_ref[...] = jnp.broadcast_to(counts, cnt_ref.shape)


def _route(logits):
    tm = ROUTE_TM
    return pl.pallas_call(
        _route_body,
        grid=(N_TOK // tm,),
        in_specs=[pl.BlockSpec((tm, LANES), lambda i: (i, 0))],
        out_specs=[pl.BlockSpec((tm, LANES), lambda i: (i, 0)),
                   pl.BlockSpec((SUBLANES, LANES), lambda i: (0, 0))],
        out_shape=[jax.ShapeDtypeStruct((N_TOK, LANES), F32),
                   jax.ShapeDtypeStruct((SUBLANES, LANES), F32)],
        scratch_shapes=[pltpu.VMEM((1, LANES), F32)],
        compiler_params=_cparams(("arbitrary",)),
        name="route",
    )(logits)


def _plan_body(rt_ref, cnt_ref, pos_ref, tt_ref):
    t = PLAN_TM
    lane = lax.broadcasted_iota(jnp.int32, (t, LANES), 1)
    rt = rt_ref[...]
    oh0 = lane.astype(F32) == rt[:, 0:1]
    oh1 = lane.astype(F32) == rt[:, 1:2]
    counts = cnt_ref[0:1, :]
    tiles = jnp.floor((counts + (MOE_TM - 1)) * (1.0 / MOE_TM))
    r = lax.broadcasted_iota(jnp.int32, (LANES, LANES), 0)
    c = lax.broadcasted_iota(jnp.int32, (LANES, LANES), 1)
    upto = jnp.where(r <= c, 1.0, 0.0).astype(BF16)
    tiles8 = jnp.broadcast_to(tiles, (SUBLANES, LANES)).astype(BF16)
    tile_end = jnp.dot(tiles8, upto, preferred_element_type=F32)[0:1]
    offs = (tile_end - tiles) * MOE_TM
    p0 = jnp.sum(jnp.where(oh0, offs, 0.0), axis=-1, keepdims=True) + rt[:, RT_RANK:RT_RANK + 1]
    p1 = jnp.sum(jnp.where(oh1, offs, 0.0), axis=-1, keepdims=True) + rt[:, RT_RANK + 1:RT_RANK + 2]
    pos_ref[...] = (jnp.where(lane == 0, p0, jnp.where(lane == 1, p1, 0.0)) * SUBLANES).astype(jnp.int32)
    tt_ref[...] = _tile_table(tiles, tile_end)


def _tile_table(tiles, tile_end):
    lane_e = lax.broadcasted_iota(jnp.int32, (LANES, LANES), 1)
    lane_f = lane_e.astype(F32)
    tile_id = lax.broadcasted_iota(jnp.int32, (LANES, LANES), 0).astype(F32)
    is_e = lane_e < N_EXPERTS
    owns = is_e & (tiles > 0)

    def count(cond):
        return jnp.sum(jnp.where(cond, 1.0, 0.0), axis=-1, keepdims=True)

    def first_owner_after(e):
        return jnp.min(jnp.where(owns & (lane_f > e), lane_f, jnp.float32(LANES)), axis=-1, keepdims=True)

    total = jnp.max(tile_end, axis=-1, keepdims=True)
    t_valid = jnp.where(tile_id[:, 0:1] < total, 1.0, 0.0)
    t_exp = jnp.where(t_valid > 0, count(is_e & (tile_end <= tile_id)), count(is_e & (tile_end <= total - 1.0)))
    t_first = count(owns & ((tile_end - tiles) == tile_id))
    t_clear = jnp.maximum(count(owns & ((tile_end - 1.0) == tile_id)), 1.0 - t_valid)
    t_next = first_owner_after(t_exp)
    t_next2 = first_owner_after(t_next)
    t_next3 = first_owner_after(t_next2)
    cols = {T_EXP: t_exp, T_VALID: t_valid, T_FIRST: t_first, T_CLEAR: t_clear, T_NEXT: t_next,
            T_NEXT2: t_next2, T_NEXT3: t_next3, T_ORD: count(owns & (lane_f < t_exp)), T_LAST: total - 1.0}
    table = jnp.zeros((LANES, LANES), F32)
    for k, col in cols.items():
        table = jnp.where(lane_e == k, col, table)
    return table.astype(jnp.int32)


def _plan(route, counts):
    t = PLAN_TM
    return pl.pallas_call(
        _plan_body,
        grid=(N_TOK // t,),
        in_specs=[pl.BlockSpec((t, LANES), lambda i: (i, 0)),
                  pl.BlockSpec((SUBLANES, LANES), lambda i: (0, 0))],
        out_specs=[pl.BlockSpec((t, LANES), lambda i: (i, 0)),
                   pl.BlockSpec((LANES, LANES), lambda i: (0, 0))],
        out_shape=[jax.ShapeDtypeStruct((N_TOK, LANES), jnp.int32),
                   jax.ShapeDtypeStruct((LANES, LANES), jnp.int32)],
        compiler_params=_cparams(("arbitrary",)),
        name="plan",
    )(route, counts)


def _rows_wait(ref, n_rows, sem):
    pltpu.make_async_copy(ref.at[pl.ds(0, n_rows)], ref.at[pl.ds(0, n_rows)], sem).wait()


def _dispatch_body(pos_ref, tt_ref, h_ref, xs_hbm, zbuf, sem_z, sem_s):
    i = pl.program_id(0)
    tile_rows = MOE_TM * SUBLANES

    @pl.when(i == 0)
    def _():
        zbuf[...] = _pack_halves(jnp.zeros((tile_rows, 2 * LANES), F32))

        def zero_copy(tile):
            start = pl.multiple_of(tile * tile_rows, tile_rows)
            return pltpu.make_async_copy(zbuf, xs_hbm.at[pl.ds(start, tile_rows)], sem_z)

        def clear_start(tile, carry):
            @pl.when(tt_ref[tile, T_CLEAR] > 0)
            def _():
                zero_copy(tile).start()
            return carry

        def clear_wait(tile, carry):
            @pl.when(tt_ref[tile, T_CLEAR] > 0)
            def _():
                zero_copy(tile).wait()
            return carry

        lax.fori_loop(0, MOE_TILES, clear_start, 0)
        lax.fori_loop(0, MOE_TILES, clear_wait, 0)

    def tok(j, carry):
        src = h_ref.at[pl.ds(pl.multiple_of(j * SUBLANES, SUBLANES), SUBLANES)]
        pair = TOP_K * (i * DISPATCH_TM + j)
        for k in range(TOP_K):
            dst_row = pl.multiple_of(pos_ref[pair + k], SUBLANES)
            pltpu.make_async_copy(src, xs_hbm.at[pl.ds(dst_row, SUBLANES)], sem_s).start(priority=k)
        return carry

    lax.fori_loop(0, DISPATCH_TM, tok, 0, unroll=8)
    _rows_wait(xs_hbm, TOP_K * DISPATCH_TM * SUBLANES, sem_s)


def _dispatch(pos_rows, tile_clear, h2p):
    grid_spec = pltpu.PrefetchScalarGridSpec(
        num_scalar_prefetch=2,
        grid=(N_TOK // DISPATCH_TM,),
        in_specs=[pl.BlockSpec((DISPATCH_TM * SUBLANES, LANES), lambda i, *_: (i, 0))],
        out_specs=pl.BlockSpec(memory_space=pl.ANY),
        scratch_shapes=[pltpu.VMEM((MOE_TM * SUBLANES, LANES), U32),
                        pltpu.SemaphoreType.DMA(()), pltpu.SemaphoreType.DMA(())],
    )
    return pl.pallas_call(
        _dispatch_body,
        grid_spec=grid_spec,
        out_shape=jax.ShapeDtypeStruct((MOE_ROWS * SUBLANES, LANES), U32),
        compiler_params=_cparams(("arbitrary",)),
        name="dispatch",
    )(pos_rows, tile_clear, h2p)


def _moe_body(tt_ref, x_ref, wg_hbm, wu_hbm, wd_hbm, y_ref,
              wg_s, wu_s, wd_s, stg_g, stg_u, stg_d, sem):
    i = pl.program_id(0)
    ahead = (T_EXP, T_NEXT, T_NEXT2, T_NEXT3)

    def fetch(e, slot):
        return (pltpu.make_async_copy(wg_hbm.at[e], stg_g.at[slot], sem.at[slot, 0]),
                pltpu.make_async_copy(wu_hbm.at[e], stg_u.at[slot], sem.at[slot, 1]),
                pltpu.make_async_copy(wd_hbm.at[e], stg_d.at[slot], sem.at[slot, 2]))

    @pl.when(i == 0)
    def _():
        for d in range(MOE_SLOTS):
            @pl.when(tt_ref[0, ahead[d]] < N_EXPERTS)
            def _():
                for cp in fetch(tt_ref[0, ahead[d]], d):
                    cp.start(priority=WEIGHT_DMA_PRIORITY)

    @pl.when(tt_ref[i, T_FIRST] > 0)
    def _():
        slot = tt_ref[i, T_ORD] % MOE_SLOTS
        for cp in fetch(tt_ref[i, T_EXP], slot):
            cp.wait()
        wg_s[...] = stg_g[slot].astype(BF16)
        wu_s[...] = stg_u[slot].astype(BF16)
        wd_s[...] = stg_d[slot].astype(BF16)

        @pl.when(tt_ref[i, ahead[MOE_SLOTS]] < N_EXPERTS)
        def _():
            for cp in fetch(tt_ref[i, ahead[MOE_SLOTS]], slot):
                cp.start(priority=WEIGHT_DMA_PRIORITY)

    @pl.when(tt_ref[i, T_VALID] > 0)
    def _():
        lo, hi = _unpack_halves(_load_token_tiles(x_ref, 0, MOE_TM))
        xa = lo.astype(BF16)
        xb = hi.astype(BF16)
        y = None
        for c in range(2):
            ff = slice(c * (EXPERT_FF // 2), (c + 1) * (EXPERT_FF // 2))
            g = (jnp.dot(xa, wg_s[:HALF_D, ff], preferred_element_type=F32)
                 + jnp.dot(xb, wg_s[HALF_D:, ff], preferred_element_type=F32))
            u = (jnp.dot(xa, wu_s[:HALF_D, ff], preferred_element_type=F32)
                 + jnp.dot(xb, wu_s[HALF_D:, ff], preferred_element_type=F32))
            hid = (g * jax.nn.sigmoid(g) * u).astype(BF16)
            part = jnp.dot(hid, wd_s[ff, :], preferred_element_type=F32)
            y = part if y is None else y + part
        _store_token_tiles(y_ref, _pack_halves(y))

    @pl.when(tt_ref[i, T_VALID] == 0)
    def _():
        y_ref[...] = _pack_halves(jnp.zeros((MOE_TM * SUBLANES, 2 * LANES), F32))


def _moe(tile_tab, xs, wg, wu, wd):
    tm = MOE_TM
    grid_spec = pltpu.PrefetchScalarGridSpec(
        num_scalar_prefetch=1,
        grid=(MOE_TILES,),
        in_specs=[pl.BlockSpec((tm * SUBLANES, LANES), lambda i, tt: (jnp.minimum(i, tt[0, T_LAST]), 0)),
                  pl.BlockSpec(memory_space=pl.ANY), pl.BlockSpec(memory_space=pl.ANY),
                  pl.BlockSpec(memory_space=pl.ANY)],
        out_specs=pl.BlockSpec((tm * SUBLANES, LANES), lambda i, *_: (i, 0)),
        scratch_shapes=[pltpu.VMEM((D_MODEL, EXPERT_FF), BF16), pltpu.VMEM((D_MODEL, EXPERT_FF), BF16),
                        pltpu.VMEM((EXPERT_FF, D_MODEL), BF16),
                        pltpu.VMEM((MOE_SLOTS, D_MODEL, EXPERT_FF), F32),
                        pltpu.VMEM((MOE_SLOTS, D_MODEL, EXPERT_FF), F32),
                        pltpu.VMEM((MOE_SLOTS, EXPERT_FF, D_MODEL), F32),
                        pltpu.SemaphoreType.DMA((MOE_SLOTS, 3))],
    )
    return pl.pallas_call(
        _moe_body,
        grid_spec=grid_spec,
        out_shape=jax.ShapeDtypeStruct((MOE_ROWS * SUBLANES, LANES), U32),
        compiler_params=_cparams(("arbitrary",)),
        name="moe",
    )(tile_tab, xs, wg, wu, wd)


def _final_body(pos_ref, x1_ref, mod_ref, rt_ref, g2_ref, b2_ref, ys_hbm, o_ref, buf, sem):
    i = pl.program_id(0)
    n = pl.num_programs(0)
    tm = FINAL_TM
    slot = i % FINAL_SLOTS
    nxt_slot = (i + 2) % FINAL_SLOTS
    nxt_tile = jnp.minimum(i + 2, n - 1)

    def issue(tile, dst_slot, j):
        pair = TOP_K * (tile * tm + j)
        for k in range(TOP_K):
            src_row = pl.multiple_of(pos_ref[pair + k], SUBLANES)
            dst_row = pl.multiple_of((k * tm + j) * SUBLANES, SUBLANES)
            pltpu.make_async_copy(ys_hbm.at[pl.ds(src_row, SUBLANES)],
                                  buf.at[dst_slot, pl.ds(dst_row, SUBLANES)], sem.at[dst_slot]).start(priority=k)

    def wait(s):
        pltpu.make_async_copy(ys_hbm.at[pl.ds(0, TOP_K * tm * SUBLANES)], buf.at[s], sem.at[s]).wait()

    @pl.when(i == 0)
    def _():
        def tok(j, carry):
            issue(0, 0, j)
            issue(jnp.minimum(1, n - 1), 1, j)
            return carry
        lax.fori_loop(0, tm, tok, 0, unroll=8)

    wait(slot)
    gate2 = mod_ref[0, 5:6, :]
    cur = buf.at[slot]

    def chunk(c, carry):
        r0 = pl.multiple_of(c * FINAL_CHUNK, FINAL_CHUNK)
        rows = pl.ds(r0, FINAL_CHUNK)
        a_lo, a_hi = _unpack_halves(_load_token_tiles(cur, r0, FINAL_CHUNK))
        b_lo, b_hi = _unpack_halves(_load_token_tiles(cur, tm + r0, FINAL_CHUNK))
        x1 = x1_ref[rows, :]
        rt = rt_ref[rows, :]
        for r in range(FINAL_CHUNK):
            issue(nxt_tile, nxt_slot, r0 + r)
        w0 = rt[:, 2:3]
        w1 = rt[:, 3:4]
        y = jnp.concatenate([w0 * a_lo + w1 * b_lo, w0 * a_hi + w1 * b_hi], axis=1)
        o_ref[rows, :] = _ln_rows(DEEPNORM_ALPHA * x1 + gate2 * y) * g2_ref[...] + b2_ref[...]
        return carry

    lax.fori_loop(0, tm // FINAL_CHUNK, chunk, 0)

    @pl.when(i == n - 1)
    def _():
        wait((i + 1) % FINAL_SLOTS)
        wait(nxt_slot)


def _final(pos_rows, x1, mod3, route, g2, b2, ys):
    tm = FINAL_TM
    tpb = SEQ // tm
    row = lambda w: pl.BlockSpec((tm, w), lambda i, *_: (i, 0))
    grid_spec = pltpu.PrefetchScalarGridSpec(
        num_scalar_prefetch=1,
        grid=(N_TOK // tm,),
        in_specs=[row(D_MODEL),
                  pl.BlockSpec((1, N_MOD, D_MODEL), lambda i, *_: (i // tpb, 0, 0)),
                  row(LANES),
                  pl.BlockSpec(g2.shape, lambda i, *_: (0, 0)),
                  pl.BlockSpec(b2.shape, lambda i, *_: (0, 0)),
                  pl.BlockSpec(memory_space=pl.ANY)],
        out_specs=row(D_MODEL),
        scratch_shapes=[pltpu.VMEM((FINAL_SLOTS, TOP_K * tm * SUBLANES, LANES), U32),
                        pltpu.SemaphoreType.DMA((FINAL_SLOTS,))],
    )
    return pl.pallas_call(
        _final_body,
        grid_spec=grid_spec,
        out_shape=jax.ShapeDtypeStruct((N_TOK, D_MODEL), F32),
        compiler_params=_cparams(("arbitrary",)),
        name="final",
    )(pos_rows, x1, mod3, route, g2, b2, ys)


def kernel(x, c, positions, w_ada, b_ada, w_in, q_norm_g, w_uq, kv_norm_g, w_ukv, sgu_norm_g, sgu_norm_b,
           w_spatial, b_spatial, w_o, ln1_g, ln1_b, w_router_group, b_router_group, w_router_expert,
           b_router_expert, w_gate, w_up, w_down, ln2_g, ln2_b):
    l = 0
    x2 = x.reshape(N_TOK, D_MODEL)
    mod3 = _ada(c, w_ada[l], b_ada[l][None, :]).reshape(BATCH, N_MOD, D_MODEL)

    wt, wuq, wukv, wo = _prep(w_in[l].T, w_uq[l], w_ukv[l], w_o[l])
    n_r = N_GROUPS + N_EXPERTS
    wr = jnp.pad(jnp.concatenate([w_router_group[l], w_router_expert[l]], axis=1),
                 ((0, 0), (0, LANES - n_r))).astype(BF16)
    br = jnp.pad(jnp.concatenate([b_router_group[l], b_router_expert[l]]), (0, LANES - n_r))[None, :]
    inv_freq = 1.0 / (ROPE_THETA ** (jnp.arange(0, QK_ROPE_DIM, 2, dtype=F32) / QK_ROPE_DIM))
    invf = jnp.tile(inv_freq, 2 * LANES // QK_ROPE_DIM)[None, :]

    cqn, ckvn, kpe, u, vs = _inproj(x2, mod3, wt, q_norm_g[l][None, :], kv_norm_g[l][None, :],
                                    sgu_norm_g[l][None, :], sgu_norm_b[l][None, :])
    q, k, v = _qkv(cqn, ckvn, kpe, positions.reshape(N_TOK, 1), invf, wuq, wukv)
    attn = _attn(q, k, v).reshape(N_TOK, MLA_WIDTH)
    x1, h2, logits = _mixout(x2, mod3, attn, u, vs, w_spatial[l], b_spatial[l].T, wo,
                             ln1_g[l][None, :], ln1_b[l][None, :], wr, br)
    route, counts = _route(logits)
    pos_tab, tile_tab = _plan(route, counts)
    pos_rows = pos_tab[:, 0:TOP_K].reshape(-1)
    xs = _dispatch(pos_rows, tile_tab, h2)
    ys = _moe(tile_tab, xs, w_gate[l], w_up[l], w_down[l])
    out = _final(pos_rows, x1, mod3, route, ln2_g[l][None, :], ln2_b[l][None, :], ys)
    return out.reshape(BATCH, SEQ, D_MODEL)
```

```python
import jax
import jax.numpy as jnp
import numpy as np
from jax import lax
from jax.experimental import pallas as pl
from jax.experimental.pallas import tpu as pltpu

D_MODEL = 2048
BATCH = 4
SEQ = 2048
N_TOK = BATCH * SEQ

MLA_HEADS = 8
QK_NOPE_DIM = 128
QK_ROPE_DIM = 64
QK_DIM = QK_NOPE_DIM + QK_ROPE_DIM
V_HEAD_DIM = 128
Q_LORA_RANK = 768
KV_LORA_RANK = 512
ROPE_THETA = 10000.0
MLA_WIDTH = MLA_HEADS * V_HEAD_DIM

SGU_GROUPS = 8
SGU_GROUP_DIM = 128
SGU_CHUNK = 128
SGU_WIDTH = SGU_GROUPS * SGU_GROUP_DIM

N_GROUPS = 4
EXPERTS_PER_GROUP = 8
N_EXPERTS = N_GROUPS * EXPERTS_PER_GROUP
TOP_K = 2
EXPERT_FF = 512

DEEPNORM_ALPHA = 2.0 ** 0.25
EPS = 1e-6
N_MOD = 6
NEG_BIG = -1e30

LANES = 128
SUBLANES = 8
V7X_VMEM_BYTES = 64 * 1024 * 1024
VMEM_LIMIT = V7X_VMEM_BYTES * 7 // 8

ADA_TN = 1024
TOK_TM = 512
QKV_TM = 1024
MIX_TM = 256
PREP_STEPS = 4
ATT_TQ = 1024
ATT_TK = 1024
ATT_HEADS = 4
MOE_TM = 256
MOE_TILES = (N_TOK * TOP_K + N_EXPERTS * (MOE_TM - 1)) // MOE_TM + 1
MOE_ROWS = MOE_TILES * MOE_TM
PLAN_TM = 2048
ROUTE_TM = 1024
RT_RANK = 4
DISPATCH_TM = 1024
FINAL_TM = 256
FINAL_CHUNK = 128
FINAL_SLOTS = 3
assert MOE_TILES <= LANES
T_EXP, T_VALID, T_FIRST, T_CLEAR, T_NEXT, T_NEXT2, T_NEXT3, T_ORD, T_LAST = range(9)
MOE_SLOTS = 2
WEIGHT_DMA_PRIORITY = 1

F32 = jnp.float32
BF16 = jnp.bfloat16
U32 = jnp.uint32
HALF_D = D_MODEL // 2


def _cparams(sem):
    return pltpu.CompilerParams(dimension_semantics=sem, vmem_limit_bytes=VMEM_LIMIT)


def _const_spec(shape):
    nd = len(shape)
    return pl.BlockSpec(shape, lambda *_: (0,) * nd, pipeline_mode=pl.Buffered(1))


def _ln_rows(x):
    mu = jnp.mean(x, axis=-1, keepdims=True)
    xc = x - mu
    var = jnp.mean(xc * xc, axis=-1, keepdims=True)
    return xc * lax.rsqrt(var + EPS)


def _rms_rows(x):
    return x * lax.rsqrt(jnp.mean(x * x, axis=-1, keepdims=True) + EPS)


def _pack_halves(x):
    half = x.shape[-1] // 2
    return pltpu.pack_elementwise([x[:, :half], x[:, half:]], packed_dtype=BF16)


def _unpack_halves(w):
    lo = pltpu.unpack_elementwise(w, index=0, packed_dtype=BF16, unpacked_dtype=F32)
    hi = pltpu.unpack_elementwise(w, index=1, packed_dtype=BF16, unpacked_dtype=F32)
    return lo, hi


def _store_token_tiles(ref, w):
    rows = w.shape[0]
    for s in range(SUBLANES):
        ref[pl.ds(s, rows, stride=SUBLANES), :] = w[:, s * LANES:(s + 1) * LANES]


def _load_token_tiles(ref, start_row, rows):
    return jnp.concatenate([ref[pl.ds(start_row * SUBLANES + s, rows, stride=SUBLANES), :]
                            for s in range(SUBLANES)], axis=1)


def _gelu_tanh(x):
    c = np.sqrt(2.0 / np.pi).astype(np.float32)
    return 0.5 * x * (1.0 + jnp.tanh(c * (x + 0.044715 * (x * x * x))))


def _ada_body(c_ref, w_ref, b_ref, o_ref):
    o_ref[...] = jnp.dot(c_ref[...].astype(BF16), w_ref[...].astype(BF16),
                         preferred_element_type=F32) + b_ref[...]


def _ada(c, w, b):
    n = w.shape[1]
    return pl.pallas_call(
        _ada_body,
        grid=(n // ADA_TN,),
        in_specs=[pl.BlockSpec((BATCH, D_MODEL), lambda j: (0, 0)),
                  pl.BlockSpec((D_MODEL, ADA_TN), lambda j: (0, j)),
                  pl.BlockSpec((1, ADA_TN), lambda j: (0, j))],
        out_specs=pl.BlockSpec((BATCH, ADA_TN), lambda j: (0, j)),
        out_shape=jax.ShapeDtypeStruct((BATCH, n), F32),
        compiler_params=_cparams(("parallel",)),
        name="ada",
    )(c, w, b)


def _prep_body(win_ref, wuq_ref, wukv_ref, wo_ref, win_o, wuq_o, wukv_o, wo_o):
    win_o[...] = win_ref[...].astype(BF16)
    u = wuq_ref[...]
    nope = [u[:, h * QK_DIM:h * QK_DIM + QK_NOPE_DIM] for h in range(MLA_HEADS)]
    rope = [u[:, h * QK_DIM + QK_NOPE_DIM:(h + 1) * QK_DIM] for h in range(MLA_HEADS)]
    wuq_o[...] = jnp.concatenate(nope + rope, axis=1).astype(BF16)
    kv = wukv_ref[...]
    hw = QK_NOPE_DIM + V_HEAD_DIM
    kn = [kv[:, h * hw:h * hw + QK_NOPE_DIM] for h in range(MLA_HEADS)]
    vv = [kv[:, h * hw + QK_NOPE_DIM:(h + 1) * hw] for h in range(MLA_HEADS)]
    wukv_o[...] = jnp.concatenate(kn + vv, axis=1).astype(BF16)
    wo_o[...] = wo_ref[...].astype(BF16)


def _prep(w_in_t, w_uq, w_ukv, w_o):
    steps = PREP_STEPS
    blk = lambda a: pl.BlockSpec((a.shape[0] // steps, a.shape[1]), lambda i: (i, 0))
    ins = (w_in_t, w_uq, w_ukv, w_o)
    return pl.pallas_call(
        _prep_body,
        grid=(steps,),
        in_specs=[blk(a) for a in ins],
        out_specs=[blk(a) for a in ins],
        out_shape=[jax.ShapeDtypeStruct(a.shape, BF16) for a in ins],
        compiler_params=_cparams(("parallel",)),
        name="prep",
    )(*ins)


def _inproj_body(x_ref, mod_ref, wt_ref, gq_ref, gkv_ref, sg_ref, sb_ref,
                 cq_ref, ckv_ref, kpe_ref, u_ref, vs_ref):
    o1, o2, o3 = Q_LORA_RANK, Q_LORA_RANK + KV_LORA_RANK, Q_LORA_RANK + KV_LORA_RANK + QK_ROPE_DIM

    def proj(lo, hi):
        return lax.dot_general(h, wt_ref[lo:hi, :], (((1,), (1,)), ((), ())), preferred_element_type=F32)

    sh = mod_ref[0, 0:1, :]
    sc = mod_ref[0, 1:2, :]
    h = (_ln_rows(x_ref[...]) * (1.0 + sc) + sh).astype(BF16)
    cq_ref[...] = (_rms_rows(proj(0, o1)) * gq_ref[...]).astype(BF16)
    ckv_ref[...] = (_rms_rows(proj(o1, o2)) * gkv_ref[...]).astype(BF16)
    kpe_ref[...] = proj(o2, o2 + LANES)
    gz = _gelu_tanh(proj(o3, o3 + 2 * SGU_WIDTH))
    u_ref[...] = gz[:, :SGU_WIDTH]
    vs_ref[...] = (_ln_rows(gz[:, SGU_WIDTH:]) * sg_ref[...] + sb_ref[...]).astype(BF16)


def _inproj(x2, mod3, wt, gq, gkv, sg, sb):
    tm = TOK_TM
    tiles_per_batch = SEQ // tm
    row = lambda w: pl.BlockSpec((tm, w), lambda i: (i, 0))
    return pl.pallas_call(
        _inproj_body,
        grid=(N_TOK // tm,),
        in_specs=[row(D_MODEL),
                  pl.BlockSpec((1, N_MOD, D_MODEL), lambda i: (i // tiles_per_batch, 0, 0)),
                  _const_spec(wt.shape),
                  _const_spec(gq.shape), _const_spec(gkv.shape), _const_spec(sg.shape), _const_spec(sb.shape)],
        out_specs=[row(Q_LORA_RANK), row(KV_LORA_RANK), row(LANES), row(SGU_WIDTH), row(SGU_WIDTH)],
        out_shape=[jax.ShapeDtypeStruct((N_TOK, Q_LORA_RANK), BF16),
                   jax.ShapeDtypeStruct((N_TOK, KV_LORA_RANK), BF16),
                   jax.ShapeDtypeStruct((N_TOK, LANES), F32),
                   jax.ShapeDtypeStruct((N_TOK, SGU_WIDTH), F32),
                   jax.ShapeDtypeStruct((N_TOK, SGU_WIDTH), BF16)],
        compiler_params=_cparams(("parallel",)),
        name="inproj",
    )(x2, mod3, wt, gq, gkv, sg, sb)


def _rope(x, cos, sin):
    w = x.shape[-1]
    lane = lax.broadcasted_iota(jnp.int32, x.shape, 1)
    first_half = (lane % QK_ROPE_DIM) < (QK_ROPE_DIM // 2)
    rot = jnp.where(first_half,
                    -pltpu.roll(x, w - QK_ROPE_DIM // 2, 1),
                    pltpu.roll(x, QK_ROPE_DIM // 2, 1))
    return x * cos + rot * sin


def _qkv_body(cq_ref, ckv_ref, kpe_ref, pos_ref, invf_ref, wuq_ref, wukv_ref, q_ref, k_ref, v_ref):
    ang = pos_ref[...].astype(F32) * invf_ref[...]
    cos1 = jnp.cos(ang)
    sin1 = jnp.sin(ang)
    reps = MLA_HEADS * QK_ROPE_DIM // LANES
    cos = jnp.concatenate([cos1] * reps, axis=1)
    sin = jnp.concatenate([sin1] * reps, axis=1)
    scale = np.float32(QK_DIM ** -0.5)
    q = jnp.dot(cq_ref[...], wuq_ref[...], preferred_element_type=F32) * scale
    q_pe = _rope(q[:, MLA_HEADS * QK_NOPE_DIM:], cos, sin)
    kv = jnp.dot(ckv_ref[...], wukv_ref[...], preferred_element_type=F32)
    k_pe = _rope(kpe_ref[...], cos1, sin1)[:, :QK_ROPE_DIM].astype(BF16)
    for h in range(MLA_HEADS):
        q_ref[0, h, :, 0:QK_NOPE_DIM] = q[:, h * QK_NOPE_DIM:(h + 1) * QK_NOPE_DIM].astype(BF16)
        q_ref[0, h, :, QK_NOPE_DIM:QK_DIM] = q_pe[:, h * QK_ROPE_DIM:(h + 1) * QK_ROPE_DIM].astype(BF16)
        k_ref[0, h, :, 0:QK_NOPE_DIM] = kv[:, h * QK_NOPE_DIM:(h + 1) * QK_NOPE_DIM].astype(BF16)
        k_ref[0, h, :, QK_NOPE_DIM:QK_DIM] = k_pe
        v_ref[0, h, :, :] = kv[:, MLA_WIDTH + h * V_HEAD_DIM:MLA_WIDTH + (h + 1) * V_HEAD_DIM].T.astype(BF16)


def _qkv(cqn, ckvn, kpe, pos2, invf, wuq, wukv):
    tm = QKV_TM
    tpb = SEQ // tm
    row = lambda w: pl.BlockSpec((tm, w), lambda i: (i, 0))
    head_out = lambda w: pl.BlockSpec((1, MLA_HEADS, tm, w), lambda i: (i // tpb, 0, i % tpb, 0))
    return pl.pallas_call(
        _qkv_body,
        grid=(N_TOK // tm,),
        in_specs=[row(Q_LORA_RANK), row(KV_LORA_RANK), row(LANES), row(1),
                  _const_spec(invf.shape), _const_spec(wuq.shape), _const_spec(wukv.shape)],
        out_specs=[head_out(QK_DIM), head_out(QK_DIM),
                   pl.BlockSpec((1, MLA_HEADS, V_HEAD_DIM, tm), lambda i: (i // tpb, 0, 0, i % tpb))],
        out_shape=[jax.ShapeDtypeStruct((BATCH, MLA_HEADS, SEQ, QK_DIM), BF16),
                   jax.ShapeDtypeStruct((BATCH, MLA_HEADS, SEQ, QK_DIM), BF16),
                   jax.ShapeDtypeStruct((BATCH, MLA_HEADS, V_HEAD_DIM, SEQ), BF16)],
        compiler_params=_cparams(("parallel",)),
        name="qkv",
    )(cqn, ckvn, kpe, pos2, invf, wuq, wukv)


def _attn_body(q_ref, k_ref, v_ref, o_ref):
    i = pl.program_id(2)

    def step(h, j, carry, masked):
        m, l, acc = carry
        start = pl.multiple_of(j * ATT_TK, ATT_TK)
        k = k_ref[0, h, pl.ds(start, ATT_TK), :]
        vt = v_ref[0, h, :, pl.ds(start, ATT_TK)]
        st = lax.dot_general(k, q_ref[0, h], (((1,), (1,)), ((), ())), preferred_element_type=F32)
        if masked:
            kpos = lax.broadcasted_iota(jnp.int32, st.shape, 0)
            qpos = lax.broadcasted_iota(jnp.int32, st.shape, 1)
            st = jnp.where(kpos <= qpos, st, NEG_BIG)
        m_new = jnp.maximum(m, jnp.max(st, axis=0, keepdims=True))
        p = jnp.exp(st - m_new)
        a = jnp.exp(m - m_new)
        l = a * l + jnp.sum(p, axis=0, keepdims=True)
        pv = jnp.dot(vt, p.astype(BF16), preferred_element_type=F32)
        return m_new, l, a * acc + pv

    def steps(j, carries, masked):
        return tuple(step(h, j, carries[h], masked) for h in range(ATT_HEADS))

    init = tuple((jnp.full((1, ATT_TQ), NEG_BIG, F32), jnp.zeros((1, ATT_TQ), F32),
                  jnp.zeros((V_HEAD_DIM, ATT_TQ), F32)) for _ in range(ATT_HEADS))
    carries = lax.fori_loop(0, i, lambda j, c: steps(j, c, False), init)
    carries = steps(i, carries, True)
    for h, (m, l, acc) in enumerate(carries):
        o_ref[0, :, h * V_HEAD_DIM:(h + 1) * V_HEAD_DIM] = (acc / l).T.astype(BF16)


def _attn(q, k, v):
    assert ATT_TQ == ATT_TK
    hb = ATT_HEADS
    return pl.pallas_call(
        _attn_body,
        grid=(BATCH, MLA_HEADS // hb, SEQ // ATT_TQ),
        in_specs=[pl.BlockSpec((1, hb, ATT_TQ, QK_DIM), lambda b, h, i: (b, h, i, 0)),
                  pl.BlockSpec((1, hb, SEQ, QK_DIM), lambda b, h, i: (b, h, 0, 0)),
                  pl.BlockSpec((1, hb, V_HEAD_DIM, SEQ), lambda b, h, i: (b, h, 0, 0))],
        out_specs=pl.BlockSpec((1, ATT_TQ, hb * V_HEAD_DIM), lambda b, h, i: (b, i, h)),
        out_shape=jax.ShapeDtypeStruct((BATCH, SEQ, MLA_WIDTH), BF16),
        compiler_params=_cparams(("parallel", "parallel", "arbitrary")),
        name="attn",
    )(q, k, v)


def _mixout_body(x_ref, mod_ref, attn_ref, u_ref, vs_ref, wsp_ref, bsp_ref, woa_ref, wos_ref,
                 g1_ref, b1_ref, wr_ref, br_ref, x1_ref, h2_ref, lg_ref, sgu_scr):
    r = lax.broadcasted_iota(jnp.int32, (SGU_CHUNK, SGU_CHUNK), 0)
    c = lax.broadcasted_iota(jnp.int32, (SGU_CHUNK, SGU_CHUNK), 1)
    causal = c <= r
    for g in range(SGU_GROUPS):
        ws = jnp.where(causal, wsp_ref[g], 0.0).astype(BF16)
        bias = bsp_ref[:, g:g + 1]
        cols = slice(g * SGU_GROUP_DIM, (g + 1) * SGU_GROUP_DIM)
        for ch in range(MIX_TM // SGU_CHUNK):
            rows = slice(ch * SGU_CHUNK, (ch + 1) * SGU_CHUNK)
            mixed = jnp.dot(ws, vs_ref[rows, cols], preferred_element_type=F32) + bias
            sgu_scr[rows, cols] = (u_ref[rows, cols] * mixed).astype(BF16)
    y = (jnp.dot(attn_ref[...], woa_ref[...], preferred_element_type=F32)
         + jnp.dot(sgu_scr[...], wos_ref[...], preferred_element_type=F32))
    gate1 = mod_ref[0, 2:3, :]
    sh2 = mod_ref[0, 3:4, :]
    sc2 = mod_ref[0, 4:5, :]
    x1 = _ln_rows(DEEPNORM_ALPHA * x_ref[...] + gate1 * y) * g1_ref[...] + b1_ref[...]
    x1_ref[...] = x1
    h2 = _ln_rows(x1) * (1.0 + sc2) + sh2
    _store_token_tiles(h2_ref, _pack_halves(h2))
    lg_ref[...] = jnp.dot(h2.astype(BF16), wr_ref[...], preferred_element_type=F32) + br_ref[...]


def _mixout(x2, mod3, attn, u, vs, wsp, bsp_t, wo, g1, b1, wr, br):
    tm = MIX_TM
    tpb = SEQ // tm
    row = lambda w: pl.BlockSpec((tm, w), lambda i: (i, 0))
    wo_half = lambda j: pl.BlockSpec((MLA_WIDTH, D_MODEL), lambda i: (j, 0), pipeline_mode=pl.Buffered(1))
    return pl.pallas_call(
        _mixout_body,
        grid=(N_TOK // tm,),
        in_specs=[row(D_MODEL),
                  pl.BlockSpec((1, N_MOD, D_MODEL), lambda i: (i // tpb, 0, 0)),
                  row(MLA_WIDTH), row(SGU_WIDTH), row(SGU_WIDTH),
                  _const_spec(wsp.shape), _const_spec(bsp_t.shape), wo_half(0), wo_half(1),
                  _const_spec(g1.shape), _const_spec(b1.shape), _const_spec(wr.shape), _const_spec(br.shape)],
        out_specs=[row(D_MODEL), pl.BlockSpec((tm * SUBLANES, LANES), lambda i: (i, 0)), row(LANES)],
        out_shape=[jax.ShapeDtypeStruct((N_TOK, D_MODEL), F32),
                   jax.ShapeDtypeStruct((N_TOK * SUBLANES, LANES), U32),
                   jax.ShapeDtypeStruct((N_TOK, LANES), F32)],
        scratch_shapes=[pltpu.VMEM((tm, SGU_WIDTH), BF16)],
        compiler_params=_cparams(("parallel",)),
        name="mix_out",
    )(x2, mod3, attn, u, vs, wsp, bsp_t, wo, wo, g1, b1, wr, br)


def _route_math(lg):
    lane = lax.broadcasted_iota(jnp.int32, lg.shape, 1)
    big = jnp.int32(LANES)

    def top1(vals):
        m = jnp.max(vals, axis=-1, keepdims=True)
        idx = jnp.min(jnp.where(vals == m, lane, big), axis=-1, keepdims=True)
        return m, idx

    is_group = lane < N_GROUPS
    glog = jnp.where(is_group, lg, -jnp.inf)
    gmax, gidx = top1(glog)
    pg_top = 1.0 / jnp.sum(jnp.exp(glog - gmax), axis=-1, keepdims=True)
    eid = lane - N_GROUPS
    sel = (eid >= gidx * EXPERTS_PER_GROUP) & (eid < (gidx + 1) * EXPERTS_PER_GROUP)
    elog = jnp.where(sel, lg, -jnp.inf)
    m1, i1 = top1(elog)
    m2, i2 = top1(jnp.where(lane == i1, -jnp.inf, elog))
    e2 = jnp.exp(m2 - m1)
    w1 = pg_top / (1.0 + e2)
    w2 = pg_top * e2 / (1.0 + e2)
    return jnp.where(lane == 0, (i1 - N_GROUPS).astype(F32),
                     jnp.where(lane == 1, (i2 - N_GROUPS).astype(F32),
                               jnp.where(lane == 2, w1, jnp.where(lane == 3, w2, 0.0))))


def _rank_math(rt, counts):
    t = rt.shape[0]
    lane = lax.broadcasted_iota(jnp.int32, (t, LANES), 1).astype(F32)
    oh0 = lane == rt[:, 0:1]
    oh1 = lane == rt[:, 1:2]
    s = jnp.where(oh0 | oh1, 1.0, 0.0)
    r = lax.broadcasted_iota(jnp.int32, (t, t), 0)
    c = lax.broadcasted_iota(jnp.int32, (t, t), 1)
    before = jnp.where(c < r, 1.0, 0.0).astype(BF16)
    csum = jnp.dot(before, s.astype(BF16), preferred_element_type=F32) + counts
    rank0 = jnp.sum(jnp.where(oh0, csum, 0.0), axis=-1, keepdims=True)
    rank1 = jnp.sum(jnp.where(oh1, csum, 0.0), axis=-1, keepdims=True)
    return rank0, rank1, counts + jnp.sum(s, axis=0, keepdims=True)


def _route_body(lg_ref, rt_ref, cnt_ref, cnt_scr):
    @pl.when(pl.program_id(0) == 0)
    def _():
        cnt_scr[...] = jnp.zeros_like(cnt_scr)

    rt = _route_math(lg_ref[...])
    rank0, rank1, counts = _rank_math(rt, cnt_scr[...])
    cnt_scr[...] = counts
    lane = lax.broadcasted_iota(jnp.int32, rt.shape, 1)
    rt_ref[...] = jnp.where(lane == RT_RANK, rank0, jnp.where(lane == RT_RANK + 1, rank1, rt))
    cnt_ref[...] = jnp.broadcast_to(counts, cnt_ref.shape)


def _route(logits):
    tm = ROUTE_TM
    return pl.pallas_call(
        _route_body,
        grid=(N_TOK // tm,),
        in_specs=[pl.BlockSpec((tm, LANES), lambda i: (i, 0))],
        out_specs=[pl.BlockSpec((tm, LANES), lambda i: (i, 0)),
                   pl.BlockSpec((SUBLANES, LANES), lambda i: (0, 0))],
        out_shape=[jax.ShapeDtypeStruct((N_TOK, LANES), F32),
                   jax.ShapeDtypeStruct((SUBLANES, LANES), F32)],
        scratch_shapes=[pltpu.VMEM((1, LANES), F32)],
        compiler_params=_cparams(("arbitrary",)),
        name="route",
    )(logits)


def _plan_body(rt_ref, cnt_ref, pos_ref, tt_ref):
    t = PLAN_TM
    lane = lax.broadcasted_iota(jnp.int32, (t, LANES), 1)
    rt = rt_ref[...]
    oh0 = lane.astype(F32) == rt[:, 0:1]
    oh1 = lane.astype(F32) == rt[:, 1:2]
    counts = cnt_ref[0:1, :]
    tiles = jnp.floor((counts + (MOE_TM - 1)) * (1.0 / MOE_TM))
    r = lax.broadcasted_iota(jnp.int32, (LANES, LANES), 0)
    c = lax.broadcasted_iota(jnp.int32, (LANES, LANES), 1)
    upto = jnp.where(r <= c, 1.0, 0.0).astype(BF16)
    tiles8 = jnp.broadcast_to(tiles, (SUBLANES, LANES)).astype(BF16)
    tile_end = jnp.dot(tiles8, upto, preferred_element_type=F32)[0:1]
    offs = (tile_end - tiles) * MOE_TM
    p0 = jnp.sum(jnp.where(oh0, offs, 0.0), axis=-1, keepdims=True) + rt[:, RT_RANK:RT_RANK + 1]
    p1 = jnp.sum(jnp.where(oh1, offs, 0.0), axis=-1, keepdims=True) + rt[:, RT_RANK + 1:RT_RANK + 2]
    pos_ref[...] = (jnp.where(lane == 0, p0, jnp.where(lane == 1, p1, 0.0)) * SUBLANES).astype(jnp.int32)
    tt_ref[...] = _tile_table(tiles, tile_end)


def _tile_table(tiles, tile_end):
    lane_e = lax.broadcasted_iota(jnp.int32, (LANES, LANES), 1)
    lane_f = lane_e.astype(F32)
    tile_id = lax.broadcasted_iota(jnp.int32, (LANES, LANES), 0).astype(F32)
    is_e = lane_e < N_EXPERTS
    owns = is_e & (tiles > 0)

    def count(cond):
        return jnp.sum(jnp.where(cond, 1.0, 0.0), axis=-1, keepdims=True)

    def first_owner_after(e):
        return jnp.min(jnp.where(owns & (lane_f > e), lane_f, jnp.float32(LANES)), axis=-1, keepdims=True)

    total = jnp.max(tile_end, axis=-1, keepdims=True)
    t_valid = jnp.where(tile_id[:, 0:1] < total, 1.0, 0.0)
    t_exp = jnp.where(t_valid > 0, count(is_e & (tile_end <= tile_id)), count(is_e & (tile_end <= total - 1.0)))
    t_first = count(owns & ((tile_end - tiles) == tile_id))
    t_clear = jnp.maximum(count(owns & ((tile_end - 1.0) == tile_id)), 1.0 - t_valid)
    t_next = first_owner_after(t_exp)
    t_next2 = first_owner_after(t_next)
    t_next3 = first_owner_after(t_next2)
    cols = {T_EXP: t_exp, T_VALID: t_valid, T_FIRST: t_first, T_CLEAR: t_clear, T_NEXT: t_next,
            T_NEXT2: t_next2, T_NEXT3: t_next3, T_ORD: count(owns & (lane_f < t_exp)), T_LAST: total - 1.0}
    table = jnp.zeros((LANES, LANES), F32)
    for k, col in cols.items():
        table = jnp.where(lane_e == k, col, table)
    return table.astype(jnp.int32)


def _plan(route, counts):
    t = PLAN_TM
    return pl.pallas_call(
        _plan_body,
        grid=(N_TOK // t,),
        in_specs=[pl.BlockSpec((t, LANES), lambda i: (i, 0)),
                  pl.BlockSpec((SUBLANES, LANES), lambda i: (0, 0))],
        out_specs=[pl.BlockSpec((t, LANES), lambda i: (i, 0)),
                   pl.BlockSpec((LANES, LANES), lambda i: (0, 0))],
        out_shape=[jax.ShapeDtypeStruct((N_TOK, LANES), jnp.int32),
                   jax.ShapeDtypeStruct((LANES, LANES), jnp.int32)],
        compiler_params=_cparams(("arbitrary",)),
        name="plan",
    )(route, counts)


def _rows_wait(ref, n_rows, sem):
    pltpu.make_async_copy(ref.at[pl.ds(0, n_rows)], ref.at[pl.ds(0, n_rows)], sem).wait()


def _dispatch_body(pos_ref, tt_ref, h_ref, xs_hbm, zbuf, sem_z, sem_s):
    i = pl.program_id(0)
    tile_rows = MOE_TM * SUBLANES

    @pl.when(i == 0)
    def _():
        zbuf[...] = _pack_halves(jnp.zeros((tile_rows, 2 * LANES), F32))

        def zero_copy(tile):
            start = pl.multiple_of(tile * tile_rows, tile_rows)
            return pltpu.make_async_copy(zbuf, xs_hbm.at[pl.ds(start, tile_rows)], sem_z)

        def clear_start(tile, carry):
            @pl.when(tt_ref[tile, T_CLEAR] > 0)
            def _():
                zero_copy(tile).start()
            return carry

        def clear_wait(tile, carry):
            @pl.when(tt_ref[tile, T_CLEAR] > 0)
            def _():
                zero_copy(tile).wait()
            return carry

        lax.fori_loop(0, MOE_TILES, clear_start, 0)
        lax.fori_loop(0, MOE_TILES, clear_wait, 0)

    def tok(j, carry):
        src = h_ref.at[pl.ds(pl.multiple_of(j * SUBLANES, SUBLANES), SUBLANES)]
        pair = TOP_K * (i * DISPATCH_TM + j)
        for k in range(TOP_K):
            dst_row = pl.multiple_of(pos_ref[pair + k], SUBLANES)
            pltpu.make_async_copy(src, xs_hbm.at[pl.ds(dst_row, SUBLANES)], sem_s).start(priority=k)
        return carry

    lax.fori_loop(0, DISPATCH_TM, tok, 0, unroll=8)
    _rows_wait(xs_hbm, TOP_K * DISPATCH_TM * SUBLANES, sem_s)


def _dispatch(pos_rows, tile_clear, h2p):
    grid_spec = pltpu.PrefetchScalarGridSpec(
        num_scalar_prefetch=2,
        grid=(N_TOK // DISPATCH_TM,),
        in_specs=[pl.BlockSpec((DISPATCH_TM * SUBLANES, LANES), lambda i, *_: (i, 0))],
        out_specs=pl.BlockSpec(memory_space=pl.ANY),
        scratch_shapes=[pltpu.VMEM((MOE_TM * SUBLANES, LANES), U32),
                        pltpu.SemaphoreType.DMA(()), pltpu.SemaphoreType.DMA(())],
    )
    return pl.pallas_call(
        _dispatch_body,
        grid_spec=grid_spec,
        out_shape=jax.ShapeDtypeStruct((MOE_ROWS * SUBLANES, LANES), U32),
        compiler_params=_cparams(("arbitrary",)),
        name="dispatch",
    )(pos_rows, tile_clear, h2p)


def _moe_body(tt_ref, x_ref, wg_hbm, wu_hbm, wd_hbm, y_ref,
              wg_s, wu_s, wd_s, stg_g, stg_u, stg_d, sem):
    i = pl.program_id(0)
    ahead = (T_EXP, T_NEXT, T_NEXT2, T_NEXT3)

    def fetch(e, slot):
        return (pltpu.make_async_copy(wg_hbm.at[e], stg_g.at[slot], sem.at[slot, 0]),
                pltpu.make_async_copy(wu_hbm.at[e], stg_u.at[slot], sem.at[slot, 1]),
                pltpu.make_async_copy(wd_hbm.at[e], stg_d.at[slot], sem.at[slot, 2]))

    @pl.when(i == 0)
    def _():
        for d in range(MOE_SLOTS):
            @pl.when(tt_ref[0, ahead[d]] < N_EXPERTS)
            def _():
                for cp in fetch(tt_ref[0, ahead[d]], d):
                    cp.start(priority=WEIGHT_DMA_PRIORITY)

    @pl.when(tt_ref[i, T_FIRST] > 0)
    def _():
        slot = tt_ref[i, T_ORD] % MOE_SLOTS
        for cp in fetch(tt_ref[i, T_EXP], slot):
            cp.wait()
        wg_s[...] = stg_g[slot].astype(BF16)
        wu_s[...] = stg_u[slot].astype(BF16)
        wd_s[...] = stg_d[slot].astype(BF16)

        @pl.when(tt_ref[i, ahead[MOE_SLOTS]] < N_EXPERTS)
        def _():
            for cp in fetch(tt_ref[i, ahead[MOE_SLOTS]], slot):
                cp.start(priority=WEIGHT_DMA_PRIORITY)

    @pl.when(tt_ref[i, T_VALID] > 0)
    def _():
        lo, hi = _unpack_halves(_load_token_tiles(x_ref, 0, MOE_TM))
        xa = lo.astype(BF16)
        xb = hi.astype(BF16)
        g = (jnp.dot(xa, wg_s[:HALF_D, :], preferred_element_type=F32)
             + jnp.dot(xb, wg_s[HALF_D:, :], preferred_element_type=F32))
        u = (jnp.dot(xa, wu_s[:HALF_D, :], preferred_element_type=F32)
             + jnp.dot(xb, wu_s[HALF_D:, :], preferred_element_type=F32))
        hid = (g * jax.nn.sigmoid(g) * u).astype(BF16)
        _store_token_tiles(y_ref, _pack_halves(jnp.dot(hid, wd_s[...], preferred_element_type=F32)))

    @pl.when(tt_ref[i, T_VALID] == 0)
    def _():
        y_ref[...] = _pack_halves(jnp.zeros((MOE_TM * SUBLANES, 2 * LANES), F32))


def _moe(tile_tab, xs, wg, wu, wd):
    tm = MOE_TM
    grid_spec = pltpu.PrefetchScalarGridSpec(
        num_scalar_prefetch=1,
        grid=(MOE_TILES,),
        in_specs=[pl.BlockSpec((tm * SUBLANES, LANES), lambda i, tt: (jnp.minimum(i, tt[0, T_LAST]), 0)),
                  pl.BlockSpec(memory_space=pl.ANY), pl.BlockSpec(memory_space=pl.ANY),
                  pl.BlockSpec(memory_space=pl.ANY)],
        out_specs=pl.BlockSpec((tm * SUBLANES, LANES), lambda i, *_: (i, 0)),
        scratch_shapes=[pltpu.VMEM((D_MODEL, EXPERT_FF), BF16), pltpu.VMEM((D_MODEL, EXPERT_FF), BF16),
                        pltpu.VMEM((EXPERT_FF, D_MODEL), BF16),
                        pltpu.VMEM((MOE_SLOTS, D_MODEL, EXPERT_FF), F32),
                        pltpu.VMEM((MOE_SLOTS, D_MODEL, EXPERT_FF), F32),
                        pltpu.VMEM((MOE_SLOTS, EXPERT_FF, D_MODEL), F32),
                        pltpu.SemaphoreType.DMA((MOE_SLOTS, 3))],
    )
    return pl.pallas_call(
        _moe_body,
        grid_spec=grid_spec,
        out_shape=jax.ShapeDtypeStruct((MOE_ROWS * SUBLANES, LANES), U32),
        compiler_params=_cparams(("arbitrary",)),
        name="moe",
    )(tile_tab, xs, wg, wu, wd)


def _final_body(pos_ref, x1_ref, mod_ref, rt_ref, g2_ref, b2_ref, ys_hbm, o_ref, buf, sem):
    i = pl.program_id(0)
    n = pl.num_programs(0)
    tm = FINAL_TM
    slot = i % FINAL_SLOTS
    nxt_slot = (i + 2) % FINAL_SLOTS
    nxt_tile = jnp.minimum(i + 2, n - 1)

    def issue(tile, dst_slot, j):
        pair = TOP_K * (tile * tm + j)
        for k in range(TOP_K):
            src_row = pl.multiple_of(pos_ref[pair + k], SUBLANES)
            dst_row = pl.multiple_of((k * tm + j) * SUBLANES, SUBLANES)
            pltpu.make_async_copy(ys_hbm.at[pl.ds(src_row, SUBLANES)],
                                  buf.at[dst_slot, pl.ds(dst_row, SUBLANES)], sem.at[dst_slot]).start(priority=k)

    def wait(s):
        pltpu.make_async_copy(ys_hbm.at[pl.ds(0, TOP_K * tm * SUBLANES)], buf.at[s], sem.at[s]).wait()

    @pl.when(i == 0)
    def _():
        def tok(j, carry):
            issue(0, 0, j)
            issue(jnp.minimum(1, n - 1), 1, j)
            return carry
        lax.fori_loop(0, tm, tok, 0, unroll=8)

    wait(slot)
    gate2 = mod_ref[0, 5:6, :]
    cur = buf.at[slot]

    def chunk(c, carry):
        r0 = pl.multiple_of(c * FINAL_CHUNK, FINAL_CHUNK)
        rows = pl.ds(r0, FINAL_CHUNK)
        a_lo, a_hi = _unpack_halves(_load_token_tiles(cur, r0, FINAL_CHUNK))
        b_lo, b_hi = _unpack_halves(_load_token_tiles(cur, tm + r0, FINAL_CHUNK))
        x1 = x1_ref[rows, :]
        rt = rt_ref[rows, :]
        for r in range(FINAL_CHUNK):
            issue(nxt_tile, nxt_slot, r0 + r)
        w0 = rt[:, 2:3]
        w1 = rt[:, 3:4]
        y = jnp.concatenate([w0 * a_lo + w1 * b_lo, w0 * a_hi + w1 * b_hi], axis=1)
        o_ref[rows, :] = _ln_rows(DEEPNORM_ALPHA * x1 + gate2 * y) * g2_ref[...] + b2_ref[...]
        return carry

    lax.fori_loop(0, tm // FINAL_CHUNK, chunk, 0)

    @pl.when(i == n - 1)
    def _():
        wait((i + 1) % FINAL_SLOTS)
        wait(nxt_slot)


def _final(pos_rows, x1, mod3, route, g2, b2, ys):
    tm = FINAL_TM
    tpb = SEQ // tm
    row = lambda w: pl.BlockSpec((tm, w), lambda i, *_: (i, 0))
    grid_spec = pltpu.PrefetchScalarGridSpec(
        num_scalar_prefetch=1,
        grid=(N_TOK // tm,),
        in_specs=[row(D_MODEL),
                  pl.BlockSpec((1, N_MOD, D_MODEL), lambda i, *_: (i // tpb, 0, 0)),
                  row(LANES),
                  pl.BlockSpec(g2.shape, lambda i, *_: (0, 0)),
                  pl.BlockSpec(b2.shape, lambda i, *_: (0, 0)),
                  pl.BlockSpec(memory_space=pl.ANY)],
        out_specs=row(D_MODEL),
        scratch_shapes=[pltpu.VMEM((FINAL_SLOTS, TOP_K * tm * SUBLANES, LANES), U32),
                        pltpu.SemaphoreType.DMA((FINAL_SLOTS,))],
    )
    return pl.pallas_call(
        _final_body,
        grid_spec=grid_spec,
        out_shape=jax.ShapeDtypeStruct((N_TOK, D_MODEL), F32),
        compiler_params=_cparams(("arbitrary",)),
        name="final",
    )(pos_rows, x1, mod3, route, g2, b2, ys)


def kernel(x, c, positions, w_ada, b_ada, w_in, q_norm_g, w_uq, kv_norm_g, w_ukv, sgu_norm_g, sgu_norm_b,
           w_spatial, b_spatial, w_o, ln1_g, ln1_b, w_router_group, b_router_group, w_router_expert,
           b_router_expert, w_gate, w_up, w_down, ln2_g, ln2_b):
    l = 0
    x2 = x.reshape(N_TOK, D_MODEL)
    mod3 = _ada(c, w_ada[l], b_ada[l][None, :]).reshape(BATCH, N_MOD, D_MODEL)

    wt, wuq, wukv, wo = _prep(w_in[l].T, w_uq[l], w_ukv[l], w_o[l])
    n_r = N_GROUPS + N_EXPERTS
    wr = jnp.pad(jnp.concatenate([w_router_group[l], w_router_expert[l]], axis=1),
                 ((0, 0), (0, LANES - n_r))).astype(BF16)
    br = jnp.pad(jnp.concatenate([b_router_group[l], b_router_expert[l]]), (0, LANES - n_r))[None, :]
    inv_freq = 1.0 / (ROPE_THETA ** (jnp.arange(0, QK_ROPE_DIM, 2, dtype=F32) / QK_ROPE_DIM))
    invf = jnp.tile(inv_freq, 2 * LANES // QK_ROPE_DIM)[None, :]

    cqn, ckvn, kpe, u, vs = _inproj(x2, mod3, wt, q_norm_g[l][None, :], kv_norm_g[l][None, :],
                                    sgu_norm_g[l][None, :], sgu_norm_b[l][None, :])
    q, k, v = _qkv(cqn, ckvn, kpe, positions.reshape(N_TOK, 1), invf, wuq, wukv)
    attn = _attn(q, k, v).reshape(N_TOK, MLA_WIDTH)
    x1, h2, logits = _mixout(x2, mod3, attn, u, vs, w_spatial[l], b_spatial[l].T, wo,
                             ln1_g[l][None, :], ln1_b[l][None, :], wr, br)
    route, counts = _route(logits)
    pos_tab, tile_tab = _plan(route, counts)
    pos_rows = pos_tab[:, 0:TOP_K].reshape(-1)
    xs = _dispatch(pos_rows, tile_tab, h2)
    ys = _moe(tile_tab, xs, w_gate[l], w_up[l], w_down[l])
    out = _final(pos_rows, x1, mod3, route, ln2_g[l][None, :], ln2_b[l][None, :], ys)
    return out.reshape(BATCH, SEQ, D_MODEL)
```
